```python
import jax
import jax.numpy as jnp
from jax import lax
import numpy as np

D_MODEL = 2048
BATCH = 8
SEQ = 2048
DEPTH = 1

GRID_W = 64
CTX_LEN = 256
HGRN_HEADS = 8
HGRN_HEAD_DIM = 128
HGRN_WIDTH = HGRN_HEADS * HGRN_HEAD_DIM
HGRN_CHUNK = 64
NA_HEADS = 8
NA_HEAD_DIM = 128
NA_WIDTH = NA_HEADS * NA_HEAD_DIM
WIN_R = 8
WIN_C = 16
ROPE_THETA = 10000.0
FFN_HIDDEN = 5632
CONV_W = 3
N_MOD = 6
EPS = 1e-6
IN_COLS = 5 * HGRN_WIDTH + 3 * NA_WIDTH + 2 * D_MODEL

kernel_name = 'hybrid_hgrn2_natten_convffn_dit'


def rmsnorm(x, g):
    xf = x.astype(jnp.float32)
    y = xf * lax.rsqrt(jnp.mean(xf * xf, axis=-1, keepdims=True) + EPS)
    return (y * g.astype(jnp.float32)).astype(x.dtype)


def modulate(h, shift, scale):
    return h * (1.0 + scale) + shift


def split_heads(t, n_heads):
    return t.reshape(t.shape[:-1] + (n_heads, t.shape[-1] // n_heads))


def split_columns(p):
    sizes = (HGRN_WIDTH,) * 5 + (NA_WIDTH,) * 3 + (D_MODEL, D_MODEL)
    outs, off = [], 0
    for s in sizes:
        outs.append(p[..., off:off + s])
        off += s
    return outs


def hgrn2_chunk_scan(q, logf, k, v, s0):
    bsz, L, nh, d = q.shape
    nc = L // HGRN_CHUNK

    def to_chunks(t):
        return t.reshape(bsz, nc, HGRN_CHUNK, nh, d).transpose(1, 0, 3, 2, 4)

    causal = jnp.tril(jnp.ones((HGRN_CHUNK, HGRN_CHUNK), dtype=bool))[:, :, None]

    def step(S, inp):
        qc, gc, kc, vc = inp
        cum = jnp.cumsum(gc, axis=2)
        o_inter = jnp.einsum('bhtk,bhkv->bhtv', qc * jnp.exp(cum), S)
        diff = cum[:, :, :, None, :] - cum[:, :, None, :, :]
        decay = jnp.where(causal, jnp.exp(jnp.where(causal, diff, 0.0)), 0.0)
        scores = jnp.einsum('bhtk,bhsk,bhtsk->bhts', qc, kc, decay)
        o_intra = jnp.einsum('bhts,bhsv->bhtv', scores, vc)
        last = cum[:, :, -1:, :]
        S_new = jnp.exp(last[:, :, 0, :])[..., None] * S + jnp.einsum('bhsk,bhsv->bhkv', kc * jnp.exp(last - cum), vc)
        return S_new, o_inter + o_intra

    s_fin, o = lax.scan(step, s0, (to_chunks(q), to_chunks(logf), to_chunks(k), to_chunks(v)))
    return o.transpose(1, 0, 3, 2, 4).reshape(bsz, L, nh, d), s_fin


def hgrn2_prep(q, f_logit, i_val, lb):
    f = lb + (1.0 - lb) * jax.nn.sigmoid(f_logit.astype(jnp.float32))
    return (split_heads(q.astype(jnp.float32), HGRN_HEADS),
            split_heads(jnp.log(f), HGRN_HEADS),
            split_heads(1.0 - f, HGRN_HEADS),
            split_heads(i_val.astype(jnp.float32), HGRN_HEADS))


def hgrn2_direction(ctx_in, lat_in, lb, reverse):
    ctx_t = hgrn2_prep(ctx_in[0], ctx_in[1], ctx_in[2], lb)
    lat_t = hgrn2_prep(lat_in[0], lat_in[1], lat_in[2], lb)
    if reverse:
        ctx_t = tuple(jnp.flip(t, axis=1) for t in ctx_t)
        lat_t = tuple(jnp.flip(t, axis=1) for t in lat_t)
    bsz = lat_t[0].shape[0]
    s0 = jnp.zeros((bsz, HGRN_HEADS, HGRN_HEAD_DIM, HGRN_HEAD_DIM), jnp.float32)
    o_ctx, s_ctx = hgrn2_chunk_scan(ctx_t[0], ctx_t[1], ctx_t[2], ctx_t[3], s0)
    o_lat, _ = hgrn2_chunk_scan(lat_t[0], lat_t[1], lat_t[2], lat_t[3], s_ctx)
    if reverse:
        o_ctx = jnp.flip(o_ctx, axis=1)
        o_lat = jnp.flip(o_lat, axis=1)
    return o_lat, o_ctx


def hgrn2_readout(o, g, norm_g, dtype):
    on = o * lax.rsqrt(jnp.mean(o * o, axis=-1, keepdims=True) + EPS) * norm_g.astype(jnp.float32)
    y = on.reshape(o.shape[:2] + (HGRN_WIDTH,)) * jax.nn.silu(g.astype(jnp.float32))
    return y.astype(dtype)


def qk_norm(t, g):
    tf = t.astype(jnp.float32)
    return tf * lax.rsqrt(jnp.mean(tf * tf, axis=-1, keepdims=True) + EPS) * g.astype(jnp.float32)


def axial_rope(t):
    L, d = t.shape[1], t.shape[-1]
    pos = jnp.arange(L, dtype=jnp.int32)
    row = (pos // GRID_W).astype(jnp.float32)
    col = (pos % GRID_W).astype(jnp.float32)
    half = d // 2
    nf = half // 2
    inv = ROPE_THETA ** (-jnp.arange(nf, dtype=jnp.float32) / nf)

    def rot(u, p):
        ang = p[:, None] * inv[None, :]
        cos = jnp.cos(ang)[None, :, None, :]
        sin = jnp.sin(ang)[None, :, None, :]
        u1, u2 = u[..., :nf], u[..., nf:]
        return jnp.concatenate([u1 * cos - u2 * sin, u1 * sin + u2 * cos], axis=-1)

    return jnp.concatenate([rot(t[..., :half], row), rot(t[..., half:], col)], axis=-1)


def neighbourhood_attention(q, k, v, k_ctx, v_ctx, rel_bias):
    bsz, L, nh, d = q.shape
    rows = L // GRID_W
    kr = min(WIN_R, rows)
    scale = d ** -0.5
    r = jnp.arange(rows)
    w = jnp.arange(GRID_W)
    row_start = jnp.clip(r - WIN_R // 2, 0, rows - kr)
    row_idx = row_start[:, None] + jnp.arange(kr)[None, :]
    col_start = jnp.clip(w - WIN_C // 2, 0, GRID_W - WIN_C)
    col_in = (w[None, :] >= col_start[:, None]) & (w[None, :] < col_start[:, None] + WIN_C)
    qg = q.reshape(bsz, rows, GRID_W, nh, d)
    kg = k.reshape(bsz, rows, GRID_W, nh, d)[:, row_idx]
    vg = v.reshape(bsz, rows, GRID_W, nh, d)[:, row_idx]
    s_band = jnp.einsum('brchd,brjwhd->bhrcjw', qg, kg).astype(jnp.float32) * scale
    dr = row_idx - r[:, None]
    dc = jnp.clip(w[None, :] - w[:, None], -(WIN_C - 1), WIN_C - 1)
    bias = rel_bias[:, (dr + WIN_R - 1)[:, None, :, None], (dc + WIN_C - 1)[None, :, None, :]]
    s_band = jnp.where(col_in[:, None, :], s_band + bias.astype(jnp.float32)[None], -jnp.inf)
    s_ctx = jnp.einsum('brchd,bnhd->bhrcn', qg, k_ctx).astype(jnp.float32) * scale
    n_band = kr * GRID_W
    s = jnp.concatenate([s_band.reshape(bsz, nh, rows, GRID_W, n_band), s_ctx], axis=-1)
    p = jax.nn.softmax(s, axis=-1)
    p_band = p[..., :n_band].reshape(bsz, nh, rows, GRID_W, kr, GRID_W)
    p_ctx = p[..., n_band:]
    o = jnp.einsum('bhrcjw,brjwhd->brchd', p_band, vg) + jnp.einsum('bhrcn,bnhd->brchd', p_ctx, v_ctx)
    return o.reshape(bsz, L, nh * d)


def context_attention(q, k, v):
    bsz, n, nh, d = q.shape
    s = jnp.einsum('bnhd,bmhd->bhnm', q, k).astype(jnp.float32) * (d ** -0.5)
    p = jax.nn.softmax(s, axis=-1)
    return jnp.einsum('bhnm,bmhd->bnhd', p, v).reshape(bsz, n, nh * d)


def branch_merge(y_a, y_b, gate_a, gate_b, w_a, w_b, w_o):
    z = jax.nn.sigmoid(gate_a) * (y_a @ w_a) + jax.nn.sigmoid(gate_b) * (y_b @ w_b)
    return z @ w_o


def dwconv_centred(u, w, b):
    L = u.shape[1]
    up = jnp.pad(u, ((0, 0), (1, 1), (0, 0)))
    return up[:, 0:L] * w[0] + up[:, 1:L + 1] * w[1] + up[:, 2:L + 2] * w[2] + b


def conv_ffn(h, w1, w3, cw, cb, w2):
    u = dwconv_centred(h @ w1, cw, cb)
    return (jax.nn.silu(u) * (h @ w3)) @ w2


def _fwd_setup_inputs(seed: int = 0) -> dict:
    key = jax.random.key(seed)
    ks = jax.random.split(key, 24)

    def nrm(k, shape, scale):
        return jax.random.normal(k, shape, jnp.float32) * scale

    return {
        'x': nrm(ks[0], (BATCH, SEQ, D_MODEL), 1.0),
        'c': nrm(ks[1], (BATCH, D_MODEL), 1.0),
        'ctx': nrm(ks[2], (BATCH, CTX_LEN, D_MODEL), 1.0),
        'c_ctx': nrm(ks[3], (D_MODEL,), 1.0),
        'ada_w': nrm(ks[4], (DEPTH, D_MODEL, N_MOD * D_MODEL), D_MODEL ** -0.5),
        'ada_b': nrm(ks[5], (DEPTH, N_MOD * D_MODEL), 0.01),
        'norm1_g': 1.0 + nrm(ks[6], (DEPTH, D_MODEL), 0.02),
        'norm2_g': 1.0 + nrm(ks[7], (DEPTH, D_MODEL), 0.02),
        'w_in': nrm(ks[8], (DEPTH, D_MODEL, IN_COLS), D_MODEL ** -0.5),
        'hgrn_lb_logits': nrm(ks[9], (2, DEPTH + 1, HGRN_WIDTH), 0.5),
        'hgrn_norm_g': 1.0 + nrm(ks[10], (DEPTH, HGRN_HEAD_DIM), 0.02),
        'na_q_norm_g': 1.0 + nrm(ks[11], (DEPTH, NA_HEAD_DIM), 0.02),
        'na_k_norm_g': 1.0 + nrm(ks[12], (DEPTH, NA_HEAD_DIM), 0.02),
        'na_rel_bias': nrm(ks[13], (DEPTH, NA_HEADS, 2 * WIN_R - 1, 2 * WIN_C - 1), 0.1),
        'w_branch_a': nrm(ks[14], (DEPTH, HGRN_WIDTH, D_MODEL), HGRN_WIDTH ** -0.5),
        'w_branch_b': nrm(ks[15], (DEPTH, NA_WIDTH, D_MODEL), NA_WIDTH ** -0.5),
        'w_out': nrm(ks[16], (DEPTH, D_MODEL, D_MODEL), D_MODEL ** -0.5),
        'ffn_w1': nrm(ks[17], (DEPTH, D_MODEL, FFN_HIDDEN), D_MODEL ** -0.5),
        'ffn_w3': nrm(ks[18], (DEPTH, D_MODEL, FFN_HIDDEN), D_MODEL ** -0.5),
        'ffn_conv_w': nrm(ks[19], (DEPTH, CONV_W, FFN_HIDDEN), CONV_W ** -0.5),
        'ffn_conv_b': nrm(ks[20], (DEPTH, FFN_HIDDEN), 0.01),
        'ffn_w2': nrm(ks[21], (DEPTH, FFN_HIDDEN, D_MODEL), FFN_HIDDEN ** -0.5),
    }


def _fwd_reference(x, c, ctx, c_ctx, ada_w, ada_b, norm1_g, norm2_g, w_in, hgrn_lb_logits, hgrn_norm_g,
              na_q_norm_g, na_k_norm_g, na_rel_bias, w_branch_a, w_branch_b, w_out,
              ffn_w1, ffn_w3, ffn_conv_w, ffn_conv_b, ffn_w2):
    lower_bounds = jnp.cumsum(jax.nn.softmax(hgrn_lb_logits.astype(jnp.float32), axis=1), axis=1)
    xc = ctx
    for l in range(DEPTH):
        last_layer = l == DEPTH - 1
        mod_l = jax.nn.silu(c) @ ada_w[l] + ada_b[l]
        mod_c = jax.nn.silu(c_ctx) @ ada_w[l] + ada_b[l]
        sh1, sc1, g1, sh2, sc2, g2 = [m[:, None, :] for m in jnp.split(mod_l, N_MOD, axis=-1)]
        sh1c, sc1c, g1c, sh2c, sc2c, g2c = jnp.split(mod_c, N_MOD, axis=-1)

        h = modulate(rmsnorm(x, norm1_g[l]), sh1, sc1)
        hc = modulate(rmsnorm(xc, norm1_g[l]), sh1c, sc1c)
        qa, fwa, fba, ia, ga, qn, kn, vn, gta, gtb = split_columns(h @ w_in[l])
        qa_c, fwa_c, fba_c, ia_c, ga_c, qn_c, kn_c, vn_c, gta_c, gtb_c = split_columns(hc @ w_in[l])

        o_lf, o_cf = hgrn2_direction((qa_c, fwa_c, ia_c), (qa, fwa, ia), lower_bounds[0, l], False)
        o_lb, o_cb = hgrn2_direction((qa_c, fba_c, ia_c), (qa, fba, ia), lower_bounds[1, l], True)
        y_a = hgrn2_readout(o_lf + o_lb, ga, hgrn_norm_g[l], x.dtype)

        q_n = axial_rope(qk_norm(split_heads(qn, NA_HEADS), na_q_norm_g[l]))
        k_n = axial_rope(qk_norm(split_heads(kn, NA_HEADS), na_k_norm_g[l]))
        v_n = split_heads(vn, NA_HEADS)
        k_c = qk_norm(split_heads(kn_c, NA_HEADS), na_k_norm_g[l])
        v_c = split_heads(vn_c, NA_HEADS)
        y_b = neighbourhood_attention(q_n, k_n, v_n, k_c, v_c, na_rel_bias[l]).astype(x.dtype)

        x_mid = x + g1 * branch_merge(y_a, y_b, gta, gtb, w_branch_a[l], w_branch_b[l], w_out[l])

        h2 = modulate(rmsnorm(x_mid, norm2_g[l]), sh2, sc2)
        x_new = x_mid + g2 * conv_ffn(h2, ffn_w1[l], ffn_w3[l], ffn_conv_w[l], ffn_conv_b[l], ffn_w2[l])

        if not last_layer:
            y_a_c = hgrn2_readout(o_cf + o_cb, ga_c, hgrn_norm_g[l], x.dtype)
            q_c = qk_norm(split_heads(qn_c, NA_HEADS), na_q_norm_g[l])
            y_b_c = context_attention(q_c, k_c, v_c).astype(x.dtype)
            xc_mid = xc + g1c * branch_merge(y_a_c, y_b_c, gta_c, gtb_c, w_branch_a[l], w_branch_b[l], w_out[l])
            h2c = modulate(rmsnorm(xc_mid, norm2_g[l]), sh2c, sc2c)
            xc = xc_mid + g2c * conv_ffn(h2c, ffn_w1[l], ffn_w3[l], ffn_conv_w[l], ffn_conv_b[l], ffn_w2[l])
        x = x_new
    return x


import jax as _jax
import jax.numpy as _jnp

TWIN_FORMAT = 'train_step'
FWD_PARAMS = ['x', 'c', 'ctx', 'c_ctx', 'ada_w', 'ada_b', 'norm1_g', 'norm2_g', 'w_in', 'hgrn_lb_logits', 'hgrn_norm_g', 'na_q_norm_g', 'na_k_norm_g', 'na_rel_bias', 'w_branch_a', 'w_branch_b', 'w_out', 'ffn_w1', 'ffn_w3', 'ffn_conv_w', 'ffn_conv_b', 'ffn_w2']
TWIN_WEIGHTS = ['c_ctx', 'ada_w', 'ada_b', 'norm1_g', 'norm2_g', 'w_in', 'hgrn_lb_logits', 'hgrn_norm_g', 'na_q_norm_g', 'na_k_norm_g', 'na_rel_bias', 'w_branch_a', 'w_branch_b', 'w_out', 'ffn_w1', 'ffn_w3', 'ffn_conv_w', 'ffn_conv_b', 'ffn_w2']
TWIN_DIFF_INPUT = 'x'
TWIN_INPUTS = ['x', 'c', 'ctx', 'c_ctx', 'ada_w', 'ada_b', 'norm1_g', 'norm2_g', 'w_in', 'hgrn_lb_logits', 'hgrn_norm_g', 'na_q_norm_g', 'na_k_norm_g', 'na_rel_bias', 'w_branch_a', 'w_branch_b', 'w_out', 'ffn_w1', 'ffn_w3', 'ffn_conv_w', 'ffn_conv_b', 'ffn_w2', 'loss_target', 'm_c_ctx', 'm_ada_w', 'm_ada_b', 'm_norm1_g', 'm_norm2_g', 'm_w_in', 'm_hgrn_lb_logits', 'm_hgrn_norm_g', 'm_na_q_norm_g', 'm_na_k_norm_g', 'm_na_rel_bias', 'm_w_branch_a', 'm_w_branch_b', 'm_w_out', 'm_ffn_w1', 'm_ffn_w3', 'm_ffn_conv_w', 'm_ffn_conv_b', 'm_ffn_w2', 'v_c_ctx', 'v_ada_w', 'v_ada_b', 'v_norm1_g', 'v_norm2_g', 'v_w_in', 'v_hgrn_lb_logits', 'v_hgrn_norm_g', 'v_na_q_norm_g', 'v_na_k_norm_g', 'v_na_rel_bias', 'v_w_branch_a', 'v_w_branch_b', 'v_w_out', 'v_ffn_w1', 'v_ffn_w3', 'v_ffn_conv_w', 'v_ffn_conv_b', 'v_ffn_w2']
TWIN_OUTPUTS = ['loss', 'grad_x', 'grad_c_ctx', 'grad_ada_w', 'grad_ada_b', 'grad_norm1_g', 'grad_norm2_g', 'grad_w_in', 'grad_hgrn_lb_logits', 'grad_hgrn_norm_g', 'grad_na_q_norm_g', 'grad_na_k_norm_g', 'grad_na_rel_bias', 'grad_w_branch_a', 'grad_w_branch_b', 'grad_w_out', 'grad_ffn_w1', 'grad_ffn_w3', 'grad_ffn_conv_w', 'grad_ffn_conv_b', 'grad_ffn_w2', 'delta_c_ctx', 'delta_ada_w', 'delta_ada_b', 'delta_norm1_g', 'delta_norm2_g', 'delta_w_in', 'delta_hgrn_lb_logits', 'delta_hgrn_norm_g', 'delta_na_q_norm_g', 'delta_na_k_norm_g', 'delta_na_rel_bias', 'delta_w_branch_a', 'delta_w_branch_b', 'delta_w_out', 'delta_ffn_w1', 'delta_ffn_w3', 'delta_ffn_conv_w', 'delta_ffn_conv_b', 'delta_ffn_w2', 'new_m_c_ctx', 'new_m_ada_w', 'new_m_ada_b', 'new_m_norm1_g', 'new_m_norm2_g', 'new_m_w_in', 'new_m_hgrn_lb_logits', 'new_m_hgrn_norm_g', 'new_m_na_q_norm_g', 'new_m_na_k_norm_g', 'new_m_na_rel_bias', 'new_m_w_branch_a', 'new_m_w_branch_b', 'new_m_w_out', 'new_m_ffn_w1', 'new_m_ffn_w3', 'new_m_ffn_conv_w', 'new_m_ffn_conv_b', 'new_m_ffn_w2', 'new_v_c_ctx', 'new_v_ada_w', 'new_v_ada_b', 'new_v_norm1_g', 'new_v_norm2_g', 'new_v_w_in', 'new_v_hgrn_lb_logits', 'new_v_hgrn_norm_g', 'new_v_na_q_norm_g', 'new_v_na_k_norm_g', 'new_v_na_rel_bias', 'new_v_w_branch_a', 'new_v_w_branch_b', 'new_v_w_out', 'new_v_ffn_w1', 'new_v_ffn_w3', 'new_v_ffn_conv_w', 'new_v_ffn_conv_b', 'new_v_ffn_w2']
TWIN_LEAF_KINDS = {'loss': 'loss', 'grad_x': 'grad_x', 'grad_c_ctx': 'grad_w', 'grad_ada_w': 'grad_w', 'grad_ada_b': 'grad_w', 'grad_norm1_g': 'grad_w', 'grad_norm2_g': 'grad_w', 'grad_w_in': 'grad_w', 'grad_hgrn_lb_logits': 'grad_w', 'grad_hgrn_norm_g': 'grad_w', 'grad_na_q_norm_g': 'grad_w', 'grad_na_k_norm_g': 'grad_w', 'grad_na_rel_bias': 'grad_w', 'grad_w_branch_a': 'grad_w', 'grad_w_branch_b': 'grad_w', 'grad_w_out': 'grad_w', 'grad_ffn_w1': 'grad_w', 'grad_ffn_w3': 'grad_w', 'grad_ffn_conv_w': 'grad_w', 'grad_ffn_conv_b': 'grad_w', 'grad_ffn_w2': 'grad_w', 'delta_c_ctx': 'delta_w', 'delta_ada_w': 'delta_w', 'delta_ada_b': 'delta_w', 'delta_norm1_g': 'delta_w', 'delta_norm2_g': 'delta_w', 'delta_w_in': 'delta_w', 'delta_hgrn_lb_logits': 'delta_w', 'delta_hgrn_norm_g': 'delta_w', 'delta_na_q_norm_g': 'delta_w', 'delta_na_k_norm_g': 'delta_w', 'delta_na_rel_bias': 'delta_w', 'delta_w_branch_a': 'delta_w', 'delta_w_branch_b': 'delta_w', 'delta_w_out': 'delta_w', 'delta_ffn_w1': 'delta_w', 'delta_ffn_w3': 'delta_w', 'delta_ffn_conv_w': 'delta_w', 'delta_ffn_conv_b': 'delta_w', 'delta_ffn_w2': 'delta_w', 'new_m_c_ctx': 'new_m', 'new_m_ada_w': 'new_m', 'new_m_ada_b': 'new_m', 'new_m_norm1_g': 'new_m', 'new_m_norm2_g': 'new_m', 'new_m_w_in': 'new_m', 'new_m_hgrn_lb_logits': 'new_m', 'new_m_hgrn_norm_g': 'new_m', 'new_m_na_q_norm_g': 'new_m', 'new_m_na_k_norm_g': 'new_m', 'new_m_na_rel_bias': 'new_m', 'new_m_w_branch_a': 'new_m', 'new_m_w_branch_b': 'new_m', 'new_m_w_out': 'new_m', 'new_m_ffn_w1': 'new_m', 'new_m_ffn_w3': 'new_m', 'new_m_ffn_conv_w': 'new_m', 'new_m_ffn_conv_b': 'new_m', 'new_m_ffn_w2': 'new_m', 'new_v_c_ctx': 'new_v', 'new_v_ada_w': 'new_v', 'new_v_ada_b': 'new_v', 'new_v_norm1_g': 'new_v', 'new_v_norm2_g': 'new_v', 'new_v_w_in': 'new_v', 'new_v_hgrn_lb_logits': 'new_v', 'new_v_hgrn_norm_g': 'new_v', 'new_v_na_q_norm_g': 'new_v', 'new_v_na_k_norm_g': 'new_v', 'new_v_na_rel_bias': 'new_v', 'new_v_w_branch_a': 'new_v', 'new_v_w_branch_b': 'new_v', 'new_v_w_out': 'new_v', 'new_v_ffn_w1': 'new_v', 'new_v_ffn_w3': 'new_v', 'new_v_ffn_conv_w': 'new_v', 'new_v_ffn_conv_b': 'new_v', 'new_v_ffn_w2': 'new_v'}


def _forward(args):
    return _fwd_reference(*[args[k] for k in FWD_PARAMS])


def _output_shape():
    out = _jax.eval_shape(lambda: _forward(_fwd_setup_inputs(0)))
    return out.shape, out.dtype

N_MICROBATCH = 1
ADAM_LR = 0.001
ADAM_B1 = 0.9
ADAM_B2 = 0.999
ADAM_EPS = 1e-08
ADAM_WD = 0.01
ADAM_STEP = 10
PER_EXAMPLE_BATCH_AXIS = {'x': 0, 'c': 0, 'ctx': 0, 'loss_target': 0}
SHARED_INPUTS = []
_WEIGHT_DTYPES = {'c_ctx': _jnp.float32, 'ada_w': _jnp.float32, 'ada_b': _jnp.float32, 'norm1_g': _jnp.float32, 'norm2_g': _jnp.float32, 'w_in': _jnp.float32, 'hgrn_lb_logits': _jnp.float32, 'hgrn_norm_g': _jnp.float32, 'na_q_norm_g': _jnp.float32, 'na_k_norm_g': _jnp.float32, 'na_rel_bias': _jnp.float32, 'w_branch_a': _jnp.float32, 'w_branch_b': _jnp.float32, 'w_out': _jnp.float32, 'ffn_w1': _jnp.float32, 'ffn_w3': _jnp.float32, 'ffn_conv_w': _jnp.float32, 'ffn_conv_b': _jnp.float32, 'ffn_w2': _jnp.float32}
MOMENT_SCALE = {'c_ctx': 4.006410e-01, 'ada_w': 1.214054e+00, 'ada_b': 2.709620e+00, 'norm1_g': 6.756956e-01, 'norm2_g': 6.685937e+00, 'w_in': 2.167454e-01, 'hgrn_lb_logits': 2.125671e-02, 'hgrn_norm_g': 9.565575e+00, 'na_q_norm_g': 3.157639e-01, 'na_k_norm_g': 3.122017e-01, 'na_rel_bias': 9.412243e-03, 'w_branch_a': 1.186112e-01, 'w_branch_b': 3.941871e-01, 'w_out': 3.748116e-01, 'ffn_w1': 5.860206e-01, 'ffn_w3': 3.332970e-01, 'ffn_conv_w': 1.102663e+00, 'ffn_conv_b': 9.000466e-01, 'ffn_w2': 2.633744e-01}


def _to_microbatches(a, axis):
    t = _jnp.moveaxis(a, axis, 0)
    t = t.reshape((N_MICROBATCH, t.shape[0] // N_MICROBATCH) + t.shape[1:])
    return _jnp.moveaxis(t, 1, axis + 1)


def setup_inputs(seed: int = 0) -> dict:
    inp = _fwd_setup_inputs(seed)
    key = _jax.random.fold_in(_jax.random.key(seed), 7919)
    shape, _ = _output_shape()
    out = dict(inp)
    out["loss_target"] = _jax.random.normal(_jax.random.fold_in(key, 0), shape, _jnp.float32)
    for i, name in enumerate(TWIN_WEIGHTS):
        w = inp[name].astype(_jnp.float32)
        if MOMENT_SCALE is None:
            s = _jnp.sqrt(_jnp.mean(_jnp.square(w)) + 1e-30)
        else:
            s = MOMENT_SCALE[name]
        km, kv = _jax.random.split(_jax.random.fold_in(key, i + 1))
        out[name] = w
        out["m_" + name] = s * _jax.random.normal(km, w.shape, _jnp.float32)
        out["v_" + name] = (s * s) * _jax.random.uniform(kv, w.shape, _jnp.float32, 0.5, 1.5)
    if N_MICROBATCH > 1:
        for name, axis in PER_EXAMPLE_BATCH_AXIS.items():
            out[name] = _to_microbatches(out[name], axis)
    return {'x': out['x'], 'c': out['c'], 'ctx': out['ctx'], 'c_ctx': out['c_ctx'], 'ada_w': out['ada_w'], 'ada_b': out['ada_b'], 'norm1_g': out['norm1_g'], 'norm2_g': out['norm2_g'], 'w_in': out['w_in'], 'hgrn_lb_logits': out['hgrn_lb_logits'], 'hgrn_norm_g': out['hgrn_norm_g'], 'na_q_norm_g': out['na_q_norm_g'], 'na_k_norm_g': out['na_k_norm_g'], 'na_rel_bias': out['na_rel_bias'], 'w_branch_a': out['w_branch_a'], 'w_branch_b': out['w_branch_b'], 'w_out': out['w_out'], 'ffn_w1': out['ffn_w1'], 'ffn_w3': out['ffn_w3'], 'ffn_conv_w': out['ffn_conv_w'], 'ffn_conv_b': out['ffn_conv_b'], 'ffn_w2': out['ffn_w2'], 'loss_target': out['loss_target'], 'm_c_ctx': out['m_c_ctx'], 'm_ada_w': out['m_ada_w'], 'm_ada_b': out['m_ada_b'], 'm_norm1_g': out['m_norm1_g'], 'm_norm2_g': out['m_norm2_g'], 'm_w_in': out['m_w_in'], 'm_hgrn_lb_logits': out['m_hgrn_lb_logits'], 'm_hgrn_norm_g': out['m_hgrn_norm_g'], 'm_na_q_norm_g': out['m_na_q_norm_g'], 'm_na_k_norm_g': out['m_na_k_norm_g'], 'm_na_rel_bias': out['m_na_rel_bias'], 'm_w_branch_a': out['m_w_branch_a'], 'm_w_branch_b': out['m_w_branch_b'], 'm_w_out': out['m_w_out'], 'm_ffn_w1': out['m_ffn_w1'], 'm_ffn_w3': out['m_ffn_w3'], 'm_ffn_conv_w': out['m_ffn_conv_w'], 'm_ffn_conv_b': out['m_ffn_conv_b'], 'm_ffn_w2': out['m_ffn_w2'], 'v_c_ctx': out['v_c_ctx'], 'v_ada_w': out['v_ada_w'], 'v_ada_b': out['v_ada_b'], 'v_norm1_g': out['v_norm1_g'], 'v_norm2_g': out['v_norm2_g'], 'v_w_in': out['v_w_in'], 'v_hgrn_lb_logits': out['v_hgrn_lb_logits'], 'v_hgrn_norm_g': out['v_hgrn_norm_g'], 'v_na_q_norm_g': out['v_na_q_norm_g'], 'v_na_k_norm_g': out['v_na_k_norm_g'], 'v_na_rel_bias': out['v_na_rel_bias'], 'v_w_branch_a': out['v_w_branch_a'], 'v_w_branch_b': out['v_w_branch_b'], 'v_w_out': out['v_w_out'], 'v_ffn_w1': out['v_ffn_w1'], 'v_ffn_w3': out['v_ffn_w3'], 'v_ffn_conv_w': out['v_ffn_conv_w'], 'v_ffn_conv_b': out['v_ffn_conv_b'], 'v_ffn_w2': out['v_ffn_w2']}


def _loss(weights, diff, rest, loss_target):
    with _jax.named_scope("forward"):
        args = {**rest, TWIN_DIFF_INPUT: diff, **{k: w.astype(_WEIGHT_DTYPES[k]) for k, w in weights.items()}}
        y = _forward(args)
    with _jax.named_scope("loss_head"):
        err = _jnp.square(y.astype(_jnp.float32) - loss_target)
        return 0.5 * _jnp.sum(_jnp.mean(err, axis=-1)) if err.ndim else 0.5 * err


def _adamw(w, g, m, v):
    m = ADAM_B1 * m + (1.0 - ADAM_B1) * g
    v = ADAM_B2 * v + (1.0 - ADAM_B2) * _jnp.square(g)
    m_hat = m / (1.0 - ADAM_B1 ** ADAM_STEP)
    v_hat = v / (1.0 - ADAM_B2 ** ADAM_STEP)
    delta = -ADAM_LR * (m_hat / (_jnp.sqrt(v_hat) + ADAM_EPS) + ADAM_WD * w)
    return delta, m, v


def reference(x, c, ctx, c_ctx, ada_w, ada_b, norm1_g, norm2_g, w_in, hgrn_lb_logits, hgrn_norm_g, na_q_norm_g, na_k_norm_g, na_rel_bias, w_branch_a, w_branch_b, w_out, ffn_w1, ffn_w3, ffn_conv_w, ffn_conv_b, ffn_w2, loss_target, m_c_ctx, m_ada_w, m_ada_b, m_norm1_g, m_norm2_g, m_w_in, m_hgrn_lb_logits, m_hgrn_norm_g, m_na_q_norm_g, m_na_k_norm_g, m_na_rel_bias, m_w_branch_a, m_w_branch_b, m_w_out, m_ffn_w1, m_ffn_w3, m_ffn_conv_w, m_ffn_conv_b, m_ffn_w2, v_c_ctx, v_ada_w, v_ada_b, v_norm1_g, v_norm2_g, v_w_in, v_hgrn_lb_logits, v_hgrn_norm_g, v_na_q_norm_g, v_na_k_norm_g, v_na_rel_bias, v_w_branch_a, v_w_branch_b, v_w_out, v_ffn_w1, v_ffn_w3, v_ffn_conv_w, v_ffn_conv_b, v_ffn_w2):
    given = dict(x=x, c=c, ctx=ctx, c_ctx=c_ctx, ada_w=ada_w, ada_b=ada_b, norm1_g=norm1_g, norm2_g=norm2_g, w_in=w_in, hgrn_lb_logits=hgrn_lb_logits, hgrn_norm_g=hgrn_norm_g, na_q_norm_g=na_q_norm_g, na_k_norm_g=na_k_norm_g, na_rel_bias=na_rel_bias, w_branch_a=w_branch_a, w_branch_b=w_branch_b, w_out=w_out, ffn_w1=ffn_w1, ffn_w3=ffn_w3, ffn_conv_w=ffn_conv_w, ffn_conv_b=ffn_conv_b, ffn_w2=ffn_w2, loss_target=loss_target, m_c_ctx=m_c_ctx, m_ada_w=m_ada_w, m_ada_b=m_ada_b, m_norm1_g=m_norm1_g, m_norm2_g=m_norm2_g, m_w_in=m_w_in, m_hgrn_lb_logits=m_hgrn_lb_logits, m_hgrn_norm_g=m_hgrn_norm_g, m_na_q_norm_g=m_na_q_norm_g, m_na_k_norm_g=m_na_k_norm_g, m_na_rel_bias=m_na_rel_bias, m_w_branch_a=m_w_branch_a, m_w_branch_b=m_w_branch_b, m_w_out=m_w_out, m_ffn_w1=m_ffn_w1, m_ffn_w3=m_ffn_w3, m_ffn_conv_w=m_ffn_conv_w, m_ffn_conv_b=m_ffn_conv_b, m_ffn_w2=m_ffn_w2, v_c_ctx=v_c_ctx, v_ada_w=v_ada_w, v_ada_b=v_ada_b, v_norm1_g=v_norm1_g, v_norm2_g=v_norm2_g, v_w_in=v_w_in, v_hgrn_lb_logits=v_hgrn_lb_logits, v_hgrn_norm_g=v_hgrn_norm_g, v_na_q_norm_g=v_na_q_norm_g, v_na_k_norm_g=v_na_k_norm_g, v_na_rel_bias=v_na_rel_bias, v_w_branch_a=v_w_branch_a, v_w_branch_b=v_w_branch_b, v_w_out=v_w_out, v_ffn_w1=v_ffn_w1, v_ffn_w3=v_ffn_w3, v_ffn_conv_w=v_ffn_conv_w, v_ffn_conv_b=v_ffn_conv_b, v_ffn_w2=v_ffn_w2)
    weights = {n: given[n] for n in TWIN_WEIGHTS}
    shared = {n: given[n] for n in SHARED_INPUTS}
    per_example = {n: given[n] for n in ['x', 'c', 'ctx']}
    grad_fn = _jax.value_and_grad(_loss, argnums=(0, 1))

    def one_microbatch(ex, loss_target):
        ex = dict(ex)
        diff = ex.pop(TWIN_DIFF_INPUT)
        return grad_fn(weights, diff, {**shared, **ex}, loss_target)

    if N_MICROBATCH == 1:
        loss, (grad_w, grad_x) = one_microbatch(per_example, given["loss_target"])
    else:
        def body(carry, xs):
            loss_sum, grad_sum = carry
            l_k, (gw_k, gx_k) = one_microbatch(xs[0], xs[1])
            with _jax.named_scope("update"):
                return (loss_sum + l_k, _jax.tree.map(_jnp.add, grad_sum, gw_k)), gx_k

        init = (_jnp.zeros((), _jnp.float32), _jax.tree.map(_jnp.zeros_like, weights))
        (loss, grad_w), grad_x = _jax.lax.scan(body, init, (per_example, given["loss_target"]))
    with _jax.named_scope("update"):
        delta_w, new_m, new_v = {}, {}, {}
        for n in TWIN_WEIGHTS:
            delta_w[n], new_m[n], new_v[n] = _adamw(weights[n], grad_w[n], given["m_" + n], given["v_" + n])
    return (loss, grad_x, *[grad_w[n] for n in TWIN_WEIGHTS], *[delta_w[n] for n in TWIN_WEIGHTS],
            *[new_m[n] for n in TWIN_WEIGHTS], *[new_v[n] for n in TWIN_WEIGHTS])
```

```python
import functools

import numpy as np
import jax
import jax.numpy as jnp
from jax import lax
from jax.experimental import pallas as pl
from jax.experimental.pallas import tpu as pltpu

F32 = jnp.float32
BF16 = jnp.bfloat16
HIGHEST = lax.Precision.HIGHEST

N_DEV = 8
MESH_ID = pl.DeviceIdType.MESH
LANE = 128
HEAD_DIM = 128
N_HEADS = 8
GRID_W = 64
WIN_R = 8
WIN_C = 16
ROPE_THETA = 10000.0
EPS = 1e-6
N_MOD = 6
HGRN_BLOCK = 16
NEG_BIG = -1e30
VMEM_LIMIT = 56 << 20

ADAM_LR = 0.001
ADAM_B1 = 0.9
ADAM_B2 = 0.999
ADAM_EPS = 1e-08
ADAM_WD = 0.01
ADAM_STEP = 10

HBM_SPEC = pl.BlockSpec(memory_space=pltpu.HBM)


def _pcall(body, *, name, out_shape, grid=None, in_specs=None, out_specs=None, scratch=(), aliases=None):
    kw = {}
    if grid is not None:
        kw["grid"] = grid
    if in_specs is not None:
        kw["in_specs"] = in_specs
    if out_specs is not None:
        kw["out_specs"] = out_specs
    if scratch:
        kw["scratch_shapes"] = list(scratch)
    if aliases:
        kw["input_output_aliases"] = aliases
    return pl.pallas_call(body, name=name, out_shape=out_shape,
                          compiler_params=pltpu.CompilerParams(vmem_limit_bytes=VMEM_LIMIT), **kw)


def _pick(dim, cands):
    for c in cands:
        if c <= dim and dim % c == 0:
            return c
    return dim


def _sds(shape, dtype):
    return jax.ShapeDtypeStruct(tuple(shape), dtype)


def _peers():
    x, y, c = lax.axis_index("x"), lax.axis_index("y"), lax.axis_index("c")
    out = []
    for k in range(1, N_DEV):
        px = 1 - x if (k >> 2) & 1 else x
        py = 1 - y if (k >> 1) & 1 else y
        pc = 1 - c if k & 1 else c
        out.append((k, (px, py, pc), 4 * px + 2 * py + pc))
    return 4 * x + 2 * y + c, out


def _exchange(arrs, name, scatter):
    n = len(arrs)

    def body(*refs):
        ins, outs = refs[:n], refs[n:2 * n]
        send, recv, loc = refs[2 * n:]
        me, peers = _peers()
        started = []
        for i in range(n):
            src = ins[i].at[me] if scatter else ins[i]
            cp = pltpu.make_async_copy(src, outs[i].at[me], loc.at[i])
            cp.start()
            started.append(cp)
        sends = []
        for k, peer, pidx in peers:
            for i in range(n):
                src = ins[i].at[pidx] if scatter else ins[i]
                cp = pltpu.make_async_remote_copy(src_ref=src, dst_ref=outs[i].at[me], send_sem=send.at[i * 7 + k - 1],
                                                  recv_sem=recv.at[i * 7 + k - 1], device_id=peer, device_id_type=MESH_ID)
                cp.start()
                sends.append(cp)
        for k, peer, pidx in peers:
            for i in range(n):
                src = ins[i].at[pidx] if scatter else ins[i]
                pltpu.make_async_remote_copy(src_ref=src, dst_ref=outs[i].at[pidx], send_sem=send.at[i * 7 + k - 1],
                                             recv_sem=recv.at[i * 7 + k - 1], device_id=peer, device_id_type=MESH_ID).wait_recv()
        for cp in sends:
            cp.wait_send()
        for cp in started:
            cp.wait()

    out_shape = [_sds(a.shape if scatter else (N_DEV,) + a.shape, a.dtype) for a in arrs]
    res = _pcall(body, name=name, out_shape=out_shape, in_specs=[HBM_SPEC] * n, out_specs=[HBM_SPEC] * n,
                 scratch=[pltpu.SemaphoreType.DMA((7 * n,)), pltpu.SemaphoreType.DMA((7 * n,)), pltpu.SemaphoreType.DMA((n,))])(*arrs)
    return list(res)


def _mm_xw(x, g, out_dtype, name, tm_c=(768, 512, 384, 256, 128, 64, 16), tn_c=(768, 512, 256, 128), tk_c=(2048, 1024, 768, 512, 256)):
    m, r = x.shape
    nb, r2, cl = g.shape
    assert r == r2
    tm, tn, tk = _pick(m, tm_c), _pick(cl, tn_c), _pick(r, tk_c)
    q, nk = cl // tn, r // tk

    def body(x_ref, g_ref, o_ref, *acc):
        p = lax.dot_general(x_ref[...].astype(BF16), g_ref[...], (((1,), (0,)), ((), ())), preferred_element_type=F32)
        if nk == 1:
            o_ref[...] = p.astype(o_ref.dtype)
        else:
            k = pl.program_id(2)

            @pl.when(k == 0)
            def _():
                acc[0][...] = p

            @pl.when(k > 0)
            def _():
                acc[0][...] += p

            @pl.when(k == nk - 1)
            def _():
                o_ref[...] = acc[0][...].astype(o_ref.dtype)

    return _pcall(
        body, name=name, grid=(m // tm, nb * q, nk),
        in_specs=[pl.BlockSpec((tm, tk), lambda i, j, k: (i, k)), pl.BlockSpec((None, tk, tn), lambda i, j, k: (j // q, k, j % q))],
        out_specs=pl.BlockSpec((tm, tn), lambda i, j, k: (i, j)),
        out_shape=_sds((m, nb * cl), out_dtype),
        scratch=[] if nk == 1 else [pltpu.VMEM((tm, tn), F32)])(x, g)


def _mm_dyw(dy, g, out_dtype, name, init=None, tm_c=(768, 512, 384, 256, 128), tn_c=(1024, 512, 256, 128), tk_c=(768, 512, 256, 128)):
    m, n = dy.shape
    nb, r, cl = g.shape
    assert n == nb * cl
    tm, tn, tk = _pick(m, tm_c), _pick(r, tn_c), _pick(cl, tk_c)
    q = cl // tk
    nk = nb * q
    has_init = init is not None

    def body(*refs):
        if has_init:
            dy_ref, g_ref, i_ref, o_ref, acc = refs
        else:
            dy_ref, g_ref, o_ref, acc = refs
        k = pl.program_id(2)
        p = lax.dot_general(dy_ref[...].astype(BF16), g_ref[...], (((1,), (1,)), ((), ())), preferred_element_type=F32)

        @pl.when(k == 0)
        def _():
            acc[...] = p + i_ref[...].astype(F32) if has_init else p

        @pl.when(k > 0)
        def _():
            acc[...] += p

        @pl.when(k == nk - 1)
        def _():
            o_ref[...] = acc[...].astype(o_ref.dtype)

    in_specs = [pl.BlockSpec((tm, tk), lambda i, j, k: (i, k)), pl.BlockSpec((None, tn, tk), lambda i, j, k: (k // q, j, k % q))]
    args = [dy, g]
    if has_init:
        in_specs.append(pl.BlockSpec((tm, tn), lambda i, j, k: (i, j)))
        args.append(init)
    return _pcall(body, name=name, grid=(m // tm, r // tn, nk), in_specs=in_specs,
                  out_specs=pl.BlockSpec((tm, tn), lambda i, j, k: (i, j)), out_shape=_sds((m, r), out_dtype),
                  scratch=[pltpu.VMEM((tm, tn), F32)])(*args)


def _mm_xtdy(x, dy, nb, out_dtype, name, tm_c=(1024, 512, 256, 128), tn_c=(768, 512, 256, 128), tk_c=(768, 512, 256, 128, 16)):
    t, r = x.shape
    t2, n = dy.shape
    assert t == t2 and n % nb == 0
    cl = n // nb
    tm, tn, tk = _pick(r, tm_c), _pick(cl, tn_c), _pick(t, tk_c)
    q, nk = cl // tn, t // tk

    def body(x_ref, dy_ref, o_ref, *acc):
        p = lax.dot_general(x_ref[...].astype(BF16), dy_ref[...].astype(BF16), (((0,), (0,)), ((), ())), preferred_element_type=F32)
        if nk == 1:
            o_ref[...] = p.astype(o_ref.dtype)
        else:
            k = pl.program_id(2)

            @pl.when(k == 0)
            def _():
                acc[0][...] = p

            @pl.when(k > 0)
            def _():
                acc[0][...] += p

            @pl.when(k == nk - 1)
            def _():
                o_ref[...] = acc[0][...].astype(o_ref.dtype)

    return _pcall(
        body, name=name, grid=(r // tm, nb * q, nk),
        in_specs=[pl.BlockSpec((tk, tm), lambda i, j, k: (k, i)), pl.BlockSpec((tk, tn), lambda i, j, k: (k, j))],
        out_specs=pl.BlockSpec((None, tm, tn), lambda i, j, k: (j // q, i, j % q)),
        out_shape=_sds((nb, r, cl), out_dtype),
        scratch=[] if nk == 1 else [pltpu.VMEM((tm, tn), F32)])(x, dy)


def _mm_f32(a, b, name, trans_b=False):
    dims = (((1,), (1,)), ((), ())) if trans_b else (((1,), (0,)), ((), ()))
    n = b.shape[0] if trans_b else b.shape[1]

    def body(a_ref, b_ref, o_ref):
        o_ref[...] = lax.dot_general(a_ref[...], b_ref[...], dims, precision=HIGHEST, preferred_element_type=F32)

    return _pcall(body, name=name, out_shape=_sds((a.shape[0], n), F32))(a, b)


def _tiled(fn, ins, outs, grid, name, acc=()):
    n_in = len(ins)
    grid = tuple(grid) or (1,)
    nd = len(grid)

    def body(*refs):
        vals = fn(*[r[...] for r in refs[:n_in]])
        if not isinstance(vals, (tuple, list)):
            vals = (vals,)
        first = None
        for o, (ref, v) in enumerate(zip(refs[n_in:], vals)):
            if o in acc:
                if first is None:
                    first = pl.program_id(0) == 0
                    for a in range(1, nd):
                        first = jnp.logical_and(first, pl.program_id(a) == 0)

                @pl.when(first)
                def _(ref=ref, v=v):
                    ref[...] = v.astype(ref.dtype)

                @pl.when(jnp.logical_not(first))
                def _(ref=ref, v=v):
                    ref[...] += v.astype(ref.dtype)
            else:
                ref[...] = v.astype(ref.dtype)

    res = _pcall(body, name=name, grid=grid,
                 in_specs=[pl.BlockSpec(b, im) for _, b, im in ins],
                 out_specs=[pl.BlockSpec(b, im) for _, _, b, im in outs],
                 out_shape=[_sds(s, d) for s, d, _, _ in outs])(*[a for a, _, _ in ins])
    return list(res)


def _rows(a, tr):
    return (a, (tr, a.shape[1]), lambda i, *_: (i, 0))


def _const(a):
    nd = a.ndim
    return (a, a.shape, lambda *_: (0,) * nd)


def _cast_bf16(w, name):
    r, c = w.shape
    tr = _pick(r, (256, 128, 64, 16))
    return _tiled(lambda v: v, [_rows(w, tr)], [((r, c), BF16, (tr, c), lambda i: (i, 0))], (r // tr,), name)[0]


def _f_norm_mod(x, g, sh, sc):
    y = x * lax.rsqrt(jnp.mean(x * x, axis=-1, keepdims=True) + EPS) * g
    return y * (1.0 + sc) + sh


def _f_qk(t, g, cos, sin, perm):
    y = t * lax.rsqrt(jnp.mean(t * t, axis=-1, keepdims=True) + EPS) * g
    return y * cos + jnp.dot(y, perm, precision=HIGHEST, preferred_element_type=F32) * sin


def _f_readout(of, ob, gate, g):
    o = of + ob
    on = o * lax.rsqrt(jnp.mean(o * o, axis=-1, keepdims=True) + EPS) * g
    return on * (gate * jax.nn.sigmoid(gate))


def _f_merge(ga, gb, pa, pb):
    return jax.nn.sigmoid(ga) * pa + jax.nn.sigmoid(gb) * pb


def _f_resid(x, gate, m):
    return x + gate * m


def _norm_mod_fwd(xcat, g, mods, n_ctx_tiles, tr):
    tt, d = xcat.shape
    which = lambda i: (jnp.where(i >= n_ctx_tiles, 1, 0), 0, 0, 0)

    def fn(x, gg, md):
        return _f_norm_mod(x, gg, md[0], md[1])

    return _tiled(fn, [_rows(xcat, tr), _const(g), (mods, (None, 2, 1, d), which)],
                  [((tt, d), BF16, (tr, d), lambda i: (i, 0))], (tt // tr,), "norm_mod_fwd")[0]


def _norm_mod_bwd(xcat, g, mods, dh, extra, n_ctx_tiles, tr, name):
    tt, d = xcat.shape
    nt = tt // tr
    has_extra = extra is not None

    def body(*refs):
        if has_extra:
            x_ref, g_ref, m_ref, dh_ref, e_ref, dx_ref, dg_ref, dm_ref = refs
        else:
            x_ref, g_ref, m_ref, dh_ref, dx_ref, dg_ref, dm_ref = refs
        i = pl.program_id(0)
        md = m_ref[...]
        _, vjp = jax.vjp(_f_norm_mod, x_ref[...], g_ref[...], md[0], md[1])
        dx, dg, dsh, dsc = vjp(dh_ref[...].astype(F32))
        dx_ref[...] = dx + e_ref[...] if has_extra else dx

        @pl.when(i == 0)
        def _():
            dg_ref[...] = dg

        @pl.when(i > 0)
        def _():
            dg_ref[...] += dg

        fresh = jnp.logical_or(i == 0, i == n_ctx_tiles)

        @pl.when(fresh)
        def _():
            dm_ref[0] = dsh
            dm_ref[1] = dsc

        @pl.when(jnp.logical_not(fresh))
        def _():
            dm_ref[0] += dsh
            dm_ref[1] += dsc

    which = lambda i: (jnp.where(i >= n_ctx_tiles, 1, 0), 0, 0, 0)
    row = pl.BlockSpec((tr, d), lambda i: (i, 0))
    in_specs = [row, pl.BlockSpec((1, d), lambda i: (0, 0)), pl.BlockSpec((None, 2, 1, d), which), row]
    args = [xcat, g, mods, dh]
    if has_extra:
        in_specs.append(row)
        args.append(extra)
    return _pcall(body, name=name, grid=(nt,), in_specs=in_specs,
                  out_specs=[row, pl.BlockSpec((1, d), lambda i: (0, 0)), pl.BlockSpec((None, 2, 1, d), which)],
                  out_shape=[_sds((tt, d), F32), _sds((1, d), F32), _sds((2, 2, 1, d), F32)])(*args)


def _hgrn_tri(reverse):
    t = np.arange(HGRN_BLOCK)
    tri = (t[None, :] >= t[:, None]) if reverse else (t[None, :] <= t[:, None])
    tri = tri.astype(np.float32)
    return jnp.asarray(tri), jnp.asarray(tri.T.copy())


def _hgrn_rowblock(n, n_ctx_blocks, n_blocks, reverse):
    if not reverse:
        return n
    return jnp.where(n < n_ctx_blocks, n_ctx_blocks - 1 - n, n_blocks - 1 - n + n_ctx_blocks)


def _hgrn_gates(fl, lb):
    sg = jax.nn.sigmoid(fl)
    f = lb + (1.0 - lb) * sg
    return sg, f, jnp.log(f), 1.0 - f


def _hgrn_intra_mask(reverse):
    tio = lax.broadcasted_iota(jnp.int32, (HGRN_BLOCK, HEAD_DIM), 0)
    return (lambda s: tio <= s) if reverse else (lambda s: tio >= s)


def _hgrn_fwd(p, lb, seg_f, reverse, n_ctx_rows):
    tt = p.shape[0]
    hb = HGRN_BLOCK
    nblk, nctx = tt // hb, n_ctx_rows // hb
    tri, _ = _hgrn_tri(reverse)
    valid = None

    def body(q_ref, f_ref, v_ref, lb_ref, tri_ref, o_ref, st_all, dec_all, qe_buf, cum_blk, k_blk, v_blk):
        mask = _hgrn_intra_mask(reverse)
        lbv = lb_ref[...]

        def phase1(n, c):
            r0 = pl.multiple_of(_hgrn_rowblock(n, nctx, nblk, reverse) * hb, hb)
            q, v = q_ref[pl.ds(r0, hb), :], v_ref[pl.ds(r0, hb), :]
            _, f, g, k = _hgrn_gates(f_ref[pl.ds(r0, hb), :], lbv)
            cum = jnp.dot(tri_ref[...], g, precision=HIGHEST, preferred_element_type=F32)
            tot = jnp.sum(g, axis=0, keepdims=True)
            cum_blk[...] = cum
            k_blk[...] = k
            v_blk[...] = v
            oi = jnp.zeros((hb, HEAD_DIM), F32)
            for s in range(hb):
                e = jnp.where(mask(s), jnp.exp(jnp.minimum(cum - cum_blk[s:s + 1, :], 0.0)), 0.0)
                a_s = jnp.sum(q * e * k_blk[s:s + 1, :], axis=-1, keepdims=True)
                oi = oi + a_s * v_blk[s:s + 1, :]
            o_ref[pl.ds(r0, hb), :] = oi
            qe_buf[pl.ds(r0, hb), :] = q * jnp.exp(cum)
            kl = k * jnp.exp(tot - cum)
            st_all[n] = lax.dot_general(v.astype(BF16), kl.astype(BF16), (((0,), (0,)), ((), ())), preferred_element_type=F32)
            dec_all[pl.ds(n, 1), :] = jnp.exp(tot)
            return c

        lax.fori_loop(0, nblk, phase1, 0)

        def phase2(n, st):
            kv = st_all[n]
            st_all[n] = st
            return st * dec_all[pl.ds(n, 1), :] + kv

        lax.fori_loop(0, nblk, phase2, jnp.zeros((HEAD_DIM, HEAD_DIM), F32))

        def phase3(n, c):
            r0 = pl.multiple_of(_hgrn_rowblock(n, nctx, nblk, reverse) * hb, hb)
            o_ref[pl.ds(r0, hb), :] += lax.dot_general(qe_buf[pl.ds(r0, hb), :].astype(BF16), st_all[n].astype(BF16),
                                                        (((1,), (1,)), ((), ())), preferred_element_type=F32)
            return c

        lax.fori_loop(0, nblk, phase3, 0)

    col = lambda seg: pl.BlockSpec((tt, HEAD_DIM), lambda h, seg=seg: (0, seg * N_HEADS + h))
    return _pcall(
        body, name="hgrn_fwd_rev" if reverse else "hgrn_fwd", grid=(N_HEADS,),
        in_specs=[col(0), col(seg_f), col(3), pl.BlockSpec((1, HEAD_DIM), lambda h: (0, h)), pl.BlockSpec((hb, hb), lambda h: (0, 0))],
        out_specs=pl.BlockSpec((tt, HEAD_DIM), lambda h: (0, h)),
        out_shape=_sds((tt, N_HEADS * HEAD_DIM), F32),
        scratch=[pltpu.VMEM((nblk, HEAD_DIM, HEAD_DIM), F32), pltpu.VMEM((nblk, HEAD_DIM), F32), pltpu.VMEM((tt, HEAD_DIM), F32),
                 pltpu.VMEM((hb, HEAD_DIM), F32), pltpu.VMEM((hb, HEAD_DIM), F32), pltpu.VMEM((hb, HEAD_DIM), F32)])(p, p, p, lb, tri)


def _hgrn_bwd(p, lb, do, seg_f, reverse, n_ctx_rows, prev):
    tt = p.shape[0]
    hb = HGRN_BLOCK
    nblk, nctx = tt // hb, n_ctx_rows // hb
    tri, tri_t = _hgrn_tri(reverse)
    last_row = 0 if reverse else hb - 1
    has_prev = prev is not None

    def body(*refs):
        q_ref, f_ref, v_ref, lb_ref, tri_ref, trit_ref, do_ref = refs[:7]
        refs = refs[7:]
        if has_prev:
            pq_ref, pv_ref = refs[:2]
            refs = refs[2:]
        dq_ref, dfl_ref, dv_ref, dlb_ref, st_all, dd_all, dec_all, cum_buf, cum_blk, k_blk, v_blk, dk_blk, dv_blk = refs
        mask = _hgrn_intra_mask(reverse)
        lbv = lb_ref[...]
        tio = lax.broadcasted_iota(jnp.int32, (hb, HEAD_DIM), 0)

        def rows_of(n):
            rb = _hgrn_rowblock(n, nctx, nblk, reverse)
            return rb, pl.multiple_of(rb * hb, hb)

        def load_do(rb):
            lat0 = pl.multiple_of(jnp.maximum(rb - nctx, 0) * hb, hb)
            return jnp.where(rb >= nctx, do_ref[pl.ds(lat0, hb), :], 0.0)

        def phase1(n, c):
            rb, r0 = rows_of(n)
            q, v = q_ref[pl.ds(r0, hb), :], v_ref[pl.ds(r0, hb), :]
            _, f, g, k = _hgrn_gates(f_ref[pl.ds(r0, hb), :], lbv)
            cum = jnp.dot(tri_ref[...], g, precision=HIGHEST, preferred_element_type=F32)
            tot = jnp.sum(g, axis=0, keepdims=True)
            cum_buf[pl.ds(r0, hb), :] = cum
            kl = k * jnp.exp(tot - cum)
            st_all[n] = lax.dot_general(v.astype(BF16), kl.astype(BF16), (((0,), (0,)), ((), ())), preferred_element_type=F32)
            dec_all[pl.ds(n, 1), :] = jnp.exp(tot)
            qe = q * jnp.exp(cum)
            dd_all[n] = lax.dot_general(load_do(rb).astype(BF16), qe.astype(BF16), (((0,), (0,)), ((), ())), preferred_element_type=F32)
            return c

        lax.fori_loop(0, nblk, phase1, 0)

        def phase2(n, st):
            kv = st_all[n]
            st_all[n] = st
            return st * dec_all[pl.ds(n, 1), :] + kv

        lax.fori_loop(0, nblk, phase2, jnp.zeros((HEAD_DIM, HEAD_DIM), F32))

        def phase2r(i, dst):
            n = nblk - 1 - i
            u = dd_all[n]
            dd_all[n] = dst
            return u + dst * dec_all[pl.ds(n, 1), :]

        lax.fori_loop(0, nblk, phase2r, jnp.zeros((HEAD_DIM, HEAD_DIM), F32))

        def phase3(n, dlb):
            rb, r0 = rows_of(n)
            q, v = q_ref[pl.ds(r0, hb), :], v_ref[pl.ds(r0, hb), :]
            sg, f, g, k = _hgrn_gates(f_ref[pl.ds(r0, hb), :], lbv)
            cum = cum_buf[pl.ds(r0, hb), :]
            tot = jnp.sum(g, axis=0, keepdims=True)
            dob = load_do(rb)
            st, dst = st_all[n], dd_all[n]
            e_cum = jnp.exp(cum)
            e_rest = jnp.exp(tot - cum)
            dq = jnp.dot(dob.astype(BF16), st.astype(BF16), preferred_element_type=F32) * e_cum
            dk_inter = jnp.dot(v.astype(BF16), dst.astype(BF16), preferred_element_type=F32) * e_rest
            dv = lax.dot_general((k * e_rest).astype(BF16), dst.astype(BF16), (((1,), (1,)), ((), ())), preferred_element_type=F32)
            cum_blk[...] = cum
            k_blk[...] = k
            v_blk[...] = v
            for s in range(hb):
                e = jnp.where(mask(s), jnp.exp(jnp.minimum(cum - cum_blk[s:s + 1, :], 0.0)), 0.0)
                a_s = jnp.sum(q * e * k_blk[s:s + 1, :], axis=-1, keepdims=True)
                da_s = jnp.sum(dob * v_blk[s:s + 1, :], axis=-1, keepdims=True)
                gs = da_s * e
                dq = dq + gs * k_blk[s:s + 1, :]
                dk_blk[s:s + 1, :] = jnp.sum(gs * q, axis=0, keepdims=True)
                dv_blk[s:s + 1, :] = jnp.sum(a_s * dob, axis=0, keepdims=True)
            dk = dk_inter + dk_blk[...]
            dv = dv + dv_blk[...]
            d_tot = jnp.sum(k * dk_inter, axis=0, keepdims=True) + jnp.exp(tot) * jnp.sum(dst * st, axis=0, keepdims=True)
            dcum = q * dq - k * dk + jnp.where(tio == last_row, d_tot, 0.0)
            dg = jnp.dot(trit_ref[...], dcum, precision=HIGHEST, preferred_element_type=F32)
            df = dg / f - dk
            if has_prev:
                dq = dq + pq_ref[pl.ds(r0, hb), :].astype(F32)
                dv = dv + pv_ref[pl.ds(r0, hb), :].astype(F32)
            dq_ref[pl.ds(r0, hb), :] = dq.astype(dq_ref.dtype)
            dv_ref[pl.ds(r0, hb), :] = dv.astype(dv_ref.dtype)
            dfl_ref[pl.ds(r0, hb), :] = (df * (1.0 - lbv) * sg * (1.0 - sg)).astype(dfl_ref.dtype)
            return dlb + jnp.sum(df * (1.0 - sg), axis=0, keepdims=True)

        dlb_ref[...] = lax.fori_loop(0, nblk, phase3, jnp.zeros((1, HEAD_DIM), F32))

    col = lambda seg: pl.BlockSpec((tt, HEAD_DIM), lambda h, seg=seg: (0, seg * N_HEADS + h))
    head = pl.BlockSpec((tt, HEAD_DIM), lambda h: (0, h))
    lbs = pl.BlockSpec((1, HEAD_DIM), lambda h: (0, h))
    tris = pl.BlockSpec((hb, hb), lambda h: (0, 0))
    in_specs = [col(0), col(seg_f), col(3), lbs, tris, tris, pl.BlockSpec((do.shape[0], HEAD_DIM), lambda h: (0, h))]
    args = [p, p, p, lb, tri, tri_t, do]
    mid = F32 if not has_prev else BF16
    if has_prev:
        in_specs += [head, head]
        args += list(prev)
    w = N_HEADS * HEAD_DIM
    blk = pltpu.VMEM((hb, HEAD_DIM), F32)
    return _pcall(
        body, name="hgrn_bwd_rev" if reverse else "hgrn_bwd", grid=(N_HEADS,), in_specs=in_specs,
        out_specs=[head, head, head, lbs],
        out_shape=[_sds((tt, w), mid), _sds((tt, w), BF16), _sds((tt, w), mid), _sds((1, w), F32)],
        scratch=[pltpu.VMEM((nblk, HEAD_DIM, HEAD_DIM), F32), pltpu.VMEM((nblk, HEAD_DIM, HEAD_DIM), F32), pltpu.VMEM((nblk, HEAD_DIM), F32),
                 pltpu.VMEM((tt, HEAD_DIM), F32), blk, blk, blk, blk, blk])(*args)


def _na_geometry(rows):
    r = pl.program_id(1)
    rs = jnp.clip(r - WIN_R // 2, 0, rows - WIN_R)
    return r, rs, r - rs


def _na_scores(q, kb, kc, bias):
    scale = HEAD_DIM ** -0.5
    nt = (((1,), (1,)), ((), ()))
    sb = lax.dot_general(q, kb, nt, preferred_element_type=F32) * scale + bias
    sc = lax.dot_general(q, kc, nt, preferred_element_type=F32) * scale
    m = jnp.maximum(jnp.max(sb, axis=-1, keepdims=True), jnp.max(sc, axis=-1, keepdims=True))
    pb, pc = jnp.exp(sb - m), jnp.exp(sc - m)
    inv = 1.0 / (jnp.sum(pb, axis=-1, keepdims=True) + jnp.sum(pc, axis=-1, keepdims=True))
    return pb * inv, pc * inv


def _na_fwd(qn, kall, p, seg_v, bias, n_ctx_rows):
    t, tt = qn.shape[0], kall.shape[0]
    rows = t // GRID_W
    nband = WIN_R * GRID_W

    def body(q_ref, k_ref, v_ref, b_ref, o_ref):
        r, rs, _ = _na_geometry(rows)
        k0 = pl.multiple_of(n_ctx_rows + rs * GRID_W, GRID_W)
        q = q_ref[...]
        kb, kc = k_ref[pl.ds(k0, nband), :], k_ref[pl.ds(0, n_ctx_rows), :]
        vb, vc = v_ref[pl.ds(k0, nband), :].astype(BF16), v_ref[pl.ds(0, n_ctx_rows), :].astype(BF16)
        pb, pc = _na_scores(q, kb, kc, b_ref[...])
        o = jnp.dot(pb.astype(BF16), vb, preferred_element_type=F32) + jnp.dot(pc.astype(BF16), vc, preferred_element_type=F32)
        o_ref[...] = o.astype(o_ref.dtype)

    return _pcall(
        body, name="na_fwd", grid=(N_HEADS, rows),
        in_specs=[pl.BlockSpec((GRID_W, HEAD_DIM), lambda h, r: (r, h)),
                  pl.BlockSpec((tt, HEAD_DIM), lambda h, r: (0, h)),
                  pl.BlockSpec((tt, HEAD_DIM), lambda h, r: (0, seg_v * N_HEADS + h)),
                  pl.BlockSpec((None, None, GRID_W, nband), lambda h, r: (h, r - jnp.clip(r - WIN_R // 2, 0, rows - WIN_R), 0, 0))],
        out_specs=pl.BlockSpec((GRID_W, HEAD_DIM), lambda h, r: (r, h)),
        out_shape=_sds((t, N_HEADS * HEAD_DIM), BF16))(qn, kall, p, bias)


def _na_bwd(qn, kall, p, seg_v, bias, do, n_ctx_rows):
    t, tt = qn.shape[0], kall.shape[0]
    rows = t // GRID_W
    nband = WIN_R * GRID_W
    scale = HEAD_DIM ** -0.5
    tn = (((0,), (0,)), ((), ()))
    nt = (((1,), (1,)), ((), ()))

    def body(q_ref, k_ref, v_ref, b_ref, do_ref, dq_ref, dk_ref, dv_ref, db_ref, dv_acc):
        r, rs, var = _na_geometry(rows)
        k0 = pl.multiple_of(n_ctx_rows + rs * GRID_W, GRID_W)
        q = q_ref[...]
        kb, kc = k_ref[pl.ds(k0, nband), :], k_ref[pl.ds(0, n_ctx_rows), :]
        vb, vc = v_ref[pl.ds(k0, nband), :].astype(BF16), v_ref[pl.ds(0, n_ctx_rows), :].astype(BF16)
        pb, pc = _na_scores(q, kb, kc, b_ref[...])
        dob = do_ref[...].astype(BF16)
        o = jnp.dot(pb.astype(BF16), vb, preferred_element_type=F32) + jnp.dot(pc.astype(BF16), vc, preferred_element_type=F32)
        delta = jnp.sum(do_ref[...].astype(F32) * o, axis=-1, keepdims=True)
        dsb = pb * (lax.dot_general(dob, vb, nt, preferred_element_type=F32) - delta)
        dsc = pc * (lax.dot_general(dob, vc, nt, preferred_element_type=F32) - delta)
        dsb16, dsc16 = dsb.astype(BF16), dsc.astype(BF16)
        dq_ref[...] = (jnp.dot(dsb16, kb, preferred_element_type=F32) + jnp.dot(dsc16, kc, preferred_element_type=F32)) * scale

        @pl.when(r == 0)
        def _():
            dk_ref[...] = jnp.zeros_like(dk_ref)
            dv_acc[...] = jnp.zeros_like(dv_acc)

        dk_ref[pl.ds(k0, nband), :] += lax.dot_general(dsb16, q, tn, preferred_element_type=F32) * scale
        dk_ref[pl.ds(0, n_ctx_rows), :] += lax.dot_general(dsc16, q, tn, preferred_element_type=F32) * scale
        dv_acc[pl.ds(k0, nband), :] += lax.dot_general(pb.astype(BF16), dob, tn, preferred_element_type=F32)
        dv_acc[pl.ds(0, n_ctx_rows), :] += lax.dot_general(pc.astype(BF16), dob, tn, preferred_element_type=F32)

        @pl.when(r == rows - 1)
        def _():
            dv_ref[...] = dv_acc[...].astype(dv_ref.dtype)

        fresh = jnp.logical_or(r <= WIN_R // 2, r > rows - WIN_R // 2)

        @pl.when(fresh)
        def _():
            db_ref[...] = dsb

        @pl.when(jnp.logical_not(fresh))
        def _():
            db_ref[...] += dsb

    variant = lambda h, r: (h, r - jnp.clip(r - WIN_R // 2, 0, rows - WIN_R), 0, 0)
    head_all = pl.BlockSpec((tt, HEAD_DIM), lambda h, r: (0, h))
    qspec = pl.BlockSpec((GRID_W, HEAD_DIM), lambda h, r: (r, h))
    w = N_HEADS * HEAD_DIM
    return _pcall(
        body, name="na_bwd", grid=(N_HEADS, rows),
        in_specs=[qspec, head_all, pl.BlockSpec((tt, HEAD_DIM), lambda h, r: (0, seg_v * N_HEADS + h)),
                  pl.BlockSpec((None, None, GRID_W, nband), variant), qspec],
        out_specs=[qspec, head_all, head_all, pl.BlockSpec((None, None, GRID_W, nband), variant)],
        out_shape=[_sds((t, w), F32), _sds((tt, w), F32), _sds((tt, w), BF16), _sds((N_HEADS, WIN_R, GRID_W, nband), F32)],
        scratch=[pltpu.VMEM((tt, HEAD_DIM), F32)])(qn, kall, p, bias, do)


def _na_tables(t, n_ctx_rows):
    half, nf = HEAD_DIM // 2, HEAD_DIM // 4
    pos = np.arange(t)
    lane = np.arange(HEAD_DIM)
    inv = ROPE_THETA ** (-(np.arange(nf, dtype=np.float32)) / nf)
    which = np.where(lane < half, pos[:, None] // GRID_W, pos[:, None] % GRID_W).astype(np.float32)
    ang = which * inv[lane % nf][None, :]
    first = (lane % half) < nf
    cos = np.concatenate([np.ones((n_ctx_rows, HEAD_DIM), np.float32), np.cos(ang).astype(np.float32)])
    sin = np.concatenate([np.zeros((n_ctx_rows, HEAD_DIM), np.float32), np.where(first[None, :], -np.sin(ang), np.sin(ang)).astype(np.float32)])
    partner = np.where(first, lane + nf, lane - nf)
    perm = np.zeros((HEAD_DIM, HEAD_DIM), np.float32)
    perm[partner, lane] = 1.0
    w = np.arange(GRID_W)
    dc = np.clip(w[None, :] - w[:, None], -(WIN_C - 1), WIN_C - 1) + WIN_C - 1
    onehot = np.zeros((32, GRID_W * GRID_W), np.float32)
    onehot[dc.reshape(-1), np.arange(GRID_W * GRID_W)] = 1.0
    cs = np.clip(w - WIN_C // 2, 0, GRID_W - WIN_C)
    col_in = (w[None, :] >= cs[:, None]) & (w[None, :] < cs[:, None] + WIN_C)
    onehot *= col_in.reshape(1, -1)
    neg = np.where(col_in, 0.0, NEG_BIG).astype(np.float32)
    return jnp.asarray(cos), jnp.asarray(sin), jnp.asarray(perm), jnp.asarray(onehot), jnp.asarray(neg)


def _bias_slabs(rel_bias, onehot, neg):
    nr = 2 * WIN_R - 1
    rb = jnp.pad(rel_bias.reshape(N_HEADS * nr, 2 * WIN_C - 1), ((0, 0), (0, 1)))
    spread = _mm_f32(rb, onehot, "bias_spread").reshape(N_HEADS, nr, GRID_W, GRID_W)
    slabs = [spread[:, WIN_R - 1 - v:2 * WIN_R - 1 - v] for v in range(WIN_R)]
    b = jnp.stack(slabs, axis=1) + neg[None, None, None]
    return b.transpose(0, 1, 3, 2, 4).reshape(N_HEADS, WIN_R, GRID_W, WIN_R * GRID_W)


def _bias_grad(dbias, onehot):
    nr = 2 * WIN_R - 1
    d = dbias.reshape(N_HEADS, WIN_R, GRID_W, WIN_R, GRID_W).transpose(0, 1, 3, 2, 4)
    tot = jnp.zeros((N_HEADS, nr, GRID_W, GRID_W), F32)
    for v in range(WIN_R):
        tot = tot + jnp.pad(d[:, v], ((0, 0), (WIN_R - 1 - v, v), (0, 0), (0, 0)))
    g = _mm_f32(tot.reshape(N_HEADS * nr, GRID_W * GRID_W), onehot, "bias_grad", trans_b=True)
    return g[:, :2 * WIN_C - 1].reshape(1, N_HEADS, nr, 2 * WIN_C - 1)


def _shift_rows(u, up):
    n = u.shape[0]
    tio = lax.broadcasted_iota(jnp.int32, u.shape, 0)
    if up:
        return jnp.where(tio == n - 1, 0.0, pltpu.roll(u, n - 1, 0))
    return jnp.where(tio == 0, 0.0, pltpu.roll(u, 1, 0))


def _conv3(u, w_ref, b_ref):
    um, up = _shift_rows(u, False), _shift_rows(u, True)
    return um, up, um * w_ref[0:1, :] + u * w_ref[1:2, :] + up * w_ref[2:3, :] + b_ref[...]


def _ffn_act_fwd(u0, t3, cw, cb):
    t, n = u0.shape
    tc = _pick(n, (256, 128))

    def body(u_ref, t_ref, w_ref, b_ref, a_ref):
        _, _, uc = _conv3(u_ref[...], w_ref, b_ref)
        a_ref[...] = (uc * jax.nn.sigmoid(uc) * t_ref[...]).astype(a_ref.dtype)

    col = lambda rows_: pl.BlockSpec((rows_, tc), lambda j: (0, j))
    return _pcall(body, name="ffn_act_fwd", grid=(n // tc,), in_specs=[col(t), col(t), col(8), col(1)], out_specs=col(t),
                  out_shape=_sds((t, n), BF16))(u0, t3, cw, cb)


def _ffn_act_bwd(u0, t3, cw, cb, da):
    t, n = u0.shape
    tc = _pick(n, (256, 128))

    def body(u_ref, t_ref, w_ref, b_ref, da_ref, du_ref, dt_ref, dw_ref, db_ref):
        u = u_ref[...]
        um, up, uc = _conv3(u, w_ref, b_ref)
        sg = jax.nn.sigmoid(uc)
        dav = da_ref[...].astype(F32)
        dt_ref[...] = (dav * uc * sg).astype(dt_ref.dtype)
        duc = dav * t_ref[...] * sg * (1.0 + uc * (1.0 - sg))
        du = _shift_rows(duc, True) * w_ref[0:1, :] + duc * w_ref[1:2, :] + _shift_rows(duc, False) * w_ref[2:3, :]
        du_ref[...] = du.astype(du_ref.dtype)
        dw_ref[...] = jnp.zeros_like(dw_ref)
        dw_ref[0:1, :] = jnp.sum(duc * um, axis=0, keepdims=True)
        dw_ref[1:2, :] = jnp.sum(duc * u, axis=0, keepdims=True)
        dw_ref[2:3, :] = jnp.sum(duc * up, axis=0, keepdims=True)
        db_ref[...] = jnp.sum(duc, axis=0, keepdims=True)

    col = lambda rows_: pl.BlockSpec((rows_, tc), lambda j: (0, j))
    return _pcall(body, name="ffn_act_bwd", grid=(n // tc,), in_specs=[col(t), col(t), col(8), col(1), col(t)],
                  out_specs=[col(t), col(t), col(8), col(1)],
                  out_shape=[_sds((t, n), BF16), _sds((t, n), BF16), _sds((8, n), F32), _sds((1, n), F32)])(u0, t3, cw, cb, da)


def _adam_math(g, w, m, v):
    m2 = ADAM_B1 * m + (1.0 - ADAM_B1) * g
    v2 = ADAM_B2 * v + (1.0 - ADAM_B2) * (g * g)
    m_hat = m2 / (1.0 - ADAM_B1 ** ADAM_STEP)
    v_hat = v2 / (1.0 - ADAM_B2 ** ADAM_STEP)
    return -ADAM_LR * (m_hat / (jnp.sqrt(v_hat) + ADAM_EPS) + ADAM_WD * w), m2, v2


def _adam_big(parts, w, m, v, name):
    r, c = w.shape
    npart, _, cp = parts.shape
    tr = _pick(r, (64, 32, 16, 8))

    def body(p_ref, w_ref, m_ref, v_ref, g_ref, d_ref, m2_ref, v2_ref):
        g = p_ref[0, :, 0:c].astype(F32)
        for i in range(1, npart):
            g = g + p_ref[i, :, 0:c].astype(F32)
        d, m2, v2 = _adam_math(g, w_ref[...], m_ref[...], v_ref[...])
        g_ref[...] = g
        d_ref[...] = d
        m2_ref[...] = m2
        v2_ref[...] = v2

    row = pl.BlockSpec((tr, c), lambda i: (i, 0))
    return _pcall(body, name=name, grid=(r // tr,),
                  in_specs=[pl.BlockSpec((npart, tr, cp), lambda i: (0, i, 0)), row, row, row],
                  out_specs=[row] * 4, out_shape=[_sds((r, c), F32)] * 4)(parts, w, m, v)


def _adam_small(g, w, m, v):
    def body(g_ref, w_ref, m_ref, v_ref, d_ref, m2_ref, v2_ref):
        d_ref[...], m2_ref[...], v2_ref[...] = _adam_math(g_ref[...], w_ref[...], m_ref[...], v_ref[...])

    return _pcall(body, name="adam_small", out_shape=[_sds(g.shape, F32)] * 3)(g, w, m, v)


def _sum_parts(parts, name):
    def body(p_ref, o_ref):
        s = p_ref[0]
        for i in range(1, N_DEV):
            s = s + p_ref[i]
        o_ref[...] = s

    return _pcall(body, name=name, out_shape=_sds(parts.shape[1:], F32))(parts)


class _Pack:
    def __init__(self, shapes):
        self.shapes = shapes
        self.sizes = [int(np.prod(s)) for s in shapes]
        self.padded = [-(-n // (8 * LANE)) * 8 * LANE for n in self.sizes]
        self.offs = np.concatenate([[0], np.cumsum(self.padded)]).tolist()

    def pack(self, arrs):
        flat = [jnp.pad(a.reshape(-1).astype(F32), (0, p - n)) for a, n, p in zip(arrs, self.sizes, self.padded)]
        return jnp.concatenate(flat).reshape(-1, LANE)

    def unpack(self, slab):
        flat = slab.reshape(-1)
        return [flat[o:o + n].reshape(s) for o, n, s in zip(self.offs, self.sizes, self.shapes)]


def kernel(x, c, ctx, c_ctx, ada_w, ada_b, norm1_g, norm2_g, w_in, hgrn_lb_logits, hgrn_norm_g, na_q_norm_g, na_k_norm_g, na_rel_bias, w_branch_a, w_branch_b, w_out, ffn_w1, ffn_w3, ffn_conv_w, ffn_conv_b, ffn_w2, loss_target, m_c_ctx, m_ada_w, m_ada_b, m_norm1_g, m_norm2_g, m_w_in, m_hgrn_lb_logits, m_hgrn_norm_g, m_na_q_norm_g, m_na_k_norm_g, m_na_rel_bias, m_w_branch_a, m_w_branch_b, m_w_out, m_ffn_w1, m_ffn_w3, m_ffn_conv_w, m_ffn_conv_b, m_ffn_w2, v_c_ctx, v_ada_w, v_ada_b, v_norm1_g, v_norm2_g, v_w_in, v_hgrn_lb_logits, v_hgrn_norm_g, v_na_q_norm_g, v_na_k_norm_g, v_na_rel_bias, v_w_branch_a, v_w_branch_b, v_w_out, v_ffn_w1, v_ffn_w3, v_ffn_conv_w, v_ffn_conv_b, v_ffn_w2):
    t, d = x.shape[1], x.shape[2]
    n_ctx = ctx.shape[1]
    tt = n_ctx + t
    hw = N_HEADS * HEAD_DIM
    ci = w_in.shape[2]
    ca = ada_w.shape[2]
    ff_l = ffn_w1.shape[2]
    ff_p = -(-ff_l // LANE) * LANE
    rows = t // GRID_W
    assert rows >= WIN_R and t % GRID_W == 0 and n_ctx % GRID_W == 0 and ci % LANE == 0 and d % LANE == 0
    me = 4 * lax.axis_index("x") + 2 * lax.axis_index("y") + lax.axis_index("c")
    tr = _pick(n_ctx, (256, 128, 64))
    n_ctx_tiles = n_ctx // tr

    pad_c = lambda w: jnp.pad(w, ((0, 0), (0, ff_p - ff_l)))
    locals16 = [
        _cast_bf16(w_in[0], "cast_w_in"), _cast_bf16(w_branch_a[0], "cast_w_a"), _cast_bf16(w_branch_b[0], "cast_w_b"),
        _cast_bf16(w_out[0], "cast_w_out"), _cast_bf16(pad_c(ffn_w1[0]), "cast_w1"), _cast_bf16(pad_c(ffn_w3[0]), "cast_w3"),
        _cast_bf16(jnp.pad(ffn_w2[0], ((0, ff_p - ff_l), (0, 0))), "cast_w2"),
    ]
    small_in = [c, hgrn_lb_logits.reshape(4, HEAD_DIM), jnp.pad(ffn_conv_w[0], ((0, 5), (0, ff_p - ff_l)))]
    g_in, g_a, g_b, g_out, g_w1, g_w3, g_w2, c_all, lb_parts, cw_all = _exchange(locals16 + small_in, "gather_weights", scatter=False)
    g_out = g_out.reshape(1, d, d)
    g_w2 = g_w2.reshape(1, N_DEV * ff_p, d)
    ff = N_DEV * ff_p

    cc = jnp.concatenate([c_all.reshape(N_DEV, d), jnp.broadcast_to(c_ctx[None, :], (N_DEV, d))], axis=0)
    act = _tiled(lambda v: v * jax.nn.sigmoid(v), [_const(cc)], [(cc.shape, BF16, cc.shape, lambda *_: (0, 0))], (), "silu_c")[0]
    ada16 = _cast_bf16(ada_w[0], "cast_ada")
    mod_cols = _mm_xw(act, ada16.reshape(1, d, ca), F32, "ada_fwd")
    mod_all = _exchange([mod_cols], "gather_mod", scatter=False)[0]
    mod_all = mod_all.transpose(1, 0, 2).reshape(2 * N_DEV, N_MOD * d) + ada_b
    mod_l = lax.dynamic_slice_in_dim(mod_all, me, 1, axis=0).reshape(N_MOD, 1, d)
    mod_c = mod_all[N_DEV:N_DEV + 1].reshape(N_MOD, 1, d)
    mods1 = jnp.stack([mod_c[0:2], mod_l[0:2]])
    mods2 = jnp.stack([mod_l[3:5], mod_l[3:5]])
    gate1, gate2 = mod_l[2], mod_l[5]

    xcat = jnp.concatenate([ctx[0], x[0]], axis=0)
    hcat = _norm_mod_fwd(xcat, norm1_g, mods1, n_ctx_tiles, tr)
    p = _mm_xw(hcat, g_in, F32, "in_proj")
    lb_logits = lb_parts.transpose(1, 0, 2).reshape(2, 2, hw)
    lb_soft = _tiled(lambda a, b: (1.0 / (1.0 + jnp.exp(b - a)),), [_const(lb_logits[:, 0]), _const(lb_logits[:, 1])],
                     [((2, hw), F32, (2, hw), lambda *_: (0, 0))], (), "lb_softmax")[0]
    lb_f, lb_b = lb_soft[0:1], lb_soft[1:2]
    o_f = _hgrn_fwd(p, lb_f, 1, False, n_ctx)
    o_b = _hgrn_fwd(p, lb_b, 2, True, n_ctx)

    cos, sin, perm, onehot, neg = _na_tables(t, n_ctx)
    bias = _bias_slabs(na_rel_bias[0], onehot, neg)
    hcol = lambda seg, off=0: (lambda i, h: (i + off, seg * N_HEADS + h))
    tq = GRID_W * 2
    lat0 = n_ctx // tq
    qk_fn = lambda tv, g, cs, sn, pm: (_f_qk(tv, g, cs, sn, pm),)
    tab = lambda a, off=0: (a, (tq, HEAD_DIM), lambda i, h: (i + off, 0))
    qn = _tiled(qk_fn, [(p, (tq, HEAD_DIM), hcol(5, lat0)), _const(na_q_norm_g), tab(cos, lat0), tab(sin, lat0), _const(perm)],
                [((t, hw), BF16, (tq, HEAD_DIM), lambda i, h: (i, h))], (t // tq, N_HEADS), "q_norm_rope")[0]
    kall = _tiled(qk_fn, [(p, (tq, HEAD_DIM), hcol(6)), _const(na_k_norm_g), tab(cos), tab(sin), _const(perm)],
                  [((tt, hw), BF16, (tq, HEAD_DIM), lambda i, h: (i, h))], (tt // tq, N_HEADS), "k_norm_rope")[0]
    y_b = _na_fwd(qn, kall, p, 7, bias, n_ctx)

    trh = tr
    lat_h = n_ctx // trh
    ospec = lambda a: (a, (trh, HEAD_DIM), lambda i, h: (i + lat_h, h))
    y_a = _tiled(lambda a, b, gt, g: (_f_readout(a, b, gt, g),),
                 [ospec(o_f), ospec(o_b), (p, (trh, HEAD_DIM), hcol(4, lat_h)), _const(hgrn_norm_g)],
                 [((t, hw), BF16, (trh, HEAD_DIM), lambda i, h: (i, h))], (t // trh, N_HEADS), "hgrn_readout")[0]

    p_a = _mm_xw(y_a, g_a, F32, "branch_a")
    p_b = _mm_xw(y_b, g_b, F32, "branch_b")
    td = _pick(d, (512, 256, 128))
    nd_t = d // td
    lat_r = n_ctx // tr
    gcol = lambda k: (p, (tr, td), lambda i, j, k=k: (i + lat_r, 8 * hw // td + k * nd_t + j))
    dtile = lambda a: (a, (tr, td), lambda i, j: (i, j))
    z = _tiled(lambda ga, gb, pa, pb: (_f_merge(ga, gb, pa, pb),), [gcol(0), gcol(1), dtile(p_a), dtile(p_b)],
               [((t, d), BF16, (tr, td), lambda i, j: (i, j))], (t // tr, nd_t), "merge")[0]
    mix = _mm_xw(z, g_out, F32, "out_proj")
    xl = x[0]
    x_mid = _tiled(lambda a, g, m_: (_f_resid(a, g, m_),), [_rows(xl, tr), _const(gate1), _rows(mix, tr)],
                   [((t, d), F32, (tr, d), lambda i: (i, 0))], (t // tr,), "resid1")[0]

    h2 = _norm_mod_fwd(x_mid, norm2_g, mods2, 0, tr)
    u0 = _mm_xw(h2, g_w1, F32, "ffn_up1")
    t3 = _mm_xw(h2, g_w3, F32, "ffn_up3")
    cw_full = cw_all.transpose(1, 0, 2).reshape(8, ff)
    cb_full = jnp.pad(ffn_conv_b.reshape(N_DEV, ff_l), ((0, 0), (0, ff_p - ff_l))).reshape(1, ff)
    a_act = _ffn_act_fwd(u0, t3, cw_full, cb_full)
    f_out = _mm_xw(a_act, g_w2, F32, "ffn_down")

    def loss_fn(xm, g, f, tg):
        err = xm + g * f - tg
        return err * (1.0 / d), jnp.sum(err * err, axis=0, keepdims=True) * (0.5 / d), jnp.sum(err * (1.0 / d) * f, axis=0, keepdims=True)

    dy, loss_cols, d_gate2 = _tiled(loss_fn, [_rows(x_mid, tr), _const(gate2), _rows(f_out, tr), _rows(loss_target[0], tr)],
                                    [((t, d), F32, (tr, d), lambda i: (i, 0)), ((1, d), F32, (1, d), lambda i: (0, 0)),
                                     ((1, d), F32, (1, d), lambda i: (0, 0))], (t // tr,), "loss", acc=(1, 2))
    loss = lax.psum(jnp.sum(loss_cols), ("x", "y", "c"))

    df = _tiled(lambda a, g: (a * g,), [_rows(dy, tr), _const(gate2)], [((t, d), BF16, (tr, d), lambda i: (i, 0))], (t // tr,), "d_ffn_out")[0]
    d_w2 = _mm_xtdy(a_act, df, 1, BF16, "d_w2")
    da = _mm_dyw(df, g_w2, BF16, "d_act")
    du0, dt3, d_cw, d_cb = _ffn_act_bwd(u0, t3, cw_full, cb_full, da)
    d_w1 = _mm_xtdy(h2, du0, N_DEV, BF16, "d_w1")
    d_w3 = _mm_xtdy(h2, dt3, N_DEV, BF16, "d_w3")
    dh2 = _mm_dyw(du0, g_w1, F32, "d_h2_a")
    dh2 = _mm_dyw(dt3, g_w3, F32, "d_h2_b", init=dh2)
    dx_mid, d_norm2, d_mods2 = _norm_mod_bwd(x_mid, norm2_g, mods2, dh2, dy, 0, tr, "norm_mod2_bwd")

    dm, d_gate1 = _tiled(lambda dxm, g, m_: (dxm * g, jnp.sum(dxm * m_, axis=0, keepdims=True)),
                         [_rows(dx_mid, tr), _const(gate1), _rows(mix, tr)],
                         [((t, d), BF16, (tr, d), lambda i: (i, 0)), ((1, d), F32, (1, d), lambda i: (0, 0))], (t // tr,), "d_resid1", acc=(1,))
    d_wout = _mm_xtdy(z, dm, 1, BF16, "d_w_out")
    dz = _mm_dyw(dm, g_out, F32, "d_merge")

    def merge_bwd(ga, gb, pa, pb, dzv):
        _, vjp = jax.vjp(_f_merge, ga, gb, pa, pb)
        return vjp(dzv)

    dga, dgb, dpa, dpb = _tiled(merge_bwd, [gcol(0), gcol(1), dtile(p_a), dtile(p_b), dtile(dz)],
                                [((t, d), BF16, (tr, td), lambda i, j: (i, j))] * 4, (t // tr, nd_t), "merge_bwd")
    d_wa = _mm_xtdy(y_a, dpa, N_DEV, BF16, "d_w_a")
    d_wb = _mm_xtdy(y_b, dpb, N_DEV, BF16, "d_w_b")
    dy_a = _mm_dyw(dpa, g_a, F32, "d_y_a")
    dy_b = _mm_dyw(dpb, g_b, BF16, "d_y_b")

    def readout_bwd(a, b, gt, g, ct):
        _, vjp = jax.vjp(_f_readout, a, b, gt, g)
        da_, _, dgt, dg = vjp(ct)
        return da_, dgt, dg

    hsp = lambda dt: ((t, hw), dt, (trh, HEAD_DIM), lambda i, h: (i, h))
    do_h, d_gate_o, d_hnorm = _tiled(
        readout_bwd, [ospec(o_f), ospec(o_b), (p, (trh, HEAD_DIM), hcol(4, lat_h)), _const(hgrn_norm_g), (dy_a, (trh, HEAD_DIM), lambda i, h: (i, h))],
        [hsp(F32), hsp(BF16), ((1, HEAD_DIM), F32, (1, HEAD_DIM), lambda i, h: (0, 0))], (t // trh, N_HEADS), "readout_bwd", acc=(2,))
    dq1, dfl_f, dv1, dlb_f = _hgrn_bwd(p, lb_f, do_h, 1, False, n_ctx, None)
    dq_h, dfl_b, dv_h, dlb_b = _hgrn_bwd(p, lb_b, do_h, 2, True, n_ctx, (dq1, dv1))

    dqn, dkall, dv_na, dbias = _na_bwd(qn, kall, p, 7, bias, dy_b, n_ctx)

    def qk_bwd(tv, g, cs, sn, pm, ct):
        _, vjp = jax.vjp(lambda a, b: _f_qk(a, b, cs, sn, pm), tv, g)
        return vjp(ct)

    d_pq, d_qnorm = _tiled(qk_bwd, [(p, (tq, HEAD_DIM), hcol(5, lat0)), _const(na_q_norm_g), tab(cos, lat0), tab(sin, lat0), _const(perm),
                                    (dqn, (tq, HEAD_DIM), lambda i, h: (i, h))],
                           [((t, hw), BF16, (tq, HEAD_DIM), lambda i, h: (i, h)), ((1, HEAD_DIM), F32, (1, HEAD_DIM), lambda i, h: (0, 0))],
                           (t // tq, N_HEADS), "q_norm_rope_bwd", acc=(1,))
    d_pk, d_knorm = _tiled(qk_bwd, [(p, (tq, HEAD_DIM), hcol(6)), _const(na_k_norm_g), tab(cos), tab(sin), _const(perm),
                                    (dkall, (tq, HEAD_DIM), lambda i, h: (i, h))],
                           [((tt, hw), BF16, (tq, HEAD_DIM), lambda i, h: (i, h)), ((1, HEAD_DIM), F32, (1, HEAD_DIM), lambda i, h: (0, 0))],
                           (tt // tq, N_HEADS), "k_norm_rope_bwd", acc=(1,))

    zc = lambda w_: jnp.zeros((n_ctx, w_), BF16)
    lat_only = lambda a: jnp.concatenate([zc(a.shape[1]), a], axis=0)
    dp = jnp.concatenate([dq_h, dfl_f, dfl_b, dv_h, lat_only(d_gate_o), lat_only(d_pq), d_pk, dv_na, lat_only(dga), lat_only(dgb)], axis=1)
    d_win = _mm_xtdy(hcat, dp, N_DEV, BF16, "d_w_in")
    dhcat = _mm_dyw(dp, g_in, BF16, "d_hcat")
    zero_ctx = jnp.concatenate([jnp.zeros((n_ctx, d), F32), dx_mid], axis=0)
    dxcat, d_norm1, d_mods1 = _norm_mod_bwd(xcat, norm1_g, mods1, dhcat, zero_ctx, n_ctx_tiles, tr, "norm_mod1_bwd")
    grad_x = dxcat[n_ctx:][None]

    zd = jnp.zeros((1, d), F32)
    dmod_l = jnp.concatenate([d_mods1[1, 0], d_mods1[1, 1], d_gate1, d_mods2[1, 0], d_mods2[1, 1], d_gate2], axis=1)
    dmod_c = jnp.concatenate([d_mods1[0, 0], d_mods1[0, 1], zd, zd, zd, zd], axis=1)
    dmods = jnp.concatenate([dmod_l, dmod_c], axis=0).reshape(2, N_DEV, ca).transpose(1, 0, 2)
    dmods = jnp.pad(dmods, ((0, 0), (0, 6), (0, 0)))
    got = _exchange([dmods], "scatter_dmod", scatter=True)[0]
    dm_rows = jnp.concatenate([got[:, 0], got[:, 1]], axis=0)
    d_ada = _mm_xtdy(act, dm_rows, 1, F32, "d_ada_w")[0]
    back = _mm_dyw(dm_rows, ada16.reshape(1, d, ca), F32, "d_silu_c")
    d_cctx_part = _tiled(lambda b, v: (jnp.sum(b[N_DEV:], axis=0, keepdims=True) * (jax.nn.sigmoid(v) * (1.0 + v * (1.0 - jax.nn.sigmoid(v)))),),
                         [_const(back), _const(c_ctx.reshape(1, d))], [((1, d), F32, (1, d), lambda *_: (0, 0))], (), "d_c_ctx")[0]

    d_rel = _bias_grad(dbias, onehot)
    d_lb_soft = jnp.concatenate([dlb_f, dlb_b], axis=0)
    d_lb0 = _tiled(lambda s, g: (g * s * (1.0 - s),), [_const(lb_soft), _const(d_lb_soft)], [((2, hw), F32, (2, hw), lambda *_: (0, 0))], (), "d_lb")[0]
    d_lb_full = jnp.stack([d_lb0, -d_lb0], axis=1)
    d_cw_l = d_cw[:3].reshape(3, N_DEV, ff_p)[:, :, :ff_l].reshape(1, 3, N_DEV * ff_l)
    d_cb_l = d_cb.reshape(N_DEV, ff_p)[:, :ff_l].reshape(1, N_DEV * ff_l)
    small = [d_cctx_part.reshape(d), (dmod_l + dmod_c), d_norm1, d_norm2, d_lb_full, d_hnorm, d_qnorm, d_knorm, d_rel, d_cw_l, d_cb_l]
    pk = _Pack([a.shape for a in small])
    tot = _sum_parts(_exchange([pk.pack(small)], "gather_small", scatter=False)[0], "sum_small")
    g_cctx, g_ada_b, g_n1, g_n2, g_lb, g_hn, g_qn, g_kn, g_rel, g_cw, g_cb = pk.unpack(tot)
    g_lb = lax.dynamic_slice_in_dim(g_lb, me * HEAD_DIM, HEAD_DIM, axis=2)
    g_cw = lax.dynamic_slice_in_dim(g_cw, me * ff_l, ff_l, axis=2)
    small_names = [("c_ctx", g_cctx, c_ctx, m_c_ctx, v_c_ctx), ("ada_b", g_ada_b, ada_b, m_ada_b, v_ada_b),
                   ("norm1_g", g_n1, norm1_g, m_norm1_g, v_norm1_g), ("norm2_g", g_n2, norm2_g, m_norm2_g, v_norm2_g),
                   ("hgrn_lb_logits", g_lb, hgrn_lb_logits, m_hgrn_lb_logits, v_hgrn_lb_logits),
                   ("hgrn_norm_g", g_hn, hgrn_norm_g, m_hgrn_norm_g, v_hgrn_norm_g), ("na_q_norm_g", g_qn, na_q_norm_g, m_na_q_norm_g, v_na_q_norm_g),
                   ("na_k_norm_g", g_kn, na_k_norm_g, m_na_k_norm_g, v_na_k_norm_g), ("na_rel_bias", g_rel, na_rel_bias, m_na_rel_bias, v_na_rel_bias),
                   ("ffn_conv_w", g_cw, ffn_conv_w, m_ffn_conv_w, v_ffn_conv_w), ("ffn_conv_b", g_cb, ffn_conv_b, m_ffn_conv_b, v_ffn_conv_b)]
    pk2 = _Pack([s[1].shape for s in small_names])
    sd, sm, sv = _adam_small(*[pk2.pack([s[i] for s in small_names]) for i in (1, 2, 3, 4)])
    sd, sm, sv = pk2.unpack(sd), pk2.unpack(sm), pk2.unpack(sv)
    res = {s[0]: (s[1], sd[i], sm[i], sv[i]) for i, s in enumerate(small_names)}

    d_wout_b = d_wout.reshape(N_DEV, d // N_DEV, d)
    d_w2_b = d_w2.reshape(N_DEV, ff_p, d)
    r_in, r_a, r_b, r_out, r_w1, r_w3, r_w2 = _exchange([d_win, d_wa, d_wb, d_wout_b, d_w1, d_w3, d_w2_b], "scatter_grads", scatter=True)
    res["w_in"] = _adam_big(r_in, w_in[0], m_w_in[0], v_w_in[0], "adam_w_in")
    res["w_branch_a"] = _adam_big(r_a, w_branch_a[0], m_w_branch_a[0], v_w_branch_a[0], "adam_w_a")
    res["w_branch_b"] = _adam_big(r_b, w_branch_b[0], m_w_branch_b[0], v_w_branch_b[0], "adam_w_b")
    res["w_out"] = _adam_big(r_out, w_out[0], m_w_out[0], v_w_out[0], "adam_w_out")
    res["ffn_w1"] = _adam_big(r_w1, ffn_w1[0], m_ffn_w1[0], v_ffn_w1[0], "adam_w1")
    res["ffn_w3"] = _adam_big(r_w3, ffn_w3[0], m_ffn_w3[0], v_ffn_w3[0], "adam_w3")
    res["ffn_w2"] = _adam_big(r_w2, ffn_w2[0], m_ffn_w2[0], v_ffn_w2[0], "adam_w2")
    res["ada_w"] = _adam_big(d_ada[None], ada_w[0], m_ada_w[0], v_ada_w[0], "adam_ada")
    for k in ("w_in", "w_branch_a", "w_branch_b", "w_out", "ffn_w1", "ffn_w3", "ffn_w2", "ada_w"):
        res[k] = tuple(a[None] for a in res[k])

    order = ["c_ctx", "ada_w", "ada_b", "norm1_g", "norm2_g", "w_in", "hgrn_lb_logits", "hgrn_norm_g", "na_q_norm_g", "na_k_norm_g",
             "na_rel_bias", "w_branch_a", "w_branch_b", "w_out", "ffn_w1", "ffn_w3", "ffn_conv_w", "ffn_conv_b", "ffn_w2"]
    shapes = {"c_ctx": c_ctx.shape, "ada_b": ada_b.shape, "norm1_g": norm1_g.shape, "norm2_g": norm2_g.shape,
              "hgrn_lb_logits": hgrn_lb_logits.shape, "hgrn_norm_g": hgrn_norm_g.shape, "na_q_norm_g": na_q_norm_g.shape,
              "na_k_norm_g": na_k_norm_g.shape, "na_rel_bias": na_rel_bias.shape, "ffn_conv_w": ffn_conv_w.shape, "ffn_conv_b": ffn_conv_b.shape}
    outs = [loss, grad_x]
    for part in range(4):
        for k in order:
            a = res[k][part]
            outs.append(a.reshape(shapes[k]) if k in shapes else a)
    return tuple(outs)
```

```python
import functools

import numpy as np
import jax
import jax.numpy as jnp
from jax import lax
from jax.experimental import pallas as pl
from jax.experimental.pallas import tpu as pltpu

F32 = jnp.float32
BF16 = jnp.bfloat16
HIGHEST = lax.Precision.HIGHEST

N_DEV = 8
MESH_ID = pl.DeviceIdType.MESH
LANE = 128
HEAD_DIM = 128
N_HEADS = 8
GRID_W = 64
WIN_R = 8
WIN_C = 16
ROPE_THETA = 10000.0
EPS = 1e-6
N_MOD = 6
HGRN_BLOCK = 16
NEG_BIG = -1e30
VMEM_LIMIT = 56 << 20

ADAM_LR = 0.001
ADAM_B1 = 0.9
ADAM_B2 = 0.999
ADAM_EPS = 1e-08
ADAM_WD = 0.01
ADAM_STEP = 10

HBM_SPEC = pl.BlockSpec(memory_space=pltpu.HBM)


_ORDER_AFTER = []


class _after:
    def __init__(self, *arrs):
        self.arrs = list(arrs)

    def __enter__(self):
        _ORDER_AFTER.extend(self.arrs)

    def __exit__(self, *exc):
        del _ORDER_AFTER[:]


def _pcall(body, *, name, out_shape, grid=None, in_specs=None, out_specs=None, scratch=(), aliases=None):
    kw = {}
    if grid is not None:
        kw["grid"] = grid
    extra = []
    if _ORDER_AFTER and in_specs is not None:
        extra = list(_ORDER_AFTER)
        del _ORDER_AFTER[:]
        n_in, n_extra, inner = len(in_specs), len(extra), body
        in_specs = list(in_specs) + [pl.BlockSpec(memory_space=pl.ANY)] * n_extra

        def body(*refs):
            return inner(*refs[:n_in], *refs[n_in + n_extra:])

    if extra:
        call = _pcall_inner(body, name, out_shape, kw, in_specs, out_specs, scratch, aliases)
        return lambda *args: call(*args, *extra)
    return _pcall_inner(body, name, out_shape, kw, in_specs, out_specs, scratch, aliases)


def _pcall_inner(body, name, out_shape, kw, in_specs, out_specs, scratch, aliases):
    if in_specs is not None:
        kw["in_specs"] = in_specs
    if out_specs is not None:
        kw["out_specs"] = out_specs
    if scratch:
        kw["scratch_shapes"] = list(scratch)
    if aliases:
        kw["input_output_aliases"] = aliases
    return pl.pallas_call(body, name=name, out_shape=out_shape,
                          compiler_params=pltpu.CompilerParams(vmem_limit_bytes=VMEM_LIMIT), **kw)


def _pick(dim, cands):
    for c in cands:
        if c <= dim and dim % c == 0:
            return c
    return dim


def _sds(shape, dtype):
    return jax.ShapeDtypeStruct(tuple(shape), dtype)


def _peers():
    x, y, c = lax.axis_index("x"), lax.axis_index("y"), lax.axis_index("c")
    out = []
    for k in range(1, N_DEV):
        px = 1 - x if (k >> 2) & 1 else x
        py = 1 - y if (k >> 1) & 1 else y
        pc = 1 - c if k & 1 else c
        out.append((k, (px, py, pc), 4 * px + 2 * py + pc))
    return 4 * x + 2 * y + c, out


def _exchange(arrs, name, scatter):
    n = len(arrs)

    def body(*refs):
        ins, outs = refs[:n], refs[n:2 * n]
        send, recv, loc = refs[2 * n:]
        me, peers = _peers()
        started = []
        for i in range(n):
            src = ins[i].at[me] if scatter else ins[i]
            cp = pltpu.make_async_copy(src, outs[i].at[me], loc.at[i])
            cp.start()
            started.append(cp)
        sends = []
        for k, peer, pidx in peers:
            for i in range(n):
                src = ins[i].at[pidx] if scatter else ins[i]
                cp = pltpu.make_async_remote_copy(src_ref=src, dst_ref=outs[i].at[me], send_sem=send.at[i * 7 + k - 1],
                                                  recv_sem=recv.at[i * 7 + k - 1], device_id=peer, device_id_type=MESH_ID)
                cp.start()
                sends.append(cp)
        for k, peer, pidx in peers:
            for i in range(n):
                src = ins[i].at[pidx] if scatter else ins[i]
                pltpu.make_async_remote_copy(src_ref=src, dst_ref=outs[i].at[pidx], send_sem=send.at[i * 7 + k - 1],
                                             recv_sem=recv.at[i * 7 + k - 1], device_id=peer, device_id_type=MESH_ID).wait_recv()
        for cp in sends:
            cp.wait_send()
        for cp in started:
            cp.wait()

    out_shape = [_sds(a.shape if scatter else (N_DEV,) + a.shape, a.dtype) for a in arrs]
    res = _pcall(body, name=name, out_shape=out_shape, in_specs=[HBM_SPEC] * n, out_specs=[HBM_SPEC] * n,
                 scratch=[pltpu.SemaphoreType.DMA((7 * n,)), pltpu.SemaphoreType.DMA((7 * n,)), pltpu.SemaphoreType.DMA((n,))])(*arrs)
    return list(res)


class _InFlight:
    def __init__(self, send, recv, srcs, lands, token, scatter):
        self.send, self.recv, self.srcs, self.lands, self.token, self.scatter = send, recv, srcs, lands, token, scatter


SEM_SPEC = pl.BlockSpec(memory_space=pltpu.SEMAPHORE)
SIDE_EFFECT = pltpu.SideEffectType.DATAFLOW_SIDE_EFFECTING


def _exchange_start(arrs, name, scatter):
    n = len(arrs)

    def body(*refs):
        ins, lands = refs[:n], refs[n:2 * n]
        send, recv = refs[2 * n], refs[2 * n + 1]
        token = refs[4 * n + 2]
        loc = refs[4 * n + 3]
        me, peers = _peers()
        for k, peer, pidx in peers:
            for i in range(n):
                src = ins[i].at[pidx] if scatter else ins[i]
                pltpu.make_async_remote_copy(src_ref=src, dst_ref=lands[i].at[me], send_sem=send.at[i * 7 + k - 1],
                                             recv_sem=recv.at[i * 7 + k - 1], device_id=peer, device_id_type=MESH_ID).start()
        own = [pltpu.make_async_copy(ins[i].at[me] if scatter else ins[i], lands[i].at[me], loc.at[i]) for i in range(n)]
        for cp in own:
            cp.start()
        for cp in own:
            cp.wait()
        token[...] = jnp.zeros_like(token)

    land_shapes = [a.shape if scatter else (N_DEV,) + a.shape for a in arrs]
    hbm = lambda a: pltpu.with_memory_space_constraint(a, pltpu.HBM)
    args = [hbm(a) for a in arrs] + [hbm(lax.empty(s, a.dtype)) for s, a in zip(land_shapes, arrs)]
    out_shape = ([pltpu.SemaphoreType.DMA((7 * n,)), pltpu.SemaphoreType.DMA((7 * n,))]
                 + [pltpu.HBM(a.shape, a.dtype) for a in arrs] + [pltpu.HBM(s, a.dtype) for s, a in zip(land_shapes, arrs)]
                 + [_sds((8, LANE), F32)])
    res = pl.pallas_call(
        body, name=name, out_shape=out_shape, in_specs=[HBM_SPEC] * (2 * n),
        out_specs=[SEM_SPEC, SEM_SPEC] + [HBM_SPEC] * (2 * n) + [pl.BlockSpec(memory_space=pltpu.VMEM)],
        input_output_aliases={i: 2 + i for i in range(2 * n)},
        scratch_shapes=[pltpu.SemaphoreType.DMA((n,))],
        compiler_params=pltpu.CompilerParams(has_side_effects=SIDE_EFFECT))(*args)
    return _InFlight(res[0], res[1], list(res[2:2 + n]), list(res[2 + n:2 + 2 * n]), res[2 + 2 * n], scatter)


def _exchange_wait(h, after, name):
    n = len(h.srcs)
    scatter = h.scatter
    after = list(after)

    def body(*refs):
        ins, lands = refs[:n], refs[n:2 * n]
        send, recv = refs[2 * n], refs[2 * n + 1]
        _, peers = _peers()
        for k, peer, pidx in peers:
            for i in range(n):
                src = ins[i].at[pidx] if scatter else ins[i]
                cp = pltpu.make_async_remote_copy(src_ref=src, dst_ref=lands[i].at[pidx], send_sem=send.at[i * 7 + k - 1],
                                                  recv_sem=recv.at[i * 7 + k - 1], device_id=peer, device_id_type=MESH_ID)
                cp.wait_send()
                cp.wait_recv()

    res = pl.pallas_call(
        body, name=name, out_shape=[pltpu.HBM(a.shape, a.dtype) for a in h.srcs + h.lands],
        in_specs=[HBM_SPEC] * (2 * n) + [SEM_SPEC, SEM_SPEC] + [pl.BlockSpec(memory_space=pl.ANY)] * len(after),
        out_specs=[HBM_SPEC] * (2 * n), input_output_aliases={i: i for i in range(2 * n)},
        compiler_params=pltpu.CompilerParams(has_side_effects=SIDE_EFFECT))(*h.srcs, *h.lands, h.send, h.recv, *after)
    return list(res[n:])


def _mm_xw(x, g, out_dtype, name, tm_c=(768, 512, 384, 256, 128, 64, 16), tn_c=(768, 512, 256, 128), tk_c=(2048, 1024, 768, 512, 256)):
    m, r = x.shape
    nb, r2, cl = g.shape
    assert r == r2
    tm, tn, tk = _pick(m, tm_c), _pick(cl, tn_c), _pick(r, tk_c)
    q, nk = cl // tn, r // tk

    def body(x_ref, g_ref, o_ref, *acc):
        p = lax.dot_general(x_ref[...].astype(BF16), g_ref[...], (((1,), (0,)), ((), ())), preferred_element_type=F32)
        if nk == 1:
            o_ref[...] = p.astype(o_ref.dtype)
        else:
            k = pl.program_id(2)

            @pl.when(k == 0)
            def _():
                acc[0][...] = p

            @pl.when(k > 0)
            def _():
                acc[0][...] += p

            @pl.when(k == nk - 1)
            def _():
                o_ref[...] = acc[0][...].astype(o_ref.dtype)

    return _pcall(
        body, name=name, grid=(m // tm, nb * q, nk),
        in_specs=[pl.BlockSpec((tm, tk), lambda i, j, k: (i, k)), pl.BlockSpec((None, tk, tn), lambda i, j, k: (j // q, k, j % q))],
        out_specs=pl.BlockSpec((tm, tn), lambda i, j, k: (i, j)),
        out_shape=_sds((m, nb * cl), out_dtype),
        scratch=[] if nk == 1 else [pltpu.VMEM((tm, tn), F32)])(x, g)


def _mm_dyw(dy, g, out_dtype, name, init=None, tm_c=(768, 512, 384, 256, 128), tn_c=(1024, 512, 256, 128), tk_c=(768, 512, 256, 128)):
    m, n = dy.shape
    nb, r, cl = g.shape
    assert n == nb * cl
    tm, tn, tk = _pick(m, tm_c), _pick(r, tn_c), _pick(cl, tk_c)
    q = cl // tk
    nk = nb * q
    has_init = init is not None

    def body(*refs):
        if has_init:
            dy_ref, g_ref, i_ref, o_ref, acc = refs
        else:
            dy_ref, g_ref, o_ref, acc = refs
        k = pl.program_id(2)
        p = lax.dot_general(dy_ref[...].astype(BF16), g_ref[...], (((1,), (1,)), ((), ())), preferred_element_type=F32)

        @pl.when(k == 0)
        def _():
            acc[...] = p + i_ref[...].astype(F32) if has_init else p

        @pl.when(k > 0)
        def _():
            acc[...] += p

        @pl.when(k == nk - 1)
        def _():
            o_ref[...] = acc[...].astype(o_ref.dtype)

    in_specs = [pl.BlockSpec((tm, tk), lambda i, j, k: (i, k)), pl.BlockSpec((None, tn, tk), lambda i, j, k: (k // q, j, k % q))]
    args = [dy, g]
    if has_init:
        in_specs.append(pl.BlockSpec((tm, tn), lambda i, j, k: (i, j)))
        args.append(init)
    return _pcall(body, name=name, grid=(m // tm, r // tn, nk), in_specs=in_specs,
                  out_specs=pl.BlockSpec((tm, tn), lambda i, j, k: (i, j)), out_shape=_sds((m, r), out_dtype),
                  scratch=[pltpu.VMEM((tm, tn), F32)])(*args)


def _mm_xtdy(x, dy, nb, out_dtype, name, tm_c=(1024, 512, 256, 128), tn_c=(768, 512, 256, 128), tk_c=(768, 512, 256, 128, 16)):
    t, r = x.shape
    t2, n = dy.shape
    assert t == t2 and n % nb == 0
    cl = n // nb
    tm, tn, tk = _pick(r, tm_c), _pick(cl, tn_c), _pick(t, tk_c)
    q, nk = cl // tn, t // tk

    def body(x_ref, dy_ref, o_ref, *acc):
        p = lax.dot_general(x_ref[...].astype(BF16), dy_ref[...].astype(BF16), (((0,), (0,)), ((), ())), preferred_element_type=F32)
        if nk == 1:
            o_ref[...] = p.astype(o_ref.dtype)
        else:
            k = pl.program_id(2)

            @pl.when(k == 0)
            def _():
                acc[0][...] = p

            @pl.when(k > 0)
            def _():
                acc[0][...] += p

            @pl.when(k == nk - 1)
            def _():
                o_ref[...] = acc[0][...].astype(o_ref.dtype)

    return _pcall(
        body, name=name, grid=(r // tm, nb * q, nk),
        in_specs=[pl.BlockSpec((tk, tm), lambda i, j, k: (k, i)), pl.BlockSpec((tk, tn), lambda i, j, k: (k, j))],
        out_specs=pl.BlockSpec((None, tm, tn), lambda i, j, k: (j // q, i, j % q)),
        out_shape=_sds((nb, r, cl), out_dtype),
        scratch=[] if nk == 1 else [pltpu.VMEM((tm, tn), F32)])(x, dy)


def _mm_f32(a, b, name, trans_b=False):
    dims = (((1,), (1,)), ((), ())) if trans_b else (((1,), (0,)), ((), ()))
    n = b.shape[0] if trans_b else b.shape[1]

    def body(a_ref, b_ref, o_ref):
        o_ref[...] = lax.dot_general(a_ref[...], b_ref[...], dims, precision=HIGHEST, preferred_element_type=F32)

    return _pcall(body, name=name, out_shape=_sds((a.shape[0], n), F32))(a, b)


def _tiled(fn, ins, outs, grid, name, acc=()):
    n_in = len(ins)
    grid = tuple(grid) or (1,)
    nd = len(grid)

    def body(*refs):
        vals = fn(*[r[...] for r in refs[:n_in]])
        if not isinstance(vals, (tuple, list)):
            vals = (vals,)
        first = None
        for o, (ref, v) in enumerate(zip(refs[n_in:], vals)):
            if o in acc:
                if first is None:
                    first = pl.program_id(0) == 0
                    for a in range(1, nd):
                        first = jnp.logical_and(first, pl.program_id(a) == 0)

                @pl.when(first)
                def _(ref=ref, v=v):
                    ref[...] = v.astype(ref.dtype)

                @pl.when(jnp.logical_not(first))
                def _(ref=ref, v=v):
                    ref[...] += v.astype(ref.dtype)
            else:
                ref[...] = v.astype(ref.dtype)

    res = _pcall(body, name=name, grid=grid,
                 in_specs=[pl.BlockSpec(b, im) for _, b, im in ins],
                 out_specs=[pl.BlockSpec(b, im) for _, _, b, im in outs],
                 out_shape=[_sds(s, d) for s, d, _, _ in outs])(*[a for a, _, _ in ins])
    return list(res)


def _rows(a, tr):
    return (a, (tr, a.shape[1]), lambda i, *_: (i, 0))


def _const(a):
    nd = a.ndim
    return (a, a.shape, lambda *_: (0,) * nd)


def _cast_bf16(w, name):
    r, c = w.shape
    tr = _pick(r, (256, 128, 64, 16))
    return _tiled(lambda v: v, [_rows(w, tr)], [((r, c), BF16, (tr, c), lambda i: (i, 0))], (r // tr,), name)[0]


def _f_norm_mod(x, g, sh, sc):
    y = x * lax.rsqrt(jnp.mean(x * x, axis=-1, keepdims=True) + EPS) * g
    return y * (1.0 + sc) + sh


def _f_qk(t, g, cos, sin, perm):
    y = t * lax.rsqrt(jnp.mean(t * t, axis=-1, keepdims=True) + EPS) * g
    return y * cos + jnp.dot(y, perm, precision=HIGHEST, preferred_element_type=F32) * sin


def _f_readout(of, ob, gate, g):
    o = of + ob
    on = o * lax.rsqrt(jnp.mean(o * o, axis=-1, keepdims=True) + EPS) * g
    return on * (gate * jax.nn.sigmoid(gate))


def _f_merge(ga, gb, pa, pb):
    return jax.nn.sigmoid(ga) * pa + jax.nn.sigmoid(gb) * pb


def _f_resid(x, gate, m):
    return x + gate * m


def _norm_mod_fwd(xcat, g, mods, n_ctx_tiles, tr):
    tt, d = xcat.shape
    which = lambda i: (jnp.where(i >= n_ctx_tiles, 1, 0), 0, 0, 0)

    def fn(x, gg, md):
        return _f_norm_mod(x, gg, md[0], md[1])

    return _tiled(fn, [_rows(xcat, tr), _const(g), (mods, (None, 2, 1, d), which)],
                  [((tt, d), BF16, (tr, d), lambda i: (i, 0))], (tt // tr,), "norm_mod_fwd")[0]


def _norm_mod_bwd(xcat, g, mods, dh, extra, n_ctx_tiles, tr, name):
    tt, d = xcat.shape
    nt = tt // tr
    has_extra = extra is not None

    def body(*refs):
        if has_extra:
            x_ref, g_ref, m_ref, dh_ref, e_ref, dx_ref, dg_ref, dm_ref = refs
        else:
            x_ref, g_ref, m_ref, dh_ref, dx_ref, dg_ref, dm_ref = refs
        i = pl.program_id(0)
        md = m_ref[...]
        _, vjp = jax.vjp(_f_norm_mod, x_ref[...], g_ref[...], md[0], md[1])
        dx, dg, dsh, dsc = vjp(dh_ref[...].astype(F32))
        dx_ref[...] = dx + e_ref[...] if has_extra else dx

        @pl.when(i == 0)
        def _():
            dg_ref[...] = dg

        @pl.when(i > 0)
        def _():
            dg_ref[...] += dg

        fresh = jnp.logical_or(i == 0, i == n_ctx_tiles)

        @pl.when(fresh)
        def _():
            dm_ref[0] = dsh
            dm_ref[1] = dsc

        @pl.when(jnp.logical_not(fresh))
        def _():
            dm_ref[0] += dsh
            dm_ref[1] += dsc

    which = lambda i: (jnp.where(i >= n_ctx_tiles, 1, 0), 0, 0, 0)
    row = pl.BlockSpec((tr, d), lambda i: (i, 0))
    in_specs = [row, pl.BlockSpec((1, d), lambda i: (0, 0)), pl.BlockSpec((None, 2, 1, d), which), row]
    args = [xcat, g, mods, dh]
    if has_extra:
        in_specs.append(row)
        args.append(extra)
    return _pcall(body, name=name, grid=(nt,), in_specs=in_specs,
                  out_specs=[row, pl.BlockSpec((1, d), lambda i: (0, 0)), pl.BlockSpec((None, 2, 1, d), which)],
                  out_shape=[_sds((tt, d), F32), _sds((1, d), F32), _sds((2, 2, 1, d), F32)])(*args)


def _hgrn_tri(reverse):
    t = np.arange(HGRN_BLOCK)
    tri = (t[None, :] >= t[:, None]) if reverse else (t[None, :] <= t[:, None])
    tri = tri.astype(np.float32)
    return jnp.asarray(tri), jnp.asarray(tri.T.copy())


def _hgrn_rowblock(n, n_ctx_blocks, n_blocks, reverse):
    if not reverse:
        return n
    return jnp.where(n < n_ctx_blocks, n_ctx_blocks - 1 - n, n_blocks - 1 - n + n_ctx_blocks)


def _hgrn_gates(fl, lb):
    sg = jax.nn.sigmoid(fl)
    f = lb + (1.0 - lb) * sg
    return sg, f, jnp.log(f), 1.0 - f


def _hgrn_intra_mask(reverse):
    tio = lax.broadcasted_iota(jnp.int32, (HGRN_BLOCK, HEAD_DIM), 0)
    return (lambda s: tio <= s) if reverse else (lambda s: tio >= s)


def _hgrn_fwd(p, lb, seg_f, reverse, n_ctx_rows):
    tt = p.shape[0]
    hb = HGRN_BLOCK
    nblk, nctx = tt // hb, n_ctx_rows // hb
    tri, _ = _hgrn_tri(reverse)
    valid = None

    def body(q_ref, f_ref, v_ref, lb_ref, tri_ref, o_ref, st_all, dec_all, qe_buf, cum_blk, k_blk, v_blk):
        mask = _hgrn_intra_mask(reverse)
        lbv = lb_ref[...]

        def phase1(n, c):
            r0 = pl.multiple_of(_hgrn_rowblock(n, nctx, nblk, reverse) * hb, hb)
            q, v = q_ref[pl.ds(r0, hb), :], v_ref[pl.ds(r0, hb), :]
            _, f, g, k = _hgrn_gates(f_ref[pl.ds(r0, hb), :], lbv)
            cum = jnp.dot(tri_ref[...], g, precision=HIGHEST, preferred_element_type=F32)
            tot = jnp.sum(g, axis=0, keepdims=True)
            cum_blk[...] = cum
            k_blk[...] = k
            v_blk[...] = v
            oi = jnp.zeros((hb, HEAD_DIM), F32)
            for s in range(hb):
                e = jnp.where(mask(s), jnp.exp(jnp.minimum(cum - cum_blk[s:s + 1, :], 0.0)), 0.0)
                a_s = jnp.sum(q * e * k_blk[s:s + 1, :], axis=-1, keepdims=True)
                oi = oi + a_s * v_blk[s:s + 1, :]
            o_ref[pl.ds(r0, hb), :] = oi
            qe_buf[pl.ds(r0, hb), :] = q * jnp.exp(cum)
            kl = k * jnp.exp(tot - cum)
            st_all[n] = lax.dot_general(v.astype(BF16), kl.astype(BF16), (((0,), (0,)), ((), ())), preferred_element_type=F32)
            dec_all[pl.ds(n, 1), :] = jnp.exp(tot)
            return c

        lax.fori_loop(0, nblk, phase1, 0)

        def phase2(n, st):
            kv = st_all[n]
            st_all[n] = st
            return st * dec_all[pl.ds(n, 1), :] + kv

        lax.fori_loop(0, nblk, phase2, jnp.zeros((HEAD_DIM, HEAD_DIM), F32))

        def phase3(n, c):
            r0 = pl.multiple_of(_hgrn_rowblock(n, nctx, nblk, reverse) * hb, hb)
            o_ref[pl.ds(r0, hb), :] += lax.dot_general(qe_buf[pl.ds(r0, hb), :].astype(BF16), st_all[n].astype(BF16),
                                                        (((1,), (1,)), ((), ())), preferred_element_type=F32)
            return c

        lax.fori_loop(0, nblk, phase3, 0)

    col = lambda seg: pl.BlockSpec((tt, HEAD_DIM), lambda h, seg=seg: (0, seg * N_HEADS + h))
    return _pcall(
        body, name="hgrn_fwd_rev" if reverse else "hgrn_fwd", grid=(N_HEADS,),
        in_specs=[col(0), col(seg_f), col(3), pl.BlockSpec((1, HEAD_DIM), lambda h: (0, h)), pl.BlockSpec((hb, hb), lambda h: (0, 0))],
        out_specs=pl.BlockSpec((tt, HEAD_DIM), lambda h: (0, h)),
        out_shape=_sds((tt, N_HEADS * HEAD_DIM), F32),
        scratch=[pltpu.VMEM((nblk, HEAD_DIM, HEAD_DIM), F32), pltpu.VMEM((nblk, HEAD_DIM), F32), pltpu.VMEM((tt, HEAD_DIM), F32),
                 pltpu.VMEM((hb, HEAD_DIM), F32), pltpu.VMEM((hb, HEAD_DIM), F32), pltpu.VMEM((hb, HEAD_DIM), F32)])(p, p, p, lb, tri)


def _hgrn_bwd(p, lb, do, seg_f, reverse, n_ctx_rows, prev):
    tt = p.shape[0]
    hb = HGRN_BLOCK
    nblk, nctx = tt // hb, n_ctx_rows // hb
    tri, tri_t = _hgrn_tri(reverse)
    last_row = 0 if reverse else hb - 1
    has_prev = prev is not None

    def body(*refs):
        q_ref, f_ref, v_ref, lb_ref, tri_ref, trit_ref, do_ref = refs[:7]
        refs = refs[7:]
        if has_prev:
            pq_ref, pv_ref = refs[:2]
            refs = refs[2:]
        dq_ref, dfl_ref, dv_ref, dlb_ref, st_all, dd_all, dec_all, cum_buf, cum_blk, k_blk, v_blk, dk_blk, dv_blk = refs
        mask = _hgrn_intra_mask(reverse)
        lbv = lb_ref[...]
        tio = lax.broadcasted_iota(jnp.int32, (hb, HEAD_DIM), 0)

        def rows_of(n):
            rb = _hgrn_rowblock(n, nctx, nblk, reverse)
            return rb, pl.multiple_of(rb * hb, hb)

        def load_do(rb):
            lat0 = pl.multiple_of(jnp.maximum(rb - nctx, 0) * hb, hb)
            return jnp.where(rb >= nctx, do_ref[pl.ds(lat0, hb), :], 0.0)

        def phase1(n, c):
            rb, r0 = rows_of(n)
            q, v = q_ref[pl.ds(r0, hb), :], v_ref[pl.ds(r0, hb), :]
            _, f, g, k = _hgrn_gates(f_ref[pl.ds(r0, hb), :], lbv)
            cum = jnp.dot(tri_ref[...], g, precision=HIGHEST, preferred_element_type=F32)
            tot = jnp.sum(g, axis=0, keepdims=True)
            cum_buf[pl.ds(r0, hb), :] = cum
            kl = k * jnp.exp(tot - cum)
            st_all[n] = lax.dot_general(v.astype(BF16), kl.astype(BF16), (((0,), (0,)), ((), ())), preferred_element_type=F32)
            dec_all[pl.ds(n, 1), :] = jnp.exp(tot)
            qe = q * jnp.exp(cum)
            dd_all[n] = lax.dot_general(load_do(rb).astype(BF16), qe.astype(BF16), (((0,), (0,)), ((), ())), preferred_element_type=F32)
            return c

        lax.fori_loop(0, nblk, phase1, 0)

        def phase2(n, st):
            kv = st_all[n]
            st_all[n] = st
            return st * dec_all[pl.ds(n, 1), :] + kv

        lax.fori_loop(0, nblk, phase2, jnp.zeros((HEAD_DIM, HEAD_DIM), F32))

        def phase2r(i, dst):
            n = nblk - 1 - i
            u = dd_all[n]
            dd_all[n] = dst
            return u + dst * dec_all[pl.ds(n, 1), :]

        lax.fori_loop(0, nblk, phase2r, jnp.zeros((HEAD_DIM, HEAD_DIM), F32))

        def phase3(n, dlb):
            rb, r0 = rows_of(n)
            q, v = q_ref[pl.ds(r0, hb), :], v_ref[pl.ds(r0, hb), :]
            sg, f, g, k = _hgrn_gates(f_ref[pl.ds(r0, hb), :], lbv)
            cum = cum_buf[pl.ds(r0, hb), :]
            tot = jnp.sum(g, axis=0, keepdims=True)
            dob = load_do(rb)
            st, dst = st_all[n], dd_all[n]
            e_cum = jnp.exp(cum)
            e_rest = jnp.exp(tot - cum)
            dq = jnp.dot(dob.astype(BF16), st.astype(BF16), preferred_element_type=F32) * e_cum
            dk_inter = jnp.dot(v.astype(BF16), dst.astype(BF16), preferred_element_type=F32) * e_rest
            dv = lax.dot_general((k * e_rest).astype(BF16), dst.astype(BF16), (((1,), (1,)), ((), ())), preferred_element_type=F32)
            cum_blk[...] = cum
            k_blk[...] = k
            v_blk[...] = v
            for s in range(hb):
                e = jnp.where(mask(s), jnp.exp(jnp.minimum(cum - cum_blk[s:s + 1, :], 0.0)), 0.0)
                a_s = jnp.sum(q * e * k_blk[s:s + 1, :], axis=-1, keepdims=True)
                da_s = jnp.sum(dob * v_blk[s:s + 1, :], axis=-1, keepdims=True)
                gs = da_s * e
                dq = dq + gs * k_blk[s:s + 1, :]
                dk_blk[s:s + 1, :] = jnp.sum(gs * q, axis=0, keepdims=True)
                dv_blk[s:s + 1, :] = jnp.sum(a_s * dob, axis=0, keepdims=True)
            dk = dk_inter + dk_blk[...]
            dv = dv + dv_blk[...]
            d_tot = jnp.sum(k * dk_inter, axis=0, keepdims=True) + jnp.exp(tot) * jnp.sum(dst * st, axis=0, keepdims=True)
            dcum = q * dq - k * dk + jnp.where(tio == last_row, d_tot, 0.0)
            dg = jnp.dot(trit_ref[...], dcum, precision=HIGHEST, preferred_element_type=F32)
            df = dg / f - dk
            if has_prev:
                dq = dq + pq_ref[pl.ds(r0, hb), :].astype(F32)
                dv = dv + pv_ref[pl.ds(r0, hb), :].astype(F32)
            dq_ref[pl.ds(r0, hb), :] = dq.astype(dq_ref.dtype)
            dv_ref[pl.ds(r0, hb), :] = dv.astype(dv_ref.dtype)
            dfl_ref[pl.ds(r0, hb), :] = (df * (1.0 - lbv) * sg * (1.0 - sg)).astype(dfl_ref.dtype)
            return dlb + jnp.sum(df * (1.0 - sg), axis=0, keepdims=True)

        dlb_ref[...] = lax.fori_loop(0, nblk, phase3, jnp.zeros((1, HEAD_DIM), F32))

    col = lambda seg: pl.BlockSpec((tt, HEAD_DIM), lambda h, seg=seg: (0, seg * N_HEADS + h))
    head = pl.BlockSpec((tt, HEAD_DIM), lambda h: (0, h))
    lbs = pl.BlockSpec((1, HEAD_DIM), lambda h: (0, h))
    tris = pl.BlockSpec((hb, hb), lambda h: (0, 0))
    in_specs = [col(0), col(seg_f), col(3), lbs, tris, tris, pl.BlockSpec((do.shape[0], HEAD_DIM), lambda h: (0, h))]
    args = [p, p, p, lb, tri, tri_t, do]
    mid = F32 if not has_prev else BF16
    if has_prev:
        in_specs += [head, head]
        args += list(prev)
    w = N_HEADS * HEAD_DIM
    blk = pltpu.VMEM((hb, HEAD_DIM), F32)
    return _pcall(
        body, name="hgrn_bwd_rev" if reverse else "hgrn_bwd", grid=(N_HEADS,), in_specs=in_specs,
        out_specs=[head, head, head, lbs],
        out_shape=[_sds((tt, w), mid), _sds((tt, w), BF16), _sds((tt, w), mid), _sds((1, w), F32)],
        scratch=[pltpu.VMEM((nblk, HEAD_DIM, HEAD_DIM), F32), pltpu.VMEM((nblk, HEAD_DIM, HEAD_DIM), F32), pltpu.VMEM((nblk, HEAD_DIM), F32),
                 pltpu.VMEM((tt, HEAD_DIM), F32), blk, blk, blk, blk, blk])(*args)


def _na_geometry(rows):
    r = pl.program_id(1)
    rs = jnp.clip(r - WIN_R // 2, 0, rows - WIN_R)
    return r, rs, r - rs


def _na_scores(q, kb, kc, bias):
    scale = HEAD_DIM ** -0.5
    nt = (((1,), (1,)), ((), ()))
    sb = lax.dot_general(q, kb, nt, preferred_element_type=F32) * scale + bias
    sc = lax.dot_general(q, kc, nt, preferred_element_type=F32) * scale
    m = jnp.maximum(jnp.max(sb, axis=-1, keepdims=True), jnp.max(sc, axis=-1, keepdims=True))
    pb, pc = jnp.exp(sb - m), jnp.exp(sc - m)
    inv = 1.0 / (jnp.sum(pb, axis=-1, keepdims=True) + jnp.sum(pc, axis=-1, keepdims=True))
    return pb * inv, pc * inv


def _na_fwd(qn, kall, p, seg_v, bias, n_ctx_rows):
    t, tt = qn.shape[0], kall.shape[0]
    rows = t // GRID_W
    nband = WIN_R * GRID_W

    def body(q_ref, k_ref, v_ref, b_ref, o_ref):
        r, rs, _ = _na_geometry(rows)
        k0 = pl.multiple_of(n_ctx_rows + rs * GRID_W, GRID_W)
        q = q_ref[...]
        kb, kc = k_ref[pl.ds(k0, nband), :], k_ref[pl.ds(0, n_ctx_rows), :]
        vb, vc = v_ref[pl.ds(k0, nband), :].astype(BF16), v_ref[pl.ds(0, n_ctx_rows), :].astype(BF16)
        pb, pc = _na_scores(q, kb, kc, b_ref[...])
        o = jnp.dot(pb.astype(BF16), vb, preferred_element_type=F32) + jnp.dot(pc.astype(BF16), vc, preferred_element_type=F32)
        o_ref[...] = o.astype(o_ref.dtype)

    return _pcall(
        body, name="na_fwd", grid=(N_HEADS, rows),
        in_specs=[pl.BlockSpec((GRID_W, HEAD_DIM), lambda h, r: (r, h)),
                  pl.BlockSpec((tt, HEAD_DIM), lambda h, r: (0, h)),
                  pl.BlockSpec((tt, HEAD_DIM), lambda h, r: (0, seg_v * N_HEADS + h)),
                  pl.BlockSpec((None, None, GRID_W, nband), lambda h, r: (h, r - jnp.clip(r - WIN_R // 2, 0, rows - WIN_R), 0, 0))],
        out_specs=pl.BlockSpec((GRID_W, HEAD_DIM), lambda h, r: (r, h)),
        out_shape=_sds((t, N_HEADS * HEAD_DIM), BF16))(qn, kall, p, bias)


def _na_bwd(qn, kall, p, seg_v, bias, do, n_ctx_rows):
    t, tt = qn.shape[0], kall.shape[0]
    rows = t // GRID_W
    nband = WIN_R * GRID_W
    scale = HEAD_DIM ** -0.5
    tn = (((0,), (0,)), ((), ()))
    nt = (((1,), (1,)), ((), ()))

    def body(q_ref, k_ref, v_ref, b_ref, do_ref, dq_ref, dk_ref, dv_ref, db_ref, dv_acc):
        r, rs, var = _na_geometry(rows)
        k0 = pl.multiple_of(n_ctx_rows + rs * GRID_W, GRID_W)
        q = q_ref[...]
        kb, kc = k_ref[pl.ds(k0, nband), :], k_ref[pl.ds(0, n_ctx_rows), :]
        vb, vc = v_ref[pl.ds(k0, nband), :].astype(BF16), v_ref[pl.ds(0, n_ctx_rows), :].astype(BF16)
        pb, pc = _na_scores(q, kb, kc, b_ref[...])
        dob = do_ref[...].astype(BF16)
        o = jnp.dot(pb.astype(BF16), vb, preferred_element_type=F32) + jnp.dot(pc.astype(BF16), vc, preferred_element_type=F32)
        delta = jnp.sum(do_ref[...].astype(F32) * o, axis=-1, keepdims=True)
        dsb = pb * (lax.dot_general(dob, vb, nt, preferred_element_type=F32) - delta)
        dsc = pc * (lax.dot_general(dob, vc, nt, preferred_element_type=F32) - delta)
        dsb16, dsc16 = dsb.astype(BF16), dsc.astype(BF16)
        dq_ref[...] = (jnp.dot(dsb16, kb, preferred_element_type=F32) + jnp.dot(dsc16, kc, preferred_element_type=F32)) * scale

        @pl.when(r == 0)
        def _():
            dk_ref[...] = jnp.zeros_like(dk_ref)
            dv_acc[...] = jnp.zeros_like(dv_acc)

        dk_ref[pl.ds(k0, nband), :] += lax.dot_general(dsb16, q, tn, preferred_element_type=F32) * scale
        dk_ref[pl.ds(0, n_ctx_rows), :] += lax.dot_general(dsc16, q, tn, preferred_element_type=F32) * scale
        dv_acc[pl.ds(k0, nband), :] += lax.dot_general(pb.astype(BF16), dob, tn, preferred_element_type=F32)
        dv_acc[pl.ds(0, n_ctx_rows), :] += lax.dot_general(pc.astype(BF16), dob, tn, preferred_element_type=F32)

        @pl.when(r == rows - 1)
        def _():
            dv_ref[...] = dv_acc[...].astype(dv_ref.dtype)

        fresh = jnp.logical_or(r <= WIN_R // 2, r > rows - WIN_R // 2)

        @pl.when(fresh)
        def _():
            db_ref[...] = dsb

        @pl.when(jnp.logical_not(fresh))
        def _():
            db_ref[...] += dsb

    variant = lambda h, r: (h, r - jnp.clip(r - WIN_R // 2, 0, rows - WIN_R), 0, 0)
    head_all = pl.BlockSpec((tt, HEAD_DIM), lambda h, r: (0, h))
    qspec = pl.BlockSpec((GRID_W, HEAD_DIM), lambda h, r: (r, h))
    w = N_HEADS * HEAD_DIM
    return _pcall(
        body, name="na_bwd", grid=(N_HEADS, rows),
        in_specs=[qspec, head_all, pl.BlockSpec((tt, HEAD_DIM), lambda h, r: (0, seg_v * N_HEADS + h)),
                  pl.BlockSpec((None, None, GRID_W, nband), variant), qspec],
        out_specs=[qspec, head_all, head_all, pl.BlockSpec((None, None, GRID_W, nband), variant)],
        out_shape=[_sds((t, w), F32), _sds((tt, w), F32), _sds((tt, w), BF16), _sds((N_HEADS, WIN_R, GRID_W, nband), F32)],
        scratch=[pltpu.VMEM((tt, HEAD_DIM), F32)])(qn, kall, p, bias, do)


def _na_tables(t, n_ctx_rows):
    half, nf = HEAD_DIM // 2, HEAD_DIM // 4
    pos = np.arange(t)
    lane = np.arange(HEAD_DIM)
    inv = ROPE_THETA ** (-(np.arange(nf, dtype=np.float32)) / nf)
    which = np.where(lane < half, pos[:, None] // GRID_W, pos[:, None] % GRID_W).astype(np.float32)
    ang = which * inv[lane % nf][None, :]
    first = (lane % half) < nf
    cos = np.concatenate([np.ones((n_ctx_rows, HEAD_DIM), np.float32), np.cos(ang).astype(np.float32)])
    sin = np.concatenate([np.zeros((n_ctx_rows, HEAD_DIM), np.float32), np.where(first[None, :], -np.sin(ang), np.sin(ang)).astype(np.float32)])
    partner = np.where(first, lane + nf, lane - nf)
    perm = np.zeros((HEAD_DIM, HEAD_DIM), np.float32)
    perm[partner, lane] = 1.0
    w = np.arange(GRID_W)
    dc = np.clip(w[None, :] - w[:, None], -(WIN_C - 1), WIN_C - 1) + WIN_C - 1
    onehot = np.zeros((32, GRID_W * GRID_W), np.float32)
    onehot[dc.reshape(-1), np.arange(GRID_W * GRID_W)] = 1.0
    cs = np.clip(w - WIN_C // 2, 0, GRID_W - WIN_C)
    col_in = (w[None, :] >= cs[:, None]) & (w[None, :] < cs[:, None] + WIN_C)
    onehot *= col_in.reshape(1, -1)
    neg = np.where(col_in, 0.0, NEG_BIG).astype(np.float32)
    return jnp.asarray(cos), jnp.asarray(sin), jnp.asarray(perm), jnp.asarray(onehot), jnp.asarray(neg)


def _bias_slabs(rel_bias, onehot, neg):
    nr = 2 * WIN_R - 1
    rb = jnp.pad(rel_bias.reshape(N_HEADS * nr, 2 * WIN_C - 1), ((0, 0), (0, 1)))
    spread = _mm_f32(rb, onehot, "bias_spread").reshape(N_HEADS, nr, GRID_W, GRID_W)
    slabs = [spread[:, WIN_R - 1 - v:2 * WIN_R - 1 - v] for v in range(WIN_R)]
    b = jnp.stack(slabs, axis=1) + neg[None, None, None]
    return b.transpose(0, 1, 3, 2, 4).reshape(N_HEADS, WIN_R, GRID_W, WIN_R * GRID_W)


def _bias_grad(dbias, onehot):
    nr = 2 * WIN_R - 1
    d = dbias.reshape(N_HEADS, WIN_R, GRID_W, WIN_R, GRID_W).transpose(0, 1, 3, 2, 4)
    tot = jnp.zeros((N_HEADS, nr, GRID_W, GRID_W), F32)
    for v in range(WIN_R):
        tot = tot + jnp.pad(d[:, v], ((0, 0), (WIN_R - 1 - v, v), (0, 0), (0, 0)))
    g = _mm_f32(tot.reshape(N_HEADS * nr, GRID_W * GRID_W), onehot, "bias_grad", trans_b=True)
    return g[:, :2 * WIN_C - 1].reshape(1, N_HEADS, nr, 2 * WIN_C - 1)


def _shift_rows(u, up):
    n = u.shape[0]
    tio = lax.broadcasted_iota(jnp.int32, u.shape, 0)
    if up:
        return jnp.where(tio == n - 1, 0.0, pltpu.roll(u, n - 1, 0))
    return jnp.where(tio == 0, 0.0, pltpu.roll(u, 1, 0))


def _conv3(u, w_ref, b_ref):
    um, up = _shift_rows(u, False), _shift_rows(u, True)
    return um, up, um * w_ref[0:1, :] + u * w_ref[1:2, :] + up * w_ref[2:3, :] + b_ref[...]


def _ffn_act_fwd(u0, t3, cw, cb):
    t, n = u0.shape
    tc = _pick(n, (256, 128))

    def body(u_ref, t_ref, w_ref, b_ref, a_ref):
        _, _, uc = _conv3(u_ref[...], w_ref, b_ref)
        a_ref[...] = (uc * jax.nn.sigmoid(uc) * t_ref[...]).astype(a_ref.dtype)

    col = lambda rows_: pl.BlockSpec((rows_, tc), lambda j: (0, j))
    return _pcall(body, name="ffn_act_fwd", grid=(n // tc,), in_specs=[col(t), col(t), col(8), col(1)], out_specs=col(t),
                  out_shape=_sds((t, n), BF16))(u0, t3, cw, cb)


def _ffn_act_bwd(u0, t3, cw, cb, da):
    t, n = u0.shape
    tc = _pick(n, (256, 128))

    def body(u_ref, t_ref, w_ref, b_ref, da_ref, du_ref, dt_ref, dw_ref, db_ref):
        u = u_ref[...]
        um, up, uc = _conv3(u, w_ref, b_ref)
        sg = jax.nn.sigmoid(uc)
        dav = da_ref[...].astype(F32)
        dt_ref[...] = (dav * uc * sg).astype(dt_ref.dtype)
        duc = dav * t_ref[...] * sg * (1.0 + uc * (1.0 - sg))
        du = _shift_rows(duc, True) * w_ref[0:1, :] + duc * w_ref[1:2, :] + _shift_rows(duc, False) * w_ref[2:3, :]
        du_ref[...] = du.astype(du_ref.dtype)
        dw_ref[...] = jnp.zeros_like(dw_ref)
        dw_ref[0:1, :] = jnp.sum(duc * um, axis=0, keepdims=True)
        dw_ref[1:2, :] = jnp.sum(duc * u, axis=0, keepdims=True)
        dw_ref[2:3, :] = jnp.sum(duc * up, axis=0, keepdims=True)
        db_ref[...] = jnp.sum(duc, axis=0, keepdims=True)

    col = lambda rows_: pl.BlockSpec((rows_, tc), lambda j: (0, j))
    return _pcall(body, name="ffn_act_bwd", grid=(n // tc,), in_specs=[col(t), col(t), col(8), col(1), col(t)],
                  out_specs=[col(t), col(t), col(8), col(1)],
                  out_shape=[_sds((t, n), BF16), _sds((t, n), BF16), _sds((8, n), F32), _sds((1, n), F32)])(u0, t3, cw, cb, da)


def _adam_math(g, w, m, v):
    m2 = ADAM_B1 * m + (1.0 - ADAM_B1) * g
    v2 = ADAM_B2 * v + (1.0 - ADAM_B2) * (g * g)
    m_hat = m2 / (1.0 - ADAM_B1 ** ADAM_STEP)
    v_hat = v2 / (1.0 - ADAM_B2 ** ADAM_STEP)
    return -ADAM_LR * (m_hat / (jnp.sqrt(v_hat) + ADAM_EPS) + ADAM_WD * w), m2, v2


def _adam_big(parts, w, m, v, name):
    r, c = w.shape
    npart, _, cp = parts.shape
    tr = _pick(r, (64, 32, 16, 8))

    def body(p_ref, w_ref, m_ref, v_ref, g_ref, d_ref, m2_ref, v2_ref):
        g = p_ref[0, :, 0:c].astype(F32)
        for i in range(1, npart):
            g = g + p_ref[i, :, 0:c].astype(F32)
        d, m2, v2 = _adam_math(g, w_ref[...], m_ref[...], v_ref[...])
        g_ref[...] = g
        d_ref[...] = d
        m2_ref[...] = m2
        v2_ref[...] = v2

    row = pl.BlockSpec((tr, c), lambda i: (i, 0))
    return _pcall(body, name=name, grid=(r // tr,),
                  in_specs=[pl.BlockSpec((npart, tr, cp), lambda i: (0, i, 0)), row, row, row],
                  out_specs=[row] * 4, out_shape=[_sds((r, c), F32)] * 4)(parts, w, m, v)


def _adam_small(g, w, m, v):
    def body(g_ref, w_ref, m_ref, v_ref, d_ref, m2_ref, v2_ref):
        d_ref[...], m2_ref[...], v2_ref[...] = _adam_math(g_ref[...], w_ref[...], m_ref[...], v_ref[...])

    return _pcall(body, name="adam_small", out_shape=[_sds(g.shape, F32)] * 3)(g, w, m, v)


def _sum_parts(parts, name):
    def body(p_ref, o_ref):
        s = p_ref[0]
        for i in range(1, N_DEV):
            s = s + p_ref[i]
        o_ref[...] = s

    return _pcall(body, name=name, out_shape=_sds(parts.shape[1:], F32))(parts)


class _Pack:
    def __init__(self, shapes):
        self.shapes = shapes
        self.sizes = [int(np.prod(s)) for s in shapes]
        self.padded = [-(-n // (8 * LANE)) * 8 * LANE for n in self.sizes]
        self.offs = np.concatenate([[0], np.cumsum(self.padded)]).tolist()

    def pack(self, arrs):
        flat = [jnp.pad(a.reshape(-1).astype(F32), (0, p - n)) for a, n, p in zip(arrs, self.sizes, self.padded)]
        return jnp.concatenate(flat).reshape(-1, LANE)

    def unpack(self, slab):
        flat = slab.reshape(-1)
        return [flat[o:o + n].reshape(s) for o, n, s in zip(self.offs, self.sizes, self.shapes)]


def kernel(x, c, ctx, c_ctx, ada_w, ada_b, norm1_g, norm2_g, w_in, hgrn_lb_logits, hgrn_norm_g, na_q_norm_g, na_k_norm_g, na_rel_bias, w_branch_a, w_branch_b, w_out, ffn_w1, ffn_w3, ffn_conv_w, ffn_conv_b, ffn_w2, loss_target, m_c_ctx, m_ada_w, m_ada_b, m_norm1_g, m_norm2_g, m_w_in, m_hgrn_lb_logits, m_hgrn_norm_g, m_na_q_norm_g, m_na_k_norm_g, m_na_rel_bias, m_w_branch_a, m_w_branch_b, m_w_out, m_ffn_w1, m_ffn_w3, m_ffn_conv_w, m_ffn_conv_b, m_ffn_w2, v_c_ctx, v_ada_w, v_ada_b, v_norm1_g, v_norm2_g, v_w_in, v_hgrn_lb_logits, v_hgrn_norm_g, v_na_q_norm_g, v_na_k_norm_g, v_na_rel_bias, v_w_branch_a, v_w_branch_b, v_w_out, v_ffn_w1, v_ffn_w3, v_ffn_conv_w, v_ffn_conv_b, v_ffn_w2):
    t, d = x.shape[1], x.shape[2]
    n_ctx = ctx.shape[1]
    tt = n_ctx + t
    hw = N_HEADS * HEAD_DIM
    ci = w_in.shape[2]
    ca = ada_w.shape[2]
    ff_l = ffn_w1.shape[2]
    ff_p = -(-ff_l // LANE) * LANE
    rows = t // GRID_W
    assert rows >= WIN_R and t % GRID_W == 0 and n_ctx % GRID_W == 0 and ci % LANE == 0 and d % LANE == 0
    me = 4 * lax.axis_index("x") + 2 * lax.axis_index("y") + lax.axis_index("c")
    tr = _pick(n_ctx, (256, 128, 64))
    n_ctx_tiles = n_ctx // tr

    pad_c = lambda w: jnp.pad(w, ((0, 0), (0, ff_p - ff_l)))
    fly_in = _exchange_start([_cast_bf16(w_in[0], "cast_w_in")], "gather_w_in_start", scatter=False)
    rest16 = []
    for w_, nm in ((w_branch_a[0], "cast_w_a"), (w_branch_b[0], "cast_w_b"), (w_out[0], "cast_w_out"), (pad_c(ffn_w1[0]), "cast_w1"),
                   (pad_c(ffn_w3[0]), "cast_w3"), (jnp.pad(ffn_w2[0], ((0, ff_p - ff_l), (0, 0))), "cast_w2")):
        with _after(fly_in.token):
            rest16.append(_cast_bf16(w_, nm))
    fly_rest = _exchange_start(rest16, "gather_rest_start", scatter=False)
    small_in = [c, hgrn_lb_logits.reshape(4, HEAD_DIM), jnp.pad(ffn_conv_w[0], ((0, 5), (0, ff_p - ff_l)))]
    c_all, lb_parts, cw_all = _exchange(small_in, "gather_params", scatter=False)
    ff = N_DEV * ff_p

    cc = jnp.concatenate([c_all.reshape(N_DEV, d), jnp.broadcast_to(c_ctx[None, :], (N_DEV, d))], axis=0)
    act = _tiled(lambda v: v * jax.nn.sigmoid(v), [_const(cc)], [(cc.shape, BF16, cc.shape, lambda *_: (0, 0))], (), "silu_c")[0]
    ada16 = _cast_bf16(ada_w[0], "cast_ada")
    mod_cols = _mm_xw(act, ada16.reshape(1, d, ca), F32, "ada_fwd")
    mod_all = _exchange([mod_cols], "gather_mod", scatter=False)[0]
    mod_all = mod_all.transpose(1, 0, 2).reshape(2 * N_DEV, N_MOD * d) + ada_b
    mod_l = lax.dynamic_slice_in_dim(mod_all, me, 1, axis=0).reshape(N_MOD, 1, d)
    mod_c = mod_all[N_DEV:N_DEV + 1].reshape(N_MOD, 1, d)
    mods1 = jnp.stack([mod_c[0:2], mod_l[0:2]])
    mods2 = jnp.stack([mod_l[3:5], mod_l[3:5]])
    gate1, gate2 = mod_l[2], mod_l[5]

    xcat = jnp.concatenate([ctx[0], x[0]], axis=0)
    with _after(fly_rest.token):
        hcat = _norm_mod_fwd(xcat, norm1_g, mods1, n_ctx_tiles, tr)
    g_in, = _exchange_wait(fly_in, [hcat], "gather_w_in_wait")
    p = _mm_xw(hcat, g_in, F32, "in_proj")
    lb_logits = lb_parts.transpose(1, 0, 2).reshape(2, 2, hw)
    lb_soft = _tiled(lambda a, b: (1.0 / (1.0 + jnp.exp(b - a)),), [_const(lb_logits[:, 0]), _const(lb_logits[:, 1])],
                     [((2, hw), F32, (2, hw), lambda *_: (0, 0))], (), "lb_softmax")[0]
    lb_f, lb_b = lb_soft[0:1], lb_soft[1:2]
    o_f = _hgrn_fwd(p, lb_f, 1, False, n_ctx)
    o_b = _hgrn_fwd(p, lb_b, 2, True, n_ctx)

    cos, sin, perm, onehot, neg = _na_tables(t, n_ctx)
    bias = _bias_slabs(na_rel_bias[0], onehot, neg)
    hcol = lambda seg, off=0: (lambda i, h: (i + off, seg * N_HEADS + h))
    tq = GRID_W * 2
    lat0 = n_ctx // tq
    qk_fn = lambda tv, g, cs, sn, pm: (_f_qk(tv, g, cs, sn, pm),)
    tab = lambda a, off=0: (a, (tq, HEAD_DIM), lambda i, h: (i + off, 0))
    qn = _tiled(qk_fn, [(p, (tq, HEAD_DIM), hcol(5, lat0)), _const(na_q_norm_g), tab(cos, lat0), tab(sin, lat0), _const(perm)],
                [((t, hw), BF16, (tq, HEAD_DIM), lambda i, h: (i, h))], (t // tq, N_HEADS), "q_norm_rope")[0]
    kall = _tiled(qk_fn, [(p, (tq, HEAD_DIM), hcol(6)), _const(na_k_norm_g), tab(cos), tab(sin), _const(perm)],
                  [((tt, hw), BF16, (tq, HEAD_DIM), lambda i, h: (i, h))], (tt // tq, N_HEADS), "k_norm_rope")[0]
    y_b = _na_fwd(qn, kall, p, 7, bias, n_ctx)

    trh = tr
    lat_h = n_ctx // trh
    ospec = lambda a: (a, (trh, HEAD_DIM), lambda i, h: (i + lat_h, h))
    y_a = _tiled(lambda a, b, gt, g: (_f_readout(a, b, gt, g),),
                 [ospec(o_f), ospec(o_b), (p, (trh, HEAD_DIM), hcol(4, lat_h)), _const(hgrn_norm_g)],
                 [((t, hw), BF16, (trh, HEAD_DIM), lambda i, h: (i, h))], (t // trh, N_HEADS), "hgrn_readout")[0]

    g_a, g_b, g_out, g_w1, g_w3, g_w2 = _exchange_wait(fly_rest, [y_a, y_b], "gather_rest_wait")
    g_out = g_out.reshape(1, d, d)
    g_w2 = g_w2.reshape(1, ff, d)
    p_a = _mm_xw(y_a, g_a, F32, "branch_a")
    p_b = _mm_xw(y_b, g_b, F32, "branch_b")
    td = _pick(d, (512, 256, 128))
    nd_t = d // td
    lat_r = n_ctx // tr
    gcol = lambda k: (p, (tr, td), lambda i, j, k=k: (i + lat_r, 8 * hw // td + k * nd_t + j))
    dtile = lambda a: (a, (tr, td), lambda i, j: (i, j))
    z = _tiled(lambda ga, gb, pa, pb: (_f_merge(ga, gb, pa, pb),), [gcol(0), gcol(1), dtile(p_a), dtile(p_b)],
               [((t, d), BF16, (tr, td), lambda i, j: (i, j))], (t // tr, nd_t), "merge")[0]
    mix = _mm_xw(z, g_out, F32, "out_proj")
    xl = x[0]
    x_mid = _tiled(lambda a, g, m_: (_f_resid(a, g, m_),), [_rows(xl, tr), _const(gate1), _rows(mix, tr)],
                   [((t, d), F32, (tr, d), lambda i: (i, 0))], (t // tr,), "resid1")[0]

    h2 = _norm_mod_fwd(x_mid, norm2_g, mods2, 0, tr)
    u0 = _mm_xw(h2, g_w1, F32, "ffn_up1")
    t3 = _mm_xw(h2, g_w3, F32, "ffn_up3")
    cw_full = cw_all.transpose(1, 0, 2).reshape(8, ff)
    cb_full = jnp.pad(ffn_conv_b.reshape(N_DEV, ff_l), ((0, 0), (0, ff_p - ff_l))).reshape(1, ff)
    a_act = _ffn_act_fwd(u0, t3, cw_full, cb_full)
    f_out = _mm_xw(a_act, g_w2, F32, "ffn_down")

    def loss_fn(xm, g, f, tg):
        err = xm + g * f - tg
        return err * (1.0 / d), jnp.sum(err * err, axis=0, keepdims=True) * (0.5 / d), jnp.sum(err * (1.0 / d) * f, axis=0, keepdims=True)

    dy, loss_cols, d_gate2 = _tiled(loss_fn, [_rows(x_mid, tr), _const(gate2), _rows(f_out, tr), _rows(loss_target[0], tr)],
                                    [((t, d), F32, (tr, d), lambda i: (i, 0)), ((1, d), F32, (1, d), lambda i: (0, 0)),
                                     ((1, d), F32, (1, d), lambda i: (0, 0))], (t // tr,), "loss", acc=(1, 2))
    loss = lax.psum(jnp.sum(loss_cols), ("x", "y", "c"))

    df = _tiled(lambda a, g: (a * g,), [_rows(dy, tr), _const(gate2)], [((t, d), BF16, (tr, d), lambda i: (i, 0))], (t // tr,), "d_ffn_out")[0]
    d_w2 = _mm_xtdy(a_act, df, 1, BF16, "d_w2")
    da = _mm_dyw(df, g_w2, BF16, "d_act")
    du0, dt3, d_cw, d_cb = _ffn_act_bwd(u0, t3, cw_full, cb_full, da)
    d_w1 = _mm_xtdy(h2, du0, N_DEV, BF16, "d_w1")
    d_w3 = _mm_xtdy(h2, dt3, N_DEV, BF16, "d_w3")
    fly_ffn = _exchange_start([d_w1, d_w3, d_w2.reshape(N_DEV, ff_p, d)], "scatter_ffn_start", scatter=True)
    with _after(fly_ffn.token):
        dh2 = _mm_dyw(du0, g_w1, F32, "d_h2_a")
    dh2 = _mm_dyw(dt3, g_w3, F32, "d_h2_b", init=dh2)
    dx_mid, d_norm2, d_mods2 = _norm_mod_bwd(x_mid, norm2_g, mods2, dh2, dy, 0, tr, "norm_mod2_bwd")

    dm, d_gate1 = _tiled(lambda dxm, g, m_: (dxm * g, jnp.sum(dxm * m_, axis=0, keepdims=True)),
                         [_rows(dx_mid, tr), _const(gate1), _rows(mix, tr)],
                         [((t, d), BF16, (tr, d), lambda i: (i, 0)), ((1, d), F32, (1, d), lambda i: (0, 0))], (t // tr,), "d_resid1", acc=(1,))
    d_wout = _mm_xtdy(z, dm, 1, BF16, "d_w_out")
    dz = _mm_dyw(dm, g_out, F32, "d_merge")

    def merge_bwd(ga, gb, pa, pb, dzv):
        _, vjp = jax.vjp(_f_merge, ga, gb, pa, pb)
        return vjp(dzv)

    dga, dgb, dpa, dpb = _tiled(merge_bwd, [gcol(0), gcol(1), dtile(p_a), dtile(p_b), dtile(dz)],
                                [((t, d), BF16, (tr, td), lambda i, j: (i, j))] * 4, (t // tr, nd_t), "merge_bwd")
    d_wa = _mm_xtdy(y_a, dpa, N_DEV, BF16, "d_w_a")
    d_wb = _mm_xtdy(y_b, dpb, N_DEV, BF16, "d_w_b")
    fly_mix = _exchange_start([d_wa, d_wb, d_wout.reshape(N_DEV, d // N_DEV, d)], "scatter_mix_start", scatter=True)
    with _after(fly_mix.token):
        dy_a = _mm_dyw(dpa, g_a, F32, "d_y_a")
    dy_b = _mm_dyw(dpb, g_b, BF16, "d_y_b")

    def readout_bwd(a, b, gt, g, ct):
        _, vjp = jax.vjp(_f_readout, a, b, gt, g)
        da_, _, dgt, dg = vjp(ct)
        return da_, dgt, dg

    hsp = lambda dt: ((t, hw), dt, (trh, HEAD_DIM), lambda i, h: (i, h))
    do_h, d_gate_o, d_hnorm = _tiled(
        readout_bwd, [ospec(o_f), ospec(o_b), (p, (trh, HEAD_DIM), hcol(4, lat_h)), _const(hgrn_norm_g), (dy_a, (trh, HEAD_DIM), lambda i, h: (i, h))],
        [hsp(F32), hsp(BF16), ((1, HEAD_DIM), F32, (1, HEAD_DIM), lambda i, h: (0, 0))], (t // trh, N_HEADS), "readout_bwd", acc=(2,))
    dq1, dfl_f, dv1, dlb_f = _hgrn_bwd(p, lb_f, do_h, 1, False, n_ctx, None)
    dq_h, dfl_b, dv_h, dlb_b = _hgrn_bwd(p, lb_b, do_h, 2, True, n_ctx, (dq1, dv1))

    dqn, dkall, dv_na, dbias = _na_bwd(qn, kall, p, 7, bias, dy_b, n_ctx)

    def qk_bwd(tv, g, cs, sn, pm, ct):
        _, vjp = jax.vjp(lambda a, b: _f_qk(a, b, cs, sn, pm), tv, g)
        return vjp(ct)

    d_pq, d_qnorm = _tiled(qk_bwd, [(p, (tq, HEAD_DIM), hcol(5, lat0)), _const(na_q_norm_g), tab(cos, lat0), tab(sin, lat0), _const(perm),
                                    (dqn, (tq, HEAD_DIM), lambda i, h: (i, h))],
                           [((t, hw), BF16, (tq, HEAD_DIM), lambda i, h: (i, h)), ((1, HEAD_DIM), F32, (1, HEAD_DIM), lambda i, h: (0, 0))],
                           (t // tq, N_HEADS), "q_norm_rope_bwd", acc=(1,))
    d_pk, d_knorm = _tiled(qk_bwd, [(p, (tq, HEAD_DIM), hcol(6)), _const(na_k_norm_g), tab(cos), tab(sin), _const(perm),
                                    (dkall, (tq, HEAD_DIM), lambda i, h: (i, h))],
                           [((tt, hw), BF16, (tq, HEAD_DIM), lambda i, h: (i, h)), ((1, HEAD_DIM), F32, (1, HEAD_DIM), lambda i, h: (0, 0))],
                           (tt // tq, N_HEADS), "k_norm_rope_bwd", acc=(1,))

    zc = lambda w_: jnp.zeros((n_ctx, w_), BF16)
    lat_only = lambda a: jnp.concatenate([zc(a.shape[1]), a], axis=0)
    dp = jnp.concatenate([dq_h, dfl_f, dfl_b, dv_h, lat_only(d_gate_o), lat_only(d_pq), d_pk, dv_na, lat_only(dga), lat_only(dgb)], axis=1)
    d_win = _mm_xtdy(hcat, dp, N_DEV, BF16, "d_w_in")
    fly_win = _exchange_start([d_win], "scatter_w_in_start", scatter=True)
    with _after(fly_win.token):
        dhcat = _mm_dyw(dp, g_in, BF16, "d_hcat")
    zero_ctx = jnp.concatenate([jnp.zeros((n_ctx, d), F32), dx_mid], axis=0)
    dxcat, d_norm1, d_mods1 = _norm_mod_bwd(xcat, norm1_g, mods1, dhcat, zero_ctx, n_ctx_tiles, tr, "norm_mod1_bwd")
    grad_x = dxcat[n_ctx:][None]

    zd = jnp.zeros((1, d), F32)
    dmod_l = jnp.concatenate([d_mods1[1, 0], d_mods1[1, 1], d_gate1, d_mods2[1, 0], d_mods2[1, 1], d_gate2], axis=1)
    dmod_c = jnp.concatenate([d_mods1[0, 0], d_mods1[0, 1], zd, zd, zd, zd], axis=1)
    dmods = jnp.concatenate([dmod_l, dmod_c], axis=0).reshape(2, N_DEV, ca).transpose(1, 0, 2)
    dmods = jnp.pad(dmods, ((0, 0), (0, 6), (0, 0)))
    got = _exchange([dmods], "scatter_dmod", scatter=True)[0]
    dm_rows = jnp.concatenate([got[:, 0], got[:, 1]], axis=0)
    d_ada = _mm_xtdy(act, dm_rows, 1, F32, "d_ada_w")[0]
    back = _mm_dyw(dm_rows, ada16.reshape(1, d, ca), F32, "d_silu_c")
    d_cctx_part = _tiled(lambda b, v: (jnp.sum(b[N_DEV:], axis=0, keepdims=True) * (jax.nn.sigmoid(v) * (1.0 + v * (1.0 - jax.nn.sigmoid(v)))),),
                         [_const(back), _const(c_ctx.reshape(1, d))], [((1, d), F32, (1, d), lambda *_: (0, 0))], (), "d_c_ctx")[0]

    d_rel = _bias_grad(dbias, onehot)
    d_lb_soft = jnp.concatenate([dlb_f, dlb_b], axis=0)
    d_lb0 = _tiled(lambda s, g: (g * s * (1.0 - s),), [_const(lb_soft), _const(d_lb_soft)], [((2, hw), F32, (2, hw), lambda *_: (0, 0))], (), "d_lb")[0]
    d_lb_full = jnp.stack([d_lb0, -d_lb0], axis=1)
    d_cw_l = d_cw[:3].reshape(3, N_DEV, ff_p)[:, :, :ff_l].reshape(1, 3, N_DEV * ff_l)
    d_cb_l = d_cb.reshape(N_DEV, ff_p)[:, :ff_l].reshape(1, N_DEV * ff_l)
    small = [d_cctx_part.reshape(d), (dmod_l + dmod_c), d_norm1, d_norm2, d_lb_full, d_hnorm, d_qnorm, d_knorm, d_rel, d_cw_l, d_cb_l]
    pk = _Pack([a.shape for a in small])
    tot = _sum_parts(_exchange([pk.pack(small)], "gather_small", scatter=False)[0], "sum_small")
    g_cctx, g_ada_b, g_n1, g_n2, g_lb, g_hn, g_qn, g_kn, g_rel, g_cw, g_cb = pk.unpack(tot)
    g_lb = lax.dynamic_slice_in_dim(g_lb, me * HEAD_DIM, HEAD_DIM, axis=2)
    g_cw = lax.dynamic_slice_in_dim(g_cw, me * ff_l, ff_l, axis=2)
    small_names = [("c_ctx", g_cctx, c_ctx, m_c_ctx, v_c_ctx), ("ada_b", g_ada_b, ada_b, m_ada_b, v_ada_b),
                   ("norm1_g", g_n1, norm1_g, m_norm1_g, v_norm1_g), ("norm2_g", g_n2, norm2_g, m_norm2_g, v_norm2_g),
                   ("hgrn_lb_logits", g_lb, hgrn_lb_logits, m_hgrn_lb_logits, v_hgrn_lb_logits),
                   ("hgrn_norm_g", g_hn, hgrn_norm_g, m_hgrn_norm_g, v_hgrn_norm_g), ("na_q_norm_g", g_qn, na_q_norm_g, m_na_q_norm_g, v_na_q_norm_g),
                   ("na_k_norm_g", g_kn, na_k_norm_g, m_na_k_norm_g, v_na_k_norm_g), ("na_rel_bias", g_rel, na_rel_bias, m_na_rel_bias, v_na_rel_bias),
                   ("ffn_conv_w", g_cw, ffn_conv_w, m_ffn_conv_w, v_ffn_conv_w), ("ffn_conv_b", g_cb, ffn_conv_b, m_ffn_conv_b, v_ffn_conv_b)]
    pk2 = _Pack([s[1].shape for s in small_names])
    sd, sm, sv = _adam_small(*[pk2.pack([s[i] for s in small_names]) for i in (1, 2, 3, 4)])
    sd, sm, sv = pk2.unpack(sd), pk2.unpack(sm), pk2.unpack(sv)
    res = {s[0]: (s[1], sd[i], sm[i], sv[i]) for i, s in enumerate(small_names)}

    res["ada_w"] = _adam_big(d_ada[None], ada_w[0], m_ada_w[0], v_ada_w[0], "adam_ada")
    r_w1, r_w3, r_w2 = _exchange_wait(fly_ffn, [res["ada_w"][1], sd[0]], "scatter_ffn_wait")
    res["ffn_w1"] = _adam_big(r_w1, ffn_w1[0], m_ffn_w1[0], v_ffn_w1[0], "adam_w1")
    res["ffn_w3"] = _adam_big(r_w3, ffn_w3[0], m_ffn_w3[0], v_ffn_w3[0], "adam_w3")
    res["ffn_w2"] = _adam_big(r_w2, ffn_w2[0], m_ffn_w2[0], v_ffn_w2[0], "adam_w2")
    r_a, r_b, r_out = _exchange_wait(fly_mix, [res["ffn_w2"][1]], "scatter_mix_wait")
    res["w_branch_a"] = _adam_big(r_a, w_branch_a[0], m_w_branch_a[0], v_w_branch_a[0], "adam_w_a")
    res["w_branch_b"] = _adam_big(r_b, w_branch_b[0], m_w_branch_b[0], v_w_branch_b[0], "adam_w_b")
    res["w_out"] = _adam_big(r_out, w_out[0], m_w_out[0], v_w_out[0], "adam_w_out")
    r_in, = _exchange_wait(fly_win, [res["w_out"][1]], "scatter_w_in_wait")
    res["w_in"] = _adam_big(r_in, w_in[0], m_w_in[0], v_w_in[0], "adam_w_in")
    for k in ("w_in", "w_branch_a", "w_branch_b", "w_out", "ffn_w1", "ffn_w3", "ffn_w2", "ada_w"):
        res[k] = tuple(a[None] for a in res[k])

    order = ["c_ctx", "ada_w", "ada_b", "norm1_g", "norm2_g", "w_in", "hgrn_lb_logits", "hgrn_norm_g", "na_q_norm_g", "na_k_norm_g",
             "na_rel_bias", "w_branch_a", "w_branch_b", "w_out", "ffn_w1", "ffn_w3", "ffn_conv_w", "ffn_conv_b", "ffn_w2"]
    shapes = {"c_ctx": c_ctx.shape, "ada_b": ada_b.shape, "norm1_g": norm1_g.shape, "norm2_g": norm2_g.shape,
              "hgrn_lb_logits": hgrn_lb_logits.shape, "hgrn_norm_g": hgrn_norm_g.shape, "na_q_norm_g": na_q_norm_g.shape,
              "na_k_norm_g": na_k_norm_g.shape, "na_rel_bias": na_rel_bias.shape, "ffn_conv_w": ffn_conv_w.shape, "ffn_conv_b": ffn_conv_b.shape}
    outs = [loss, grad_x]
    for part in range(4):
        for k in order:
            a = res[k][part]
            outs.append(a.reshape(shapes[k]) if k in shapes else a)
    return tuple(outs)
```

```python
import functools

import numpy as np
import jax
import jax.numpy as jnp
from jax import lax
from jax.experimental import pallas as pl
from jax.experimental.pallas import tpu as pltpu

F32 = jnp.float32
BF16 = jnp.bfloat16
HIGHEST = lax.Precision.HIGHEST

N_DEV = 8
MESH_ID = pl.DeviceIdType.MESH
LANE = 128
HEAD_DIM = 128
N_HEADS = 8
GRID_W = 64
WIN_R = 8
WIN_C = 16
ROPE_THETA = 10000.0
EPS = 1e-6
N_MOD = 6
HGRN_BLOCK = 16
NEG_BIG = -1e30
VMEM_LIMIT = 56 << 20

ADAM_LR = 0.001
ADAM_B1 = 0.9
ADAM_B2 = 0.999
ADAM_EPS = 1e-08
ADAM_WD = 0.01
ADAM_STEP = 10

HBM_SPEC = pl.BlockSpec(memory_space=pltpu.HBM)


_ORDER_AFTER = []


class _after:
    def __init__(self, *arrs):
        self.arrs = list(arrs)

    def __enter__(self):
        _ORDER_AFTER.extend(self.arrs)

    def __exit__(self, *exc):
        del _ORDER_AFTER[:]


def _pcall(body, *, name, out_shape, grid=None, in_specs=None, out_specs=None, scratch=(), aliases=None):
    kw = {}
    if grid is not None:
        kw["grid"] = grid
    extra = []
    if _ORDER_AFTER and in_specs is not None:
        extra = list(_ORDER_AFTER)
        del _ORDER_AFTER[:]
        n_in, n_extra, inner = len(in_specs), len(extra), body
        in_specs = list(in_specs) + [pl.BlockSpec(memory_space=pl.ANY)] * n_extra

        def body(*refs):
            return inner(*refs[:n_in], *refs[n_in + n_extra:])

    if extra:
        call = _pcall_inner(body, name, out_shape, kw, in_specs, out_specs, scratch, aliases)
        return lambda *args: call(*args, *extra)
    return _pcall_inner(body, name, out_shape, kw, in_specs, out_specs, scratch, aliases)


def _pcall_inner(body, name, out_shape, kw, in_specs, out_specs, scratch, aliases):
    if in_specs is not None:
        kw["in_specs"] = in_specs
    if out_specs is not None:
        kw["out_specs"] = out_specs
    if scratch:
        kw["scratch_shapes"] = list(scratch)
    if aliases:
        kw["input_output_aliases"] = aliases
    return pl.pallas_call(body, name=name, out_shape=out_shape,
                          compiler_params=pltpu.CompilerParams(vmem_limit_bytes=VMEM_LIMIT), **kw)


def _pick(dim, cands):
    for c in cands:
        if c <= dim and dim % c == 0:
            return c
    return dim


def _sds(shape, dtype):
    return jax.ShapeDtypeStruct(tuple(shape), dtype)


def _peers():
    x, y, c = lax.axis_index("x"), lax.axis_index("y"), lax.axis_index("c")
    out = []
    for k in range(1, N_DEV):
        px = 1 - x if (k >> 2) & 1 else x
        py = 1 - y if (k >> 1) & 1 else y
        pc = 1 - c if k & 1 else c
        out.append((k, (px, py, pc), 4 * px + 2 * py + pc))
    return 4 * x + 2 * y + c, out


def _exchange(arrs, name, scatter):
    n = len(arrs)

    def body(*refs):
        ins, outs = refs[:n], refs[n:2 * n]
        send, recv, loc = refs[2 * n:]
        me, peers = _peers()
        started = []
        for i in range(n):
            src = ins[i].at[me] if scatter else ins[i]
            cp = pltpu.make_async_copy(src, outs[i].at[me], loc.at[i])
            cp.start()
            started.append(cp)
        sends = []
        for k, peer, pidx in peers:
            for i in range(n):
                src = ins[i].at[pidx] if scatter else ins[i]
                cp = pltpu.make_async_remote_copy(src_ref=src, dst_ref=outs[i].at[me], send_sem=send.at[i * 7 + k - 1],
                                                  recv_sem=recv.at[i * 7 + k - 1], device_id=peer, device_id_type=MESH_ID)
                cp.start()
                sends.append(cp)
        for k, peer, pidx in peers:
            for i in range(n):
                src = ins[i].at[pidx] if scatter else ins[i]
                pltpu.make_async_remote_copy(src_ref=src, dst_ref=outs[i].at[pidx], send_sem=send.at[i * 7 + k - 1],
                                             recv_sem=recv.at[i * 7 + k - 1], device_id=peer, device_id_type=MESH_ID).wait_recv()
        for cp in sends:
            cp.wait_send()
        for cp in started:
            cp.wait()

    out_shape = [_sds(a.shape if scatter else (N_DEV,) + a.shape, a.dtype) for a in arrs]
    res = _pcall(body, name=name, out_shape=out_shape, in_specs=[HBM_SPEC] * n, out_specs=[HBM_SPEC] * n,
                 scratch=[pltpu.SemaphoreType.DMA((7 * n,)), pltpu.SemaphoreType.DMA((7 * n,)), pltpu.SemaphoreType.DMA((n,))])(*arrs)
    return list(res)


class _InFlight:
    def __init__(self, send, recv, srcs, lands, token, scatter):
        self.send, self.recv, self.srcs, self.lands, self.token, self.scatter = send, recv, srcs, lands, token, scatter


SEM_SPEC = pl.BlockSpec(memory_space=pltpu.SEMAPHORE)
SIDE_EFFECT = pltpu.SideEffectType.DATAFLOW_SIDE_EFFECTING


def _exchange_start(arrs, name, scatter):
    n = len(arrs)

    def body(*refs):
        ins, lands = refs[:n], refs[n:2 * n]
        send, recv = refs[2 * n], refs[2 * n + 1]
        token = refs[4 * n + 2]
        loc = refs[4 * n + 3]
        me, peers = _peers()
        own = [pltpu.make_async_copy(ins[i].at[me] if scatter else ins[i], lands[i].at[me], loc.at[i]) for i in range(n)]
        for cp in own:
            cp.start()
        for cp in own:
            cp.wait()
        for k, peer, pidx in peers:
            for i in range(n):
                src = ins[i].at[pidx] if scatter else ins[i]
                pltpu.make_async_remote_copy(src_ref=src, dst_ref=lands[i].at[me], send_sem=send.at[i * 7 + k - 1],
                                             recv_sem=recv.at[i * 7 + k - 1], device_id=peer, device_id_type=MESH_ID).start()
        token[...] = jnp.zeros_like(token)

    land_shapes = [a.shape if scatter else (N_DEV,) + a.shape for a in arrs]
    hbm = lambda a: pltpu.with_memory_space_constraint(a, pltpu.HBM)
    args = [hbm(a) for a in arrs] + [hbm(lax.empty(s, a.dtype)) for s, a in zip(land_shapes, arrs)]
    out_shape = ([pltpu.SemaphoreType.DMA((7 * n,)), pltpu.SemaphoreType.DMA((7 * n,))]
                 + [pltpu.HBM(a.shape, a.dtype) for a in arrs] + [pltpu.HBM(s, a.dtype) for s, a in zip(land_shapes, arrs)]
                 + [_sds((8, LANE), F32)])
    res = pl.pallas_call(
        body, name=name, out_shape=out_shape, in_specs=[HBM_SPEC] * (2 * n),
        out_specs=[SEM_SPEC, SEM_SPEC] + [HBM_SPEC] * (2 * n) + [pl.BlockSpec(memory_space=pltpu.VMEM)],
        input_output_aliases={i: 2 + i for i in range(2 * n)},
        scratch_shapes=[pltpu.SemaphoreType.DMA((n,))],
        compiler_params=pltpu.CompilerParams(has_side_effects=SIDE_EFFECT))(*args)
    return _InFlight(res[0], res[1], list(res[2:2 + n]), list(res[2 + n:2 + 2 * n]), res[2 + 2 * n], scatter)


def _exchange_wait(h, after, name):
    n = len(h.srcs)
    scatter = h.scatter
    after = list(after)

    def body(*refs):
        ins, lands = refs[:n], refs[n:2 * n]
        send, recv = refs[2 * n], refs[2 * n + 1]
        _, peers = _peers()
        for k, peer, pidx in peers:
            for i in range(n):
                src = ins[i].at[pidx] if scatter else ins[i]
                cp = pltpu.make_async_remote_copy(src_ref=src, dst_ref=lands[i].at[pidx], send_sem=send.at[i * 7 + k - 1],
                                                  recv_sem=recv.at[i * 7 + k - 1], device_id=peer, device_id_type=MESH_ID)
                cp.wait_send()
                cp.wait_recv()

    res = pl.pallas_call(
        body, name=name, out_shape=[pltpu.HBM(a.shape, a.dtype) for a in h.srcs + h.lands],
        in_specs=[HBM_SPEC] * (2 * n) + [SEM_SPEC, SEM_SPEC] + [pl.BlockSpec(memory_space=pl.ANY)] * len(after),
        out_specs=[HBM_SPEC] * (2 * n), input_output_aliases={i: i for i in range(2 * n)},
        compiler_params=pltpu.CompilerParams(has_side_effects=SIDE_EFFECT))(*h.srcs, *h.lands, h.send, h.recv, *after)
    return list(res[n:])


def _mm_xw(x, g, out_dtype, name, tm_c=(768, 512, 384, 256, 128, 64, 16), tn_c=(768, 512, 256, 128), tk_c=(2048, 1024, 768, 512, 256)):
    m, r = x.shape
    nb, r2, cl = g.shape
    assert r == r2
    tm, tn, tk = _pick(m, tm_c), _pick(cl, tn_c), _pick(r, tk_c)
    q, nk = cl // tn, r // tk

    def body(x_ref, g_ref, o_ref, *acc):
        p = lax.dot_general(x_ref[...].astype(BF16), g_ref[...], (((1,), (0,)), ((), ())), preferred_element_type=F32)
        if nk == 1:
            o_ref[...] = p.astype(o_ref.dtype)
        else:
            k = pl.program_id(2)

            @pl.when(k == 0)
            def _():
                acc[0][...] = p

            @pl.when(k > 0)
            def _():
                acc[0][...] += p

            @pl.when(k == nk - 1)
            def _():
                o_ref[...] = acc[0][...].astype(o_ref.dtype)

    return _pcall(
        body, name=name, grid=(m // tm, nb * q, nk),
        in_specs=[pl.BlockSpec((tm, tk), lambda i, j, k: (i, k)), pl.BlockSpec((None, tk, tn), lambda i, j, k: (j // q, k, j % q))],
        out_specs=pl.BlockSpec((tm, tn), lambda i, j, k: (i, j)),
        out_shape=_sds((m, nb * cl), out_dtype),
        scratch=[] if nk == 1 else [pltpu.VMEM((tm, tn), F32)])(x, g)


def _mm_dyw(dy, g, out_dtype, name, init=None, tm_c=(768, 512, 384, 256, 128), tn_c=(1024, 512, 256, 128), tk_c=(768, 512, 256, 128)):
    m, n = dy.shape
    nb, r, cl = g.shape
    assert n == nb * cl
    tm, tn, tk = _pick(m, tm_c), _pick(r, tn_c), _pick(cl, tk_c)
    q = cl // tk
    nk = nb * q
    has_init = init is not None

    def body(*refs):
        if has_init:
            dy_ref, g_ref, i_ref, o_ref, acc = refs
        else:
            dy_ref, g_ref, o_ref, acc = refs
        k = pl.program_id(2)
        p = lax.dot_general(dy_ref[...].astype(BF16), g_ref[...], (((1,), (1,)), ((), ())), preferred_element_type=F32)

        @pl.when(k == 0)
        def _():
            acc[...] = p + i_ref[...].astype(F32) if has_init else p

        @pl.when(k > 0)
        def _():
            acc[...] += p

        @pl.when(k == nk - 1)
        def _():
            o_ref[...] = acc[...].astype(o_ref.dtype)

    in_specs = [pl.BlockSpec((tm, tk), lambda i, j, k: (i, k)), pl.BlockSpec((None, tn, tk), lambda i, j, k: (k // q, j, k % q))]
    args = [dy, g]
    if has_init:
        in_specs.append(pl.BlockSpec((tm, tn), lambda i, j, k: (i, j)))
        args.append(init)
    return _pcall(body, name=name, grid=(m // tm, r // tn, nk), in_specs=in_specs,
                  out_specs=pl.BlockSpec((tm, tn), lambda i, j, k: (i, j)), out_shape=_sds((m, r), out_dtype),
                  scratch=[pltpu.VMEM((tm, tn), F32)])(*args)


def _mm_xtdy(x, dy, nb, out_dtype, name, tm_c=(1024, 512, 256, 128), tn_c=(768, 512, 256, 128), tk_c=(768, 512, 256, 128, 16)):
    t, r = x.shape
    t2, n = dy.shape
    assert t == t2 and n % nb == 0
    cl = n // nb
    tm, tn, tk = _pick(r, tm_c), _pick(cl, tn_c), _pick(t, tk_c)
    q, nk = cl // tn, t // tk

    def body(x_ref, dy_ref, o_ref, *acc):
        p = lax.dot_general(x_ref[...].astype(BF16), dy_ref[...].astype(BF16), (((0,), (0,)), ((), ())), preferred_element_type=F32)
        if nk == 1:
            o_ref[...] = p.astype(o_ref.dtype)
        else:
            k = pl.program_id(2)

            @pl.when(k == 0)
            def _():
                acc[0][...] = p

            @pl.when(k > 0)
            def _():
                acc[0][...] += p

            @pl.when(k == nk - 1)
            def _():
                o_ref[...] = acc[0][...].astype(o_ref.dtype)

    return _pcall(
        body, name=name, grid=(r // tm, nb * q, nk),
        in_specs=[pl.BlockSpec((tk, tm), lambda i, j, k: (k, i)), pl.BlockSpec((tk, tn), lambda i, j, k: (k, j))],
        out_specs=pl.BlockSpec((None, tm, tn), lambda i, j, k: (j // q, i, j % q)),
        out_shape=_sds((nb, r, cl), out_dtype),
        scratch=[] if nk == 1 else [pltpu.VMEM((tm, tn), F32)])(x, dy)


def _mm_f32(a, b, name, trans_b=False):
    dims = (((1,), (1,)), ((), ())) if trans_b else (((1,), (0,)), ((), ()))
    n = b.shape[0] if trans_b else b.shape[1]

    def body(a_ref, b_ref, o_ref):
        o_ref[...] = lax.dot_general(a_ref[...], b_ref[...], dims, precision=HIGHEST, preferred_element_type=F32)

    return _pcall(body, name=name, out_shape=_sds((a.shape[0], n), F32))(a, b)


def _tiled(fn, ins, outs, grid, name, acc=()):
    n_in = len(ins)
    grid = tuple(grid) or (1,)
    nd = len(grid)

    def body(*refs):
        vals = fn(*[r[...] for r in refs[:n_in]])
        if not isinstance(vals, (tuple, list)):
            vals = (vals,)
        first = None
        for o, (ref, v) in enumerate(zip(refs[n_in:], vals)):
            if o in acc:
                if first is None:
                    first = pl.program_id(0) == 0
                    for a in range(1, nd):
                        first = jnp.logical_and(first, pl.program_id(a) == 0)

                @pl.when(first)
                def _(ref=ref, v=v):
                    ref[...] = v.astype(ref.dtype)

                @pl.when(jnp.logical_not(first))
                def _(ref=ref, v=v):
                    ref[...] += v.astype(ref.dtype)
            else:
                ref[...] = v.astype(ref.dtype)

    res = _pcall(body, name=name, grid=grid,
                 in_specs=[pl.BlockSpec(b, im) for _, b, im in ins],
                 out_specs=[pl.BlockSpec(b, im) for _, _, b, im in outs],
                 out_shape=[_sds(s, d) for s, d, _, _ in outs])(*[a for a, _, _ in ins])
    return list(res)


def _rows(a, tr):
    return (a, (tr, a.shape[1]), lambda i, *_: (i, 0))


def _const(a):
    nd = a.ndim
    return (a, a.shape, lambda *_: (0,) * nd)


def _cast_bf16(w, name):
    r, c = w.shape
    tr = _pick(r, (256, 128, 64, 16))
    return _tiled(lambda v: v, [_rows(w, tr)], [((r, c), BF16, (tr, c), lambda i: (i, 0))], (r // tr,), name)[0]


def _f_norm_mod(x, g, sh, sc):
    y = x * lax.rsqrt(jnp.mean(x * x, axis=-1, keepdims=True) + EPS) * g
    return y * (1.0 + sc) + sh


def _f_qk(t, g, cos, sin, perm):
    y = t * lax.rsqrt(jnp.mean(t * t, axis=-1, keepdims=True) + EPS) * g
    return y * cos + jnp.dot(y, perm, precision=HIGHEST, preferred_element_type=F32) * sin


def _f_readout(of, ob, gate, g):
    o = of + ob
    on = o * lax.rsqrt(jnp.mean(o * o, axis=-1, keepdims=True) + EPS) * g
    return on * (gate * jax.nn.sigmoid(gate))


def _f_merge(ga, gb, pa, pb):
    return jax.nn.sigmoid(ga) * pa + jax.nn.sigmoid(gb) * pb


def _f_resid(x, gate, m):
    return x + gate * m


def _norm_mod_fwd(xcat, g, mods, n_ctx_tiles, tr):
    tt, d = xcat.shape
    which = lambda i: (jnp.where(i >= n_ctx_tiles, 1, 0), 0, 0, 0)

    def fn(x, gg, md):
        return _f_norm_mod(x, gg, md[0], md[1])

    return _tiled(fn, [_rows(xcat, tr), _const(g), (mods, (None, 2, 1, d), which)],
                  [((tt, d), BF16, (tr, d), lambda i: (i, 0))], (tt // tr,), "norm_mod_fwd")[0]


def _norm_mod_bwd(xcat, g, mods, dh, extra, n_ctx_tiles, tr, name):
    tt, d = xcat.shape
    nt = tt // tr
    has_extra = extra is not None

    def body(*refs):
        if has_extra:
            x_ref, g_ref, m_ref, dh_ref, e_ref, dx_ref, dg_ref, dm_ref = refs
        else:
            x_ref, g_ref, m_ref, dh_ref, dx_ref, dg_ref, dm_ref = refs
        i = pl.program_id(0)
        md = m_ref[...]
        _, vjp = jax.vjp(_f_norm_mod, x_ref[...], g_ref[...], md[0], md[1])
        dx, dg, dsh, dsc = vjp(dh_ref[...].astype(F32))
        dx_ref[...] = dx + e_ref[...] if has_extra else dx

        @pl.when(i == 0)
        def _():
            dg_ref[...] = dg

        @pl.when(i > 0)
        def _():
            dg_ref[...] += dg

        fresh = jnp.logical_or(i == 0, i == n_ctx_tiles)

        @pl.when(fresh)
        def _():
            dm_ref[0] = dsh
            dm_ref[1] = dsc

        @pl.when(jnp.logical_not(fresh))
        def _():
            dm_ref[0] += dsh
            dm_ref[1] += dsc

    which = lambda i: (jnp.where(i >= n_ctx_tiles, 1, 0), 0, 0, 0)
    row = pl.BlockSpec((tr, d), lambda i: (i, 0))
    in_specs = [row, pl.BlockSpec((1, d), lambda i: (0, 0)), pl.BlockSpec((None, 2, 1, d), which), row]
    args = [xcat, g, mods, dh]
    if has_extra:
        in_specs.append(row)
        args.append(extra)
    return _pcall(body, name=name, grid=(nt,), in_specs=in_specs,
                  out_specs=[row, pl.BlockSpec((1, d), lambda i: (0, 0)), pl.BlockSpec((None, 2, 1, d), which)],
                  out_shape=[_sds((tt, d), F32), _sds((1, d), F32), _sds((2, 2, 1, d), F32)])(*args)


def _hgrn_tri(reverse):
    t = np.arange(HGRN_BLOCK)
    tri = (t[None, :] >= t[:, None]) if reverse else (t[None, :] <= t[:, None])
    tri = tri.astype(np.float32)
    return jnp.asarray(tri), jnp.asarray(tri.T.copy())


def _hgrn_rowblock(n, n_ctx_blocks, n_blocks, reverse):
    if not reverse:
        return n
    return jnp.where(n < n_ctx_blocks, n_ctx_blocks - 1 - n, n_blocks - 1 - n + n_ctx_blocks)


def _hgrn_gates(fl, lb):
    sg = jax.nn.sigmoid(fl)
    f = lb + (1.0 - lb) * sg
    return sg, f, jnp.log(f), 1.0 - f


def _hgrn_intra_mask(reverse):
    tio = lax.broadcasted_iota(jnp.int32, (HGRN_BLOCK, HEAD_DIM), 0)
    return (lambda s: tio <= s) if reverse else (lambda s: tio >= s)


def _unrolled(nblk, u, fn, init):
    assert nblk % u == 0

    def trip(b, c):
        for j in range(u):
            c = fn(b * u + j, j, c)
        return c

    return lax.fori_loop(0, nblk // u, trip, init)


HGRN_UNROLL_FWD = (4, 8)
HGRN_UNROLL_BWD = (8, 4)


def _hgrn_fwd(p, lb, seg_f, reverse, n_ctx_rows):
    tt = p.shape[0]
    hb = HGRN_BLOCK
    nblk, nctx = tt // hb, n_ctx_rows // hb
    tri, _ = _hgrn_tri(reverse)
    u1, u3 = HGRN_UNROLL_FWD

    def body(q_ref, f_ref, v_ref, lb_ref, tri_ref, o_ref, st_all, dec_all, qe_buf, cum_blk, k_blk, v_blk):
        mask = _hgrn_intra_mask(reverse)
        lbv = lb_ref[...]

        def phase1(n, slot, c):
            r0 = pl.multiple_of(_hgrn_rowblock(n, nctx, nblk, reverse) * hb, hb)
            q, v = q_ref[pl.ds(r0, hb), :], v_ref[pl.ds(r0, hb), :]
            _, f, g, k = _hgrn_gates(f_ref[pl.ds(r0, hb), :], lbv)
            cum = jnp.dot(tri_ref[...], g, precision=HIGHEST, preferred_element_type=F32)
            tot = jnp.sum(g, axis=0, keepdims=True)
            cum_blk[slot] = cum
            k_blk[slot] = k
            v_blk[slot] = v
            oi = jnp.zeros((hb, HEAD_DIM), F32)
            for s in range(hb):
                e = jnp.where(mask(s), jnp.exp(jnp.minimum(cum - cum_blk[slot, s:s + 1, :], 0.0)), 0.0)
                a_s = jnp.sum(q * e * k_blk[slot, s:s + 1, :], axis=-1, keepdims=True)
                oi = oi + a_s * v_blk[slot, s:s + 1, :]
            o_ref[pl.ds(r0, hb), :] = oi
            qe_buf[pl.ds(r0, hb), :] = q * jnp.exp(cum)
            kl = k * jnp.exp(tot - cum)
            st_all[n] = lax.dot_general(v.astype(BF16), kl.astype(BF16), (((0,), (0,)), ((), ())), preferred_element_type=F32)
            dec_all[pl.ds(n, 1), :] = jnp.exp(tot)
            return c

        _unrolled(nblk, u1, phase1, 0)

        def phase2(n, st):
            kv = st_all[n]
            st_all[n] = st
            return st * dec_all[pl.ds(n, 1), :] + kv

        lax.fori_loop(0, nblk, phase2, jnp.zeros((HEAD_DIM, HEAD_DIM), F32))

        def phase3(n, slot, c):
            r0 = pl.multiple_of(_hgrn_rowblock(n, nctx, nblk, reverse) * hb, hb)
            o_ref[pl.ds(r0, hb), :] += lax.dot_general(qe_buf[pl.ds(r0, hb), :].astype(BF16), st_all[n].astype(BF16),
                                                        (((1,), (1,)), ((), ())), preferred_element_type=F32)
            return c

        _unrolled(nblk, u3, phase3, 0)

    col = lambda seg: pl.BlockSpec((tt, HEAD_DIM), lambda h, seg=seg: (0, seg * N_HEADS + h))
    blk = pltpu.VMEM((u1, hb, HEAD_DIM), F32)
    return _pcall(
        body, name="hgrn_fwd_rev" if reverse else "hgrn_fwd", grid=(N_HEADS,),
        in_specs=[col(0), col(seg_f), col(3), pl.BlockSpec((1, HEAD_DIM), lambda h: (0, h)), pl.BlockSpec((hb, hb), lambda h: (0, 0))],
        out_specs=pl.BlockSpec((tt, HEAD_DIM), lambda h: (0, h)),
        out_shape=_sds((tt, N_HEADS * HEAD_DIM), F32),
        scratch=[pltpu.VMEM((nblk, HEAD_DIM, HEAD_DIM), F32), pltpu.VMEM((nblk, HEAD_DIM), F32), pltpu.VMEM((tt, HEAD_DIM), F32),
                 blk, blk, blk])(p, p, p, lb, tri)


def _hgrn_bwd(p, lb, do, seg_f, reverse, n_ctx_rows, prev):
    tt = p.shape[0]
    hb = HGRN_BLOCK
    nblk, nctx = tt // hb, n_ctx_rows // hb
    tri, tri_t = _hgrn_tri(reverse)
    last_row = 0 if reverse else hb - 1
    has_prev = prev is not None
    u1, u3 = HGRN_UNROLL_BWD

    def body(*refs):
        q_ref, f_ref, v_ref, lb_ref, tri_ref, trit_ref, do_ref = refs[:7]
        refs = refs[7:]
        if has_prev:
            pq_ref, pv_ref = refs[:2]
            refs = refs[2:]
        dq_ref, dfl_ref, dv_ref, dlb_ref, st_all, dd_all, dec_all, cum_buf, cum_blk, k_blk, v_blk, dk_blk, dv_blk = refs
        mask = _hgrn_intra_mask(reverse)
        lbv = lb_ref[...]
        tio = lax.broadcasted_iota(jnp.int32, (hb, HEAD_DIM), 0)

        def rows_of(n):
            rb = _hgrn_rowblock(n, nctx, nblk, reverse)
            return rb, pl.multiple_of(rb * hb, hb)

        def load_do(rb):
            lat0 = pl.multiple_of(jnp.maximum(rb - nctx, 0) * hb, hb)
            return jnp.where(rb >= nctx, do_ref[pl.ds(lat0, hb), :], 0.0)

        def phase1(n, slot, c):
            rb, r0 = rows_of(n)
            q, v = q_ref[pl.ds(r0, hb), :], v_ref[pl.ds(r0, hb), :]
            _, f, g, k = _hgrn_gates(f_ref[pl.ds(r0, hb), :], lbv)
            cum = jnp.dot(tri_ref[...], g, precision=HIGHEST, preferred_element_type=F32)
            tot = jnp.sum(g, axis=0, keepdims=True)
            cum_buf[pl.ds(r0, hb), :] = cum
            kl = k * jnp.exp(tot - cum)
            st_all[n] = lax.dot_general(v.astype(BF16), kl.astype(BF16), (((0,), (0,)), ((), ())), preferred_element_type=F32)
            dec_all[pl.ds(n, 1), :] = jnp.exp(tot)
            qe = q * jnp.exp(cum)
            dd_all[n] = lax.dot_general(load_do(rb).astype(BF16), qe.astype(BF16), (((0,), (0,)), ((), ())), preferred_element_type=F32)
            return c

        _unrolled(nblk, u1, phase1, 0)

        def phase2(n, st):
            kv = st_all[n]
            st_all[n] = st
            return st * dec_all[pl.ds(n, 1), :] + kv

        lax.fori_loop(0, nblk, phase2, jnp.zeros((HEAD_DIM, HEAD_DIM), F32))

        def phase2r(i, dst):
            n = nblk - 1 - i
            u = dd_all[n]
            dd_all[n] = dst
            return u + dst * dec_all[pl.ds(n, 1), :]

        lax.fori_loop(0, nblk, phase2r, jnp.zeros((HEAD_DIM, HEAD_DIM), F32))

        def phase3(n, slot, dlb):
            rb, r0 = rows_of(n)
            q, v = q_ref[pl.ds(r0, hb), :], v_ref[pl.ds(r0, hb), :]
            sg, f, g, k = _hgrn_gates(f_ref[pl.ds(r0, hb), :], lbv)
            cum = cum_buf[pl.ds(r0, hb), :]
            tot = jnp.sum(g, axis=0, keepdims=True)
            dob = load_do(rb)
            st, dst = st_all[n], dd_all[n]
            e_cum = jnp.exp(cum)
            e_rest = jnp.exp(tot - cum)
            dq = jnp.dot(dob.astype(BF16), st.astype(BF16), preferred_element_type=F32) * e_cum
            dk_inter = jnp.dot(v.astype(BF16), dst.astype(BF16), preferred_element_type=F32) * e_rest
            dv = lax.dot_general((k * e_rest).astype(BF16), dst.astype(BF16), (((1,), (1,)), ((), ())), preferred_element_type=F32)
            cum_blk[slot] = cum
            k_blk[slot] = k
            v_blk[slot] = v
            for s in range(hb):
                e = jnp.where(mask(s), jnp.exp(jnp.minimum(cum - cum_blk[slot, s:s + 1, :], 0.0)), 0.0)
                a_s = jnp.sum(q * e * k_blk[slot, s:s + 1, :], axis=-1, keepdims=True)
                da_s = jnp.sum(dob * v_blk[slot, s:s + 1, :], axis=-1, keepdims=True)
                gs = da_s * e
                dq = dq + gs * k_blk[slot, s:s + 1, :]
                dk_blk[slot, s:s + 1, :] = jnp.sum(gs * q, axis=0, keepdims=True)
                dv_blk[slot, s:s + 1, :] = jnp.sum(a_s * dob, axis=0, keepdims=True)
            dk = dk_inter + dk_blk[slot]
            dv = dv + dv_blk[slot]
            d_tot = jnp.sum(k * dk_inter, axis=0, keepdims=True) + jnp.exp(tot) * jnp.sum(dst * st, axis=0, keepdims=True)
            dcum = q * dq - k * dk + jnp.where(tio == last_row, d_tot, 0.0)
            dg = jnp.dot(trit_ref[...], dcum, precision=HIGHEST, preferred_element_type=F32)
            df = dg / f - dk
            if has_prev:
                dq = dq + pq_ref[pl.ds(r0, hb), :].astype(F32)
                dv = dv + pv_ref[pl.ds(r0, hb), :].astype(F32)
            dq_ref[pl.ds(r0, hb), :] = dq.astype(dq_ref.dtype)
            dv_ref[pl.ds(r0, hb), :] = dv.astype(dv_ref.dtype)
            dfl_ref[pl.ds(r0, hb), :] = (df * (1.0 - lbv) * sg * (1.0 - sg)).astype(dfl_ref.dtype)
            return dlb + jnp.sum(df * (1.0 - sg), axis=0, keepdims=True)

        dlb_ref[...] = _unrolled(nblk, u3, phase3, jnp.zeros((1, HEAD_DIM), F32))

    col = lambda seg: pl.BlockSpec((tt, HEAD_DIM), lambda h, seg=seg: (0, seg * N_HEADS + h))
    head = pl.BlockSpec((tt, HEAD_DIM), lambda h: (0, h))
    lbs = pl.BlockSpec((1, HEAD_DIM), lambda h: (0, h))
    tris = pl.BlockSpec((hb, hb), lambda h: (0, 0))
    in_specs = [col(0), col(seg_f), col(3), lbs, tris, tris, pl.BlockSpec((do.shape[0], HEAD_DIM), lambda h: (0, h))]
    args = [p, p, p, lb, tri, tri_t, do]
    mid = F32 if not has_prev else BF16
    if has_prev:
        in_specs += [head, head]
        args += list(prev)
    w = N_HEADS * HEAD_DIM
    blk = pltpu.VMEM((u3, hb, HEAD_DIM), F32)
    return _pcall(
        body, name="hgrn_bwd_rev" if reverse else "hgrn_bwd", grid=(N_HEADS,), in_specs=in_specs,
        out_specs=[head, head, head, lbs],
        out_shape=[_sds((tt, w), mid), _sds((tt, w), BF16), _sds((tt, w), mid), _sds((1, w), F32)],
        scratch=[pltpu.VMEM((nblk, HEAD_DIM, HEAD_DIM), F32), pltpu.VMEM((nblk, HEAD_DIM, HEAD_DIM), F32), pltpu.VMEM((nblk, HEAD_DIM), F32),
                 pltpu.VMEM((tt, HEAD_DIM), F32), blk, blk, blk, blk, blk])(*args)


def _na_geometry(rows):
    r = pl.program_id(1)
    rs = jnp.clip(r - WIN_R // 2, 0, rows - WIN_R)
    return r, rs, r - rs


def _na_scores(q, kb, kc, bias):
    scale = HEAD_DIM ** -0.5
    nt = (((1,), (1,)), ((), ()))
    sb = lax.dot_general(q, kb, nt, preferred_element_type=F32) * scale + bias
    sc = lax.dot_general(q, kc, nt, preferred_element_type=F32) * scale
    m = jnp.maximum(jnp.max(sb, axis=-1, keepdims=True), jnp.max(sc, axis=-1, keepdims=True))
    pb, pc = jnp.exp(sb - m), jnp.exp(sc - m)
    inv = 1.0 / (jnp.sum(pb, axis=-1, keepdims=True) + jnp.sum(pc, axis=-1, keepdims=True))
    return pb * inv, pc * inv


def _na_fwd(qn, kall, p, seg_v, bias, n_ctx_rows):
    t, tt = qn.shape[0], kall.shape[0]
    rows = t // GRID_W
    nband = WIN_R * GRID_W

    def body(q_ref, k_ref, v_ref, b_ref, o_ref):
        r, rs, _ = _na_geometry(rows)
        k0 = pl.multiple_of(n_ctx_rows + rs * GRID_W, GRID_W)
        q = q_ref[...]
        kb, kc = k_ref[pl.ds(k0, nband), :], k_ref[pl.ds(0, n_ctx_rows), :]
        vb, vc = v_ref[pl.ds(k0, nband), :].astype(BF16), v_ref[pl.ds(0, n_ctx_rows), :].astype(BF16)
        pb, pc = _na_scores(q, kb, kc, b_ref[...])
        o = jnp.dot(pb.astype(BF16), vb, preferred_element_type=F32) + jnp.dot(pc.astype(BF16), vc, preferred_element_type=F32)
        o_ref[...] = o.astype(o_ref.dtype)

    return _pcall(
        body, name="na_fwd", grid=(N_HEADS, rows),
        in_specs=[pl.BlockSpec((GRID_W, HEAD_DIM), lambda h, r: (r, h)),
                  pl.BlockSpec((tt, HEAD_DIM), lambda h, r: (0, h)),
                  pl.BlockSpec((tt, HEAD_DIM), lambda h, r: (0, seg_v * N_HEADS + h)),
                  pl.BlockSpec((None, None, GRID_W, nband), lambda h, r: (h, r - jnp.clip(r - WIN_R // 2, 0, rows - WIN_R), 0, 0))],
        out_specs=pl.BlockSpec((GRID_W, HEAD_DIM), lambda h, r: (r, h)),
        out_shape=_sds((t, N_HEADS * HEAD_DIM), BF16))(qn, kall, p, bias)


def _na_bwd(qn, kall, p, seg_v, bias, do, n_ctx_rows):
    t, tt = qn.shape[0], kall.shape[0]
    rows = t // GRID_W
    nband = WIN_R * GRID_W
    scale = HEAD_DIM ** -0.5
    tn = (((0,), (0,)), ((), ()))
    nt = (((1,), (1,)), ((), ()))

    def body(q_ref, k_ref, v_ref, b_ref, do_ref, dq_ref, dk_ref, dv_ref, db_ref, dv_acc):
        r, rs, var = _na_geometry(rows)
        k0 = pl.multiple_of(n_ctx_rows + rs * GRID_W, GRID_W)
        q = q_ref[...]
        kb, kc = k_ref[pl.ds(k0, nband), :], k_ref[pl.ds(0, n_ctx_rows), :]
        vb, vc = v_ref[pl.ds(k0, nband), :].astype(BF16), v_ref[pl.ds(0, n_ctx_rows), :].astype(BF16)
        pb, pc = _na_scores(q, kb, kc, b_ref[...])
        dob = do_ref[...].astype(BF16)
        o = jnp.dot(pb.astype(BF16), vb, preferred_element_type=F32) + jnp.dot(pc.astype(BF16), vc, preferred_element_type=F32)
        delta = jnp.sum(do_ref[...].astype(F32) * o, axis=-1, keepdims=True)
        dsb = pb * (lax.dot_general(dob, vb, nt, preferred_element_type=F32) - delta)
        dsc = pc * (lax.dot_general(dob, vc, nt, preferred_element_type=F32) - delta)
        dsb16, dsc16 = dsb.astype(BF16), dsc.astype(BF16)
        dq_ref[...] = (jnp.dot(dsb16, kb, preferred_element_type=F32) + jnp.dot(dsc16, kc, preferred_element_type=F32)) * scale

        @pl.when(r == 0)
        def _():
            dk_ref[...] = jnp.zeros_like(dk_ref)
            dv_acc[...] = jnp.zeros_like(dv_acc)

        dk_ref[pl.ds(k0, nband), :] += lax.dot_general(dsb16, q, tn, preferred_element_type=F32) * scale
        dk_ref[pl.ds(0, n_ctx_rows), :] += lax.dot_general(dsc16, q, tn, preferred_element_type=F32) * scale
        dv_acc[pl.ds(k0, nband), :] += lax.dot_general(pb.astype(BF16), dob, tn, preferred_element_type=F32)
        dv_acc[pl.ds(0, n_ctx_rows), :] += lax.dot_general(pc.astype(BF16), dob, tn, preferred_element_type=F32)

        @pl.when(r == rows - 1)
        def _():
            dv_ref[...] = dv_acc[...].astype(dv_ref.dtype)

        fresh = jnp.logical_or(r <= WIN_R // 2, r > rows - WIN_R // 2)

        @pl.when(fresh)
        def _():
            db_ref[...] = dsb

        @pl.when(jnp.logical_not(fresh))
        def _():
            db_ref[...] += dsb

    variant = lambda h, r: (h, r - jnp.clip(r - WIN_R // 2, 0, rows - WIN_R), 0, 0)
    head_all = pl.BlockSpec((tt, HEAD_DIM), lambda h, r: (0, h))
    qspec = pl.BlockSpec((GRID_W, HEAD_DIM), lambda h, r: (r, h))
    w = N_HEADS * HEAD_DIM
    return _pcall(
        body, name="na_bwd", grid=(N_HEADS, rows),
        in_specs=[qspec, head_all, pl.BlockSpec((tt, HEAD_DIM), lambda h, r: (0, seg_v * N_HEADS + h)),
                  pl.BlockSpec((None, None, GRID_W, nband), variant), qspec],
        out_specs=[qspec, head_all, head_all, pl.BlockSpec((None, None, GRID_W, nband), variant)],
        out_shape=[_sds((t, w), F32), _sds((tt, w), F32), _sds((tt, w), BF16), _sds((N_HEADS, WIN_R, GRID_W, nband), F32)],
        scratch=[pltpu.VMEM((tt, HEAD_DIM), F32)])(qn, kall, p, bias, do)


def _na_tables(t, n_ctx_rows):
    half, nf = HEAD_DIM // 2, HEAD_DIM // 4
    pos = np.arange(t)
    lane = np.arange(HEAD_DIM)
    inv = ROPE_THETA ** (-(np.arange(nf, dtype=np.float32)) / nf)
    which = np.where(lane < half, pos[:, None] // GRID_W, pos[:, None] % GRID_W).astype(np.float32)
    ang = which * inv[lane % nf][None, :]
    first = (lane % half) < nf
    cos = np.concatenate([np.ones((n_ctx_rows, HEAD_DIM), np.float32), np.cos(ang).astype(np.float32)])
    sin = np.concatenate([np.zeros((n_ctx_rows, HEAD_DIM), np.float32), np.where(first[None, :], -np.sin(ang), np.sin(ang)).astype(np.float32)])
    partner = np.where(first, lane + nf, lane - nf)
    perm = np.zeros((HEAD_DIM, HEAD_DIM), np.float32)
    perm[partner, lane] = 1.0
    w = np.arange(GRID_W)
    dc = np.clip(w[None, :] - w[:, None], -(WIN_C - 1), WIN_C - 1) + WIN_C - 1
    onehot = np.zeros((32, GRID_W * GRID_W), np.float32)
    onehot[dc.reshape(-1), np.arange(GRID_W * GRID_W)] = 1.0
    cs = np.clip(w - WIN_C // 2, 0, GRID_W - WIN_C)
    col_in = (w[None, :] >= cs[:, None]) & (w[None, :] < cs[:, None] + WIN_C)
    onehot *= col_in.reshape(1, -1)
    neg = np.where(col_in, 0.0, NEG_BIG).astype(np.float32)
    return jnp.asarray(cos), jnp.asarray(sin), jnp.asarray(perm), jnp.asarray(onehot), jnp.asarray(neg)


def _bias_slabs(rel_bias, onehot, neg):
    nr = 2 * WIN_R - 1
    rb = jnp.pad(rel_bias.reshape(N_HEADS * nr, 2 * WIN_C - 1), ((0, 0), (0, 1)))
    spread = _mm_f32(rb, onehot, "bias_spread").reshape(N_HEADS, nr, GRID_W, GRID_W)
    slabs = [spread[:, WIN_R - 1 - v:2 * WIN_R - 1 - v] for v in range(WIN_R)]
    b = jnp.stack(slabs, axis=1) + neg[None, None, None]
    return b.transpose(0, 1, 3, 2, 4).reshape(N_HEADS, WIN_R, GRID_W, WIN_R * GRID_W)


def _bias_grad(dbias, onehot):
    nr = 2 * WIN_R - 1
    d = dbias.reshape(N_HEADS, WIN_R, GRID_W, WIN_R, GRID_W).transpose(0, 1, 3, 2, 4)
    tot = jnp.zeros((N_HEADS, nr, GRID_W, GRID_W), F32)
    for v in range(WIN_R):
        tot = tot + jnp.pad(d[:, v], ((0, 0), (WIN_R - 1 - v, v), (0, 0), (0, 0)))
    g = _mm_f32(tot.reshape(N_HEADS * nr, GRID_W * GRID_W), onehot, "bias_grad", trans_b=True)
    return g[:, :2 * WIN_C - 1].reshape(1, N_HEADS, nr, 2 * WIN_C - 1)


def _shift_rows(u, up):
    n = u.shape[0]
    tio = lax.broadcasted_iota(jnp.int32, u.shape, 0)
    if up:
        return jnp.where(tio == n - 1, 0.0, pltpu.roll(u, n - 1, 0))
    return jnp.where(tio == 0, 0.0, pltpu.roll(u, 1, 0))


def _conv3(u, w_ref, b_ref):
    um, up = _shift_rows(u, False), _shift_rows(u, True)
    return um, up, um * w_ref[0:1, :] + u * w_ref[1:2, :] + up * w_ref[2:3, :] + b_ref[...]


def _ffn_act_fwd(u0, t3, cw, cb):
    t, n = u0.shape
    tc = _pick(n, (256, 128))

    def body(u_ref, t_ref, w_ref, b_ref, a_ref):
        _, _, uc = _conv3(u_ref[...], w_ref, b_ref)
        a_ref[...] = (uc * jax.nn.sigmoid(uc) * t_ref[...]).astype(a_ref.dtype)

    col = lambda rows_: pl.BlockSpec((rows_, tc), lambda j: (0, j))
    return _pcall(body, name="ffn_act_fwd", grid=(n // tc,), in_specs=[col(t), col(t), col(8), col(1)], out_specs=col(t),
                  out_shape=_sds((t, n), BF16))(u0, t3, cw, cb)


def _ffn_act_bwd(u0, t3, cw, cb, da):
    t, n = u0.shape
    tc = _pick(n, (256, 128))

    def body(u_ref, t_ref, w_ref, b_ref, da_ref, du_ref, dt_ref, dw_ref, db_ref):
        u = u_ref[...]
        um, up, uc = _conv3(u, w_ref, b_ref)
        sg = jax.nn.sigmoid(uc)
        dav = da_ref[...].astype(F32)
        dt_ref[...] = (dav * uc * sg).astype(dt_ref.dtype)
        duc = dav * t_ref[...] * sg * (1.0 + uc * (1.0 - sg))
        du = _shift_rows(duc, True) * w_ref[0:1, :] + duc * w_ref[1:2, :] + _shift_rows(duc, False) * w_ref[2:3, :]
        du_ref[...] = du.astype(du_ref.dtype)
        dw_ref[...] = jnp.zeros_like(dw_ref)
        dw_ref[0:1, :] = jnp.sum(duc * um, axis=0, keepdims=True)
        dw_ref[1:2, :] = jnp.sum(duc * u, axis=0, keepdims=True)
        dw_ref[2:3, :] = jnp.sum(duc * up, axis=0, keepdims=True)
        db_ref[...] = jnp.sum(duc, axis=0, keepdims=True)

    col = lambda rows_: pl.BlockSpec((rows_, tc), lambda j: (0, j))
    return _pcall(body, name="ffn_act_bwd", grid=(n // tc,), in_specs=[col(t), col(t), col(8), col(1), col(t)],
                  out_specs=[col(t), col(t), col(8), col(1)],
                  out_shape=[_sds((t, n), BF16), _sds((t, n), BF16), _sds((8, n), F32), _sds((1, n), F32)])(u0, t3, cw, cb, da)


def _adam_math(g, w, m, v):
    m2 = ADAM_B1 * m + (1.0 - ADAM_B1) * g
    v2 = ADAM_B2 * v + (1.0 - ADAM_B2) * (g * g)
    m_hat = m2 / (1.0 - ADAM_B1 ** ADAM_STEP)
    v_hat = v2 / (1.0 - ADAM_B2 ** ADAM_STEP)
    return -ADAM_LR * (m_hat / (jnp.sqrt(v_hat) + ADAM_EPS) + ADAM_WD * w), m2, v2


def _adam_big(parts, w, m, v, name):
    r, c = w.shape
    npart, _, cp = parts.shape
    tr = _pick(r, (64, 32, 16, 8))

    def body(p_ref, w_ref, m_ref, v_ref, g_ref, d_ref, m2_ref, v2_ref):
        g = p_ref[0, :, 0:c].astype(F32)
        for i in range(1, npart):
            g = g + p_ref[i, :, 0:c].astype(F32)
        d, m2, v2 = _adam_math(g, w_ref[...], m_ref[...], v_ref[...])
        g_ref[...] = g
        d_ref[...] = d
        m2_ref[...] = m2
        v2_ref[...] = v2

    row = pl.BlockSpec((tr, c), lambda i: (i, 0))
    return _pcall(body, name=name, grid=(r // tr,),
                  in_specs=[pl.BlockSpec((npart, tr, cp), lambda i: (0, i, 0)), row, row, row],
                  out_specs=[row] * 4, out_shape=[_sds((r, c), F32)] * 4)(parts, w, m, v)


def _adam_small(g, w, m, v):
    def body(g_ref, w_ref, m_ref, v_ref, d_ref, m2_ref, v2_ref):
        d_ref[...], m2_ref[...], v2_ref[...] = _adam_math(g_ref[...], w_ref[...], m_ref[...], v_ref[...])

    return _pcall(body, name="adam_small", out_shape=[_sds(g.shape, F32)] * 3)(g, w, m, v)


def _sum_parts(parts, name):
    def body(p_ref, o_ref):
        s = p_ref[0]
        for i in range(1, N_DEV):
            s = s + p_ref[i]
        o_ref[...] = s

    return _pcall(body, name=name, out_shape=_sds(parts.shape[1:], F32))(parts)


class _Pack:
    def __init__(self, shapes):
        self.shapes = shapes
        self.sizes = [int(np.prod(s)) for s in shapes]
        self.padded = [-(-n // (8 * LANE)) * 8 * LANE for n in self.sizes]
        self.offs = np.concatenate([[0], np.cumsum(self.padded)]).tolist()

    def pack(self, arrs):
        flat = [jnp.pad(a.reshape(-1).astype(F32), (0, p - n)) for a, n, p in zip(arrs, self.sizes, self.padded)]
        return jnp.concatenate(flat).reshape(-1, LANE)

    def unpack(self, slab):
        flat = slab.reshape(-1)
        return [flat[o:o + n].reshape(s) for o, n, s in zip(self.offs, self.sizes, self.shapes)]


def kernel(x, c, ctx, c_ctx, ada_w, ada_b, norm1_g, norm2_g, w_in, hgrn_lb_logits, hgrn_norm_g, na_q_norm_g, na_k_norm_g, na_rel_bias, w_branch_a, w_branch_b, w_out, ffn_w1, ffn_w3, ffn_conv_w, ffn_conv_b, ffn_w2, loss_target, m_c_ctx, m_ada_w, m_ada_b, m_norm1_g, m_norm2_g, m_w_in, m_hgrn_lb_logits, m_hgrn_norm_g, m_na_q_norm_g, m_na_k_norm_g, m_na_rel_bias, m_w_branch_a, m_w_branch_b, m_w_out, m_ffn_w1, m_ffn_w3, m_ffn_conv_w, m_ffn_conv_b, m_ffn_w2, v_c_ctx, v_ada_w, v_ada_b, v_norm1_g, v_norm2_g, v_w_in, v_hgrn_lb_logits, v_hgrn_norm_g, v_na_q_norm_g, v_na_k_norm_g, v_na_rel_bias, v_w_branch_a, v_w_branch_b, v_w_out, v_ffn_w1, v_ffn_w3, v_ffn_conv_w, v_ffn_conv_b, v_ffn_w2):
    t, d = x.shape[1], x.shape[2]
    n_ctx = ctx.shape[1]
    tt = n_ctx + t
    hw = N_HEADS * HEAD_DIM
    ci = w_in.shape[2]
    ca = ada_w.shape[2]
    ff_l = ffn_w1.shape[2]
    ff_p = -(-ff_l // LANE) * LANE
    rows = t // GRID_W
    assert rows >= WIN_R and t % GRID_W == 0 and n_ctx % GRID_W == 0 and ci % LANE == 0 and d % LANE == 0
    me = 4 * lax.axis_index("x") + 2 * lax.axis_index("y") + lax.axis_index("c")
    tr = _pick(n_ctx, (256, 128, 64))
    n_ctx_tiles = n_ctx // tr

    pad_c = lambda w: jnp.pad(w, ((0, 0), (0, ff_p - ff_l)))
    small_in = [c, hgrn_lb_logits.reshape(4, HEAD_DIM), jnp.pad(ffn_conv_w[0], ((0, 5), (0, ff_p - ff_l)))]
    c_all, lb_parts, cw_all = _exchange(small_in, "gather_params", scatter=False)
    ff = N_DEV * ff_p

    cc = jnp.concatenate([c_all.reshape(N_DEV, d), jnp.broadcast_to(c_ctx[None, :], (N_DEV, d))], axis=0)
    act = _tiled(lambda v: v * jax.nn.sigmoid(v), [_const(cc)], [(cc.shape, BF16, cc.shape, lambda *_: (0, 0))], (), "silu_c")[0]
    ada16 = _cast_bf16(ada_w[0], "cast_ada")
    mod_cols = _mm_xw(act, ada16.reshape(1, d, ca), F32, "ada_fwd")
    mod_all = _exchange([mod_cols], "gather_mod", scatter=False)[0]

    with _after(mod_all):
        w_in16 = _cast_bf16(w_in[0], "cast_w_in")
    fly_in = _exchange_start([w_in16], "gather_w_in_start", scatter=False)
    rest16 = []
    for w_, nm in ((w_branch_a[0], "cast_w_a"), (w_branch_b[0], "cast_w_b"), (w_out[0], "cast_w_out"), (pad_c(ffn_w1[0]), "cast_w1"),
                   (pad_c(ffn_w3[0]), "cast_w3"), (jnp.pad(ffn_w2[0], ((0, ff_p - ff_l), (0, 0))), "cast_w2")):
        with _after(fly_in.token):
            rest16.append(_cast_bf16(w_, nm))
    fly_rest = _exchange_start(rest16, "gather_rest_start", scatter=False)
    mod_all = mod_all.transpose(1, 0, 2).reshape(2 * N_DEV, N_MOD * d) + ada_b
    mod_l = lax.dynamic_slice_in_dim(mod_all, me, 1, axis=0).reshape(N_MOD, 1, d)
    mod_c = mod_all[N_DEV:N_DEV + 1].reshape(N_MOD, 1, d)
    mods1 = jnp.stack([mod_c[0:2], mod_l[0:2]])
    mods2 = jnp.stack([mod_l[3:5], mod_l[3:5]])
    gate1, gate2 = mod_l[2], mod_l[5]

    xcat = jnp.concatenate([ctx[0], x[0]], axis=0)
    with _after(fly_rest.token):
        hcat = _norm_mod_fwd(xcat, norm1_g, mods1, n_ctx_tiles, tr)
    g_in, = _exchange_wait(fly_in, [hcat], "gather_w_in_wait")
    p = _mm_xw(hcat, g_in, F32, "in_proj")
    lb_logits = lb_parts.transpose(1, 0, 2).reshape(2, 2, hw)
    lb_soft = _tiled(lambda a, b: (1.0 / (1.0 + jnp.exp(b - a)),), [_const(lb_logits[:, 0]), _const(lb_logits[:, 1])],
                     [((2, hw), F32, (2, hw), lambda *_: (0, 0))], (), "lb_softmax")[0]
    lb_f, lb_b = lb_soft[0:1], lb_soft[1:2]
    o_f = _hgrn_fwd(p, lb_f, 1, False, n_ctx)
    o_b = _hgrn_fwd(p, lb_b, 2, True, n_ctx)

    cos, sin, perm, onehot, neg = _na_tables(t, n_ctx)
    bias = _bias_slabs(na_rel_bias[0], onehot, neg)
    hcol = lambda seg, off=0: (lambda i, h: (i + off, seg * N_HEADS + h))
    tq = GRID_W * 2
    lat0 = n_ctx // tq
    qk_fn = lambda tv, g, cs, sn, pm: (_f_qk(tv, g, cs, sn, pm),)
    tab = lambda a, off=0: (a, (tq, HEAD_DIM), lambda i, h: (i + off, 0))
    qn = _tiled(qk_fn, [(p, (tq, HEAD_DIM), hcol(5, lat0)), _const(na_q_norm_g), tab(cos, lat0), tab(sin, lat0), _const(perm)],
                [((t, hw), BF16, (tq, HEAD_DIM), lambda i, h: (i, h))], (t // tq, N_HEADS), "q_norm_rope")[0]
    kall = _tiled(qk_fn, [(p, (tq, HEAD_DIM), hcol(6)), _const(na_k_norm_g), tab(cos), tab(sin), _const(perm)],
                  [((tt, hw), BF16, (tq, HEAD_DIM), lambda i, h: (i, h))], (tt // tq, N_HEADS), "k_norm_rope")[0]
    y_b = _na_fwd(qn, kall, p, 7, bias, n_ctx)

    trh = tr
    lat_h = n_ctx // trh
    ospec = lambda a: (a, (trh, HEAD_DIM), lambda i, h: (i + lat_h, h))
    y_a = _tiled(lambda a, b, gt, g: (_f_readout(a, b, gt, g),),
                 [ospec(o_f), ospec(o_b), (p, (trh, HEAD_DIM), hcol(4, lat_h)), _const(hgrn_norm_g)],
                 [((t, hw), BF16, (trh, HEAD_DIM), lambda i, h: (i, h))], (t // trh, N_HEADS), "hgrn_readout")[0]

    g_a, g_b, g_out, g_w1, g_w3, g_w2 = _exchange_wait(fly_rest, [y_a, y_b], "gather_rest_wait")
    g_out = g_out.reshape(1, d, d)
    g_w2 = g_w2.reshape(1, ff, d)
    p_a = _mm_xw(y_a, g_a, F32, "branch_a")
    p_b = _mm_xw(y_b, g_b, F32, "branch_b")
    td = _pick(d, (512, 256, 128))
    nd_t = d // td
    lat_r = n_ctx // tr
    gcol = lambda k: (p, (tr, td), lambda i, j, k=k: (i + lat_r, 8 * hw // td + k * nd_t + j))
    dtile = lambda a: (a, (tr, td), lambda i, j: (i, j))
    z = _tiled(lambda ga, gb, pa, pb: (_f_merge(ga, gb, pa, pb),), [gcol(0), gcol(1), dtile(p_a), dtile(p_b)],
               [((t, d), BF16, (tr, td), lambda i, j: (i, j))], (t // tr, nd_t), "merge")[0]
    mix = _mm_xw(z, g_out, F32, "out_proj")
    xl = x[0]
    x_mid = _tiled(lambda a, g, m_: (_f_resid(a, g, m_),), [_rows(xl, tr), _const(gate1), _rows(mix, tr)],
                   [((t, d), F32, (tr, d), lambda i: (i, 0))], (t // tr,), "resid1")[0]

    h2 = _norm_mod_fwd(x_mid, norm2_g, mods2, 0, tr)
    u0 = _mm_xw(h2, g_w1, F32, "ffn_up1")
    t3 = _mm_xw(h2, g_w3, F32, "ffn_up3")
    cw_full = cw_all.transpose(1, 0, 2).reshape(8, ff)
    cb_full = jnp.pad(ffn_conv_b.reshape(N_DEV, ff_l), ((0, 0), (0, ff_p - ff_l))).reshape(1, ff)
    a_act = _ffn_act_fwd(u0, t3, cw_full, cb_full)
    f_out = _mm_xw(a_act, g_w2, F32, "ffn_down")

    def loss_fn(xm, g, f, tg):
        err = xm + g * f - tg
        return err * (1.0 / d), jnp.sum(err * err, axis=0, keepdims=True) * (0.5 / d), jnp.sum(err * (1.0 / d) * f, axis=0, keepdims=True)

    dy, loss_cols, d_gate2 = _tiled(loss_fn, [_rows(x_mid, tr), _const(gate2), _rows(f_out, tr), _rows(loss_target[0], tr)],
                                    [((t, d), F32, (tr, d), lambda i: (i, 0)), ((1, d), F32, (1, d), lambda i: (0, 0)),
                                     ((1, d), F32, (1, d), lambda i: (0, 0))], (t // tr,), "loss", acc=(1, 2))
    loss = lax.psum(jnp.sum(loss_cols), ("x", "y", "c"))

    df = _tiled(lambda a, g: (a * g,), [_rows(dy, tr), _const(gate2)], [((t, d), BF16, (tr, d), lambda i: (i, 0))], (t // tr,), "d_ffn_out")[0]
    d_w2 = _mm_xtdy(a_act, df, 1, BF16, "d_w2")
    da = _mm_dyw(df, g_w2, BF16, "d_act")
    du0, dt3, d_cw, d_cb = _ffn_act_bwd(u0, t3, cw_full, cb_full, da)
    d_w1 = _mm_xtdy(h2, du0, N_DEV, BF16, "d_w1")
    d_w3 = _mm_xtdy(h2, dt3, N_DEV, BF16, "d_w3")
    fly_ffn = _exchange_start([d_w1, d_w3, d_w2.reshape(N_DEV, ff_p, d)], "scatter_ffn_start", scatter=True)
    with _after(fly_ffn.token):
        dh2 = _mm_dyw(du0, g_w1, F32, "d_h2_a")
    dh2 = _mm_dyw(dt3, g_w3, F32, "d_h2_b", init=dh2)
    dx_mid, d_norm2, d_mods2 = _norm_mod_bwd(x_mid, norm2_g, mods2, dh2, dy, 0, tr, "norm_mod2_bwd")

    dm, d_gate1 = _tiled(lambda dxm, g, m_: (dxm * g, jnp.sum(dxm * m_, axis=0, keepdims=True)),
                         [_rows(dx_mid, tr), _const(gate1), _rows(mix, tr)],
                         [((t, d), BF16, (tr, d), lambda i: (i, 0)), ((1, d), F32, (1, d), lambda i: (0, 0))], (t // tr,), "d_resid1", acc=(1,))
    d_wout = _mm_xtdy(z, dm, 1, BF16, "d_w_out")
    dz = _mm_dyw(dm, g_out, F32, "d_merge")

    def merge_bwd(ga, gb, pa, pb, dzv):
        _, vjp = jax.vjp(_f_merge, ga, gb, pa, pb)
        return vjp(dzv)

    dga, dgb, dpa, dpb = _tiled(merge_bwd, [gcol(0), gcol(1), dtile(p_a), dtile(p_b), dtile(dz)],
                                [((t, d), BF16, (tr, td), lambda i, j: (i, j))] * 4, (t // tr, nd_t), "merge_bwd")
    d_wa = _mm_xtdy(y_a, dpa, N_DEV, BF16, "d_w_a")
    d_wb = _mm_xtdy(y_b, dpb, N_DEV, BF16, "d_w_b")
    fly_mix = _exchange_start([d_wa, d_wb, d_wout.reshape(N_DEV, d // N_DEV, d)], "scatter_mix_start", scatter=True)
    with _after(fly_mix.token):
        dy_a = _mm_dyw(dpa, g_a, F32, "d_y_a")
    dy_b = _mm_dyw(dpb, g_b, BF16, "d_y_b")

    def readout_bwd(a, b, gt, g, ct):
        _, vjp = jax.vjp(_f_readout, a, b, gt, g)
        da_, _, dgt, dg = vjp(ct)
        return da_, dgt, dg

    hsp = lambda dt: ((t, hw), dt, (trh, HEAD_DIM), lambda i, h: (i, h))
    do_h, d_gate_o, d_hnorm = _tiled(
        readout_bwd, [ospec(o_f), ospec(o_b), (p, (trh, HEAD_DIM), hcol(4, lat_h)), _const(hgrn_norm_g), (dy_a, (trh, HEAD_DIM), lambda i, h: (i, h))],
        [hsp(F32), hsp(BF16), ((1, HEAD_DIM), F32, (1, HEAD_DIM), lambda i, h: (0, 0))], (t // trh, N_HEADS), "readout_bwd", acc=(2,))
    dq1, dfl_f, dv1, dlb_f = _hgrn_bwd(p, lb_f, do_h, 1, False, n_ctx, None)
    dq_h, dfl_b, dv_h, dlb_b = _hgrn_bwd(p, lb_b, do_h, 2, True, n_ctx, (dq1, dv1))

    dqn, dkall, dv_na, dbias = _na_bwd(qn, kall, p, 7, bias, dy_b, n_ctx)

    def qk_bwd(tv, g, cs, sn, pm, ct):
        _, vjp = jax.vjp(lambda a, b: _f_qk(a, b, cs, sn, pm), tv, g)
        return vjp(ct)

    d_pq, d_qnorm = _tiled(qk_bwd, [(p, (tq, HEAD_DIM), hcol(5, lat0)), _const(na_q_norm_g), tab(cos, lat0), tab(sin, lat0), _const(perm),
                                    (dqn, (tq, HEAD_DIM), lambda i, h: (i, h))],
                           [((t, hw), BF16, (tq, HEAD_DIM), lambda i, h: (i, h)), ((1, HEAD_DIM), F32, (1, HEAD_DIM), lambda i, h: (0, 0))],
                           (t // tq, N_HEADS), "q_norm_rope_bwd", acc=(1,))
    d_pk, d_knorm = _tiled(qk_bwd, [(p, (tq, HEAD_DIM), hcol(6)), _const(na_k_norm_g), tab(cos), tab(sin), _const(perm),
                                    (dkall, (tq, HEAD_DIM), lambda i, h: (i, h))],
                           [((tt, hw), BF16, (tq, HEAD_DIM), lambda i, h: (i, h)), ((1, HEAD_DIM), F32, (1, HEAD_DIM), lambda i, h: (0, 0))],
                           (tt // tq, N_HEADS), "k_norm_rope_bwd", acc=(1,))

    zc = lambda w_: jnp.zeros((n_ctx, w_), BF16)
    lat_only = lambda a: jnp.concatenate([zc(a.shape[1]), a], axis=0)
    dp = jnp.concatenate([dq_h, dfl_f, dfl_b, dv_h, lat_only(d_gate_o), lat_only(d_pq), d_pk, dv_na, lat_only(dga), lat_only(dgb)], axis=1)
    dhcat = _mm_dyw(dp, g_in, BF16, "d_hcat")
    zero_ctx = jnp.concatenate([jnp.zeros((n_ctx, d), F32), dx_mid], axis=0)
    dxcat, d_norm1, d_mods1 = _norm_mod_bwd(xcat, norm1_g, mods1, dhcat, zero_ctx, n_ctx_tiles, tr, "norm_mod1_bwd")
    grad_x = dxcat[n_ctx:][None]

    zd = jnp.zeros((1, d), F32)
    dmod_l = jnp.concatenate([d_mods1[1, 0], d_mods1[1, 1], d_gate1, d_mods2[1, 0], d_mods2[1, 1], d_gate2], axis=1)
    dmod_c = jnp.concatenate([d_mods1[0, 0], d_mods1[0, 1], zd, zd, zd, zd], axis=1)
    dmods = jnp.concatenate([dmod_l, dmod_c], axis=0).reshape(2, N_DEV, ca).transpose(1, 0, 2)
    dmods = jnp.pad(dmods, ((0, 0), (0, 6), (0, 0)))
    got = _exchange([dmods], "scatter_dmod", scatter=True)[0]
    dm_rows = jnp.concatenate([got[:, 0], got[:, 1]], axis=0)
    d_ada = _mm_xtdy(act, dm_rows, 1, F32, "d_ada_w")[0]
    back = _mm_dyw(dm_rows, ada16.reshape(1, d, ca), F32, "d_silu_c")
    d_cctx_part = _tiled(lambda b, v: (jnp.sum(b[N_DEV:], axis=0, keepdims=True) * (jax.nn.sigmoid(v) * (1.0 + v * (1.0 - jax.nn.sigmoid(v)))),),
                         [_const(back), _const(c_ctx.reshape(1, d))], [((1, d), F32, (1, d), lambda *_: (0, 0))], (), "d_c_ctx")[0]

    d_rel = _bias_grad(dbias, onehot)
    d_lb_soft = jnp.concatenate([dlb_f, dlb_b], axis=0)
    d_lb0 = _tiled(lambda s, g: (g * s * (1.0 - s),), [_const(lb_soft), _const(d_lb_soft)], [((2, hw), F32, (2, hw), lambda *_: (0, 0))], (), "d_lb")[0]
    d_lb_full = jnp.stack([d_lb0, -d_lb0], axis=1)
    d_cw_l = d_cw[:3].reshape(3, N_DEV, ff_p)[:, :, :ff_l].reshape(1, 3, N_DEV * ff_l)
    d_cb_l = d_cb.reshape(N_DEV, ff_p)[:, :ff_l].reshape(1, N_DEV * ff_l)
    small = [d_cctx_part.reshape(d), (dmod_l + dmod_c), d_norm1, d_norm2, d_lb_full, d_hnorm, d_qnorm, d_knorm, d_rel, d_cw_l, d_cb_l]
    pk = _Pack([a.shape for a in small])
    tot = _sum_parts(_exchange([pk.pack(small)], "gather_small", scatter=False)[0], "sum_small")
    g_cctx, g_ada_b, g_n1, g_n2, g_lb, g_hn, g_qn, g_kn, g_rel, g_cw, g_cb = pk.unpack(tot)
    g_lb = lax.dynamic_slice_in_dim(g_lb, me * HEAD_DIM, HEAD_DIM, axis=2)
    g_cw = lax.dynamic_slice_in_dim(g_cw, me * ff_l, ff_l, axis=2)
    small_names = [("c_ctx", g_cctx, c_ctx, m_c_ctx, v_c_ctx), ("ada_b", g_ada_b, ada_b, m_ada_b, v_ada_b),
                   ("norm1_g", g_n1, norm1_g, m_norm1_g, v_norm1_g), ("norm2_g", g_n2, norm2_g, m_norm2_g, v_norm2_g),
                   ("hgrn_lb_logits", g_lb, hgrn_lb_logits, m_hgrn_lb_logits, v_hgrn_lb_logits),
                   ("hgrn_norm_g", g_hn, hgrn_norm_g, m_hgrn_norm_g, v_hgrn_norm_g), ("na_q_norm_g", g_qn, na_q_norm_g, m_na_q_norm_g, v_na_q_norm_g),
                   ("na_k_norm_g", g_kn, na_k_norm_g, m_na_k_norm_g, v_na_k_norm_g), ("na_rel_bias", g_rel, na_rel_bias, m_na_rel_bias, v_na_rel_bias),
                   ("ffn_conv_w", g_cw, ffn_conv_w, m_ffn_conv_w, v_ffn_conv_w), ("ffn_conv_b", g_cb, ffn_conv_b, m_ffn_conv_b, v_ffn_conv_b)]
    pk2 = _Pack([s[1].shape for s in small_names])
    sd, sm, sv = _adam_small(*[pk2.pack([s[i] for s in small_names]) for i in (1, 2, 3, 4)])
    sd, sm, sv = pk2.unpack(sd), pk2.unpack(sm), pk2.unpack(sv)
    res = {s[0]: (s[1], sd[i], sm[i], sv[i]) for i, s in enumerate(small_names)}

    with _after(sd[0]):
        d_win = _mm_xtdy(hcat, dp, N_DEV, BF16, "d_w_in")
    fly_win = _exchange_start([d_win], "scatter_w_in_start", scatter=True)
    with _after(fly_win.token):
        res["ada_w"] = _adam_big(d_ada[None], ada_w[0], m_ada_w[0], v_ada_w[0], "adam_ada")
    r_w1, r_w3, r_w2 = _exchange_wait(fly_ffn, [res["ada_w"][1]], "scatter_ffn_wait")
    res["ffn_w1"] = _adam_big(r_w1, ffn_w1[0], m_ffn_w1[0], v_ffn_w1[0], "adam_w1")
    res["ffn_w3"] = _adam_big(r_w3, ffn_w3[0], m_ffn_w3[0], v_ffn_w3[0], "adam_w3")
    res["ffn_w2"] = _adam_big(r_w2, ffn_w2[0], m_ffn_w2[0], v_ffn_w2[0], "adam_w2")
    r_a, r_b, r_out = _exchange_wait(fly_mix, [res["ffn_w2"][1]], "scatter_mix_wait")
    res["w_branch_a"] = _adam_big(r_a, w_branch_a[0], m_w_branch_a[0], v_w_branch_a[0], "adam_w_a")
    res["w_branch_b"] = _adam_big(r_b, w_branch_b[0], m_w_branch_b[0], v_w_branch_b[0], "adam_w_b")
    res["w_out"] = _adam_big(r_out, w_out[0], m_w_out[0], v_w_out[0], "adam_w_out")
    r_in, = _exchange_wait(fly_win, [res["w_out"][1]], "scatter_w_in_wait")
    res["w_in"] = _adam_big(r_in, w_in[0], m_w_in[0], v_w_in[0], "adam_w_in")
    for k in ("w_in", "w_branch_a", "w_branch_b", "w_out", "ffn_w1", "ffn_w3", "ffn_w2", "ada_w"):
        res[k] = tuple(a[None] for a in res[k])

    order = ["c_ctx", "ada_w", "ada_b", "norm1_g", "norm2_g", "w_in", "hgrn_lb_logits", "hgrn_norm_g", "na_q_norm_g", "na_k_norm_g",
             "na_rel_bias", "w_branch_a", "w_branch_b", "w_out", "ffn_w1", "ffn_w3", "ffn_conv_w", "ffn_conv_b", "ffn_w2"]
    shapes = {"c_ctx": c_ctx.shape, "ada_b": ada_b.shape, "norm1_g": norm1_g.shape, "norm2_g": norm2_g.shape,
              "hgrn_lb_logits": hgrn_lb_logits.shape, "hgrn_norm_g": hgrn_norm_g.shape, "na_q_norm_g": na_q_norm_g.shape,
              "na_k_norm_g": na_k_norm_g.shape, "na_rel_bias": na_rel_bias.shape, "ffn_conv_w": ffn_conv_w.shape, "ffn_conv_b": ffn_conv_b.shape}
    outs = [loss, grad_x]
    for part in range(4):
        for k in order:
            a = res[k][part]
            outs.append(a.reshape(shapes[k]) if k in shapes else a)
    return tuple(outs)
```

```python
import functools

import numpy as np
import jax
import jax.numpy as jnp
from jax import lax
from jax.experimental import pallas as pl
from jax.experimental.pallas import tpu as pltpu

F32 = jnp.float32
BF16 = jnp.bfloat16
HIGHEST = lax.Precision.HIGHEST

N_DEV = 8
MESH_ID = pl.DeviceIdType.MESH
LANE = 128
HEAD_DIM = 128
N_HEADS = 8
GRID_W = 64
WIN_R = 8
WIN_C = 16
ROPE_THETA = 10000.0
EPS = 1e-6
N_MOD = 6
HGRN_BLOCK = 16
NEG_BIG = -1e30
VMEM_LIMIT = 56 << 20

ADAM_LR = 0.001
ADAM_B1 = 0.9
ADAM_B2 = 0.999
ADAM_EPS = 1e-08
ADAM_WD = 0.01
ADAM_STEP = 10

HBM_SPEC = pl.BlockSpec(memory_space=pltpu.HBM)


_ORDER_AFTER = []


class _after:
    def __init__(self, *arrs):
        self.arrs = list(arrs)

    def __enter__(self):
        _ORDER_AFTER.extend(self.arrs)

    def __exit__(self, *exc):
        del _ORDER_AFTER[:]


def _pcall(body, *, name, out_shape, grid=None, in_specs=None, out_specs=None, scratch=(), aliases=None):
    kw = {}
    if grid is not None:
        kw["grid"] = grid
    extra = []
    if _ORDER_AFTER and in_specs is not None:
        extra = list(_ORDER_AFTER)
        del _ORDER_AFTER[:]
        n_in, n_extra, inner = len(in_specs), len(extra), body
        in_specs = list(in_specs) + [pl.BlockSpec(memory_space=pl.ANY)] * n_extra

        def body(*refs):
            return inner(*refs[:n_in], *refs[n_in + n_extra:])

    if extra:
        call = _pcall_inner(body, name, out_shape, kw, in_specs, out_specs, scratch, aliases)
        return lambda *args: call(*args, *extra)
    return _pcall_inner(body, name, out_shape, kw, in_specs, out_specs, scratch, aliases)


def _pcall_inner(body, name, out_shape, kw, in_specs, out_specs, scratch, aliases):
    if in_specs is not None:
        kw["in_specs"] = in_specs
    if out_specs is not None:
        kw["out_specs"] = out_specs
    if scratch:
        kw["scratch_shapes"] = list(scratch)
    if aliases:
        kw["input_output_aliases"] = aliases
    return pl.pallas_call(body, name=name, out_shape=out_shape,
                          compiler_params=pltpu.CompilerParams(vmem_limit_bytes=VMEM_LIMIT), **kw)


def _pick(dim, cands):
    for c in cands:
        if c <= dim and dim % c == 0:
            return c
    return dim


def _sds(shape, dtype):
    return jax.ShapeDtypeStruct(tuple(shape), dtype)


def _peers():
    x, y, c = lax.axis_index("x"), lax.axis_index("y"), lax.axis_index("c")
    out = []
    for k in range(1, N_DEV):
        px = 1 - x if (k >> 2) & 1 else x
        py = 1 - y if (k >> 1) & 1 else y
        pc = 1 - c if k & 1 else c
        out.append((k, (px, py, pc), 4 * px + 2 * py + pc))
    return 4 * x + 2 * y + c, out


def _exchange(arrs, name, scatter):
    n = len(arrs)

    def body(*refs):
        ins, outs = refs[:n], refs[n:2 * n]
        send, recv, loc = refs[2 * n:]
        me, peers = _peers()
        started = []
        for i in range(n):
            src = ins[i].at[me] if scatter else ins[i]
            cp = pltpu.make_async_copy(src, outs[i].at[me], loc.at[i])
            cp.start()
            started.append(cp)
        sends = []
        for k, peer, pidx in peers:
            for i in range(n):
                src = ins[i].at[pidx] if scatter else ins[i]
                cp = pltpu.make_async_remote_copy(src_ref=src, dst_ref=outs[i].at[me], send_sem=send.at[i * 7 + k - 1],
                                                  recv_sem=recv.at[i * 7 + k - 1], device_id=peer, device_id_type=MESH_ID)
                cp.start()
                sends.append(cp)
        for k, peer, pidx in peers:
            for i in range(n):
                src = ins[i].at[pidx] if scatter else ins[i]
                pltpu.make_async_remote_copy(src_ref=src, dst_ref=outs[i].at[pidx], send_sem=send.at[i * 7 + k - 1],
                                             recv_sem=recv.at[i * 7 + k - 1], device_id=peer, device_id_type=MESH_ID).wait_recv()
        for cp in sends:
            cp.wait_send()
        for cp in started:
            cp.wait()

    out_shape = [_sds(a.shape if scatter else (N_DEV,) + a.shape, a.dtype) for a in arrs]
    res = _pcall(body, name=name, out_shape=out_shape, in_specs=[HBM_SPEC] * n, out_specs=[HBM_SPEC] * n,
                 scratch=[pltpu.SemaphoreType.DMA((7 * n,)), pltpu.SemaphoreType.DMA((7 * n,)), pltpu.SemaphoreType.DMA((n,))])(*arrs)
    return list(res)


class _InFlight:
    def __init__(self, send, recv, srcs, lands, token, scatter):
        self.send, self.recv, self.srcs, self.lands, self.token, self.scatter = send, recv, srcs, lands, token, scatter


SEM_SPEC = pl.BlockSpec(memory_space=pltpu.SEMAPHORE)
SIDE_EFFECT = pltpu.SideEffectType.DATAFLOW_SIDE_EFFECTING


def _exchange_start(arrs, name, scatter):
    n = len(arrs)

    def body(*refs):
        ins, lands = refs[:n], refs[n:2 * n]
        send, recv = refs[2 * n], refs[2 * n + 1]
        token = refs[4 * n + 2]
        loc = refs[4 * n + 3]
        me, peers = _peers()
        own = [pltpu.make_async_copy(ins[i].at[me] if scatter else ins[i], lands[i].at[me], loc.at[i]) for i in range(n)]
        for cp in own:
            cp.start()
        for cp in own:
            cp.wait()
        for k, peer, pidx in peers:
            for i in range(n):
                src = ins[i].at[pidx] if scatter else ins[i]
                pltpu.make_async_remote_copy(src_ref=src, dst_ref=lands[i].at[me], send_sem=send.at[i * 7 + k - 1],
                                             recv_sem=recv.at[i * 7 + k - 1], device_id=peer, device_id_type=MESH_ID).start()
        token[...] = jnp.zeros_like(token)

    land_shapes = [a.shape if scatter else (N_DEV,) + a.shape for a in arrs]
    hbm = lambda a: pltpu.with_memory_space_constraint(a, pltpu.HBM)
    args = [hbm(a) for a in arrs] + [hbm(lax.empty(s, a.dtype)) for s, a in zip(land_shapes, arrs)]
    out_shape = ([pltpu.SemaphoreType.DMA((7 * n,)), pltpu.SemaphoreType.DMA((7 * n,))]
                 + [pltpu.HBM(a.shape, a.dtype) for a in arrs] + [pltpu.HBM(s, a.dtype) for s, a in zip(land_shapes, arrs)]
                 + [_sds((8, LANE), F32)])
    res = pl.pallas_call(
        body, name=name, out_shape=out_shape, in_specs=[HBM_SPEC] * (2 * n),
        out_specs=[SEM_SPEC, SEM_SPEC] + [HBM_SPEC] * (2 * n) + [pl.BlockSpec(memory_space=pltpu.VMEM)],
        input_output_aliases={i: 2 + i for i in range(2 * n)},
        scratch_shapes=[pltpu.SemaphoreType.DMA((n,))],
        compiler_params=pltpu.CompilerParams(has_side_effects=SIDE_EFFECT))(*args)
    return _InFlight(res[0], res[1], list(res[2:2 + n]), list(res[2 + n:2 + 2 * n]), res[2 + 2 * n], scatter)


def _exchange_wait(h, after, name):
    n = len(h.srcs)
    scatter = h.scatter
    after = list(after)

    def body(*refs):
        ins, lands = refs[:n], refs[n:2 * n]
        send, recv = refs[2 * n], refs[2 * n + 1]
        _, peers = _peers()
        for k, peer, pidx in peers:
            for i in range(n):
                src = ins[i].at[pidx] if scatter else ins[i]
                cp = pltpu.make_async_remote_copy(src_ref=src, dst_ref=lands[i].at[pidx], send_sem=send.at[i * 7 + k - 1],
                                                  recv_sem=recv.at[i * 7 + k - 1], device_id=peer, device_id_type=MESH_ID)
                cp.wait_send()
                cp.wait_recv()

    res = pl.pallas_call(
        body, name=name, out_shape=[pltpu.HBM(a.shape, a.dtype) for a in h.srcs + h.lands],
        in_specs=[HBM_SPEC] * (2 * n) + [SEM_SPEC, SEM_SPEC] + [pl.BlockSpec(memory_space=pl.ANY)] * len(after),
        out_specs=[HBM_SPEC] * (2 * n), input_output_aliases={i: i for i in range(2 * n)},
        compiler_params=pltpu.CompilerParams(has_side_effects=SIDE_EFFECT))(*h.srcs, *h.lands, h.send, h.recv, *after)
    return list(res[n:])


def _mm_xw(x, g, out_dtype, name, tm_c=(768, 512, 384, 256, 128, 64, 16), tn_c=(1024, 768, 512, 256, 128), tk_c=(2048, 1024, 768, 512, 256)):
    m, r = x.shape
    nb, r2, cl = g.shape
    assert r == r2
    tm, tn, tk = _pick(m, tm_c), _pick(cl, tn_c), _pick(r, tk_c)
    q, nk = cl // tn, r // tk

    def body(x_ref, g_ref, o_ref, *acc):
        p = lax.dot_general(x_ref[...].astype(BF16), g_ref[...], (((1,), (0,)), ((), ())), preferred_element_type=F32)
        if nk == 1:
            o_ref[...] = p.astype(o_ref.dtype)
        else:
            k = pl.program_id(2)

            @pl.when(k == 0)
            def _():
                acc[0][...] = p

            @pl.when(k > 0)
            def _():
                acc[0][...] += p

            @pl.when(k == nk - 1)
            def _():
                o_ref[...] = acc[0][...].astype(o_ref.dtype)

    return _pcall(
        body, name=name, grid=(m // tm, nb * q, nk),
        in_specs=[pl.BlockSpec((tm, tk), lambda i, j, k: (i, k)), pl.BlockSpec((None, tk, tn), lambda i, j, k: (j // q, k, j % q))],
        out_specs=pl.BlockSpec((tm, tn), lambda i, j, k: (i, j)),
        out_shape=_sds((m, nb * cl), out_dtype),
        scratch=[] if nk == 1 else [pltpu.VMEM((tm, tn), F32)])(x, g)


def _mm_dyw(dy, g, out_dtype, name, init=None, tm_c=(768, 512, 384, 256, 128), tn_c=(1024, 512, 256, 128), tk_c=(2048, 1536, 1024, 768, 512, 256, 128)):
    m, n = dy.shape
    nb, r, cl = g.shape
    assert n == nb * cl
    tm, tn, tk = _pick(m, tm_c), _pick(r, tn_c), _pick(cl, tk_c)
    q = cl // tk
    nk = nb * q
    has_init = init is not None

    def body(*refs):
        if has_init:
            dy_ref, g_ref, i_ref, o_ref, acc = refs
        else:
            dy_ref, g_ref, o_ref, acc = refs
        k = pl.program_id(2)
        p = lax.dot_general(dy_ref[...].astype(BF16), g_ref[...], (((1,), (1,)), ((), ())), preferred_element_type=F32)

        @pl.when(k == 0)
        def _():
            acc[...] = p + i_ref[...].astype(F32) if has_init else p

        @pl.when(k > 0)
        def _():
            acc[...] += p

        @pl.when(k == nk - 1)
        def _():
            o_ref[...] = acc[...].astype(o_ref.dtype)

    in_specs = [pl.BlockSpec((tm, tk), lambda i, j, k: (i, k)), pl.BlockSpec((None, tn, tk), lambda i, j, k: (k // q, j, k % q))]
    args = [dy, g]
    if has_init:
        in_specs.append(pl.BlockSpec((tm, tn), lambda i, j, k: (i, j)))
        args.append(init)
    return _pcall(body, name=name, grid=(m // tm, r // tn, nk), in_specs=in_specs,
                  out_specs=pl.BlockSpec((tm, tn), lambda i, j, k: (i, j)), out_shape=_sds((m, r), out_dtype),
                  scratch=[pltpu.VMEM((tm, tn), F32)])(*args)


def _mm_xtdy(x, dy, nb, out_dtype, name, tm_c=(1024, 512, 256, 128), tn_c=(768, 512, 256, 128), tk_c=(2304, 2048, 1152, 1024, 768, 512, 256, 128, 16)):
    t, r = x.shape
    t2, n = dy.shape
    assert t == t2 and n % nb == 0
    cl = n // nb
    tm, tn, tk = _pick(r, tm_c), _pick(cl, tn_c), _pick(t, tk_c)
    q, nk = cl // tn, t // tk

    def body(x_ref, dy_ref, o_ref, *acc):
        p = lax.dot_general(x_ref[...].astype(BF16), dy_ref[...].astype(BF16), (((0,), (0,)), ((), ())), preferred_element_type=F32)
        if nk == 1:
            o_ref[...] = p.astype(o_ref.dtype)
        else:
            k = pl.program_id(2)

            @pl.when(k == 0)
            def _():
                acc[0][...] = p

            @pl.when(k > 0)
            def _():
                acc[0][...] += p

            @pl.when(k == nk - 1)
            def _():
                o_ref[...] = acc[0][...].astype(o_ref.dtype)

    return _pcall(
        body, name=name, grid=(r // tm, nb * q, nk),
        in_specs=[pl.BlockSpec((tk, tm), lambda i, j, k: (k, i)), pl.BlockSpec((tk, tn), lambda i, j, k: (k, j))],
        out_specs=pl.BlockSpec((None, tm, tn), lambda i, j, k: (j // q, i, j % q)),
        out_shape=_sds((nb, r, cl), out_dtype),
        scratch=[] if nk == 1 else [pltpu.VMEM((tm, tn), F32)])(x, dy)


def _mm_f32(a, b, name, trans_b=False):
    dims = (((1,), (1,)), ((), ())) if trans_b else (((1,), (0,)), ((), ()))
    n = b.shape[0] if trans_b else b.shape[1]

    def body(a_ref, b_ref, o_ref):
        o_ref[...] = lax.dot_general(a_ref[...], b_ref[...], dims, precision=HIGHEST, preferred_element_type=F32)

    return _pcall(body, name=name, out_shape=_sds((a.shape[0], n), F32))(a, b)


def _tiled(fn, ins, outs, grid, name, acc=()):
    n_in = len(ins)
    grid = tuple(grid) or (1,)
    nd = len(grid)

    def body(*refs):
        vals = fn(*[r[...] for r in refs[:n_in]])
        if not isinstance(vals, (tuple, list)):
            vals = (vals,)
        first = None
        for o, (ref, v) in enumerate(zip(refs[n_in:], vals)):
            if o in acc:
                if first is None:
                    first = pl.program_id(0) == 0
                    for a in range(1, nd):
                        first = jnp.logical_and(first, pl.program_id(a) == 0)

                @pl.when(first)
                def _(ref=ref, v=v):
                    ref[...] = v.astype(ref.dtype)

                @pl.when(jnp.logical_not(first))
                def _(ref=ref, v=v):
                    ref[...] += v.astype(ref.dtype)
            else:
                ref[...] = v.astype(ref.dtype)

    res = _pcall(body, name=name, grid=grid,
                 in_specs=[pl.BlockSpec(b, im) for _, b, im in ins],
                 out_specs=[pl.BlockSpec(b, im) for _, _, b, im in outs],
                 out_shape=[_sds(s, d) for s, d, _, _ in outs])(*[a for a, _, _ in ins])
    return list(res)


def _rows(a, tr):
    return (a, (tr, a.shape[1]), lambda i, *_: (i, 0))


def _const(a):
    nd = a.ndim
    return (a, a.shape, lambda *_: (0,) * nd)


def _cast_bf16(w, name):
    r, c = w.shape
    tr = _pick(r, (256, 128, 64, 16))
    return _tiled(lambda v: v, [_rows(w, tr)], [((r, c), BF16, (tr, c), lambda i: (i, 0))], (r // tr,), name)[0]


def _f_norm_mod(x, g, sh, sc):
    y = x * lax.rsqrt(jnp.mean(x * x, axis=-1, keepdims=True) + EPS) * g
    return y * (1.0 + sc) + sh


def _rope_partner_impl(y):
    nf = HEAD_DIM // 4
    lane = lax.broadcasted_iota(jnp.int32, y.shape, 1)
    return jnp.where(lane % (2 * nf) < nf, pltpu.roll(y, HEAD_DIM - nf, 1), pltpu.roll(y, nf, 1))


_rope_partner = jax.custom_vjp(_rope_partner_impl)
_rope_partner.defvjp(lambda y: (_rope_partner_impl(y), None), lambda _, ct: (_rope_partner_impl(ct),))


def _f_qk(t, g, cos, sin):
    y = t * lax.rsqrt(jnp.mean(t * t, axis=-1, keepdims=True) + EPS) * g
    return y * cos + _rope_partner(y) * sin


def _f_readout(of, ob, gate, g):
    o = of + ob
    on = o * lax.rsqrt(jnp.mean(o * o, axis=-1, keepdims=True) + EPS) * g
    return on * (gate * jax.nn.sigmoid(gate))


def _f_merge(ga, gb, pa, pb):
    return jax.nn.sigmoid(ga) * pa + jax.nn.sigmoid(gb) * pb


def _f_resid(x, gate, m):
    return x + gate * m


def _norm_mod_fwd(xcat, g, mods, n_ctx_tiles, tr):
    tt, d = xcat.shape
    which = lambda i: (jnp.where(i >= n_ctx_tiles, 1, 0), 0, 0, 0)

    def fn(x, gg, md):
        return _f_norm_mod(x, gg, md[0], md[1])

    return _tiled(fn, [_rows(xcat, tr), _const(g), (mods, (None, 2, 1, d), which)],
                  [((tt, d), BF16, (tr, d), lambda i: (i, 0))], (tt // tr,), "norm_mod_fwd")[0]


def _norm_mod_bwd(xcat, g, mods, dh, extra, n_ctx_tiles, tr, name):
    tt, d = xcat.shape
    nt = tt // tr
    has_extra = extra is not None

    def body(*refs):
        if has_extra:
            x_ref, g_ref, m_ref, dh_ref, e_ref, dx_ref, dg_ref, dm_ref = refs
        else:
            x_ref, g_ref, m_ref, dh_ref, dx_ref, dg_ref, dm_ref = refs
        i = pl.program_id(0)
        md = m_ref[...]
        _, vjp = jax.vjp(_f_norm_mod, x_ref[...], g_ref[...], md[0], md[1])
        dx, dg, dsh, dsc = vjp(dh_ref[...].astype(F32))
        dx_ref[...] = dx + e_ref[...] if has_extra else dx

        @pl.when(i == 0)
        def _():
            dg_ref[...] = dg

        @pl.when(i > 0)
        def _():
            dg_ref[...] += dg

        fresh = jnp.logical_or(i == 0, i == n_ctx_tiles)

        @pl.when(fresh)
        def _():
            dm_ref[0] = dsh
            dm_ref[1] = dsc

        @pl.when(jnp.logical_not(fresh))
        def _():
            dm_ref[0] += dsh
            dm_ref[1] += dsc

    which = lambda i: (jnp.where(i >= n_ctx_tiles, 1, 0), 0, 0, 0)
    row = pl.BlockSpec((tr, d), lambda i: (i, 0))
    in_specs = [row, pl.BlockSpec((1, d), lambda i: (0, 0)), pl.BlockSpec((None, 2, 1, d), which), row]
    args = [xcat, g, mods, dh]
    if has_extra:
        in_specs.append(row)
        args.append(extra)
    return _pcall(body, name=name, grid=(nt,), in_specs=in_specs,
                  out_specs=[row, pl.BlockSpec((1, d), lambda i: (0, 0)), pl.BlockSpec((None, 2, 1, d), which)],
                  out_shape=[_sds((tt, d), F32), _sds((1, d), F32), _sds((2, 2, 1, d), F32)])(*args)


def _hgrn_tri(reverse):
    t = np.arange(HGRN_BLOCK)
    tri = (t[None, :] >= t[:, None]) if reverse else (t[None, :] <= t[:, None])
    tri = tri.astype(np.float32)
    return jnp.asarray(tri), jnp.asarray(tri.T.copy())


def _hgrn_rowblock(n, n_ctx_blocks, n_blocks, reverse):
    if not reverse:
        return n
    return jnp.where(n < n_ctx_blocks, n_ctx_blocks - 1 - n, n_blocks - 1 - n + n_ctx_blocks)


def _hgrn_gates(fl, lb):
    sg = jax.nn.sigmoid(fl)
    f = lb + (1.0 - lb) * sg
    return sg, f, jnp.log(f), 1.0 - f


def _hgrn_intra_mask(reverse):
    tio = lax.broadcasted_iota(jnp.int32, (HGRN_BLOCK, HEAD_DIM), 0)
    return (lambda s: tio <= s) if reverse else (lambda s: tio >= s)


def _unrolled(nblk, u, fn, init):
    assert nblk % u == 0

    def trip(b, c):
        for j in range(u):
            c = fn(b * u + j, j, c)
        return c

    return lax.fori_loop(0, nblk // u, trip, init)


HGRN_UNROLL_FWD = (4, 8)
HGRN_UNROLL_BWD = (8, 4)


def _hgrn_fwd(p, lb, seg_f, reverse, n_ctx_rows):
    tt = p.shape[0]
    hb = HGRN_BLOCK
    nblk, nctx = tt // hb, n_ctx_rows // hb
    tri, _ = _hgrn_tri(reverse)
    u1, u3 = HGRN_UNROLL_FWD

    def body(q_ref, f_ref, v_ref, lb_ref, tri_ref, o_ref, st_all, dec_all, qe_buf, cum_blk, k_blk, v_blk):
        mask = _hgrn_intra_mask(reverse)
        lbv = lb_ref[...]

        def phase1(n, slot, c):
            r0 = pl.multiple_of(_hgrn_rowblock(n, nctx, nblk, reverse) * hb, hb)
            q, v = q_ref[pl.ds(r0, hb), :], v_ref[pl.ds(r0, hb), :]
            _, f, g, k = _hgrn_gates(f_ref[pl.ds(r0, hb), :], lbv)
            cum = jnp.dot(tri_ref[...], g, precision=HIGHEST, preferred_element_type=F32)
            tot = jnp.sum(g, axis=0, keepdims=True)
            cum_blk[slot] = cum
            k_blk[slot] = k
            v_blk[slot] = v
            oi = jnp.zeros((hb, HEAD_DIM), F32)
            for s in range(hb):
                e = jnp.where(mask(s), jnp.exp(jnp.minimum(cum - cum_blk[slot, s:s + 1, :], 0.0)), 0.0)
                a_s = jnp.sum(q * e * k_blk[slot, s:s + 1, :], axis=-1, keepdims=True)
                oi = oi + a_s * v_blk[slot, s:s + 1, :]
            o_ref[pl.ds(r0, hb), :] = oi
            qe_buf[pl.ds(r0, hb), :] = q * jnp.exp(cum)
            kl = k * jnp.exp(tot - cum)
            st_all[n] = lax.dot_general(v.astype(BF16), kl.astype(BF16), (((0,), (0,)), ((), ())), preferred_element_type=F32)
            dec_all[pl.ds(n, 1), :] = jnp.exp(tot)
            return c

        _unrolled(nblk, u1, phase1, 0)

        def phase2(n, st):
            kv = st_all[n]
            st_all[n] = st
            return st * dec_all[pl.ds(n, 1), :] + kv

        lax.fori_loop(0, nblk, phase2, jnp.zeros((HEAD_DIM, HEAD_DIM), F32))

        def phase3(n, slot, c):
            r0 = pl.multiple_of(_hgrn_rowblock(n, nctx, nblk, reverse) * hb, hb)
            o_ref[pl.ds(r0, hb), :] += lax.dot_general(qe_buf[pl.ds(r0, hb), :].astype(BF16), st_all[n].astype(BF16),
                                                        (((1,), (1,)), ((), ())), preferred_element_type=F32)
            return c

        _unrolled(nblk, u3, phase3, 0)

    col = lambda seg: pl.BlockSpec((tt, HEAD_DIM), lambda h, seg=seg: (0, seg * N_HEADS + h))
    blk = pltpu.VMEM((u1, hb, HEAD_DIM), F32)
    return _pcall(
        body, name="hgrn_fwd_rev" if reverse else "hgrn_fwd", grid=(N_HEADS,),
        in_specs=[col(0), col(seg_f), col(3), pl.BlockSpec((1, HEAD_DIM), lambda h: (0, h)), pl.BlockSpec((hb, hb), lambda h: (0, 0))],
        out_specs=pl.BlockSpec((tt, HEAD_DIM), lambda h: (0, h)),
        out_shape=_sds((tt, N_HEADS * HEAD_DIM), F32),
        scratch=[pltpu.VMEM((nblk, HEAD_DIM, HEAD_DIM), F32), pltpu.VMEM((nblk, HEAD_DIM), F32), pltpu.VMEM((tt, HEAD_DIM), F32),
                 blk, blk, blk])(p, p, p, lb, tri)


def _hgrn_bwd(p, lb, do, seg_f, reverse, n_ctx_rows, prev):
    tt = p.shape[0]
    hb = HGRN_BLOCK
    nblk, nctx = tt // hb, n_ctx_rows // hb
    tri, tri_t = _hgrn_tri(reverse)
    last_row = 0 if reverse else hb - 1
    has_prev = prev is not None
    u1, u3 = HGRN_UNROLL_BWD

    def body(*refs):
        q_ref, f_ref, v_ref, lb_ref, tri_ref, trit_ref, do_ref = refs[:7]
        refs = refs[7:]
        if has_prev:
            pq_ref, pv_ref = refs[:2]
            refs = refs[2:]
        dq_ref, dfl_ref, dv_ref, dlb_ref, st_all, dd_all, dec_all, cum_buf, cum_blk, k_blk, v_blk, dk_blk, dv_blk = refs
        mask = _hgrn_intra_mask(reverse)
        lbv = lb_ref[...]
        tio = lax.broadcasted_iota(jnp.int32, (hb, HEAD_DIM), 0)

        def rows_of(n):
            rb = _hgrn_rowblock(n, nctx, nblk, reverse)
            return rb, pl.multiple_of(rb * hb, hb)

        def load_do(rb):
            lat0 = pl.multiple_of(jnp.maximum(rb - nctx, 0) * hb, hb)
            return jnp.where(rb >= nctx, do_ref[pl.ds(lat0, hb), :], 0.0)

        def phase1(n, slot, c):
            rb, r0 = rows_of(n)
            q, v = q_ref[pl.ds(r0, hb), :], v_ref[pl.ds(r0, hb), :]
            _, f, g, k = _hgrn_gates(f_ref[pl.ds(r0, hb), :], lbv)
            cum = jnp.dot(tri_ref[...], g, precision=HIGHEST, preferred_element_type=F32)
            tot = jnp.sum(g, axis=0, keepdims=True)
            cum_buf[pl.ds(r0, hb), :] = cum
            kl = k * jnp.exp(tot - cum)
            st_all[n] = lax.dot_general(v.astype(BF16), kl.astype(BF16), (((0,), (0,)), ((), ())), preferred_element_type=F32)
            dec_all[pl.ds(n, 1), :] = jnp.exp(tot)
            qe = q * jnp.exp(cum)
            dd_all[n] = lax.dot_general(load_do(rb).astype(BF16), qe.astype(BF16), (((0,), (0,)), ((), ())), preferred_element_type=F32)
            return c

        _unrolled(nblk, u1, phase1, 0)

        def phase2(n, st):
            kv = st_all[n]
            st_all[n] = st
            return st * dec_all[pl.ds(n, 1), :] + kv

        lax.fori_loop(0, nblk, phase2, jnp.zeros((HEAD_DIM, HEAD_DIM), F32))

        def phase2r(i, dst):
            n = nblk - 1 - i
            u = dd_all[n]
            dd_all[n] = dst
            return u + dst * dec_all[pl.ds(n, 1), :]

        lax.fori_loop(0, nblk, phase2r, jnp.zeros((HEAD_DIM, HEAD_DIM), F32))

        def phase3(n, slot, dlb):
            rb, r0 = rows_of(n)
            q, v = q_ref[pl.ds(r0, hb), :], v_ref[pl.ds(r0, hb), :]
            sg, f, g, k = _hgrn_gates(f_ref[pl.ds(r0, hb), :], lbv)
            cum = cum_buf[pl.ds(r0, hb), :]
            tot = jnp.sum(g, axis=0, keepdims=True)
            dob = load_do(rb)
            st, dst = st_all[n], dd_all[n]
            e_cum = jnp.exp(cum)
            e_rest = jnp.exp(tot - cum)
            dq = jnp.dot(dob.astype(BF16), st.astype(BF16), preferred_element_type=F32) * e_cum
            dk_inter = jnp.dot(v.astype(BF16), dst.astype(BF16), preferred_element_type=F32) * e_rest
            dv = lax.dot_general((k * e_rest).astype(BF16), dst.astype(BF16), (((1,), (1,)), ((), ())), preferred_element_type=F32)
            cum_blk[slot] = cum
            k_blk[slot] = k
            v_blk[slot] = v
            for s in range(hb):
                e = jnp.where(mask(s), jnp.exp(jnp.minimum(cum - cum_blk[slot, s:s + 1, :], 0.0)), 0.0)
                a_s = jnp.sum(q * e * k_blk[slot, s:s + 1, :], axis=-1, keepdims=True)
                da_s = jnp.sum(dob * v_blk[slot, s:s + 1, :], axis=-1, keepdims=True)
                gs = da_s * e
                dq = dq + gs * k_blk[slot, s:s + 1, :]
                dk_blk[slot, s:s + 1, :] = jnp.sum(gs * q, axis=0, keepdims=True)
                dv_blk[slot, s:s + 1, :] = jnp.sum(a_s * dob, axis=0, keepdims=True)
            dk = dk_inter + dk_blk[slot]
            dv = dv + dv_blk[slot]
            d_tot = jnp.sum(k * dk_inter, axis=0, keepdims=True) + jnp.exp(tot) * jnp.sum(dst * st, axis=0, keepdims=True)
            dcum = q * dq - k * dk + jnp.where(tio == last_row, d_tot, 0.0)
            dg = jnp.dot(trit_ref[...], dcum, precision=HIGHEST, preferred_element_type=F32)
            df = dg / f - dk
            if has_prev:
                dq = dq + pq_ref[pl.ds(r0, hb), :].astype(F32)
                dv = dv + pv_ref[pl.ds(r0, hb), :].astype(F32)
            dq_ref[pl.ds(r0, hb), :] = dq.astype(dq_ref.dtype)
            dv_ref[pl.ds(r0, hb), :] = dv.astype(dv_ref.dtype)
            dfl_ref[pl.ds(r0, hb), :] = (df * (1.0 - lbv) * sg * (1.0 - sg)).astype(dfl_ref.dtype)
            return dlb + jnp.sum(df * (1.0 - sg), axis=0, keepdims=True)

        dlb_ref[...] = _unrolled(nblk, u3, phase3, jnp.zeros((1, HEAD_DIM), F32))

    col = lambda seg: pl.BlockSpec((tt, HEAD_DIM), lambda h, seg=seg: (0, seg * N_HEADS + h))
    head = pl.BlockSpec((tt, HEAD_DIM), lambda h: (0, h))
    lbs = pl.BlockSpec((1, HEAD_DIM), lambda h: (0, h))
    tris = pl.BlockSpec((hb, hb), lambda h: (0, 0))
    in_specs = [col(0), col(seg_f), col(3), lbs, tris, tris, pl.BlockSpec((do.shape[0], HEAD_DIM), lambda h: (0, h))]
    args = [p, p, p, lb, tri, tri_t, do]
    mid = F32 if not has_prev else BF16
    if has_prev:
        in_specs += [head, head]
        args += list(prev)
    w = N_HEADS * HEAD_DIM
    blk = pltpu.VMEM((u3, hb, HEAD_DIM), F32)
    return _pcall(
        body, name="hgrn_bwd_rev" if reverse else "hgrn_bwd", grid=(N_HEADS,), in_specs=in_specs,
        out_specs=[head, head, head, lbs],
        out_shape=[_sds((tt, w), mid), _sds((tt, w), BF16), _sds((tt, w), mid), _sds((1, w), F32)],
        scratch=[pltpu.VMEM((nblk, HEAD_DIM, HEAD_DIM), F32), pltpu.VMEM((nblk, HEAD_DIM, HEAD_DIM), F32), pltpu.VMEM((nblk, HEAD_DIM), F32),
                 pltpu.VMEM((tt, HEAD_DIM), F32), blk, blk, blk, blk, blk])(*args)


def _na_geometry(rows):
    r = pl.program_id(1)
    rs = jnp.clip(r - WIN_R // 2, 0, rows - WIN_R)
    return r, rs, r - rs


def _na_scores(q, kb, kc, bias):
    scale = HEAD_DIM ** -0.5
    nt = (((1,), (1,)), ((), ()))
    sb = lax.dot_general(q, kb, nt, preferred_element_type=F32) * scale + bias
    sc = lax.dot_general(q, kc, nt, preferred_element_type=F32) * scale
    m = jnp.maximum(jnp.max(sb, axis=-1, keepdims=True), jnp.max(sc, axis=-1, keepdims=True))
    pb, pc = jnp.exp(sb - m), jnp.exp(sc - m)
    inv = 1.0 / (jnp.sum(pb, axis=-1, keepdims=True) + jnp.sum(pc, axis=-1, keepdims=True))
    return pb * inv, pc * inv


def _na_fwd(qn, kall, p, seg_v, bias, n_ctx_rows):
    t, tt = qn.shape[0], kall.shape[0]
    rows = t // GRID_W
    nband = WIN_R * GRID_W

    def body(q_ref, k_ref, v_ref, b_ref, o_ref):
        r, rs, _ = _na_geometry(rows)
        k0 = pl.multiple_of(n_ctx_rows + rs * GRID_W, GRID_W)
        q = q_ref[...]
        kb, kc = k_ref[pl.ds(k0, nband), :], k_ref[pl.ds(0, n_ctx_rows), :]
        vb, vc = v_ref[pl.ds(k0, nband), :].astype(BF16), v_ref[pl.ds(0, n_ctx_rows), :].astype(BF16)
        pb, pc = _na_scores(q, kb, kc, b_ref[...])
        o = jnp.dot(pb.astype(BF16), vb, preferred_element_type=F32) + jnp.dot(pc.astype(BF16), vc, preferred_element_type=F32)
        o_ref[...] = o.astype(o_ref.dtype)

    return _pcall(
        body, name="na_fwd", grid=(N_HEADS, rows),
        in_specs=[pl.BlockSpec((GRID_W, HEAD_DIM), lambda h, r: (r, h)),
                  pl.BlockSpec((tt, HEAD_DIM), lambda h, r: (0, h)),
                  pl.BlockSpec((tt, HEAD_DIM), lambda h, r: (0, seg_v * N_HEADS + h)),
                  pl.BlockSpec((None, None, GRID_W, nband), lambda h, r: (h, r - jnp.clip(r - WIN_R // 2, 0, rows - WIN_R), 0, 0))],
        out_specs=pl.BlockSpec((GRID_W, HEAD_DIM), lambda h, r: (r, h)),
        out_shape=_sds((t, N_HEADS * HEAD_DIM), BF16))(qn, kall, p, bias)


def _na_bwd(qn, kall, p, seg_v, bias, do, n_ctx_rows):
    t, tt = qn.shape[0], kall.shape[0]
    rows = t // GRID_W
    nband = WIN_R * GRID_W
    scale = HEAD_DIM ** -0.5
    tn = (((0,), (0,)), ((), ()))
    nt = (((1,), (1,)), ((), ()))

    def body(q_ref, k_ref, v_ref, b_ref, do_ref, dq_ref, dk_ref, dv_ref, db_ref, dv_acc):
        r, rs, var = _na_geometry(rows)
        k0 = pl.multiple_of(n_ctx_rows + rs * GRID_W, GRID_W)
        q = q_ref[...]
        kb, kc = k_ref[pl.ds(k0, nband), :], k_ref[pl.ds(0, n_ctx_rows), :]
        vb, vc = v_ref[pl.ds(k0, nband), :].astype(BF16), v_ref[pl.ds(0, n_ctx_rows), :].astype(BF16)
        pb, pc = _na_scores(q, kb, kc, b_ref[...])
        dob = do_ref[...].astype(BF16)
        o = jnp.dot(pb.astype(BF16), vb, preferred_element_type=F32) + jnp.dot(pc.astype(BF16), vc, preferred_element_type=F32)
        delta = jnp.sum(do_ref[...].astype(F32) * o, axis=-1, keepdims=True)
        dsb = pb * (lax.dot_general(dob, vb, nt, preferred_element_type=F32) - delta)
        dsc = pc * (lax.dot_general(dob, vc, nt, preferred_element_type=F32) - delta)
        dsb16, dsc16 = dsb.astype(BF16), dsc.astype(BF16)
        dq_ref[...] = (jnp.dot(dsb16, kb, preferred_element_type=F32) + jnp.dot(dsc16, kc, preferred_element_type=F32)) * scale

        @pl.when(r == 0)
        def _():
            dk_ref[...] = jnp.zeros_like(dk_ref)
            dv_acc[...] = jnp.zeros_like(dv_acc)

        dk_ref[pl.ds(k0, nband), :] += lax.dot_general(dsb16, q, tn, preferred_element_type=F32) * scale
        dk_ref[pl.ds(0, n_ctx_rows), :] += lax.dot_general(dsc16, q, tn, preferred_element_type=F32) * scale
        dv_acc[pl.ds(k0, nband), :] += lax.dot_general(pb.astype(BF16), dob, tn, preferred_element_type=F32)
        dv_acc[pl.ds(0, n_ctx_rows), :] += lax.dot_general(pc.astype(BF16), dob, tn, preferred_element_type=F32)

        @pl.when(r == rows - 1)
        def _():
            dv_ref[...] = dv_acc[...].astype(dv_ref.dtype)

        fresh = jnp.logical_or(r <= WIN_R // 2, r > rows - WIN_R // 2)

        @pl.when(fresh)
        def _():
            db_ref[...] = dsb

        @pl.when(jnp.logical_not(fresh))
        def _():
            db_ref[...] += dsb

    variant = lambda h, r: (h, r - jnp.clip(r - WIN_R // 2, 0, rows - WIN_R), 0, 0)
    head_all = pl.BlockSpec((tt, HEAD_DIM), lambda h, r: (0, h))
    qspec = pl.BlockSpec((GRID_W, HEAD_DIM), lambda h, r: (r, h))
    w = N_HEADS * HEAD_DIM
    return _pcall(
        body, name="na_bwd", grid=(N_HEADS, rows),
        in_specs=[qspec, head_all, pl.BlockSpec((tt, HEAD_DIM), lambda h, r: (0, seg_v * N_HEADS + h)),
                  pl.BlockSpec((None, None, GRID_W, nband), variant), qspec],
        out_specs=[qspec, head_all, head_all, pl.BlockSpec((None, None, GRID_W, nband), variant)],
        out_shape=[_sds((t, w), F32), _sds((tt, w), F32), _sds((tt, w), BF16), _sds((N_HEADS, WIN_R, GRID_W, nband), F32)],
        scratch=[pltpu.VMEM((tt, HEAD_DIM), F32)])(qn, kall, p, bias, do)


def _na_tables(t, n_ctx_rows):
    half, nf = HEAD_DIM // 2, HEAD_DIM // 4
    pos = np.arange(t)
    lane = np.arange(HEAD_DIM)
    inv = ROPE_THETA ** (-(np.arange(nf, dtype=np.float32)) / nf)
    which = np.where(lane < half, pos[:, None] // GRID_W, pos[:, None] % GRID_W).astype(np.float32)
    ang = which * inv[lane % nf][None, :]
    first = (lane % half) < nf
    cos = np.concatenate([np.ones((n_ctx_rows, HEAD_DIM), np.float32), np.cos(ang).astype(np.float32)])
    sin = np.concatenate([np.zeros((n_ctx_rows, HEAD_DIM), np.float32), np.where(first[None, :], -np.sin(ang), np.sin(ang)).astype(np.float32)])
    w = np.arange(GRID_W)
    dc = np.clip(w[None, :] - w[:, None], -(WIN_C - 1), WIN_C - 1) + WIN_C - 1
    onehot = np.zeros((32, GRID_W * GRID_W), np.float32)
    onehot[dc.reshape(-1), np.arange(GRID_W * GRID_W)] = 1.0
    cs = np.clip(w - WIN_C // 2, 0, GRID_W - WIN_C)
    col_in = (w[None, :] >= cs[:, None]) & (w[None, :] < cs[:, None] + WIN_C)
    onehot *= col_in.reshape(1, -1)
    neg = np.where(col_in, 0.0, NEG_BIG).astype(np.float32)
    return jnp.asarray(cos), jnp.asarray(sin), jnp.asarray(onehot), jnp.asarray(neg)


def _bias_slabs(rel_bias, onehot, neg):
    nr = 2 * WIN_R - 1
    rb = jnp.pad(rel_bias.reshape(N_HEADS * nr, 2 * WIN_C - 1), ((0, 0), (0, 1)))
    spread = _mm_f32(rb, onehot, "bias_spread").reshape(N_HEADS, nr, GRID_W, GRID_W)
    slabs = [spread[:, WIN_R - 1 - v:2 * WIN_R - 1 - v] for v in range(WIN_R)]
    b = jnp.stack(slabs, axis=1) + neg[None, None, None]
    return b.transpose(0, 1, 3, 2, 4).reshape(N_HEADS, WIN_R, GRID_W, WIN_R * GRID_W)


def _bias_grad(dbias, onehot):
    nr = 2 * WIN_R - 1
    d = dbias.reshape(N_HEADS, WIN_R, GRID_W, WIN_R, GRID_W).transpose(0, 1, 3, 2, 4)
    tot = jnp.zeros((N_HEADS, nr, GRID_W, GRID_W), F32)
    for v in range(WIN_R):
        tot = tot + jnp.pad(d[:, v], ((0, 0), (WIN_R - 1 - v, v), (0, 0), (0, 0)))
    g = _mm_f32(tot.reshape(N_HEADS * nr, GRID_W * GRID_W), onehot, "bias_grad", trans_b=True)
    return g[:, :2 * WIN_C - 1].reshape(1, N_HEADS, nr, 2 * WIN_C - 1)


def _shift_rows(u, up):
    n = u.shape[0]
    tio = lax.broadcasted_iota(jnp.int32, u.shape, 0)
    if up:
        return jnp.where(tio == n - 1, 0.0, pltpu.roll(u, n - 1, 0))
    return jnp.where(tio == 0, 0.0, pltpu.roll(u, 1, 0))


def _conv3(u, w_ref, b_ref):
    um, up = _shift_rows(u, False), _shift_rows(u, True)
    return um, up, um * w_ref[0:1, :] + u * w_ref[1:2, :] + up * w_ref[2:3, :] + b_ref[...]


def _ffn_act_fwd(u0, t3, cw, cb):
    t, n = u0.shape
    tc = _pick(n, (256, 128))

    def body(u_ref, t_ref, w_ref, b_ref, a_ref):
        _, _, uc = _conv3(u_ref[...], w_ref, b_ref)
        a_ref[...] = (uc * jax.nn.sigmoid(uc) * t_ref[...]).astype(a_ref.dtype)

    col = lambda rows_: pl.BlockSpec((rows_, tc), lambda j: (0, j))
    return _pcall(body, name="ffn_act_fwd", grid=(n // tc,), in_specs=[col(t), col(t), col(8), col(1)], out_specs=col(t),
                  out_shape=_sds((t, n), BF16))(u0, t3, cw, cb)


def _ffn_act_bwd(u0, t3, cw, cb, da):
    t, n = u0.shape
    tc = _pick(n, (256, 128))

    def body(u_ref, t_ref, w_ref, b_ref, da_ref, du_ref, dt_ref, dw_ref, db_ref):
        u = u_ref[...]
        um, up, uc = _conv3(u, w_ref, b_ref)
        sg = jax.nn.sigmoid(uc)
        dav = da_ref[...].astype(F32)
        dt_ref[...] = (dav * uc * sg).astype(dt_ref.dtype)
        duc = dav * t_ref[...] * sg * (1.0 + uc * (1.0 - sg))
        du = _shift_rows(duc, True) * w_ref[0:1, :] + duc * w_ref[1:2, :] + _shift_rows(duc, False) * w_ref[2:3, :]
        du_ref[...] = du.astype(du_ref.dtype)
        dw_ref[...] = jnp.zeros_like(dw_ref)
        dw_ref[0:1, :] = jnp.sum(duc * um, axis=0, keepdims=True)
        dw_ref[1:2, :] = jnp.sum(duc * u, axis=0, keepdims=True)
        dw_ref[2:3, :] = jnp.sum(duc * up, axis=0, keepdims=True)
        db_ref[...] = jnp.sum(duc, axis=0, keepdims=True)

    col = lambda rows_: pl.BlockSpec((rows_, tc), lambda j: (0, j))
    return _pcall(body, name="ffn_act_bwd", grid=(n // tc,), in_specs=[col(t), col(t), col(8), col(1), col(t)],
                  out_specs=[col(t), col(t), col(8), col(1)],
                  out_shape=[_sds((t, n), BF16), _sds((t, n), BF16), _sds((8, n), F32), _sds((1, n), F32)])(u0, t3, cw, cb, da)


def _adam_math(g, w, m, v):
    m2 = ADAM_B1 * m + (1.0 - ADAM_B1) * g
    v2 = ADAM_B2 * v + (1.0 - ADAM_B2) * (g * g)
    m_hat = m2 / (1.0 - ADAM_B1 ** ADAM_STEP)
    v_hat = v2 / (1.0 - ADAM_B2 ** ADAM_STEP)
    return -ADAM_LR * (m_hat / (jnp.sqrt(v_hat) + ADAM_EPS) + ADAM_WD * w), m2, v2


def _adam_big(parts, w, m, v, name):
    r, c = w.shape
    npart, _, cp = parts.shape
    tr = _pick(r, (64, 32, 16, 8))

    def body(p_ref, w_ref, m_ref, v_ref, g_ref, d_ref, m2_ref, v2_ref):
        g = p_ref[0, :, 0:c].astype(F32)
        for i in range(1, npart):
            g = g + p_ref[i, :, 0:c].astype(F32)
        d, m2, v2 = _adam_math(g, w_ref[...], m_ref[...], v_ref[...])
        g_ref[...] = g
        d_ref[...] = d
        m2_ref[...] = m2
        v2_ref[...] = v2

    row = pl.BlockSpec((tr, c), lambda i: (i, 0))
    return _pcall(body, name=name, grid=(r // tr,),
                  in_specs=[pl.BlockSpec((npart, tr, cp), lambda i: (0, i, 0)), row, row, row],
                  out_specs=[row] * 4, out_shape=[_sds((r, c), F32)] * 4)(parts, w, m, v)


def _adam_small(g, w, m, v):
    def body(g_ref, w_ref, m_ref, v_ref, d_ref, m2_ref, v2_ref):
        d_ref[...], m2_ref[...], v2_ref[...] = _adam_math(g_ref[...], w_ref[...], m_ref[...], v_ref[...])

    return _pcall(body, name="adam_small", out_shape=[_sds(g.shape, F32)] * 3)(g, w, m, v)


def _sum_parts(parts, name):
    def body(p_ref, o_ref):
        s = p_ref[0]
        for i in range(1, N_DEV):
            s = s + p_ref[i]
        o_ref[...] = s

    return _pcall(body, name=name, out_shape=_sds(parts.shape[1:], F32))(parts)


class _Pack:
    def __init__(self, shapes):
        self.shapes = shapes
        self.sizes = [int(np.prod(s)) for s in shapes]
        self.padded = [-(-n // (8 * LANE)) * 8 * LANE for n in self.sizes]
        self.offs = np.concatenate([[0], np.cumsum(self.padded)]).tolist()

    def pack(self, arrs):
        flat = [jnp.pad(a.reshape(-1).astype(F32), (0, p - n)) for a, n, p in zip(arrs, self.sizes, self.padded)]
        return jnp.concatenate(flat).reshape(-1, LANE)

    def unpack(self, slab):
        flat = slab.reshape(-1)
        return [flat[o:o + n].reshape(s) for o, n, s in zip(self.offs, self.sizes, self.shapes)]


def kernel(x, c, ctx, c_ctx, ada_w, ada_b, norm1_g, norm2_g, w_in, hgrn_lb_logits, hgrn_norm_g, na_q_norm_g, na_k_norm_g, na_rel_bias, w_branch_a, w_branch_b, w_out, ffn_w1, ffn_w3, ffn_conv_w, ffn_conv_b, ffn_w2, loss_target, m_c_ctx, m_ada_w, m_ada_b, m_norm1_g, m_norm2_g, m_w_in, m_hgrn_lb_logits, m_hgrn_norm_g, m_na_q_norm_g, m_na_k_norm_g, m_na_rel_bias, m_w_branch_a, m_w_branch_b, m_w_out, m_ffn_w1, m_ffn_w3, m_ffn_conv_w, m_ffn_conv_b, m_ffn_w2, v_c_ctx, v_ada_w, v_ada_b, v_norm1_g, v_norm2_g, v_w_in, v_hgrn_lb_logits, v_hgrn_norm_g, v_na_q_norm_g, v_na_k_norm_g, v_na_rel_bias, v_w_branch_a, v_w_branch_b, v_w_out, v_ffn_w1, v_ffn_w3, v_ffn_conv_w, v_ffn_conv_b, v_ffn_w2):
    t, d = x.shape[1], x.shape[2]
    n_ctx = ctx.shape[1]
    tt = n_ctx + t
    hw = N_HEADS * HEAD_DIM
    ci = w_in.shape[2]
    ca = ada_w.shape[2]
    ff_l = ffn_w1.shape[2]
    ff_p = -(-ff_l // LANE) * LANE
    rows = t // GRID_W
    assert rows >= WIN_R and t % GRID_W == 0 and n_ctx % GRID_W == 0 and ci % LANE == 0 and d % LANE == 0
    me = 4 * lax.axis_index("x") + 2 * lax.axis_index("y") + lax.axis_index("c")
    tr = _pick(n_ctx, (256, 128, 64))
    n_ctx_tiles = n_ctx // tr

    pad_c = lambda w: jnp.pad(w, ((0, 0), (0, ff_p - ff_l)))
    small_in = [c, hgrn_lb_logits.reshape(4, HEAD_DIM), jnp.pad(ffn_conv_w[0], ((0, 5), (0, ff_p - ff_l)))]
    c_all, lb_parts, cw_all = _exchange(small_in, "gather_params", scatter=False)
    ff = N_DEV * ff_p

    cc = jnp.concatenate([c_all.reshape(N_DEV, d), jnp.broadcast_to(c_ctx[None, :], (N_DEV, d))], axis=0)
    act = _tiled(lambda v: v * jax.nn.sigmoid(v), [_const(cc)], [(cc.shape, BF16, cc.shape, lambda *_: (0, 0))], (), "silu_c")[0]
    ada16 = _cast_bf16(ada_w[0], "cast_ada")
    mod_cols = _mm_xw(act, ada16.reshape(1, d, ca), F32, "ada_fwd")
    mod_all = _exchange([mod_cols], "gather_mod", scatter=False)[0]

    with _after(mod_all):
        w_in16 = _cast_bf16(w_in[0], "cast_w_in")
    fly_in = _exchange_start([w_in16], "gather_w_in_start", scatter=False)
    rest16 = []
    for w_, nm in ((w_branch_a[0], "cast_w_a"), (w_branch_b[0], "cast_w_b"), (w_out[0], "cast_w_out"), (pad_c(ffn_w1[0]), "cast_w1"),
                   (pad_c(ffn_w3[0]), "cast_w3"), (jnp.pad(ffn_w2[0], ((0, ff_p - ff_l), (0, 0))), "cast_w2")):
        with _after(fly_in.token):
            rest16.append(_cast_bf16(w_, nm))
    fly_rest = _exchange_start(rest16, "gather_rest_start", scatter=False)
    mod_all = mod_all.transpose(1, 0, 2).reshape(2 * N_DEV, N_MOD * d) + ada_b
    mod_l = lax.dynamic_slice_in_dim(mod_all, me, 1, axis=0).reshape(N_MOD, 1, d)
    mod_c = mod_all[N_DEV:N_DEV + 1].reshape(N_MOD, 1, d)
    mods1 = jnp.stack([mod_c[0:2], mod_l[0:2]])
    mods2 = jnp.stack([mod_l[3:5], mod_l[3:5]])
    gate1, gate2 = mod_l[2], mod_l[5]

    xcat = jnp.concatenate([ctx[0], x[0]], axis=0)
    with _after(fly_rest.token):
        hcat = _norm_mod_fwd(xcat, norm1_g, mods1, n_ctx_tiles, tr)
    g_in, = _exchange_wait(fly_in, [hcat], "gather_w_in_wait")
    p = _mm_xw(hcat, g_in, F32, "in_proj")
    lb_logits = lb_parts.transpose(1, 0, 2).reshape(2, 2, hw)
    lb_soft = _tiled(lambda a, b: (1.0 / (1.0 + jnp.exp(b - a)),), [_const(lb_logits[:, 0]), _const(lb_logits[:, 1])],
                     [((2, hw), F32, (2, hw), lambda *_: (0, 0))], (), "lb_softmax")[0]
    lb_f, lb_b = lb_soft[0:1], lb_soft[1:2]
    o_f = _hgrn_fwd(p, lb_f, 1, False, n_ctx)
    o_b = _hgrn_fwd(p, lb_b, 2, True, n_ctx)

    cos, sin, onehot, neg = _na_tables(t, n_ctx)
    bias = _bias_slabs(na_rel_bias[0], onehot, neg)
    hcol = lambda seg, off=0: (lambda i, h: (i + off, seg * N_HEADS + h))
    tq = GRID_W * 2
    lat0 = n_ctx // tq
    qk_fn = lambda tv, g, cs, sn: (_f_qk(tv, g, cs, sn),)
    tab = lambda a, off=0: (a, (tq, HEAD_DIM), lambda i, h: (i + off, 0))
    qn = _tiled(qk_fn, [(p, (tq, HEAD_DIM), hcol(5, lat0)), _const(na_q_norm_g), tab(cos, lat0), tab(sin, lat0)],
                [((t, hw), BF16, (tq, HEAD_DIM), lambda i, h: (i, h))], (t // tq, N_HEADS), "q_norm_rope")[0]
    kall = _tiled(qk_fn, [(p, (tq, HEAD_DIM), hcol(6)), _const(na_k_norm_g), tab(cos), tab(sin)],
                  [((tt, hw), BF16, (tq, HEAD_DIM), lambda i, h: (i, h))], (tt // tq, N_HEADS), "k_norm_rope")[0]
    y_b = _na_fwd(qn, kall, p, 7, bias, n_ctx)

    trh = tr
    lat_h = n_ctx // trh
    ospec = lambda a: (a, (trh, HEAD_DIM), lambda i, h: (i + lat_h, h))
    y_a = _tiled(lambda a, b, gt, g: (_f_readout(a, b, gt, g),),
                 [ospec(o_f), ospec(o_b), (p, (trh, HEAD_DIM), hcol(4, lat_h)), _const(hgrn_norm_g)],
                 [((t, hw), BF16, (trh, HEAD_DIM), lambda i, h: (i, h))], (t // trh, N_HEADS), "hgrn_readout")[0]

    g_a, g_b, g_out, g_w1, g_w3, g_w2 = _exchange_wait(fly_rest, [y_a, y_b], "gather_rest_wait")
    g_out = g_out.reshape(1, d, d)
    g_w2 = g_w2.reshape(1, ff, d)
    p_a = _mm_xw(y_a, g_a, F32, "branch_a")
    p_b = _mm_xw(y_b, g_b, F32, "branch_b")
    td = _pick(d, (512, 256, 128))
    nd_t = d // td
    lat_r = n_ctx // tr
    gcol = lambda k: (p, (tr, td), lambda i, j, k=k: (i + lat_r, 8 * hw // td + k * nd_t + j))
    dtile = lambda a: (a, (tr, td), lambda i, j: (i, j))
    z = _tiled(lambda ga, gb, pa, pb: (_f_merge(ga, gb, pa, pb),), [gcol(0), gcol(1), dtile(p_a), dtile(p_b)],
               [((t, d), BF16, (tr, td), lambda i, j: (i, j))], (t // tr, nd_t), "merge")[0]
    mix = _mm_xw(z, g_out, F32, "out_proj")
    xl = x[0]
    x_mid = _tiled(lambda a, g, m_: (_f_resid(a, g, m_),), [_rows(xl, tr), _const(gate1), _rows(mix, tr)],
                   [((t, d), F32, (tr, d), lambda i: (i, 0))], (t // tr,), "resid1")[0]

    h2 = _norm_mod_fwd(x_mid, norm2_g, mods2, 0, tr)
    u0 = _mm_xw(h2, g_w1, F32, "ffn_up1")
    t3 = _mm_xw(h2, g_w3, F32, "ffn_up3")
    cw_full = cw_all.transpose(1, 0, 2).reshape(8, ff)
    cb_full = jnp.pad(ffn_conv_b.reshape(N_DEV, ff_l), ((0, 0), (0, ff_p - ff_l))).reshape(1, ff)
    a_act = _ffn_act_fwd(u0, t3, cw_full, cb_full)
    f_out = _mm_xw(a_act, g_w2, F32, "ffn_down")

    def loss_fn(xm, g, f, tg):
        err = xm + g * f - tg
        return err * (1.0 / d), jnp.sum(err * err, axis=0, keepdims=True) * (0.5 / d), jnp.sum(err * (1.0 / d) * f, axis=0, keepdims=True)

    dy, loss_cols, d_gate2 = _tiled(loss_fn, [_rows(x_mid, tr), _const(gate2), _rows(f_out, tr), _rows(loss_target[0], tr)],
                                    [((t, d), F32, (tr, d), lambda i: (i, 0)), ((1, d), F32, (1, d), lambda i: (0, 0)),
                                     ((1, d), F32, (1, d), lambda i: (0, 0))], (t // tr,), "loss", acc=(1, 2))

    df = _tiled(lambda a, g: (a * g,), [_rows(dy, tr), _const(gate2)], [((t, d), BF16, (tr, d), lambda i: (i, 0))], (t // tr,), "d_ffn_out")[0]
    d_w2 = _mm_xtdy(a_act, df, 1, BF16, "d_w2")
    da = _mm_dyw(df, g_w2, BF16, "d_act")
    du0, dt3, d_cw, d_cb = _ffn_act_bwd(u0, t3, cw_full, cb_full, da)
    d_w1 = _mm_xtdy(h2, du0, N_DEV, BF16, "d_w1")
    d_w3 = _mm_xtdy(h2, dt3, N_DEV, BF16, "d_w3")
    fly_ffn = _exchange_start([d_w1, d_w3, d_w2.reshape(N_DEV, ff_p, d)], "scatter_ffn_start", scatter=True)
    with _after(fly_ffn.token):
        dh2 = _mm_dyw(du0, g_w1, F32, "d_h2_a")
    dh2 = _mm_dyw(dt3, g_w3, F32, "d_h2_b", init=dh2)
    dx_mid, d_norm2, d_mods2 = _norm_mod_bwd(x_mid, norm2_g, mods2, dh2, dy, 0, tr, "norm_mod2_bwd")

    dm, d_gate1 = _tiled(lambda dxm, g, m_: (dxm * g, jnp.sum(dxm * m_, axis=0, keepdims=True)),
                         [_rows(dx_mid, tr), _const(gate1), _rows(mix, tr)],
                         [((t, d), BF16, (tr, d), lambda i: (i, 0)), ((1, d), F32, (1, d), lambda i: (0, 0))], (t // tr,), "d_resid1", acc=(1,))
    d_wout = _mm_xtdy(z, dm, 1, BF16, "d_w_out")
    dz = _mm_dyw(dm, g_out, F32, "d_merge")

    def merge_bwd(ga, gb, pa, pb, dzv):
        _, vjp = jax.vjp(_f_merge, ga, gb, pa, pb)
        return vjp(dzv)

    dga, dgb, dpa, dpb = _tiled(merge_bwd, [gcol(0), gcol(1), dtile(p_a), dtile(p_b), dtile(dz)],
                                [((t, d), BF16, (tr, td), lambda i, j: (i, j))] * 4, (t // tr, nd_t), "merge_bwd")
    d_wa = _mm_xtdy(y_a, dpa, N_DEV, BF16, "d_w_a")
    d_wb = _mm_xtdy(y_b, dpb, N_DEV, BF16, "d_w_b")
    fly_mix = _exchange_start([d_wa, d_wb, d_wout.reshape(N_DEV, d // N_DEV, d)], "scatter_mix_start", scatter=True)
    with _after(fly_mix.token):
        dy_a = _mm_dyw(dpa, g_a, F32, "d_y_a")
    dy_b = _mm_dyw(dpb, g_b, BF16, "d_y_b")

    def readout_bwd(a, b, gt, g, ct):
        _, vjp = jax.vjp(_f_readout, a, b, gt, g)
        da_, _, dgt, dg = vjp(ct)
        return da_, dgt, dg

    hsp = lambda dt: ((t, hw), dt, (trh, HEAD_DIM), lambda i, h: (i, h))
    do_h, d_gate_o, d_hnorm = _tiled(
        readout_bwd, [ospec(o_f), ospec(o_b), (p, (trh, HEAD_DIM), hcol(4, lat_h)), _const(hgrn_norm_g), (dy_a, (trh, HEAD_DIM), lambda i, h: (i, h))],
        [hsp(F32), hsp(BF16), ((1, HEAD_DIM), F32, (1, HEAD_DIM), lambda i, h: (0, 0))], (t // trh, N_HEADS), "readout_bwd", acc=(2,))
    dq1, dfl_f, dv1, dlb_f = _hgrn_bwd(p, lb_f, do_h, 1, False, n_ctx, None)
    dq_h, dfl_b, dv_h, dlb_b = _hgrn_bwd(p, lb_b, do_h, 2, True, n_ctx, (dq1, dv1))

    dqn, dkall, dv_na, dbias = _na_bwd(qn, kall, p, 7, bias, dy_b, n_ctx)

    def qk_bwd(tv, g, cs, sn, ct):
        _, vjp = jax.vjp(lambda a, b: _f_qk(a, b, cs, sn), tv, g)
        return vjp(ct)

    d_pq, d_qnorm = _tiled(qk_bwd, [(p, (tq, HEAD_DIM), hcol(5, lat0)), _const(na_q_norm_g), tab(cos, lat0), tab(sin, lat0),
                                    (dqn, (tq, HEAD_DIM), lambda i, h: (i, h))],
                           [((t, hw), BF16, (tq, HEAD_DIM), lambda i, h: (i, h)), ((1, HEAD_DIM), F32, (1, HEAD_DIM), lambda i, h: (0, 0))],
                           (t // tq, N_HEADS), "q_norm_rope_bwd", acc=(1,))
    d_pk, d_knorm = _tiled(qk_bwd, [(p, (tq, HEAD_DIM), hcol(6)), _const(na_k_norm_g), tab(cos), tab(sin),
                                    (dkall, (tq, HEAD_DIM), lambda i, h: (i, h))],
                           [((tt, hw), BF16, (tq, HEAD_DIM), lambda i, h: (i, h)), ((1, HEAD_DIM), F32, (1, HEAD_DIM), lambda i, h: (0, 0))],
                           (tt // tq, N_HEADS), "k_norm_rope_bwd", acc=(1,))

    zc = lambda w_: jnp.zeros((n_ctx, w_), BF16)
    lat_only = lambda a: jnp.concatenate([zc(a.shape[1]), a], axis=0)
    dp = jnp.concatenate([dq_h, dfl_f, dfl_b, dv_h, lat_only(d_gate_o), lat_only(d_pq), d_pk, dv_na, lat_only(dga), lat_only(dgb)], axis=1)
    dhcat = _mm_dyw(dp, g_in, BF16, "d_hcat")
    zero_ctx = jnp.concatenate([jnp.zeros((n_ctx, d), F32), dx_mid], axis=0)
    dxcat, d_norm1, d_mods1 = _norm_mod_bwd(xcat, norm1_g, mods1, dhcat, zero_ctx, n_ctx_tiles, tr, "norm_mod1_bwd")
    grad_x = dxcat[n_ctx:][None]

    zd = jnp.zeros((1, d), F32)
    dmod_l = jnp.concatenate([d_mods1[1, 0], d_mods1[1, 1], d_gate1, d_mods2[1, 0], d_mods2[1, 1], d_gate2], axis=1)
    dmod_c = jnp.concatenate([d_mods1[0, 0], d_mods1[0, 1], zd, zd, zd, zd], axis=1)
    dmods = jnp.concatenate([dmod_l, dmod_c], axis=0).reshape(2, N_DEV, ca).transpose(1, 0, 2)
    dmods = jnp.pad(dmods, ((0, 0), (0, 6), (0, 0)))
    got = _exchange([dmods], "scatter_dmod", scatter=True)[0]
    dm_rows = jnp.concatenate([got[:, 0], got[:, 1]], axis=0)
    d_ada = _mm_xtdy(act, dm_rows, 1, F32, "d_ada_w")[0]
    back = _mm_dyw(dm_rows, ada16.reshape(1, d, ca), F32, "d_silu_c")
    d_cctx_part = _tiled(lambda b, v: (jnp.sum(b[N_DEV:], axis=0, keepdims=True) * (jax.nn.sigmoid(v) * (1.0 + v * (1.0 - jax.nn.sigmoid(v)))),),
                         [_const(back), _const(c_ctx.reshape(1, d))], [((1, d), F32, (1, d), lambda *_: (0, 0))], (), "d_c_ctx")[0]

    d_rel = _bias_grad(dbias, onehot)
    d_lb_soft = jnp.concatenate([dlb_f, dlb_b], axis=0)
    d_lb0 = _tiled(lambda s, g: (g * s * (1.0 - s),), [_const(lb_soft), _const(d_lb_soft)], [((2, hw), F32, (2, hw), lambda *_: (0, 0))], (), "d_lb")[0]
    d_lb_full = jnp.stack([d_lb0, -d_lb0], axis=1)
    d_cw_l = d_cw[:3].reshape(3, N_DEV, ff_p)[:, :, :ff_l].reshape(1, 3, N_DEV * ff_l)
    d_cb_l = d_cb.reshape(N_DEV, ff_p)[:, :ff_l].reshape(1, N_DEV * ff_l)
    small = [d_cctx_part.reshape(d), (dmod_l + dmod_c), d_norm1, d_norm2, d_lb_full, d_hnorm, d_qnorm, d_knorm, d_rel, d_cw_l, d_cb_l, loss_cols]
    pk = _Pack([a.shape for a in small])
    tot = _sum_parts(_exchange([pk.pack(small)], "gather_small", scatter=False)[0], "sum_small")
    g_cctx, g_ada_b, g_n1, g_n2, g_lb, g_hn, g_qn, g_kn, g_rel, g_cw, g_cb, loss_all = pk.unpack(tot)
    loss = _tiled(lambda v: (jnp.sum(v, axis=1, keepdims=True),), [_const(loss_all)], [((1, 1), F32, (1, 1), lambda *_: (0, 0))], (), "loss_total")[0][0, 0]
    g_lb = lax.dynamic_slice_in_dim(g_lb, me * HEAD_DIM, HEAD_DIM, axis=2)
    g_cw = lax.dynamic_slice_in_dim(g_cw, me * ff_l, ff_l, axis=2)
    small_names = [("c_ctx", g_cctx, c_ctx, m_c_ctx, v_c_ctx), ("ada_b", g_ada_b, ada_b, m_ada_b, v_ada_b),
                   ("norm1_g", g_n1, norm1_g, m_norm1_g, v_norm1_g), ("norm2_g", g_n2, norm2_g, m_norm2_g, v_norm2_g),
                   ("hgrn_lb_logits", g_lb, hgrn_lb_logits, m_hgrn_lb_logits, v_hgrn_lb_logits),
                   ("hgrn_norm_g", g_hn, hgrn_norm_g, m_hgrn_norm_g, v_hgrn_norm_g), ("na_q_norm_g", g_qn, na_q_norm_g, m_na_q_norm_g, v_na_q_norm_g),
                   ("na_k_norm_g", g_kn, na_k_norm_g, m_na_k_norm_g, v_na_k_norm_g), ("na_rel_bias", g_rel, na_rel_bias, m_na_rel_bias, v_na_rel_bias),
                   ("ffn_conv_w", g_cw, ffn_conv_w, m_ffn_conv_w, v_ffn_conv_w), ("ffn_conv_b", g_cb, ffn_conv_b, m_ffn_conv_b, v_ffn_conv_b)]
    pk2 = _Pack([s[1].shape for s in small_names])
    sd, sm, sv = _adam_small(*[pk2.pack([s[i] for s in small_names]) for i in (1, 2, 3, 4)])
    sd, sm, sv = pk2.unpack(sd), pk2.unpack(sm), pk2.unpack(sv)
    res = {s[0]: (s[1], sd[i], sm[i], sv[i]) for i, s in enumerate(small_names)}

    with _after(sd[0]):
        d_win = _mm_xtdy(hcat, dp, N_DEV, BF16, "d_w_in")
    fly_win = _exchange_start([d_win], "scatter_w_in_start", scatter=True)
    with _after(fly_win.token):
        res["ada_w"] = _adam_big(d_ada[None], ada_w[0], m_ada_w[0], v_ada_w[0], "adam_ada")
    r_w1, r_w3, r_w2 = _exchange_wait(fly_ffn, [res["ada_w"][1]], "scatter_ffn_wait")
    res["ffn_w1"] = _adam_big(r_w1, ffn_w1[0], m_ffn_w1[0], v_ffn_w1[0], "adam_w1")
    res["ffn_w3"] = _adam_big(r_w3, ffn_w3[0], m_ffn_w3[0], v_ffn_w3[0], "adam_w3")
    res["ffn_w2"] = _adam_big(r_w2, ffn_w2[0], m_ffn_w2[0], v_ffn_w2[0], "adam_w2")
    r_a, r_b, r_out = _exchange_wait(fly_mix, [res["ffn_w2"][1]], "scatter_mix_wait")
    res["w_branch_a"] = _adam_big(r_a, w_branch_a[0], m_w_branch_a[0], v_w_branch_a[0], "adam_w_a")
    res["w_branch_b"] = _adam_big(r_b, w_branch_b[0], m_w_branch_b[0], v_w_branch_b[0], "adam_w_b")
    res["w_out"] = _adam_big(r_out, w_out[0], m_w_out[0], v_w_out[0], "adam_w_out")
    r_in, = _exchange_wait(fly_win, [res["w_out"][1]], "scatter_w_in_wait")
    res["w_in"] = _adam_big(r_in, w_in[0], m_w_in[0], v_w_in[0], "adam_w_in")
    for k in ("w_in", "w_branch_a", "w_branch_b", "w_out", "ffn_w1", "ffn_w3", "ffn_w2", "ada_w"):
        res[k] = tuple(a[None] for a in res[k])

    order = ["c_ctx", "ada_w", "ada_b", "norm1_g", "norm2_g", "w_in", "hgrn_lb_logits", "hgrn_norm_g", "na_q_norm_g", "na_k_norm_g",
             "na_rel_bias", "w_branch_a", "w_branch_b", "w_out", "ffn_w1", "ffn_w3", "ffn_conv_w", "ffn_conv_b", "ffn_w2"]
    shapes = {"c_ctx": c_ctx.shape, "ada_b": ada_b.shape, "norm1_g": norm1_g.shape, "norm2_g": norm2_g.shape,
              "hgrn_lb_logits": hgrn_lb_logits.shape, "hgrn_norm_g": hgrn_norm_g.shape, "na_q_norm_g": na_q_norm_g.shape,
              "na_k_norm_g": na_k_norm_g.shape, "na_rel_bias": na_rel_bias.shape, "ffn_conv_w": ffn_conv_w.shape, "ffn_conv_b": ffn_conv_b.shape}
    outs = [loss, grad_x]
    for part in range(4):
        for k in order:
            a = res[k][part]
            outs.append(a.reshape(shapes[k]) if k in shapes else a)
    return tuple(outs)
```

```python
import functools

import numpy as np
import jax
import jax.numpy as jnp
from jax import lax
from jax.experimental import pallas as pl
from jax.experimental.pallas import tpu as pltpu
from jax.experimental.pallas import tpu_sc as plsc

F32 = jnp.float32
BF16 = jnp.bfloat16
HIGHEST = lax.Precision.HIGHEST

N_DEV = 8
MESH_ID = pl.DeviceIdType.MESH
LANE = 128
HEAD_DIM = 128
N_HEADS = 8
GRID_W = 64
WIN_R = 8
WIN_C = 16
ROPE_THETA = 10000.0
EPS = 1e-6
N_MOD = 6
HGRN_BLOCK = 16
NEG_BIG = -1e30
VMEM_LIMIT = 56 << 20

ADAM_LR = 0.001
ADAM_B1 = 0.9
ADAM_B2 = 0.999
ADAM_EPS = 1e-08
ADAM_WD = 0.01
ADAM_STEP = 10

HBM_SPEC = pl.BlockSpec(memory_space=pltpu.HBM)


_ORDER_AFTER = []


class _after:
    def __init__(self, *arrs):
        self.arrs = list(arrs)

    def __enter__(self):
        _ORDER_AFTER.extend(self.arrs)

    def __exit__(self, *exc):
        del _ORDER_AFTER[:]


def _pcall(body, *, name, out_shape, grid=None, in_specs=None, out_specs=None, scratch=(), aliases=None):
    kw = {}
    if grid is not None:
        kw["grid"] = grid
    extra = []
    if _ORDER_AFTER and in_specs is not None:
        extra = list(_ORDER_AFTER)
        del _ORDER_AFTER[:]
        n_in, n_extra, inner = len(in_specs), len(extra), body
        in_specs = list(in_specs) + [pl.BlockSpec(memory_space=pl.ANY)] * n_extra

        def body(*refs):
            return inner(*refs[:n_in], *refs[n_in + n_extra:])

    if extra:
        call = _pcall_inner(body, name, out_shape, kw, in_specs, out_specs, scratch, aliases)
        return lambda *args: call(*args, *extra)
    return _pcall_inner(body, name, out_shape, kw, in_specs, out_specs, scratch, aliases)


def _pcall_inner(body, name, out_shape, kw, in_specs, out_specs, scratch, aliases):
    if in_specs is not None:
        kw["in_specs"] = in_specs
    if out_specs is not None:
        kw["out_specs"] = out_specs
    if scratch:
        kw["scratch_shapes"] = list(scratch)
    if aliases:
        kw["input_output_aliases"] = aliases
    return pl.pallas_call(body, name=name, out_shape=out_shape,
                          compiler_params=pltpu.CompilerParams(vmem_limit_bytes=VMEM_LIMIT), **kw)


def _pick(dim, cands):
    for c in cands:
        if c <= dim and dim % c == 0:
            return c
    return dim


def _sds(shape, dtype):
    return jax.ShapeDtypeStruct(tuple(shape), dtype)


def _peers():
    x, y, c = lax.axis_index("x"), lax.axis_index("y"), lax.axis_index("c")
    out = []
    for k in range(1, N_DEV):
        px = 1 - x if (k >> 2) & 1 else x
        py = 1 - y if (k >> 1) & 1 else y
        pc = 1 - c if k & 1 else c
        out.append((k, (px, py, pc), 4 * px + 2 * py + pc))
    return 4 * x + 2 * y + c, out


def _exchange(arrs, name, scatter):
    n = len(arrs)

    def body(*refs):
        ins, outs = refs[:n], refs[n:2 * n]
        send, recv, loc = refs[2 * n:]
        me, peers = _peers()
        started = []
        for i in range(n):
            src = ins[i].at[me] if scatter else ins[i]
            cp = pltpu.make_async_copy(src, outs[i].at[me], loc.at[i])
            cp.start()
            started.append(cp)
        sends = []
        for k, peer, pidx in peers:
            for i in range(n):
                src = ins[i].at[pidx] if scatter else ins[i]
                cp = pltpu.make_async_remote_copy(src_ref=src, dst_ref=outs[i].at[me], send_sem=send.at[i * 7 + k - 1],
                                                  recv_sem=recv.at[i * 7 + k - 1], device_id=peer, device_id_type=MESH_ID)
                cp.start()
                sends.append(cp)
        for k, peer, pidx in peers:
            for i in range(n):
                src = ins[i].at[pidx] if scatter else ins[i]
                pltpu.make_async_remote_copy(src_ref=src, dst_ref=outs[i].at[pidx], send_sem=send.at[i * 7 + k - 1],
                                             recv_sem=recv.at[i * 7 + k - 1], device_id=peer, device_id_type=MESH_ID).wait_recv()
        for cp in sends:
            cp.wait_send()
        for cp in started:
            cp.wait()

    out_shape = [_sds(a.shape if scatter else (N_DEV,) + a.shape, a.dtype) for a in arrs]
    res = _pcall(body, name=name, out_shape=out_shape, in_specs=[HBM_SPEC] * n, out_specs=[HBM_SPEC] * n,
                 scratch=[pltpu.SemaphoreType.DMA((7 * n,)), pltpu.SemaphoreType.DMA((7 * n,)), pltpu.SemaphoreType.DMA((n,))])(*arrs)
    return list(res)


def _exchange_sc(arrs, name, scatter, collective_id):
    n = len(arrs)
    srcs = [jax.new_ref(a, memory_space=pltpu.MemorySpace.HBM) for a in arrs]
    lands = [jax.empty_ref(_sds(a.shape if scatter else (N_DEV,) + a.shape, a.dtype), memory_space=pltpu.MemorySpace.HBM) for a in arrs]

    @pl.kernel(mesh=plsc.ScalarSubcoreMesh(axis_name="seq", num_cores=1), name=name,
               scratch_types=(pltpu.SemaphoreType.DMA((7 * n,)), pltpu.SemaphoreType.DMA((7 * n,)), pltpu.SemaphoreType.DMA((n,))),
               compiler_params=pltpu.CompilerParams(collective_id=collective_id))
    def launch(send, recv, loc):
        me, peers = _peers()
        barrier = pltpu.get_barrier_semaphore()
        for _, peer, _ in peers:
            pl.semaphore_signal(barrier, inc=1, device_id=peer, device_id_type=MESH_ID)
        pl.semaphore_wait(barrier, N_DEV - 1)
        own = [pltpu.make_async_copy(srcs[i].at[me] if scatter else srcs[i], lands[i].at[me], loc.at[i]) for i in range(n)]
        for cp in own:
            cp.start()
        sends = []
        for k, peer, pidx in peers:
            for i in range(n):
                src = srcs[i].at[pidx] if scatter else srcs[i]
                cp = pltpu.make_async_remote_copy(src_ref=src, dst_ref=lands[i].at[me], send_sem=send.at[i * 7 + k - 1],
                                                  recv_sem=recv.at[i * 7 + k - 1], device_id=peer, device_id_type=MESH_ID)
                cp.start()
                sends.append(cp)
        for k, peer, pidx in peers:
            for i in range(n):
                src = srcs[i].at[pidx] if scatter else srcs[i]
                pltpu.make_async_remote_copy(src_ref=src, dst_ref=lands[i].at[pidx], send_sem=send.at[i * 7 + k - 1],
                                             recv_sem=recv.at[i * 7 + k - 1], device_id=peer, device_id_type=MESH_ID).wait_recv()
        for cp in sends:
            cp.wait_send()
        for cp in own:
            cp.wait()

    launch()
    return [r[...] for r in lands]


class _InFlight:
    def __init__(self, send, recv, srcs, lands, token, scatter):
        self.send, self.recv, self.srcs, self.lands, self.token, self.scatter = send, recv, srcs, lands, token, scatter


SEM_SPEC = pl.BlockSpec(memory_space=pltpu.SEMAPHORE)
SIDE_EFFECT = pltpu.SideEffectType.DATAFLOW_SIDE_EFFECTING


def _exchange_start(arrs, name, scatter):
    n = len(arrs)

    def body(*refs):
        ins, lands = refs[:n], refs[n:2 * n]
        send, recv = refs[2 * n], refs[2 * n + 1]
        token = refs[4 * n + 2]
        loc = refs[4 * n + 3]
        me, peers = _peers()
        own = [pltpu.make_async_copy(ins[i].at[me] if scatter else ins[i], lands[i].at[me], loc.at[i]) for i in range(n)]
        for cp in own:
            cp.start()
        for cp in own:
            cp.wait()
        for k, peer, pidx in peers:
            for i in range(n):
                src = ins[i].at[pidx] if scatter else ins[i]
                pltpu.make_async_remote_copy(src_ref=src, dst_ref=lands[i].at[me], send_sem=send.at[i * 7 + k - 1],
                                             recv_sem=recv.at[i * 7 + k - 1], device_id=peer, device_id_type=MESH_ID).start()
        token[...] = jnp.zeros_like(token)

    land_shapes = [a.shape if scatter else (N_DEV,) + a.shape for a in arrs]
    hbm = lambda a: pltpu.with_memory_space_constraint(a, pltpu.HBM)
    args = [hbm(a) for a in arrs] + [hbm(lax.empty(s, a.dtype)) for s, a in zip(land_shapes, arrs)]
    out_shape = ([pltpu.SemaphoreType.DMA((7 * n,)), pltpu.SemaphoreType.DMA((7 * n,))]
                 + [pltpu.HBM(a.shape, a.dtype) for a in arrs] + [pltpu.HBM(s, a.dtype) for s, a in zip(land_shapes, arrs)]
                 + [_sds((8, LANE), F32)])
    res = pl.pallas_call(
        body, name=name, out_shape=out_shape, in_specs=[HBM_SPEC] * (2 * n),
        out_specs=[SEM_SPEC, SEM_SPEC] + [HBM_SPEC] * (2 * n) + [pl.BlockSpec(memory_space=pltpu.VMEM)],
        input_output_aliases={i: 2 + i for i in range(2 * n)},
        scratch_shapes=[pltpu.SemaphoreType.DMA((n,))],
        compiler_params=pltpu.CompilerParams(has_side_effects=SIDE_EFFECT))(*args)
    return _InFlight(res[0], res[1], list(res[2:2 + n]), list(res[2 + n:2 + 2 * n]), res[2 + 2 * n], scatter)


def _exchange_wait(h, after, name):
    n = len(h.srcs)
    scatter = h.scatter
    after = list(after)

    def body(*refs):
        ins, lands = refs[:n], refs[n:2 * n]
        send, recv = refs[2 * n], refs[2 * n + 1]
        _, peers = _peers()
        for k, peer, pidx in peers:
            for i in range(n):
                src = ins[i].at[pidx] if scatter else ins[i]
                cp = pltpu.make_async_remote_copy(src_ref=src, dst_ref=lands[i].at[pidx], send_sem=send.at[i * 7 + k - 1],
                                                  recv_sem=recv.at[i * 7 + k - 1], device_id=peer, device_id_type=MESH_ID)
                cp.wait_send()
                cp.wait_recv()

    res = pl.pallas_call(
        body, name=name, out_shape=[pltpu.HBM(a.shape, a.dtype) for a in h.srcs + h.lands],
        in_specs=[HBM_SPEC] * (2 * n) + [SEM_SPEC, SEM_SPEC] + [pl.BlockSpec(memory_space=pl.ANY)] * len(after),
        out_specs=[HBM_SPEC] * (2 * n), input_output_aliases={i: i for i in range(2 * n)},
        compiler_params=pltpu.CompilerParams(has_side_effects=SIDE_EFFECT))(*h.srcs, *h.lands, h.send, h.recv, *after)
    return list(res[n:])


def _mm_xw(x, g, out_dtype, name, tm_c=(768, 512, 384, 256, 128, 64, 16), tn_c=(1024, 768, 512, 256, 128), tk_c=(2048, 1024, 768, 512, 256)):
    m, r = x.shape
    nb, r2, cl = g.shape
    assert r == r2
    tm, tn, tk = _pick(m, tm_c), _pick(cl, tn_c), _pick(r, tk_c)
    q, nk = cl // tn, r // tk

    def body(x_ref, g_ref, o_ref, *acc):
        p = lax.dot_general(x_ref[...].astype(BF16), g_ref[...], (((1,), (0,)), ((), ())), preferred_element_type=F32)
        if nk == 1:
            o_ref[...] = p.astype(o_ref.dtype)
        else:
            k = pl.program_id(2)

            @pl.when(k == 0)
            def _():
                acc[0][...] = p

            @pl.when(k > 0)
            def _():
                acc[0][...] += p

            @pl.when(k == nk - 1)
            def _():
                o_ref[...] = acc[0][...].astype(o_ref.dtype)

    return _pcall(
        body, name=name, grid=(m // tm, nb * q, nk),
        in_specs=[pl.BlockSpec((tm, tk), lambda i, j, k: (i, k)), pl.BlockSpec((None, tk, tn), lambda i, j, k: (j // q, k, j % q))],
        out_specs=pl.BlockSpec((tm, tn), lambda i, j, k: (i, j)),
        out_shape=_sds((m, nb * cl), out_dtype),
        scratch=[] if nk == 1 else [pltpu.VMEM((tm, tn), F32)])(x, g)


def _mm_dyw(dy, g, out_dtype, name, init=None, tm_c=(768, 512, 384, 256, 128), tn_c=(1024, 512, 256, 128), tk_c=(2048, 1536, 1024, 768, 512, 256, 128)):
    m, n = dy.shape
    nb, r, cl = g.shape
    assert n == nb * cl
    tm, tn, tk = _pick(m, tm_c), _pick(r, tn_c), _pick(cl, tk_c)
    q = cl // tk
    nk = nb * q
    has_init = init is not None

    def body(*refs):
        if has_init:
            dy_ref, g_ref, i_ref, o_ref, acc = refs
        else:
            dy_ref, g_ref, o_ref, acc = refs
        k = pl.program_id(2)
        p = lax.dot_general(dy_ref[...].astype(BF16), g_ref[...], (((1,), (1,)), ((), ())), preferred_element_type=F32)

        @pl.when(k == 0)
        def _():
            acc[...] = p + i_ref[...].astype(F32) if has_init else p

        @pl.when(k > 0)
        def _():
            acc[...] += p

        @pl.when(k == nk - 1)
        def _():
            o_ref[...] = acc[...].astype(o_ref.dtype)

    in_specs = [pl.BlockSpec((tm, tk), lambda i, j, k: (i, k)), pl.BlockSpec((None, tn, tk), lambda i, j, k: (k // q, j, k % q))]
    args = [dy, g]
    if has_init:
        in_specs.append(pl.BlockSpec((tm, tn), lambda i, j, k: (i, j)))
        args.append(init)
    return _pcall(body, name=name, grid=(m // tm, r // tn, nk), in_specs=in_specs,
                  out_specs=pl.BlockSpec((tm, tn), lambda i, j, k: (i, j)), out_shape=_sds((m, r), out_dtype),
                  scratch=[pltpu.VMEM((tm, tn), F32)])(*args)


def _mm_xtdy(x, dy, nb, out_dtype, name, tm_c=(1024, 512, 256, 128), tn_c=(768, 512, 256, 128), tk_c=(2304, 2048, 1152, 1024, 768, 512, 256, 128, 16)):
    t, r = x.shape
    t2, n = dy.shape
    assert t == t2 and n % nb == 0
    cl = n // nb
    tm, tn, tk = _pick(r, tm_c), _pick(cl, tn_c), _pick(t, tk_c)
    q, nk = cl // tn, t // tk

    def body(x_ref, dy_ref, o_ref, *acc):
        p = lax.dot_general(x_ref[...].astype(BF16), dy_ref[...].astype(BF16), (((0,), (0,)), ((), ())), preferred_element_type=F32)
        if nk == 1:
            o_ref[...] = p.astype(o_ref.dtype)
        else:
            k = pl.program_id(2)

            @pl.when(k == 0)
            def _():
                acc[0][...] = p

            @pl.when(k > 0)
            def _():
                acc[0][...] += p

            @pl.when(k == nk - 1)
            def _():
                o_ref[...] = acc[0][...].astype(o_ref.dtype)

    return _pcall(
        body, name=name, grid=(r // tm, nb * q, nk),
        in_specs=[pl.BlockSpec((tk, tm), lambda i, j, k: (k, i)), pl.BlockSpec((tk, tn), lambda i, j, k: (k, j))],
        out_specs=pl.BlockSpec((None, tm, tn), lambda i, j, k: (j // q, i, j % q)),
        out_shape=_sds((nb, r, cl), out_dtype),
        scratch=[] if nk == 1 else [pltpu.VMEM((tm, tn), F32)])(x, dy)


def _mm_f32(a, b, name, trans_b=False):
    dims = (((1,), (1,)), ((), ())) if trans_b else (((1,), (0,)), ((), ()))
    n = b.shape[0] if trans_b else b.shape[1]

    def body(a_ref, b_ref, o_ref):
        o_ref[...] = lax.dot_general(a_ref[...], b_ref[...], dims, precision=HIGHEST, preferred_element_type=F32)

    return _pcall(body, name=name, out_shape=_sds((a.shape[0], n), F32))(a, b)


def _tiled(fn, ins, outs, grid, name, acc=()):
    n_in = len(ins)
    grid = tuple(grid) or (1,)
    nd = len(grid)

    def body(*refs):
        vals = fn(*[r[...] for r in refs[:n_in]])
        if not isinstance(vals, (tuple, list)):
            vals = (vals,)
        first = None
        for o, (ref, v) in enumerate(zip(refs[n_in:], vals)):
            if o in acc:
                if first is None:
                    first = pl.program_id(0) == 0
                    for a in range(1, nd):
                        first = jnp.logical_and(first, pl.program_id(a) == 0)

                @pl.when(first)
                def _(ref=ref, v=v):
                    ref[...] = v.astype(ref.dtype)

                @pl.when(jnp.logical_not(first))
                def _(ref=ref, v=v):
                    ref[...] += v.astype(ref.dtype)
            else:
                ref[...] = v.astype(ref.dtype)

    res = _pcall(body, name=name, grid=grid,
                 in_specs=[pl.BlockSpec(b, im) for _, b, im in ins],
                 out_specs=[pl.BlockSpec(b, im) for _, _, b, im in outs],
                 out_shape=[_sds(s, d) for s, d, _, _ in outs])(*[a for a, _, _ in ins])
    return list(res)


def _rows(a, tr):
    return (a, (tr, a.shape[1]), lambda i, *_: (i, 0))


def _const(a):
    nd = a.ndim
    return (a, a.shape, lambda *_: (0,) * nd)


def _cast_bf16(w, name):
    r, c = w.shape
    tr = _pick(r, (256, 128, 64, 16))
    return _tiled(lambda v: v, [_rows(w, tr)], [((r, c), BF16, (tr, c), lambda i: (i, 0))], (r // tr,), name)[0]


def _f_norm_mod(x, g, sh, sc):
    y = x * lax.rsqrt(jnp.mean(x * x, axis=-1, keepdims=True) + EPS) * g
    return y * (1.0 + sc) + sh


def _rope_partner_impl(y):
    nf = HEAD_DIM // 4
    lane = lax.broadcasted_iota(jnp.int32, y.shape, 1)
    return jnp.where(lane % (2 * nf) < nf, pltpu.roll(y, HEAD_DIM - nf, 1), pltpu.roll(y, nf, 1))


_rope_partner = jax.custom_vjp(_rope_partner_impl)
_rope_partner.defvjp(lambda y: (_rope_partner_impl(y), None), lambda _, ct: (_rope_partner_impl(ct),))


def _f_qk(t, g, cos, sin):
    y = t * lax.rsqrt(jnp.mean(t * t, axis=-1, keepdims=True) + EPS) * g
    return y * cos + _rope_partner(y) * sin


def _f_readout(of, ob, gate, g):
    o = of + ob
    on = o * lax.rsqrt(jnp.mean(o * o, axis=-1, keepdims=True) + EPS) * g
    return on * (gate * jax.nn.sigmoid(gate))


def _f_merge(ga, gb, pa, pb):
    return jax.nn.sigmoid(ga) * pa + jax.nn.sigmoid(gb) * pb


def _f_resid(x, gate, m):
    return x + gate * m


def _norm_mod_fwd(xcat, g, mods, n_ctx_tiles, tr):
    tt, d = xcat.shape
    which = lambda i: (jnp.where(i >= n_ctx_tiles, 1, 0), 0, 0, 0)

    def fn(x, gg, md):
        return _f_norm_mod(x, gg, md[0], md[1])

    return _tiled(fn, [_rows(xcat, tr), _const(g), (mods, (None, 2, 1, d), which)],
                  [((tt, d), BF16, (tr, d), lambda i: (i, 0))], (tt // tr,), "norm_mod_fwd")[0]


def _norm_mod_bwd(xcat, g, mods, dh, extra, n_ctx_tiles, tr, name):
    tt, d = xcat.shape
    nt = tt // tr
    has_extra = extra is not None

    def body(*refs):
        if has_extra:
            x_ref, g_ref, m_ref, dh_ref, e_ref, dx_ref, dg_ref, dm_ref = refs
        else:
            x_ref, g_ref, m_ref, dh_ref, dx_ref, dg_ref, dm_ref = refs
        i = pl.program_id(0)
        md = m_ref[...]
        _, vjp = jax.vjp(_f_norm_mod, x_ref[...], g_ref[...], md[0], md[1])
        dx, dg, dsh, dsc = vjp(dh_ref[...].astype(F32))
        dx_ref[...] = dx + e_ref[...] if has_extra else dx

        @pl.when(i == 0)
        def _():
            dg_ref[...] = dg

        @pl.when(i > 0)
        def _():
            dg_ref[...] += dg

        fresh = jnp.logical_or(i == 0, i == n_ctx_tiles)

        @pl.when(fresh)
        def _():
            dm_ref[0] = dsh
            dm_ref[1] = dsc

        @pl.when(jnp.logical_not(fresh))
        def _():
            dm_ref[0] += dsh
            dm_ref[1] += dsc

    which = lambda i: (jnp.where(i >= n_ctx_tiles, 1, 0), 0, 0, 0)
    row = pl.BlockSpec((tr, d), lambda i: (i, 0))
    in_specs = [row, pl.BlockSpec((1, d), lambda i: (0, 0)), pl.BlockSpec((None, 2, 1, d), which), row]
    args = [xcat, g, mods, dh]
    if has_extra:
        in_specs.append(row)
        args.append(extra)
    return _pcall(body, name=name, grid=(nt,), in_specs=in_specs,
                  out_specs=[row, pl.BlockSpec((1, d), lambda i: (0, 0)), pl.BlockSpec((None, 2, 1, d), which)],
                  out_shape=[_sds((tt, d), F32), _sds((1, d), F32), _sds((2, 2, 1, d), F32)])(*args)


def _hgrn_tri(reverse):
    t = np.arange(HGRN_BLOCK)
    tri = (t[None, :] >= t[:, None]) if reverse else (t[None, :] <= t[:, None])
    tri = tri.astype(np.float32)
    return jnp.asarray(tri), jnp.asarray(tri.T.copy())


def _hgrn_rowblock(n, n_ctx_blocks, n_blocks, reverse):
    if not reverse:
        return n
    return jnp.where(n < n_ctx_blocks, n_ctx_blocks - 1 - n, n_blocks - 1 - n + n_ctx_blocks)


def _hgrn_gates(fl, lb):
    sg = jax.nn.sigmoid(fl)
    f = lb + (1.0 - lb) * sg
    return sg, f, jnp.log(f), 1.0 - f


def _hgrn_intra_mask(reverse):
    tio = lax.broadcasted_iota(jnp.int32, (HGRN_BLOCK, HEAD_DIM), 0)
    return (lambda s: tio <= s) if reverse else (lambda s: tio >= s)


def _unrolled(nblk, u, fn, init):
    assert nblk % u == 0

    def trip(b, c):
        for j in range(u):
            c = fn(b * u + j, j, c)
        return c

    return lax.fori_loop(0, nblk // u, trip, init)


HGRN_UNROLL_FWD = (4, 8)
HGRN_UNROLL_BWD = (8, 4)


def _hgrn_fwd(p, lb, seg_f, reverse, n_ctx_rows):
    tt = p.shape[0]
    hb = HGRN_BLOCK
    nblk, nctx = tt // hb, n_ctx_rows // hb
    tri, _ = _hgrn_tri(reverse)
    u1, u3 = HGRN_UNROLL_FWD

    def body(q_ref, f_ref, v_ref, lb_ref, tri_ref, o_ref, st_all, dec_all, qe_buf, cum_blk, k_blk, v_blk):
        mask = _hgrn_intra_mask(reverse)
        lbv = lb_ref[...]

        def phase1(n, slot, c):
            r0 = pl.multiple_of(_hgrn_rowblock(n, nctx, nblk, reverse) * hb, hb)
            q, v = q_ref[pl.ds(r0, hb), :], v_ref[pl.ds(r0, hb), :]
            _, f, g, k = _hgrn_gates(f_ref[pl.ds(r0, hb), :], lbv)
            cum = jnp.dot(tri_ref[...], g, precision=HIGHEST, preferred_element_type=F32)
            tot = jnp.sum(g, axis=0, keepdims=True)
            cum_blk[slot] = cum
            k_blk[slot] = k
            v_blk[slot] = v
            oi = jnp.zeros((hb, HEAD_DIM), F32)
            for s in range(hb):
                e = jnp.where(mask(s), jnp.exp(jnp.minimum(cum - cum_blk[slot, s:s + 1, :], 0.0)), 0.0)
                a_s = jnp.sum(q * e * k_blk[slot, s:s + 1, :], axis=-1, keepdims=True)
                oi = oi + a_s * v_blk[slot, s:s + 1, :]
            o_ref[pl.ds(r0, hb), :] = oi
            qe_buf[pl.ds(r0, hb), :] = q * jnp.exp(cum)
            kl = k * jnp.exp(tot - cum)
            st_all[n] = lax.dot_general(v.astype(BF16), kl.astype(BF16), (((0,), (0,)), ((), ())), preferred_element_type=F32)
            dec_all[pl.ds(n, 1), :] = jnp.exp(tot)
            return c

        _unrolled(nblk, u1, phase1, 0)

        def phase2(n, st):
            kv = st_all[n]
            st_all[n] = st
            return st * dec_all[pl.ds(n, 1), :] + kv

        lax.fori_loop(0, nblk, phase2, jnp.zeros((HEAD_DIM, HEAD_DIM), F32))

        def phase3(n, slot, c):
            r0 = pl.multiple_of(_hgrn_rowblock(n, nctx, nblk, reverse) * hb, hb)
            o_ref[pl.ds(r0, hb), :] += lax.dot_general(qe_buf[pl.ds(r0, hb), :].astype(BF16), st_all[n].astype(BF16),
                                                        (((1,), (1,)), ((), ())), preferred_element_type=F32)
            return c

        _unrolled(nblk, u3, phase3, 0)

    col = lambda seg: pl.BlockSpec((tt, HEAD_DIM), lambda h, seg=seg: (0, seg * N_HEADS + h))
    blk = pltpu.VMEM((u1, hb, HEAD_DIM), F32)
    return _pcall(
        body, name="hgrn_fwd_rev" if reverse else "hgrn_fwd", grid=(N_HEADS,),
        in_specs=[col(0), col(seg_f), col(3), pl.BlockSpec((1, HEAD_DIM), lambda h: (0, h)), pl.BlockSpec((hb, hb), lambda h: (0, 0))],
        out_specs=pl.BlockSpec((tt, HEAD_DIM), lambda h: (0, h)),
        out_shape=_sds((tt, N_HEADS * HEAD_DIM), F32),
        scratch=[pltpu.VMEM((nblk, HEAD_DIM, HEAD_DIM), F32), pltpu.VMEM((nblk, HEAD_DIM), F32), pltpu.VMEM((tt, HEAD_DIM), F32),
                 blk, blk, blk])(p, p, p, lb, tri)


def _hgrn_bwd(p, lb, do, seg_f, reverse, n_ctx_rows, prev):
    tt = p.shape[0]
    hb = HGRN_BLOCK
    nblk, nctx = tt // hb, n_ctx_rows // hb
    tri, tri_t = _hgrn_tri(reverse)
    last_row = 0 if reverse else hb - 1
    has_prev = prev is not None
    u1, u3 = HGRN_UNROLL_BWD

    def body(*refs):
        q_ref, f_ref, v_ref, lb_ref, tri_ref, trit_ref, do_ref = refs[:7]
        refs = refs[7:]
        if has_prev:
            pq_ref, pv_ref = refs[:2]
            refs = refs[2:]
        dq_ref, dfl_ref, dv_ref, dlb_ref, st_all, dd_all, dec_all, cum_buf, cum_blk, k_blk, v_blk, dk_blk, dv_blk = refs
        mask = _hgrn_intra_mask(reverse)
        lbv = lb_ref[...]
        tio = lax.broadcasted_iota(jnp.int32, (hb, HEAD_DIM), 0)

        def rows_of(n):
            rb = _hgrn_rowblock(n, nctx, nblk, reverse)
            return rb, pl.multiple_of(rb * hb, hb)

        def load_do(rb):
            lat0 = pl.multiple_of(jnp.maximum(rb - nctx, 0) * hb, hb)
            return jnp.where(rb >= nctx, do_ref[pl.ds(lat0, hb), :], 0.0)

        def phase1(n, slot, c):
            rb, r0 = rows_of(n)
            q, v = q_ref[pl.ds(r0, hb), :], v_ref[pl.ds(r0, hb), :]
            _, f, g, k = _hgrn_gates(f_ref[pl.ds(r0, hb), :], lbv)
            cum = jnp.dot(tri_ref[...], g, precision=HIGHEST, preferred_element_type=F32)
            tot = jnp.sum(g, axis=0, keepdims=True)
            cum_buf[pl.ds(r0, hb), :] = cum
            kl = k * jnp.exp(tot - cum)
            st_all[n] = lax.dot_general(v.astype(BF16), kl.astype(BF16), (((0,), (0,)), ((), ())), preferred_element_type=F32)
            dec_all[pl.ds(n, 1), :] = jnp.exp(tot)
            qe = q * jnp.exp(cum)
            dd_all[n] = lax.dot_general(load_do(rb).astype(BF16), qe.astype(BF16), (((0,), (0,)), ((), ())), preferred_element_type=F32)
            return c

        _unrolled(nblk, u1, phase1, 0)

        def phase2(n, st):
            kv = st_all[n]
            st_all[n] = st
            return st * dec_all[pl.ds(n, 1), :] + kv

        lax.fori_loop(0, nblk, phase2, jnp.zeros((HEAD_DIM, HEAD_DIM), F32))

        def phase2r(i, dst):
            n = nblk - 1 - i
            u = dd_all[n]
            dd_all[n] = dst
            return u + dst * dec_all[pl.ds(n, 1), :]

        lax.fori_loop(0, nblk, phase2r, jnp.zeros((HEAD_DIM, HEAD_DIM), F32))

        def phase3(n, slot, dlb):
            rb, r0 = rows_of(n)
            q, v = q_ref[pl.ds(r0, hb), :], v_ref[pl.ds(r0, hb), :]
            sg, f, g, k = _hgrn_gates(f_ref[pl.ds(r0, hb), :], lbv)
            cum = cum_buf[pl.ds(r0, hb), :]
            tot = jnp.sum(g, axis=0, keepdims=True)
            dob = load_do(rb)
            st, dst = st_all[n], dd_all[n]
            e_cum = jnp.exp(cum)
            e_rest = jnp.exp(tot - cum)
            dq = jnp.dot(dob.astype(BF16), st.astype(BF16), preferred_element_type=F32) * e_cum
            dk_inter = jnp.dot(v.astype(BF16), dst.astype(BF16), preferred_element_type=F32) * e_rest
            dv = lax.dot_general((k * e_rest).astype(BF16), dst.astype(BF16), (((1,), (1,)), ((), ())), preferred_element_type=F32)
            cum_blk[slot] = cum
            k_blk[slot] = k
            v_blk[slot] = v
            for s in range(hb):
                e = jnp.where(mask(s), jnp.exp(jnp.minimum(cum - cum_blk[slot, s:s + 1, :], 0.0)), 0.0)
                a_s = jnp.sum(q * e * k_blk[slot, s:s + 1, :], axis=-1, keepdims=True)
                da_s = jnp.sum(dob * v_blk[slot, s:s + 1, :], axis=-1, keepdims=True)
                gs = da_s * e
                dq = dq + gs * k_blk[slot, s:s + 1, :]
                dk_blk[slot, s:s + 1, :] = jnp.sum(gs * q, axis=0, keepdims=True)
                dv_blk[slot, s:s + 1, :] = jnp.sum(a_s * dob, axis=0, keepdims=True)
            dk = dk_inter + dk_blk[slot]
            dv = dv + dv_blk[slot]
            d_tot = jnp.sum(k * dk_inter, axis=0, keepdims=True) + jnp.exp(tot) * jnp.sum(dst * st, axis=0, keepdims=True)
            dcum = q * dq - k * dk + jnp.where(tio == last_row, d_tot, 0.0)
            dg = jnp.dot(trit_ref[...], dcum, precision=HIGHEST, preferred_element_type=F32)
            df = dg / f - dk
            if has_prev:
                dq = dq + pq_ref[pl.ds(r0, hb), :].astype(F32)
                dv = dv + pv_ref[pl.ds(r0, hb), :].astype(F32)
            dq_ref[pl.ds(r0, hb), :] = dq.astype(dq_ref.dtype)
            dv_ref[pl.ds(r0, hb), :] = dv.astype(dv_ref.dtype)
            dfl_ref[pl.ds(r0, hb), :] = (df * (1.0 - lbv) * sg * (1.0 - sg)).astype(dfl_ref.dtype)
            return dlb + jnp.sum(df * (1.0 - sg), axis=0, keepdims=True)

        dlb_ref[...] = _unrolled(nblk, u3, phase3, jnp.zeros((1, HEAD_DIM), F32))

    col = lambda seg: pl.BlockSpec((tt, HEAD_DIM), lambda h, seg=seg: (0, seg * N_HEADS + h))
    head = pl.BlockSpec((tt, HEAD_DIM), lambda h: (0, h))
    lbs = pl.BlockSpec((1, HEAD_DIM), lambda h: (0, h))
    tris = pl.BlockSpec((hb, hb), lambda h: (0, 0))
    in_specs = [col(0), col(seg_f), col(3), lbs, tris, tris, pl.BlockSpec((do.shape[0], HEAD_DIM), lambda h: (0, h))]
    args = [p, p, p, lb, tri, tri_t, do]
    mid = F32 if not has_prev else BF16
    if has_prev:
        in_specs += [head, head]
        args += list(prev)
    w = N_HEADS * HEAD_DIM
    blk = pltpu.VMEM((u3, hb, HEAD_DIM), F32)
    return _pcall(
        body, name="hgrn_bwd_rev" if reverse else "hgrn_bwd", grid=(N_HEADS,), in_specs=in_specs,
        out_specs=[head, head, head, lbs],
        out_shape=[_sds((tt, w), mid), _sds((tt, w), BF16), _sds((tt, w), mid), _sds((1, w), F32)],
        scratch=[pltpu.VMEM((nblk, HEAD_DIM, HEAD_DIM), F32), pltpu.VMEM((nblk, HEAD_DIM, HEAD_DIM), F32), pltpu.VMEM((nblk, HEAD_DIM), F32),
                 pltpu.VMEM((tt, HEAD_DIM), F32), blk, blk, blk, blk, blk])(*args)


def _na_geometry(rows):
    r = pl.program_id(1)
    rs = jnp.clip(r - WIN_R // 2, 0, rows - WIN_R)
    return r, rs, r - rs


def _na_scores(q, kb, kc, bias):
    scale = HEAD_DIM ** -0.5
    nt = (((1,), (1,)), ((), ()))
    sb = lax.dot_general(q, kb, nt, preferred_element_type=F32) * scale + bias
    sc = lax.dot_general(q, kc, nt, preferred_element_type=F32) * scale
    m = jnp.maximum(jnp.max(sb, axis=-1, keepdims=True), jnp.max(sc, axis=-1, keepdims=True))
    pb, pc = jnp.exp(sb - m), jnp.exp(sc - m)
    inv = 1.0 / (jnp.sum(pb, axis=-1, keepdims=True) + jnp.sum(pc, axis=-1, keepdims=True))
    return pb * inv, pc * inv


def _na_fwd(qn, kall, p, seg_v, bias, n_ctx_rows):
    t, tt = qn.shape[0], kall.shape[0]
    rows = t // GRID_W
    nband = WIN_R * GRID_W

    def body(q_ref, k_ref, v_ref, b_ref, o_ref):
        r, rs, _ = _na_geometry(rows)
        k0 = pl.multiple_of(n_ctx_rows + rs * GRID_W, GRID_W)
        q = q_ref[...]
        kb, kc = k_ref[pl.ds(k0, nband), :], k_ref[pl.ds(0, n_ctx_rows), :]
        vb, vc = v_ref[pl.ds(k0, nband), :].astype(BF16), v_ref[pl.ds(0, n_ctx_rows), :].astype(BF16)
        pb, pc = _na_scores(q, kb, kc, b_ref[...])
        o = jnp.dot(pb.astype(BF16), vb, preferred_element_type=F32) + jnp.dot(pc.astype(BF16), vc, preferred_element_type=F32)
        o_ref[...] = o.astype(o_ref.dtype)

    return _pcall(
        body, name="na_fwd", grid=(N_HEADS, rows),
        in_specs=[pl.BlockSpec((GRID_W, HEAD_DIM), lambda h, r: (r, h)),
                  pl.BlockSpec((tt, HEAD_DIM), lambda h, r: (0, h)),
                  pl.BlockSpec((tt, HEAD_DIM), lambda h, r: (0, seg_v * N_HEADS + h)),
                  pl.BlockSpec((None, None, GRID_W, nband), lambda h, r: (h, r - jnp.clip(r - WIN_R // 2, 0, rows - WIN_R), 0, 0))],
        out_specs=pl.BlockSpec((GRID_W, HEAD_DIM), lambda h, r: (r, h)),
        out_shape=_sds((t, N_HEADS * HEAD_DIM), BF16))(qn, kall, p, bias)


def _na_bwd(qn, kall, p, seg_v, bias, do, n_ctx_rows):
    t, tt = qn.shape[0], kall.shape[0]
    rows = t // GRID_W
    nband = WIN_R * GRID_W
    scale = HEAD_DIM ** -0.5
    tn = (((0,), (0,)), ((), ()))
    nt = (((1,), (1,)), ((), ()))

    def body(q_ref, k_ref, v_ref, b_ref, do_ref, dq_ref, dk_ref, dv_ref, db_ref, dv_acc):
        r, rs, var = _na_geometry(rows)
        k0 = pl.multiple_of(n_ctx_rows + rs * GRID_W, GRID_W)
        q = q_ref[...]
        kb, kc = k_ref[pl.ds(k0, nband), :], k_ref[pl.ds(0, n_ctx_rows), :]
        vb, vc = v_ref[pl.ds(k0, nband), :].astype(BF16), v_ref[pl.ds(0, n_ctx_rows), :].astype(BF16)
        pb, pc = _na_scores(q, kb, kc, b_ref[...])
        dob = do_ref[...].astype(BF16)
        o = jnp.dot(pb.astype(BF16), vb, preferred_element_type=F32) + jnp.dot(pc.astype(BF16), vc, preferred_element_type=F32)
        delta = jnp.sum(do_ref[...].astype(F32) * o, axis=-1, keepdims=True)
        dsb = pb * (lax.dot_general(dob, vb, nt, preferred_element_type=F32) - delta)
        dsc = pc * (lax.dot_general(dob, vc, nt, preferred_element_type=F32) - delta)
        dsb16, dsc16 = dsb.astype(BF16), dsc.astype(BF16)
        dq_ref[...] = (jnp.dot(dsb16, kb, preferred_element_type=F32) + jnp.dot(dsc16, kc, preferred_element_type=F32)) * scale

        @pl.when(r == 0)
        def _():
            dk_ref[...] = jnp.zeros_like(dk_ref)
            dv_acc[...] = jnp.zeros_like(dv_acc)

        dk_ref[pl.ds(k0, nband), :] += lax.dot_general(dsb16, q, tn, preferred_element_type=F32) * scale
        dk_ref[pl.ds(0, n_ctx_rows), :] += lax.dot_general(dsc16, q, tn, preferred_element_type=F32) * scale
        dv_acc[pl.ds(k0, nband), :] += lax.dot_general(pb.astype(BF16), dob, tn, preferred_element_type=F32)
        dv_acc[pl.ds(0, n_ctx_rows), :] += lax.dot_general(pc.astype(BF16), dob, tn, preferred_element_type=F32)

        @pl.when(r == rows - 1)
        def _():
            dv_ref[...] = dv_acc[...].astype(dv_ref.dtype)

        fresh = jnp.logical_or(r <= WIN_R // 2, r > rows - WIN_R // 2)

        @pl.when(fresh)
        def _():
            db_ref[...] = dsb

        @pl.when(jnp.logical_not(fresh))
        def _():
            db_ref[...] += dsb

    variant = lambda h, r: (h, r - jnp.clip(r - WIN_R // 2, 0, rows - WIN_R), 0, 0)
    head_all = pl.BlockSpec((tt, HEAD_DIM), lambda h, r: (0, h))
    qspec = pl.BlockSpec((GRID_W, HEAD_DIM), lambda h, r: (r, h))
    w = N_HEADS * HEAD_DIM
    return _pcall(
        body, name="na_bwd", grid=(N_HEADS, rows),
        in_specs=[qspec, head_all, pl.BlockSpec((tt, HEAD_DIM), lambda h, r: (0, seg_v * N_HEADS + h)),
                  pl.BlockSpec((None, None, GRID_W, nband), variant), qspec],
        out_specs=[qspec, head_all, head_all, pl.BlockSpec((None, None, GRID_W, nband), variant)],
        out_shape=[_sds((t, w), F32), _sds((tt, w), F32), _sds((tt, w), BF16), _sds((N_HEADS, WIN_R, GRID_W, nband), F32)],
        scratch=[pltpu.VMEM((tt, HEAD_DIM), F32)])(qn, kall, p, bias, do)


def _na_tables(t, n_ctx_rows):
    half, nf = HEAD_DIM // 2, HEAD_DIM // 4
    pos = np.arange(t)
    lane = np.arange(HEAD_DIM)
    inv = ROPE_THETA ** (-(np.arange(nf, dtype=np.float32)) / nf)
    which = np.where(lane < half, pos[:, None] // GRID_W, pos[:, None] % GRID_W).astype(np.float32)
    ang = which * inv[lane % nf][None, :]
    first = (lane % half) < nf
    cos = np.concatenate([np.ones((n_ctx_rows, HEAD_DIM), np.float32), np.cos(ang).astype(np.float32)])
    sin = np.concatenate([np.zeros((n_ctx_rows, HEAD_DIM), np.float32), np.where(first[None, :], -np.sin(ang), np.sin(ang)).astype(np.float32)])
    w = np.arange(GRID_W)
    dc = np.clip(w[None, :] - w[:, None], -(WIN_C - 1), WIN_C - 1) + WIN_C - 1
    onehot = np.zeros((32, GRID_W * GRID_W), np.float32)
    onehot[dc.reshape(-1), np.arange(GRID_W * GRID_W)] = 1.0
    cs = np.clip(w - WIN_C // 2, 0, GRID_W - WIN_C)
    col_in = (w[None, :] >= cs[:, None]) & (w[None, :] < cs[:, None] + WIN_C)
    onehot *= col_in.reshape(1, -1)
    neg = np.where(col_in, 0.0, NEG_BIG).astype(np.float32)
    return jnp.asarray(cos), jnp.asarray(sin), jnp.asarray(onehot), jnp.asarray(neg)


def _bias_slabs(rel_bias, onehot, neg):
    nr = 2 * WIN_R - 1
    rb = jnp.pad(rel_bias.reshape(N_HEADS * nr, 2 * WIN_C - 1), ((0, 0), (0, 1)))
    spread = _mm_f32(rb, onehot, "bias_spread").reshape(N_HEADS, nr, GRID_W, GRID_W)
    slabs = [spread[:, WIN_R - 1 - v:2 * WIN_R - 1 - v] for v in range(WIN_R)]
    b = jnp.stack(slabs, axis=1) + neg[None, None, None]
    return b.transpose(0, 1, 3, 2, 4).reshape(N_HEADS, WIN_R, GRID_W, WIN_R * GRID_W)


def _bias_grad(dbias, onehot):
    nr = 2 * WIN_R - 1
    d = dbias.reshape(N_HEADS, WIN_R, GRID_W, WIN_R, GRID_W).transpose(0, 1, 3, 2, 4)
    tot = jnp.zeros((N_HEADS, nr, GRID_W, GRID_W), F32)
    for v in range(WIN_R):
        tot = tot + jnp.pad(d[:, v], ((0, 0), (WIN_R - 1 - v, v), (0, 0), (0, 0)))
    g = _mm_f32(tot.reshape(N_HEADS * nr, GRID_W * GRID_W), onehot, "bias_grad", trans_b=True)
    return g[:, :2 * WIN_C - 1].reshape(1, N_HEADS, nr, 2 * WIN_C - 1)


def _shift_rows(u, up):
    n = u.shape[0]
    tio = lax.broadcasted_iota(jnp.int32, u.shape, 0)
    if up:
        return jnp.where(tio == n - 1, 0.0, pltpu.roll(u, n - 1, 0))
    return jnp.where(tio == 0, 0.0, pltpu.roll(u, 1, 0))


def _conv3(u, w_ref, b_ref):
    um, up = _shift_rows(u, False), _shift_rows(u, True)
    return um, up, um * w_ref[0:1, :] + u * w_ref[1:2, :] + up * w_ref[2:3, :] + b_ref[...]


def _ffn_act_fwd(u0, t3, cw, cb):
    t, n = u0.shape
    tc = _pick(n, (256, 128))

    def body(u_ref, t_ref, w_ref, b_ref, a_ref):
        _, _, uc = _conv3(u_ref[...], w_ref, b_ref)
        a_ref[...] = (uc * jax.nn.sigmoid(uc) * t_ref[...]).astype(a_ref.dtype)

    col = lambda rows_: pl.BlockSpec((rows_, tc), lambda j: (0, j))
    return _pcall(body, name="ffn_act_fwd", grid=(n // tc,), in_specs=[col(t), col(t), col(8), col(1)], out_specs=col(t),
                  out_shape=_sds((t, n), BF16))(u0, t3, cw, cb)


def _ffn_act_bwd(u0, t3, cw, cb, da):
    t, n = u0.shape
    tc = _pick(n, (256, 128))

    def body(u_ref, t_ref, w_ref, b_ref, da_ref, du_ref, dt_ref, dw_ref, db_ref):
        u = u_ref[...]
        um, up, uc = _conv3(u, w_ref, b_ref)
        sg = jax.nn.sigmoid(uc)
        dav = da_ref[...].astype(F32)
        dt_ref[...] = (dav * uc * sg).astype(dt_ref.dtype)
        duc = dav * t_ref[...] * sg * (1.0 + uc * (1.0 - sg))
        du = _shift_rows(duc, True) * w_ref[0:1, :] + duc * w_ref[1:2, :] + _shift_rows(duc, False) * w_ref[2:3, :]
        du_ref[...] = du.astype(du_ref.dtype)
        dw_ref[...] = jnp.zeros_like(dw_ref)
        dw_ref[0:1, :] = jnp.sum(duc * um, axis=0, keepdims=True)
        dw_ref[1:2, :] = jnp.sum(duc * u, axis=0, keepdims=True)
        dw_ref[2:3, :] = jnp.sum(duc * up, axis=0, keepdims=True)
        db_ref[...] = jnp.sum(duc, axis=0, keepdims=True)

    col = lambda rows_: pl.BlockSpec((rows_, tc), lambda j: (0, j))
    return _pcall(body, name="ffn_act_bwd", grid=(n // tc,), in_specs=[col(t), col(t), col(8), col(1), col(t)],
                  out_specs=[col(t), col(t), col(8), col(1)],
                  out_shape=[_sds((t, n), BF16), _sds((t, n), BF16), _sds((8, n), F32), _sds((1, n), F32)])(u0, t3, cw, cb, da)


def _adam_math(g, w, m, v):
    m2 = ADAM_B1 * m + (1.0 - ADAM_B1) * g
    v2 = ADAM_B2 * v + (1.0 - ADAM_B2) * (g * g)
    m_hat = m2 / (1.0 - ADAM_B1 ** ADAM_STEP)
    v_hat = v2 / (1.0 - ADAM_B2 ** ADAM_STEP)
    return -ADAM_LR * (m_hat / (jnp.sqrt(v_hat) + ADAM_EPS) + ADAM_WD * w), m2, v2


def _adam_big(parts, w, m, v, name):
    r, c = w.shape
    npart, _, cp = parts.shape
    tr = _pick(r, (64, 32, 16, 8))

    def body(p_ref, w_ref, m_ref, v_ref, g_ref, d_ref, m2_ref, v2_ref):
        g = p_ref[0, :, 0:c].astype(F32)
        for i in range(1, npart):
            g = g + p_ref[i, :, 0:c].astype(F32)
        d, m2, v2 = _adam_math(g, w_ref[...], m_ref[...], v_ref[...])
        g_ref[...] = g
        d_ref[...] = d
        m2_ref[...] = m2
        v2_ref[...] = v2

    row = pl.BlockSpec((tr, c), lambda i: (i, 0))
    return _pcall(body, name=name, grid=(r // tr,),
                  in_specs=[pl.BlockSpec((npart, tr, cp), lambda i: (0, i, 0)), row, row, row],
                  out_specs=[row] * 4, out_shape=[_sds((r, c), F32)] * 4)(parts, w, m, v)


def _adam_small(g, w, m, v):
    def body(g_ref, w_ref, m_ref, v_ref, d_ref, m2_ref, v2_ref):
        d_ref[...], m2_ref[...], v2_ref[...] = _adam_math(g_ref[...], w_ref[...], m_ref[...], v_ref[...])

    return _pcall(body, name="adam_small", out_shape=[_sds(g.shape, F32)] * 3)(g, w, m, v)


def _sum_parts(parts, name):
    def body(p_ref, o_ref):
        s = p_ref[0]
        for i in range(1, N_DEV):
            s = s + p_ref[i]
        o_ref[...] = s

    return _pcall(body, name=name, out_shape=_sds(parts.shape[1:], F32))(parts)


class _Pack:
    def __init__(self, shapes):
        self.shapes = shapes
        self.sizes = [int(np.prod(s)) for s in shapes]
        self.padded = [-(-n // (8 * LANE)) * 8 * LANE for n in self.sizes]
        self.offs = np.concatenate([[0], np.cumsum(self.padded)]).tolist()

    def pack(self, arrs):
        flat = [jnp.pad(a.reshape(-1).astype(F32), (0, p - n)) for a, n, p in zip(arrs, self.sizes, self.padded)]
        return jnp.concatenate(flat).reshape(-1, LANE)

    def unpack(self, slab):
        flat = slab.reshape(-1)
        return [flat[o:o + n].reshape(s) for o, n, s in zip(self.offs, self.sizes, self.shapes)]


def kernel(x, c, ctx, c_ctx, ada_w, ada_b, norm1_g, norm2_g, w_in, hgrn_lb_logits, hgrn_norm_g, na_q_norm_g, na_k_norm_g, na_rel_bias, w_branch_a, w_branch_b, w_out, ffn_w1, ffn_w3, ffn_conv_w, ffn_conv_b, ffn_w2, loss_target, m_c_ctx, m_ada_w, m_ada_b, m_norm1_g, m_norm2_g, m_w_in, m_hgrn_lb_logits, m_hgrn_norm_g, m_na_q_norm_g, m_na_k_norm_g, m_na_rel_bias, m_w_branch_a, m_w_branch_b, m_w_out, m_ffn_w1, m_ffn_w3, m_ffn_conv_w, m_ffn_conv_b, m_ffn_w2, v_c_ctx, v_ada_w, v_ada_b, v_norm1_g, v_norm2_g, v_w_in, v_hgrn_lb_logits, v_hgrn_norm_g, v_na_q_norm_g, v_na_k_norm_g, v_na_rel_bias, v_w_branch_a, v_w_branch_b, v_w_out, v_ffn_w1, v_ffn_w3, v_ffn_conv_w, v_ffn_conv_b, v_ffn_w2):
    t, d = x.shape[1], x.shape[2]
    n_ctx = ctx.shape[1]
    tt = n_ctx + t
    hw = N_HEADS * HEAD_DIM
    ci = w_in.shape[2]
    ca = ada_w.shape[2]
    ff_l = ffn_w1.shape[2]
    ff_p = -(-ff_l // LANE) * LANE
    rows = t // GRID_W
    assert rows >= WIN_R and t % GRID_W == 0 and n_ctx % GRID_W == 0 and ci % LANE == 0 and d % LANE == 0
    me = 4 * lax.axis_index("x") + 2 * lax.axis_index("y") + lax.axis_index("c")
    tr = _pick(n_ctx, (256, 128, 64))
    n_ctx_tiles = n_ctx // tr

    pad_c = lambda w: jnp.pad(w, ((0, 0), (0, ff_p - ff_l)))
    small_in = [c, hgrn_lb_logits.reshape(4, HEAD_DIM), jnp.pad(ffn_conv_w[0], ((0, 5), (0, ff_p - ff_l)))]
    c_all, lb_parts, cw_all = _exchange(small_in, "gather_params", scatter=False)
    ff = N_DEV * ff_p

    cc = jnp.concatenate([c_all.reshape(N_DEV, d), jnp.broadcast_to(c_ctx[None, :], (N_DEV, d))], axis=0)
    act = _tiled(lambda v: v * jax.nn.sigmoid(v), [_const(cc)], [(cc.shape, BF16, cc.shape, lambda *_: (0, 0))], (), "silu_c")[0]
    ada16 = _cast_bf16(ada_w[0], "cast_ada")
    mod_cols = _mm_xw(act, ada16.reshape(1, d, ca), F32, "ada_fwd")
    mod_all = _exchange([mod_cols], "gather_mod", scatter=False)[0]

    with _after(mod_all):
        w_in16 = _cast_bf16(w_in[0], "cast_w_in")
    g_in, = _exchange_sc([w_in16], "gather_w_in", False, 1)
    rest16 = []
    for w_, nm in ((w_branch_a[0], "cast_w_a"), (w_branch_b[0], "cast_w_b"), (w_out[0], "cast_w_out"), (pad_c(ffn_w1[0]), "cast_w1"),
                   (pad_c(ffn_w3[0]), "cast_w3"), (jnp.pad(ffn_w2[0], ((0, ff_p - ff_l), (0, 0))), "cast_w2")):
        with _after(w_in16):
            rest16.append(_cast_bf16(w_, nm))
    g_a, g_b, g_out, g_w1, g_w3, g_w2 = _exchange_sc(rest16, "gather_rest", False, 2)
    g_out = g_out.reshape(1, d, d)
    g_w2 = g_w2.reshape(1, ff, d)
    mod_all = mod_all.transpose(1, 0, 2).reshape(2 * N_DEV, N_MOD * d) + ada_b
    mod_l = lax.dynamic_slice_in_dim(mod_all, me, 1, axis=0).reshape(N_MOD, 1, d)
    mod_c = mod_all[N_DEV:N_DEV + 1].reshape(N_MOD, 1, d)
    mods1 = jnp.stack([mod_c[0:2], mod_l[0:2]])
    mods2 = jnp.stack([mod_l[3:5], mod_l[3:5]])
    gate1, gate2 = mod_l[2], mod_l[5]

    xcat = jnp.concatenate([ctx[0], x[0]], axis=0)
    hcat = _norm_mod_fwd(xcat, norm1_g, mods1, n_ctx_tiles, tr)
    p = _mm_xw(hcat, g_in, F32, "in_proj")
    lb_logits = lb_parts.transpose(1, 0, 2).reshape(2, 2, hw)
    lb_soft = _tiled(lambda a, b: (1.0 / (1.0 + jnp.exp(b - a)),), [_const(lb_logits[:, 0]), _const(lb_logits[:, 1])],
                     [((2, hw), F32, (2, hw), lambda *_: (0, 0))], (), "lb_softmax")[0]
    lb_f, lb_b = lb_soft[0:1], lb_soft[1:2]
    o_f = _hgrn_fwd(p, lb_f, 1, False, n_ctx)
    o_b = _hgrn_fwd(p, lb_b, 2, True, n_ctx)

    cos, sin, onehot, neg = _na_tables(t, n_ctx)
    bias = _bias_slabs(na_rel_bias[0], onehot, neg)
    hcol = lambda seg, off=0: (lambda i, h: (i + off, seg * N_HEADS + h))
    tq = tr
    lat0 = n_ctx // tq
    qk_fn = lambda tv, g, cs, sn: (_f_qk(tv, g, cs, sn),)
    tab = lambda a, off=0: (a, (tq, HEAD_DIM), lambda i, h: (i + off, 0))
    qn = _tiled(qk_fn, [(p, (tq, HEAD_DIM), hcol(5, lat0)), _const(na_q_norm_g), tab(cos, lat0), tab(sin, lat0)],
                [((t, hw), BF16, (tq, HEAD_DIM), lambda i, h: (i, h))], (t // tq, N_HEADS), "q_norm_rope")[0]
    kall = _tiled(qk_fn, [(p, (tq, HEAD_DIM), hcol(6)), _const(na_k_norm_g), tab(cos), tab(sin)],
                  [((tt, hw), BF16, (tq, HEAD_DIM), lambda i, h: (i, h))], (tt // tq, N_HEADS), "k_norm_rope")[0]
    y_b = _na_fwd(qn, kall, p, 7, bias, n_ctx)

    trh = tr
    lat_h = n_ctx // trh
    ospec = lambda a: (a, (trh, HEAD_DIM), lambda i, h: (i + lat_h, h))
    y_a = _tiled(lambda a, b, gt, g: (_f_readout(a, b, gt, g),),
                 [ospec(o_f), ospec(o_b), (p, (trh, HEAD_DIM), hcol(4, lat_h)), _const(hgrn_norm_g)],
                 [((t, hw), BF16, (trh, HEAD_DIM), lambda i, h: (i, h))], (t // trh, N_HEADS), "hgrn_readout")[0]

    p_a = _mm_xw(y_a, g_a, F32, "branch_a")
    p_b = _mm_xw(y_b, g_b, F32, "branch_b")
    td = _pick(d, (512, 256, 128))
    nd_t = d // td
    lat_r = n_ctx // tr
    gcol = lambda k: (p, (tr, td), lambda i, j, k=k: (i + lat_r, 8 * hw // td + k * nd_t + j))
    dtile = lambda a: (a, (tr, td), lambda i, j: (i, j))
    z = _tiled(lambda ga, gb, pa, pb: (_f_merge(ga, gb, pa, pb),), [gcol(0), gcol(1), dtile(p_a), dtile(p_b)],
               [((t, d), BF16, (tr, td), lambda i, j: (i, j))], (t // tr, nd_t), "merge")[0]
    mix = _mm_xw(z, g_out, F32, "out_proj")
    xl = x[0]
    x_mid = _tiled(lambda a, g, m_: (_f_resid(a, g, m_),), [_rows(xl, tr), _const(gate1), _rows(mix, tr)],
                   [((t, d), F32, (tr, d), lambda i: (i, 0))], (t // tr,), "resid1")[0]

    h2 = _norm_mod_fwd(x_mid, norm2_g, mods2, 0, tr)
    u0 = _mm_xw(h2, g_w1, F32, "ffn_up1")
    t3 = _mm_xw(h2, g_w3, F32, "ffn_up3")
    cw_full = cw_all.transpose(1, 0, 2).reshape(8, ff)
    cb_full = jnp.pad(ffn_conv_b.reshape(N_DEV, ff_l), ((0, 0), (0, ff_p - ff_l))).reshape(1, ff)
    a_act = _ffn_act_fwd(u0, t3, cw_full, cb_full)
    f_out = _mm_xw(a_act, g_w2, F32, "ffn_down")

    def loss_fn(xm, g, f, tg):
        err = xm + g * f - tg
        return err * (1.0 / d), jnp.sum(err * err, axis=0, keepdims=True) * (0.5 / d), jnp.sum(err * (1.0 / d) * f, axis=0, keepdims=True)

    dy, loss_cols, d_gate2 = _tiled(loss_fn, [_rows(x_mid, tr), _const(gate2), _rows(f_out, tr), _rows(loss_target[0], tr)],
                                    [((t, d), F32, (tr, d), lambda i: (i, 0)), ((1, d), F32, (1, d), lambda i: (0, 0)),
                                     ((1, d), F32, (1, d), lambda i: (0, 0))], (t // tr,), "loss", acc=(1, 2))

    df = _tiled(lambda a, g: (a * g,), [_rows(dy, tr), _const(gate2)], [((t, d), BF16, (tr, d), lambda i: (i, 0))], (t // tr,), "d_ffn_out")[0]
    d_w2 = _mm_xtdy(a_act, df, 1, BF16, "d_w2")
    da = _mm_dyw(df, g_w2, BF16, "d_act")
    du0, dt3, d_cw, d_cb = _ffn_act_bwd(u0, t3, cw_full, cb_full, da)
    d_w1 = _mm_xtdy(h2, du0, N_DEV, BF16, "d_w1")
    d_w3 = _mm_xtdy(h2, dt3, N_DEV, BF16, "d_w3")
    r_w1, r_w3, r_w2 = _exchange_sc([d_w1, d_w3, d_w2.reshape(N_DEV, ff_p, d)], "scatter_ffn", True, 3)
    dh2 = _mm_dyw(du0, g_w1, F32, "d_h2_a")
    dh2 = _mm_dyw(dt3, g_w3, F32, "d_h2_b", init=dh2)
    dx_mid, d_norm2, d_mods2 = _norm_mod_bwd(x_mid, norm2_g, mods2, dh2, dy, 0, tr, "norm_mod2_bwd")

    dm, d_gate1 = _tiled(lambda dxm, g, m_: (dxm * g, jnp.sum(dxm * m_, axis=0, keepdims=True)),
                         [_rows(dx_mid, tr), _const(gate1), _rows(mix, tr)],
                         [((t, d), BF16, (tr, d), lambda i: (i, 0)), ((1, d), F32, (1, d), lambda i: (0, 0))], (t // tr,), "d_resid1", acc=(1,))
    d_wout = _mm_xtdy(z, dm, 1, BF16, "d_w_out")
    dz = _mm_dyw(dm, g_out, F32, "d_merge")

    def merge_bwd(ga, gb, pa, pb, dzv):
        _, vjp = jax.vjp(_f_merge, ga, gb, pa, pb)
        return vjp(dzv)

    dga, dgb, dpa, dpb = _tiled(merge_bwd, [gcol(0), gcol(1), dtile(p_a), dtile(p_b), dtile(dz)],
                                [((t, d), BF16, (tr, td), lambda i, j: (i, j))] * 4, (t // tr, nd_t), "merge_bwd")
    d_wa = _mm_xtdy(y_a, dpa, N_DEV, BF16, "d_w_a")
    d_wb = _mm_xtdy(y_b, dpb, N_DEV, BF16, "d_w_b")
    r_a, r_b, r_out = _exchange_sc([d_wa, d_wb, d_wout.reshape(N_DEV, d // N_DEV, d)], "scatter_mix", True, 4)
    dy_a = _mm_dyw(dpa, g_a, F32, "d_y_a")
    dy_b = _mm_dyw(dpb, g_b, BF16, "d_y_b")

    def readout_bwd(a, b, gt, g, ct):
        _, vjp = jax.vjp(_f_readout, a, b, gt, g)
        da_, _, dgt, dg = vjp(ct)
        return da_, dgt, dg

    hsp = lambda dt: ((t, hw), dt, (trh, HEAD_DIM), lambda i, h: (i, h))
    do_h, d_gate_o, d_hnorm = _tiled(
        readout_bwd, [ospec(o_f), ospec(o_b), (p, (trh, HEAD_DIM), hcol(4, lat_h)), _const(hgrn_norm_g), (dy_a, (trh, HEAD_DIM), lambda i, h: (i, h))],
        [hsp(F32), hsp(BF16), ((1, HEAD_DIM), F32, (1, HEAD_DIM), lambda i, h: (0, 0))], (t // trh, N_HEADS), "readout_bwd", acc=(2,))
    dq1, dfl_f, dv1, dlb_f = _hgrn_bwd(p, lb_f, do_h, 1, False, n_ctx, None)
    dq_h, dfl_b, dv_h, dlb_b = _hgrn_bwd(p, lb_b, do_h, 2, True, n_ctx, (dq1, dv1))

    dqn, dkall, dv_na, dbias = _na_bwd(qn, kall, p, 7, bias, dy_b, n_ctx)

    def qk_bwd(tv, g, cs, sn, ct):
        _, vjp = jax.vjp(lambda a, b: _f_qk(a, b, cs, sn), tv, g)
        return vjp(ct)

    d_pq, d_qnorm = _tiled(qk_bwd, [(p, (tq, HEAD_DIM), hcol(5, lat0)), _const(na_q_norm_g), tab(cos, lat0), tab(sin, lat0),
                                    (dqn, (tq, HEAD_DIM), lambda i, h: (i, h))],
                           [((t, hw), BF16, (tq, HEAD_DIM), lambda i, h: (i, h)), ((1, HEAD_DIM), F32, (1, HEAD_DIM), lambda i, h: (0, 0))],
                           (t // tq, N_HEADS), "q_norm_rope_bwd", acc=(1,))
    d_pk, d_knorm = _tiled(qk_bwd, [(p, (tq, HEAD_DIM), hcol(6)), _const(na_k_norm_g), tab(cos), tab(sin),
                                    (dkall, (tq, HEAD_DIM), lambda i, h: (i, h))],
                           [((tt, hw), BF16, (tq, HEAD_DIM), lambda i, h: (i, h)), ((1, HEAD_DIM), F32, (1, HEAD_DIM), lambda i, h: (0, 0))],
                           (tt // tq, N_HEADS), "k_norm_rope_bwd", acc=(1,))

    zc = lambda w_: jnp.zeros((n_ctx, w_), BF16)
    lat_only = lambda a: jnp.concatenate([zc(a.shape[1]), a], axis=0)
    dp = jnp.concatenate([dq_h, dfl_f, dfl_b, dv_h, lat_only(d_gate_o), lat_only(d_pq), d_pk, dv_na, lat_only(dga), lat_only(dgb)], axis=1)
    dhcat = _mm_dyw(dp, g_in, BF16, "d_hcat")
    zero_ctx = jnp.concatenate([jnp.zeros((n_ctx, d), F32), dx_mid], axis=0)
    dxcat, d_norm1, d_mods1 = _norm_mod_bwd(xcat, norm1_g, mods1, dhcat, zero_ctx, n_ctx_tiles, tr, "norm_mod1_bwd")
    grad_x = dxcat[n_ctx:][None]

    zd = jnp.zeros((1, d), F32)
    dmod_l = jnp.concatenate([d_mods1[1, 0], d_mods1[1, 1], d_gate1, d_mods2[1, 0], d_mods2[1, 1], d_gate2], axis=1)
    dmod_c = jnp.concatenate([d_mods1[0, 0], d_mods1[0, 1], zd, zd, zd, zd], axis=1)
    dmods = jnp.concatenate([dmod_l, dmod_c], axis=0).reshape(2, N_DEV, ca).transpose(1, 0, 2)
    dmods = jnp.pad(dmods, ((0, 0), (0, 6), (0, 0)))
    got = _exchange([dmods], "scatter_dmod", scatter=True)[0]
    dm_rows = jnp.concatenate([got[:, 0], got[:, 1]], axis=0)
    d_ada = _mm_xtdy(act, dm_rows, 1, F32, "d_ada_w")[0]
    back = _mm_dyw(dm_rows, ada16.reshape(1, d, ca), F32, "d_silu_c")
    d_cctx_part = _tiled(lambda b, v: (jnp.sum(b[N_DEV:], axis=0, keepdims=True) * (jax.nn.sigmoid(v) * (1.0 + v * (1.0 - jax.nn.sigmoid(v)))),),
                         [_const(back), _const(c_ctx.reshape(1, d))], [((1, d), F32, (1, d), lambda *_: (0, 0))], (), "d_c_ctx")[0]

    d_rel = _bias_grad(dbias, onehot)
    d_lb_soft = jnp.concatenate([dlb_f, dlb_b], axis=0)
    d_lb0 = _tiled(lambda s, g: (g * s * (1.0 - s),), [_const(lb_soft), _const(d_lb_soft)], [((2, hw), F32, (2, hw), lambda *_: (0, 0))], (), "d_lb")[0]
    d_lb_full = jnp.stack([d_lb0, -d_lb0], axis=1)
    d_cw_l = d_cw[:3].reshape(3, N_DEV, ff_p)[:, :, :ff_l].reshape(1, 3, N_DEV * ff_l)
    d_cb_l = d_cb.reshape(N_DEV, ff_p)[:, :ff_l].reshape(1, N_DEV * ff_l)
    small = [d_cctx_part.reshape(d), (dmod_l + dmod_c), d_norm1, d_norm2, d_lb_full, d_hnorm, d_qnorm, d_knorm, d_rel, d_cw_l, d_cb_l, loss_cols]
    pk = _Pack([a.shape for a in small])
    tot = _sum_parts(_exchange([pk.pack(small)], "gather_small", scatter=False)[0], "sum_small")
    g_cctx, g_ada_b, g_n1, g_n2, g_lb, g_hn, g_qn, g_kn, g_rel, g_cw, g_cb, loss_all = pk.unpack(tot)
    loss = _tiled(lambda v: (jnp.sum(v, axis=1, keepdims=True),), [_const(loss_all)], [((1, 1), F32, (1, 1), lambda *_: (0, 0))], (), "loss_total")[0][0, 0]
    g_lb = lax.dynamic_slice_in_dim(g_lb, me * HEAD_DIM, HEAD_DIM, axis=2)
    g_cw = lax.dynamic_slice_in_dim(g_cw, me * ff_l, ff_l, axis=2)
    small_names = [("c_ctx", g_cctx, c_ctx, m_c_ctx, v_c_ctx), ("ada_b", g_ada_b, ada_b, m_ada_b, v_ada_b),
                   ("norm1_g", g_n1, norm1_g, m_norm1_g, v_norm1_g), ("norm2_g", g_n2, norm2_g, m_norm2_g, v_norm2_g),
                   ("hgrn_lb_logits", g_lb, hgrn_lb_logits, m_hgrn_lb_logits, v_hgrn_lb_logits),
                   ("hgrn_norm_g", g_hn, hgrn_norm_g, m_hgrn_norm_g, v_hgrn_norm_g), ("na_q_norm_g", g_qn, na_q_norm_g, m_na_q_norm_g, v_na_q_norm_g),
                   ("na_k_norm_g", g_kn, na_k_norm_g, m_na_k_norm_g, v_na_k_norm_g), ("na_rel_bias", g_rel, na_rel_bias, m_na_rel_bias, v_na_rel_bias),
                   ("ffn_conv_w", g_cw, ffn_conv_w, m_ffn_conv_w, v_ffn_conv_w), ("ffn_conv_b", g_cb, ffn_conv_b, m_ffn_conv_b, v_ffn_conv_b)]
    pk2 = _Pack([s[1].shape for s in small_names])
    sd, sm, sv = _adam_small(*[pk2.pack([s[i] for s in small_names]) for i in (1, 2, 3, 4)])
    sd, sm, sv = pk2.unpack(sd), pk2.unpack(sm), pk2.unpack(sv)
    res = {s[0]: (s[1], sd[i], sm[i], sv[i]) for i, s in enumerate(small_names)}

    with _after(sd[0]):
        d_win = _mm_xtdy(hcat, dp, N_DEV, BF16, "d_w_in")
    r_in, = _exchange_sc([d_win], "scatter_w_in", True, 5)
    with _after(d_win):
        res["ada_w"] = _adam_big(d_ada[None], ada_w[0], m_ada_w[0], v_ada_w[0], "adam_ada")
    res["ffn_w1"] = _adam_big(r_w1, ffn_w1[0], m_ffn_w1[0], v_ffn_w1[0], "adam_w1")
    res["ffn_w3"] = _adam_big(r_w3, ffn_w3[0], m_ffn_w3[0], v_ffn_w3[0], "adam_w3")
    res["ffn_w2"] = _adam_big(r_w2, ffn_w2[0], m_ffn_w2[0], v_ffn_w2[0], "adam_w2")
    res["w_branch_a"] = _adam_big(r_a, w_branch_a[0], m_w_branch_a[0], v_w_branch_a[0], "adam_w_a")
    res["w_branch_b"] = _adam_big(r_b, w_branch_b[0], m_w_branch_b[0], v_w_branch_b[0], "adam_w_b")
    res["w_out"] = _adam_big(r_out, w_out[0], m_w_out[0], v_w_out[0], "adam_w_out")
    res["w_in"] = _adam_big(r_in, w_in[0], m_w_in[0], v_w_in[0], "adam_w_in")
    for k in ("w_in", "w_branch_a", "w_branch_b", "w_out", "ffn_w1", "ffn_w3", "ffn_w2", "ada_w"):
        res[k] = tuple(a[None] for a in res[k])

    order = ["c_ctx", "ada_w", "ada_b", "norm1_g", "norm2_g", "w_in", "hgrn_lb_logits", "hgrn_norm_g", "na_q_norm_g", "na_k_norm_g",
             "na_rel_bias", "w_branch_a", "w_branch_b", "w_out", "ffn_w1", "ffn_w3", "ffn_conv_w", "ffn_conv_b", "ffn_w2"]
    shapes = {"c_ctx": c_ctx.shape, "ada_b": ada_b.shape, "norm1_g": norm1_g.shape, "norm2_g": norm2_g.shape,
              "hgrn_lb_logits": hgrn_lb_logits.shape, "hgrn_norm_g": hgrn_norm_g.shape, "na_q_norm_g": na_q_norm_g.shape,
              "na_k_norm_g": na_k_norm_g.shape, "na_rel_bias": na_rel_bias.shape, "ffn_conv_w": ffn_conv_w.shape, "ffn_conv_b": ffn_conv_b.shape}
    outs = [loss, grad_x]
    for part in range(4):
        for k in order:
            a = res[k][part]
            outs.append(a.reshape(shapes[k]) if k in shapes else a)
    return tuple(outs)
```

```python
import functools

import numpy as np
import jax
import jax.numpy as jnp
from jax import lax
from jax.experimental import pallas as pl
from jax.experimental.pallas import tpu as pltpu
from jax.experimental.pallas import tpu_sc as plsc

F32 = jnp.float32
BF16 = jnp.bfloat16
HIGHEST = lax.Precision.HIGHEST

N_DEV = 8
MESH_ID = pl.DeviceIdType.MESH
LANE = 128
HEAD_DIM = 128
N_HEADS = 8
GRID_W = 64
WIN_R = 8
WIN_C = 16
ROPE_THETA = 10000.0
EPS = 1e-6
N_MOD = 6
HGRN_BLOCK = 16
NEG_BIG = -1e30
VMEM_LIMIT = 56 << 20

ADAM_LR = 0.001
ADAM_B1 = 0.9
ADAM_B2 = 0.999
ADAM_EPS = 1e-08
ADAM_WD = 0.01
ADAM_STEP = 10

HBM_SPEC = pl.BlockSpec(memory_space=pltpu.HBM)


_ORDER_AFTER = []


class _after:
    def __init__(self, *arrs):
        self.arrs = list(arrs)

    def __enter__(self):
        _ORDER_AFTER.extend(self.arrs)

    def __exit__(self, *exc):
        del _ORDER_AFTER[:]


def _pcall(body, *, name, out_shape, grid=None, in_specs=None, out_specs=None, scratch=(), aliases=None):
    kw = {}
    if grid is not None:
        kw["grid"] = grid
    extra = []
    if _ORDER_AFTER and in_specs is not None:
        extra = list(_ORDER_AFTER)
        del _ORDER_AFTER[:]
        n_in, n_extra, inner = len(in_specs), len(extra), body
        in_specs = list(in_specs) + [pl.BlockSpec(memory_space=pl.ANY)] * n_extra

        def body(*refs):
            return inner(*refs[:n_in], *refs[n_in + n_extra:])

    if extra:
        call = _pcall_inner(body, name, out_shape, kw, in_specs, out_specs, scratch, aliases)
        return lambda *args: call(*args, *extra)
    return _pcall_inner(body, name, out_shape, kw, in_specs, out_specs, scratch, aliases)


def _pcall_inner(body, name, out_shape, kw, in_specs, out_specs, scratch, aliases):
    if in_specs is not None:
        kw["in_specs"] = in_specs
    if out_specs is not None:
        kw["out_specs"] = out_specs
    if scratch:
        kw["scratch_shapes"] = list(scratch)
    if aliases:
        kw["input_output_aliases"] = aliases
    return pl.pallas_call(body, name=name, out_shape=out_shape,
                          compiler_params=pltpu.CompilerParams(vmem_limit_bytes=VMEM_LIMIT), **kw)


def _pick(dim, cands):
    for c in cands:
        if c <= dim and dim % c == 0:
            return c
    return dim


def _sds(shape, dtype):
    return jax.ShapeDtypeStruct(tuple(shape), dtype)


def _peers():
    x, y, c = lax.axis_index("x"), lax.axis_index("y"), lax.axis_index("c")
    out = []
    for k in range(1, N_DEV):
        px = 1 - x if (k >> 2) & 1 else x
        py = 1 - y if (k >> 1) & 1 else y
        pc = 1 - c if k & 1 else c
        out.append((k, (px, py, pc), 4 * px + 2 * py + pc))
    return 4 * x + 2 * y + c, out


def _exchange(arrs, name, scatter):
    n = len(arrs)

    def body(*refs):
        ins, outs = refs[:n], refs[n:2 * n]
        send, recv, loc = refs[2 * n:]
        me, peers = _peers()
        started = []
        for i in range(n):
            src = ins[i].at[me] if scatter else ins[i]
            cp = pltpu.make_async_copy(src, outs[i].at[me], loc.at[i])
            cp.start()
            started.append(cp)
        sends = []
        for k, peer, pidx in peers:
            for i in range(n):
                src = ins[i].at[pidx] if scatter else ins[i]
                cp = pltpu.make_async_remote_copy(src_ref=src, dst_ref=outs[i].at[me], send_sem=send.at[i * 7 + k - 1],
                                                  recv_sem=recv.at[i * 7 + k - 1], device_id=peer, device_id_type=MESH_ID)
                cp.start()
                sends.append(cp)
        for k, peer, pidx in peers:
            for i in range(n):
                src = ins[i].at[pidx] if scatter else ins[i]
                pltpu.make_async_remote_copy(src_ref=src, dst_ref=outs[i].at[pidx], send_sem=send.at[i * 7 + k - 1],
                                             recv_sem=recv.at[i * 7 + k - 1], device_id=peer, device_id_type=MESH_ID).wait_recv()
        for cp in sends:
            cp.wait_send()
        for cp in started:
            cp.wait()

    out_shape = [_sds(a.shape if scatter else (N_DEV,) + a.shape, a.dtype) for a in arrs]
    res = _pcall(body, name=name, out_shape=out_shape, in_specs=[HBM_SPEC] * n, out_specs=[HBM_SPEC] * n,
                 scratch=[pltpu.SemaphoreType.DMA((7 * n,)), pltpu.SemaphoreType.DMA((7 * n,)), pltpu.SemaphoreType.DMA((n,))])(*arrs)
    return list(res)


def _exchange_sc(arrs, name, scatter, collective_id):
    n = len(arrs)
    srcs = [jax.new_ref(a, memory_space=pltpu.MemorySpace.HBM) for a in arrs]
    lands = [jax.empty_ref(_sds(a.shape if scatter else (N_DEV,) + a.shape, a.dtype), memory_space=pltpu.MemorySpace.HBM) for a in arrs]

    @pl.kernel(mesh=plsc.ScalarSubcoreMesh(axis_name="seq", num_cores=1), name=name,
               scratch_types=(pltpu.SemaphoreType.DMA((7 * n,)), pltpu.SemaphoreType.DMA((7 * n,)), pltpu.SemaphoreType.DMA((n,))),
               compiler_params=pltpu.CompilerParams(collective_id=collective_id))
    def launch(send, recv, loc):
        me, peers = _peers()
        barrier = pltpu.get_barrier_semaphore()
        for _, peer, _ in peers:
            pl.semaphore_signal(barrier, inc=1, device_id=peer, device_id_type=MESH_ID)
        pl.semaphore_wait(barrier, N_DEV - 1)
        own = [pltpu.make_async_copy(srcs[i].at[me] if scatter else srcs[i], lands[i].at[me], loc.at[i]) for i in range(n)]
        for cp in own:
            cp.start()
        sends = []
        for k, peer, pidx in peers:
            for i in range(n):
                src = srcs[i].at[pidx] if scatter else srcs[i]
                cp = pltpu.make_async_remote_copy(src_ref=src, dst_ref=lands[i].at[me], send_sem=send.at[i * 7 + k - 1],
                                                  recv_sem=recv.at[i * 7 + k - 1], device_id=peer, device_id_type=MESH_ID)
                cp.start()
                sends.append(cp)
        for k, peer, pidx in peers:
            for i in range(n):
                src = srcs[i].at[pidx] if scatter else srcs[i]
                pltpu.make_async_remote_copy(src_ref=src, dst_ref=lands[i].at[pidx], send_sem=send.at[i * 7 + k - 1],
                                             recv_sem=recv.at[i * 7 + k - 1], device_id=peer, device_id_type=MESH_ID).wait_recv()
        for cp in sends:
            cp.wait_send()
        for cp in own:
            cp.wait()

    launch()
    return [r[...] for r in lands]


def _gather_sc(arrs, name, collective_id):
    n = len(arrs)
    srcs = [jax.new_ref(a, memory_space=pltpu.MemorySpace.HBM) for a in arrs]
    lands = [jax.empty_ref(_sds((N_DEV,) + a.shape, a.dtype), memory_space=pltpu.MemorySpace.HBM) for a in arrs]

    @pl.kernel(mesh=plsc.ScalarSubcoreMesh(axis_name="seq", num_cores=1), name=name,
               scratch_types=(pltpu.SemaphoreType.DMA((7 * n,)), pltpu.SemaphoreType.DMA((7 * n,)), pltpu.SemaphoreType.DMA((n,))),
               compiler_params=pltpu.CompilerParams(collective_id=collective_id))
    def launch(send, recv, loc):
        x, y, c = lax.axis_index("x"), lax.axis_index("y"), lax.axis_index("c")
        me, sibling = (x, y, c), (x, y, 1 - c)
        chips = [(1 - x, y), (x, 1 - y), (1 - x, 1 - y)]
        index = lambda px, py, pc: 4 * px + 2 * py + pc
        barrier = pltpu.get_barrier_semaphore()
        for peer in [sibling] + [(*chip, c) for chip in chips]:
            pl.semaphore_signal(barrier, inc=1, device_id=peer, device_id_type=MESH_ID)
        pl.semaphore_wait(barrier, 4)

        def copy(i, k, block, to, from_src):
            return pltpu.make_async_remote_copy(src_ref=srcs[i] if from_src else lands[i].at[index(*block)], dst_ref=lands[i].at[index(*block)],
                                                send_sem=send.at[i * 7 + k], recv_sem=recv.at[i * 7 + k], device_id=to, device_id_type=MESH_ID)

        own = [pltpu.make_async_copy(srcs[i], lands[i].at[index(*me)], loc.at[i]) for i in range(n)]
        for cp in own:
            cp.start()
        started = []
        for i in range(n):
            started.append(copy(i, 0, me, sibling, True))
            started += [copy(i, 1 + j, me, (*chip, c), True) for j, chip in enumerate(chips)]
        for cp in started:
            cp.start()
        for j, chip in enumerate(chips):
            for i in range(n):
                copy(i, 1 + j, (*chip, c), me, False).wait_recv()
                fwd = copy(i, 4 + j, (*chip, c), sibling, False)
                fwd.start()
                started.append(fwd)
        for i in range(n):
            copy(i, 0, sibling, me, False).wait_recv()
            for j, chip in enumerate(chips):
                copy(i, 4 + j, (*chip, 1 - c), me, False).wait_recv()
        for cp in started:
            cp.wait_send()
        for cp in own:
            cp.wait()

    launch()
    return [r[...] for r in lands]


def _pair_swap_sc(blocks, name, collective_id):
    _, r, c_ = blocks.shape
    src = jax.new_ref(blocks, memory_space=pltpu.MemorySpace.HBM)
    land = jax.empty_ref(_sds((4, r, c_), blocks.dtype), memory_space=pltpu.MemorySpace.HBM)

    @pl.kernel(mesh=plsc.ScalarSubcoreMesh(axis_name="seq", num_cores=1), name=name,
               scratch_types=(pltpu.SemaphoreType.DMA((4,)), pltpu.SemaphoreType.DMA((4,))),
               compiler_params=pltpu.CompilerParams(collective_id=collective_id))
    def launch(send, recv):
        x, y, c = lax.axis_index("x"), lax.axis_index("y"), lax.axis_index("c")
        sibling = (x, y, 1 - c)
        barrier = pltpu.get_barrier_semaphore()
        pl.semaphore_signal(barrier, inc=1, device_id=sibling, device_id_type=MESH_ID)
        pl.semaphore_wait(barrier, 1)
        copies = [pltpu.make_async_remote_copy(src_ref=src.at[2 * q + 1 - c], dst_ref=land.at[q], send_sem=send.at[q], recv_sem=recv.at[q],
                                               device_id=sibling, device_id_type=MESH_ID) for q in range(4)]
        for cp in copies:
            cp.start()
        for cp in copies:
            cp.wait()

    launch()
    return land[...]


def _pair_add(blocks, theirs):
    _, r, c_ = blocks.shape
    mine = lax.dynamic_index_in_dim(blocks.reshape(4, 2, r, c_), lax.axis_index("c"), axis=1, keepdims=False)
    tr = _pick(r, (256, 128, 64, 16))
    spec = (None, tr, c_)
    return _tiled(lambda a, b: (a.astype(F32) + b.astype(F32),), [(mine, spec, lambda q, i: (q, i, 0)), (theirs, spec, lambda q, i: (q, i, 0))],
                  [((4, r, c_), BF16, spec, lambda q, i: (q, i, 0))], (4, r // tr), "pair_add")[0]


def _chip_scatter_sc(sums, name, collective_id):
    _, r, c_ = sums.shape
    src = jax.new_ref(sums, memory_space=pltpu.MemorySpace.HBM)
    land = jax.empty_ref(_sds((4, r, c_), sums.dtype), memory_space=pltpu.MemorySpace.HBM)

    @pl.kernel(mesh=plsc.ScalarSubcoreMesh(axis_name="seq", num_cores=1), name=name,
               scratch_types=(pltpu.SemaphoreType.DMA((3,)), pltpu.SemaphoreType.DMA((3,)), pltpu.SemaphoreType.DMA),
               compiler_params=pltpu.CompilerParams(collective_id=collective_id))
    def launch(send, recv, loc):
        x, y, c = lax.axis_index("x"), lax.axis_index("y"), lax.axis_index("c")
        chips = [(1 - x, y), (x, 1 - y), (1 - x, 1 - y)]
        my_chip = 2 * x + y
        barrier = pltpu.get_barrier_semaphore()
        for chip in chips:
            pl.semaphore_signal(barrier, inc=1, device_id=(*chip, c), device_id_type=MESH_ID)
        pl.semaphore_wait(barrier, 3)
        own = pltpu.make_async_copy(src.at[my_chip], land.at[my_chip], loc)
        own.start()
        sends = [pltpu.make_async_remote_copy(src_ref=src.at[2 * px + py], dst_ref=land.at[my_chip], send_sem=send.at[j], recv_sem=recv.at[j],
                                              device_id=(px, py, c), device_id_type=MESH_ID) for j, (px, py) in enumerate(chips)]
        for cp in sends:
            cp.start()
        for j, (px, py) in enumerate(chips):
            pltpu.make_async_remote_copy(src_ref=src.at[my_chip], dst_ref=land.at[2 * px + py], send_sem=send.at[j], recv_sem=recv.at[j],
                                         device_id=(px, py, c), device_id_type=MESH_ID).wait_recv()
        for cp in sends:
            cp.wait_send()
        own.wait()

    launch()
    return land[...]


class _InFlight:
    def __init__(self, send, recv, srcs, lands, token, scatter):
        self.send, self.recv, self.srcs, self.lands, self.token, self.scatter = send, recv, srcs, lands, token, scatter


SEM_SPEC = pl.BlockSpec(memory_space=pltpu.SEMAPHORE)
SIDE_EFFECT = pltpu.SideEffectType.DATAFLOW_SIDE_EFFECTING


def _exchange_start(arrs, name, scatter):
    n = len(arrs)

    def body(*refs):
        ins, lands = refs[:n], refs[n:2 * n]
        send, recv = refs[2 * n], refs[2 * n + 1]
        token = refs[4 * n + 2]
        loc = refs[4 * n + 3]
        me, peers = _peers()
        own = [pltpu.make_async_copy(ins[i].at[me] if scatter else ins[i], lands[i].at[me], loc.at[i]) for i in range(n)]
        for cp in own:
            cp.start()
        for cp in own:
            cp.wait()
        for k, peer, pidx in peers:
            for i in range(n):
                src = ins[i].at[pidx] if scatter else ins[i]
                pltpu.make_async_remote_copy(src_ref=src, dst_ref=lands[i].at[me], send_sem=send.at[i * 7 + k - 1],
                                             recv_sem=recv.at[i * 7 + k - 1], device_id=peer, device_id_type=MESH_ID).start()
        token[...] = jnp.zeros_like(token)

    land_shapes = [a.shape if scatter else (N_DEV,) + a.shape for a in arrs]
    hbm = lambda a: pltpu.with_memory_space_constraint(a, pltpu.HBM)
    args = [hbm(a) for a in arrs] + [hbm(lax.empty(s, a.dtype)) for s, a in zip(land_shapes, arrs)]
    out_shape = ([pltpu.SemaphoreType.DMA((7 * n,)), pltpu.SemaphoreType.DMA((7 * n,))]
                 + [pltpu.HBM(a.shape, a.dtype) for a in arrs] + [pltpu.HBM(s, a.dtype) for s, a in zip(land_shapes, arrs)]
                 + [_sds((8, LANE), F32)])
    res = pl.pallas_call(
        body, name=name, out_shape=out_shape, in_specs=[HBM_SPEC] * (2 * n),
        out_specs=[SEM_SPEC, SEM_SPEC] + [HBM_SPEC] * (2 * n) + [pl.BlockSpec(memory_space=pltpu.VMEM)],
        input_output_aliases={i: 2 + i for i in range(2 * n)},
        scratch_shapes=[pltpu.SemaphoreType.DMA((n,))],
        compiler_params=pltpu.CompilerParams(has_side_effects=SIDE_EFFECT))(*args)
    return _InFlight(res[0], res[1], list(res[2:2 + n]), list(res[2 + n:2 + 2 * n]), res[2 + 2 * n], scatter)


def _exchange_wait(h, after, name):
    n = len(h.srcs)
    scatter = h.scatter
    after = list(after)

    def body(*refs):
        ins, lands = refs[:n], refs[n:2 * n]
        send, recv = refs[2 * n], refs[2 * n + 1]
        _, peers = _peers()
        for k, peer, pidx in peers:
            for i in range(n):
                src = ins[i].at[pidx] if scatter else ins[i]
                cp = pltpu.make_async_remote_copy(src_ref=src, dst_ref=lands[i].at[pidx], send_sem=send.at[i * 7 + k - 1],
                                                  recv_sem=recv.at[i * 7 + k - 1], device_id=peer, device_id_type=MESH_ID)
                cp.wait_send()
                cp.wait_recv()

    res = pl.pallas_call(
        body, name=name, out_shape=[pltpu.HBM(a.shape, a.dtype) for a in h.srcs + h.lands],
        in_specs=[HBM_SPEC] * (2 * n) + [SEM_SPEC, SEM_SPEC] + [pl.BlockSpec(memory_space=pl.ANY)] * len(after),
        out_specs=[HBM_SPEC] * (2 * n), input_output_aliases={i: i for i in range(2 * n)},
        compiler_params=pltpu.CompilerParams(has_side_effects=SIDE_EFFECT))(*h.srcs, *h.lands, h.send, h.recv, *after)
    return list(res[n:])


def _mm_xw(x, g, out_dtype, name, tm_c=(768, 512, 384, 256, 128, 64, 16), tn_c=(1024, 768, 512, 256, 128), tk_c=(2048, 1024, 768, 512, 256)):
    m, r = x.shape
    nb, r2, cl = g.shape
    assert r == r2
    tm, tn, tk = _pick(m, tm_c), _pick(cl, tn_c), _pick(r, tk_c)
    q, nk = cl // tn, r // tk

    def body(x_ref, g_ref, o_ref, *acc):
        p = lax.dot_general(x_ref[...].astype(BF16), g_ref[...], (((1,), (0,)), ((), ())), preferred_element_type=F32)
        if nk == 1:
            o_ref[...] = p.astype(o_ref.dtype)
        else:
            k = pl.program_id(2)

            @pl.when(k == 0)
            def _():
                acc[0][...] = p

            @pl.when(k > 0)
            def _():
                acc[0][...] += p

            @pl.when(k == nk - 1)
            def _():
                o_ref[...] = acc[0][...].astype(o_ref.dtype)

    return _pcall(
        body, name=name, grid=(m // tm, nb * q, nk),
        in_specs=[pl.BlockSpec((tm, tk), lambda i, j, k: (i, k)), pl.BlockSpec((None, tk, tn), lambda i, j, k: (j // q, k, j % q))],
        out_specs=pl.BlockSpec((tm, tn), lambda i, j, k: (i, j)),
        out_shape=_sds((m, nb * cl), out_dtype),
        scratch=[] if nk == 1 else [pltpu.VMEM((tm, tn), F32)])(x, g)


def _mm_dyw(dy, g, out_dtype, name, init=None, tm_c=(768, 512, 384, 256, 128), tn_c=(1024, 512, 256, 128), tk_c=(2048, 1536, 1024, 768, 512, 256, 128)):
    m, n = dy.shape
    nb, r, cl = g.shape
    assert n == nb * cl
    tm, tn, tk = _pick(m, tm_c), _pick(r, tn_c), _pick(cl, tk_c)
    q = cl // tk
    nk = nb * q
    has_init = init is not None

    def body(*refs):
        if has_init:
            dy_ref, g_ref, i_ref, o_ref, acc = refs
        else:
            dy_ref, g_ref, o_ref, acc = refs
        k = pl.program_id(2)
        p = lax.dot_general(dy_ref[...].astype(BF16), g_ref[...], (((1,), (1,)), ((), ())), preferred_element_type=F32)

        @pl.when(k == 0)
        def _():
            acc[...] = p + i_ref[...].astype(F32) if has_init else p

        @pl.when(k > 0)
        def _():
            acc[...] += p

        @pl.when(k == nk - 1)
        def _():
            o_ref[...] = acc[...].astype(o_ref.dtype)

    in_specs = [pl.BlockSpec((tm, tk), lambda i, j, k: (i, k)), pl.BlockSpec((None, tn, tk), lambda i, j, k: (k // q, j, k % q))]
    args = [dy, g]
    if has_init:
        in_specs.append(pl.BlockSpec((tm, tn), lambda i, j, k: (i, j)))
        args.append(init)
    return _pcall(body, name=name, grid=(m // tm, r // tn, nk), in_specs=in_specs,
                  out_specs=pl.BlockSpec((tm, tn), lambda i, j, k: (i, j)), out_shape=_sds((m, r), out_dtype),
                  scratch=[pltpu.VMEM((tm, tn), F32)])(*args)


def _mm_xtdy(x, dy, nb, out_dtype, name, tm_c=(1024, 512, 256, 128), tn_c=(768, 512, 256, 128), tk_c=(2304, 2048, 1152, 1024, 768, 512, 256, 128, 16)):
    t, r = x.shape
    t2, n = dy.shape
    assert t == t2 and n % nb == 0
    cl = n // nb
    tm, tn, tk = _pick(r, tm_c), _pick(cl, tn_c), _pick(t, tk_c)
    q, nk = cl // tn, t // tk

    def body(x_ref, dy_ref, o_ref, *acc):
        p = lax.dot_general(x_ref[...].astype(BF16), dy_ref[...].astype(BF16), (((0,), (0,)), ((), ())), preferred_element_type=F32)
        if nk == 1:
            o_ref[...] = p.astype(o_ref.dtype)
        else:
            k = pl.program_id(2)

            @pl.when(k == 0)
            def _():
                acc[0][...] = p

            @pl.when(k > 0)
            def _():
                acc[0][...] += p

            @pl.when(k == nk - 1)
            def _():
                o_ref[...] = acc[0][...].astype(o_ref.dtype)

    return _pcall(
        body, name=name, grid=(r // tm, nb * q, nk),
        in_specs=[pl.BlockSpec((tk, tm), lambda i, j, k: (k, i)), pl.BlockSpec((tk, tn), lambda i, j, k: (k, j))],
        out_specs=pl.BlockSpec((None, tm, tn), lambda i, j, k: (j // q, i, j % q)),
        out_shape=_sds((nb, r, cl), out_dtype),
        scratch=[] if nk == 1 else [pltpu.VMEM((tm, tn), F32)])(x, dy)


def _mm_f32(a, b, name, trans_b=False):
    dims = (((1,), (1,)), ((), ())) if trans_b else (((1,), (0,)), ((), ()))
    n = b.shape[0] if trans_b else b.shape[1]

    def body(a_ref, b_ref, o_ref):
        o_ref[...] = lax.dot_general(a_ref[...], b_ref[...], dims, precision=HIGHEST, preferred_element_type=F32)

    return _pcall(body, name=name, out_shape=_sds((a.shape[0], n), F32))(a, b)


def _tiled(fn, ins, outs, grid, name, acc=()):
    n_in = len(ins)
    grid = tuple(grid) or (1,)
    nd = len(grid)

    def body(*refs):
        vals = fn(*[r[...] for r in refs[:n_in]])
        if not isinstance(vals, (tuple, list)):
            vals = (vals,)
        first = None
        for o, (ref, v) in enumerate(zip(refs[n_in:], vals)):
            if o in acc:
                if first is None:
                    first = pl.program_id(0) == 0
                    for a in range(1, nd):
                        first = jnp.logical_and(first, pl.program_id(a) == 0)

                @pl.when(first)
                def _(ref=ref, v=v):
                    ref[...] = v.astype(ref.dtype)

                @pl.when(jnp.logical_not(first))
                def _(ref=ref, v=v):
                    ref[...] += v.astype(ref.dtype)
            else:
                ref[...] = v.astype(ref.dtype)

    res = _pcall(body, name=name, grid=grid,
                 in_specs=[pl.BlockSpec(b, im) for _, b, im in ins],
                 out_specs=[pl.BlockSpec(b, im) for _, _, b, im in outs],
                 out_shape=[_sds(s, d) for s, d, _, _ in outs])(*[a for a, _, _ in ins])
    return list(res)


def _rows(a, tr):
    return (a, (tr, a.shape[1]), lambda i, *_: (i, 0))


def _const(a):
    nd = a.ndim
    return (a, a.shape, lambda *_: (0,) * nd)


def _cast_bf16(w, name):
    r, c = w.shape
    tr = _pick(r, (256, 128, 64, 16))
    return _tiled(lambda v: v, [_rows(w, tr)], [((r, c), BF16, (tr, c), lambda i: (i, 0))], (r // tr,), name)[0]


def _f_norm_mod(x, g, sh, sc):
    y = x * lax.rsqrt(jnp.mean(x * x, axis=-1, keepdims=True) + EPS) * g
    return y * (1.0 + sc) + sh


def _rope_partner_impl(y):
    nf = HEAD_DIM // 4
    lane = lax.broadcasted_iota(jnp.int32, y.shape, 1)
    return jnp.where(lane % (2 * nf) < nf, pltpu.roll(y, HEAD_DIM - nf, 1), pltpu.roll(y, nf, 1))


_rope_partner = jax.custom_vjp(_rope_partner_impl)
_rope_partner.defvjp(lambda y: (_rope_partner_impl(y), None), lambda _, ct: (_rope_partner_impl(ct),))


def _f_qk(t, g, cos, sin):
    y = t * lax.rsqrt(jnp.mean(t * t, axis=-1, keepdims=True) + EPS) * g
    return y * cos + _rope_partner(y) * sin


def _f_readout(of, ob, gate, g):
    o = of + ob
    on = o * lax.rsqrt(jnp.mean(o * o, axis=-1, keepdims=True) + EPS) * g
    return on * (gate * jax.nn.sigmoid(gate))


def _f_merge(ga, gb, pa, pb):
    return jax.nn.sigmoid(ga) * pa + jax.nn.sigmoid(gb) * pb


def _f_resid(x, gate, m):
    return x + gate * m


def _norm_mod_fwd(xcat, g, mods, n_ctx_tiles, tr):
    tt, d = xcat.shape
    which = lambda i: (jnp.where(i >= n_ctx_tiles, 1, 0), 0, 0, 0)

    def fn(x, gg, md):
        return _f_norm_mod(x, gg, md[0], md[1])

    return _tiled(fn, [_rows(xcat, tr), _const(g), (mods, (None, 2, 1, d), which)],
                  [((tt, d), BF16, (tr, d), lambda i: (i, 0))], (tt // tr,), "norm_mod_fwd")[0]


def _norm_mod_bwd(xcat, g, mods, dh, extra, n_ctx_tiles, tr, name):
    tt, d = xcat.shape
    nt = tt // tr
    has_extra = extra is not None

    def body(*refs):
        if has_extra:
            x_ref, g_ref, m_ref, dh_ref, e_ref, dx_ref, dg_ref, dm_ref = refs
        else:
            x_ref, g_ref, m_ref, dh_ref, dx_ref, dg_ref, dm_ref = refs
        i = pl.program_id(0)
        md = m_ref[...]
        _, vjp = jax.vjp(_f_norm_mod, x_ref[...], g_ref[...], md[0], md[1])
        dx, dg, dsh, dsc = vjp(dh_ref[...].astype(F32))
        dx_ref[...] = dx + e_ref[...] if has_extra else dx

        @pl.when(i == 0)
        def _():
            dg_ref[...] = dg

        @pl.when(i > 0)
        def _():
            dg_ref[...] += dg

        fresh = jnp.logical_or(i == 0, i == n_ctx_tiles)

        @pl.when(fresh)
        def _():
            dm_ref[0] = dsh
            dm_ref[1] = dsc

        @pl.when(jnp.logical_not(fresh))
        def _():
            dm_ref[0] += dsh
            dm_ref[1] += dsc

    which = lambda i: (jnp.where(i >= n_ctx_tiles, 1, 0), 0, 0, 0)
    row = pl.BlockSpec((tr, d), lambda i: (i, 0))
    in_specs = [row, pl.BlockSpec((1, d), lambda i: (0, 0)), pl.BlockSpec((None, 2, 1, d), which), row]
    args = [xcat, g, mods, dh]
    if has_extra:
        in_specs.append(row)
        args.append(extra)
    return _pcall(body, name=name, grid=(nt,), in_specs=in_specs,
                  out_specs=[row, pl.BlockSpec((1, d), lambda i: (0, 0)), pl.BlockSpec((None, 2, 1, d), which)],
                  out_shape=[_sds((tt, d), F32), _sds((1, d), F32), _sds((2, 2, 1, d), F32)])(*args)


def _hgrn_tri(reverse):
    t = np.arange(HGRN_BLOCK)
    tri = (t[None, :] >= t[:, None]) if reverse else (t[None, :] <= t[:, None])
    tri = tri.astype(np.float32)
    return jnp.asarray(tri), jnp.asarray(tri.T.copy())


def _hgrn_rowblock(n, n_ctx_blocks, n_blocks, reverse):
    if not reverse:
        return n
    return jnp.where(n < n_ctx_blocks, n_ctx_blocks - 1 - n, n_blocks - 1 - n + n_ctx_blocks)


def _hgrn_gates(fl, lb):
    sg = jax.nn.sigmoid(fl)
    f = lb + (1.0 - lb) * sg
    return sg, f, jnp.log(f), 1.0 - f


def _hgrn_intra_mask(reverse):
    tio = lax.broadcasted_iota(jnp.int32, (HGRN_BLOCK, HEAD_DIM), 0)
    return (lambda s: tio <= s) if reverse else (lambda s: tio >= s)


def _unrolled(nblk, u, fn, init):
    assert nblk % u == 0

    def trip(b, c):
        for j in range(u):
            c = fn(b * u + j, j, c)
        return c

    return lax.fori_loop(0, nblk // u, trip, init)


HGRN_UNROLL_FWD = (4, 8)
HGRN_UNROLL_BWD = (8, 4)


def _hgrn_fwd(p, lb, seg_f, reverse, n_ctx_rows):
    tt = p.shape[0]
    hb = HGRN_BLOCK
    nblk, nctx = tt // hb, n_ctx_rows // hb
    tri, _ = _hgrn_tri(reverse)
    u1, u3 = HGRN_UNROLL_FWD

    def body(q_ref, f_ref, v_ref, lb_ref, tri_ref, o_ref, st_all, dec_all, qe_buf, cum_blk, k_blk, v_blk):
        mask = _hgrn_intra_mask(reverse)
        lbv = lb_ref[...]

        def phase1(n, slot, c):
            r0 = pl.multiple_of(_hgrn_rowblock(n, nctx, nblk, reverse) * hb, hb)
            q, v = q_ref[pl.ds(r0, hb), :], v_ref[pl.ds(r0, hb), :]
            _, f, g, k = _hgrn_gates(f_ref[pl.ds(r0, hb), :], lbv)
            cum = jnp.dot(tri_ref[...], g, precision=HIGHEST, preferred_element_type=F32)
            tot = jnp.sum(g, axis=0, keepdims=True)
            cum_blk[slot] = cum
            k_blk[slot] = k
            v_blk[slot] = v
            oi = jnp.zeros((hb, HEAD_DIM), F32)
            for s in range(hb):
                e = jnp.where(mask(s), jnp.exp(jnp.minimum(cum - cum_blk[slot, s:s + 1, :], 0.0)), 0.0)
                a_s = jnp.sum(q * e * k_blk[slot, s:s + 1, :], axis=-1, keepdims=True)
                oi = oi + a_s * v_blk[slot, s:s + 1, :]
            o_ref[pl.ds(r0, hb), :] = oi
            qe_buf[pl.ds(r0, hb), :] = q * jnp.exp(cum)
            kl = k * jnp.exp(tot - cum)
            st_all[n] = lax.dot_general(v.astype(BF16), kl.astype(BF16), (((0,), (0,)), ((), ())), preferred_element_type=F32)
            dec_all[pl.ds(n, 1), :] = jnp.exp(tot)
            return c

        _unrolled(nblk, u1, phase1, 0)

        def phase2(n, st):
            kv = st_all[n]
            st_all[n] = st
            return st * dec_all[pl.ds(n, 1), :] + kv

        lax.fori_loop(0, nblk, phase2, jnp.zeros((HEAD_DIM, HEAD_DIM), F32))

        def phase3(n, slot, c):
            r0 = pl.multiple_of(_hgrn_rowblock(n, nctx, nblk, reverse) * hb, hb)
            o_ref[pl.ds(r0, hb), :] += lax.dot_general(qe_buf[pl.ds(r0, hb), :].astype(BF16), st_all[n].astype(BF16),
                                                        (((1,), (1,)), ((), ())), preferred_element_type=F32)
            return c

        _unrolled(nblk, u3, phase3, 0)

    col = lambda seg: pl.BlockSpec((tt, HEAD_DIM), lambda h, seg=seg: (0, seg * N_HEADS + h))
    blk = pltpu.VMEM((u1, hb, HEAD_DIM), F32)
    return _pcall(
        body, name="hgrn_fwd_rev" if reverse else "hgrn_fwd", grid=(N_HEADS,),
        in_specs=[col(0), col(seg_f), col(3), pl.BlockSpec((1, HEAD_DIM), lambda h: (0, h)), pl.BlockSpec((hb, hb), lambda h: (0, 0))],
        out_specs=pl.BlockSpec((tt, HEAD_DIM), lambda h: (0, h)),
        out_shape=_sds((tt, N_HEADS * HEAD_DIM), F32),
        scratch=[pltpu.VMEM((nblk, HEAD_DIM, HEAD_DIM), F32), pltpu.VMEM((nblk, HEAD_DIM), F32), pltpu.VMEM((tt, HEAD_DIM), F32),
                 blk, blk, blk])(p, p, p, lb, tri)


def _hgrn_bwd(p, lb, do, seg_f, reverse, n_ctx_rows, prev):
    tt = p.shape[0]
    hb = HGRN_BLOCK
    nblk, nctx = tt // hb, n_ctx_rows // hb
    tri, tri_t = _hgrn_tri(reverse)
    last_row = 0 if reverse else hb - 1
    has_prev = prev is not None
    u1, u3 = HGRN_UNROLL_BWD

    def body(*refs):
        q_ref, f_ref, v_ref, lb_ref, tri_ref, trit_ref, do_ref = refs[:7]
        refs = refs[7:]
        if has_prev:
            pq_ref, pv_ref = refs[:2]
            refs = refs[2:]
        dq_ref, dfl_ref, dv_ref, dlb_ref, st_all, dd_all, dec_all, cum_buf, cum_blk, k_blk, v_blk, dk_blk, dv_blk = refs
        mask = _hgrn_intra_mask(reverse)
        lbv = lb_ref[...]
        tio = lax.broadcasted_iota(jnp.int32, (hb, HEAD_DIM), 0)

        def rows_of(n):
            rb = _hgrn_rowblock(n, nctx, nblk, reverse)
            return rb, pl.multiple_of(rb * hb, hb)

        def load_do(rb):
            lat0 = pl.multiple_of(jnp.maximum(rb - nctx, 0) * hb, hb)
            return jnp.where(rb >= nctx, do_ref[pl.ds(lat0, hb), :], 0.0)

        def phase1(n, slot, c):
            rb, r0 = rows_of(n)
            q, v = q_ref[pl.ds(r0, hb), :], v_ref[pl.ds(r0, hb), :]
            _, f, g, k = _hgrn_gates(f_ref[pl.ds(r0, hb), :], lbv)
            cum = jnp.dot(tri_ref[...], g, precision=HIGHEST, preferred_element_type=F32)
            tot = jnp.sum(g, axis=0, keepdims=True)
            cum_buf[pl.ds(r0, hb), :] = cum
            kl = k * jnp.exp(tot - cum)
            st_all[n] = lax.dot_general(v.astype(BF16), kl.astype(BF16), (((0,), (0,)), ((), ())), preferred_element_type=F32)
            dec_all[pl.ds(n, 1), :] = jnp.exp(tot)
            qe = q * jnp.exp(cum)
            dd_all[n] = lax.dot_general(load_do(rb).astype(BF16), qe.astype(BF16), (((0,), (0,)), ((), ())), preferred_element_type=F32)
            return c

        _unrolled(nblk, u1, phase1, 0)

        def phase2(n, st):
            kv = st_all[n]
            st_all[n] = st
            return st * dec_all[pl.ds(n, 1), :] + kv

        lax.fori_loop(0, nblk, phase2, jnp.zeros((HEAD_DIM, HEAD_DIM), F32))

        def phase2r(i, dst):
            n = nblk - 1 - i
            u = dd_all[n]
            dd_all[n] = dst
            return u + dst * dec_all[pl.ds(n, 1), :]

        lax.fori_loop(0, nblk, phase2r, jnp.zeros((HEAD_DIM, HEAD_DIM), F32))

        def phase3(n, slot, dlb):
            rb, r0 = rows_of(n)
            q, v = q_ref[pl.ds(r0, hb), :], v_ref[pl.ds(r0, hb), :]
            sg, f, g, k = _hgrn_gates(f_ref[pl.ds(r0, hb), :], lbv)
            cum = cum_buf[pl.ds(r0, hb), :]
            tot = jnp.sum(g, axis=0, keepdims=True)
            dob = load_do(rb)
            st, dst = st_all[n], dd_all[n]
            e_cum = jnp.exp(cum)
            e_rest = jnp.exp(tot - cum)
            dq = jnp.dot(dob.astype(BF16), st.astype(BF16), preferred_element_type=F32) * e_cum
            dk_inter = jnp.dot(v.astype(BF16), dst.astype(BF16), preferred_element_type=F32) * e_rest
            dv = lax.dot_general((k * e_rest).astype(BF16), dst.astype(BF16), (((1,), (1,)), ((), ())), preferred_element_type=F32)
            cum_blk[slot] = cum
            k_blk[slot] = k
            v_blk[slot] = v
            for s in range(hb):
                e = jnp.where(mask(s), jnp.exp(jnp.minimum(cum - cum_blk[slot, s:s + 1, :], 0.0)), 0.0)
                a_s = jnp.sum(q * e * k_blk[slot, s:s + 1, :], axis=-1, keepdims=True)
                da_s = jnp.sum(dob * v_blk[slot, s:s + 1, :], axis=-1, keepdims=True)
                gs = da_s * e
                dq = dq + gs * k_blk[slot, s:s + 1, :]
                dk_blk[slot, s:s + 1, :] = jnp.sum(gs * q, axis=0, keepdims=True)
                dv_blk[slot, s:s + 1, :] = jnp.sum(a_s * dob, axis=0, keepdims=True)
            dk = dk_inter + dk_blk[slot]
            dv = dv + dv_blk[slot]
            d_tot = jnp.sum(k * dk_inter, axis=0, keepdims=True) + jnp.exp(tot) * jnp.sum(dst * st, axis=0, keepdims=True)
            dcum = q * dq - k * dk + jnp.where(tio == last_row, d_tot, 0.0)
            dg = jnp.dot(trit_ref[...], dcum, precision=HIGHEST, preferred_element_type=F32)
            df = dg / f - dk
            if has_prev:
                dq = dq + pq_ref[pl.ds(r0, hb), :].astype(F32)
                dv = dv + pv_ref[pl.ds(r0, hb), :].astype(F32)
            dq_ref[pl.ds(r0, hb), :] = dq.astype(dq_ref.dtype)
            dv_ref[pl.ds(r0, hb), :] = dv.astype(dv_ref.dtype)
            dfl_ref[pl.ds(r0, hb), :] = (df * (1.0 - lbv) * sg * (1.0 - sg)).astype(dfl_ref.dtype)
            return dlb + jnp.sum(df * (1.0 - sg), axis=0, keepdims=True)

        dlb_ref[...] = _unrolled(nblk, u3, phase3, jnp.zeros((1, HEAD_DIM), F32))

    col = lambda seg: pl.BlockSpec((tt, HEAD_DIM), lambda h, seg=seg: (0, seg * N_HEADS + h))
    head = pl.BlockSpec((tt, HEAD_DIM), lambda h: (0, h))
    lbs = pl.BlockSpec((1, HEAD_DIM), lambda h: (0, h))
    tris = pl.BlockSpec((hb, hb), lambda h: (0, 0))
    in_specs = [col(0), col(seg_f), col(3), lbs, tris, tris, pl.BlockSpec((do.shape[0], HEAD_DIM), lambda h: (0, h))]
    args = [p, p, p, lb, tri, tri_t, do]
    mid = F32 if not has_prev else BF16
    if has_prev:
        in_specs += [head, head]
        args += list(prev)
    w = N_HEADS * HEAD_DIM
    blk = pltpu.VMEM((u3, hb, HEAD_DIM), F32)
    return _pcall(
        body, name="hgrn_bwd_rev" if reverse else "hgrn_bwd", grid=(N_HEADS,), in_specs=in_specs,
        out_specs=[head, head, head, lbs],
        out_shape=[_sds((tt, w), mid), _sds((tt, w), BF16), _sds((tt, w), mid), _sds((1, w), F32)],
        scratch=[pltpu.VMEM((nblk, HEAD_DIM, HEAD_DIM), F32), pltpu.VMEM((nblk, HEAD_DIM, HEAD_DIM), F32), pltpu.VMEM((nblk, HEAD_DIM), F32),
                 pltpu.VMEM((tt, HEAD_DIM), F32), blk, blk, blk, blk, blk])(*args)


def _na_geometry(rows):
    r = pl.program_id(1)
    rs = jnp.clip(r - WIN_R // 2, 0, rows - WIN_R)
    return r, rs, r - rs


def _na_scores(q, kb, kc, bias):
    scale = HEAD_DIM ** -0.5
    nt = (((1,), (1,)), ((), ()))
    sb = lax.dot_general(q, kb, nt, preferred_element_type=F32) * scale + bias
    sc = lax.dot_general(q, kc, nt, preferred_element_type=F32) * scale
    m = jnp.maximum(jnp.max(sb, axis=-1, keepdims=True), jnp.max(sc, axis=-1, keepdims=True))
    pb, pc = jnp.exp(sb - m), jnp.exp(sc - m)
    inv = 1.0 / (jnp.sum(pb, axis=-1, keepdims=True) + jnp.sum(pc, axis=-1, keepdims=True))
    return pb * inv, pc * inv


def _na_fwd(qn, kall, p, seg_v, bias, n_ctx_rows):
    t, tt = qn.shape[0], kall.shape[0]
    rows = t // GRID_W
    nband = WIN_R * GRID_W

    def body(q_ref, k_ref, v_ref, b_ref, o_ref):
        r, rs, _ = _na_geometry(rows)
        k0 = pl.multiple_of(n_ctx_rows + rs * GRID_W, GRID_W)
        q = q_ref[...]
        kb, kc = k_ref[pl.ds(k0, nband), :], k_ref[pl.ds(0, n_ctx_rows), :]
        vb, vc = v_ref[pl.ds(k0, nband), :].astype(BF16), v_ref[pl.ds(0, n_ctx_rows), :].astype(BF16)
        pb, pc = _na_scores(q, kb, kc, b_ref[...])
        o = jnp.dot(pb.astype(BF16), vb, preferred_element_type=F32) + jnp.dot(pc.astype(BF16), vc, preferred_element_type=F32)
        o_ref[...] = o.astype(o_ref.dtype)

    return _pcall(
        body, name="na_fwd", grid=(N_HEADS, rows),
        in_specs=[pl.BlockSpec((GRID_W, HEAD_DIM), lambda h, r: (r, h)),
                  pl.BlockSpec((tt, HEAD_DIM), lambda h, r: (0, h)),
                  pl.BlockSpec((tt, HEAD_DIM), lambda h, r: (0, seg_v * N_HEADS + h)),
                  pl.BlockSpec((None, None, GRID_W, nband), lambda h, r: (h, r - jnp.clip(r - WIN_R // 2, 0, rows - WIN_R), 0, 0))],
        out_specs=pl.BlockSpec((GRID_W, HEAD_DIM), lambda h, r: (r, h)),
        out_shape=_sds((t, N_HEADS * HEAD_DIM), BF16))(qn, kall, p, bias)


def _na_bwd(qn, kall, p, seg_v, bias, do, n_ctx_rows):
    t, tt = qn.shape[0], kall.shape[0]
    rows = t // GRID_W
    nband = WIN_R * GRID_W
    scale = HEAD_DIM ** -0.5
    tn = (((0,), (0,)), ((), ()))
    nt = (((1,), (1,)), ((), ()))

    def body(q_ref, k_ref, v_ref, b_ref, do_ref, dq_ref, dk_ref, dv_ref, db_ref, dv_acc):
        r, rs, var = _na_geometry(rows)
        k0 = pl.multiple_of(n_ctx_rows + rs * GRID_W, GRID_W)
        q = q_ref[...]
        kb, kc = k_ref[pl.ds(k0, nband), :], k_ref[pl.ds(0, n_ctx_rows), :]
        vb, vc = v_ref[pl.ds(k0, nband), :].astype(BF16), v_ref[pl.ds(0, n_ctx_rows), :].astype(BF16)
        pb, pc = _na_scores(q, kb, kc, b_ref[...])
        dob = do_ref[...].astype(BF16)
        o = jnp.dot(pb.astype(BF16), vb, preferred_element_type=F32) + jnp.dot(pc.astype(BF16), vc, preferred_element_type=F32)
        delta = jnp.sum(do_ref[...].astype(F32) * o, axis=-1, keepdims=True)
        dsb = pb * (lax.dot_general(dob, vb, nt, preferred_element_type=F32) - delta)
        dsc = pc * (lax.dot_general(dob, vc, nt, preferred_element_type=F32) - delta)
        dsb16, dsc16 = dsb.astype(BF16), dsc.astype(BF16)
        dq_ref[...] = (jnp.dot(dsb16, kb, preferred_element_type=F32) + jnp.dot(dsc16, kc, preferred_element_type=F32)) * scale

        @pl.when(r == 0)
        def _():
            dk_ref[...] = jnp.zeros_like(dk_ref)
            dv_acc[...] = jnp.zeros_like(dv_acc)

        dk_ref[pl.ds(k0, nband), :] += lax.dot_general(dsb16, q, tn, preferred_element_type=F32) * scale
        dk_ref[pl.ds(0, n_ctx_rows), :] += lax.dot_general(dsc16, q, tn, preferred_element_type=F32) * scale
        dv_acc[pl.ds(k0, nband), :] += lax.dot_general(pb.astype(BF16), dob, tn, preferred_element_type=F32)
        dv_acc[pl.ds(0, n_ctx_rows), :] += lax.dot_general(pc.astype(BF16), dob, tn, preferred_element_type=F32)

        @pl.when(r == rows - 1)
        def _():
            dv_ref[...] = dv_acc[...].astype(dv_ref.dtype)

        fresh = jnp.logical_or(r <= WIN_R // 2, r > rows - WIN_R // 2)

        @pl.when(fresh)
        def _():
            db_ref[...] = dsb

        @pl.when(jnp.logical_not(fresh))
        def _():
            db_ref[...] += dsb

    variant = lambda h, r: (h, r - jnp.clip(r - WIN_R // 2, 0, rows - WIN_R), 0, 0)
    head_all = pl.BlockSpec((tt, HEAD_DIM), lambda h, r: (0, h))
    qspec = pl.BlockSpec((GRID_W, HEAD_DIM), lambda h, r: (r, h))
    w = N_HEADS * HEAD_DIM
    return _pcall(
        body, name="na_bwd", grid=(N_HEADS, rows),
        in_specs=[qspec, head_all, pl.BlockSpec((tt, HEAD_DIM), lambda h, r: (0, seg_v * N_HEADS + h)),
                  pl.BlockSpec((None, None, GRID_W, nband), variant), qspec],
        out_specs=[qspec, head_all, head_all, pl.BlockSpec((None, None, GRID_W, nband), variant)],
        out_shape=[_sds((t, w), F32), _sds((tt, w), F32), _sds((tt, w), BF16), _sds((N_HEADS, WIN_R, GRID_W, nband), F32)],
        scratch=[pltpu.VMEM((tt, HEAD_DIM), F32)])(qn, kall, p, bias, do)


def _na_tables(t, n_ctx_rows):
    half, nf = HEAD_DIM // 2, HEAD_DIM // 4
    pos = np.arange(t)
    lane = np.arange(HEAD_DIM)
    inv = ROPE_THETA ** (-(np.arange(nf, dtype=np.float32)) / nf)
    which = np.where(lane < half, pos[:, None] // GRID_W, pos[:, None] % GRID_W).astype(np.float32)
    ang = which * inv[lane % nf][None, :]
    first = (lane % half) < nf
    cos = np.concatenate([np.ones((n_ctx_rows, HEAD_DIM), np.float32), np.cos(ang).astype(np.float32)])
    sin = np.concatenate([np.zeros((n_ctx_rows, HEAD_DIM), np.float32), np.where(first[None, :], -np.sin(ang), np.sin(ang)).astype(np.float32)])
    w = np.arange(GRID_W)
    dc = np.clip(w[None, :] - w[:, None], -(WIN_C - 1), WIN_C - 1) + WIN_C - 1
    onehot = np.zeros((32, GRID_W * GRID_W), np.float32)
    onehot[dc.reshape(-1), np.arange(GRID_W * GRID_W)] = 1.0
    cs = np.clip(w - WIN_C // 2, 0, GRID_W - WIN_C)
    col_in = (w[None, :] >= cs[:, None]) & (w[None, :] < cs[:, None] + WIN_C)
    onehot *= col_in.reshape(1, -1)
    neg = np.where(col_in, 0.0, NEG_BIG).astype(np.float32)
    return jnp.asarray(cos), jnp.asarray(sin), jnp.asarray(onehot), jnp.asarray(neg)


def _bias_slabs(rel_bias, onehot, neg):
    nr = 2 * WIN_R - 1
    rb = jnp.pad(rel_bias.reshape(N_HEADS * nr, 2 * WIN_C - 1), ((0, 0), (0, 1)))
    spread = _mm_f32(rb, onehot, "bias_spread").reshape(N_HEADS, nr, GRID_W, GRID_W)
    slabs = [spread[:, WIN_R - 1 - v:2 * WIN_R - 1 - v] for v in range(WIN_R)]
    b = jnp.stack(slabs, axis=1) + neg[None, None, None]
    return b.transpose(0, 1, 3, 2, 4).reshape(N_HEADS, WIN_R, GRID_W, WIN_R * GRID_W)


def _bias_grad(dbias, onehot):
    nr = 2 * WIN_R - 1
    d = dbias.reshape(N_HEADS, WIN_R, GRID_W, WIN_R, GRID_W).transpose(0, 1, 3, 2, 4)
    tot = jnp.zeros((N_HEADS, nr, GRID_W, GRID_W), F32)
    for v in range(WIN_R):
        tot = tot + jnp.pad(d[:, v], ((0, 0), (WIN_R - 1 - v, v), (0, 0), (0, 0)))
    g = _mm_f32(tot.reshape(N_HEADS * nr, GRID_W * GRID_W), onehot, "bias_grad", trans_b=True)
    return g[:, :2 * WIN_C - 1].reshape(1, N_HEADS, nr, 2 * WIN_C - 1)


def _shift_rows(u, up):
    n = u.shape[0]
    tio = lax.broadcasted_iota(jnp.int32, u.shape, 0)
    if up:
        return jnp.where(tio == n - 1, 0.0, pltpu.roll(u, n - 1, 0))
    return jnp.where(tio == 0, 0.0, pltpu.roll(u, 1, 0))


def _conv3(u, w_ref, b_ref):
    um, up = _shift_rows(u, False), _shift_rows(u, True)
    return um, up, um * w_ref[0:1, :] + u * w_ref[1:2, :] + up * w_ref[2:3, :] + b_ref[...]


def _ffn_act_fwd(u0, t3, cw, cb):
    t, n = u0.shape
    tc = _pick(n, (256, 128))

    def body(u_ref, t_ref, w_ref, b_ref, a_ref):
        _, _, uc = _conv3(u_ref[...], w_ref, b_ref)
        a_ref[...] = (uc * jax.nn.sigmoid(uc) * t_ref[...]).astype(a_ref.dtype)

    col = lambda rows_: pl.BlockSpec((rows_, tc), lambda j: (0, j))
    return _pcall(body, name="ffn_act_fwd", grid=(n // tc,), in_specs=[col(t), col(t), col(8), col(1)], out_specs=col(t),
                  out_shape=_sds((t, n), BF16))(u0, t3, cw, cb)


def _ffn_act_bwd(u0, t3, cw, cb, da):
    t, n = u0.shape
    tc = _pick(n, (256, 128))

    def body(u_ref, t_ref, w_ref, b_ref, da_ref, du_ref, dt_ref, dw_ref, db_ref):
        u = u_ref[...]
        um, up, uc = _conv3(u, w_ref, b_ref)
        sg = jax.nn.sigmoid(uc)
        dav = da_ref[...].astype(F32)
        dt_ref[...] = (dav * uc * sg).astype(dt_ref.dtype)
        duc = dav * t_ref[...] * sg * (1.0 + uc * (1.0 - sg))
        du = _shift_rows(duc, True) * w_ref[0:1, :] + duc * w_ref[1:2, :] + _shift_rows(duc, False) * w_ref[2:3, :]
        du_ref[...] = du.astype(du_ref.dtype)
        dw_ref[...] = jnp.zeros_like(dw_ref)
        dw_ref[0:1, :] = jnp.sum(duc * um, axis=0, keepdims=True)
        dw_ref[1:2, :] = jnp.sum(duc * u, axis=0, keepdims=True)
        dw_ref[2:3, :] = jnp.sum(duc * up, axis=0, keepdims=True)
        db_ref[...] = jnp.sum(duc, axis=0, keepdims=True)

    col = lambda rows_: pl.BlockSpec((rows_, tc), lambda j: (0, j))
    return _pcall(body, name="ffn_act_bwd", grid=(n // tc,), in_specs=[col(t), col(t), col(8), col(1), col(t)],
                  out_specs=[col(t), col(t), col(8), col(1)],
                  out_shape=[_sds((t, n), BF16), _sds((t, n), BF16), _sds((8, n), F32), _sds((1, n), F32)])(u0, t3, cw, cb, da)


def _adam_math(g, w, m, v):
    m2 = ADAM_B1 * m + (1.0 - ADAM_B1) * g
    v2 = ADAM_B2 * v + (1.0 - ADAM_B2) * (g * g)
    m_hat = m2 / (1.0 - ADAM_B1 ** ADAM_STEP)
    v_hat = v2 / (1.0 - ADAM_B2 ** ADAM_STEP)
    return -ADAM_LR * (m_hat / (jnp.sqrt(v_hat) + ADAM_EPS) + ADAM_WD * w), m2, v2


def _adam_big(parts, w, m, v, name):
    r, c = w.shape
    npart, _, cp = parts.shape
    tr = _pick(r, (64, 32, 16, 8))

    def body(p_ref, w_ref, m_ref, v_ref, g_ref, d_ref, m2_ref, v2_ref):
        g = p_ref[0, :, 0:c].astype(F32)
        for i in range(1, npart):
            g = g + p_ref[i, :, 0:c].astype(F32)
        d, m2, v2 = _adam_math(g, w_ref[...], m_ref[...], v_ref[...])
        g_ref[...] = g
        d_ref[...] = d
        m2_ref[...] = m2
        v2_ref[...] = v2

    row = pl.BlockSpec((tr, c), lambda i: (i, 0))
    return _pcall(body, name=name, grid=(r // tr,),
                  in_specs=[pl.BlockSpec((npart, tr, cp), lambda i: (0, i, 0)), row, row, row],
                  out_specs=[row] * 4, out_shape=[_sds((r, c), F32)] * 4)(parts, w, m, v)


def _adam_small(g, w, m, v):
    def body(g_ref, w_ref, m_ref, v_ref, d_ref, m2_ref, v2_ref):
        d_ref[...], m2_ref[...], v2_ref[...] = _adam_math(g_ref[...], w_ref[...], m_ref[...], v_ref[...])

    return _pcall(body, name="adam_small", out_shape=[_sds(g.shape, F32)] * 3)(g, w, m, v)


def _sum_parts(parts, name):
    def body(p_ref, o_ref):
        s = p_ref[0]
        for i in range(1, N_DEV):
            s = s + p_ref[i]
        o_ref[...] = s

    return _pcall(body, name=name, out_shape=_sds(parts.shape[1:], F32))(parts)


class _Pack:
    def __init__(self, shapes):
        self.shapes = shapes
        self.sizes = [int(np.prod(s)) for s in shapes]
        self.padded = [-(-n // (8 * LANE)) * 8 * LANE for n in self.sizes]
        self.offs = np.concatenate([[0], np.cumsum(self.padded)]).tolist()

    def pack(self, arrs):
        flat = [jnp.pad(a.reshape(-1).astype(F32), (0, p - n)) for a, n, p in zip(arrs, self.sizes, self.padded)]
        return jnp.concatenate(flat).reshape(-1, LANE)

    def unpack(self, slab):
        flat = slab.reshape(-1)
        return [flat[o:o + n].reshape(s) for o, n, s in zip(self.offs, self.sizes, self.shapes)]


def kernel(x, c, ctx, c_ctx, ada_w, ada_b, norm1_g, norm2_g, w_in, hgrn_lb_logits, hgrn_norm_g, na_q_norm_g, na_k_norm_g, na_rel_bias, w_branch_a, w_branch_b, w_out, ffn_w1, ffn_w3, ffn_conv_w, ffn_conv_b, ffn_w2, loss_target, m_c_ctx, m_ada_w, m_ada_b, m_norm1_g, m_norm2_g, m_w_in, m_hgrn_lb_logits, m_hgrn_norm_g, m_na_q_norm_g, m_na_k_norm_g, m_na_rel_bias, m_w_branch_a, m_w_branch_b, m_w_out, m_ffn_w1, m_ffn_w3, m_ffn_conv_w, m_ffn_conv_b, m_ffn_w2, v_c_ctx, v_ada_w, v_ada_b, v_norm1_g, v_norm2_g, v_w_in, v_hgrn_lb_logits, v_hgrn_norm_g, v_na_q_norm_g, v_na_k_norm_g, v_na_rel_bias, v_w_branch_a, v_w_branch_b, v_w_out, v_ffn_w1, v_ffn_w3, v_ffn_conv_w, v_ffn_conv_b, v_ffn_w2):
    t, d = x.shape[1], x.shape[2]
    n_ctx = ctx.shape[1]
    tt = n_ctx + t
    hw = N_HEADS * HEAD_DIM
    ci = w_in.shape[2]
    ca = ada_w.shape[2]
    ff_l = ffn_w1.shape[2]
    ff_p = -(-ff_l // LANE) * LANE
    rows = t // GRID_W
    assert rows >= WIN_R and t % GRID_W == 0 and n_ctx % GRID_W == 0 and ci % LANE == 0 and d % LANE == 0
    me = 4 * lax.axis_index("x") + 2 * lax.axis_index("y") + lax.axis_index("c")
    tr = _pick(n_ctx, (256, 128, 64))
    n_ctx_tiles = n_ctx // tr

    pad_c = lambda w: jnp.pad(w, ((0, 0), (0, ff_p - ff_l)))
    small_in = [c, hgrn_lb_logits.reshape(4, HEAD_DIM), jnp.pad(ffn_conv_w[0], ((0, 5), (0, ff_p - ff_l)))]
    c_all, lb_parts, cw_all = _exchange(small_in, "gather_params", scatter=False)
    ff = N_DEV * ff_p

    cc = jnp.concatenate([c_all.reshape(N_DEV, d), jnp.broadcast_to(c_ctx[None, :], (N_DEV, d))], axis=0)
    act = _tiled(lambda v: v * jax.nn.sigmoid(v), [_const(cc)], [(cc.shape, BF16, cc.shape, lambda *_: (0, 0))], (), "silu_c")[0]
    ada16 = _cast_bf16(ada_w[0], "cast_ada")
    mod_cols = _mm_xw(act, ada16.reshape(1, d, ca), F32, "ada_fwd")
    mod_all = _exchange([mod_cols], "gather_mod", scatter=False)[0]

    with _after(mod_all):
        w_in16 = _cast_bf16(w_in[0], "cast_w_in")
    g_in, = _gather_sc([w_in16], "gather_w_in", 1)
    rest16 = []
    for w_, nm in ((w_branch_a[0], "cast_w_a"), (w_branch_b[0], "cast_w_b"), (w_out[0], "cast_w_out"), (pad_c(ffn_w1[0]), "cast_w1"),
                   (pad_c(ffn_w3[0]), "cast_w3"), (jnp.pad(ffn_w2[0], ((0, ff_p - ff_l), (0, 0))), "cast_w2")):
        with _after(w_in16):
            rest16.append(_cast_bf16(w_, nm))
    g_a, g_b, g_out, g_w1, g_w3, g_w2 = _gather_sc(rest16, "gather_rest", 2)
    g_out = g_out.reshape(1, d, d)
    g_w2 = g_w2.reshape(1, ff, d)
    mod_all = mod_all.transpose(1, 0, 2).reshape(2 * N_DEV, N_MOD * d) + ada_b
    mod_l = lax.dynamic_slice_in_dim(mod_all, me, 1, axis=0).reshape(N_MOD, 1, d)
    mod_c = mod_all[N_DEV:N_DEV + 1].reshape(N_MOD, 1, d)
    mods1 = jnp.stack([mod_c[0:2], mod_l[0:2]])
    mods2 = jnp.stack([mod_l[3:5], mod_l[3:5]])
    gate1, gate2 = mod_l[2], mod_l[5]

    xcat = jnp.concatenate([ctx[0], x[0]], axis=0)
    hcat = _norm_mod_fwd(xcat, norm1_g, mods1, n_ctx_tiles, tr)
    p = _mm_xw(hcat, g_in, F32, "in_proj")
    lb_logits = lb_parts.transpose(1, 0, 2).reshape(2, 2, hw)
    lb_soft = _tiled(lambda a, b: (1.0 / (1.0 + jnp.exp(b - a)),), [_const(lb_logits[:, 0]), _const(lb_logits[:, 1])],
                     [((2, hw), F32, (2, hw), lambda *_: (0, 0))], (), "lb_softmax")[0]
    lb_f, lb_b = lb_soft[0:1], lb_soft[1:2]
    o_f = _hgrn_fwd(p, lb_f, 1, False, n_ctx)
    o_b = _hgrn_fwd(p, lb_b, 2, True, n_ctx)

    cos, sin, onehot, neg = _na_tables(t, n_ctx)
    bias = _bias_slabs(na_rel_bias[0], onehot, neg)
    hcol = lambda seg, off=0: (lambda i, h: (i + off, seg * N_HEADS + h))
    tq = tr
    lat0 = n_ctx // tq
    qk_fn = lambda tv, g, cs, sn: (_f_qk(tv, g, cs, sn),)
    tab = lambda a, off=0: (a, (tq, HEAD_DIM), lambda i, h: (i + off, 0))
    qn = _tiled(qk_fn, [(p, (tq, HEAD_DIM), hcol(5, lat0)), _const(na_q_norm_g), tab(cos, lat0), tab(sin, lat0)],
                [((t, hw), BF16, (tq, HEAD_DIM), lambda i, h: (i, h))], (t // tq, N_HEADS), "q_norm_rope")[0]
    kall = _tiled(qk_fn, [(p, (tq, HEAD_DIM), hcol(6)), _const(na_k_norm_g), tab(cos), tab(sin)],
                  [((tt, hw), BF16, (tq, HEAD_DIM), lambda i, h: (i, h))], (tt // tq, N_HEADS), "k_norm_rope")[0]
    y_b = _na_fwd(qn, kall, p, 7, bias, n_ctx)

    trh = tr
    lat_h = n_ctx // trh
    ospec = lambda a: (a, (trh, HEAD_DIM), lambda i, h: (i + lat_h, h))
    y_a = _tiled(lambda a, b, gt, g: (_f_readout(a, b, gt, g),),
                 [ospec(o_f), ospec(o_b), (p, (trh, HEAD_DIM), hcol(4, lat_h)), _const(hgrn_norm_g)],
                 [((t, hw), BF16, (trh, HEAD_DIM), lambda i, h: (i, h))], (t // trh, N_HEADS), "hgrn_readout")[0]

    p_a = _mm_xw(y_a, g_a, F32, "branch_a")
    p_b = _mm_xw(y_b, g_b, F32, "branch_b")
    td = _pick(d, (512, 256, 128))
    nd_t = d // td
    lat_r = n_ctx // tr
    gcol = lambda k: (p, (tr, td), lambda i, j, k=k: (i + lat_r, 8 * hw // td + k * nd_t + j))
    dtile = lambda a: (a, (tr, td), lambda i, j: (i, j))
    z = _tiled(lambda ga, gb, pa, pb: (_f_merge(ga, gb, pa, pb),), [gcol(0), gcol(1), dtile(p_a), dtile(p_b)],
               [((t, d), BF16, (tr, td), lambda i, j: (i, j))], (t // tr, nd_t), "merge")[0]
    mix = _mm_xw(z, g_out, F32, "out_proj")
    xl = x[0]
    x_mid = _tiled(lambda a, g, m_: (_f_resid(a, g, m_),), [_rows(xl, tr), _const(gate1), _rows(mix, tr)],
                   [((t, d), F32, (tr, d), lambda i: (i, 0))], (t // tr,), "resid1")[0]

    h2 = _norm_mod_fwd(x_mid, norm2_g, mods2, 0, tr)
    u0 = _mm_xw(h2, g_w1, F32, "ffn_up1")
    t3 = _mm_xw(h2, g_w3, F32, "ffn_up3")
    cw_full = cw_all.transpose(1, 0, 2).reshape(8, ff)
    cb_full = jnp.pad(ffn_conv_b.reshape(N_DEV, ff_l), ((0, 0), (0, ff_p - ff_l))).reshape(1, ff)
    a_act = _ffn_act_fwd(u0, t3, cw_full, cb_full)
    f_out = _mm_xw(a_act, g_w2, F32, "ffn_down")

    def loss_fn(xm, g, f, tg):
        err = xm + g * f - tg
        return err * (1.0 / d), jnp.sum(err * err, axis=0, keepdims=True) * (0.5 / d), jnp.sum(err * (1.0 / d) * f, axis=0, keepdims=True)

    dy, loss_cols, d_gate2 = _tiled(loss_fn, [_rows(x_mid, tr), _const(gate2), _rows(f_out, tr), _rows(loss_target[0], tr)],
                                    [((t, d), F32, (tr, d), lambda i: (i, 0)), ((1, d), F32, (1, d), lambda i: (0, 0)),
                                     ((1, d), F32, (1, d), lambda i: (0, 0))], (t // tr,), "loss", acc=(1, 2))

    df = _tiled(lambda a, g: (a * g,), [_rows(dy, tr), _const(gate2)], [((t, d), BF16, (tr, d), lambda i: (i, 0))], (t // tr,), "d_ffn_out")[0]
    d_w2 = _mm_xtdy(a_act, df, 1, BF16, "d_w2")
    da = _mm_dyw(df, g_w2, BF16, "d_act")
    du0, dt3, d_cw, d_cb = _ffn_act_bwd(u0, t3, cw_full, cb_full, da)
    d_w1 = _mm_xtdy(h2, du0, N_DEV, BF16, "d_w1")
    d_w3 = _mm_xtdy(h2, dt3, N_DEV, BF16, "d_w3")
    r_w1, r_w3, r_w2 = _exchange_sc([d_w1, d_w3, d_w2.reshape(N_DEV, ff_p, d)], "scatter_ffn", True, 3)
    dh2 = _mm_dyw(du0, g_w1, F32, "d_h2_a")
    dh2 = _mm_dyw(dt3, g_w3, F32, "d_h2_b", init=dh2)
    dx_mid, d_norm2, d_mods2 = _norm_mod_bwd(x_mid, norm2_g, mods2, dh2, dy, 0, tr, "norm_mod2_bwd")

    dm, d_gate1 = _tiled(lambda dxm, g, m_: (dxm * g, jnp.sum(dxm * m_, axis=0, keepdims=True)),
                         [_rows(dx_mid, tr), _const(gate1), _rows(mix, tr)],
                         [((t, d), BF16, (tr, d), lambda i: (i, 0)), ((1, d), F32, (1, d), lambda i: (0, 0))], (t // tr,), "d_resid1", acc=(1,))
    d_wout = _mm_xtdy(z, dm, 1, BF16, "d_w_out")
    dz = _mm_dyw(dm, g_out, F32, "d_merge")

    def merge_bwd(ga, gb, pa, pb, dzv):
        _, vjp = jax.vjp(_f_merge, ga, gb, pa, pb)
        return vjp(dzv)

    dga, dgb, dpa, dpb = _tiled(merge_bwd, [gcol(0), gcol(1), dtile(p_a), dtile(p_b), dtile(dz)],
                                [((t, d), BF16, (tr, td), lambda i, j: (i, j))] * 4, (t // tr, nd_t), "merge_bwd")
    d_wa = _mm_xtdy(y_a, dpa, N_DEV, BF16, "d_w_a")
    d_wb = _mm_xtdy(y_b, dpb, N_DEV, BF16, "d_w_b")
    r_a, r_b, r_out = _exchange_sc([d_wa, d_wb, d_wout.reshape(N_DEV, d // N_DEV, d)], "scatter_mix", True, 4)
    dy_a = _mm_dyw(dpa, g_a, F32, "d_y_a")
    dy_b = _mm_dyw(dpb, g_b, BF16, "d_y_b")

    def readout_bwd(a, b, gt, g, ct):
        _, vjp = jax.vjp(_f_readout, a, b, gt, g)
        da_, _, dgt, dg = vjp(ct)
        return da_, dgt, dg

    hsp = lambda dt: ((t, hw), dt, (trh, HEAD_DIM), lambda i, h: (i, h))
    do_h, d_gate_o, d_hnorm = _tiled(
        readout_bwd, [ospec(o_f), ospec(o_b), (p, (trh, HEAD_DIM), hcol(4, lat_h)), _const(hgrn_norm_g), (dy_a, (trh, HEAD_DIM), lambda i, h: (i, h))],
        [hsp(F32), hsp(BF16), ((1, HEAD_DIM), F32, (1, HEAD_DIM), lambda i, h: (0, 0))], (t // trh, N_HEADS), "readout_bwd", acc=(2,))
    dq1, dfl_f, dv1, dlb_f = _hgrn_bwd(p, lb_f, do_h, 1, False, n_ctx, None)
    dq_h, dfl_b, dv_h, dlb_b = _hgrn_bwd(p, lb_b, do_h, 2, True, n_ctx, (dq1, dv1))

    dqn, dkall, dv_na, dbias = _na_bwd(qn, kall, p, 7, bias, dy_b, n_ctx)

    def qk_bwd(tv, g, cs, sn, ct):
        _, vjp = jax.vjp(lambda a, b: _f_qk(a, b, cs, sn), tv, g)
        return vjp(ct)

    d_pq, d_qnorm = _tiled(qk_bwd, [(p, (tq, HEAD_DIM), hcol(5, lat0)), _const(na_q_norm_g), tab(cos, lat0), tab(sin, lat0),
                                    (dqn, (tq, HEAD_DIM), lambda i, h: (i, h))],
                           [((t, hw), BF16, (tq, HEAD_DIM), lambda i, h: (i, h)), ((1, HEAD_DIM), F32, (1, HEAD_DIM), lambda i, h: (0, 0))],
                           (t // tq, N_HEADS), "q_norm_rope_bwd", acc=(1,))
    d_pk, d_knorm = _tiled(qk_bwd, [(p, (tq, HEAD_DIM), hcol(6)), _const(na_k_norm_g), tab(cos), tab(sin),
                                    (dkall, (tq, HEAD_DIM), lambda i, h: (i, h))],
                           [((tt, hw), BF16, (tq, HEAD_DIM), lambda i, h: (i, h)), ((1, HEAD_DIM), F32, (1, HEAD_DIM), lambda i, h: (0, 0))],
                           (tt // tq, N_HEADS), "k_norm_rope_bwd", acc=(1,))

    zc = lambda w_: jnp.zeros((n_ctx, w_), BF16)
    lat_only = lambda a: jnp.concatenate([zc(a.shape[1]), a], axis=0)
    dp = jnp.concatenate([dq_h, dfl_f, dfl_b, dv_h, lat_only(d_gate_o), lat_only(d_pq), d_pk, dv_na, lat_only(dga), lat_only(dgb)], axis=1)
    dhcat = _mm_dyw(dp, g_in, BF16, "d_hcat")
    zero_ctx = jnp.concatenate([jnp.zeros((n_ctx, d), F32), dx_mid], axis=0)
    dxcat, d_norm1, d_mods1 = _norm_mod_bwd(xcat, norm1_g, mods1, dhcat, zero_ctx, n_ctx_tiles, tr, "norm_mod1_bwd")
    grad_x = dxcat[n_ctx:][None]

    zd = jnp.zeros((1, d), F32)
    dmod_l = jnp.concatenate([d_mods1[1, 0], d_mods1[1, 1], d_gate1, d_mods2[1, 0], d_mods2[1, 1], d_gate2], axis=1)
    dmod_c = jnp.concatenate([d_mods1[0, 0], d_mods1[0, 1], zd, zd, zd, zd], axis=1)
    dmods = jnp.concatenate([dmod_l, dmod_c], axis=0).reshape(2, N_DEV, ca).transpose(1, 0, 2)
    dmods = jnp.pad(dmods, ((0, 0), (0, 6), (0, 0)))
    got = _exchange([dmods], "scatter_dmod", scatter=True)[0]
    dm_rows = jnp.concatenate([got[:, 0], got[:, 1]], axis=0)
    d_ada = _mm_xtdy(act, dm_rows, 1, F32, "d_ada_w")[0]
    back = _mm_dyw(dm_rows, ada16.reshape(1, d, ca), F32, "d_silu_c")
    d_cctx_part = _tiled(lambda b, v: (jnp.sum(b[N_DEV:], axis=0, keepdims=True) * (jax.nn.sigmoid(v) * (1.0 + v * (1.0 - jax.nn.sigmoid(v)))),),
                         [_const(back), _const(c_ctx.reshape(1, d))], [((1, d), F32, (1, d), lambda *_: (0, 0))], (), "d_c_ctx")[0]

    d_rel = _bias_grad(dbias, onehot)
    d_lb_soft = jnp.concatenate([dlb_f, dlb_b], axis=0)
    d_lb0 = _tiled(lambda s, g: (g * s * (1.0 - s),), [_const(lb_soft), _const(d_lb_soft)], [((2, hw), F32, (2, hw), lambda *_: (0, 0))], (), "d_lb")[0]
    d_lb_full = jnp.stack([d_lb0, -d_lb0], axis=1)
    d_cw_l = d_cw[:3].reshape(3, N_DEV, ff_p)[:, :, :ff_l].reshape(1, 3, N_DEV * ff_l)
    d_cb_l = d_cb.reshape(N_DEV, ff_p)[:, :ff_l].reshape(1, N_DEV * ff_l)
    small = [d_cctx_part.reshape(d), (dmod_l + dmod_c), d_norm1, d_norm2, d_lb_full, d_hnorm, d_qnorm, d_knorm, d_rel, d_cw_l, d_cb_l, loss_cols]
    pk = _Pack([a.shape for a in small])
    tot = _sum_parts(_exchange([pk.pack(small)], "gather_small", scatter=False)[0], "sum_small")
    g_cctx, g_ada_b, g_n1, g_n2, g_lb, g_hn, g_qn, g_kn, g_rel, g_cw, g_cb, loss_all = pk.unpack(tot)
    loss = _tiled(lambda v: (jnp.sum(v, axis=1, keepdims=True),), [_const(loss_all)], [((1, 1), F32, (1, 1), lambda *_: (0, 0))], (), "loss_total")[0][0, 0]
    g_lb = lax.dynamic_slice_in_dim(g_lb, me * HEAD_DIM, HEAD_DIM, axis=2)
    g_cw = lax.dynamic_slice_in_dim(g_cw, me * ff_l, ff_l, axis=2)
    small_names = [("c_ctx", g_cctx, c_ctx, m_c_ctx, v_c_ctx), ("ada_b", g_ada_b, ada_b, m_ada_b, v_ada_b),
                   ("norm1_g", g_n1, norm1_g, m_norm1_g, v_norm1_g), ("norm2_g", g_n2, norm2_g, m_norm2_g, v_norm2_g),
                   ("hgrn_lb_logits", g_lb, hgrn_lb_logits, m_hgrn_lb_logits, v_hgrn_lb_logits),
                   ("hgrn_norm_g", g_hn, hgrn_norm_g, m_hgrn_norm_g, v_hgrn_norm_g), ("na_q_norm_g", g_qn, na_q_norm_g, m_na_q_norm_g, v_na_q_norm_g),
                   ("na_k_norm_g", g_kn, na_k_norm_g, m_na_k_norm_g, v_na_k_norm_g), ("na_rel_bias", g_rel, na_rel_bias, m_na_rel_bias, v_na_rel_bias),
                   ("ffn_conv_w", g_cw, ffn_conv_w, m_ffn_conv_w, v_ffn_conv_w), ("ffn_conv_b", g_cb, ffn_conv_b, m_ffn_conv_b, v_ffn_conv_b)]
    pk2 = _Pack([s[1].shape for s in small_names])
    sd, sm, sv = _adam_small(*[pk2.pack([s[i] for s in small_names]) for i in (1, 2, 3, 4)])
    sd, sm, sv = pk2.unpack(sd), pk2.unpack(sm), pk2.unpack(sv)
    res = {s[0]: (s[1], sd[i], sm[i], sv[i]) for i, s in enumerate(small_names)}

    with _after(sd[0]):
        d_win = _mm_xtdy(hcat, dp, N_DEV, BF16, "d_w_in")
    pair_sums = _pair_add(d_win, _pair_swap_sc(d_win, "scatter_w_in_pair", 5))
    r_in = _chip_scatter_sc(pair_sums, "scatter_w_in_chips", 6)
    with _after(d_win):
        res["ada_w"] = _adam_big(d_ada[None], ada_w[0], m_ada_w[0], v_ada_w[0], "adam_ada")
    res["ffn_w1"] = _adam_big(r_w1, ffn_w1[0], m_ffn_w1[0], v_ffn_w1[0], "adam_w1")
    res["ffn_w3"] = _adam_big(r_w3, ffn_w3[0], m_ffn_w3[0], v_ffn_w3[0], "adam_w3")
    res["ffn_w2"] = _adam_big(r_w2, ffn_w2[0], m_ffn_w2[0], v_ffn_w2[0], "adam_w2")
    res["w_branch_a"] = _adam_big(r_a, w_branch_a[0], m_w_branch_a[0], v_w_branch_a[0], "adam_w_a")
    res["w_branch_b"] = _adam_big(r_b, w_branch_b[0], m_w_branch_b[0], v_w_branch_b[0], "adam_w_b")
    res["w_out"] = _adam_big(r_out, w_out[0], m_w_out[0], v_w_out[0], "adam_w_out")
    res["w_in"] = _adam_big(r_in, w_in[0], m_w_in[0], v_w_in[0], "adam_w_in")
    for k in ("w_in", "w_branch_a", "w_branch_b", "w_out", "ffn_w1", "ffn_w3", "ffn_w2", "ada_w"):
        res[k] = tuple(a[None] for a in res[k])

    order = ["c_ctx", "ada_w", "ada_b", "norm1_g", "norm2_g", "w_in", "hgrn_lb_logits", "hgrn_norm_g", "na_q_norm_g", "na_k_norm_g",
             "na_rel_bias", "w_branch_a", "w_branch_b", "w_out", "ffn_w1", "ffn_w3", "ffn_conv_w", "ffn_conv_b", "ffn_w2"]
    shapes = {"c_ctx": c_ctx.shape, "ada_b": ada_b.shape, "norm1_g": norm1_g.shape, "norm2_g": norm2_g.shape,
              "hgrn_lb_logits": hgrn_lb_logits.shape, "hgrn_norm_g": hgrn_norm_g.shape, "na_q_norm_g": na_q_norm_g.shape,
              "na_k_norm_g": na_k_norm_g.shape, "na_rel_bias": na_rel_bias.shape, "ffn_conv_w": ffn_conv_w.shape, "ffn_conv_b": ffn_conv_b.shape}
    outs = [loss, grad_x]
    for part in range(4):
        for k in order:
            a = res[k][part]
            outs.append(a.reshape(shapes[k]) if k in shapes else a)
    return tuple(outs)
```

```python
import functools

import numpy as np
import jax
import jax.numpy as jnp
from jax import lax
from jax.experimental import pallas as pl
from jax.experimental.pallas import tpu as pltpu
from jax.experimental.pallas import tpu_sc as plsc

F32 = jnp.float32
BF16 = jnp.bfloat16
HIGHEST = lax.Precision.HIGHEST

N_DEV = 8
MESH_ID = pl.DeviceIdType.MESH
LANE = 128
HEAD_DIM = 128
N_HEADS = 8
GRID_W = 64
WIN_R = 8
WIN_C = 16
ROPE_THETA = 10000.0
EPS = 1e-6
N_MOD = 6
HGRN_BLOCK = 16
NEG_BIG = -1e30
VMEM_LIMIT = 56 << 20

ADAM_LR = 0.001
ADAM_B1 = 0.9
ADAM_B2 = 0.999
ADAM_EPS = 1e-08
ADAM_WD = 0.01
ADAM_STEP = 10

HBM_SPEC = pl.BlockSpec(memory_space=pltpu.HBM)


_ORDER_AFTER = []


class _after:
    def __init__(self, *arrs):
        self.arrs = list(arrs)

    def __enter__(self):
        _ORDER_AFTER.extend(self.arrs)

    def __exit__(self, *exc):
        del _ORDER_AFTER[:]


def _pcall(body, *, name, out_shape, grid=None, in_specs=None, out_specs=None, scratch=(), aliases=None):
    kw = {}
    if grid is not None:
        kw["grid"] = grid
    extra = []
    if _ORDER_AFTER and in_specs is not None:
        extra = list(_ORDER_AFTER)
        del _ORDER_AFTER[:]
        n_in, n_extra, inner = len(in_specs), len(extra), body
        in_specs = list(in_specs) + [pl.BlockSpec(memory_space=pl.ANY)] * n_extra

        def body(*refs):
            return inner(*refs[:n_in], *refs[n_in + n_extra:])

    if extra:
        call = _pcall_inner(body, name, out_shape, kw, in_specs, out_specs, scratch, aliases)
        return lambda *args: call(*args, *extra)
    return _pcall_inner(body, name, out_shape, kw, in_specs, out_specs, scratch, aliases)


def _pcall_inner(body, name, out_shape, kw, in_specs, out_specs, scratch, aliases):
    if in_specs is not None:
        kw["in_specs"] = in_specs
    if out_specs is not None:
        kw["out_specs"] = out_specs
    if scratch:
        kw["scratch_shapes"] = list(scratch)
    if aliases:
        kw["input_output_aliases"] = aliases
    return pl.pallas_call(body, name=name, out_shape=out_shape,
                          compiler_params=pltpu.CompilerParams(vmem_limit_bytes=VMEM_LIMIT), **kw)


def _pick(dim, cands):
    for c in cands:
        if c <= dim and dim % c == 0:
            return c
    return dim


def _sds(shape, dtype):
    return jax.ShapeDtypeStruct(tuple(shape), dtype)


def _peers():
    x, y, c = lax.axis_index("x"), lax.axis_index("y"), lax.axis_index("c")
    out = []
    for k in range(1, N_DEV):
        px = 1 - x if (k >> 2) & 1 else x
        py = 1 - y if (k >> 1) & 1 else y
        pc = 1 - c if k & 1 else c
        out.append((k, (px, py, pc), 4 * px + 2 * py + pc))
    return 4 * x + 2 * y + c, out


def _exchange(arrs, name, scatter):
    n = len(arrs)

    def body(*refs):
        ins, outs = refs[:n], refs[n:2 * n]
        send, recv, loc = refs[2 * n:]
        me, peers = _peers()
        started = []
        for i in range(n):
            src = ins[i].at[me] if scatter else ins[i]
            cp = pltpu.make_async_copy(src, outs[i].at[me], loc.at[i])
            cp.start()
            started.append(cp)
        sends = []
        for k, peer, pidx in peers:
            for i in range(n):
                src = ins[i].at[pidx] if scatter else ins[i]
                cp = pltpu.make_async_remote_copy(src_ref=src, dst_ref=outs[i].at[me], send_sem=send.at[i * 7 + k - 1],
                                                  recv_sem=recv.at[i * 7 + k - 1], device_id=peer, device_id_type=MESH_ID)
                cp.start()
                sends.append(cp)
        for k, peer, pidx in peers:
            for i in range(n):
                src = ins[i].at[pidx] if scatter else ins[i]
                pltpu.make_async_remote_copy(src_ref=src, dst_ref=outs[i].at[pidx], send_sem=send.at[i * 7 + k - 1],
                                             recv_sem=recv.at[i * 7 + k - 1], device_id=peer, device_id_type=MESH_ID).wait_recv()
        for cp in sends:
            cp.wait_send()
        for cp in started:
            cp.wait()

    out_shape = [_sds(a.shape if scatter else (N_DEV,) + a.shape, a.dtype) for a in arrs]
    res = _pcall(body, name=name, out_shape=out_shape, in_specs=[HBM_SPEC] * n, out_specs=[HBM_SPEC] * n,
                 scratch=[pltpu.SemaphoreType.DMA((7 * n,)), pltpu.SemaphoreType.DMA((7 * n,)), pltpu.SemaphoreType.DMA((n,))])(*arrs)
    return list(res)


def _exchange_sc(arrs, name, scatter, collective_id):
    n = len(arrs)
    srcs = [jax.new_ref(a, memory_space=pltpu.MemorySpace.HBM) for a in arrs]
    lands = [jax.empty_ref(_sds(a.shape if scatter else (N_DEV,) + a.shape, a.dtype), memory_space=pltpu.MemorySpace.HBM) for a in arrs]

    @pl.kernel(mesh=plsc.ScalarSubcoreMesh(axis_name="seq", num_cores=1), name=name,
               scratch_types=(pltpu.SemaphoreType.DMA((7 * n,)), pltpu.SemaphoreType.DMA((7 * n,)), pltpu.SemaphoreType.DMA((n,))),
               compiler_params=pltpu.CompilerParams(collective_id=collective_id))
    def launch(send, recv, loc):
        me, peers = _peers()
        barrier = pltpu.get_barrier_semaphore()
        for _, peer, _ in peers:
            pl.semaphore_signal(barrier, inc=1, device_id=peer, device_id_type=MESH_ID)
        pl.semaphore_wait(barrier, N_DEV - 1)
        own = [pltpu.make_async_copy(srcs[i].at[me] if scatter else srcs[i], lands[i].at[me], loc.at[i]) for i in range(n)]
        for cp in own:
            cp.start()
        sends = []
        for k, peer, pidx in peers:
            for i in range(n):
                src = srcs[i].at[pidx] if scatter else srcs[i]
                cp = pltpu.make_async_remote_copy(src_ref=src, dst_ref=lands[i].at[me], send_sem=send.at[i * 7 + k - 1],
                                                  recv_sem=recv.at[i * 7 + k - 1], device_id=peer, device_id_type=MESH_ID)
                cp.start()
                sends.append(cp)
        for k, peer, pidx in peers:
            for i in range(n):
                src = srcs[i].at[pidx] if scatter else srcs[i]
                pltpu.make_async_remote_copy(src_ref=src, dst_ref=lands[i].at[pidx], send_sem=send.at[i * 7 + k - 1],
                                             recv_sem=recv.at[i * 7 + k - 1], device_id=peer, device_id_type=MESH_ID).wait_recv()
        for cp in sends:
            cp.wait_send()
        for cp in own:
            cp.wait()

    launch()
    return [r[...] for r in lands]


def _gather_sc(arrs, name, collective_id):
    n = len(arrs)
    srcs = [jax.new_ref(a, memory_space=pltpu.MemorySpace.HBM) for a in arrs]
    lands = [jax.empty_ref(_sds((N_DEV,) + a.shape, a.dtype), memory_space=pltpu.MemorySpace.HBM) for a in arrs]

    @pl.kernel(mesh=plsc.ScalarSubcoreMesh(axis_name="seq", num_cores=1), name=name,
               scratch_types=(pltpu.SemaphoreType.DMA((7 * n,)), pltpu.SemaphoreType.DMA((7 * n,)), pltpu.SemaphoreType.DMA((n,))),
               compiler_params=pltpu.CompilerParams(collective_id=collective_id))
    def launch(send, recv, loc):
        x, y, c = lax.axis_index("x"), lax.axis_index("y"), lax.axis_index("c")
        me, sibling = (x, y, c), (x, y, 1 - c)
        chips = [(1 - x, y), (x, 1 - y), (1 - x, 1 - y)]
        index = lambda px, py, pc: 4 * px + 2 * py + pc
        barrier = pltpu.get_barrier_semaphore()
        for peer in [sibling] + [(*chip, c) for chip in chips]:
            pl.semaphore_signal(barrier, inc=1, device_id=peer, device_id_type=MESH_ID)
        pl.semaphore_wait(barrier, 4)

        def copy(i, k, block, to, from_src):
            return pltpu.make_async_remote_copy(src_ref=srcs[i] if from_src else lands[i].at[index(*block)], dst_ref=lands[i].at[index(*block)],
                                                send_sem=send.at[i * 7 + k], recv_sem=recv.at[i * 7 + k], device_id=to, device_id_type=MESH_ID)

        own = [pltpu.make_async_copy(srcs[i], lands[i].at[index(*me)], loc.at[i]) for i in range(n)]
        for cp in own:
            cp.start()
        started = []
        for i in range(n):
            started.append(copy(i, 0, me, sibling, True))
            started += [copy(i, 1 + j, me, (*chip, c), True) for j, chip in enumerate(chips)]
        for cp in started:
            cp.start()
        for j, chip in enumerate(chips):
            for i in range(n):
                copy(i, 1 + j, (*chip, c), me, False).wait_recv()
                fwd = copy(i, 4 + j, (*chip, c), sibling, False)
                fwd.start()
                started.append(fwd)
        for i in range(n):
            copy(i, 0, sibling, me, False).wait_recv()
            for j, chip in enumerate(chips):
                copy(i, 4 + j, (*chip, 1 - c), me, False).wait_recv()
        for cp in started:
            cp.wait_send()
        for cp in own:
            cp.wait()

    launch()
    return [r[...] for r in lands]


def _pair_swap_sc(blocks, name, collective_id):
    n = len(blocks)
    srcs = [jax.new_ref(b, memory_space=pltpu.MemorySpace.HBM) for b in blocks]
    lands = [jax.empty_ref(_sds((4,) + b.shape[1:], b.dtype), memory_space=pltpu.MemorySpace.HBM) for b in blocks]

    @pl.kernel(mesh=plsc.ScalarSubcoreMesh(axis_name="seq", num_cores=1), name=name,
               scratch_types=(pltpu.SemaphoreType.DMA((4 * n,)), pltpu.SemaphoreType.DMA((4 * n,))),
               compiler_params=pltpu.CompilerParams(collective_id=collective_id))
    def launch(send, recv):
        x, y, c = lax.axis_index("x"), lax.axis_index("y"), lax.axis_index("c")
        sibling = (x, y, 1 - c)
        barrier = pltpu.get_barrier_semaphore()
        pl.semaphore_signal(barrier, inc=1, device_id=sibling, device_id_type=MESH_ID)
        pl.semaphore_wait(barrier, 1)
        copies = [pltpu.make_async_remote_copy(src_ref=srcs[i].at[2 * q + 1 - c], dst_ref=lands[i].at[q], send_sem=send.at[4 * i + q],
                                               recv_sem=recv.at[4 * i + q], device_id=sibling, device_id_type=MESH_ID)
                  for i in range(n) for q in range(4)]
        for cp in copies:
            cp.start()
        for cp in copies:
            cp.wait()

    launch()
    return [r[...] for r in lands]


def _pair_add(blocks, theirs, name):
    _, r, c_ = blocks.shape
    mine = lax.dynamic_index_in_dim(blocks.reshape(4, 2, r, c_), lax.axis_index("c"), axis=1, keepdims=False)
    tr = _pick(r, (256, 128, 64, 16))
    spec = (None, tr, c_)
    return _tiled(lambda a, b: (a.astype(F32) + b.astype(F32),), [(mine, spec, lambda q, i: (q, i, 0)), (theirs, spec, lambda q, i: (q, i, 0))],
                  [((4, r, c_), BF16, spec, lambda q, i: (q, i, 0))], (4, r // tr), name)[0]


def _chip_scatter_sc(sums, name, collective_id):
    n = len(sums)
    srcs = [jax.new_ref(s, memory_space=pltpu.MemorySpace.HBM) for s in sums]
    lands = [jax.empty_ref(_sds(s.shape, s.dtype), memory_space=pltpu.MemorySpace.HBM) for s in sums]

    @pl.kernel(mesh=plsc.ScalarSubcoreMesh(axis_name="seq", num_cores=1), name=name,
               scratch_types=(pltpu.SemaphoreType.DMA((3 * n,)), pltpu.SemaphoreType.DMA((3 * n,)), pltpu.SemaphoreType.DMA((n,))),
               compiler_params=pltpu.CompilerParams(collective_id=collective_id))
    def launch(send, recv, loc):
        x, y, c = lax.axis_index("x"), lax.axis_index("y"), lax.axis_index("c")
        chips = [(1 - x, y), (x, 1 - y), (1 - x, 1 - y)]
        my_chip = 2 * x + y
        barrier = pltpu.get_barrier_semaphore()
        for chip in chips:
            pl.semaphore_signal(barrier, inc=1, device_id=(*chip, c), device_id_type=MESH_ID)
        pl.semaphore_wait(barrier, 3)
        own = [pltpu.make_async_copy(srcs[i].at[my_chip], lands[i].at[my_chip], loc.at[i]) for i in range(n)]
        for cp in own:
            cp.start()
        sends = [pltpu.make_async_remote_copy(src_ref=srcs[i].at[2 * px + py], dst_ref=lands[i].at[my_chip], send_sem=send.at[3 * i + j],
                                              recv_sem=recv.at[3 * i + j], device_id=(px, py, c), device_id_type=MESH_ID)
                 for j, (px, py) in enumerate(chips) for i in range(n)]
        for cp in sends:
            cp.start()
        for j, (px, py) in enumerate(chips):
            for i in range(n):
                pltpu.make_async_remote_copy(src_ref=srcs[i].at[my_chip], dst_ref=lands[i].at[2 * px + py], send_sem=send.at[3 * i + j],
                                             recv_sem=recv.at[3 * i + j], device_id=(px, py, c), device_id_type=MESH_ID).wait_recv()
        for cp in sends:
            cp.wait_send()
        for cp in own:
            cp.wait()

    launch()
    return [r[...] for r in lands]


class _InFlight:
    def __init__(self, send, recv, srcs, lands, token, scatter):
        self.send, self.recv, self.srcs, self.lands, self.token, self.scatter = send, recv, srcs, lands, token, scatter


SEM_SPEC = pl.BlockSpec(memory_space=pltpu.SEMAPHORE)
SIDE_EFFECT = pltpu.SideEffectType.DATAFLOW_SIDE_EFFECTING


def _exchange_start(arrs, name, scatter):
    n = len(arrs)

    def body(*refs):
        ins, lands = refs[:n], refs[n:2 * n]
        send, recv = refs[2 * n], refs[2 * n + 1]
        token = refs[4 * n + 2]
        loc = refs[4 * n + 3]
        me, peers = _peers()
        own = [pltpu.make_async_copy(ins[i].at[me] if scatter else ins[i], lands[i].at[me], loc.at[i]) for i in range(n)]
        for cp in own:
            cp.start()
        for cp in own:
            cp.wait()
        for k, peer, pidx in peers:
            for i in range(n):
                src = ins[i].at[pidx] if scatter else ins[i]
                pltpu.make_async_remote_copy(src_ref=src, dst_ref=lands[i].at[me], send_sem=send.at[i * 7 + k - 1],
                                             recv_sem=recv.at[i * 7 + k - 1], device_id=peer, device_id_type=MESH_ID).start()
        token[...] = jnp.zeros_like(token)

    land_shapes = [a.shape if scatter else (N_DEV,) + a.shape for a in arrs]
    hbm = lambda a: pltpu.with_memory_space_constraint(a, pltpu.HBM)
    args = [hbm(a) for a in arrs] + [hbm(lax.empty(s, a.dtype)) for s, a in zip(land_shapes, arrs)]
    out_shape = ([pltpu.SemaphoreType.DMA((7 * n,)), pltpu.SemaphoreType.DMA((7 * n,))]
                 + [pltpu.HBM(a.shape, a.dtype) for a in arrs] + [pltpu.HBM(s, a.dtype) for s, a in zip(land_shapes, arrs)]
                 + [_sds((8, LANE), F32)])
    res = pl.pallas_call(
        body, name=name, out_shape=out_shape, in_specs=[HBM_SPEC] * (2 * n),
        out_specs=[SEM_SPEC, SEM_SPEC] + [HBM_SPEC] * (2 * n) + [pl.BlockSpec(memory_space=pltpu.VMEM)],
        input_output_aliases={i: 2 + i for i in range(2 * n)},
        scratch_shapes=[pltpu.SemaphoreType.DMA((n,))],
        compiler_params=pltpu.CompilerParams(has_side_effects=SIDE_EFFECT))(*args)
    return _InFlight(res[0], res[1], list(res[2:2 + n]), list(res[2 + n:2 + 2 * n]), res[2 + 2 * n], scatter)


def _exchange_wait(h, after, name):
    n = len(h.srcs)
    scatter = h.scatter
    after = list(after)

    def body(*refs):
        ins, lands = refs[:n], refs[n:2 * n]
        send, recv = refs[2 * n], refs[2 * n + 1]
        _, peers = _peers()
        for k, peer, pidx in peers:
            for i in range(n):
                src = ins[i].at[pidx] if scatter else ins[i]
                cp = pltpu.make_async_remote_copy(src_ref=src, dst_ref=lands[i].at[pidx], send_sem=send.at[i * 7 + k - 1],
                                                  recv_sem=recv.at[i * 7 + k - 1], device_id=peer, device_id_type=MESH_ID)
                cp.wait_send()
                cp.wait_recv()

    res = pl.pallas_call(
        body, name=name, out_shape=[pltpu.HBM(a.shape, a.dtype) for a in h.srcs + h.lands],
        in_specs=[HBM_SPEC] * (2 * n) + [SEM_SPEC, SEM_SPEC] + [pl.BlockSpec(memory_space=pl.ANY)] * len(after),
        out_specs=[HBM_SPEC] * (2 * n), input_output_aliases={i: i for i in range(2 * n)},
        compiler_params=pltpu.CompilerParams(has_side_effects=SIDE_EFFECT))(*h.srcs, *h.lands, h.send, h.recv, *after)
    return list(res[n:])


def _mm_xw(x, g, out_dtype, name, tm_c=(768, 512, 384, 256, 128, 64, 16), tn_c=(1024, 768, 512, 256, 128), tk_c=(2048, 1024, 768, 512, 256)):
    m, r = x.shape
    nb, r2, cl = g.shape
    assert r == r2
    tm, tn, tk = _pick(m, tm_c), _pick(cl, tn_c), _pick(r, tk_c)
    q, nk = cl // tn, r // tk

    def body(x_ref, g_ref, o_ref, *acc):
        p = lax.dot_general(x_ref[...].astype(BF16), g_ref[...], (((1,), (0,)), ((), ())), preferred_element_type=F32)
        if nk == 1:
            o_ref[...] = p.astype(o_ref.dtype)
        else:
            k = pl.program_id(2)

            @pl.when(k == 0)
            def _():
                acc[0][...] = p

            @pl.when(k > 0)
            def _():
                acc[0][...] += p

            @pl.when(k == nk - 1)
            def _():
                o_ref[...] = acc[0][...].astype(o_ref.dtype)

    return _pcall(
        body, name=name, grid=(m // tm, nb * q, nk),
        in_specs=[pl.BlockSpec((tm, tk), lambda i, j, k: (i, k)), pl.BlockSpec((None, tk, tn), lambda i, j, k: (j // q, k, j % q))],
        out_specs=pl.BlockSpec((tm, tn), lambda i, j, k: (i, j)),
        out_shape=_sds((m, nb * cl), out_dtype),
        scratch=[] if nk == 1 else [pltpu.VMEM((tm, tn), F32)])(x, g)


def _mm_dyw(dy, g, out_dtype, name, init=None, tm_c=(768, 512, 384, 256, 128), tn_c=(1024, 512, 256, 128), tk_c=(2048, 1536, 1024, 768, 512, 256, 128)):
    m, n = dy.shape
    nb, r, cl = g.shape
    assert n == nb * cl
    tm, tn, tk = _pick(m, tm_c), _pick(r, tn_c), _pick(cl, tk_c)
    q = cl // tk
    nk = nb * q
    has_init = init is not None

    def body(*refs):
        if has_init:
            dy_ref, g_ref, i_ref, o_ref, acc = refs
        else:
            dy_ref, g_ref, o_ref, acc = refs
        k = pl.program_id(2)
        p = lax.dot_general(dy_ref[...].astype(BF16), g_ref[...], (((1,), (1,)), ((), ())), preferred_element_type=F32)

        @pl.when(k == 0)
        def _():
            acc[...] = p + i_ref[...].astype(F32) if has_init else p

        @pl.when(k > 0)
        def _():
            acc[...] += p

        @pl.when(k == nk - 1)
        def _():
            o_ref[...] = acc[...].astype(o_ref.dtype)

    in_specs = [pl.BlockSpec((tm, tk), lambda i, j, k: (i, k)), pl.BlockSpec((None, tn, tk), lambda i, j, k: (k // q, j, k % q))]
    args = [dy, g]
    if has_init:
        in_specs.append(pl.BlockSpec((tm, tn), lambda i, j, k: (i, j)))
        args.append(init)
    return _pcall(body, name=name, grid=(m // tm, r // tn, nk), in_specs=in_specs,
                  out_specs=pl.BlockSpec((tm, tn), lambda i, j, k: (i, j)), out_shape=_sds((m, r), out_dtype),
                  scratch=[pltpu.VMEM((tm, tn), F32)])(*args)


def _mm_xtdy(x, dy, nb, out_dtype, name, tm_c=(1024, 512, 256, 128), tn_c=(768, 512, 256, 128), tk_c=(2304, 2048, 1152, 1024, 768, 512, 256, 128, 16)):
    t, r = x.shape
    t2, n = dy.shape
    assert t == t2 and n % nb == 0
    cl = n // nb
    tm, tn, tk = _pick(r, tm_c), _pick(cl, tn_c), _pick(t, tk_c)
    q, nk = cl // tn, t // tk

    def body(x_ref, dy_ref, o_ref, *acc):
        p = lax.dot_general(x_ref[...].astype(BF16), dy_ref[...].astype(BF16), (((0,), (0,)), ((), ())), preferred_element_type=F32)
        if nk == 1:
            o_ref[...] = p.astype(o_ref.dtype)
        else:
            k = pl.program_id(2)

            @pl.when(k == 0)
            def _():
                acc[0][...] = p

            @pl.when(k > 0)
            def _():
                acc[0][...] += p

            @pl.when(k == nk - 1)
            def _():
                o_ref[...] = acc[0][...].astype(o_ref.dtype)

    return _pcall(
        body, name=name, grid=(r // tm, nb * q, nk),
        in_specs=[pl.BlockSpec((tk, tm), lambda i, j, k: (k, i)), pl.BlockSpec((tk, tn), lambda i, j, k: (k, j))],
        out_specs=pl.BlockSpec((None, tm, tn), lambda i, j, k: (j // q, i, j % q)),
        out_shape=_sds((nb, r, cl), out_dtype),
        scratch=[] if nk == 1 else [pltpu.VMEM((tm, tn), F32)])(x, dy)


def _mm_f32(a, b, name, trans_b=False):
    dims = (((1,), (1,)), ((), ())) if trans_b else (((1,), (0,)), ((), ()))
    n = b.shape[0] if trans_b else b.shape[1]

    def body(a_ref, b_ref, o_ref):
        o_ref[...] = lax.dot_general(a_ref[...], b_ref[...], dims, precision=HIGHEST, preferred_element_type=F32)

    return _pcall(body, name=name, out_shape=_sds((a.shape[0], n), F32))(a, b)


def _tiled(fn, ins, outs, grid, name, acc=()):
    n_in = len(ins)
    grid = tuple(grid) or (1,)
    nd = len(grid)

    def body(*refs):
        vals = fn(*[r[...] for r in refs[:n_in]])
        if not isinstance(vals, (tuple, list)):
            vals = (vals,)
        first = None
        for o, (ref, v) in enumerate(zip(refs[n_in:], vals)):
            if o in acc:
                if first is None:
                    first = pl.program_id(0) == 0
                    for a in range(1, nd):
                        first = jnp.logical_and(first, pl.program_id(a) == 0)

                @pl.when(first)
                def _(ref=ref, v=v):
                    ref[...] = v.astype(ref.dtype)

                @pl.when(jnp.logical_not(first))
                def _(ref=ref, v=v):
                    ref[...] += v.astype(ref.dtype)
            else:
                ref[...] = v.astype(ref.dtype)

    res = _pcall(body, name=name, grid=grid,
                 in_specs=[pl.BlockSpec(b, im) for _, b, im in ins],
                 out_specs=[pl.BlockSpec(b, im) for _, _, b, im in outs],
                 out_shape=[_sds(s, d) for s, d, _, _ in outs])(*[a for a, _, _ in ins])
    return list(res)


def _rows(a, tr):
    return (a, (tr, a.shape[1]), lambda i, *_: (i, 0))


def _const(a):
    nd = a.ndim
    return (a, a.shape, lambda *_: (0,) * nd)


def _cast_bf16(w, name):
    r, c = w.shape
    tr = _pick(r, (256, 128, 64, 16))
    return _tiled(lambda v: v, [_rows(w, tr)], [((r, c), BF16, (tr, c), lambda i: (i, 0))], (r // tr,), name)[0]


def _f_norm_mod(x, g, sh, sc):
    y = x * lax.rsqrt(jnp.mean(x * x, axis=-1, keepdims=True) + EPS) * g
    return y * (1.0 + sc) + sh


def _rope_partner_impl(y):
    nf = HEAD_DIM // 4
    lane = lax.broadcasted_iota(jnp.int32, y.shape, 1)
    return jnp.where(lane % (2 * nf) < nf, pltpu.roll(y, HEAD_DIM - nf, 1), pltpu.roll(y, nf, 1))


_rope_partner = jax.custom_vjp(_rope_partner_impl)
_rope_partner.defvjp(lambda y: (_rope_partner_impl(y), None), lambda _, ct: (_rope_partner_impl(ct),))


def _f_qk(t, g, cos, sin):
    y = t * lax.rsqrt(jnp.mean(t * t, axis=-1, keepdims=True) + EPS) * g
    return y * cos + _rope_partner(y) * sin


def _f_readout(of, ob, gate, g):
    o = of + ob
    on = o * lax.rsqrt(jnp.mean(o * o, axis=-1, keepdims=True) + EPS) * g
    return on * (gate * jax.nn.sigmoid(gate))


def _f_merge(ga, gb, pa, pb):
    return jax.nn.sigmoid(ga) * pa + jax.nn.sigmoid(gb) * pb


def _f_resid(x, gate, m):
    return x + gate * m


def _norm_mod_fwd(xcat, g, mods, n_ctx_tiles, tr):
    tt, d = xcat.shape
    which = lambda i: (jnp.where(i >= n_ctx_tiles, 1, 0), 0, 0, 0)

    def fn(x, gg, md):
        return _f_norm_mod(x, gg, md[0], md[1])

    return _tiled(fn, [_rows(xcat, tr), _const(g), (mods, (None, 2, 1, d), which)],
                  [((tt, d), BF16, (tr, d), lambda i: (i, 0))], (tt // tr,), "norm_mod_fwd")[0]


def _norm_mod_bwd(xcat, g, mods, dh, extra, n_ctx_tiles, tr, name):
    tt, d = xcat.shape
    nt = tt // tr
    has_extra = extra is not None

    def body(*refs):
        if has_extra:
            x_ref, g_ref, m_ref, dh_ref, e_ref, dx_ref, dg_ref, dm_ref = refs
        else:
            x_ref, g_ref, m_ref, dh_ref, dx_ref, dg_ref, dm_ref = refs
        i = pl.program_id(0)
        md = m_ref[...]
        _, vjp = jax.vjp(_f_norm_mod, x_ref[...], g_ref[...], md[0], md[1])
        dx, dg, dsh, dsc = vjp(dh_ref[...].astype(F32))
        dx_ref[...] = dx + e_ref[...] if has_extra else dx

        @pl.when(i == 0)
        def _():
            dg_ref[...] = dg

        @pl.when(i > 0)
        def _():
            dg_ref[...] += dg

        fresh = jnp.logical_or(i == 0, i == n_ctx_tiles)

        @pl.when(fresh)
        def _():
            dm_ref[0] = dsh
            dm_ref[1] = dsc

        @pl.when(jnp.logical_not(fresh))
        def _():
            dm_ref[0] += dsh
            dm_ref[1] += dsc

    which = lambda i: (jnp.where(i >= n_ctx_tiles, 1, 0), 0, 0, 0)
    row = pl.BlockSpec((tr, d), lambda i: (i, 0))
    in_specs = [row, pl.BlockSpec((1, d), lambda i: (0, 0)), pl.BlockSpec((None, 2, 1, d), which), row]
    args = [xcat, g, mods, dh]
    if has_extra:
        in_specs.append(row)
        args.append(extra)
    return _pcall(body, name=name, grid=(nt,), in_specs=in_specs,
                  out_specs=[row, pl.BlockSpec((1, d), lambda i: (0, 0)), pl.BlockSpec((None, 2, 1, d), which)],
                  out_shape=[_sds((tt, d), F32), _sds((1, d), F32), _sds((2, 2, 1, d), F32)])(*args)


def _hgrn_tri(reverse):
    t = np.arange(HGRN_BLOCK)
    tri = (t[None, :] >= t[:, None]) if reverse else (t[None, :] <= t[:, None])
    tri = tri.astype(np.float32)
    return jnp.asarray(tri), jnp.asarray(tri.T.copy())


def _hgrn_rowblock(n, n_ctx_blocks, n_blocks, reverse):
    if not reverse:
        return n
    return jnp.where(n < n_ctx_blocks, n_ctx_blocks - 1 - n, n_blocks - 1 - n + n_ctx_blocks)


def _hgrn_gates(fl, lb):
    sg = jax.nn.sigmoid(fl)
    f = lb + (1.0 - lb) * sg
    return sg, f, jnp.log(f), 1.0 - f


def _hgrn_intra_mask(reverse):
    tio = lax.broadcasted_iota(jnp.int32, (HGRN_BLOCK, HEAD_DIM), 0)
    return (lambda s: tio <= s) if reverse else (lambda s: tio >= s)


def _unrolled(nblk, u, fn, init):
    assert nblk % u == 0

    def trip(b, c):
        for j in range(u):
            c = fn(b * u + j, j, c)
        return c

    return lax.fori_loop(0, nblk // u, trip, init)


HGRN_UNROLL_FWD = (4, 8)
HGRN_UNROLL_BWD = (8, 4)


def _hgrn_fwd(p, lb, seg_f, reverse, n_ctx_rows):
    tt = p.shape[0]
    hb = HGRN_BLOCK
    nblk, nctx = tt // hb, n_ctx_rows // hb
    tri, _ = _hgrn_tri(reverse)
    u1, u3 = HGRN_UNROLL_FWD

    def body(q_ref, f_ref, v_ref, lb_ref, tri_ref, o_ref, st_all, dec_all, qe_buf, cum_blk, k_blk, v_blk):
        mask = _hgrn_intra_mask(reverse)
        lbv = lb_ref[...]

        def phase1(n, slot, c):
            r0 = pl.multiple_of(_hgrn_rowblock(n, nctx, nblk, reverse) * hb, hb)
            q, v = q_ref[pl.ds(r0, hb), :], v_ref[pl.ds(r0, hb), :]
            _, f, g, k = _hgrn_gates(f_ref[pl.ds(r0, hb), :], lbv)
            cum = jnp.dot(tri_ref[...], g, precision=HIGHEST, preferred_element_type=F32)
            tot = jnp.sum(g, axis=0, keepdims=True)
            cum_blk[slot] = cum
            k_blk[slot] = k
            v_blk[slot] = v
            oi = jnp.zeros((hb, HEAD_DIM), F32)
            for s in range(hb):
                e = jnp.where(mask(s), jnp.exp(jnp.minimum(cum - cum_blk[slot, s:s + 1, :], 0.0)), 0.0)
                a_s = jnp.sum(q * e * k_blk[slot, s:s + 1, :], axis=-1, keepdims=True)
                oi = oi + a_s * v_blk[slot, s:s + 1, :]
            o_ref[pl.ds(r0, hb), :] = oi
            qe_buf[pl.ds(r0, hb), :] = q * jnp.exp(cum)
            kl = k * jnp.exp(tot - cum)
            st_all[n] = lax.dot_general(v.astype(BF16), kl.astype(BF16), (((0,), (0,)), ((), ())), preferred_element_type=F32)
            dec_all[pl.ds(n, 1), :] = jnp.exp(tot)
            return c

        _unrolled(nblk, u1, phase1, 0)

        def phase2(n, st):
            kv = st_all[n]
            st_all[n] = st
            return st * dec_all[pl.ds(n, 1), :] + kv

        lax.fori_loop(0, nblk, phase2, jnp.zeros((HEAD_DIM, HEAD_DIM), F32))

        def phase3(n, slot, c):
            r0 = pl.multiple_of(_hgrn_rowblock(n, nctx, nblk, reverse) * hb, hb)
            o_ref[pl.ds(r0, hb), :] += lax.dot_general(qe_buf[pl.ds(r0, hb), :].astype(BF16), st_all[n].astype(BF16),
                                                        (((1,), (1,)), ((), ())), preferred_element_type=F32)
            return c

        _unrolled(nblk, u3, phase3, 0)

    col = lambda seg: pl.BlockSpec((tt, HEAD_DIM), lambda h, seg=seg: (0, seg * N_HEADS + h))
    blk = pltpu.VMEM((u1, hb, HEAD_DIM), F32)
    return _pcall(
        body, name="hgrn_fwd_rev" if reverse else "hgrn_fwd", grid=(N_HEADS,),
        in_specs=[col(0), col(seg_f), col(3), pl.BlockSpec((1, HEAD_DIM), lambda h: (0, h)), pl.BlockSpec((hb, hb), lambda h: (0, 0))],
        out_specs=pl.BlockSpec((tt, HEAD_DIM), lambda h: (0, h)),
        out_shape=_sds((tt, N_HEADS * HEAD_DIM), F32),
        scratch=[pltpu.VMEM((nblk, HEAD_DIM, HEAD_DIM), F32), pltpu.VMEM((nblk, HEAD_DIM), F32), pltpu.VMEM((tt, HEAD_DIM), F32),
                 blk, blk, blk])(p, p, p, lb, tri)


def _hgrn_bwd(p, lb, do, seg_f, reverse, n_ctx_rows, prev):
    tt = p.shape[0]
    hb = HGRN_BLOCK
    nblk, nctx = tt // hb, n_ctx_rows // hb
    tri, tri_t = _hgrn_tri(reverse)
    last_row = 0 if reverse else hb - 1
    has_prev = prev is not None
    u1, u3 = HGRN_UNROLL_BWD

    def body(*refs):
        q_ref, f_ref, v_ref, lb_ref, tri_ref, trit_ref, do_ref = refs[:7]
        refs = refs[7:]
        if has_prev:
            pq_ref, pv_ref = refs[:2]
            refs = refs[2:]
        dq_ref, dfl_ref, dv_ref, dlb_ref, st_all, dd_all, dec_all, cum_buf, cum_blk, k_blk, v_blk, dk_blk, dv_blk = refs
        mask = _hgrn_intra_mask(reverse)
        lbv = lb_ref[...]
        tio = lax.broadcasted_iota(jnp.int32, (hb, HEAD_DIM), 0)

        def rows_of(n):
            rb = _hgrn_rowblock(n, nctx, nblk, reverse)
            return rb, pl.multiple_of(rb * hb, hb)

        def load_do(rb):
            lat0 = pl.multiple_of(jnp.maximum(rb - nctx, 0) * hb, hb)
            return jnp.where(rb >= nctx, do_ref[pl.ds(lat0, hb), :], 0.0)

        def phase1(n, slot, c):
            rb, r0 = rows_of(n)
            q, v = q_ref[pl.ds(r0, hb), :], v_ref[pl.ds(r0, hb), :]
            _, f, g, k = _hgrn_gates(f_ref[pl.ds(r0, hb), :], lbv)
            cum = jnp.dot(tri_ref[...], g, precision=HIGHEST, preferred_element_type=F32)
            tot = jnp.sum(g, axis=0, keepdims=True)
            cum_buf[pl.ds(r0, hb), :] = cum
            kl = k * jnp.exp(tot - cum)
            st_all[n] = lax.dot_general(v.astype(BF16), kl.astype(BF16), (((0,), (0,)), ((), ())), preferred_element_type=F32)
            dec_all[pl.ds(n, 1), :] = jnp.exp(tot)
            qe = q * jnp.exp(cum)
            dd_all[n] = lax.dot_general(load_do(rb).astype(BF16), qe.astype(BF16), (((0,), (0,)), ((), ())), preferred_element_type=F32)
            return c

        _unrolled(nblk, u1, phase1, 0)

        def phase2(n, st):
            kv = st_all[n]
            st_all[n] = st
            return st * dec_all[pl.ds(n, 1), :] + kv

        lax.fori_loop(0, nblk, phase2, jnp.zeros((HEAD_DIM, HEAD_DIM), F32))

        def phase2r(i, dst):
            n = nblk - 1 - i
            u = dd_all[n]
            dd_all[n] = dst
            return u + dst * dec_all[pl.ds(n, 1), :]

        lax.fori_loop(0, nblk, phase2r, jnp.zeros((HEAD_DIM, HEAD_DIM), F32))

        def phase3(n, slot, dlb):
            rb, r0 = rows_of(n)
            q, v = q_ref[pl.ds(r0, hb), :], v_ref[pl.ds(r0, hb), :]
            sg, f, g, k = _hgrn_gates(f_ref[pl.ds(r0, hb), :], lbv)
            cum = cum_buf[pl.ds(r0, hb), :]
            tot = jnp.sum(g, axis=0, keepdims=True)
            dob = load_do(rb)
            st, dst = st_all[n], dd_all[n]
            e_cum = jnp.exp(cum)
            e_rest = jnp.exp(tot - cum)
            dq = jnp.dot(dob.astype(BF16), st.astype(BF16), preferred_element_type=F32) * e_cum
            dk_inter = jnp.dot(v.astype(BF16), dst.astype(BF16), preferred_element_type=F32) * e_rest
            dv = lax.dot_general((k * e_rest).astype(BF16), dst.astype(BF16), (((1,), (1,)), ((), ())), preferred_element_type=F32)
            cum_blk[slot] = cum
            k_blk[slot] = k
            v_blk[slot] = v
            for s in range(hb):
                e = jnp.where(mask(s), jnp.exp(jnp.minimum(cum - cum_blk[slot, s:s + 1, :], 0.0)), 0.0)
                a_s = jnp.sum(q * e * k_blk[slot, s:s + 1, :], axis=-1, keepdims=True)
                da_s = jnp.sum(dob * v_blk[slot, s:s + 1, :], axis=-1, keepdims=True)
                gs = da_s * e
                dq = dq + gs * k_blk[slot, s:s + 1, :]
                dk_blk[slot, s:s + 1, :] = jnp.sum(gs * q, axis=0, keepdims=True)
                dv_blk[slot, s:s + 1, :] = jnp.sum(a_s * dob, axis=0, keepdims=True)
            dk = dk_inter + dk_blk[slot]
            dv = dv + dv_blk[slot]
            d_tot = jnp.sum(k * dk_inter, axis=0, keepdims=True) + jnp.exp(tot) * jnp.sum(dst * st, axis=0, keepdims=True)
            dcum = q * dq - k * dk + jnp.where(tio == last_row, d_tot, 0.0)
            dg = jnp.dot(trit_ref[...], dcum, precision=HIGHEST, preferred_element_type=F32)
            df = dg / f - dk
            if has_prev:
                dq = dq + pq_ref[pl.ds(r0, hb), :].astype(F32)
                dv = dv + pv_ref[pl.ds(r0, hb), :].astype(F32)
            dq_ref[pl.ds(r0, hb), :] = dq.astype(dq_ref.dtype)
            dv_ref[pl.ds(r0, hb), :] = dv.astype(dv_ref.dtype)
            dfl_ref[pl.ds(r0, hb), :] = (df * (1.0 - lbv) * sg * (1.0 - sg)).astype(dfl_ref.dtype)
            return dlb + jnp.sum(df * (1.0 - sg), axis=0, keepdims=True)

        dlb_ref[...] = _unrolled(nblk, u3, phase3, jnp.zeros((1, HEAD_DIM), F32))

    col = lambda seg: pl.BlockSpec((tt, HEAD_DIM), lambda h, seg=seg: (0, seg * N_HEADS + h))
    head = pl.BlockSpec((tt, HEAD_DIM), lambda h: (0, h))
    lbs = pl.BlockSpec((1, HEAD_DIM), lambda h: (0, h))
    tris = pl.BlockSpec((hb, hb), lambda h: (0, 0))
    in_specs = [col(0), col(seg_f), col(3), lbs, tris, tris, pl.BlockSpec((do.shape[0], HEAD_DIM), lambda h: (0, h))]
    args = [p, p, p, lb, tri, tri_t, do]
    mid = F32 if not has_prev else BF16
    if has_prev:
        in_specs += [head, head]
        args += list(prev)
    w = N_HEADS * HEAD_DIM
    blk = pltpu.VMEM((u3, hb, HEAD_DIM), F32)
    return _pcall(
        body, name="hgrn_bwd_rev" if reverse else "hgrn_bwd", grid=(N_HEADS,), in_specs=in_specs,
        out_specs=[head, head, head, lbs],
        out_shape=[_sds((tt, w), mid), _sds((tt, w), BF16), _sds((tt, w), mid), _sds((1, w), F32)],
        scratch=[pltpu.VMEM((nblk, HEAD_DIM, HEAD_DIM), F32), pltpu.VMEM((nblk, HEAD_DIM, HEAD_DIM), F32), pltpu.VMEM((nblk, HEAD_DIM), F32),
                 pltpu.VMEM((tt, HEAD_DIM), F32), blk, blk, blk, blk, blk])(*args)


def _na_geometry(rows):
    r = pl.program_id(1)
    rs = jnp.clip(r - WIN_R // 2, 0, rows - WIN_R)
    return r, rs, r - rs


def _na_scores(q, kb, kc, bias):
    scale = HEAD_DIM ** -0.5
    nt = (((1,), (1,)), ((), ()))
    sb = lax.dot_general(q, kb, nt, preferred_element_type=F32) * scale + bias
    sc = lax.dot_general(q, kc, nt, preferred_element_type=F32) * scale
    m = jnp.maximum(jnp.max(sb, axis=-1, keepdims=True), jnp.max(sc, axis=-1, keepdims=True))
    pb, pc = jnp.exp(sb - m), jnp.exp(sc - m)
    inv = 1.0 / (jnp.sum(pb, axis=-1, keepdims=True) + jnp.sum(pc, axis=-1, keepdims=True))
    return pb * inv, pc * inv


def _na_fwd(qn, kall, p, seg_v, bias, n_ctx_rows):
    t, tt = qn.shape[0], kall.shape[0]
    rows = t // GRID_W
    nband = WIN_R * GRID_W

    def body(q_ref, k_ref, v_ref, b_ref, o_ref):
        r, rs, _ = _na_geometry(rows)
        k0 = pl.multiple_of(n_ctx_rows + rs * GRID_W, GRID_W)
        q = q_ref[...]
        kb, kc = k_ref[pl.ds(k0, nband), :], k_ref[pl.ds(0, n_ctx_rows), :]
        vb, vc = v_ref[pl.ds(k0, nband), :].astype(BF16), v_ref[pl.ds(0, n_ctx_rows), :].astype(BF16)
        pb, pc = _na_scores(q, kb, kc, b_ref[...])
        o = jnp.dot(pb.astype(BF16), vb, preferred_element_type=F32) + jnp.dot(pc.astype(BF16), vc, preferred_element_type=F32)
        o_ref[...] = o.astype(o_ref.dtype)

    return _pcall(
        body, name="na_fwd", grid=(N_HEADS, rows),
        in_specs=[pl.BlockSpec((GRID_W, HEAD_DIM), lambda h, r: (r, h)),
                  pl.BlockSpec((tt, HEAD_DIM), lambda h, r: (0, h)),
                  pl.BlockSpec((tt, HEAD_DIM), lambda h, r: (0, seg_v * N_HEADS + h)),
                  pl.BlockSpec((None, None, GRID_W, nband), lambda h, r: (h, r - jnp.clip(r - WIN_R // 2, 0, rows - WIN_R), 0, 0))],
        out_specs=pl.BlockSpec((GRID_W, HEAD_DIM), lambda h, r: (r, h)),
        out_shape=_sds((t, N_HEADS * HEAD_DIM), BF16))(qn, kall, p, bias)


def _na_bwd(qn, kall, p, seg_v, bias, do, n_ctx_rows):
    t, tt = qn.shape[0], kall.shape[0]
    rows = t // GRID_W
    nband = WIN_R * GRID_W
    scale = HEAD_DIM ** -0.5
    tn = (((0,), (0,)), ((), ()))
    nt = (((1,), (1,)), ((), ()))

    def body(q_ref, k_ref, v_ref, b_ref, do_ref, dq_ref, dk_ref, dv_ref, db_ref, dv_acc):
        r, rs, var = _na_geometry(rows)
        k0 = pl.multiple_of(n_ctx_rows + rs * GRID_W, GRID_W)
        q = q_ref[...]
        kb, kc = k_ref[pl.ds(k0, nband), :], k_ref[pl.ds(0, n_ctx_rows), :]
        vb, vc = v_ref[pl.ds(k0, nband), :].astype(BF16), v_ref[pl.ds(0, n_ctx_rows), :].astype(BF16)
        pb, pc = _na_scores(q, kb, kc, b_ref[...])
        dob = do_ref[...].astype(BF16)
        o = jnp.dot(pb.astype(BF16), vb, preferred_element_type=F32) + jnp.dot(pc.astype(BF16), vc, preferred_element_type=F32)
        delta = jnp.sum(do_ref[...].astype(F32) * o, axis=-1, keepdims=True)
        dsb = pb * (lax.dot_general(dob, vb, nt, preferred_element_type=F32) - delta)
        dsc = pc * (lax.dot_general(dob, vc, nt, preferred_element_type=F32) - delta)
        dsb16, dsc16 = dsb.astype(BF16), dsc.astype(BF16)
        dq_ref[...] = (jnp.dot(dsb16, kb, preferred_element_type=F32) + jnp.dot(dsc16, kc, preferred_element_type=F32)) * scale

        @pl.when(r == 0)
        def _():
            dk_ref[...] = jnp.zeros_like(dk_ref)
            dv_acc[...] = jnp.zeros_like(dv_acc)

        dk_ref[pl.ds(k0, nband), :] += lax.dot_general(dsb16, q, tn, preferred_element_type=F32) * scale
        dk_ref[pl.ds(0, n_ctx_rows), :] += lax.dot_general(dsc16, q, tn, preferred_element_type=F32) * scale
        dv_acc[pl.ds(k0, nband), :] += lax.dot_general(pb.astype(BF16), dob, tn, preferred_element_type=F32)
        dv_acc[pl.ds(0, n_ctx_rows), :] += lax.dot_general(pc.astype(BF16), dob, tn, preferred_element_type=F32)

        @pl.when(r == rows - 1)
        def _():
            dv_ref[...] = dv_acc[...].astype(dv_ref.dtype)

        fresh = jnp.logical_or(r <= WIN_R // 2, r > rows - WIN_R // 2)

        @pl.when(fresh)
        def _():
            db_ref[...] = dsb

        @pl.when(jnp.logical_not(fresh))
        def _():
            db_ref[...] += dsb

    variant = lambda h, r: (h, r - jnp.clip(r - WIN_R // 2, 0, rows - WIN_R), 0, 0)
    head_all = pl.BlockSpec((tt, HEAD_DIM), lambda h, r: (0, h))
    qspec = pl.BlockSpec((GRID_W, HEAD_DIM), lambda h, r: (r, h))
    w = N_HEADS * HEAD_DIM
    return _pcall(
        body, name="na_bwd", grid=(N_HEADS, rows),
        in_specs=[qspec, head_all, pl.BlockSpec((tt, HEAD_DIM), lambda h, r: (0, seg_v * N_HEADS + h)),
                  pl.BlockSpec((None, None, GRID_W, nband), variant), qspec],
        out_specs=[qspec, head_all, head_all, pl.BlockSpec((None, None, GRID_W, nband), variant)],
        out_shape=[_sds((t, w), F32), _sds((tt, w), F32), _sds((tt, w), BF16), _sds((N_HEADS, WIN_R, GRID_W, nband), F32)],
        scratch=[pltpu.VMEM((tt, HEAD_DIM), F32)])(qn, kall, p, bias, do)


def _na_tables(t, n_ctx_rows):
    half, nf = HEAD_DIM // 2, HEAD_DIM // 4
    pos = np.arange(t)
    lane = np.arange(HEAD_DIM)
    inv = ROPE_THETA ** (-(np.arange(nf, dtype=np.float32)) / nf)
    which = np.where(lane < half, pos[:, None] // GRID_W, pos[:, None] % GRID_W).astype(np.float32)
    ang = which * inv[lane % nf][None, :]
    first = (lane % half) < nf
    cos = np.concatenate([np.ones((n_ctx_rows, HEAD_DIM), np.float32), np.cos(ang).astype(np.float32)])
    sin = np.concatenate([np.zeros((n_ctx_rows, HEAD_DIM), np.float32), np.where(first[None, :], -np.sin(ang), np.sin(ang)).astype(np.float32)])
    w = np.arange(GRID_W)
    dc = np.clip(w[None, :] - w[:, None], -(WIN_C - 1), WIN_C - 1) + WIN_C - 1
    onehot = np.zeros((32, GRID_W * GRID_W), np.float32)
    onehot[dc.reshape(-1), np.arange(GRID_W * GRID_W)] = 1.0
    cs = np.clip(w - WIN_C // 2, 0, GRID_W - WIN_C)
    col_in = (w[None, :] >= cs[:, None]) & (w[None, :] < cs[:, None] + WIN_C)
    onehot *= col_in.reshape(1, -1)
    neg = np.where(col_in, 0.0, NEG_BIG).astype(np.float32)
    return jnp.asarray(cos), jnp.asarray(sin), jnp.asarray(onehot), jnp.asarray(neg)


def _bias_slabs(rel_bias, onehot, neg):
    nr = 2 * WIN_R - 1
    rb = jnp.pad(rel_bias.reshape(N_HEADS * nr, 2 * WIN_C - 1), ((0, 0), (0, 1)))
    spread = _mm_f32(rb, onehot, "bias_spread").reshape(N_HEADS, nr, GRID_W, GRID_W)
    slabs = [spread[:, WIN_R - 1 - v:2 * WIN_R - 1 - v] for v in range(WIN_R)]
    b = jnp.stack(slabs, axis=1) + neg[None, None, None]
    return b.transpose(0, 1, 3, 2, 4).reshape(N_HEADS, WIN_R, GRID_W, WIN_R * GRID_W)


def _bias_grad(dbias, onehot):
    nr = 2 * WIN_R - 1
    d = dbias.reshape(N_HEADS, WIN_R, GRID_W, WIN_R, GRID_W).transpose(0, 1, 3, 2, 4)
    tot = jnp.zeros((N_HEADS, nr, GRID_W, GRID_W), F32)
    for v in range(WIN_R):
        tot = tot + jnp.pad(d[:, v], ((0, 0), (WIN_R - 1 - v, v), (0, 0), (0, 0)))
    g = _mm_f32(tot.reshape(N_HEADS * nr, GRID_W * GRID_W), onehot, "bias_grad", trans_b=True)
    return g[:, :2 * WIN_C - 1].reshape(1, N_HEADS, nr, 2 * WIN_C - 1)


def _shift_rows(u, up):
    n = u.shape[0]
    tio = lax.broadcasted_iota(jnp.int32, u.shape, 0)
    if up:
        return jnp.where(tio == n - 1, 0.0, pltpu.roll(u, n - 1, 0))
    return jnp.where(tio == 0, 0.0, pltpu.roll(u, 1, 0))


def _conv3(u, w_ref, b_ref):
    um, up = _shift_rows(u, False), _shift_rows(u, True)
    return um, up, um * w_ref[0:1, :] + u * w_ref[1:2, :] + up * w_ref[2:3, :] + b_ref[...]


def _ffn_act_fwd(u0, t3, cw, cb):
    t, n = u0.shape
    tc = _pick(n, (256, 128))

    def body(u_ref, t_ref, w_ref, b_ref, a_ref):
        _, _, uc = _conv3(u_ref[...], w_ref, b_ref)
        a_ref[...] = (uc * jax.nn.sigmoid(uc) * t_ref[...]).astype(a_ref.dtype)

    col = lambda rows_: pl.BlockSpec((rows_, tc), lambda j: (0, j))
    return _pcall(body, name="ffn_act_fwd", grid=(n // tc,), in_specs=[col(t), col(t), col(8), col(1)], out_specs=col(t),
                  out_shape=_sds((t, n), BF16))(u0, t3, cw, cb)


def _ffn_act_bwd(u0, t3, cw, cb, da):
    t, n = u0.shape
    tc = _pick(n, (256, 128))

    def body(u_ref, t_ref, w_ref, b_ref, da_ref, du_ref, dt_ref, dw_ref, db_ref):
        u = u_ref[...]
        um, up, uc = _conv3(u, w_ref, b_ref)
        sg = jax.nn.sigmoid(uc)
        dav = da_ref[...].astype(F32)
        dt_ref[...] = (dav * uc * sg).astype(dt_ref.dtype)
        duc = dav * t_ref[...] * sg * (1.0 + uc * (1.0 - sg))
        du = _shift_rows(duc, True) * w_ref[0:1, :] + duc * w_ref[1:2, :] + _shift_rows(duc, False) * w_ref[2:3, :]
        du_ref[...] = du.astype(du_ref.dtype)
        dw_ref[...] = jnp.zeros_like(dw_ref)
        dw_ref[0:1, :] = jnp.sum(duc * um, axis=0, keepdims=True)
        dw_ref[1:2, :] = jnp.sum(duc * u, axis=0, keepdims=True)
        dw_ref[2:3, :] = jnp.sum(duc * up, axis=0, keepdims=True)
        db_ref[...] = jnp.sum(duc, axis=0, keepdims=True)

    col = lambda rows_: pl.BlockSpec((rows_, tc), lambda j: (0, j))
    return _pcall(body, name="ffn_act_bwd", grid=(n // tc,), in_specs=[col(t), col(t), col(8), col(1), col(t)],
                  out_specs=[col(t), col(t), col(8), col(1)],
                  out_shape=[_sds((t, n), BF16), _sds((t, n), BF16), _sds((8, n), F32), _sds((1, n), F32)])(u0, t3, cw, cb, da)


def _adam_math(g, w, m, v):
    m2 = ADAM_B1 * m + (1.0 - ADAM_B1) * g
    v2 = ADAM_B2 * v + (1.0 - ADAM_B2) * (g * g)
    m_hat = m2 / (1.0 - ADAM_B1 ** ADAM_STEP)
    v_hat = v2 / (1.0 - ADAM_B2 ** ADAM_STEP)
    return -ADAM_LR * (m_hat / (jnp.sqrt(v_hat) + ADAM_EPS) + ADAM_WD * w), m2, v2


def _adam_big(parts, w, m, v, name):
    r, c = w.shape
    npart, _, cp = parts.shape
    tr = _pick(r, (64, 32, 16, 8))

    def body(p_ref, w_ref, m_ref, v_ref, g_ref, d_ref, m2_ref, v2_ref):
        g = p_ref[0, :, 0:c].astype(F32)
        for i in range(1, npart):
            g = g + p_ref[i, :, 0:c].astype(F32)
        d, m2, v2 = _adam_math(g, w_ref[...], m_ref[...], v_ref[...])
        g_ref[...] = g
        d_ref[...] = d
        m2_ref[...] = m2
        v2_ref[...] = v2

    row = pl.BlockSpec((tr, c), lambda i: (i, 0))
    return _pcall(body, name=name, grid=(r // tr,),
                  in_specs=[pl.BlockSpec((npart, tr, cp), lambda i: (0, i, 0)), row, row, row],
                  out_specs=[row] * 4, out_shape=[_sds((r, c), F32)] * 4)(parts, w, m, v)


def _adam_small(g, w, m, v):
    def body(g_ref, w_ref, m_ref, v_ref, d_ref, m2_ref, v2_ref):
        d_ref[...], m2_ref[...], v2_ref[...] = _adam_math(g_ref[...], w_ref[...], m_ref[...], v_ref[...])

    return _pcall(body, name="adam_small", out_shape=[_sds(g.shape, F32)] * 3)(g, w, m, v)


def _sum_parts(parts, name):
    def body(p_ref, o_ref):
        s = p_ref[0]
        for i in range(1, N_DEV):
            s = s + p_ref[i]
        o_ref[...] = s

    return _pcall(body, name=name, out_shape=_sds(parts.shape[1:], F32))(parts)


class _Pack:
    def __init__(self, shapes):
        self.shapes = shapes
        self.sizes = [int(np.prod(s)) for s in shapes]
        self.padded = [-(-n // (8 * LANE)) * 8 * LANE for n in self.sizes]
        self.offs = np.concatenate([[0], np.cumsum(self.padded)]).tolist()

    def pack(self, arrs):
        flat = [jnp.pad(a.reshape(-1).astype(F32), (0, p - n)) for a, n, p in zip(arrs, self.sizes, self.padded)]
        return jnp.concatenate(flat).reshape(-1, LANE)

    def unpack(self, slab):
        flat = slab.reshape(-1)
        return [flat[o:o + n].reshape(s) for o, n, s in zip(self.offs, self.sizes, self.shapes)]


def kernel(x, c, ctx, c_ctx, ada_w, ada_b, norm1_g, norm2_g, w_in, hgrn_lb_logits, hgrn_norm_g, na_q_norm_g, na_k_norm_g, na_rel_bias, w_branch_a, w_branch_b, w_out, ffn_w1, ffn_w3, ffn_conv_w, ffn_conv_b, ffn_w2, loss_target, m_c_ctx, m_ada_w, m_ada_b, m_norm1_g, m_norm2_g, m_w_in, m_hgrn_lb_logits, m_hgrn_norm_g, m_na_q_norm_g, m_na_k_norm_g, m_na_rel_bias, m_w_branch_a, m_w_branch_b, m_w_out, m_ffn_w1, m_ffn_w3, m_ffn_conv_w, m_ffn_conv_b, m_ffn_w2, v_c_ctx, v_ada_w, v_ada_b, v_norm1_g, v_norm2_g, v_w_in, v_hgrn_lb_logits, v_hgrn_norm_g, v_na_q_norm_g, v_na_k_norm_g, v_na_rel_bias, v_w_branch_a, v_w_branch_b, v_w_out, v_ffn_w1, v_ffn_w3, v_ffn_conv_w, v_ffn_conv_b, v_ffn_w2):
    t, d = x.shape[1], x.shape[2]
    n_ctx = ctx.shape[1]
    tt = n_ctx + t
    hw = N_HEADS * HEAD_DIM
    ci = w_in.shape[2]
    ca = ada_w.shape[2]
    ff_l = ffn_w1.shape[2]
    ff_p = -(-ff_l // LANE) * LANE
    rows = t // GRID_W
    assert rows >= WIN_R and t % GRID_W == 0 and n_ctx % GRID_W == 0 and ci % LANE == 0 and d % LANE == 0
    me = 4 * lax.axis_index("x") + 2 * lax.axis_index("y") + lax.axis_index("c")
    tr = _pick(n_ctx, (256, 128, 64))
    n_ctx_tiles = n_ctx // tr

    pad_c = lambda w: jnp.pad(w, ((0, 0), (0, ff_p - ff_l)))
    small_in = [c, hgrn_lb_logits.reshape(4, HEAD_DIM), jnp.pad(ffn_conv_w[0], ((0, 5), (0, ff_p - ff_l)))]
    c_all, lb_parts, cw_all = _exchange(small_in, "gather_params", scatter=False)
    ff = N_DEV * ff_p

    cc = jnp.concatenate([c_all.reshape(N_DEV, d), jnp.broadcast_to(c_ctx[None, :], (N_DEV, d))], axis=0)
    act = _tiled(lambda v: v * jax.nn.sigmoid(v), [_const(cc)], [(cc.shape, BF16, cc.shape, lambda *_: (0, 0))], (), "silu_c")[0]
    ada16 = _cast_bf16(ada_w[0], "cast_ada")
    mod_cols = _mm_xw(act, ada16.reshape(1, d, ca), F32, "ada_fwd")
    mod_all = _exchange([mod_cols], "gather_mod", scatter=False)[0]

    with _after(mod_all):
        w_in16 = _cast_bf16(w_in[0], "cast_w_in")
    g_in, = _gather_sc([w_in16], "gather_w_in", 1)
    rest16 = []
    for w_, nm in ((w_branch_a[0], "cast_w_a"), (w_branch_b[0], "cast_w_b"), (w_out[0], "cast_w_out"), (pad_c(ffn_w1[0]), "cast_w1"),
                   (pad_c(ffn_w3[0]), "cast_w3"), (jnp.pad(ffn_w2[0], ((0, ff_p - ff_l), (0, 0))), "cast_w2")):
        with _after(w_in16):
            rest16.append(_cast_bf16(w_, nm))
    g_a, g_b, g_out, g_w1, g_w3, g_w2 = _gather_sc(rest16, "gather_rest", 2)
    g_out = g_out.reshape(1, d, d)
    g_w2 = g_w2.reshape(1, ff, d)
    mod_all = mod_all.transpose(1, 0, 2).reshape(2 * N_DEV, N_MOD * d) + ada_b
    mod_l = lax.dynamic_slice_in_dim(mod_all, me, 1, axis=0).reshape(N_MOD, 1, d)
    mod_c = mod_all[N_DEV:N_DEV + 1].reshape(N_MOD, 1, d)
    mods1 = jnp.stack([mod_c[0:2], mod_l[0:2]])
    mods2 = jnp.stack([mod_l[3:5], mod_l[3:5]])
    gate1, gate2 = mod_l[2], mod_l[5]

    xcat = jnp.concatenate([ctx[0], x[0]], axis=0)
    hcat = _norm_mod_fwd(xcat, norm1_g, mods1, n_ctx_tiles, tr)
    p = _mm_xw(hcat, g_in, F32, "in_proj")
    lb_logits = lb_parts.transpose(1, 0, 2).reshape(2, 2, hw)
    lb_soft = _tiled(lambda a, b: (1.0 / (1.0 + jnp.exp(b - a)),), [_const(lb_logits[:, 0]), _const(lb_logits[:, 1])],
                     [((2, hw), F32, (2, hw), lambda *_: (0, 0))], (), "lb_softmax")[0]
    lb_f, lb_b = lb_soft[0:1], lb_soft[1:2]
    o_f = _hgrn_fwd(p, lb_f, 1, False, n_ctx)
    o_b = _hgrn_fwd(p, lb_b, 2, True, n_ctx)

    cos, sin, onehot, neg = _na_tables(t, n_ctx)
    bias = _bias_slabs(na_rel_bias[0], onehot, neg)
    hcol = lambda seg, off=0: (lambda i, h: (i + off, seg * N_HEADS + h))
    tq = tr
    lat0 = n_ctx // tq
    qk_fn = lambda tv, g, cs, sn: (_f_qk(tv, g, cs, sn),)
    tab = lambda a, off=0: (a, (tq, HEAD_DIM), lambda i, h: (i + off, 0))
    qn = _tiled(qk_fn, [(p, (tq, HEAD_DIM), hcol(5, lat0)), _const(na_q_norm_g), tab(cos, lat0), tab(sin, lat0)],
                [((t, hw), BF16, (tq, HEAD_DIM), lambda i, h: (i, h))], (t // tq, N_HEADS), "q_norm_rope")[0]
    kall = _tiled(qk_fn, [(p, (tq, HEAD_DIM), hcol(6)), _const(na_k_norm_g), tab(cos), tab(sin)],
                  [((tt, hw), BF16, (tq, HEAD_DIM), lambda i, h: (i, h))], (tt // tq, N_HEADS), "k_norm_rope")[0]
    y_b = _na_fwd(qn, kall, p, 7, bias, n_ctx)

    trh = tr
    lat_h = n_ctx // trh
    ospec = lambda a: (a, (trh, HEAD_DIM), lambda i, h: (i + lat_h, h))
    y_a = _tiled(lambda a, b, gt, g: (_f_readout(a, b, gt, g),),
                 [ospec(o_f), ospec(o_b), (p, (trh, HEAD_DIM), hcol(4, lat_h)), _const(hgrn_norm_g)],
                 [((t, hw), BF16, (trh, HEAD_DIM), lambda i, h: (i, h))], (t // trh, N_HEADS), "hgrn_readout")[0]

    p_a = _mm_xw(y_a, g_a, F32, "branch_a")
    p_b = _mm_xw(y_b, g_b, F32, "branch_b")
    td = _pick(d, (512, 256, 128))
    nd_t = d // td
    lat_r = n_ctx // tr
    gcol = lambda k: (p, (tr, td), lambda i, j, k=k: (i + lat_r, 8 * hw // td + k * nd_t + j))
    dtile = lambda a: (a, (tr, td), lambda i, j: (i, j))
    z = _tiled(lambda ga, gb, pa, pb: (_f_merge(ga, gb, pa, pb),), [gcol(0), gcol(1), dtile(p_a), dtile(p_b)],
               [((t, d), BF16, (tr, td), lambda i, j: (i, j))], (t // tr, nd_t), "merge")[0]
    mix = _mm_xw(z, g_out, F32, "out_proj")
    xl = x[0]
    x_mid = _tiled(lambda a, g, m_: (_f_resid(a, g, m_),), [_rows(xl, tr), _const(gate1), _rows(mix, tr)],
                   [((t, d), F32, (tr, d), lambda i: (i, 0))], (t // tr,), "resid1")[0]

    h2 = _norm_mod_fwd(x_mid, norm2_g, mods2, 0, tr)
    u0 = _mm_xw(h2, g_w1, F32, "ffn_up1")
    t3 = _mm_xw(h2, g_w3, F32, "ffn_up3")
    cw_full = cw_all.transpose(1, 0, 2).reshape(8, ff)
    cb_full = jnp.pad(ffn_conv_b.reshape(N_DEV, ff_l), ((0, 0), (0, ff_p - ff_l))).reshape(1, ff)
    a_act = _ffn_act_fwd(u0, t3, cw_full, cb_full)
    f_out = _mm_xw(a_act, g_w2, F32, "ffn_down")

    def loss_fn(xm, g, f, tg):
        err = xm + g * f - tg
        return err * (1.0 / d), jnp.sum(err * err, axis=0, keepdims=True) * (0.5 / d), jnp.sum(err * (1.0 / d) * f, axis=0, keepdims=True)

    dy, loss_cols, d_gate2 = _tiled(loss_fn, [_rows(x_mid, tr), _const(gate2), _rows(f_out, tr), _rows(loss_target[0], tr)],
                                    [((t, d), F32, (tr, d), lambda i: (i, 0)), ((1, d), F32, (1, d), lambda i: (0, 0)),
                                     ((1, d), F32, (1, d), lambda i: (0, 0))], (t // tr,), "loss", acc=(1, 2))

    df = _tiled(lambda a, g: (a * g,), [_rows(dy, tr), _const(gate2)], [((t, d), BF16, (tr, d), lambda i: (i, 0))], (t // tr,), "d_ffn_out")[0]
    d_w2 = _mm_xtdy(a_act, df, 1, BF16, "d_w2")
    da = _mm_dyw(df, g_w2, BF16, "d_act")
    du0, dt3, d_cw, d_cb = _ffn_act_bwd(u0, t3, cw_full, cb_full, da)
    d_w1 = _mm_xtdy(h2, du0, N_DEV, BF16, "d_w1")
    d_w3 = _mm_xtdy(h2, dt3, N_DEV, BF16, "d_w3")
    ffn_blocks = [d_w1, d_w3, d_w2.reshape(N_DEV, ff_p, d)]
    ffn_theirs = _pair_swap_sc(ffn_blocks, "scatter_ffn_pair", 3)
    dh2 = _mm_dyw(du0, g_w1, F32, "d_h2_a")
    dh2 = _mm_dyw(dt3, g_w3, F32, "d_h2_b", init=dh2)
    with _after(dh2):
        ffn_sums = [_pair_add(ffn_blocks[0], ffn_theirs[0], "pair_add_w1")]
    ffn_sums += [_pair_add(ffn_blocks[1], ffn_theirs[1], "pair_add_w3"), _pair_add(ffn_blocks[2], ffn_theirs[2], "pair_add_w2")]
    r_w1, r_w3, r_w2 = _chip_scatter_sc(ffn_sums, "scatter_ffn_chips", 4)
    dx_mid, d_norm2, d_mods2 = _norm_mod_bwd(x_mid, norm2_g, mods2, dh2, dy, 0, tr, "norm_mod2_bwd")

    dm, d_gate1 = _tiled(lambda dxm, g, m_: (dxm * g, jnp.sum(dxm * m_, axis=0, keepdims=True)),
                         [_rows(dx_mid, tr), _const(gate1), _rows(mix, tr)],
                         [((t, d), BF16, (tr, d), lambda i: (i, 0)), ((1, d), F32, (1, d), lambda i: (0, 0))], (t // tr,), "d_resid1", acc=(1,))
    d_wout = _mm_xtdy(z, dm, 1, BF16, "d_w_out")
    dz = _mm_dyw(dm, g_out, F32, "d_merge")

    def merge_bwd(ga, gb, pa, pb, dzv):
        _, vjp = jax.vjp(_f_merge, ga, gb, pa, pb)
        return vjp(dzv)

    dga, dgb, dpa, dpb = _tiled(merge_bwd, [gcol(0), gcol(1), dtile(p_a), dtile(p_b), dtile(dz)],
                                [((t, d), BF16, (tr, td), lambda i, j: (i, j))] * 4, (t // tr, nd_t), "merge_bwd")
    d_wa = _mm_xtdy(y_a, dpa, N_DEV, BF16, "d_w_a")
    d_wb = _mm_xtdy(y_b, dpb, N_DEV, BF16, "d_w_b")
    mix_blocks = [d_wa, d_wb, d_wout.reshape(N_DEV, d // N_DEV, d)]
    mix_theirs = _pair_swap_sc(mix_blocks, "scatter_mix_pair", 5)
    dy_a = _mm_dyw(dpa, g_a, F32, "d_y_a")
    dy_b = _mm_dyw(dpb, g_b, BF16, "d_y_b")
    with _after(dy_b):
        mix_sums = [_pair_add(mix_blocks[0], mix_theirs[0], "pair_add_wa")]
    mix_sums += [_pair_add(mix_blocks[1], mix_theirs[1], "pair_add_wb"), _pair_add(mix_blocks[2], mix_theirs[2], "pair_add_wout")]
    r_a, r_b, r_out = _chip_scatter_sc(mix_sums, "scatter_mix_chips", 6)

    def readout_bwd(a, b, gt, g, ct):
        _, vjp = jax.vjp(_f_readout, a, b, gt, g)
        da_, _, dgt, dg = vjp(ct)
        return da_, dgt, dg

    hsp = lambda dt: ((t, hw), dt, (trh, HEAD_DIM), lambda i, h: (i, h))
    do_h, d_gate_o, d_hnorm = _tiled(
        readout_bwd, [ospec(o_f), ospec(o_b), (p, (trh, HEAD_DIM), hcol(4, lat_h)), _const(hgrn_norm_g), (dy_a, (trh, HEAD_DIM), lambda i, h: (i, h))],
        [hsp(F32), hsp(BF16), ((1, HEAD_DIM), F32, (1, HEAD_DIM), lambda i, h: (0, 0))], (t // trh, N_HEADS), "readout_bwd", acc=(2,))
    dq1, dfl_f, dv1, dlb_f = _hgrn_bwd(p, lb_f, do_h, 1, False, n_ctx, None)
    dq_h, dfl_b, dv_h, dlb_b = _hgrn_bwd(p, lb_b, do_h, 2, True, n_ctx, (dq1, dv1))

    dqn, dkall, dv_na, dbias = _na_bwd(qn, kall, p, 7, bias, dy_b, n_ctx)

    def qk_bwd(tv, g, cs, sn, ct):
        _, vjp = jax.vjp(lambda a, b: _f_qk(a, b, cs, sn), tv, g)
        return vjp(ct)

    d_pq, d_qnorm = _tiled(qk_bwd, [(p, (tq, HEAD_DIM), hcol(5, lat0)), _const(na_q_norm_g), tab(cos, lat0), tab(sin, lat0),
                                    (dqn, (tq, HEAD_DIM), lambda i, h: (i, h))],
                           [((t, hw), BF16, (tq, HEAD_DIM), lambda i, h: (i, h)), ((1, HEAD_DIM), F32, (1, HEAD_DIM), lambda i, h: (0, 0))],
                           (t // tq, N_HEADS), "q_norm_rope_bwd", acc=(1,))
    d_pk, d_knorm = _tiled(qk_bwd, [(p, (tq, HEAD_DIM), hcol(6)), _const(na_k_norm_g), tab(cos), tab(sin),
                                    (dkall, (tq, HEAD_DIM), lambda i, h: (i, h))],
                           [((tt, hw), BF16, (tq, HEAD_DIM), lambda i, h: (i, h)), ((1, HEAD_DIM), F32, (1, HEAD_DIM), lambda i, h: (0, 0))],
                           (tt // tq, N_HEADS), "k_norm_rope_bwd", acc=(1,))

    zc = lambda w_: jnp.zeros((n_ctx, w_), BF16)
    lat_only = lambda a: jnp.concatenate([zc(a.shape[1]), a], axis=0)
    dp = jnp.concatenate([dq_h, dfl_f, dfl_b, dv_h, lat_only(d_gate_o), lat_only(d_pq), d_pk, dv_na, lat_only(dga), lat_only(dgb)], axis=1)
    d_win = _mm_xtdy(hcat, dp, N_DEV, BF16, "d_w_in")
    win_theirs, = _pair_swap_sc([d_win], "scatter_w_in_pair", 7)
    with _after(d_win):
        dhcat = _mm_dyw(dp, g_in, BF16, "d_hcat")
    zero_ctx = jnp.concatenate([jnp.zeros((n_ctx, d), F32), dx_mid], axis=0)
    dxcat, d_norm1, d_mods1 = _norm_mod_bwd(xcat, norm1_g, mods1, dhcat, zero_ctx, n_ctx_tiles, tr, "norm_mod1_bwd")
    grad_x = dxcat[n_ctx:][None]

    zd = jnp.zeros((1, d), F32)
    dmod_l = jnp.concatenate([d_mods1[1, 0], d_mods1[1, 1], d_gate1, d_mods2[1, 0], d_mods2[1, 1], d_gate2], axis=1)
    dmod_c = jnp.concatenate([d_mods1[0, 0], d_mods1[0, 1], zd, zd, zd, zd], axis=1)
    dmods = jnp.concatenate([dmod_l, dmod_c], axis=0).reshape(2, N_DEV, ca).transpose(1, 0, 2)
    dmods = jnp.pad(dmods, ((0, 0), (0, 6), (0, 0)))
    got = _exchange([dmods], "scatter_dmod", scatter=True)[0]
    dm_rows = jnp.concatenate([got[:, 0], got[:, 1]], axis=0)
    d_ada = _mm_xtdy(act, dm_rows, 1, F32, "d_ada_w")[0]
    back = _mm_dyw(dm_rows, ada16.reshape(1, d, ca), F32, "d_silu_c")
    d_cctx_part = _tiled(lambda b, v: (jnp.sum(b[N_DEV:], axis=0, keepdims=True) * (jax.nn.sigmoid(v) * (1.0 + v * (1.0 - jax.nn.sigmoid(v)))),),
                         [_const(back), _const(c_ctx.reshape(1, d))], [((1, d), F32, (1, d), lambda *_: (0, 0))], (), "d_c_ctx")[0]

    d_rel = _bias_grad(dbias, onehot)
    d_lb_soft = jnp.concatenate([dlb_f, dlb_b], axis=0)
    d_lb0 = _tiled(lambda s, g: (g * s * (1.0 - s),), [_const(lb_soft), _const(d_lb_soft)], [((2, hw), F32, (2, hw), lambda *_: (0, 0))], (), "d_lb")[0]
    d_lb_full = jnp.stack([d_lb0, -d_lb0], axis=1)
    d_cw_l = d_cw[:3].reshape(3, N_DEV, ff_p)[:, :, :ff_l].reshape(1, 3, N_DEV * ff_l)
    d_cb_l = d_cb.reshape(N_DEV, ff_p)[:, :ff_l].reshape(1, N_DEV * ff_l)
    small = [d_cctx_part.reshape(d), (dmod_l + dmod_c), d_norm1, d_norm2, d_lb_full, d_hnorm, d_qnorm, d_knorm, d_rel, d_cw_l, d_cb_l, loss_cols]
    pk = _Pack([a.shape for a in small])
    tot = _sum_parts(_exchange([pk.pack(small)], "gather_small", scatter=False)[0], "sum_small")
    g_cctx, g_ada_b, g_n1, g_n2, g_lb, g_hn, g_qn, g_kn, g_rel, g_cw, g_cb, loss_all = pk.unpack(tot)
    loss = _tiled(lambda v: (jnp.sum(v, axis=1, keepdims=True),), [_const(loss_all)], [((1, 1), F32, (1, 1), lambda *_: (0, 0))], (), "loss_total")[0][0, 0]
    g_lb = lax.dynamic_slice_in_dim(g_lb, me * HEAD_DIM, HEAD_DIM, axis=2)
    g_cw = lax.dynamic_slice_in_dim(g_cw, me * ff_l, ff_l, axis=2)
    small_names = [("c_ctx", g_cctx, c_ctx, m_c_ctx, v_c_ctx), ("ada_b", g_ada_b, ada_b, m_ada_b, v_ada_b),
                   ("norm1_g", g_n1, norm1_g, m_norm1_g, v_norm1_g), ("norm2_g", g_n2, norm2_g, m_norm2_g, v_norm2_g),
                   ("hgrn_lb_logits", g_lb, hgrn_lb_logits, m_hgrn_lb_logits, v_hgrn_lb_logits),
                   ("hgrn_norm_g", g_hn, hgrn_norm_g, m_hgrn_norm_g, v_hgrn_norm_g), ("na_q_norm_g", g_qn, na_q_norm_g, m_na_q_norm_g, v_na_q_norm_g),
                   ("na_k_norm_g", g_kn, na_k_norm_g, m_na_k_norm_g, v_na_k_norm_g), ("na_rel_bias", g_rel, na_rel_bias, m_na_rel_bias, v_na_rel_bias),
                   ("ffn_conv_w", g_cw, ffn_conv_w, m_ffn_conv_w, v_ffn_conv_w), ("ffn_conv_b", g_cb, ffn_conv_b, m_ffn_conv_b, v_ffn_conv_b)]
    pk2 = _Pack([s[1].shape for s in small_names])
    sd, sm, sv = _adam_small(*[pk2.pack([s[i] for s in small_names]) for i in (1, 2, 3, 4)])
    sd, sm, sv = pk2.unpack(sd), pk2.unpack(sm), pk2.unpack(sv)
    res = {s[0]: (s[1], sd[i], sm[i], sv[i]) for i, s in enumerate(small_names)}

    with _after(sd[0]):
        win_sums = _pair_add(d_win, win_theirs, "pair_add_w_in")
    r_in, = _chip_scatter_sc([win_sums], "scatter_w_in_chips", 8)
    with _after(win_sums):
        res["ada_w"] = _adam_big(d_ada[None], ada_w[0], m_ada_w[0], v_ada_w[0], "adam_ada")
    res["ffn_w1"] = _adam_big(r_w1, ffn_w1[0], m_ffn_w1[0], v_ffn_w1[0], "adam_w1")
    res["ffn_w3"] = _adam_big(r_w3, ffn_w3[0], m_ffn_w3[0], v_ffn_w3[0], "adam_w3")
    res["ffn_w2"] = _adam_big(r_w2, ffn_w2[0], m_ffn_w2[0], v_ffn_w2[0], "adam_w2")
    res["w_branch_a"] = _adam_big(r_a, w_branch_a[0], m_w_branch_a[0], v_w_branch_a[0], "adam_w_a")
    res["w_branch_b"] = _adam_big(r_b, w_branch_b[0], m_w_branch_b[0], v_w_branch_b[0], "adam_w_b")
    res["w_out"] = _adam_big(r_out, w_out[0], m_w_out[0], v_w_out[0], "adam_w_out")
    res["w_in"] = _adam_big(r_in, w_in[0], m_w_in[0], v_w_in[0], "adam_w_in")
    for k in ("w_in", "w_branch_a", "w_branch_b", "w_out", "ffn_w1", "ffn_w3", "ffn_w2", "ada_w"):
        res[k] = tuple(a[None] for a in res[k])

    order = ["c_ctx", "ada_w", "ada_b", "norm1_g", "norm2_g", "w_in", "hgrn_lb_logits", "hgrn_norm_g", "na_q_norm_g", "na_k_norm_g",
             "na_rel_bias", "w_branch_a", "w_branch_b", "w_out", "ffn_w1", "ffn_w3", "ffn_conv_w", "ffn_conv_b", "ffn_w2"]
    shapes = {"c_ctx": c_ctx.shape, "ada_b": ada_b.shape, "norm1_g": norm1_g.shape, "norm2_g": norm2_g.shape,
              "hgrn_lb_logits": hgrn_lb_logits.shape, "hgrn_norm_g": hgrn_norm_g.shape, "na_q_norm_g": na_q_norm_g.shape,
              "na_k_norm_g": na_k_norm_g.shape, "na_rel_bias": na_rel_bias.shape, "ffn_conv_w": ffn_conv_w.shape, "ffn_conv_b": ffn_conv_b.shape}
    outs = [loss, grad_x]
    for part in range(4):
        for k in order:
            a = res[k][part]
            outs.append(a.reshape(shapes[k]) if k in shapes else a)
    return tuple(outs)
```

```python
import functools

import numpy as np
import jax
import jax.numpy as jnp
from jax import lax
from jax.experimental import pallas as pl
from jax.experimental.pallas import tpu as pltpu
from jax.experimental.pallas import tpu_sc as plsc

F32 = jnp.float32
BF16 = jnp.bfloat16
HIGHEST = lax.Precision.HIGHEST

N_DEV = 8
MESH_ID = pl.DeviceIdType.MESH
LANE = 128
HEAD_DIM = 128
N_HEADS = 8
GRID_W = 64
WIN_R = 8
WIN_C = 16
ROPE_THETA = 10000.0
EPS = 1e-6
N_MOD = 6
HGRN_BLOCK = 16
NEG_BIG = -1e30
VMEM_LIMIT = 56 << 20

ADAM_LR = 0.001
ADAM_B1 = 0.9
ADAM_B2 = 0.999
ADAM_EPS = 1e-08
ADAM_WD = 0.01
ADAM_STEP = 10

HBM_SPEC = pl.BlockSpec(memory_space=pltpu.HBM)


_ORDER_AFTER = []


class _after:
    def __init__(self, *arrs):
        self.arrs = list(arrs)

    def __enter__(self):
        _ORDER_AFTER.extend(self.arrs)

    def __exit__(self, *exc):
        del _ORDER_AFTER[:]


def _pcall(body, *, name, out_shape, grid=None, in_specs=None, out_specs=None, scratch=(), aliases=None):
    kw = {}
    if grid is not None:
        kw["grid"] = grid
    extra = []
    if _ORDER_AFTER and in_specs is not None:
        extra = list(_ORDER_AFTER)
        del _ORDER_AFTER[:]
        n_in, n_extra, inner = len(in_specs), len(extra), body
        in_specs = list(in_specs) + [pl.BlockSpec(memory_space=pl.ANY)] * n_extra

        def body(*refs):
            return inner(*refs[:n_in], *refs[n_in + n_extra:])

    if extra:
        call = _pcall_inner(body, name, out_shape, kw, in_specs, out_specs, scratch, aliases)
        return lambda *args: call(*args, *extra)
    return _pcall_inner(body, name, out_shape, kw, in_specs, out_specs, scratch, aliases)


def _pcall_inner(body, name, out_shape, kw, in_specs, out_specs, scratch, aliases):
    if in_specs is not None:
        kw["in_specs"] = in_specs
    if out_specs is not None:
        kw["out_specs"] = out_specs
    if scratch:
        kw["scratch_shapes"] = list(scratch)
    if aliases:
        kw["input_output_aliases"] = aliases
    return pl.pallas_call(body, name=name, out_shape=out_shape,
                          compiler_params=pltpu.CompilerParams(vmem_limit_bytes=VMEM_LIMIT), **kw)


def _pick(dim, cands):
    for c in cands:
        if c <= dim and dim % c == 0:
            return c
    return dim


def _sds(shape, dtype):
    return jax.ShapeDtypeStruct(tuple(shape), dtype)


def _peers():
    x, y, c = lax.axis_index("x"), lax.axis_index("y"), lax.axis_index("c")
    out = []
    for k in range(1, N_DEV):
        px = 1 - x if (k >> 2) & 1 else x
        py = 1 - y if (k >> 1) & 1 else y
        pc = 1 - c if k & 1 else c
        out.append((k, (px, py, pc), 4 * px + 2 * py + pc))
    return 4 * x + 2 * y + c, out


def _exchange(arrs, name, scatter):
    n = len(arrs)

    def body(*refs):
        ins, outs = refs[:n], refs[n:2 * n]
        send, recv, loc = refs[2 * n:]
        me, peers = _peers()
        started = []
        for i in range(n):
            src = ins[i].at[me] if scatter else ins[i]
            cp = pltpu.make_async_copy(src, outs[i].at[me], loc.at[i])
            cp.start()
            started.append(cp)
        sends = []
        for k, peer, pidx in peers:
            for i in range(n):
                src = ins[i].at[pidx] if scatter else ins[i]
                cp = pltpu.make_async_remote_copy(src_ref=src, dst_ref=outs[i].at[me], send_sem=send.at[i * 7 + k - 1],
                                                  recv_sem=recv.at[i * 7 + k - 1], device_id=peer, device_id_type=MESH_ID)
                cp.start()
                sends.append(cp)
        for k, peer, pidx in peers:
            for i in range(n):
                src = ins[i].at[pidx] if scatter else ins[i]
                pltpu.make_async_remote_copy(src_ref=src, dst_ref=outs[i].at[pidx], send_sem=send.at[i * 7 + k - 1],
                                             recv_sem=recv.at[i * 7 + k - 1], device_id=peer, device_id_type=MESH_ID).wait_recv()
        for cp in sends:
            cp.wait_send()
        for cp in started:
            cp.wait()

    out_shape = [_sds(a.shape if scatter else (N_DEV,) + a.shape, a.dtype) for a in arrs]
    res = _pcall(body, name=name, out_shape=out_shape, in_specs=[HBM_SPEC] * n, out_specs=[HBM_SPEC] * n,
                 scratch=[pltpu.SemaphoreType.DMA((7 * n,)), pltpu.SemaphoreType.DMA((7 * n,)), pltpu.SemaphoreType.DMA((n,))])(*arrs)
    return list(res)


def _exchange_sc(arrs, name, scatter, collective_id):
    n = len(arrs)
    srcs = [jax.new_ref(a, memory_space=pltpu.MemorySpace.HBM) for a in arrs]
    lands = [jax.empty_ref(_sds(a.shape if scatter else (N_DEV,) + a.shape, a.dtype), memory_space=pltpu.MemorySpace.HBM) for a in arrs]

    @pl.kernel(mesh=plsc.ScalarSubcoreMesh(axis_name="seq", num_cores=1), name=name,
               scratch_types=(pltpu.SemaphoreType.DMA((7 * n,)), pltpu.SemaphoreType.DMA((7 * n,)), pltpu.SemaphoreType.DMA((n,))),
               compiler_params=pltpu.CompilerParams(collective_id=collective_id))
    def launch(send, recv, loc):
        me, peers = _peers()
        barrier = pltpu.get_barrier_semaphore()
        for _, peer, _ in peers:
            pl.semaphore_signal(barrier, inc=1, device_id=peer, device_id_type=MESH_ID)
        pl.semaphore_wait(barrier, N_DEV - 1)
        own = [pltpu.make_async_copy(srcs[i].at[me] if scatter else srcs[i], lands[i].at[me], loc.at[i]) for i in range(n)]
        for cp in own:
            cp.start()
        sends = []
        for k, peer, pidx in peers:
            for i in range(n):
                src = srcs[i].at[pidx] if scatter else srcs[i]
                cp = pltpu.make_async_remote_copy(src_ref=src, dst_ref=lands[i].at[me], send_sem=send.at[i * 7 + k - 1],
                                                  recv_sem=recv.at[i * 7 + k - 1], device_id=peer, device_id_type=MESH_ID)
                cp.start()
                sends.append(cp)
        for k, peer, pidx in peers:
            for i in range(n):
                src = srcs[i].at[pidx] if scatter else srcs[i]
                pltpu.make_async_remote_copy(src_ref=src, dst_ref=lands[i].at[pidx], send_sem=send.at[i * 7 + k - 1],
                                             recv_sem=recv.at[i * 7 + k - 1], device_id=peer, device_id_type=MESH_ID).wait_recv()
        for cp in sends:
            cp.wait_send()
        for cp in own:
            cp.wait()

    launch()
    return [r[...] for r in lands]


def _gather_sc(arrs, name, collective_id):
    n = len(arrs)
    srcs = [jax.new_ref(a, memory_space=pltpu.MemorySpace.HBM) for a in arrs]
    lands = [jax.empty_ref(_sds((N_DEV,) + a.shape, a.dtype), memory_space=pltpu.MemorySpace.HBM) for a in arrs]

    @pl.kernel(mesh=plsc.ScalarSubcoreMesh(axis_name="seq", num_cores=1), name=name,
               scratch_types=(pltpu.SemaphoreType.DMA((7 * n,)), pltpu.SemaphoreType.DMA((7 * n,)), pltpu.SemaphoreType.DMA((n,))),
               compiler_params=pltpu.CompilerParams(collective_id=collective_id))
    def launch(send, recv, loc):
        x, y, c = lax.axis_index("x"), lax.axis_index("y"), lax.axis_index("c")
        me, sibling = (x, y, c), (x, y, 1 - c)
        chips = [(1 - x, y), (x, 1 - y), (1 - x, 1 - y)]
        index = lambda px, py, pc: 4 * px + 2 * py + pc
        barrier = pltpu.get_barrier_semaphore()
        for peer in [sibling] + [(*chip, c) for chip in chips]:
            pl.semaphore_signal(barrier, inc=1, device_id=peer, device_id_type=MESH_ID)
        pl.semaphore_wait(barrier, 4)

        def copy(i, k, block, to, from_src):
            return pltpu.make_async_remote_copy(src_ref=srcs[i] if from_src else lands[i].at[index(*block)], dst_ref=lands[i].at[index(*block)],
                                                send_sem=send.at[i * 7 + k], recv_sem=recv.at[i * 7 + k], device_id=to, device_id_type=MESH_ID)

        own = [pltpu.make_async_copy(srcs[i], lands[i].at[index(*me)], loc.at[i]) for i in range(n)]
        for cp in own:
            cp.start()
        started = []
        for i in range(n):
            started.append(copy(i, 0, me, sibling, True))
            started += [copy(i, 1 + j, me, (*chip, c), True) for j, chip in enumerate(chips)]
        for cp in started:
            cp.start()
        for j, chip in enumerate(chips):
            for i in range(n):
                copy(i, 1 + j, (*chip, c), me, False).wait_recv()
                fwd = copy(i, 4 + j, (*chip, c), sibling, False)
                fwd.start()
                started.append(fwd)
        for i in range(n):
            copy(i, 0, sibling, me, False).wait_recv()
            for j, chip in enumerate(chips):
                copy(i, 4 + j, (*chip, 1 - c), me, False).wait_recv()
        for cp in started:
            cp.wait_send()
        for cp in own:
            cp.wait()

    launch()
    return [r[...] for r in lands]


def _pair_swap_sc(blocks, name, collective_id):
    n = len(blocks)
    srcs = [jax.new_ref(b, memory_space=pltpu.MemorySpace.HBM) for b in blocks]
    lands = [jax.empty_ref(_sds((4,) + b.shape[1:], b.dtype), memory_space=pltpu.MemorySpace.HBM) for b in blocks]

    @pl.kernel(mesh=plsc.ScalarSubcoreMesh(axis_name="seq", num_cores=1), name=name,
               scratch_types=(pltpu.SemaphoreType.DMA((4 * n,)), pltpu.SemaphoreType.DMA((4 * n,))),
               compiler_params=pltpu.CompilerParams(collective_id=collective_id))
    def launch(send, recv):
        x, y, c = lax.axis_index("x"), lax.axis_index("y"), lax.axis_index("c")
        sibling = (x, y, 1 - c)
        barrier = pltpu.get_barrier_semaphore()
        pl.semaphore_signal(barrier, inc=1, device_id=sibling, device_id_type=MESH_ID)
        pl.semaphore_wait(barrier, 1)
        copies = [pltpu.make_async_remote_copy(src_ref=srcs[i].at[2 * q + 1 - c], dst_ref=lands[i].at[q], send_sem=send.at[4 * i + q],
                                               recv_sem=recv.at[4 * i + q], device_id=sibling, device_id_type=MESH_ID)
                  for i in range(n) for q in range(4)]
        for cp in copies:
            cp.start()
        for cp in copies:
            cp.wait()

    launch()
    return [r[...] for r in lands]


def _pair_add(blocks, theirs, name):
    _, r, c_ = blocks.shape
    mine = lax.dynamic_index_in_dim(blocks.reshape(4, 2, r, c_), lax.axis_index("c"), axis=1, keepdims=False)
    tr = _pick(r, (256, 128, 64, 16))
    spec = (None, tr, c_)
    return _tiled(lambda a, b: (a.astype(F32) + b.astype(F32),), [(mine, spec, lambda q, i: (q, i, 0)), (theirs, spec, lambda q, i: (q, i, 0))],
                  [((4, r, c_), BF16, spec, lambda q, i: (q, i, 0))], (4, r // tr), name)[0]


def _chip_scatter_sc(sums, name, collective_id):
    n = len(sums)
    srcs = [jax.new_ref(s, memory_space=pltpu.MemorySpace.HBM) for s in sums]
    lands = [jax.empty_ref(_sds(s.shape, s.dtype), memory_space=pltpu.MemorySpace.HBM) for s in sums]

    @pl.kernel(mesh=plsc.ScalarSubcoreMesh(axis_name="seq", num_cores=1), name=name,
               scratch_types=(pltpu.SemaphoreType.DMA((3 * n,)), pltpu.SemaphoreType.DMA((3 * n,)), pltpu.SemaphoreType.DMA((n,))),
               compiler_params=pltpu.CompilerParams(collective_id=collective_id))
    def launch(send, recv, loc):
        x, y, c = lax.axis_index("x"), lax.axis_index("y"), lax.axis_index("c")
        chips = [(1 - x, y), (x, 1 - y), (1 - x, 1 - y)]
        my_chip = 2 * x + y
        barrier = pltpu.get_barrier_semaphore()
        for chip in chips:
            pl.semaphore_signal(barrier, inc=1, device_id=(*chip, c), device_id_type=MESH_ID)
        pl.semaphore_wait(barrier, 3)
        own = [pltpu.make_async_copy(srcs[i].at[my_chip], lands[i].at[my_chip], loc.at[i]) for i in range(n)]
        for cp in own:
            cp.start()
        sends = [pltpu.make_async_remote_copy(src_ref=srcs[i].at[2 * px + py], dst_ref=lands[i].at[my_chip], send_sem=send.at[3 * i + j],
                                              recv_sem=recv.at[3 * i + j], device_id=(px, py, c), device_id_type=MESH_ID)
                 for j, (px, py) in enumerate(chips) for i in range(n)]
        for cp in sends:
            cp.start()
        for j, (px, py) in enumerate(chips):
            for i in range(n):
                pltpu.make_async_remote_copy(src_ref=srcs[i].at[my_chip], dst_ref=lands[i].at[2 * px + py], send_sem=send.at[3 * i + j],
                                             recv_sem=recv.at[3 * i + j], device_id=(px, py, c), device_id_type=MESH_ID).wait_recv()
        for cp in sends:
            cp.wait_send()
        for cp in own:
            cp.wait()

    launch()
    return [r[...] for r in lands]


class _InFlight:
    def __init__(self, send, recv, srcs, lands, token, scatter):
        self.send, self.recv, self.srcs, self.lands, self.token, self.scatter = send, recv, srcs, lands, token, scatter


SEM_SPEC = pl.BlockSpec(memory_space=pltpu.SEMAPHORE)
SIDE_EFFECT = pltpu.SideEffectType.DATAFLOW_SIDE_EFFECTING


def _exchange_start(arrs, name, scatter):
    n = len(arrs)

    def body(*refs):
        ins, lands = refs[:n], refs[n:2 * n]
        send, recv = refs[2 * n], refs[2 * n + 1]
        token = refs[4 * n + 2]
        loc = refs[4 * n + 3]
        me, peers = _peers()
        own = [pltpu.make_async_copy(ins[i].at[me] if scatter else ins[i], lands[i].at[me], loc.at[i]) for i in range(n)]
        for cp in own:
            cp.start()
        for cp in own:
            cp.wait()
        for k, peer, pidx in peers:
            for i in range(n):
                src = ins[i].at[pidx] if scatter else ins[i]
                pltpu.make_async_remote_copy(src_ref=src, dst_ref=lands[i].at[me], send_sem=send.at[i * 7 + k - 1],
                                             recv_sem=recv.at[i * 7 + k - 1], device_id=peer, device_id_type=MESH_ID).start()
        token[...] = jnp.zeros_like(token)

    land_shapes = [a.shape if scatter else (N_DEV,) + a.shape for a in arrs]
    hbm = lambda a: pltpu.with_memory_space_constraint(a, pltpu.HBM)
    args = [hbm(a) for a in arrs] + [hbm(lax.empty(s, a.dtype)) for s, a in zip(land_shapes, arrs)]
    out_shape = ([pltpu.SemaphoreType.DMA((7 * n,)), pltpu.SemaphoreType.DMA((7 * n,))]
                 + [pltpu.HBM(a.shape, a.dtype) for a in arrs] + [pltpu.HBM(s, a.dtype) for s, a in zip(land_shapes, arrs)]
                 + [_sds((8, LANE), F32)])
    res = pl.pallas_call(
        body, name=name, out_shape=out_shape, in_specs=[HBM_SPEC] * (2 * n),
        out_specs=[SEM_SPEC, SEM_SPEC] + [HBM_SPEC] * (2 * n) + [pl.BlockSpec(memory_space=pltpu.VMEM)],
        input_output_aliases={i: 2 + i for i in range(2 * n)},
        scratch_shapes=[pltpu.SemaphoreType.DMA((n,))],
        compiler_params=pltpu.CompilerParams(has_side_effects=SIDE_EFFECT))(*args)
    return _InFlight(res[0], res[1], list(res[2:2 + n]), list(res[2 + n:2 + 2 * n]), res[2 + 2 * n], scatter)


def _exchange_wait(h, after, name):
    n = len(h.srcs)
    scatter = h.scatter
    after = list(after)

    def body(*refs):
        ins, lands = refs[:n], refs[n:2 * n]
        send, recv = refs[2 * n], refs[2 * n + 1]
        _, peers = _peers()
        for k, peer, pidx in peers:
            for i in range(n):
                src = ins[i].at[pidx] if scatter else ins[i]
                cp = pltpu.make_async_remote_copy(src_ref=src, dst_ref=lands[i].at[pidx], send_sem=send.at[i * 7 + k - 1],
                                                  recv_sem=recv.at[i * 7 + k - 1], device_id=peer, device_id_type=MESH_ID)
                cp.wait_send()
                cp.wait_recv()

    res = pl.pallas_call(
        body, name=name, out_shape=[pltpu.HBM(a.shape, a.dtype) for a in h.srcs + h.lands],
        in_specs=[HBM_SPEC] * (2 * n) + [SEM_SPEC, SEM_SPEC] + [pl.BlockSpec(memory_space=pl.ANY)] * len(after),
        out_specs=[HBM_SPEC] * (2 * n), input_output_aliases={i: i for i in range(2 * n)},
        compiler_params=pltpu.CompilerParams(has_side_effects=SIDE_EFFECT))(*h.srcs, *h.lands, h.send, h.recv, *after)
    return list(res[n:])


def _mm_xw(x, g, out_dtype, name, tm_c=(768, 512, 384, 256, 128, 64, 16), tn_c=(1024, 768, 512, 256, 128), tk_c=(2048, 1024, 768, 512, 256)):
    m, r = x.shape
    nb, r2, cl = g.shape
    assert r == r2
    tm, tn, tk = _pick(m, tm_c), _pick(cl, tn_c), _pick(r, tk_c)
    q, nk = cl // tn, r // tk

    def body(x_ref, g_ref, o_ref, *acc):
        p = lax.dot_general(x_ref[...].astype(BF16), g_ref[...], (((1,), (0,)), ((), ())), preferred_element_type=F32)
        if nk == 1:
            o_ref[...] = p.astype(o_ref.dtype)
        else:
            k = pl.program_id(2)

            @pl.when(k == 0)
            def _():
                acc[0][...] = p

            @pl.when(k > 0)
            def _():
                acc[0][...] += p

            @pl.when(k == nk - 1)
            def _():
                o_ref[...] = acc[0][...].astype(o_ref.dtype)

    return _pcall(
        body, name=name, grid=(m // tm, nb * q, nk),
        in_specs=[pl.BlockSpec((tm, tk), lambda i, j, k: (i, k)), pl.BlockSpec((None, tk, tn), lambda i, j, k: (j // q, k, j % q))],
        out_specs=pl.BlockSpec((tm, tn), lambda i, j, k: (i, j)),
        out_shape=_sds((m, nb * cl), out_dtype),
        scratch=[] if nk == 1 else [pltpu.VMEM((tm, tn), F32)])(x, g)


def _mm_dyw(dy, g, out_dtype, name, init=None, tm_c=(768, 512, 384, 256, 128), tn_c=(1024, 512, 256, 128), tk_c=(2048, 1536, 1024, 768, 512, 256, 128)):
    m, n = dy.shape
    nb, r, cl = g.shape
    assert n == nb * cl
    tm, tn, tk = _pick(m, tm_c), _pick(r, tn_c), _pick(cl, tk_c)
    q = cl // tk
    nk = nb * q
    has_init = init is not None

    def body(*refs):
        if has_init:
            dy_ref, g_ref, i_ref, o_ref, acc = refs
        else:
            dy_ref, g_ref, o_ref, acc = refs
        k = pl.program_id(2)
        p = lax.dot_general(dy_ref[...].astype(BF16), g_ref[...], (((1,), (1,)), ((), ())), preferred_element_type=F32)

        @pl.when(k == 0)
        def _():
            acc[...] = p + i_ref[...].astype(F32) if has_init else p

        @pl.when(k > 0)
        def _():
            acc[...] += p

        @pl.when(k == nk - 1)
        def _():
            o_ref[...] = acc[...].astype(o_ref.dtype)

    in_specs = [pl.BlockSpec((tm, tk), lambda i, j, k: (i, k)), pl.BlockSpec((None, tn, tk), lambda i, j, k: (k // q, j, k % q))]
    args = [dy, g]
    if has_init:
        in_specs.append(pl.BlockSpec((tm, tn), lambda i, j, k: (i, j)))
        args.append(init)
    return _pcall(body, name=name, grid=(m // tm, r // tn, nk), in_specs=in_specs,
                  out_specs=pl.BlockSpec((tm, tn), lambda i, j, k: (i, j)), out_shape=_sds((m, r), out_dtype),
                  scratch=[pltpu.VMEM((tm, tn), F32)])(*args)


def _mm_xtdy(x, dy, nb, out_dtype, name, tm_c=(1024, 512, 256, 128), tn_c=(768, 512, 256, 128), tk_c=(2304, 2048, 1152, 1024, 768, 512, 256, 128, 16)):
    t, r = x.shape
    t2, n = dy.shape
    assert t == t2 and n % nb == 0
    cl = n // nb
    tm, tn, tk = _pick(r, tm_c), _pick(cl, tn_c), _pick(t, tk_c)
    q, nk = cl // tn, t // tk

    def body(x_ref, dy_ref, o_ref, *acc):
        p = lax.dot_general(x_ref[...].astype(BF16), dy_ref[...].astype(BF16), (((0,), (0,)), ((), ())), preferred_element_type=F32)
        if nk == 1:
            o_ref[...] = p.astype(o_ref.dtype)
        else:
            k = pl.program_id(2)

            @pl.when(k == 0)
            def _():
                acc[0][...] = p

            @pl.when(k > 0)
            def _():
                acc[0][...] += p

            @pl.when(k == nk - 1)
            def _():
                o_ref[...] = acc[0][...].astype(o_ref.dtype)

    return _pcall(
        body, name=name, grid=(r // tm, nb * q, nk),
        in_specs=[pl.BlockSpec((tk, tm), lambda i, j, k: (k, i)), pl.BlockSpec((tk, tn), lambda i, j, k: (k, j))],
        out_specs=pl.BlockSpec((None, tm, tn), lambda i, j, k: (j // q, i, j % q)),
        out_shape=_sds((nb, r, cl), out_dtype),
        scratch=[] if nk == 1 else [pltpu.VMEM((tm, tn), F32)])(x, dy)


def _mm_f32(a, b, name, trans_b=False):
    dims = (((1,), (1,)), ((), ())) if trans_b else (((1,), (0,)), ((), ()))
    n = b.shape[0] if trans_b else b.shape[1]

    def body(a_ref, b_ref, o_ref):
        o_ref[...] = lax.dot_general(a_ref[...], b_ref[...], dims, precision=HIGHEST, preferred_element_type=F32)

    return _pcall(body, name=name, out_shape=_sds((a.shape[0], n), F32))(a, b)


def _tiled(fn, ins, outs, grid, name, acc=()):
    n_in = len(ins)
    grid = tuple(grid) or (1,)
    nd = len(grid)

    def body(*refs):
        vals = fn(*[r[...] for r in refs[:n_in]])
        if not isinstance(vals, (tuple, list)):
            vals = (vals,)
        first = None
        for o, (ref, v) in enumerate(zip(refs[n_in:], vals)):
            if o in acc:
                if first is None:
                    first = pl.program_id(0) == 0
                    for a in range(1, nd):
                        first = jnp.logical_and(first, pl.program_id(a) == 0)

                @pl.when(first)
                def _(ref=ref, v=v):
                    ref[...] = v.astype(ref.dtype)

                @pl.when(jnp.logical_not(first))
                def _(ref=ref, v=v):
                    ref[...] += v.astype(ref.dtype)
            else:
                ref[...] = v.astype(ref.dtype)

    res = _pcall(body, name=name, grid=grid,
                 in_specs=[pl.BlockSpec(b, im) for _, b, im in ins],
                 out_specs=[pl.BlockSpec(b, im) for _, _, b, im in outs],
                 out_shape=[_sds(s, d) for s, d, _, _ in outs])(*[a for a, _, _ in ins])
    return list(res)


def _rows(a, tr):
    return (a, (tr, a.shape[1]), lambda i, *_: (i, 0))


def _const(a):
    nd = a.ndim
    return (a, a.shape, lambda *_: (0,) * nd)


def _cast_bf16(w, name):
    r, c = w.shape
    tr = _pick(r, (256, 128, 64, 16))
    return _tiled(lambda v: v, [_rows(w, tr)], [((r, c), BF16, (tr, c), lambda i: (i, 0))], (r // tr,), name)[0]


def _f_norm_mod(x, g, sh, sc):
    y = x * lax.rsqrt(jnp.mean(x * x, axis=-1, keepdims=True) + EPS) * g
    return y * (1.0 + sc) + sh


def _rope_partner_impl(y):
    nf = HEAD_DIM // 4
    lane = lax.broadcasted_iota(jnp.int32, y.shape, 1)
    return jnp.where(lane % (2 * nf) < nf, pltpu.roll(y, HEAD_DIM - nf, 1), pltpu.roll(y, nf, 1))


_rope_partner = jax.custom_vjp(_rope_partner_impl)
_rope_partner.defvjp(lambda y: (_rope_partner_impl(y), None), lambda _, ct: (_rope_partner_impl(ct),))


def _f_qk(t, g, cos, sin):
    y = t * lax.rsqrt(jnp.mean(t * t, axis=-1, keepdims=True) + EPS) * g
    return y * cos + _rope_partner(y) * sin


def _f_readout(of, ob, gate, g):
    o = of + ob
    on = o * lax.rsqrt(jnp.mean(o * o, axis=-1, keepdims=True) + EPS) * g
    return on * (gate * jax.nn.sigmoid(gate))


def _f_merge(ga, gb, pa, pb):
    return jax.nn.sigmoid(ga) * pa + jax.nn.sigmoid(gb) * pb


def _f_resid(x, gate, m):
    return x + gate * m


def _norm_mod_fwd(xcat, g, mods, n_ctx_tiles, tr):
    tt, d = xcat.shape
    which = lambda i: (jnp.where(i >= n_ctx_tiles, 1, 0), 0, 0, 0)

    def fn(x, gg, md):
        return _f_norm_mod(x, gg, md[0], md[1])

    return _tiled(fn, [_rows(xcat, tr), _const(g), (mods, (None, 2, 1, d), which)],
                  [((tt, d), BF16, (tr, d), lambda i: (i, 0))], (tt // tr,), "norm_mod_fwd")[0]


def _norm_mod_bwd(xcat, g, mods, dh, extra, n_ctx_tiles, tr, name):
    tt, d = xcat.shape
    nt = tt // tr
    has_extra = extra is not None

    def body(*refs):
        if has_extra:
            x_ref, g_ref, m_ref, dh_ref, e_ref, dx_ref, dg_ref, dm_ref = refs
        else:
            x_ref, g_ref, m_ref, dh_ref, dx_ref, dg_ref, dm_ref = refs
        i = pl.program_id(0)
        md = m_ref[...]
        _, vjp = jax.vjp(_f_norm_mod, x_ref[...], g_ref[...], md[0], md[1])
        dx, dg, dsh, dsc = vjp(dh_ref[...].astype(F32))
        dx_ref[...] = dx + e_ref[...] if has_extra else dx

        @pl.when(i == 0)
        def _():
            dg_ref[...] = dg

        @pl.when(i > 0)
        def _():
            dg_ref[...] += dg

        fresh = jnp.logical_or(i == 0, i == n_ctx_tiles)

        @pl.when(fresh)
        def _():
            dm_ref[0] = dsh
            dm_ref[1] = dsc

        @pl.when(jnp.logical_not(fresh))
        def _():
            dm_ref[0] += dsh
            dm_ref[1] += dsc

    which = lambda i: (jnp.where(i >= n_ctx_tiles, 1, 0), 0, 0, 0)
    row = pl.BlockSpec((tr, d), lambda i: (i, 0))
    in_specs = [row, pl.BlockSpec((1, d), lambda i: (0, 0)), pl.BlockSpec((None, 2, 1, d), which), row]
    args = [xcat, g, mods, dh]
    if has_extra:
        in_specs.append(row)
        args.append(extra)
    return _pcall(body, name=name, grid=(nt,), in_specs=in_specs,
                  out_specs=[row, pl.BlockSpec((1, d), lambda i: (0, 0)), pl.BlockSpec((None, 2, 1, d), which)],
                  out_shape=[_sds((tt, d), F32), _sds((1, d), F32), _sds((2, 2, 1, d), F32)])(*args)


def _hgrn_tri(reverse):
    t = np.arange(HGRN_BLOCK)
    tri = (t[None, :] >= t[:, None]) if reverse else (t[None, :] <= t[:, None])
    tri = tri.astype(np.float32)
    return jnp.asarray(tri), jnp.asarray(tri.T.copy())


def _hgrn_rowblock(n, n_ctx_blocks, n_blocks, reverse):
    if not reverse:
        return n
    return jnp.where(n < n_ctx_blocks, n_ctx_blocks - 1 - n, n_blocks - 1 - n + n_ctx_blocks)


def _hgrn_gates(fl, lb):
    sg = jax.nn.sigmoid(fl)
    f = lb + (1.0 - lb) * sg
    return sg, f, jnp.log(f), 1.0 - f


def _hgrn_intra_mask(reverse):
    tio = lax.broadcasted_iota(jnp.int32, (HGRN_BLOCK, HEAD_DIM), 0)
    return (lambda s: tio <= s) if reverse else (lambda s: tio >= s)


class _Halves:
    def __init__(self, reverse):
        self.reverse = reverse
        self.h = HGRN_BLOCK // 2
        tio = lax.broadcasted_iota(jnp.int32, (self.h, HEAD_DIM), 0)
        self.tio = tio if reverse else tio + self.h

    def is_half(self, s):
        return s < self.h if self.reverse else s >= self.h

    def rows(self, a):
        return a[:self.h] if self.reverse else a[self.h:]

    def mask(self, s):
        return self.tio <= s if self.reverse else self.tio >= s

    def widen(self, full, half):
        z = jnp.zeros_like(half)
        return full + jnp.concatenate([half, z] if self.reverse else [z, half], axis=0)


def _unrolled(nblk, u, fn, init):
    assert nblk % u == 0

    def trip(b, c):
        for j in range(u):
            c = fn(b * u + j, j, c)
        return c

    return lax.fori_loop(0, nblk // u, trip, init)


HGRN_UNROLL_FWD = (4, 8)
HGRN_UNROLL_BWD = (8, 4)


def _hgrn_fwd(p, lb, seg_f, reverse, n_ctx_rows):
    tt = p.shape[0]
    hb = HGRN_BLOCK
    nblk, nctx = tt // hb, n_ctx_rows // hb
    tri, _ = _hgrn_tri(reverse)
    u1, u3 = HGRN_UNROLL_FWD

    def body(q_ref, f_ref, v_ref, lb_ref, tri_ref, o_ref, st_all, dec_all, qe_buf, cum_blk, k_blk, v_blk):
        mask = _hgrn_intra_mask(reverse)
        hv = _Halves(reverse)
        lbv = lb_ref[...]

        def phase1(n, slot, c):
            r0 = pl.multiple_of(_hgrn_rowblock(n, nctx, nblk, reverse) * hb, hb)
            q, v = q_ref[pl.ds(r0, hb), :], v_ref[pl.ds(r0, hb), :]
            _, f, g, k = _hgrn_gates(f_ref[pl.ds(r0, hb), :], lbv)
            cum = jnp.dot(tri_ref[...], g, precision=HIGHEST, preferred_element_type=F32)
            tot = jnp.sum(g, axis=0, keepdims=True)
            cum_blk[slot] = cum
            k_blk[slot] = k
            v_blk[slot] = v
            oi = jnp.zeros((hb, HEAD_DIM), F32)
            oi_h = jnp.zeros((hb // 2, HEAD_DIM), F32)
            q_h, cum_h = hv.rows(q), hv.rows(cum)
            for s in range(hb):
                if hv.is_half(s):
                    e = jnp.where(hv.mask(s), jnp.exp(jnp.minimum(cum_h - cum_blk[slot, s:s + 1, :], 0.0)), 0.0)
                    a_s = jnp.sum(q_h * e * k_blk[slot, s:s + 1, :], axis=-1, keepdims=True)
                    oi_h = oi_h + a_s * v_blk[slot, s:s + 1, :]
                else:
                    e = jnp.where(mask(s), jnp.exp(jnp.minimum(cum - cum_blk[slot, s:s + 1, :], 0.0)), 0.0)
                    a_s = jnp.sum(q * e * k_blk[slot, s:s + 1, :], axis=-1, keepdims=True)
                    oi = oi + a_s * v_blk[slot, s:s + 1, :]
            o_ref[pl.ds(r0, hb), :] = hv.widen(oi, oi_h)
            qe_buf[pl.ds(r0, hb), :] = q * jnp.exp(cum)
            kl = k * jnp.exp(tot - cum)
            st_all[n] = lax.dot_general(v.astype(BF16), kl.astype(BF16), (((0,), (0,)), ((), ())), preferred_element_type=F32)
            dec_all[pl.ds(n, 1), :] = jnp.exp(tot)
            return c

        _unrolled(nblk, u1, phase1, 0)

        def phase2(n, st):
            kv = st_all[n]
            st_all[n] = st
            return st * dec_all[pl.ds(n, 1), :] + kv

        lax.fori_loop(0, nblk, phase2, jnp.zeros((HEAD_DIM, HEAD_DIM), F32))

        def phase3(n, slot, c):
            r0 = pl.multiple_of(_hgrn_rowblock(n, nctx, nblk, reverse) * hb, hb)
            o_ref[pl.ds(r0, hb), :] += lax.dot_general(qe_buf[pl.ds(r0, hb), :].astype(BF16), st_all[n].astype(BF16),
                                                        (((1,), (1,)), ((), ())), preferred_element_type=F32)
            return c

        _unrolled(nblk, u3, phase3, 0)

    col = lambda seg: pl.BlockSpec((tt, HEAD_DIM), lambda h, seg=seg: (0, seg * N_HEADS + h))
    blk = pltpu.VMEM((u1, hb, HEAD_DIM), F32)
    return _pcall(
        body, name="hgrn_fwd_rev" if reverse else "hgrn_fwd", grid=(N_HEADS,),
        in_specs=[col(0), col(seg_f), col(3), pl.BlockSpec((1, HEAD_DIM), lambda h: (0, h)), pl.BlockSpec((hb, hb), lambda h: (0, 0))],
        out_specs=pl.BlockSpec((tt, HEAD_DIM), lambda h: (0, h)),
        out_shape=_sds((tt, N_HEADS * HEAD_DIM), F32),
        scratch=[pltpu.VMEM((nblk, HEAD_DIM, HEAD_DIM), F32), pltpu.VMEM((nblk, HEAD_DIM), F32), pltpu.VMEM((tt, HEAD_DIM), F32),
                 blk, blk, blk])(p, p, p, lb, tri)


def _hgrn_bwd(p, lb, do, seg_f, reverse, n_ctx_rows, prev):
    tt = p.shape[0]
    hb = HGRN_BLOCK
    nblk, nctx = tt // hb, n_ctx_rows // hb
    tri, tri_t = _hgrn_tri(reverse)
    last_row = 0 if reverse else hb - 1
    has_prev = prev is not None
    u1, u3 = HGRN_UNROLL_BWD

    def body(*refs):
        q_ref, f_ref, v_ref, lb_ref, tri_ref, trit_ref, do_ref = refs[:7]
        refs = refs[7:]
        if has_prev:
            pq_ref, pv_ref = refs[:2]
            refs = refs[2:]
        dq_ref, dfl_ref, dv_ref, dlb_ref, st_all, dd_all, dec_all, cum_buf, cum_blk, k_blk, v_blk, dk_blk, dv_blk = refs
        mask = _hgrn_intra_mask(reverse)
        hv = _Halves(reverse)
        lbv = lb_ref[...]
        tio = lax.broadcasted_iota(jnp.int32, (hb, HEAD_DIM), 0)

        def rows_of(n):
            rb = _hgrn_rowblock(n, nctx, nblk, reverse)
            return rb, pl.multiple_of(rb * hb, hb)

        def load_do(rb):
            lat0 = pl.multiple_of(jnp.maximum(rb - nctx, 0) * hb, hb)
            return jnp.where(rb >= nctx, do_ref[pl.ds(lat0, hb), :], 0.0)

        def phase1(n, slot, c):
            rb, r0 = rows_of(n)
            q, v = q_ref[pl.ds(r0, hb), :], v_ref[pl.ds(r0, hb), :]
            _, f, g, k = _hgrn_gates(f_ref[pl.ds(r0, hb), :], lbv)
            cum = jnp.dot(tri_ref[...], g, precision=HIGHEST, preferred_element_type=F32)
            tot = jnp.sum(g, axis=0, keepdims=True)
            cum_buf[pl.ds(r0, hb), :] = cum
            kl = k * jnp.exp(tot - cum)
            st_all[n] = lax.dot_general(v.astype(BF16), kl.astype(BF16), (((0,), (0,)), ((), ())), preferred_element_type=F32)
            dec_all[pl.ds(n, 1), :] = jnp.exp(tot)
            qe = q * jnp.exp(cum)
            dd_all[n] = lax.dot_general(load_do(rb).astype(BF16), qe.astype(BF16), (((0,), (0,)), ((), ())), preferred_element_type=F32)
            return c

        _unrolled(nblk, u1, phase1, 0)

        def phase2(n, st):
            kv = st_all[n]
            st_all[n] = st
            return st * dec_all[pl.ds(n, 1), :] + kv

        lax.fori_loop(0, nblk, phase2, jnp.zeros((HEAD_DIM, HEAD_DIM), F32))

        def phase2r(i, dst):
            n = nblk - 1 - i
            u = dd_all[n]
            dd_all[n] = dst
            return u + dst * dec_all[pl.ds(n, 1), :]

        lax.fori_loop(0, nblk, phase2r, jnp.zeros((HEAD_DIM, HEAD_DIM), F32))

        def phase3(n, slot, dlb):
            rb, r0 = rows_of(n)
            q, v = q_ref[pl.ds(r0, hb), :], v_ref[pl.ds(r0, hb), :]
            sg, f, g, k = _hgrn_gates(f_ref[pl.ds(r0, hb), :], lbv)
            cum = cum_buf[pl.ds(r0, hb), :]
            tot = jnp.sum(g, axis=0, keepdims=True)
            dob = load_do(rb)
            st, dst = st_all[n], dd_all[n]
            e_cum = jnp.exp(cum)
            e_rest = jnp.exp(tot - cum)
            dq = jnp.dot(dob.astype(BF16), st.astype(BF16), preferred_element_type=F32) * e_cum
            dk_inter = jnp.dot(v.astype(BF16), dst.astype(BF16), preferred_element_type=F32) * e_rest
            dv = lax.dot_general((k * e_rest).astype(BF16), dst.astype(BF16), (((1,), (1,)), ((), ())), preferred_element_type=F32)
            cum_blk[slot] = cum
            k_blk[slot] = k
            v_blk[slot] = v
            dq_h = jnp.zeros((hb // 2, HEAD_DIM), F32)
            q_h, cum_h, dob_h = hv.rows(q), hv.rows(cum), hv.rows(dob)
            for s in range(hb):
                half = hv.is_half(s)
                qs, cs, ds = (q_h, cum_h, dob_h) if half else (q, cum, dob)
                e = jnp.where(hv.mask(s) if half else mask(s), jnp.exp(jnp.minimum(cs - cum_blk[slot, s:s + 1, :], 0.0)), 0.0)
                a_s = jnp.sum(qs * e * k_blk[slot, s:s + 1, :], axis=-1, keepdims=True)
                da_s = jnp.sum(ds * v_blk[slot, s:s + 1, :], axis=-1, keepdims=True)
                gs = da_s * e
                if half:
                    dq_h = dq_h + gs * k_blk[slot, s:s + 1, :]
                else:
                    dq = dq + gs * k_blk[slot, s:s + 1, :]
                dk_blk[slot, s:s + 1, :] = jnp.sum(gs * qs, axis=0, keepdims=True)
                dv_blk[slot, s:s + 1, :] = jnp.sum(a_s * ds, axis=0, keepdims=True)
            dq = hv.widen(dq, dq_h)
            dk = dk_inter + dk_blk[slot]
            dv = dv + dv_blk[slot]
            d_tot = jnp.sum(k * dk_inter, axis=0, keepdims=True) + jnp.exp(tot) * jnp.sum(dst * st, axis=0, keepdims=True)
            dcum = q * dq - k * dk + jnp.where(tio == last_row, d_tot, 0.0)
            dg = jnp.dot(trit_ref[...], dcum, precision=HIGHEST, preferred_element_type=F32)
            df = dg / f - dk
            if has_prev:
                dq = dq + pq_ref[pl.ds(r0, hb), :].astype(F32)
                dv = dv + pv_ref[pl.ds(r0, hb), :].astype(F32)
            dq_ref[pl.ds(r0, hb), :] = dq.astype(dq_ref.dtype)
            dv_ref[pl.ds(r0, hb), :] = dv.astype(dv_ref.dtype)
            dfl_ref[pl.ds(r0, hb), :] = (df * (1.0 - lbv) * sg * (1.0 - sg)).astype(dfl_ref.dtype)
            return dlb + jnp.sum(df * (1.0 - sg), axis=0, keepdims=True)

        dlb_ref[...] = _unrolled(nblk, u3, phase3, jnp.zeros((1, HEAD_DIM), F32))

    col = lambda seg: pl.BlockSpec((tt, HEAD_DIM), lambda h, seg=seg: (0, seg * N_HEADS + h))
    head = pl.BlockSpec((tt, HEAD_DIM), lambda h: (0, h))
    lbs = pl.BlockSpec((1, HEAD_DIM), lambda h: (0, h))
    tris = pl.BlockSpec((hb, hb), lambda h: (0, 0))
    in_specs = [col(0), col(seg_f), col(3), lbs, tris, tris, pl.BlockSpec((do.shape[0], HEAD_DIM), lambda h: (0, h))]
    args = [p, p, p, lb, tri, tri_t, do]
    mid = F32 if not has_prev else BF16
    if has_prev:
        in_specs += [head, head]
        args += list(prev)
    w = N_HEADS * HEAD_DIM
    blk = pltpu.VMEM((u3, hb, HEAD_DIM), F32)
    return _pcall(
        body, name="hgrn_bwd_rev" if reverse else "hgrn_bwd", grid=(N_HEADS,), in_specs=in_specs,
        out_specs=[head, head, head, lbs],
        out_shape=[_sds((tt, w), mid), _sds((tt, w), BF16), _sds((tt, w), mid), _sds((1, w), F32)],
        scratch=[pltpu.VMEM((nblk, HEAD_DIM, HEAD_DIM), F32), pltpu.VMEM((nblk, HEAD_DIM, HEAD_DIM), F32), pltpu.VMEM((nblk, HEAD_DIM), F32),
                 pltpu.VMEM((tt, HEAD_DIM), F32), blk, blk, blk, blk, blk])(*args)


NA_HEADS_PER_STEP = 4


def _na_geometry(rows):
    r = pl.program_id(1)
    rs = jnp.clip(r - WIN_R // 2, 0, rows - WIN_R)
    return r, rs, r - rs


def _na_scores(q, kb, kc, bias):
    scale = HEAD_DIM ** -0.5
    nt = (((1,), (1,)), ((), ()))
    sb = lax.dot_general(q, kb, nt, preferred_element_type=F32) * scale + bias
    sc = lax.dot_general(q, kc, nt, preferred_element_type=F32) * scale
    m = jnp.maximum(jnp.max(sb, axis=-1, keepdims=True), jnp.max(sc, axis=-1, keepdims=True))
    pb, pc = jnp.exp(sb - m), jnp.exp(sc - m)
    inv = 1.0 / (jnp.sum(pb, axis=-1, keepdims=True) + jnp.sum(pc, axis=-1, keepdims=True))
    return pb * inv, pc * inv


def _na_fwd(qn, kall, p, seg_v, bias, n_ctx_rows):
    t, tt = qn.shape[0], kall.shape[0]
    rows = t // GRID_W
    nband = WIN_R * GRID_W

    nh = NA_HEADS_PER_STEP
    wide = nh * HEAD_DIM

    def body(q_ref, k_ref, v_ref, b_ref, o_ref):
        r, rs, _ = _na_geometry(rows)
        k0 = pl.multiple_of(n_ctx_rows + rs * GRID_W, GRID_W)
        for j in range(nh):
            sl = slice(j * HEAD_DIM, (j + 1) * HEAD_DIM)
            q = q_ref[:, sl]
            kb, kc = k_ref[pl.ds(k0, nband), sl], k_ref[pl.ds(0, n_ctx_rows), sl]
            vb, vc = v_ref[pl.ds(k0, nband), sl].astype(BF16), v_ref[pl.ds(0, n_ctx_rows), sl].astype(BF16)
            pb, pc = _na_scores(q, kb, kc, b_ref[j])
            o = jnp.dot(pb.astype(BF16), vb, preferred_element_type=F32) + jnp.dot(pc.astype(BF16), vc, preferred_element_type=F32)
            o_ref[:, sl] = o.astype(o_ref.dtype)

    variant = lambda h, r: (h, r - jnp.clip(r - WIN_R // 2, 0, rows - WIN_R), 0, 0)
    return _pcall(
        body, name="na_fwd", grid=(N_HEADS // nh, rows),
        in_specs=[pl.BlockSpec((GRID_W, wide), lambda h, r: (r, h)),
                  pl.BlockSpec((tt, wide), lambda h, r: (0, h)),
                  pl.BlockSpec((tt, wide), lambda h, r: (0, seg_v * (N_HEADS // nh) + h)),
                  pl.BlockSpec((nh, None, GRID_W, nband), variant)],
        out_specs=pl.BlockSpec((GRID_W, wide), lambda h, r: (r, h)),
        out_shape=_sds((t, N_HEADS * HEAD_DIM), BF16))(qn, kall, p, bias)


def _na_bwd(qn, kall, p, seg_v, bias, do, n_ctx_rows):
    t, tt = qn.shape[0], kall.shape[0]
    rows = t // GRID_W
    nband = WIN_R * GRID_W
    scale = HEAD_DIM ** -0.5
    tn = (((0,), (0,)), ((), ()))
    nt = (((1,), (1,)), ((), ()))

    nh = NA_HEADS_PER_STEP
    wide = nh * HEAD_DIM

    def body(q_ref, k_ref, v_ref, b_ref, do_ref, dq_ref, dk_ref, dv_ref, db_ref, dv_acc):
        r, rs, var = _na_geometry(rows)
        k0 = pl.multiple_of(n_ctx_rows + rs * GRID_W, GRID_W)
        fresh = jnp.logical_or(r <= WIN_R // 2, r > rows - WIN_R // 2)

        @pl.when(r == 0)
        def _():
            dk_ref[...] = jnp.zeros_like(dk_ref)
            dv_acc[...] = jnp.zeros_like(dv_acc)

        for j in range(nh):
            sl = slice(j * HEAD_DIM, (j + 1) * HEAD_DIM)
            q = q_ref[:, sl]
            kb, kc = k_ref[pl.ds(k0, nband), sl], k_ref[pl.ds(0, n_ctx_rows), sl]
            vb, vc = v_ref[pl.ds(k0, nband), sl].astype(BF16), v_ref[pl.ds(0, n_ctx_rows), sl].astype(BF16)
            pb, pc = _na_scores(q, kb, kc, b_ref[j])
            dof = do_ref[:, sl].astype(F32)
            dob = dof.astype(BF16)
            o = jnp.dot(pb.astype(BF16), vb, preferred_element_type=F32) + jnp.dot(pc.astype(BF16), vc, preferred_element_type=F32)
            delta = jnp.sum(dof * o, axis=-1, keepdims=True)
            dsb = pb * (lax.dot_general(dob, vb, nt, preferred_element_type=F32) - delta)
            dsc = pc * (lax.dot_general(dob, vc, nt, preferred_element_type=F32) - delta)
            dsb16, dsc16 = dsb.astype(BF16), dsc.astype(BF16)
            dq_ref[:, sl] = (jnp.dot(dsb16, kb, preferred_element_type=F32) + jnp.dot(dsc16, kc, preferred_element_type=F32)) * scale
            dk_ref[pl.ds(k0, nband), sl] += lax.dot_general(dsb16, q, tn, preferred_element_type=F32) * scale
            dk_ref[pl.ds(0, n_ctx_rows), sl] += lax.dot_general(dsc16, q, tn, preferred_element_type=F32) * scale
            dv_acc[pl.ds(k0, nband), sl] += lax.dot_general(pb.astype(BF16), dob, tn, preferred_element_type=F32)
            dv_acc[pl.ds(0, n_ctx_rows), sl] += lax.dot_general(pc.astype(BF16), dob, tn, preferred_element_type=F32)

            @pl.when(fresh)
            def _(j=j, dsb=dsb):
                db_ref[j] = dsb

            @pl.when(jnp.logical_not(fresh))
            def _(j=j, dsb=dsb):
                db_ref[j] += dsb

        @pl.when(r == rows - 1)
        def _():
            dv_ref[...] = dv_acc[...].astype(dv_ref.dtype)

    variant = lambda h, r: (h, r - jnp.clip(r - WIN_R // 2, 0, rows - WIN_R), 0, 0)
    head_all = pl.BlockSpec((tt, wide), lambda h, r: (0, h))
    qspec = pl.BlockSpec((GRID_W, wide), lambda h, r: (r, h))
    w = N_HEADS * HEAD_DIM
    return _pcall(
        body, name="na_bwd", grid=(N_HEADS // nh, rows),
        in_specs=[qspec, head_all, pl.BlockSpec((tt, wide), lambda h, r: (0, seg_v * (N_HEADS // nh) + h)),
                  pl.BlockSpec((nh, None, GRID_W, nband), variant), qspec],
        out_specs=[qspec, head_all, head_all, pl.BlockSpec((nh, None, GRID_W, nband), variant)],
        out_shape=[_sds((t, w), F32), _sds((tt, w), F32), _sds((tt, w), BF16), _sds((N_HEADS, WIN_R, GRID_W, nband), F32)],
        scratch=[pltpu.VMEM((tt, wide), F32)])(qn, kall, p, bias, do)


def _na_tables(t, n_ctx_rows):
    half, nf = HEAD_DIM // 2, HEAD_DIM // 4
    pos = np.arange(t)
    lane = np.arange(HEAD_DIM)
    inv = ROPE_THETA ** (-(np.arange(nf, dtype=np.float32)) / nf)
    which = np.where(lane < half, pos[:, None] // GRID_W, pos[:, None] % GRID_W).astype(np.float32)
    ang = which * inv[lane % nf][None, :]
    first = (lane % half) < nf
    cos = np.concatenate([np.ones((n_ctx_rows, HEAD_DIM), np.float32), np.cos(ang).astype(np.float32)])
    sin = np.concatenate([np.zeros((n_ctx_rows, HEAD_DIM), np.float32), np.where(first[None, :], -np.sin(ang), np.sin(ang)).astype(np.float32)])
    w = np.arange(GRID_W)
    dc = np.clip(w[None, :] - w[:, None], -(WIN_C - 1), WIN_C - 1) + WIN_C - 1
    onehot = np.zeros((32, GRID_W * GRID_W), np.float32)
    onehot[dc.reshape(-1), np.arange(GRID_W * GRID_W)] = 1.0
    cs = np.clip(w - WIN_C // 2, 0, GRID_W - WIN_C)
    col_in = (w[None, :] >= cs[:, None]) & (w[None, :] < cs[:, None] + WIN_C)
    onehot *= col_in.reshape(1, -1)
    neg = np.where(col_in, 0.0, NEG_BIG).astype(np.float32)
    return jnp.asarray(cos), jnp.asarray(sin), jnp.asarray(onehot), jnp.asarray(neg)


def _bias_slabs(rel_bias, onehot, neg):
    nr = 2 * WIN_R - 1
    rb = jnp.pad(rel_bias.reshape(N_HEADS * nr, 2 * WIN_C - 1), ((0, 0), (0, 1)))
    spread = _mm_f32(rb, onehot, "bias_spread").reshape(N_HEADS, nr, GRID_W, GRID_W)
    slabs = [spread[:, WIN_R - 1 - v:2 * WIN_R - 1 - v] for v in range(WIN_R)]
    b = jnp.stack(slabs, axis=1) + neg[None, None, None]
    return b.transpose(0, 1, 3, 2, 4).reshape(N_HEADS, WIN_R, GRID_W, WIN_R * GRID_W)


def _bias_grad(dbias, onehot):
    nr = 2 * WIN_R - 1
    d = dbias.reshape(N_HEADS, WIN_R, GRID_W, WIN_R, GRID_W).transpose(0, 1, 3, 2, 4)
    tot = jnp.zeros((N_HEADS, nr, GRID_W, GRID_W), F32)
    for v in range(WIN_R):
        tot = tot + jnp.pad(d[:, v], ((0, 0), (WIN_R - 1 - v, v), (0, 0), (0, 0)))
    g = _mm_f32(tot.reshape(N_HEADS * nr, GRID_W * GRID_W), onehot, "bias_grad", trans_b=True)
    return g[:, :2 * WIN_C - 1].reshape(1, N_HEADS, nr, 2 * WIN_C - 1)


def _shift_rows(u, up):
    n = u.shape[0]
    tio = lax.broadcasted_iota(jnp.int32, u.shape, 0)
    if up:
        return jnp.where(tio == n - 1, 0.0, pltpu.roll(u, n - 1, 0))
    return jnp.where(tio == 0, 0.0, pltpu.roll(u, 1, 0))


def _conv3(u, w_ref, b_ref):
    um, up = _shift_rows(u, False), _shift_rows(u, True)
    return um, up, um * w_ref[0:1, :] + u * w_ref[1:2, :] + up * w_ref[2:3, :] + b_ref[...]


def _ffn_act_fwd(u0, t3, cw, cb):
    t, n = u0.shape
    tc = _pick(n, (256, 128))

    def body(u_ref, t_ref, w_ref, b_ref, a_ref):
        _, _, uc = _conv3(u_ref[...], w_ref, b_ref)
        a_ref[...] = (uc * jax.nn.sigmoid(uc) * t_ref[...]).astype(a_ref.dtype)

    col = lambda rows_: pl.BlockSpec((rows_, tc), lambda j: (0, j))
    return _pcall(body, name="ffn_act_fwd", grid=(n // tc,), in_specs=[col(t), col(t), col(8), col(1)], out_specs=col(t),
                  out_shape=_sds((t, n), BF16))(u0, t3, cw, cb)


def _ffn_act_bwd(u0, t3, cw, cb, da):
    t, n = u0.shape
    tc = _pick(n, (256, 128))

    def body(u_ref, t_ref, w_ref, b_ref, da_ref, du_ref, dt_ref, dw_ref, db_ref):
        u = u_ref[...]
        um, up, uc = _conv3(u, w_ref, b_ref)
        sg = jax.nn.sigmoid(uc)
        dav = da_ref[...].astype(F32)
        dt_ref[...] = (dav * uc * sg).astype(dt_ref.dtype)
        duc = dav * t_ref[...] * sg * (1.0 + uc * (1.0 - sg))
        du = _shift_rows(duc, True) * w_ref[0:1, :] + duc * w_ref[1:2, :] + _shift_rows(duc, False) * w_ref[2:3, :]
        du_ref[...] = du.astype(du_ref.dtype)
        dw_ref[...] = jnp.zeros_like(dw_ref)
        dw_ref[0:1, :] = jnp.sum(duc * um, axis=0, keepdims=True)
        dw_ref[1:2, :] = jnp.sum(duc * u, axis=0, keepdims=True)
        dw_ref[2:3, :] = jnp.sum(duc * up, axis=0, keepdims=True)
        db_ref[...] = jnp.sum(duc, axis=0, keepdims=True)

    col = lambda rows_: pl.BlockSpec((rows_, tc), lambda j: (0, j))
    return _pcall(body, name="ffn_act_bwd", grid=(n // tc,), in_specs=[col(t), col(t), col(8), col(1), col(t)],
                  out_specs=[col(t), col(t), col(8), col(1)],
                  out_shape=[_sds((t, n), BF16), _sds((t, n), BF16), _sds((8, n), F32), _sds((1, n), F32)])(u0, t3, cw, cb, da)


def _adam_math(g, w, m, v):
    m2 = ADAM_B1 * m + (1.0 - ADAM_B1) * g
    v2 = ADAM_B2 * v + (1.0 - ADAM_B2) * (g * g)
    m_hat = m2 / (1.0 - ADAM_B1 ** ADAM_STEP)
    v_hat = v2 / (1.0 - ADAM_B2 ** ADAM_STEP)
    return -ADAM_LR * (m_hat / (jnp.sqrt(v_hat) + ADAM_EPS) + ADAM_WD * w), m2, v2


def _adam_big(parts, w, m, v, name):
    r, c = w.shape
    npart, _, cp = parts.shape
    tr = _pick(r, (64, 32, 16, 8))

    def body(p_ref, w_ref, m_ref, v_ref, g_ref, d_ref, m2_ref, v2_ref):
        g = p_ref[0, :, 0:c].astype(F32)
        for i in range(1, npart):
            g = g + p_ref[i, :, 0:c].astype(F32)
        d, m2, v2 = _adam_math(g, w_ref[...], m_ref[...], v_ref[...])
        g_ref[...] = g
        d_ref[...] = d
        m2_ref[...] = m2
        v2_ref[...] = v2

    row = pl.BlockSpec((tr, c), lambda i: (i, 0))
    return _pcall(body, name=name, grid=(r // tr,),
                  in_specs=[pl.BlockSpec((npart, tr, cp), lambda i: (0, i, 0)), row, row, row],
                  out_specs=[row] * 4, out_shape=[_sds((r, c), F32)] * 4)(parts, w, m, v)


def _adam_small(g, w, m, v):
    def body(g_ref, w_ref, m_ref, v_ref, d_ref, m2_ref, v2_ref):
        d_ref[...], m2_ref[...], v2_ref[...] = _adam_math(g_ref[...], w_ref[...], m_ref[...], v_ref[...])

    return _pcall(body, name="adam_small", out_shape=[_sds(g.shape, F32)] * 3)(g, w, m, v)


def _sum_parts(parts, name):
    def body(p_ref, o_ref):
        s = p_ref[0]
        for i in range(1, N_DEV):
            s = s + p_ref[i]
        o_ref[...] = s

    return _pcall(body, name=name, out_shape=_sds(parts.shape[1:], F32))(parts)


class _Pack:
    def __init__(self, shapes):
        self.shapes = shapes
        self.sizes = [int(np.prod(s)) for s in shapes]
        self.padded = [-(-n // (8 * LANE)) * 8 * LANE for n in self.sizes]
        self.offs = np.concatenate([[0], np.cumsum(self.padded)]).tolist()

    def pack(self, arrs):
        flat = [jnp.pad(a.reshape(-1).astype(F32), (0, p - n)) for a, n, p in zip(arrs, self.sizes, self.padded)]
        return jnp.concatenate(flat).reshape(-1, LANE)

    def unpack(self, slab):
        flat = slab.reshape(-1)
        return [flat[o:o + n].reshape(s) for o, n, s in zip(self.offs, self.sizes, self.shapes)]


def kernel(x, c, ctx, c_ctx, ada_w, ada_b, norm1_g, norm2_g, w_in, hgrn_lb_logits, hgrn_norm_g, na_q_norm_g, na_k_norm_g, na_rel_bias, w_branch_a, w_branch_b, w_out, ffn_w1, ffn_w3, ffn_conv_w, ffn_conv_b, ffn_w2, loss_target, m_c_ctx, m_ada_w, m_ada_b, m_norm1_g, m_norm2_g, m_w_in, m_hgrn_lb_logits, m_hgrn_norm_g, m_na_q_norm_g, m_na_k_norm_g, m_na_rel_bias, m_w_branch_a, m_w_branch_b, m_w_out, m_ffn_w1, m_ffn_w3, m_ffn_conv_w, m_ffn_conv_b, m_ffn_w2, v_c_ctx, v_ada_w, v_ada_b, v_norm1_g, v_norm2_g, v_w_in, v_hgrn_lb_logits, v_hgrn_norm_g, v_na_q_norm_g, v_na_k_norm_g, v_na_rel_bias, v_w_branch_a, v_w_branch_b, v_w_out, v_ffn_w1, v_ffn_w3, v_ffn_conv_w, v_ffn_conv_b, v_ffn_w2):
    t, d = x.shape[1], x.shape[2]
    n_ctx = ctx.shape[1]
    tt = n_ctx + t
    hw = N_HEADS * HEAD_DIM
    ci = w_in.shape[2]
    ca = ada_w.shape[2]
    ff_l = ffn_w1.shape[2]
    ff_p = -(-ff_l // LANE) * LANE
    rows = t // GRID_W
    assert rows >= WIN_R and t % GRID_W == 0 and n_ctx % GRID_W == 0 and ci % LANE == 0 and d % LANE == 0
    me = 4 * lax.axis_index("x") + 2 * lax.axis_index("y") + lax.axis_index("c")
    tr = _pick(n_ctx, (256, 128, 64))
    n_ctx_tiles = n_ctx // tr

    pad_c = lambda w: jnp.pad(w, ((0, 0), (0, ff_p - ff_l)))
    small_in = [c, hgrn_lb_logits.reshape(4, HEAD_DIM), jnp.pad(ffn_conv_w[0], ((0, 5), (0, ff_p - ff_l)))]
    c_all, lb_parts, cw_all = _exchange(small_in, "gather_params", scatter=False)
    ff = N_DEV * ff_p

    cc = jnp.concatenate([c_all.reshape(N_DEV, d), jnp.broadcast_to(c_ctx[None, :], (N_DEV, d))], axis=0)
    act = _tiled(lambda v: v * jax.nn.sigmoid(v), [_const(cc)], [(cc.shape, BF16, cc.shape, lambda *_: (0, 0))], (), "silu_c")[0]
    ada16 = _cast_bf16(ada_w[0], "cast_ada")
    mod_cols = _mm_xw(act, ada16.reshape(1, d, ca), F32, "ada_fwd")
    mod_all = _exchange([mod_cols], "gather_mod", scatter=False)[0]

    with _after(mod_all):
        w_in16 = _cast_bf16(w_in[0], "cast_w_in")
    g_in, = _gather_sc([w_in16], "gather_w_in", 1)
    rest16 = []
    for w_, nm in ((w_branch_a[0], "cast_w_a"), (w_branch_b[0], "cast_w_b"), (w_out[0], "cast_w_out"), (pad_c(ffn_w1[0]), "cast_w1"),
                   (pad_c(ffn_w3[0]), "cast_w3"), (jnp.pad(ffn_w2[0], ((0, ff_p - ff_l), (0, 0))), "cast_w2")):
        with _after(w_in16):
            rest16.append(_cast_bf16(w_, nm))
    g_a, g_b, g_out, g_w1, g_w3, g_w2 = _gather_sc(rest16, "gather_rest", 2)
    g_out = g_out.reshape(1, d, d)
    g_w2 = g_w2.reshape(1, ff, d)
    mod_all = mod_all.transpose(1, 0, 2).reshape(2 * N_DEV, N_MOD * d) + ada_b
    mod_l = lax.dynamic_slice_in_dim(mod_all, me, 1, axis=0).reshape(N_MOD, 1, d)
    mod_c = mod_all[N_DEV:N_DEV + 1].reshape(N_MOD, 1, d)
    mods1 = jnp.stack([mod_c[0:2], mod_l[0:2]])
    mods2 = jnp.stack([mod_l[3:5], mod_l[3:5]])
    gate1, gate2 = mod_l[2], mod_l[5]

    xcat = jnp.concatenate([ctx[0], x[0]], axis=0)
    hcat = _norm_mod_fwd(xcat, norm1_g, mods1, n_ctx_tiles, tr)
    p = _mm_xw(hcat, g_in, F32, "in_proj")
    lb_logits = lb_parts.transpose(1, 0, 2).reshape(2, 2, hw)
    lb_soft = _tiled(lambda a, b: (1.0 / (1.0 + jnp.exp(b - a)),), [_const(lb_logits[:, 0]), _const(lb_logits[:, 1])],
                     [((2, hw), F32, (2, hw), lambda *_: (0, 0))], (), "lb_softmax")[0]
    lb_f, lb_b = lb_soft[0:1], lb_soft[1:2]
    o_f = _hgrn_fwd(p, lb_f, 1, False, n_ctx)
    o_b = _hgrn_fwd(p, lb_b, 2, True, n_ctx)

    cos, sin, onehot, neg = _na_tables(t, n_ctx)
    bias = _bias_slabs(na_rel_bias[0], onehot, neg)
    hcol = lambda seg, off=0: (lambda i, h: (i + off, seg * N_HEADS + h))
    tq = tr
    lat0 = n_ctx // tq
    qk_fn = lambda tv, g, cs, sn: (_f_qk(tv, g, cs, sn),)
    tab = lambda a, off=0: (a, (tq, HEAD_DIM), lambda i, h: (i + off, 0))
    qn = _tiled(qk_fn, [(p, (tq, HEAD_DIM), hcol(5, lat0)), _const(na_q_norm_g), tab(cos, lat0), tab(sin, lat0)],
                [((t, hw), BF16, (tq, HEAD_DIM), lambda i, h: (i, h))], (t // tq, N_HEADS), "q_norm_rope")[0]
    kall = _tiled(qk_fn, [(p, (tq, HEAD_DIM), hcol(6)), _const(na_k_norm_g), tab(cos), tab(sin)],
                  [((tt, hw), BF16, (tq, HEAD_DIM), lambda i, h: (i, h))], (tt // tq, N_HEADS), "k_norm_rope")[0]
    y_b = _na_fwd(qn, kall, p, 7, bias, n_ctx)

    trh = tr
    lat_h = n_ctx // trh
    ospec = lambda a: (a, (trh, HEAD_DIM), lambda i, h: (i + lat_h, h))
    y_a = _tiled(lambda a, b, gt, g: (_f_readout(a, b, gt, g),),
                 [ospec(o_f), ospec(o_b), (p, (trh, HEAD_DIM), hcol(4, lat_h)), _const(hgrn_norm_g)],
                 [((t, hw), BF16, (trh, HEAD_DIM), lambda i, h: (i, h))], (t // trh, N_HEADS), "hgrn_readout")[0]

    p_a = _mm_xw(y_a, g_a, F32, "branch_a")
    p_b = _mm_xw(y_b, g_b, F32, "branch_b")
    td = _pick(d, (512, 256, 128))
    nd_t = d // td
    lat_r = n_ctx // tr
    gcol = lambda k: (p, (tr, td), lambda i, j, k=k: (i + lat_r, 8 * hw // td + k * nd_t + j))
    dtile = lambda a: (a, (tr, td), lambda i, j: (i, j))
    z = _tiled(lambda ga, gb, pa, pb: (_f_merge(ga, gb, pa, pb),), [gcol(0), gcol(1), dtile(p_a), dtile(p_b)],
               [((t, d), BF16, (tr, td), lambda i, j: (i, j))], (t // tr, nd_t), "merge")[0]
    mix = _mm_xw(z, g_out, F32, "out_proj")
    xl = x[0]
    x_mid = _tiled(lambda a, g, m_: (_f_resid(a, g, m_),), [_rows(xl, tr), _const(gate1), _rows(mix, tr)],
                   [((t, d), F32, (tr, d), lambda i: (i, 0))], (t // tr,), "resid1")[0]

    h2 = _norm_mod_fwd(x_mid, norm2_g, mods2, 0, tr)
    u0 = _mm_xw(h2, g_w1, F32, "ffn_up1")
    t3 = _mm_xw(h2, g_w3, F32, "ffn_up3")
    cw_full = cw_all.transpose(1, 0, 2).reshape(8, ff)
    cb_full = jnp.pad(ffn_conv_b.reshape(N_DEV, ff_l), ((0, 0), (0, ff_p - ff_l))).reshape(1, ff)
    a_act = _ffn_act_fwd(u0, t3, cw_full, cb_full)
    f_out = _mm_xw(a_act, g_w2, F32, "ffn_down")

    def loss_fn(xm, g, f, tg):
        err = xm + g * f - tg
        return err * (1.0 / d), jnp.sum(err * err, axis=0, keepdims=True) * (0.5 / d), jnp.sum(err * (1.0 / d) * f, axis=0, keepdims=True)

    dy, loss_cols, d_gate2 = _tiled(loss_fn, [_rows(x_mid, tr), _const(gate2), _rows(f_out, tr), _rows(loss_target[0], tr)],
                                    [((t, d), F32, (tr, d), lambda i: (i, 0)), ((1, d), F32, (1, d), lambda i: (0, 0)),
                                     ((1, d), F32, (1, d), lambda i: (0, 0))], (t // tr,), "loss", acc=(1, 2))

    df = _tiled(lambda a, g: (a * g,), [_rows(dy, tr), _const(gate2)], [((t, d), BF16, (tr, d), lambda i: (i, 0))], (t // tr,), "d_ffn_out")[0]
    d_w2 = _mm_xtdy(a_act, df, 1, BF16, "d_w2")
    da = _mm_dyw(df, g_w2, BF16, "d_act")
    du0, dt3, d_cw, d_cb = _ffn_act_bwd(u0, t3, cw_full, cb_full, da)
    d_w1 = _mm_xtdy(h2, du0, N_DEV, BF16, "d_w1")
    d_w3 = _mm_xtdy(h2, dt3, N_DEV, BF16, "d_w3")
    ffn_blocks = [d_w1, d_w3, d_w2.reshape(N_DEV, ff_p, d)]
    ffn_theirs = _pair_swap_sc(ffn_blocks, "scatter_ffn_pair", 3)
    dh2 = _mm_dyw(du0, g_w1, F32, "d_h2_a")
    dh2 = _mm_dyw(dt3, g_w3, F32, "d_h2_b", init=dh2)
    with _after(dh2):
        ffn_sums = [_pair_add(ffn_blocks[0], ffn_theirs[0], "pair_add_w1")]
    ffn_sums += [_pair_add(ffn_blocks[1], ffn_theirs[1], "pair_add_w3"), _pair_add(ffn_blocks[2], ffn_theirs[2], "pair_add_w2")]
    r_w1, r_w3, r_w2 = _chip_scatter_sc(ffn_sums, "scatter_ffn_chips", 4)
    dx_mid, d_norm2, d_mods2 = _norm_mod_bwd(x_mid, norm2_g, mods2, dh2, dy, 0, tr, "norm_mod2_bwd")

    dm, d_gate1 = _tiled(lambda dxm, g, m_: (dxm * g, jnp.sum(dxm * m_, axis=0, keepdims=True)),
                         [_rows(dx_mid, tr), _const(gate1), _rows(mix, tr)],
                         [((t, d), BF16, (tr, d), lambda i: (i, 0)), ((1, d), F32, (1, d), lambda i: (0, 0))], (t // tr,), "d_resid1", acc=(1,))
    d_wout = _mm_xtdy(z, dm, 1, BF16, "d_w_out")
    dz = _mm_dyw(dm, g_out, F32, "d_merge")

    def merge_bwd(ga, gb, pa, pb, dzv):
        _, vjp = jax.vjp(_f_merge, ga, gb, pa, pb)
        return vjp(dzv)

    dga, dgb, dpa, dpb = _tiled(merge_bwd, [gcol(0), gcol(1), dtile(p_a), dtile(p_b), dtile(dz)],
                                [((t, d), BF16, (tr, td), lambda i, j: (i, j))] * 4, (t // tr, nd_t), "merge_bwd")
    d_wa = _mm_xtdy(y_a, dpa, N_DEV, BF16, "d_w_a")
    d_wb = _mm_xtdy(y_b, dpb, N_DEV, BF16, "d_w_b")
    mix_blocks = [d_wa, d_wb, d_wout.reshape(N_DEV, d // N_DEV, d)]
    mix_theirs = _pair_swap_sc(mix_blocks, "scatter_mix_pair", 5)
    dy_a = _mm_dyw(dpa, g_a, F32, "d_y_a")
    dy_b = _mm_dyw(dpb, g_b, BF16, "d_y_b")
    with _after(dy_b):
        mix_sums = [_pair_add(mix_blocks[0], mix_theirs[0], "pair_add_wa")]
    mix_sums += [_pair_add(mix_blocks[1], mix_theirs[1], "pair_add_wb"), _pair_add(mix_blocks[2], mix_theirs[2], "pair_add_wout")]
    r_a, r_b, r_out = _chip_scatter_sc(mix_sums, "scatter_mix_chips", 6)

    def readout_bwd(a, b, gt, g, ct):
        _, vjp = jax.vjp(_f_readout, a, b, gt, g)
        da_, _, dgt, dg = vjp(ct)
        return da_, dgt, dg

    hsp = lambda dt: ((t, hw), dt, (trh, HEAD_DIM), lambda i, h: (i, h))
    do_h, d_gate_o, d_hnorm = _tiled(
        readout_bwd, [ospec(o_f), ospec(o_b), (p, (trh, HEAD_DIM), hcol(4, lat_h)), _const(hgrn_norm_g), (dy_a, (trh, HEAD_DIM), lambda i, h: (i, h))],
        [hsp(F32), hsp(BF16), ((1, HEAD_DIM), F32, (1, HEAD_DIM), lambda i, h: (0, 0))], (t // trh, N_HEADS), "readout_bwd", acc=(2,))
    dq1, dfl_f, dv1, dlb_f = _hgrn_bwd(p, lb_f, do_h, 1, False, n_ctx, None)
    dq_h, dfl_b, dv_h, dlb_b = _hgrn_bwd(p, lb_b, do_h, 2, True, n_ctx, (dq1, dv1))

    dqn, dkall, dv_na, dbias = _na_bwd(qn, kall, p, 7, bias, dy_b, n_ctx)

    def qk_bwd(tv, g, cs, sn, ct):
        _, vjp = jax.vjp(lambda a, b: _f_qk(a, b, cs, sn), tv, g)
        return vjp(ct)

    d_pq, d_qnorm = _tiled(qk_bwd, [(p, (tq, HEAD_DIM), hcol(5, lat0)), _const(na_q_norm_g), tab(cos, lat0), tab(sin, lat0),
                                    (dqn, (tq, HEAD_DIM), lambda i, h: (i, h))],
                           [((t, hw), BF16, (tq, HEAD_DIM), lambda i, h: (i, h)), ((1, HEAD_DIM), F32, (1, HEAD_DIM), lambda i, h: (0, 0))],
                           (t // tq, N_HEADS), "q_norm_rope_bwd", acc=(1,))
    d_pk, d_knorm = _tiled(qk_bwd, [(p, (tq, HEAD_DIM), hcol(6)), _const(na_k_norm_g), tab(cos), tab(sin),
                                    (dkall, (tq, HEAD_DIM), lambda i, h: (i, h))],
                           [((tt, hw), BF16, (tq, HEAD_DIM), lambda i, h: (i, h)), ((1, HEAD_DIM), F32, (1, HEAD_DIM), lambda i, h: (0, 0))],
                           (tt // tq, N_HEADS), "k_norm_rope_bwd", acc=(1,))

    zc = lambda w_: jnp.zeros((n_ctx, w_), BF16)
    lat_only = lambda a: jnp.concatenate([zc(a.shape[1]), a], axis=0)
    dp = jnp.concatenate([dq_h, dfl_f, dfl_b, dv_h, lat_only(d_gate_o), lat_only(d_pq), d_pk, dv_na, lat_only(dga), lat_only(dgb)], axis=1)
    d_win = _mm_xtdy(hcat, dp, N_DEV, BF16, "d_w_in")
    win_theirs, = _pair_swap_sc([d_win], "scatter_w_in_pair", 7)
    with _after(d_win):
        dhcat = _mm_dyw(dp, g_in, BF16, "d_hcat")
    zero_ctx = jnp.concatenate([jnp.zeros((n_ctx, d), F32), dx_mid], axis=0)
    dxcat, d_norm1, d_mods1 = _norm_mod_bwd(xcat, norm1_g, mods1, dhcat, zero_ctx, n_ctx_tiles, tr, "norm_mod1_bwd")
    grad_x = dxcat[n_ctx:][None]

    zd = jnp.zeros((1, d), F32)
    dmod_l = jnp.concatenate([d_mods1[1, 0], d_mods1[1, 1], d_gate1, d_mods2[1, 0], d_mods2[1, 1], d_gate2], axis=1)
    dmod_c = jnp.concatenate([d_mods1[0, 0], d_mods1[0, 1], zd, zd, zd, zd], axis=1)
    dmods = jnp.concatenate([dmod_l, dmod_c], axis=0).reshape(2, N_DEV, ca).transpose(1, 0, 2)
    dmods = jnp.pad(dmods, ((0, 0), (0, 6), (0, 0)))
    got = _exchange([dmods], "scatter_dmod", scatter=True)[0]
    dm_rows = jnp.concatenate([got[:, 0], got[:, 1]], axis=0)
    d_ada = _mm_xtdy(act, dm_rows, 1, F32, "d_ada_w")[0]
    back = _mm_dyw(dm_rows, ada16.reshape(1, d, ca), F32, "d_silu_c")
    d_cctx_part = _tiled(lambda b, v: (jnp.sum(b[N_DEV:], axis=0, keepdims=True) * (jax.nn.sigmoid(v) * (1.0 + v * (1.0 - jax.nn.sigmoid(v)))),),
                         [_const(back), _const(c_ctx.reshape(1, d))], [((1, d), F32, (1, d), lambda *_: (0, 0))], (), "d_c_ctx")[0]

    d_rel = _bias_grad(dbias, onehot)
    d_lb_soft = jnp.concatenate([dlb_f, dlb_b], axis=0)
    d_lb0 = _tiled(lambda s, g: (g * s * (1.0 - s),), [_const(lb_soft), _const(d_lb_soft)], [((2, hw), F32, (2, hw), lambda *_: (0, 0))], (), "d_lb")[0]
    d_lb_full = jnp.stack([d_lb0, -d_lb0], axis=1)
    d_cw_l = d_cw[:3].reshape(3, N_DEV, ff_p)[:, :, :ff_l].reshape(1, 3, N_DEV * ff_l)
    d_cb_l = d_cb.reshape(N_DEV, ff_p)[:, :ff_l].reshape(1, N_DEV * ff_l)
    small = [d_cctx_part.reshape(d), (dmod_l + dmod_c), d_norm1, d_norm2, d_lb_full, d_hnorm, d_qnorm, d_knorm, d_rel, d_cw_l, d_cb_l, loss_cols]
    pk = _Pack([a.shape for a in small])
    small_slab = pk.pack(small)
    small_parts, = _gather_sc([small_slab], "gather_small", 9)

    res = {}
    with _after(small_slab):
        win_sums = _pair_add(d_win, win_theirs, "pair_add_w_in")
    r_in, = _chip_scatter_sc([win_sums], "scatter_w_in_chips", 8)
    with _after(win_sums):
        res["ada_w"] = _adam_big(d_ada[None], ada_w[0], m_ada_w[0], v_ada_w[0], "adam_ada")
    res["ffn_w1"] = _adam_big(r_w1, ffn_w1[0], m_ffn_w1[0], v_ffn_w1[0], "adam_w1")
    res["ffn_w3"] = _adam_big(r_w3, ffn_w3[0], m_ffn_w3[0], v_ffn_w3[0], "adam_w3")
    res["ffn_w2"] = _adam_big(r_w2, ffn_w2[0], m_ffn_w2[0], v_ffn_w2[0], "adam_w2")
    res["w_branch_a"] = _adam_big(r_a, w_branch_a[0], m_w_branch_a[0], v_w_branch_a[0], "adam_w_a")
    res["w_branch_b"] = _adam_big(r_b, w_branch_b[0], m_w_branch_b[0], v_w_branch_b[0], "adam_w_b")
    res["w_out"] = _adam_big(r_out, w_out[0], m_w_out[0], v_w_out[0], "adam_w_out")

    with _after(res["w_out"][1]):
        tot = _sum_parts(small_parts, "sum_small")
    g_cctx, g_ada_b, g_n1, g_n2, g_lb, g_hn, g_qn, g_kn, g_rel, g_cw, g_cb, loss_all = pk.unpack(tot)
    loss = _tiled(lambda v: (jnp.sum(v, axis=1, keepdims=True),), [_const(loss_all)], [((1, 1), F32, (1, 1), lambda *_: (0, 0))], (), "loss_total")[0][0, 0]
    g_lb = lax.dynamic_slice_in_dim(g_lb, me * HEAD_DIM, HEAD_DIM, axis=2)
    g_cw = lax.dynamic_slice_in_dim(g_cw, me * ff_l, ff_l, axis=2)
    small_names = [("c_ctx", g_cctx, c_ctx, m_c_ctx, v_c_ctx), ("ada_b", g_ada_b, ada_b, m_ada_b, v_ada_b),
                   ("norm1_g", g_n1, norm1_g, m_norm1_g, v_norm1_g), ("norm2_g", g_n2, norm2_g, m_norm2_g, v_norm2_g),
                   ("hgrn_lb_logits", g_lb, hgrn_lb_logits, m_hgrn_lb_logits, v_hgrn_lb_logits),
                   ("hgrn_norm_g", g_hn, hgrn_norm_g, m_hgrn_norm_g, v_hgrn_norm_g), ("na_q_norm_g", g_qn, na_q_norm_g, m_na_q_norm_g, v_na_q_norm_g),
                   ("na_k_norm_g", g_kn, na_k_norm_g, m_na_k_norm_g, v_na_k_norm_g), ("na_rel_bias", g_rel, na_rel_bias, m_na_rel_bias, v_na_rel_bias),
                   ("ffn_conv_w", g_cw, ffn_conv_w, m_ffn_conv_w, v_ffn_conv_w), ("ffn_conv_b", g_cb, ffn_conv_b, m_ffn_conv_b, v_ffn_conv_b)]
    pk2 = _Pack([s[1].shape for s in small_names])
    sd, sm, sv = _adam_small(*[pk2.pack([s[i] for s in small_names]) for i in (1, 2, 3, 4)])
    sd, sm, sv = pk2.unpack(sd), pk2.unpack(sm), pk2.unpack(sv)
    res.update({s[0]: (s[1], sd[i], sm[i], sv[i]) for i, s in enumerate(small_names)})
    with _after(sd[0]):
        res["w_in"] = _adam_big(r_in, w_in[0], m_w_in[0], v_w_in[0], "adam_w_in")
    for k in ("w_in", "w_branch_a", "w_branch_b", "w_out", "ffn_w1", "ffn_w3", "ffn_w2", "ada_w"):
        res[k] = tuple(a[None] for a in res[k])

    order = ["c_ctx", "ada_w", "ada_b", "norm1_g", "norm2_g", "w_in", "hgrn_lb_logits", "hgrn_norm_g", "na_q_norm_g", "na_k_norm_g",
             "na_rel_bias", "w_branch_a", "w_branch_b", "w_out", "ffn_w1", "ffn_w3", "ffn_conv_w", "ffn_conv_b", "ffn_w2"]
    shapes = {"c_ctx": c_ctx.shape, "ada_b": ada_b.shape, "norm1_g": norm1_g.shape, "norm2_g": norm2_g.shape,
              "hgrn_lb_logits": hgrn_lb_logits.shape, "hgrn_norm_g": hgrn_norm_g.shape, "na_q_norm_g": na_q_norm_g.shape,
              "na_k_norm_g": na_k_norm_g.shape, "na_rel_bias": na_rel_bias.shape, "ffn_conv_w": ffn_conv_w.shape, "ffn_conv_b": ffn_conv_b.shape}
    outs = [loss, grad_x]
    for part in range(4):
        for k in order:
            a = res[k][part]
            outs.append(a.reshape(shapes[k]) if k in shapes else a)
    return tuple(outs)
```

```python
import functools

import numpy as np
import jax
import jax.numpy as jnp
from jax import lax
from jax.experimental import pallas as pl
from jax.experimental.pallas import tpu as pltpu
from jax.experimental.pallas import tpu_sc as plsc

F32 = jnp.float32
BF16 = jnp.bfloat16
HIGHEST = lax.Precision.HIGHEST

N_DEV = 8
MESH_ID = pl.DeviceIdType.MESH
LANE = 128
HEAD_DIM = 128
N_HEADS = 8
GRID_W = 64
WIN_R = 8
WIN_C = 16
ROPE_THETA = 10000.0
EPS = 1e-6
N_MOD = 6
HGRN_BLOCK = 16
NEG_BIG = -1e30
VMEM_LIMIT = 56 << 20

ADAM_LR = 0.001
ADAM_B1 = 0.9
ADAM_B2 = 0.999
ADAM_EPS = 1e-08
ADAM_WD = 0.01
ADAM_STEP = 10

HBM_SPEC = pl.BlockSpec(memory_space=pltpu.HBM)


_ORDER_AFTER = []


class _after:
    def __init__(self, *arrs):
        self.arrs = list(arrs)

    def __enter__(self):
        _ORDER_AFTER.extend(self.arrs)

    def __exit__(self, *exc):
        del _ORDER_AFTER[:]


def _pcall(body, *, name, out_shape, grid=None, in_specs=None, out_specs=None, scratch=(), aliases=None):
    kw = {}
    if grid is not None:
        kw["grid"] = grid
    extra = []
    if _ORDER_AFTER and in_specs is not None:
        extra = list(_ORDER_AFTER)
        del _ORDER_AFTER[:]
        n_in, n_extra, inner = len(in_specs), len(extra), body
        in_specs = list(in_specs) + [pl.BlockSpec(memory_space=pl.ANY)] * n_extra

        def body(*refs):
            return inner(*refs[:n_in], *refs[n_in + n_extra:])

    if extra:
        call = _pcall_inner(body, name, out_shape, kw, in_specs, out_specs, scratch, aliases)
        return lambda *args: call(*args, *extra)
    return _pcall_inner(body, name, out_shape, kw, in_specs, out_specs, scratch, aliases)


def _pcall_inner(body, name, out_shape, kw, in_specs, out_specs, scratch, aliases):
    if in_specs is not None:
        kw["in_specs"] = in_specs
    if out_specs is not None:
        kw["out_specs"] = out_specs
    if scratch:
        kw["scratch_shapes"] = list(scratch)
    if aliases:
        kw["input_output_aliases"] = aliases
    return pl.pallas_call(body, name=name, out_shape=out_shape,
                          compiler_params=pltpu.CompilerParams(vmem_limit_bytes=VMEM_LIMIT), **kw)


def _pick(dim, cands):
    for c in cands:
        if c <= dim and dim % c == 0:
            return c
    return dim


def _sds(shape, dtype):
    return jax.ShapeDtypeStruct(tuple(shape), dtype)


def _peers():
    x, y, c = lax.axis_index("x"), lax.axis_index("y"), lax.axis_index("c")
    out = []
    for k in range(1, N_DEV):
        px = 1 - x if (k >> 2) & 1 else x
        py = 1 - y if (k >> 1) & 1 else y
        pc = 1 - c if k & 1 else c
        out.append((k, (px, py, pc), 4 * px + 2 * py + pc))
    return 4 * x + 2 * y + c, out


def _exchange(arrs, name, scatter):
    n = len(arrs)

    def body(*refs):
        ins, outs = refs[:n], refs[n:2 * n]
        send, recv, loc = refs[2 * n:]
        me, peers = _peers()
        started = []
        for i in range(n):
            src = ins[i].at[me] if scatter else ins[i]
            cp = pltpu.make_async_copy(src, outs[i].at[me], loc.at[i])
            cp.start()
            started.append(cp)
        sends = []
        for k, peer, pidx in peers:
            for i in range(n):
                src = ins[i].at[pidx] if scatter else ins[i]
                cp = pltpu.make_async_remote_copy(src_ref=src, dst_ref=outs[i].at[me], send_sem=send.at[i * 7 + k - 1],
                                                  recv_sem=recv.at[i * 7 + k - 1], device_id=peer, device_id_type=MESH_ID)
                cp.start()
                sends.append(cp)
        for k, peer, pidx in peers:
            for i in range(n):
                src = ins[i].at[pidx] if scatter else ins[i]
                pltpu.make_async_remote_copy(src_ref=src, dst_ref=outs[i].at[pidx], send_sem=send.at[i * 7 + k - 1],
                                             recv_sem=recv.at[i * 7 + k - 1], device_id=peer, device_id_type=MESH_ID).wait_recv()
        for cp in sends:
            cp.wait_send()
        for cp in started:
            cp.wait()

    out_shape = [_sds(a.shape if scatter else (N_DEV,) + a.shape, a.dtype) for a in arrs]
    res = _pcall(body, name=name, out_shape=out_shape, in_specs=[HBM_SPEC] * n, out_specs=[HBM_SPEC] * n,
                 scratch=[pltpu.SemaphoreType.DMA((7 * n,)), pltpu.SemaphoreType.DMA((7 * n,)), pltpu.SemaphoreType.DMA((n,))])(*arrs)
    return list(res)


def _exchange_sc(arrs, name, scatter, collective_id):
    n = len(arrs)
    srcs = [jax.new_ref(a, memory_space=pltpu.MemorySpace.HBM) for a in arrs]
    lands = [jax.empty_ref(_sds(a.shape if scatter else (N_DEV,) + a.shape, a.dtype), memory_space=pltpu.MemorySpace.HBM) for a in arrs]

    @pl.kernel(mesh=plsc.ScalarSubcoreMesh(axis_name="seq", num_cores=1), name=name,
               scratch_types=(pltpu.SemaphoreType.DMA((7 * n,)), pltpu.SemaphoreType.DMA((7 * n,)), pltpu.SemaphoreType.DMA((n,))),
               compiler_params=pltpu.CompilerParams(collective_id=collective_id))
    def launch(send, recv, loc):
        me, peers = _peers()
        barrier = pltpu.get_barrier_semaphore()
        for _, peer, _ in peers:
            pl.semaphore_signal(barrier, inc=1, device_id=peer, device_id_type=MESH_ID)
        pl.semaphore_wait(barrier, N_DEV - 1)
        own = [pltpu.make_async_copy(srcs[i].at[me] if scatter else srcs[i], lands[i].at[me], loc.at[i]) for i in range(n)]
        for cp in own:
            cp.start()
        sends = []
        for k, peer, pidx in peers:
            for i in range(n):
                src = srcs[i].at[pidx] if scatter else srcs[i]
                cp = pltpu.make_async_remote_copy(src_ref=src, dst_ref=lands[i].at[me], send_sem=send.at[i * 7 + k - 1],
                                                  recv_sem=recv.at[i * 7 + k - 1], device_id=peer, device_id_type=MESH_ID)
                cp.start()
                sends.append(cp)
        for k, peer, pidx in peers:
            for i in range(n):
                src = srcs[i].at[pidx] if scatter else srcs[i]
                pltpu.make_async_remote_copy(src_ref=src, dst_ref=lands[i].at[pidx], send_sem=send.at[i * 7 + k - 1],
                                             recv_sem=recv.at[i * 7 + k - 1], device_id=peer, device_id_type=MESH_ID).wait_recv()
        for cp in sends:
            cp.wait_send()
        for cp in own:
            cp.wait()

    launch()
    return [r[...] for r in lands]


def _gather_sc(arrs, name, collective_id):
    n = len(arrs)
    srcs = [jax.new_ref(a, memory_space=pltpu.MemorySpace.HBM) for a in arrs]
    lands = [jax.empty_ref(_sds((N_DEV,) + a.shape, a.dtype), memory_space=pltpu.MemorySpace.HBM) for a in arrs]

    @pl.kernel(mesh=plsc.ScalarSubcoreMesh(axis_name="seq", num_cores=1), name=name,
               scratch_types=(pltpu.SemaphoreType.DMA((7 * n,)), pltpu.SemaphoreType.DMA((7 * n,)), pltpu.SemaphoreType.DMA((n,))),
               compiler_params=pltpu.CompilerParams(collective_id=collective_id))
    def launch(send, recv, loc):
        x, y, c = lax.axis_index("x"), lax.axis_index("y"), lax.axis_index("c")
        me, sibling = (x, y, c), (x, y, 1 - c)
        chips = [(1 - x, y), (x, 1 - y), (1 - x, 1 - y)]
        index = lambda px, py, pc: 4 * px + 2 * py + pc
        barrier = pltpu.get_barrier_semaphore()
        for peer in [sibling] + [(*chip, c) for chip in chips]:
            pl.semaphore_signal(barrier, inc=1, device_id=peer, device_id_type=MESH_ID)
        pl.semaphore_wait(barrier, 4)

        def copy(i, k, block, to, from_src):
            return pltpu.make_async_remote_copy(src_ref=srcs[i] if from_src else lands[i].at[index(*block)], dst_ref=lands[i].at[index(*block)],
                                                send_sem=send.at[i * 7 + k], recv_sem=recv.at[i * 7 + k], device_id=to, device_id_type=MESH_ID)

        own = [pltpu.make_async_copy(srcs[i], lands[i].at[index(*me)], loc.at[i]) for i in range(n)]
        for cp in own:
            cp.start()
        started = []
        for i in range(n):
            started.append(copy(i, 0, me, sibling, True))
            started += [copy(i, 1 + j, me, (*chip, c), True) for j, chip in enumerate(chips)]
        for cp in started:
            cp.start()
        for j, chip in enumerate(chips):
            for i in range(n):
                copy(i, 1 + j, (*chip, c), me, False).wait_recv()
                fwd = copy(i, 4 + j, (*chip, c), sibling, False)
                fwd.start()
                started.append(fwd)
        for i in range(n):
            copy(i, 0, sibling, me, False).wait_recv()
            for j, chip in enumerate(chips):
                copy(i, 4 + j, (*chip, 1 - c), me, False).wait_recv()
        for cp in started:
            cp.wait_send()
        for cp in own:
            cp.wait()

    launch()
    return [r[...] for r in lands]


def _pair_swap_sc(blocks, name, collective_id):
    n = len(blocks)
    srcs = [jax.new_ref(b, memory_space=pltpu.MemorySpace.HBM) for b in blocks]
    lands = [jax.empty_ref(_sds((4,) + b.shape[1:], b.dtype), memory_space=pltpu.MemorySpace.HBM) for b in blocks]

    @pl.kernel(mesh=plsc.ScalarSubcoreMesh(axis_name="seq", num_cores=1), name=name,
               scratch_types=(pltpu.SemaphoreType.DMA((4 * n,)), pltpu.SemaphoreType.DMA((4 * n,))),
               compiler_params=pltpu.CompilerParams(collective_id=collective_id))
    def launch(send, recv):
        x, y, c = lax.axis_index("x"), lax.axis_index("y"), lax.axis_index("c")
        sibling = (x, y, 1 - c)
        barrier = pltpu.get_barrier_semaphore()
        pl.semaphore_signal(barrier, inc=1, device_id=sibling, device_id_type=MESH_ID)
        pl.semaphore_wait(barrier, 1)
        copies = [pltpu.make_async_remote_copy(src_ref=srcs[i].at[2 * q + 1 - c], dst_ref=lands[i].at[q], send_sem=send.at[4 * i + q],
                                               recv_sem=recv.at[4 * i + q], device_id=sibling, device_id_type=MESH_ID)
                  for i in range(n) for q in range(4)]
        for cp in copies:
            cp.start()
        for cp in copies:
            cp.wait()

    launch()
    return [r[...] for r in lands]


def _pair_add(blocks, theirs, name):
    _, r, c_ = blocks.shape
    mine = lax.dynamic_index_in_dim(blocks.reshape(4, 2, r, c_), lax.axis_index("c"), axis=1, keepdims=False)
    tr = _pick(r, (256, 128, 64, 16))
    spec = (None, tr, c_)
    return _tiled(lambda a, b: (a.astype(F32) + b.astype(F32),), [(mine, spec, lambda q, i: (q, i, 0)), (theirs, spec, lambda q, i: (q, i, 0))],
                  [((4, r, c_), BF16, spec, lambda q, i: (q, i, 0))], (4, r // tr), name)[0]


def _chip_scatter_sc(sums, name, collective_id):
    n = len(sums)
    srcs = [jax.new_ref(s, memory_space=pltpu.MemorySpace.HBM) for s in sums]
    lands = [jax.empty_ref(_sds(s.shape, s.dtype), memory_space=pltpu.MemorySpace.HBM) for s in sums]

    @pl.kernel(mesh=plsc.ScalarSubcoreMesh(axis_name="seq", num_cores=1), name=name,
               scratch_types=(pltpu.SemaphoreType.DMA((3 * n,)), pltpu.SemaphoreType.DMA((3 * n,)), pltpu.SemaphoreType.DMA((n,))),
               compiler_params=pltpu.CompilerParams(collective_id=collective_id))
    def launch(send, recv, loc):
        x, y, c = lax.axis_index("x"), lax.axis_index("y"), lax.axis_index("c")
        chips = [(1 - x, y), (x, 1 - y), (1 - x, 1 - y)]
        my_chip = 2 * x + y
        barrier = pltpu.get_barrier_semaphore()
        for chip in chips:
            pl.semaphore_signal(barrier, inc=1, device_id=(*chip, c), device_id_type=MESH_ID)
        pl.semaphore_wait(barrier, 3)
        own = [pltpu.make_async_copy(srcs[i].at[my_chip], lands[i].at[my_chip], loc.at[i]) for i in range(n)]
        for cp in own:
            cp.start()
        sends = [pltpu.make_async_remote_copy(src_ref=srcs[i].at[2 * px + py], dst_ref=lands[i].at[my_chip], send_sem=send.at[3 * i + j],
                                              recv_sem=recv.at[3 * i + j], device_id=(px, py, c), device_id_type=MESH_ID)
                 for j, (px, py) in enumerate(chips) for i in range(n)]
        for cp in sends:
            cp.start()
        for j, (px, py) in enumerate(chips):
            for i in range(n):
                pltpu.make_async_remote_copy(src_ref=srcs[i].at[my_chip], dst_ref=lands[i].at[2 * px + py], send_sem=send.at[3 * i + j],
                                             recv_sem=recv.at[3 * i + j], device_id=(px, py, c), device_id_type=MESH_ID).wait_recv()
        for cp in sends:
            cp.wait_send()
        for cp in own:
            cp.wait()

    launch()
    return [r[...] for r in lands]


class _InFlight:
    def __init__(self, send, recv, srcs, lands, token, scatter):
        self.send, self.recv, self.srcs, self.lands, self.token, self.scatter = send, recv, srcs, lands, token, scatter


SEM_SPEC = pl.BlockSpec(memory_space=pltpu.SEMAPHORE)
SIDE_EFFECT = pltpu.SideEffectType.DATAFLOW_SIDE_EFFECTING


def _exchange_start(arrs, name, scatter):
    n = len(arrs)

    def body(*refs):
        ins, lands = refs[:n], refs[n:2 * n]
        send, recv = refs[2 * n], refs[2 * n + 1]
        token = refs[4 * n + 2]
        loc = refs[4 * n + 3]
        me, peers = _peers()
        own = [pltpu.make_async_copy(ins[i].at[me] if scatter else ins[i], lands[i].at[me], loc.at[i]) for i in range(n)]
        for cp in own:
            cp.start()
        for cp in own:
            cp.wait()
        for k, peer, pidx in peers:
            for i in range(n):
                src = ins[i].at[pidx] if scatter else ins[i]
                pltpu.make_async_remote_copy(src_ref=src, dst_ref=lands[i].at[me], send_sem=send.at[i * 7 + k - 1],
                                             recv_sem=recv.at[i * 7 + k - 1], device_id=peer, device_id_type=MESH_ID).start()
        token[...] = jnp.zeros_like(token)

    land_shapes = [a.shape if scatter else (N_DEV,) + a.shape for a in arrs]
    hbm = lambda a: pltpu.with_memory_space_constraint(a, pltpu.HBM)
    args = [hbm(a) for a in arrs] + [hbm(lax.empty(s, a.dtype)) for s, a in zip(land_shapes, arrs)]
    out_shape = ([pltpu.SemaphoreType.DMA((7 * n,)), pltpu.SemaphoreType.DMA((7 * n,))]
                 + [pltpu.HBM(a.shape, a.dtype) for a in arrs] + [pltpu.HBM(s, a.dtype) for s, a in zip(land_shapes, arrs)]
                 + [_sds((8, LANE), F32)])
    res = pl.pallas_call(
        body, name=name, out_shape=out_shape, in_specs=[HBM_SPEC] * (2 * n),
        out_specs=[SEM_SPEC, SEM_SPEC] + [HBM_SPEC] * (2 * n) + [pl.BlockSpec(memory_space=pltpu.VMEM)],
        input_output_aliases={i: 2 + i for i in range(2 * n)},
        scratch_shapes=[pltpu.SemaphoreType.DMA((n,))],
        compiler_params=pltpu.CompilerParams(has_side_effects=SIDE_EFFECT))(*args)
    return _InFlight(res[0], res[1], list(res[2:2 + n]), list(res[2 + n:2 + 2 * n]), res[2 + 2 * n], scatter)


def _exchange_wait(h, after, name):
    n = len(h.srcs)
    scatter = h.scatter
    after = list(after)

    def body(*refs):
        ins, lands = refs[:n], refs[n:2 * n]
        send, recv = refs[2 * n], refs[2 * n + 1]
        _, peers = _peers()
        for k, peer, pidx in peers:
            for i in range(n):
                src = ins[i].at[pidx] if scatter else ins[i]
                cp = pltpu.make_async_remote_copy(src_ref=src, dst_ref=lands[i].at[pidx], send_sem=send.at[i * 7 + k - 1],
                                                  recv_sem=recv.at[i * 7 + k - 1], device_id=peer, device_id_type=MESH_ID)
                cp.wait_send()
                cp.wait_recv()

    res = pl.pallas_call(
        body, name=name, out_shape=[pltpu.HBM(a.shape, a.dtype) for a in h.srcs + h.lands],
        in_specs=[HBM_SPEC] * (2 * n) + [SEM_SPEC, SEM_SPEC] + [pl.BlockSpec(memory_space=pl.ANY)] * len(after),
        out_specs=[HBM_SPEC] * (2 * n), input_output_aliases={i: i for i in range(2 * n)},
        compiler_params=pltpu.CompilerParams(has_side_effects=SIDE_EFFECT))(*h.srcs, *h.lands, h.send, h.recv, *after)
    return list(res[n:])


def _mm_xw(x, g, out_dtype, name, tm_c=(768, 512, 384, 256, 128, 64, 16), tn_c=(1024, 768, 512, 256, 128), tk_c=(2048, 1024, 768, 512, 256)):
    m, r = x.shape
    nb, r2, cl = g.shape
    assert r == r2
    tm, tn, tk = _pick(m, tm_c), _pick(cl, tn_c), _pick(r, tk_c)
    q, nk = cl // tn, r // tk

    def body(x_ref, g_ref, o_ref, *acc):
        p = lax.dot_general(x_ref[...].astype(BF16), g_ref[...], (((1,), (0,)), ((), ())), preferred_element_type=F32)
        if nk == 1:
            o_ref[...] = p.astype(o_ref.dtype)
        else:
            k = pl.program_id(2)

            @pl.when(k == 0)
            def _():
                acc[0][...] = p

            @pl.when(k > 0)
            def _():
                acc[0][...] += p

            @pl.when(k == nk - 1)
            def _():
                o_ref[...] = acc[0][...].astype(o_ref.dtype)

    return _pcall(
        body, name=name, grid=(m // tm, nb * q, nk),
        in_specs=[pl.BlockSpec((tm, tk), lambda i, j, k: (i, k)), pl.BlockSpec((None, tk, tn), lambda i, j, k: (j // q, k, j % q))],
        out_specs=pl.BlockSpec((tm, tn), lambda i, j, k: (i, j)),
        out_shape=_sds((m, nb * cl), out_dtype),
        scratch=[] if nk == 1 else [pltpu.VMEM((tm, tn), F32)])(x, g)


def _mm_dyw(dy, g, out_dtype, name, init=None, tm_c=(768, 512, 384, 256, 128), tn_c=(1024, 512, 256, 128), tk_c=(2048, 1536, 1024, 768, 512, 256, 128)):
    m, n = dy.shape
    nb, r, cl = g.shape
    assert n == nb * cl
    tm, tn, tk = _pick(m, tm_c), _pick(r, tn_c), _pick(cl, tk_c)
    q = cl // tk
    nk = nb * q
    has_init = init is not None

    def body(*refs):
        if has_init:
            dy_ref, g_ref, i_ref, o_ref, acc = refs
        else:
            dy_ref, g_ref, o_ref, acc = refs
        k = pl.program_id(2)
        p = lax.dot_general(dy_ref[...].astype(BF16), g_ref[...], (((1,), (1,)), ((), ())), preferred_element_type=F32)

        @pl.when(k == 0)
        def _():
            acc[...] = p + i_ref[...].astype(F32) if has_init else p

        @pl.when(k > 0)
        def _():
            acc[...] += p

        @pl.when(k == nk - 1)
        def _():
            o_ref[...] = acc[...].astype(o_ref.dtype)

    in_specs = [pl.BlockSpec((tm, tk), lambda i, j, k: (i, k)), pl.BlockSpec((None, tn, tk), lambda i, j, k: (k // q, j, k % q))]
    args = [dy, g]
    if has_init:
        in_specs.append(pl.BlockSpec((tm, tn), lambda i, j, k: (i, j)))
        args.append(init)
    return _pcall(body, name=name, grid=(m // tm, r // tn, nk), in_specs=in_specs,
                  out_specs=pl.BlockSpec((tm, tn), lambda i, j, k: (i, j)), out_shape=_sds((m, r), out_dtype),
                  scratch=[pltpu.VMEM((tm, tn), F32)])(*args)


def _mm_xtdy(x, dy, nb, out_dtype, name, tm_c=(1024, 512, 256, 128), tn_c=(768, 512, 256, 128), tk_c=(2304, 2048, 1152, 1024, 768, 512, 256, 128, 16)):
    t, r = x.shape
    t2, n = dy.shape
    assert t == t2 and n % nb == 0
    cl = n // nb
    tm, tn, tk = _pick(r, tm_c), _pick(cl, tn_c), _pick(t, tk_c)
    q, nk = cl // tn, t // tk

    def body(x_ref, dy_ref, o_ref, *acc):
        p = lax.dot_general(x_ref[...].astype(BF16), dy_ref[...].astype(BF16), (((0,), (0,)), ((), ())), preferred_element_type=F32)
        if nk == 1:
            o_ref[...] = p.astype(o_ref.dtype)
        else:
            k = pl.program_id(2)

            @pl.when(k == 0)
            def _():
                acc[0][...] = p

            @pl.when(k > 0)
            def _():
                acc[0][...] += p

            @pl.when(k == nk - 1)
            def _():
                o_ref[...] = acc[0][...].astype(o_ref.dtype)

    return _pcall(
        body, name=name, grid=(r // tm, nb * q, nk),
        in_specs=[pl.BlockSpec((tk, tm), lambda i, j, k: (k, i)), pl.BlockSpec((tk, tn), lambda i, j, k: (k, j))],
        out_specs=pl.BlockSpec((None, tm, tn), lambda i, j, k: (j // q, i, j % q)),
        out_shape=_sds((nb, r, cl), out_dtype),
        scratch=[] if nk == 1 else [pltpu.VMEM((tm, tn), F32)])(x, dy)


def _mm_f32(a, b, name, trans_b=False):
    dims = (((1,), (1,)), ((), ())) if trans_b else (((1,), (0,)), ((), ()))
    n = b.shape[0] if trans_b else b.shape[1]

    def body(a_ref, b_ref, o_ref):
        o_ref[...] = lax.dot_general(a_ref[...], b_ref[...], dims, precision=HIGHEST, preferred_element_type=F32)

    return _pcall(body, name=name, out_shape=_sds((a.shape[0], n), F32))(a, b)


def _tiled(fn, ins, outs, grid, name, acc=()):
    n_in = len(ins)
    grid = tuple(grid) or (1,)
    nd = len(grid)

    def body(*refs):
        vals = fn(*[r[...] for r in refs[:n_in]])
        if not isinstance(vals, (tuple, list)):
            vals = (vals,)
        first = None
        for o, (ref, v) in enumerate(zip(refs[n_in:], vals)):
            if o in acc:
                if first is None:
                    first = pl.program_id(0) == 0
                    for a in range(1, nd):
                        first = jnp.logical_and(first, pl.program_id(a) == 0)

                @pl.when(first)
                def _(ref=ref, v=v):
                    ref[...] = v.astype(ref.dtype)

                @pl.when(jnp.logical_not(first))
                def _(ref=ref, v=v):
                    ref[...] += v.astype(ref.dtype)
            else:
                ref[...] = v.astype(ref.dtype)

    res = _pcall(body, name=name, grid=grid,
                 in_specs=[pl.BlockSpec(b, im) for _, b, im in ins],
                 out_specs=[pl.BlockSpec(b, im) for _, _, b, im in outs],
                 out_shape=[_sds(s, d) for s, d, _, _ in outs])(*[a for a, _, _ in ins])
    return list(res)


def _rows(a, tr):
    return (a, (tr, a.shape[1]), lambda i, *_: (i, 0))


def _const(a):
    nd = a.ndim
    return (a, a.shape, lambda *_: (0,) * nd)


def _cast_bf16(w, name):
    r, c = w.shape
    tr = _pick(r, (256, 128, 64, 16))
    return _tiled(lambda v: v, [_rows(w, tr)], [((r, c), BF16, (tr, c), lambda i: (i, 0))], (r // tr,), name)[0]


def _f_norm_mod(x, g, sh, sc):
    y = x * lax.rsqrt(jnp.mean(x * x, axis=-1, keepdims=True) + EPS) * g
    return y * (1.0 + sc) + sh


def _rope_partner_impl(y):
    nf = HEAD_DIM // 4
    lane = lax.broadcasted_iota(jnp.int32, y.shape, 1)
    return jnp.where(lane % (2 * nf) < nf, pltpu.roll(y, HEAD_DIM - nf, 1), pltpu.roll(y, nf, 1))


_rope_partner = jax.custom_vjp(_rope_partner_impl)
_rope_partner.defvjp(lambda y: (_rope_partner_impl(y), None), lambda _, ct: (_rope_partner_impl(ct),))


def _f_qk(t, g, cos, sin):
    y = t * lax.rsqrt(jnp.mean(t * t, axis=-1, keepdims=True) + EPS) * g
    return y * cos + _rope_partner(y) * sin


def _f_readout(of, ob, gate, g):
    o = of + ob
    on = o * lax.rsqrt(jnp.mean(o * o, axis=-1, keepdims=True) + EPS) * g
    return on * (gate * jax.nn.sigmoid(gate))


def _f_merge(ga, gb, pa, pb):
    return jax.nn.sigmoid(ga) * pa + jax.nn.sigmoid(gb) * pb


def _f_resid(x, gate, m):
    return x + gate * m


def _norm_mod_fwd(xcat, g, mods, n_ctx_tiles, tr):
    tt, d = xcat.shape
    which = lambda i: (jnp.where(i >= n_ctx_tiles, 1, 0), 0, 0, 0)

    def fn(x, gg, md):
        return _f_norm_mod(x, gg, md[0], md[1])

    return _tiled(fn, [_rows(xcat, tr), _const(g), (mods, (None, 2, 1, d), which)],
                  [((tt, d), BF16, (tr, d), lambda i: (i, 0))], (tt // tr,), "norm_mod_fwd")[0]


def _norm_mod_bwd(xcat, g, mods, dh, extra, n_ctx_tiles, tr, name):
    tt, d = xcat.shape
    nt = tt // tr
    has_extra = extra is not None

    def body(*refs):
        if has_extra:
            x_ref, g_ref, m_ref, dh_ref, e_ref, dx_ref, dg_ref, dm_ref = refs
        else:
            x_ref, g_ref, m_ref, dh_ref, dx_ref, dg_ref, dm_ref = refs
        i = pl.program_id(0)
        md = m_ref[...]
        _, vjp = jax.vjp(_f_norm_mod, x_ref[...], g_ref[...], md[0], md[1])
        dx, dg, dsh, dsc = vjp(dh_ref[...].astype(F32))
        dx_ref[...] = dx + e_ref[...] if has_extra else dx

        @pl.when(i == 0)
        def _():
            dg_ref[...] = dg

        @pl.when(i > 0)
        def _():
            dg_ref[...] += dg

        fresh = jnp.logical_or(i == 0, i == n_ctx_tiles)

        @pl.when(fresh)
        def _():
            dm_ref[0] = dsh
            dm_ref[1] = dsc

        @pl.when(jnp.logical_not(fresh))
        def _():
            dm_ref[0] += dsh
            dm_ref[1] += dsc

    which = lambda i: (jnp.where(i >= n_ctx_tiles, 1, 0), 0, 0, 0)
    row = pl.BlockSpec((tr, d), lambda i: (i, 0))
    in_specs = [row, pl.BlockSpec((1, d), lambda i: (0, 0)), pl.BlockSpec((None, 2, 1, d), which), row]
    args = [xcat, g, mods, dh]
    if has_extra:
        in_specs.append(row)
        args.append(extra)
    return _pcall(body, name=name, grid=(nt,), in_specs=in_specs,
                  out_specs=[row, pl.BlockSpec((1, d), lambda i: (0, 0)), pl.BlockSpec((None, 2, 1, d), which)],
                  out_shape=[_sds((tt, d), F32), _sds((1, d), F32), _sds((2, 2, 1, d), F32)])(*args)


def _hgrn_tri(reverse):
    t = np.arange(HGRN_BLOCK)
    tri = (t[None, :] >= t[:, None]) if reverse else (t[None, :] <= t[:, None])
    tri = tri.astype(np.float32)
    return jnp.asarray(tri), jnp.asarray(tri.T.copy())


def _hgrn_rowblock(n, n_ctx_blocks, n_blocks, reverse):
    if not reverse:
        return n
    return jnp.where(n < n_ctx_blocks, n_ctx_blocks - 1 - n, n_blocks - 1 - n + n_ctx_blocks)


def _hgrn_gates(fl, lb):
    sg = jax.nn.sigmoid(fl)
    f = lb + (1.0 - lb) * sg
    return sg, f, jnp.log(f), 1.0 - f


def _hgrn_intra_mask(reverse):
    tio = lax.broadcasted_iota(jnp.int32, (HGRN_BLOCK, HEAD_DIM), 0)
    return (lambda s: tio <= s) if reverse else (lambda s: tio >= s)


class _Halves:
    def __init__(self, reverse):
        self.reverse = reverse
        self.h = HGRN_BLOCK // 2
        tio = lax.broadcasted_iota(jnp.int32, (self.h, HEAD_DIM), 0)
        self.tio = tio if reverse else tio + self.h

    def is_half(self, s):
        return s < self.h if self.reverse else s >= self.h

    def rows(self, a):
        return a[:self.h] if self.reverse else a[self.h:]

    def mask(self, s):
        return self.tio <= s if self.reverse else self.tio >= s

    def widen(self, full, half):
        z = jnp.zeros_like(half)
        return full + jnp.concatenate([half, z] if self.reverse else [z, half], axis=0)


def _unrolled(nblk, u, fn, init):
    assert nblk % u == 0

    def trip(b, c):
        for j in range(u):
            c = fn(b * u + j, j, c)
        return c

    return lax.fori_loop(0, nblk // u, trip, init)


HGRN_UNROLL_FWD = (4, 8)
HGRN_UNROLL_BWD = (8, 4)


def _hgrn_fwd(p, lb, seg_f, reverse, n_ctx_rows):
    tt = p.shape[0]
    hb = HGRN_BLOCK
    nblk, nctx = tt // hb, n_ctx_rows // hb
    tri, _ = _hgrn_tri(reverse)
    u1, u3 = HGRN_UNROLL_FWD

    def body(q_ref, f_ref, v_ref, lb_ref, tri_ref, o_ref, st_all, dec_all, qe_buf, cum_blk, k_blk, v_blk):
        mask = _hgrn_intra_mask(reverse)
        hv = _Halves(reverse)
        lbv = lb_ref[...]

        def phase1(n, slot, c):
            r0 = pl.multiple_of(_hgrn_rowblock(n, nctx, nblk, reverse) * hb, hb)
            q, v = q_ref[pl.ds(r0, hb), :], v_ref[pl.ds(r0, hb), :]
            _, f, g, k = _hgrn_gates(f_ref[pl.ds(r0, hb), :], lbv)
            cum = jnp.dot(tri_ref[...], g, precision=HIGHEST, preferred_element_type=F32)
            tot = jnp.sum(g, axis=0, keepdims=True)
            cum_blk[slot] = cum
            k_blk[slot] = k
            v_blk[slot] = v
            oi = jnp.zeros((hb, HEAD_DIM), F32)
            oi_h = jnp.zeros((hb // 2, HEAD_DIM), F32)
            q_h, cum_h = hv.rows(q), hv.rows(cum)
            for s in range(hb):
                if hv.is_half(s):
                    e = jnp.where(hv.mask(s), jnp.exp(jnp.minimum(cum_h - cum_blk[slot, s:s + 1, :], 0.0)), 0.0)
                    a_s = jnp.sum(q_h * e * k_blk[slot, s:s + 1, :], axis=-1, keepdims=True)
                    oi_h = oi_h + a_s * v_blk[slot, s:s + 1, :]
                else:
                    e = jnp.where(mask(s), jnp.exp(jnp.minimum(cum - cum_blk[slot, s:s + 1, :], 0.0)), 0.0)
                    a_s = jnp.sum(q * e * k_blk[slot, s:s + 1, :], axis=-1, keepdims=True)
                    oi = oi + a_s * v_blk[slot, s:s + 1, :]
            o_ref[pl.ds(r0, hb), :] = hv.widen(oi, oi_h)
            qe_buf[pl.ds(r0, hb), :] = q * jnp.exp(cum)
            kl = k * jnp.exp(tot - cum)
            st_all[n] = lax.dot_general(v.astype(BF16), kl.astype(BF16), (((0,), (0,)), ((), ())), preferred_element_type=F32)
            dec_all[pl.ds(n, 1), :] = jnp.exp(tot)
            return c

        _unrolled(nblk, u1, phase1, 0)

        def phase2(n, st):
            kv = st_all[n]
            st_all[n] = st
            return st * dec_all[pl.ds(n, 1), :] + kv

        lax.fori_loop(0, nblk, phase2, jnp.zeros((HEAD_DIM, HEAD_DIM), F32))

        def phase3(n, slot, c):
            r0 = pl.multiple_of(_hgrn_rowblock(n, nctx, nblk, reverse) * hb, hb)
            o_ref[pl.ds(r0, hb), :] += lax.dot_general(qe_buf[pl.ds(r0, hb), :].astype(BF16), st_all[n].astype(BF16),
                                                        (((1,), (1,)), ((), ())), preferred_element_type=F32)
            return c

        _unrolled(nblk, u3, phase3, 0)

    col = lambda seg: pl.BlockSpec((tt, HEAD_DIM), lambda h, seg=seg: (0, seg * N_HEADS + h))
    blk = pltpu.VMEM((u1, hb, HEAD_DIM), F32)
    return _pcall(
        body, name="hgrn_fwd_rev" if reverse else "hgrn_fwd", grid=(N_HEADS,),
        in_specs=[col(0), col(seg_f), col(3), pl.BlockSpec((1, HEAD_DIM), lambda h: (0, h)), pl.BlockSpec((hb, hb), lambda h: (0, 0))],
        out_specs=pl.BlockSpec((tt, HEAD_DIM), lambda h: (0, h)),
        out_shape=_sds((tt, N_HEADS * HEAD_DIM), F32),
        scratch=[pltpu.VMEM((nblk, HEAD_DIM, HEAD_DIM), F32), pltpu.VMEM((nblk, HEAD_DIM), F32), pltpu.VMEM((tt, HEAD_DIM), F32),
                 blk, blk, blk])(p, p, p, lb, tri)


def _hgrn_bwd(p, lb, do, seg_f, reverse, n_ctx_rows, prev):
    tt = p.shape[0]
    hb = HGRN_BLOCK
    nblk, nctx = tt // hb, n_ctx_rows // hb
    tri, tri_t = _hgrn_tri(reverse)
    last_row = 0 if reverse else hb - 1
    has_prev = prev is not None
    u1, u3 = HGRN_UNROLL_BWD

    def body(*refs):
        q_ref, f_ref, v_ref, lb_ref, tri_ref, trit_ref, do_ref = refs[:7]
        refs = refs[7:]
        if has_prev:
            pq_ref, pv_ref = refs[:2]
            refs = refs[2:]
        dq_ref, dfl_ref, dv_ref, dlb_ref, st_all, dd_all, dec_all, cum_buf, cum_blk, k_blk, v_blk, dk_blk, dv_blk = refs
        mask = _hgrn_intra_mask(reverse)
        hv = _Halves(reverse)
        lbv = lb_ref[...]
        tio = lax.broadcasted_iota(jnp.int32, (hb, HEAD_DIM), 0)

        def rows_of(n):
            rb = _hgrn_rowblock(n, nctx, nblk, reverse)
            return rb, pl.multiple_of(rb * hb, hb)

        def load_do(rb):
            lat0 = pl.multiple_of(jnp.maximum(rb - nctx, 0) * hb, hb)
            return jnp.where(rb >= nctx, do_ref[pl.ds(lat0, hb), :], 0.0)

        def phase1(n, slot, c):
            rb, r0 = rows_of(n)
            q, v = q_ref[pl.ds(r0, hb), :], v_ref[pl.ds(r0, hb), :]
            _, f, g, k = _hgrn_gates(f_ref[pl.ds(r0, hb), :], lbv)
            cum = jnp.dot(tri_ref[...], g, precision=HIGHEST, preferred_element_type=F32)
            tot = jnp.sum(g, axis=0, keepdims=True)
            cum_buf[pl.ds(r0, hb), :] = cum
            kl = k * jnp.exp(tot - cum)
            st_all[n] = lax.dot_general(v.astype(BF16), kl.astype(BF16), (((0,), (0,)), ((), ())), preferred_element_type=F32)
            dec_all[pl.ds(n, 1), :] = jnp.exp(tot)
            qe = q * jnp.exp(cum)
            dd_all[n] = lax.dot_general(load_do(rb).astype(BF16), qe.astype(BF16), (((0,), (0,)), ((), ())), preferred_element_type=F32)
            return c

        _unrolled(nblk, u1, phase1, 0)

        def phase2(n, st):
            kv = st_all[n]
            st_all[n] = st
            return st * dec_all[pl.ds(n, 1), :] + kv

        lax.fori_loop(0, nblk, phase2, jnp.zeros((HEAD_DIM, HEAD_DIM), F32))

        def phase2r(i, dst):
            n = nblk - 1 - i
            u = dd_all[n]
            dd_all[n] = dst
            return u + dst * dec_all[pl.ds(n, 1), :]

        lax.fori_loop(0, nblk, phase2r, jnp.zeros((HEAD_DIM, HEAD_DIM), F32))

        def phase3(n, slot, dlb):
            rb, r0 = rows_of(n)
            q, v = q_ref[pl.ds(r0, hb), :], v_ref[pl.ds(r0, hb), :]
            sg, f, g, k = _hgrn_gates(f_ref[pl.ds(r0, hb), :], lbv)
            cum = cum_buf[pl.ds(r0, hb), :]
            tot = jnp.sum(g, axis=0, keepdims=True)
            dob = load_do(rb)
            st, dst = st_all[n], dd_all[n]
            e_cum = jnp.exp(cum)
            e_rest = jnp.exp(tot - cum)
            dq = jnp.dot(dob.astype(BF16), st.astype(BF16), preferred_element_type=F32) * e_cum
            dk_inter = jnp.dot(v.astype(BF16), dst.astype(BF16), preferred_element_type=F32) * e_rest
            dv = lax.dot_general((k * e_rest).astype(BF16), dst.astype(BF16), (((1,), (1,)), ((), ())), preferred_element_type=F32)
            cum_blk[slot] = cum
            k_blk[slot] = k
            v_blk[slot] = v
            dq_h = jnp.zeros((hb // 2, HEAD_DIM), F32)
            q_h, cum_h, dob_h = hv.rows(q), hv.rows(cum), hv.rows(dob)
            for s in range(hb):
                half = hv.is_half(s)
                qs, cs, ds = (q_h, cum_h, dob_h) if half else (q, cum, dob)
                e = jnp.where(hv.mask(s) if half else mask(s), jnp.exp(jnp.minimum(cs - cum_blk[slot, s:s + 1, :], 0.0)), 0.0)
                a_s = jnp.sum(qs * e * k_blk[slot, s:s + 1, :], axis=-1, keepdims=True)
                da_s = jnp.sum(ds * v_blk[slot, s:s + 1, :], axis=-1, keepdims=True)
                gs = da_s * e
                if half:
                    dq_h = dq_h + gs * k_blk[slot, s:s + 1, :]
                else:
                    dq = dq + gs * k_blk[slot, s:s + 1, :]
                dk_blk[slot, s:s + 1, :] = jnp.sum(gs * qs, axis=0, keepdims=True)
                dv_blk[slot, s:s + 1, :] = jnp.sum(a_s * ds, axis=0, keepdims=True)
            dq = hv.widen(dq, dq_h)
            dk = dk_inter + dk_blk[slot]
            dv = dv + dv_blk[slot]
            d_tot = jnp.sum(k * dk_inter, axis=0, keepdims=True) + jnp.exp(tot) * jnp.sum(dst * st, axis=0, keepdims=True)
            dcum = q * dq - k * dk + jnp.where(tio == last_row, d_tot, 0.0)
            dg = jnp.dot(trit_ref[...], dcum, precision=HIGHEST, preferred_element_type=F32)
            df = dg / f - dk
            if has_prev:
                dq = dq + pq_ref[pl.ds(r0, hb), :].astype(F32)
                dv = dv + pv_ref[pl.ds(r0, hb), :].astype(F32)
            dq_ref[pl.ds(r0, hb), :] = dq.astype(dq_ref.dtype)
            dv_ref[pl.ds(r0, hb), :] = dv.astype(dv_ref.dtype)
            dfl_ref[pl.ds(r0, hb), :] = (df * (1.0 - lbv) * sg * (1.0 - sg)).astype(dfl_ref.dtype)
            return dlb + jnp.sum(df * (1.0 - sg), axis=0, keepdims=True)

        dlb_ref[...] = _unrolled(nblk, u3, phase3, jnp.zeros((1, HEAD_DIM), F32))

    col = lambda seg: pl.BlockSpec((tt, HEAD_DIM), lambda h, seg=seg: (0, seg * N_HEADS + h))
    head = pl.BlockSpec((tt, HEAD_DIM), lambda h: (0, h))
    lbs = pl.BlockSpec((1, HEAD_DIM), lambda h: (0, h))
    tris = pl.BlockSpec((hb, hb), lambda h: (0, 0))
    in_specs = [col(0), col(seg_f), col(3), lbs, tris, tris, pl.BlockSpec((do.shape[0], HEAD_DIM), lambda h: (0, h))]
    args = [p, p, p, lb, tri, tri_t, do]
    mid = F32 if not has_prev else BF16
    if has_prev:
        in_specs += [head, head]
        args += list(prev)
    w = N_HEADS * HEAD_DIM
    blk = pltpu.VMEM((u3, hb, HEAD_DIM), F32)
    return _pcall(
        body, name="hgrn_bwd_rev" if reverse else "hgrn_bwd", grid=(N_HEADS,), in_specs=in_specs,
        out_specs=[head, head, head, lbs],
        out_shape=[_sds((tt, w), mid), _sds((tt, w), BF16), _sds((tt, w), mid), _sds((1, w), F32)],
        scratch=[pltpu.VMEM((nblk, HEAD_DIM, HEAD_DIM), F32), pltpu.VMEM((nblk, HEAD_DIM, HEAD_DIM), F32), pltpu.VMEM((nblk, HEAD_DIM), F32),
                 pltpu.VMEM((tt, HEAD_DIM), F32), blk, blk, blk, blk, blk])(*args)


NA_HEADS_PER_STEP = 4


def _na_geometry(rows):
    r = pl.program_id(1)
    rs = jnp.clip(r - WIN_R // 2, 0, rows - WIN_R)
    return r, rs, r - rs


def _na_scores(q, kb, kc, bias):
    scale = HEAD_DIM ** -0.5
    nt = (((1,), (1,)), ((), ()))
    sb = lax.dot_general(q, kb, nt, preferred_element_type=F32) * scale + bias
    sc = lax.dot_general(q, kc, nt, preferred_element_type=F32) * scale
    m = jnp.maximum(jnp.max(sb, axis=-1, keepdims=True), jnp.max(sc, axis=-1, keepdims=True))
    pb, pc = jnp.exp(sb - m), jnp.exp(sc - m)
    inv = 1.0 / (jnp.sum(pb, axis=-1, keepdims=True) + jnp.sum(pc, axis=-1, keepdims=True))
    return pb * inv, pc * inv


def _na_fwd(qn, kall, p, seg_v, bias, n_ctx_rows):
    t, tt = qn.shape[0], kall.shape[0]
    rows = t // GRID_W
    nband = WIN_R * GRID_W

    nh = NA_HEADS_PER_STEP
    wide = nh * HEAD_DIM

    def body(q_ref, k_ref, v_ref, b_ref, o_ref):
        r, rs, _ = _na_geometry(rows)
        k0 = pl.multiple_of(n_ctx_rows + rs * GRID_W, GRID_W)
        for j in range(nh):
            sl = slice(j * HEAD_DIM, (j + 1) * HEAD_DIM)
            q = q_ref[:, sl]
            kb, kc = k_ref[pl.ds(k0, nband), sl], k_ref[pl.ds(0, n_ctx_rows), sl]
            vb, vc = v_ref[pl.ds(k0, nband), sl].astype(BF16), v_ref[pl.ds(0, n_ctx_rows), sl].astype(BF16)
            pb, pc = _na_scores(q, kb, kc, b_ref[j])
            o = jnp.dot(pb.astype(BF16), vb, preferred_element_type=F32) + jnp.dot(pc.astype(BF16), vc, preferred_element_type=F32)
            o_ref[:, sl] = o.astype(o_ref.dtype)

    variant = lambda h, r: (h, r - jnp.clip(r - WIN_R // 2, 0, rows - WIN_R), 0, 0)
    return _pcall(
        body, name="na_fwd", grid=(N_HEADS // nh, rows),
        in_specs=[pl.BlockSpec((GRID_W, wide), lambda h, r: (r, h)),
                  pl.BlockSpec((tt, wide), lambda h, r: (0, h)),
                  pl.BlockSpec((tt, wide), lambda h, r: (0, seg_v * (N_HEADS // nh) + h)),
                  pl.BlockSpec((nh, None, GRID_W, nband), variant)],
        out_specs=pl.BlockSpec((GRID_W, wide), lambda h, r: (r, h)),
        out_shape=_sds((t, N_HEADS * HEAD_DIM), BF16))(qn, kall, p, bias)


def _na_bwd(qn, kall, p, seg_v, bias, do, n_ctx_rows):
    t, tt = qn.shape[0], kall.shape[0]
    rows = t // GRID_W
    nband = WIN_R * GRID_W
    scale = HEAD_DIM ** -0.5
    tn = (((0,), (0,)), ((), ()))
    nt = (((1,), (1,)), ((), ()))

    nh = NA_HEADS_PER_STEP
    wide = nh * HEAD_DIM

    def body(q_ref, k_ref, v_ref, b_ref, do_ref, dq_ref, dk_ref, dv_ref, db_ref, dv_acc):
        r, rs, var = _na_geometry(rows)
        k0 = pl.multiple_of(n_ctx_rows + rs * GRID_W, GRID_W)
        fresh = jnp.logical_or(r <= WIN_R // 2, r > rows - WIN_R // 2)

        @pl.when(r == 0)
        def _():
            dk_ref[...] = jnp.zeros_like(dk_ref)
            dv_acc[...] = jnp.zeros_like(dv_acc)

        for j in range(nh):
            sl = slice(j * HEAD_DIM, (j + 1) * HEAD_DIM)
            q = q_ref[:, sl]
            kb, kc = k_ref[pl.ds(k0, nband), sl], k_ref[pl.ds(0, n_ctx_rows), sl]
            vb, vc = v_ref[pl.ds(k0, nband), sl].astype(BF16), v_ref[pl.ds(0, n_ctx_rows), sl].astype(BF16)
            pb, pc = _na_scores(q, kb, kc, b_ref[j])
            dof = do_ref[:, sl].astype(F32)
            dob = dof.astype(BF16)
            o = jnp.dot(pb.astype(BF16), vb, preferred_element_type=F32) + jnp.dot(pc.astype(BF16), vc, preferred_element_type=F32)
            delta = jnp.sum(dof * o, axis=-1, keepdims=True)
            dsb = pb * (lax.dot_general(dob, vb, nt, preferred_element_type=F32) - delta)
            dsc = pc * (lax.dot_general(dob, vc, nt, preferred_element_type=F32) - delta)
            dsb16, dsc16 = dsb.astype(BF16), dsc.astype(BF16)
            dq_ref[:, sl] = (jnp.dot(dsb16, kb, preferred_element_type=F32) + jnp.dot(dsc16, kc, preferred_element_type=F32)) * scale
            dk_ref[pl.ds(k0, nband), sl] += lax.dot_general(dsb16, q, tn, preferred_element_type=F32) * scale
            dk_ref[pl.ds(0, n_ctx_rows), sl] += lax.dot_general(dsc16, q, tn, preferred_element_type=F32) * scale
            dv_acc[pl.ds(k0, nband), sl] += lax.dot_general(pb.astype(BF16), dob, tn, preferred_element_type=F32)
            dv_acc[pl.ds(0, n_ctx_rows), sl] += lax.dot_general(pc.astype(BF16), dob, tn, preferred_element_type=F32)

            @pl.when(fresh)
            def _(j=j, dsb=dsb):
                db_ref[j] = dsb

            @pl.when(jnp.logical_not(fresh))
            def _(j=j, dsb=dsb):
                db_ref[j] += dsb

        @pl.when(r == rows - 1)
        def _():
            dv_ref[...] = dv_acc[...].astype(dv_ref.dtype)

    variant = lambda h, r: (h, r - jnp.clip(r - WIN_R // 2, 0, rows - WIN_R), 0, 0)
    head_all = pl.BlockSpec((tt, wide), lambda h, r: (0, h))
    qspec = pl.BlockSpec((GRID_W, wide), lambda h, r: (r, h))
    w = N_HEADS * HEAD_DIM
    return _pcall(
        body, name="na_bwd", grid=(N_HEADS // nh, rows),
        in_specs=[qspec, head_all, pl.BlockSpec((tt, wide), lambda h, r: (0, seg_v * (N_HEADS // nh) + h)),
                  pl.BlockSpec((nh, None, GRID_W, nband), variant), qspec],
        out_specs=[qspec, head_all, head_all, pl.BlockSpec((nh, None, GRID_W, nband), variant)],
        out_shape=[_sds((t, w), F32), _sds((tt, w), F32), _sds((tt, w), BF16), _sds((N_HEADS, WIN_R, GRID_W, nband), F32)],
        scratch=[pltpu.VMEM((tt, wide), F32)])(qn, kall, p, bias, do)


def _na_tables(t, n_ctx_rows):
    half, nf = HEAD_DIM // 2, HEAD_DIM // 4
    pos = np.arange(t)
    lane = np.arange(HEAD_DIM)
    inv = ROPE_THETA ** (-(np.arange(nf, dtype=np.float32)) / nf)
    which = np.where(lane < half, pos[:, None] // GRID_W, pos[:, None] % GRID_W).astype(np.float32)
    ang = which * inv[lane % nf][None, :]
    first = (lane % half) < nf
    cos = np.concatenate([np.ones((n_ctx_rows, HEAD_DIM), np.float32), np.cos(ang).astype(np.float32)])
    sin = np.concatenate([np.zeros((n_ctx_rows, HEAD_DIM), np.float32), np.where(first[None, :], -np.sin(ang), np.sin(ang)).astype(np.float32)])
    w = np.arange(GRID_W)
    dc = np.clip(w[None, :] - w[:, None], -(WIN_C - 1), WIN_C - 1) + WIN_C - 1
    onehot = np.zeros((32, GRID_W * GRID_W), np.float32)
    onehot[dc.reshape(-1), np.arange(GRID_W * GRID_W)] = 1.0
    cs = np.clip(w - WIN_C // 2, 0, GRID_W - WIN_C)
    col_in = (w[None, :] >= cs[:, None]) & (w[None, :] < cs[:, None] + WIN_C)
    onehot *= col_in.reshape(1, -1)
    neg = np.where(col_in, 0.0, NEG_BIG).astype(np.float32)
    return jnp.asarray(cos), jnp.asarray(sin), jnp.asarray(onehot), jnp.asarray(neg)


def _bias_slabs(rel_bias, onehot, neg):
    nr = 2 * WIN_R - 1
    rb = jnp.pad(rel_bias.reshape(N_HEADS * nr, 2 * WIN_C - 1), ((0, 0), (0, 1)))
    spread = _mm_f32(rb, onehot, "bias_spread").reshape(N_HEADS, nr, GRID_W, GRID_W)
    slabs = [spread[:, WIN_R - 1 - v:2 * WIN_R - 1 - v] for v in range(WIN_R)]
    b = jnp.stack(slabs, axis=1) + neg[None, None, None]
    return b.transpose(0, 1, 3, 2, 4).reshape(N_HEADS, WIN_R, GRID_W, WIN_R * GRID_W)


def _bias_grad(dbias, onehot):
    nr = 2 * WIN_R - 1
    d = dbias.reshape(N_HEADS, WIN_R, GRID_W, WIN_R, GRID_W).transpose(0, 1, 3, 2, 4)
    tot = jnp.zeros((N_HEADS, nr, GRID_W, GRID_W), F32)
    for v in range(WIN_R):
        tot = tot + jnp.pad(d[:, v], ((0, 0), (WIN_R - 1 - v, v), (0, 0), (0, 0)))
    g = _mm_f32(tot.reshape(N_HEADS * nr, GRID_W * GRID_W), onehot, "bias_grad", trans_b=True)
    return g[:, :2 * WIN_C - 1].reshape(1, N_HEADS, nr, 2 * WIN_C - 1)


def _shift_rows(u, up):
    n = u.shape[0]
    tio = lax.broadcasted_iota(jnp.int32, u.shape, 0)
    if up:
        return jnp.where(tio == n - 1, 0.0, pltpu.roll(u, n - 1, 0))
    return jnp.where(tio == 0, 0.0, pltpu.roll(u, 1, 0))


def _conv3(u, w_ref, b_ref):
    um, up = _shift_rows(u, False), _shift_rows(u, True)
    return um, up, um * w_ref[0:1, :] + u * w_ref[1:2, :] + up * w_ref[2:3, :] + b_ref[...]


def _ffn_act_fwd(u0, t3, cw, cb):
    t, n = u0.shape
    tc = _pick(n, (256, 128))

    def body(u_ref, t_ref, w_ref, b_ref, a_ref):
        _, _, uc = _conv3(u_ref[...], w_ref, b_ref)
        a_ref[...] = (uc * jax.nn.sigmoid(uc) * t_ref[...]).astype(a_ref.dtype)

    col = lambda rows_: pl.BlockSpec((rows_, tc), lambda j: (0, j))
    return _pcall(body, name="ffn_act_fwd", grid=(n // tc,), in_specs=[col(t), col(t), col(8), col(1)], out_specs=col(t),
                  out_shape=_sds((t, n), BF16))(u0, t3, cw, cb)


def _ffn_act_bwd(u0, t3, cw, cb, da):
    t, n = u0.shape
    tc = _pick(n, (256, 128))

    def body(u_ref, t_ref, w_ref, b_ref, da_ref, du_ref, dt_ref, dw_ref, db_ref):
        u = u_ref[...]
        um, up, uc = _conv3(u, w_ref, b_ref)
        sg = jax.nn.sigmoid(uc)
        dav = da_ref[...].astype(F32)
        dt_ref[...] = (dav * uc * sg).astype(dt_ref.dtype)
        duc = dav * t_ref[...] * sg * (1.0 + uc * (1.0 - sg))
        du = _shift_rows(duc, True) * w_ref[0:1, :] + duc * w_ref[1:2, :] + _shift_rows(duc, False) * w_ref[2:3, :]
        du_ref[...] = du.astype(du_ref.dtype)
        dw_ref[...] = jnp.zeros_like(dw_ref)
        dw_ref[0:1, :] = jnp.sum(duc * um, axis=0, keepdims=True)
        dw_ref[1:2, :] = jnp.sum(duc * u, axis=0, keepdims=True)
        dw_ref[2:3, :] = jnp.sum(duc * up, axis=0, keepdims=True)
        db_ref[...] = jnp.sum(duc, axis=0, keepdims=True)

    col = lambda rows_: pl.BlockSpec((rows_, tc), lambda j: (0, j))
    return _pcall(body, name="ffn_act_bwd", grid=(n // tc,), in_specs=[col(t), col(t), col(8), col(1), col(t)],
                  out_specs=[col(t), col(t), col(8), col(1)],
                  out_shape=[_sds((t, n), BF16), _sds((t, n), BF16), _sds((8, n), F32), _sds((1, n), F32)])(u0, t3, cw, cb, da)


def _adam_math(g, w, m, v):
    m2 = ADAM_B1 * m + (1.0 - ADAM_B1) * g
    v2 = ADAM_B2 * v + (1.0 - ADAM_B2) * (g * g)
    m_hat = m2 / (1.0 - ADAM_B1 ** ADAM_STEP)
    v_hat = v2 / (1.0 - ADAM_B2 ** ADAM_STEP)
    return -ADAM_LR * (m_hat / (jnp.sqrt(v_hat) + ADAM_EPS) + ADAM_WD * w), m2, v2


def _adam_big(parts, w, m, v, name):
    r, c = w.shape
    npart, _, cp = parts.shape
    tr = _pick(r, (64, 32, 16, 8))

    def body(p_ref, w_ref, m_ref, v_ref, g_ref, d_ref, m2_ref, v2_ref):
        g = p_ref[0, :, 0:c].astype(F32)
        for i in range(1, npart):
            g = g + p_ref[i, :, 0:c].astype(F32)
        d, m2, v2 = _adam_math(g, w_ref[...], m_ref[...], v_ref[...])
        g_ref[...] = g
        d_ref[...] = d
        m2_ref[...] = m2
        v2_ref[...] = v2

    row = pl.BlockSpec((tr, c), lambda i: (i, 0))
    return _pcall(body, name=name, grid=(r // tr,),
                  in_specs=[pl.BlockSpec((npart, tr, cp), lambda i: (0, i, 0)), row, row, row],
                  out_specs=[row] * 4, out_shape=[_sds((r, c), F32)] * 4)(parts, w, m, v)


def _adam_small(g, w, m, v):
    def body(g_ref, w_ref, m_ref, v_ref, d_ref, m2_ref, v2_ref):
        d_ref[...], m2_ref[...], v2_ref[...] = _adam_math(g_ref[...], w_ref[...], m_ref[...], v_ref[...])

    return _pcall(body, name="adam_small", out_shape=[_sds(g.shape, F32)] * 3)(g, w, m, v)


def _sum_parts(parts, name):
    def body(p_ref, o_ref):
        s = p_ref[0]
        for i in range(1, N_DEV):
            s = s + p_ref[i]
        o_ref[...] = s

    return _pcall(body, name=name, out_shape=_sds(parts.shape[1:], F32))(parts)


class _Pack:
    def __init__(self, shapes):
        self.shapes = shapes
        self.sizes = [int(np.prod(s)) for s in shapes]
        self.padded = [-(-n // (8 * LANE)) * 8 * LANE for n in self.sizes]
        self.offs = np.concatenate([[0], np.cumsum(self.padded)]).tolist()

    def pack(self, arrs):
        flat = [jnp.pad(a.reshape(-1).astype(F32), (0, p - n)) for a, n, p in zip(arrs, self.sizes, self.padded)]
        return jnp.concatenate(flat).reshape(-1, LANE)

    def unpack(self, slab):
        flat = slab.reshape(-1)
        return [flat[o:o + n].reshape(s) for o, n, s in zip(self.offs, self.sizes, self.shapes)]


def kernel(x, c, ctx, c_ctx, ada_w, ada_b, norm1_g, norm2_g, w_in, hgrn_lb_logits, hgrn_norm_g, na_q_norm_g, na_k_norm_g, na_rel_bias, w_branch_a, w_branch_b, w_out, ffn_w1, ffn_w3, ffn_conv_w, ffn_conv_b, ffn_w2, loss_target, m_c_ctx, m_ada_w, m_ada_b, m_norm1_g, m_norm2_g, m_w_in, m_hgrn_lb_logits, m_hgrn_norm_g, m_na_q_norm_g, m_na_k_norm_g, m_na_rel_bias, m_w_branch_a, m_w_branch_b, m_w_out, m_ffn_w1, m_ffn_w3, m_ffn_conv_w, m_ffn_conv_b, m_ffn_w2, v_c_ctx, v_ada_w, v_ada_b, v_norm1_g, v_norm2_g, v_w_in, v_hgrn_lb_logits, v_hgrn_norm_g, v_na_q_norm_g, v_na_k_norm_g, v_na_rel_bias, v_w_branch_a, v_w_branch_b, v_w_out, v_ffn_w1, v_ffn_w3, v_ffn_conv_w, v_ffn_conv_b, v_ffn_w2):
    t, d = x.shape[1], x.shape[2]
    n_ctx = ctx.shape[1]
    tt = n_ctx + t
    hw = N_HEADS * HEAD_DIM
    ci = w_in.shape[2]
    ca = ada_w.shape[2]
    ff_l = ffn_w1.shape[2]
    ff_p = -(-ff_l // LANE) * LANE
    rows = t // GRID_W
    assert rows >= WIN_R and t % GRID_W == 0 and n_ctx % GRID_W == 0 and ci % LANE == 0 and d % LANE == 0
    me = 4 * lax.axis_index("x") + 2 * lax.axis_index("y") + lax.axis_index("c")
    tr = _pick(n_ctx, (256, 128, 64))
    n_ctx_tiles = n_ctx // tr

    pad_c = lambda w: jnp.pad(w, ((0, 0), (0, ff_p - ff_l)))
    small_in = [c, hgrn_lb_logits.reshape(4, HEAD_DIM), jnp.pad(ffn_conv_w[0], ((0, 5), (0, ff_p - ff_l)))]
    c_all, lb_parts, cw_all = _exchange(small_in, "gather_params", scatter=False)
    ff = N_DEV * ff_p

    cc = jnp.concatenate([c_all.reshape(N_DEV, d), jnp.broadcast_to(c_ctx[None, :], (N_DEV, d))], axis=0)
    act = _tiled(lambda v: v * jax.nn.sigmoid(v), [_const(cc)], [(cc.shape, BF16, cc.shape, lambda *_: (0, 0))], (), "silu_c")[0]
    ada16 = _cast_bf16(ada_w[0], "cast_ada")
    mod_cols = _mm_xw(act, ada16.reshape(1, d, ca), F32, "ada_fwd")
    mod_all = _exchange([mod_cols], "gather_mod", scatter=False)[0]

    with _after(mod_all):
        w_in16 = _cast_bf16(w_in[0], "cast_w_in")
    g_in, = _gather_sc([w_in16], "gather_w_in", 1)
    rest16 = []
    for w_, nm in ((w_branch_a[0], "cast_w_a"), (w_branch_b[0], "cast_w_b"), (w_out[0], "cast_w_out"), (pad_c(ffn_w1[0]), "cast_w1"),
                   (pad_c(ffn_w3[0]), "cast_w3"), (jnp.pad(ffn_w2[0], ((0, ff_p - ff_l), (0, 0))), "cast_w2")):
        with _after(w_in16):
            rest16.append(_cast_bf16(w_, nm))
    g_a, g_b, g_out = _gather_sc(rest16[:3], "gather_mix", 2)
    g_w1, g_w3, g_w2 = _gather_sc(rest16[3:], "gather_ffn", 10)
    g_out = g_out.reshape(1, d, d)
    g_w2 = g_w2.reshape(1, ff, d)
    mod_all = mod_all.transpose(1, 0, 2).reshape(2 * N_DEV, N_MOD * d) + ada_b
    mod_l = lax.dynamic_slice_in_dim(mod_all, me, 1, axis=0).reshape(N_MOD, 1, d)
    mod_c = mod_all[N_DEV:N_DEV + 1].reshape(N_MOD, 1, d)
    mods1 = jnp.stack([mod_c[0:2], mod_l[0:2]])
    mods2 = jnp.stack([mod_l[3:5], mod_l[3:5]])
    gate1, gate2 = mod_l[2], mod_l[5]

    xcat = jnp.concatenate([ctx[0], x[0]], axis=0)
    hcat = _norm_mod_fwd(xcat, norm1_g, mods1, n_ctx_tiles, tr)
    p = _mm_xw(hcat, g_in, F32, "in_proj")
    lb_logits = lb_parts.transpose(1, 0, 2).reshape(2, 2, hw)
    lb_soft = _tiled(lambda a, b: (1.0 / (1.0 + jnp.exp(b - a)),), [_const(lb_logits[:, 0]), _const(lb_logits[:, 1])],
                     [((2, hw), F32, (2, hw), lambda *_: (0, 0))], (), "lb_softmax")[0]
    lb_f, lb_b = lb_soft[0:1], lb_soft[1:2]
    o_f = _hgrn_fwd(p, lb_f, 1, False, n_ctx)
    o_b = _hgrn_fwd(p, lb_b, 2, True, n_ctx)

    cos, sin, onehot, neg = _na_tables(t, n_ctx)
    bias = _bias_slabs(na_rel_bias[0], onehot, neg)
    hcol = lambda seg, off=0: (lambda i, h: (i + off, seg * N_HEADS + h))
    tq = tr
    lat0 = n_ctx // tq
    qk_fn = lambda tv, g, cs, sn: (_f_qk(tv, g, cs, sn),)
    tab = lambda a, off=0: (a, (tq, HEAD_DIM), lambda i, h: (i + off, 0))
    qn = _tiled(qk_fn, [(p, (tq, HEAD_DIM), hcol(5, lat0)), _const(na_q_norm_g), tab(cos, lat0), tab(sin, lat0)],
                [((t, hw), BF16, (tq, HEAD_DIM), lambda i, h: (i, h))], (t // tq, N_HEADS), "q_norm_rope")[0]
    kall = _tiled(qk_fn, [(p, (tq, HEAD_DIM), hcol(6)), _const(na_k_norm_g), tab(cos), tab(sin)],
                  [((tt, hw), BF16, (tq, HEAD_DIM), lambda i, h: (i, h))], (tt // tq, N_HEADS), "k_norm_rope")[0]
    y_b = _na_fwd(qn, kall, p, 7, bias, n_ctx)

    trh = tr
    lat_h = n_ctx // trh
    ospec = lambda a: (a, (trh, HEAD_DIM), lambda i, h: (i + lat_h, h))
    y_a = _tiled(lambda a, b, gt, g: (_f_readout(a, b, gt, g),),
                 [ospec(o_f), ospec(o_b), (p, (trh, HEAD_DIM), hcol(4, lat_h)), _const(hgrn_norm_g)],
                 [((t, hw), BF16, (trh, HEAD_DIM), lambda i, h: (i, h))], (t // trh, N_HEADS), "hgrn_readout")[0]

    p_a = _mm_xw(y_a, g_a, F32, "branch_a")
    p_b = _mm_xw(y_b, g_b, F32, "branch_b")
    td = _pick(d, (512, 256, 128))
    nd_t = d // td
    lat_r = n_ctx // tr
    gcol = lambda k: (p, (tr, td), lambda i, j, k=k: (i + lat_r, 8 * hw // td + k * nd_t + j))
    dtile = lambda a: (a, (tr, td), lambda i, j: (i, j))
    z = _tiled(lambda ga, gb, pa, pb: (_f_merge(ga, gb, pa, pb),), [gcol(0), gcol(1), dtile(p_a), dtile(p_b)],
               [((t, d), BF16, (tr, td), lambda i, j: (i, j))], (t // tr, nd_t), "merge")[0]
    mix = _mm_xw(z, g_out, F32, "out_proj")
    xl = x[0]
    x_mid = _tiled(lambda a, g, m_: (_f_resid(a, g, m_),), [_rows(xl, tr), _const(gate1), _rows(mix, tr)],
                   [((t, d), F32, (tr, d), lambda i: (i, 0))], (t // tr,), "resid1")[0]

    h2 = _norm_mod_fwd(x_mid, norm2_g, mods2, 0, tr)
    u0 = _mm_xw(h2, g_w1, F32, "ffn_up1")
    t3 = _mm_xw(h2, g_w3, F32, "ffn_up3")
    cw_full = cw_all.transpose(1, 0, 2).reshape(8, ff)
    cb_full = jnp.pad(ffn_conv_b.reshape(N_DEV, ff_l), ((0, 0), (0, ff_p - ff_l))).reshape(1, ff)
    a_act = _ffn_act_fwd(u0, t3, cw_full, cb_full)
    f_out = _mm_xw(a_act, g_w2, F32, "ffn_down")

    def loss_fn(xm, g, f, tg):
        err = xm + g * f - tg
        return err * (1.0 / d), jnp.sum(err * err, axis=0, keepdims=True) * (0.5 / d), jnp.sum(err * (1.0 / d) * f, axis=0, keepdims=True)

    dy, loss_cols, d_gate2 = _tiled(loss_fn, [_rows(x_mid, tr), _const(gate2), _rows(f_out, tr), _rows(loss_target[0], tr)],
                                    [((t, d), F32, (tr, d), lambda i: (i, 0)), ((1, d), F32, (1, d), lambda i: (0, 0)),
                                     ((1, d), F32, (1, d), lambda i: (0, 0))], (t // tr,), "loss", acc=(1, 2))

    df = _tiled(lambda a, g: (a * g,), [_rows(dy, tr), _const(gate2)], [((t, d), BF16, (tr, d), lambda i: (i, 0))], (t // tr,), "d_ffn_out")[0]
    d_w2 = _mm_xtdy(a_act, df, 1, BF16, "d_w2")
    da = _mm_dyw(df, g_w2, BF16, "d_act")
    du0, dt3, d_cw, d_cb = _ffn_act_bwd(u0, t3, cw_full, cb_full, da)
    d_w1 = _mm_xtdy(h2, du0, N_DEV, BF16, "d_w1")
    d_w3 = _mm_xtdy(h2, dt3, N_DEV, BF16, "d_w3")
    ffn_blocks = [d_w1, d_w3, d_w2.reshape(N_DEV, ff_p, d)]
    ffn_theirs = _pair_swap_sc(ffn_blocks, "scatter_ffn_pair", 3)
    dh2 = _mm_dyw(du0, g_w1, F32, "d_h2_a")
    dh2 = _mm_dyw(dt3, g_w3, F32, "d_h2_b", init=dh2)
    with _after(dh2):
        ffn_sums = [_pair_add(ffn_blocks[0], ffn_theirs[0], "pair_add_w1")]
    ffn_sums += [_pair_add(ffn_blocks[1], ffn_theirs[1], "pair_add_w3"), _pair_add(ffn_blocks[2], ffn_theirs[2], "pair_add_w2")]
    r_w1, r_w3, r_w2 = _chip_scatter_sc(ffn_sums, "scatter_ffn_chips", 4)
    with _after(*ffn_sums):
        dx_mid, d_norm2, d_mods2 = _norm_mod_bwd(x_mid, norm2_g, mods2, dh2, dy, 0, tr, "norm_mod2_bwd")

    dm, d_gate1 = _tiled(lambda dxm, g, m_: (dxm * g, jnp.sum(dxm * m_, axis=0, keepdims=True)),
                         [_rows(dx_mid, tr), _const(gate1), _rows(mix, tr)],
                         [((t, d), BF16, (tr, d), lambda i: (i, 0)), ((1, d), F32, (1, d), lambda i: (0, 0))], (t // tr,), "d_resid1", acc=(1,))
    d_wout = _mm_xtdy(z, dm, 1, BF16, "d_w_out")
    dz = _mm_dyw(dm, g_out, F32, "d_merge")

    def merge_bwd(ga, gb, pa, pb, dzv):
        _, vjp = jax.vjp(_f_merge, ga, gb, pa, pb)
        return vjp(dzv)

    dga, dgb, dpa, dpb = _tiled(merge_bwd, [gcol(0), gcol(1), dtile(p_a), dtile(p_b), dtile(dz)],
                                [((t, d), BF16, (tr, td), lambda i, j: (i, j))] * 4, (t // tr, nd_t), "merge_bwd")
    d_wa = _mm_xtdy(y_a, dpa, N_DEV, BF16, "d_w_a")
    d_wb = _mm_xtdy(y_b, dpb, N_DEV, BF16, "d_w_b")
    mix_blocks = [d_wa, d_wb, d_wout.reshape(N_DEV, d // N_DEV, d)]
    mix_theirs = _pair_swap_sc(mix_blocks, "scatter_mix_pair", 5)
    dy_a = _mm_dyw(dpa, g_a, F32, "d_y_a")
    dy_b = _mm_dyw(dpb, g_b, BF16, "d_y_b")
    with _after(dy_b):
        mix_sums = [_pair_add(mix_blocks[0], mix_theirs[0], "pair_add_wa")]
    mix_sums += [_pair_add(mix_blocks[1], mix_theirs[1], "pair_add_wb"), _pair_add(mix_blocks[2], mix_theirs[2], "pair_add_wout")]
    r_a, r_b, r_out = _chip_scatter_sc(mix_sums, "scatter_mix_chips", 6)

    def readout_bwd(a, b, gt, g, ct):
        _, vjp = jax.vjp(_f_readout, a, b, gt, g)
        da_, _, dgt, dg = vjp(ct)
        return da_, dgt, dg

    hsp = lambda dt: ((t, hw), dt, (trh, HEAD_DIM), lambda i, h: (i, h))
    with _after(*mix_sums):
        do_h, d_gate_o, d_hnorm = _tiled(
            readout_bwd, [ospec(o_f), ospec(o_b), (p, (trh, HEAD_DIM), hcol(4, lat_h)), _const(hgrn_norm_g), (dy_a, (trh, HEAD_DIM), lambda i, h: (i, h))],
            [hsp(F32), hsp(BF16), ((1, HEAD_DIM), F32, (1, HEAD_DIM), lambda i, h: (0, 0))], (t // trh, N_HEADS), "readout_bwd", acc=(2,))
    dq1, dfl_f, dv1, dlb_f = _hgrn_bwd(p, lb_f, do_h, 1, False, n_ctx, None)
    dq_h, dfl_b, dv_h, dlb_b = _hgrn_bwd(p, lb_b, do_h, 2, True, n_ctx, (dq1, dv1))

    dqn, dkall, dv_na, dbias = _na_bwd(qn, kall, p, 7, bias, dy_b, n_ctx)

    def qk_bwd(tv, g, cs, sn, ct):
        _, vjp = jax.vjp(lambda a, b: _f_qk(a, b, cs, sn), tv, g)
        return vjp(ct)

    d_pq, d_qnorm = _tiled(qk_bwd, [(p, (tq, HEAD_DIM), hcol(5, lat0)), _const(na_q_norm_g), tab(cos, lat0), tab(sin, lat0),
                                    (dqn, (tq, HEAD_DIM), lambda i, h: (i, h))],
                           [((t, hw), BF16, (tq, HEAD_DIM), lambda i, h: (i, h)), ((1, HEAD_DIM), F32, (1, HEAD_DIM), lambda i, h: (0, 0))],
                           (t // tq, N_HEADS), "q_norm_rope_bwd", acc=(1,))
    d_pk, d_knorm = _tiled(qk_bwd, [(p, (tq, HEAD_DIM), hcol(6)), _const(na_k_norm_g), tab(cos), tab(sin),
                                    (dkall, (tq, HEAD_DIM), lambda i, h: (i, h))],
                           [((tt, hw), BF16, (tq, HEAD_DIM), lambda i, h: (i, h)), ((1, HEAD_DIM), F32, (1, HEAD_DIM), lambda i, h: (0, 0))],
                           (tt // tq, N_HEADS), "k_norm_rope_bwd", acc=(1,))

    zc = lambda w_: jnp.zeros((n_ctx, w_), BF16)
    lat_only = lambda a: jnp.concatenate([zc(a.shape[1]), a], axis=0)
    dp = jnp.concatenate([dq_h, dfl_f, dfl_b, dv_h, lat_only(d_gate_o), lat_only(d_pq), d_pk, dv_na, lat_only(dga), lat_only(dgb)], axis=1)
    d_win = _mm_xtdy(hcat, dp, N_DEV, BF16, "d_w_in")
    win_theirs, = _pair_swap_sc([d_win], "scatter_w_in_pair", 7)
    with _after(d_win):
        dhcat = _mm_dyw(dp, g_in, BF16, "d_hcat")
    zero_ctx = jnp.concatenate([jnp.zeros((n_ctx, d), F32), dx_mid], axis=0)
    dxcat, d_norm1, d_mods1 = _norm_mod_bwd(xcat, norm1_g, mods1, dhcat, zero_ctx, n_ctx_tiles, tr, "norm_mod1_bwd")
    grad_x = dxcat[n_ctx:][None]

    zd = jnp.zeros((1, d), F32)
    dmod_l = jnp.concatenate([d_mods1[1, 0], d_mods1[1, 1], d_gate1, d_mods2[1, 0], d_mods2[1, 1], d_gate2], axis=1)
    dmod_c = jnp.concatenate([d_mods1[0, 0], d_mods1[0, 1], zd, zd, zd, zd], axis=1)
    dmods = jnp.concatenate([dmod_l, dmod_c], axis=0).reshape(2, N_DEV, ca).transpose(1, 0, 2)
    dmods = jnp.pad(dmods, ((0, 0), (0, 6), (0, 0)))
    got = _exchange([dmods], "scatter_dmod", scatter=True)[0]
    dm_rows = jnp.concatenate([got[:, 0], got[:, 1]], axis=0)
    d_ada = _mm_xtdy(act, dm_rows, 1, F32, "d_ada_w")[0]
    back = _mm_dyw(dm_rows, ada16.reshape(1, d, ca), F32, "d_silu_c")
    d_cctx_part = _tiled(lambda b, v: (jnp.sum(b[N_DEV:], axis=0, keepdims=True) * (jax.nn.sigmoid(v) * (1.0 + v * (1.0 - jax.nn.sigmoid(v)))),),
                         [_const(back), _const(c_ctx.reshape(1, d))], [((1, d), F32, (1, d), lambda *_: (0, 0))], (), "d_c_ctx")[0]

    d_rel = _bias_grad(dbias, onehot)
    d_lb_soft = jnp.concatenate([dlb_f, dlb_b], axis=0)
    d_lb0 = _tiled(lambda s, g: (g * s * (1.0 - s),), [_const(lb_soft), _const(d_lb_soft)], [((2, hw), F32, (2, hw), lambda *_: (0, 0))], (), "d_lb")[0]
    d_lb_full = jnp.stack([d_lb0, -d_lb0], axis=1)
    d_cw_l = d_cw[:3].reshape(3, N_DEV, ff_p)[:, :, :ff_l].reshape(1, 3, N_DEV * ff_l)
    d_cb_l = d_cb.reshape(N_DEV, ff_p)[:, :ff_l].reshape(1, N_DEV * ff_l)
    small = [d_cctx_part.reshape(d), (dmod_l + dmod_c), d_norm1, d_norm2, d_lb_full, d_hnorm, d_qnorm, d_knorm, d_rel, d_cw_l, d_cb_l, loss_cols]
    pk = _Pack([a.shape for a in small])
    small_slab = pk.pack(small)
    small_parts, = _gather_sc([small_slab], "gather_small", 9)

    res = {}
    with _after(small_slab):
        win_sums = _pair_add(d_win, win_theirs, "pair_add_w_in")
    r_in, = _chip_scatter_sc([win_sums], "scatter_w_in_chips", 8)
    with _after(win_sums):
        res["ada_w"] = _adam_big(d_ada[None], ada_w[0], m_ada_w[0], v_ada_w[0], "adam_ada")
    res["ffn_w1"] = _adam_big(r_w1, ffn_w1[0], m_ffn_w1[0], v_ffn_w1[0], "adam_w1")
    res["ffn_w3"] = _adam_big(r_w3, ffn_w3[0], m_ffn_w3[0], v_ffn_w3[0], "adam_w3")
    res["ffn_w2"] = _adam_big(r_w2, ffn_w2[0], m_ffn_w2[0], v_ffn_w2[0], "adam_w2")
    res["w_branch_a"] = _adam_big(r_a, w_branch_a[0], m_w_branch_a[0], v_w_branch_a[0], "adam_w_a")
    res["w_branch_b"] = _adam_big(r_b, w_branch_b[0], m_w_branch_b[0], v_w_branch_b[0], "adam_w_b")
    res["w_out"] = _adam_big(r_out, w_out[0], m_w_out[0], v_w_out[0], "adam_w_out")

    with _after(res["w_out"][1]):
        tot = _sum_parts(small_parts, "sum_small")
    g_cctx, g_ada_b, g_n1, g_n2, g_lb, g_hn, g_qn, g_kn, g_rel, g_cw, g_cb, loss_all = pk.unpack(tot)
    loss = _tiled(lambda v: (jnp.sum(v, axis=1, keepdims=True),), [_const(loss_all)], [((1, 1), F32, (1, 1), lambda *_: (0, 0))], (), "loss_total")[0][0, 0]
    g_lb = lax.dynamic_slice_in_dim(g_lb, me * HEAD_DIM, HEAD_DIM, axis=2)
    g_cw = lax.dynamic_slice_in_dim(g_cw, me * ff_l, ff_l, axis=2)
    small_names = [("c_ctx", g_cctx, c_ctx, m_c_ctx, v_c_ctx), ("ada_b", g_ada_b, ada_b, m_ada_b, v_ada_b),
                   ("norm1_g", g_n1, norm1_g, m_norm1_g, v_norm1_g), ("norm2_g", g_n2, norm2_g, m_norm2_g, v_norm2_g),
                   ("hgrn_lb_logits", g_lb, hgrn_lb_logits, m_hgrn_lb_logits, v_hgrn_lb_logits),
                   ("hgrn_norm_g", g_hn, hgrn_norm_g, m_hgrn_norm_g, v_hgrn_norm_g), ("na_q_norm_g", g_qn, na_q_norm_g, m_na_q_norm_g, v_na_q_norm_g),
                   ("na_k_norm_g", g_kn, na_k_norm_g, m_na_k_norm_g, v_na_k_norm_g), ("na_rel_bias", g_rel, na_rel_bias, m_na_rel_bias, v_na_rel_bias),
                   ("ffn_conv_w", g_cw, ffn_conv_w, m_ffn_conv_w, v_ffn_conv_w), ("ffn_conv_b", g_cb, ffn_conv_b, m_ffn_conv_b, v_ffn_conv_b)]
    pk2 = _Pack([s[1].shape for s in small_names])
    sd, sm, sv = _adam_small(*[pk2.pack([s[i] for s in small_names]) for i in (1, 2, 3, 4)])
    sd, sm, sv = pk2.unpack(sd), pk2.unpack(sm), pk2.unpack(sv)
    res.update({s[0]: (s[1], sd[i], sm[i], sv[i]) for i, s in enumerate(small_names)})
    with _after(sd[0]):
        res["w_in"] = _adam_big(r_in, w_in[0], m_w_in[0], v_w_in[0], "adam_w_in")
    for k in ("w_in", "w_branch_a", "w_branch_b", "w_out", "ffn_w1", "ffn_w3", "ffn_w2", "ada_w"):
        res[k] = tuple(a[None] for a in res[k])

    order = ["c_ctx", "ada_w", "ada_b", "norm1_g", "norm2_g", "w_in", "hgrn_lb_logits", "hgrn_norm_g", "na_q_norm_g", "na_k_norm_g",
             "na_rel_bias", "w_branch_a", "w_branch_b", "w_out", "ffn_w1", "ffn_w3", "ffn_conv_w", "ffn_conv_b", "ffn_w2"]
    shapes = {"c_ctx": c_ctx.shape, "ada_b": ada_b.shape, "norm1_g": norm1_g.shape, "norm2_g": norm2_g.shape,
              "hgrn_lb_logits": hgrn_lb_logits.shape, "hgrn_norm_g": hgrn_norm_g.shape, "na_q_norm_g": na_q_norm_g.shape,
              "na_k_norm_g": na_k_norm_g.shape, "na_rel_bias": na_rel_bias.shape, "ffn_conv_w": ffn_conv_w.shape, "ffn_conv_b": ffn_conv_b.shape}
    outs = [loss, grad_x]
    for part in range(4):
        for k in order:
            a = res[k][part]
            outs.append(a.reshape(shapes[k]) if k in shapes else a)
    return tuple(outs)
```

```python
import functools

import numpy as np
import jax
import jax.numpy as jnp
from jax import lax
from jax.experimental import pallas as pl
from jax.experimental.pallas import tpu as pltpu
from jax.experimental.pallas import tpu_sc as plsc

F32 = jnp.float32
BF16 = jnp.bfloat16
HIGHEST = lax.Precision.HIGHEST

N_DEV = 8
MESH_ID = pl.DeviceIdType.MESH
LANE = 128
HEAD_DIM = 128
N_HEADS = 8
GRID_W = 64
WIN_R = 8
WIN_C = 16
ROPE_THETA = 10000.0
EPS = 1e-6
N_MOD = 6
HGRN_BLOCK = 16
NEG_BIG = -1e30
VMEM_LIMIT = 56 << 20

ADAM_LR = 0.001
ADAM_B1 = 0.9
ADAM_B2 = 0.999
ADAM_EPS = 1e-08
ADAM_WD = 0.01
ADAM_STEP = 10

HBM_SPEC = pl.BlockSpec(memory_space=pltpu.HBM)


_ORDER_AFTER = []


class _after:
    def __init__(self, *arrs):
        self.arrs = list(arrs)

    def __enter__(self):
        _ORDER_AFTER.extend(self.arrs)

    def __exit__(self, *exc):
        del _ORDER_AFTER[:]


def _pcall(body, *, name, out_shape, grid=None, in_specs=None, out_specs=None, scratch=(), aliases=None):
    kw = {}
    if grid is not None:
        kw["grid"] = grid
    extra = []
    if _ORDER_AFTER and in_specs is not None:
        extra = list(_ORDER_AFTER)
        del _ORDER_AFTER[:]
        n_in, n_extra, inner = len(in_specs), len(extra), body
        in_specs = list(in_specs) + [pl.BlockSpec(memory_space=pl.ANY)] * n_extra

        def body(*refs):
            return inner(*refs[:n_in], *refs[n_in + n_extra:])

    if extra:
        call = _pcall_inner(body, name, out_shape, kw, in_specs, out_specs, scratch, aliases)
        return lambda *args: call(*args, *extra)
    return _pcall_inner(body, name, out_shape, kw, in_specs, out_specs, scratch, aliases)


def _pcall_inner(body, name, out_shape, kw, in_specs, out_specs, scratch, aliases):
    if in_specs is not None:
        kw["in_specs"] = in_specs
    if out_specs is not None:
        kw["out_specs"] = out_specs
    if scratch:
        kw["scratch_shapes"] = list(scratch)
    if aliases:
        kw["input_output_aliases"] = aliases
    return pl.pallas_call(body, name=name, out_shape=out_shape,
                          compiler_params=pltpu.CompilerParams(vmem_limit_bytes=VMEM_LIMIT), **kw)


def _pick(dim, cands):
    for c in cands:
        if c <= dim and dim % c == 0:
            return c
    return dim


def _sds(shape, dtype):
    return jax.ShapeDtypeStruct(tuple(shape), dtype)


def _peers():
    x, y, c = lax.axis_index("x"), lax.axis_index("y"), lax.axis_index("c")
    out = []
    for k in range(1, N_DEV):
        px = 1 - x if (k >> 2) & 1 else x
        py = 1 - y if (k >> 1) & 1 else y
        pc = 1 - c if k & 1 else c
        out.append((k, (px, py, pc), 4 * px + 2 * py + pc))
    return 4 * x + 2 * y + c, out


def _exchange(arrs, name, scatter):
    n = len(arrs)

    def body(*refs):
        ins, outs = refs[:n], refs[n:2 * n]
        send, recv, loc = refs[2 * n:]
        me, peers = _peers()
        started = []
        for i in range(n):
            src = ins[i].at[me] if scatter else ins[i]
            cp = pltpu.make_async_copy(src, outs[i].at[me], loc.at[i])
            cp.start()
            started.append(cp)
        sends = []
        for k, peer, pidx in peers:
            for i in range(n):
                src = ins[i].at[pidx] if scatter else ins[i]
                cp = pltpu.make_async_remote_copy(src_ref=src, dst_ref=outs[i].at[me], send_sem=send.at[i * 7 + k - 1],
                                                  recv_sem=recv.at[i * 7 + k - 1], device_id=peer, device_id_type=MESH_ID)
                cp.start()
                sends.append(cp)
        for k, peer, pidx in peers:
            for i in range(n):
                src = ins[i].at[pidx] if scatter else ins[i]
                pltpu.make_async_remote_copy(src_ref=src, dst_ref=outs[i].at[pidx], send_sem=send.at[i * 7 + k - 1],
                                             recv_sem=recv.at[i * 7 + k - 1], device_id=peer, device_id_type=MESH_ID).wait_recv()
        for cp in sends:
            cp.wait_send()
        for cp in started:
            cp.wait()

    out_shape = [_sds(a.shape if scatter else (N_DEV,) + a.shape, a.dtype) for a in arrs]
    res = _pcall(body, name=name, out_shape=out_shape, in_specs=[HBM_SPEC] * n, out_specs=[HBM_SPEC] * n,
                 scratch=[pltpu.SemaphoreType.DMA((7 * n,)), pltpu.SemaphoreType.DMA((7 * n,)), pltpu.SemaphoreType.DMA((n,))])(*arrs)
    return list(res)


def _exchange_sc(arrs, name, scatter, collective_id):
    n = len(arrs)
    srcs = [jax.new_ref(a, memory_space=pltpu.MemorySpace.HBM) for a in arrs]
    lands = [jax.empty_ref(_sds(a.shape if scatter else (N_DEV,) + a.shape, a.dtype), memory_space=pltpu.MemorySpace.HBM) for a in arrs]

    @pl.kernel(mesh=plsc.ScalarSubcoreMesh(axis_name="seq", num_cores=1), name=name,
               scratch_types=(pltpu.SemaphoreType.DMA((7 * n,)), pltpu.SemaphoreType.DMA((7 * n,)), pltpu.SemaphoreType.DMA((n,))),
               compiler_params=pltpu.CompilerParams(collective_id=collective_id))
    def launch(send, recv, loc):
        me, peers = _peers()
        barrier = pltpu.get_barrier_semaphore()
        for _, peer, _ in peers:
            pl.semaphore_signal(barrier, inc=1, device_id=peer, device_id_type=MESH_ID)
        pl.semaphore_wait(barrier, N_DEV - 1)
        own = [pltpu.make_async_copy(srcs[i].at[me] if scatter else srcs[i], lands[i].at[me], loc.at[i]) for i in range(n)]
        for cp in own:
            cp.start()
        sends = []
        for k, peer, pidx in peers:
            for i in range(n):
                src = srcs[i].at[pidx] if scatter else srcs[i]
                cp = pltpu.make_async_remote_copy(src_ref=src, dst_ref=lands[i].at[me], send_sem=send.at[i * 7 + k - 1],
                                                  recv_sem=recv.at[i * 7 + k - 1], device_id=peer, device_id_type=MESH_ID)
                cp.start()
                sends.append(cp)
        for k, peer, pidx in peers:
            for i in range(n):
                src = srcs[i].at[pidx] if scatter else srcs[i]
                pltpu.make_async_remote_copy(src_ref=src, dst_ref=lands[i].at[pidx], send_sem=send.at[i * 7 + k - 1],
                                             recv_sem=recv.at[i * 7 + k - 1], device_id=peer, device_id_type=MESH_ID).wait_recv()
        for cp in sends:
            cp.wait_send()
        for cp in own:
            cp.wait()

    launch()
    return [r[...] for r in lands]


def _gather_sc(arrs, name, collective_id):
    n = len(arrs)
    srcs = [jax.new_ref(a, memory_space=pltpu.MemorySpace.HBM) for a in arrs]
    lands = [jax.empty_ref(_sds((N_DEV,) + a.shape, a.dtype), memory_space=pltpu.MemorySpace.HBM) for a in arrs]

    @pl.kernel(mesh=plsc.ScalarSubcoreMesh(axis_name="seq", num_cores=1), name=name,
               scratch_types=(pltpu.SemaphoreType.DMA((7 * n,)), pltpu.SemaphoreType.DMA((7 * n,)), pltpu.SemaphoreType.DMA((n,))),
               compiler_params=pltpu.CompilerParams(collective_id=collective_id))
    def launch(send, recv, loc):
        x, y, c = lax.axis_index("x"), lax.axis_index("y"), lax.axis_index("c")
        me, sibling = (x, y, c), (x, y, 1 - c)
        chips = [(1 - x, y), (x, 1 - y), (1 - x, 1 - y)]
        index = lambda px, py, pc: 4 * px + 2 * py + pc
        barrier = pltpu.get_barrier_semaphore()
        for peer in [sibling] + [(*chip, c) for chip in chips]:
            pl.semaphore_signal(barrier, inc=1, device_id=peer, device_id_type=MESH_ID)
        pl.semaphore_wait(barrier, 4)

        def copy(i, k, block, to, from_src):
            return pltpu.make_async_remote_copy(src_ref=srcs[i] if from_src else lands[i].at[index(*block)], dst_ref=lands[i].at[index(*block)],
                                                send_sem=send.at[i * 7 + k], recv_sem=recv.at[i * 7 + k], device_id=to, device_id_type=MESH_ID)

        own = [pltpu.make_async_copy(srcs[i], lands[i].at[index(*me)], loc.at[i]) for i in range(n)]
        for cp in own:
            cp.start()
        started = []
        for i in range(n):
            started.append(copy(i, 0, me, sibling, True))
            started += [copy(i, 1 + j, me, (*chip, c), True) for j, chip in enumerate(chips)]
        for cp in started:
            cp.start()
        for j, chip in enumerate(chips):
            for i in range(n):
                copy(i, 1 + j, (*chip, c), me, False).wait_recv()
                fwd = copy(i, 4 + j, (*chip, c), sibling, False)
                fwd.start()
                started.append(fwd)
        for i in range(n):
            copy(i, 0, sibling, me, False).wait_recv()
            for j, chip in enumerate(chips):
                copy(i, 4 + j, (*chip, 1 - c), me, False).wait_recv()
        for cp in started:
            cp.wait_send()
        for cp in own:
            cp.wait()

    launch()
    return [r[...] for r in lands]


def _pair_swap_sc(blocks, name, collective_id):
    n = len(blocks)
    srcs = [jax.new_ref(b, memory_space=pltpu.MemorySpace.HBM) for b in blocks]
    lands = [jax.empty_ref(_sds((4,) + b.shape[1:], b.dtype), memory_space=pltpu.MemorySpace.HBM) for b in blocks]

    @pl.kernel(mesh=plsc.ScalarSubcoreMesh(axis_name="seq", num_cores=1), name=name,
               scratch_types=(pltpu.SemaphoreType.DMA((4 * n,)), pltpu.SemaphoreType.DMA((4 * n,))),
               compiler_params=pltpu.CompilerParams(collective_id=collective_id))
    def launch(send, recv):
        x, y, c = lax.axis_index("x"), lax.axis_index("y"), lax.axis_index("c")
        sibling = (x, y, 1 - c)
        barrier = pltpu.get_barrier_semaphore()
        pl.semaphore_signal(barrier, inc=1, device_id=sibling, device_id_type=MESH_ID)
        pl.semaphore_wait(barrier, 1)
        copies = [pltpu.make_async_remote_copy(src_ref=srcs[i].at[2 * q + 1 - c], dst_ref=lands[i].at[q], send_sem=send.at[4 * i + q],
                                               recv_sem=recv.at[4 * i + q], device_id=sibling, device_id_type=MESH_ID)
                  for i in range(n) for q in range(4)]
        for cp in copies:
            cp.start()
        for cp in copies:
            cp.wait()

    launch()
    return [r[...] for r in lands]


def _pair_add(blocks, theirs, name):
    _, r, c_ = blocks.shape
    mine = lax.dynamic_index_in_dim(blocks.reshape(4, 2, r, c_), lax.axis_index("c"), axis=1, keepdims=False)
    tr = _pick(r, (256, 128, 64, 16))
    spec = (None, tr, c_)
    return _tiled(lambda a, b: (a.astype(F32) + b.astype(F32),), [(mine, spec, lambda q, i: (q, i, 0)), (theirs, spec, lambda q, i: (q, i, 0))],
                  [((4, r, c_), BF16, spec, lambda q, i: (q, i, 0))], (4, r // tr), name)[0]


def _chip_scatter_sc(sums, name, collective_id):
    n = len(sums)
    srcs = [jax.new_ref(s, memory_space=pltpu.MemorySpace.HBM) for s in sums]
    lands = [jax.empty_ref(_sds(s.shape, s.dtype), memory_space=pltpu.MemorySpace.HBM) for s in sums]

    @pl.kernel(mesh=plsc.ScalarSubcoreMesh(axis_name="seq", num_cores=1), name=name,
               scratch_types=(pltpu.SemaphoreType.DMA((3 * n,)), pltpu.SemaphoreType.DMA((3 * n,)), pltpu.SemaphoreType.DMA((n,))),
               compiler_params=pltpu.CompilerParams(collective_id=collective_id))
    def launch(send, recv, loc):
        x, y, c = lax.axis_index("x"), lax.axis_index("y"), lax.axis_index("c")
        chips = [(1 - x, y), (x, 1 - y), (1 - x, 1 - y)]
        my_chip = 2 * x + y
        barrier = pltpu.get_barrier_semaphore()
        for chip in chips:
            pl.semaphore_signal(barrier, inc=1, device_id=(*chip, c), device_id_type=MESH_ID)
        pl.semaphore_wait(barrier, 3)
        own = [pltpu.make_async_copy(srcs[i].at[my_chip], lands[i].at[my_chip], loc.at[i]) for i in range(n)]
        for cp in own:
            cp.start()
        sends = [pltpu.make_async_remote_copy(src_ref=srcs[i].at[2 * px + py], dst_ref=lands[i].at[my_chip], send_sem=send.at[3 * i + j],
                                              recv_sem=recv.at[3 * i + j], device_id=(px, py, c), device_id_type=MESH_ID)
                 for j, (px, py) in enumerate(chips) for i in range(n)]
        for cp in sends:
            cp.start()
        for j, (px, py) in enumerate(chips):
            for i in range(n):
                pltpu.make_async_remote_copy(src_ref=srcs[i].at[my_chip], dst_ref=lands[i].at[2 * px + py], send_sem=send.at[3 * i + j],
                                             recv_sem=recv.at[3 * i + j], device_id=(px, py, c), device_id_type=MESH_ID).wait_recv()
        for cp in sends:
            cp.wait_send()
        for cp in own:
            cp.wait()

    launch()
    return [r[...] for r in lands]


class _InFlight:
    def __init__(self, send, recv, srcs, lands, token, scatter):
        self.send, self.recv, self.srcs, self.lands, self.token, self.scatter = send, recv, srcs, lands, token, scatter


SEM_SPEC = pl.BlockSpec(memory_space=pltpu.SEMAPHORE)
SIDE_EFFECT = pltpu.SideEffectType.DATAFLOW_SIDE_EFFECTING


def _exchange_start(arrs, name, scatter):
    n = len(arrs)

    def body(*refs):
        ins, lands = refs[:n], refs[n:2 * n]
        send, recv = refs[2 * n], refs[2 * n + 1]
        token = refs[4 * n + 2]
        loc = refs[4 * n + 3]
        me, peers = _peers()
        own = [pltpu.make_async_copy(ins[i].at[me] if scatter else ins[i], lands[i].at[me], loc.at[i]) for i in range(n)]
        for cp in own:
            cp.start()
        for cp in own:
            cp.wait()
        for k, peer, pidx in peers:
            for i in range(n):
                src = ins[i].at[pidx] if scatter else ins[i]
                pltpu.make_async_remote_copy(src_ref=src, dst_ref=lands[i].at[me], send_sem=send.at[i * 7 + k - 1],
                                             recv_sem=recv.at[i * 7 + k - 1], device_id=peer, device_id_type=MESH_ID).start()
        token[...] = jnp.zeros_like(token)

    land_shapes = [a.shape if scatter else (N_DEV,) + a.shape for a in arrs]
    hbm = lambda a: pltpu.with_memory_space_constraint(a, pltpu.HBM)
    args = [hbm(a) for a in arrs] + [hbm(lax.empty(s, a.dtype)) for s, a in zip(land_shapes, arrs)]
    out_shape = ([pltpu.SemaphoreType.DMA((7 * n,)), pltpu.SemaphoreType.DMA((7 * n,))]
                 + [pltpu.HBM(a.shape, a.dtype) for a in arrs] + [pltpu.HBM(s, a.dtype) for s, a in zip(land_shapes, arrs)]
                 + [_sds((8, LANE), F32)])
    res = pl.pallas_call(
        body, name=name, out_shape=out_shape, in_specs=[HBM_SPEC] * (2 * n),
        out_specs=[SEM_SPEC, SEM_SPEC] + [HBM_SPEC] * (2 * n) + [pl.BlockSpec(memory_space=pltpu.VMEM)],
        input_output_aliases={i: 2 + i for i in range(2 * n)},
        scratch_shapes=[pltpu.SemaphoreType.DMA((n,))],
        compiler_params=pltpu.CompilerParams(has_side_effects=SIDE_EFFECT))(*args)
    return _InFlight(res[0], res[1], list(res[2:2 + n]), list(res[2 + n:2 + 2 * n]), res[2 + 2 * n], scatter)


def _exchange_wait(h, after, name):
    n = len(h.srcs)
    scatter = h.scatter
    after = list(after)

    def body(*refs):
        ins, lands = refs[:n], refs[n:2 * n]
        send, recv = refs[2 * n], refs[2 * n + 1]
        _, peers = _peers()
        for k, peer, pidx in peers:
            for i in range(n):
                src = ins[i].at[pidx] if scatter else ins[i]
                cp = pltpu.make_async_remote_copy(src_ref=src, dst_ref=lands[i].at[pidx], send_sem=send.at[i * 7 + k - 1],
                                                  recv_sem=recv.at[i * 7 + k - 1], device_id=peer, device_id_type=MESH_ID)
                cp.wait_send()
                cp.wait_recv()

    res = pl.pallas_call(
        body, name=name, out_shape=[pltpu.HBM(a.shape, a.dtype) for a in h.srcs + h.lands],
        in_specs=[HBM_SPEC] * (2 * n) + [SEM_SPEC, SEM_SPEC] + [pl.BlockSpec(memory_space=pl.ANY)] * len(after),
        out_specs=[HBM_SPEC] * (2 * n), input_output_aliases={i: i for i in range(2 * n)},
        compiler_params=pltpu.CompilerParams(has_side_effects=SIDE_EFFECT))(*h.srcs, *h.lands, h.send, h.recv, *after)
    return list(res[n:])


def _mm_xw(x, g, out_dtype, name, tm_c=(768, 512, 384, 256, 128, 64, 16), tn_c=(1024, 768, 512, 256, 128), tk_c=(2048, 1024, 768, 512, 256)):
    m, r = x.shape
    nb, r2, cl = g.shape
    assert r == r2
    tm, tn, tk = _pick(m, tm_c), _pick(cl, tn_c), _pick(r, tk_c)
    q, nk = cl // tn, r // tk

    def body(x_ref, g_ref, o_ref, *acc):
        p = lax.dot_general(x_ref[...].astype(BF16), g_ref[...], (((1,), (0,)), ((), ())), preferred_element_type=F32)
        if nk == 1:
            o_ref[...] = p.astype(o_ref.dtype)
        else:
            k = pl.program_id(2)

            @pl.when(k == 0)
            def _():
                acc[0][...] = p

            @pl.when(k > 0)
            def _():
                acc[0][...] += p

            @pl.when(k == nk - 1)
            def _():
                o_ref[...] = acc[0][...].astype(o_ref.dtype)

    return _pcall(
        body, name=name, grid=(m // tm, nb * q, nk),
        in_specs=[pl.BlockSpec((tm, tk), lambda i, j, k: (i, k)), pl.BlockSpec((None, tk, tn), lambda i, j, k: (j // q, k, j % q))],
        out_specs=pl.BlockSpec((tm, tn), lambda i, j, k: (i, j)),
        out_shape=_sds((m, nb * cl), out_dtype),
        scratch=[] if nk == 1 else [pltpu.VMEM((tm, tn), F32)])(x, g)


def _mm_dyw(dy, g, out_dtype, name, init=None, tm_c=(768, 512, 384, 256, 128), tn_c=(1024, 512, 256, 128), tk_c=(2048, 1536, 1024, 768, 512, 256, 128)):
    m, n = dy.shape
    nb, r, cl = g.shape
    assert n == nb * cl
    tm, tn, tk = _pick(m, tm_c), _pick(r, tn_c), _pick(cl, tk_c)
    q = cl // tk
    nk = nb * q
    has_init = init is not None

    def body(*refs):
        if has_init:
            dy_ref, g_ref, i_ref, o_ref, acc = refs
        else:
            dy_ref, g_ref, o_ref, acc = refs
        k = pl.program_id(2)
        p = lax.dot_general(dy_ref[...].astype(BF16), g_ref[...], (((1,), (1,)), ((), ())), preferred_element_type=F32)

        @pl.when(k == 0)
        def _():
            acc[...] = p + i_ref[...].astype(F32) if has_init else p

        @pl.when(k > 0)
        def _():
            acc[...] += p

        @pl.when(k == nk - 1)
        def _():
            o_ref[...] = acc[...].astype(o_ref.dtype)

    in_specs = [pl.BlockSpec((tm, tk), lambda i, j, k: (i, k)), pl.BlockSpec((None, tn, tk), lambda i, j, k: (k // q, j, k % q))]
    args = [dy, g]
    if has_init:
        in_specs.append(pl.BlockSpec((tm, tn), lambda i, j, k: (i, j)))
        args.append(init)
    return _pcall(body, name=name, grid=(m // tm, r // tn, nk), in_specs=in_specs,
                  out_specs=pl.BlockSpec((tm, tn), lambda i, j, k: (i, j)), out_shape=_sds((m, r), out_dtype),
                  scratch=[pltpu.VMEM((tm, tn), F32)])(*args)


def _mm_xtdy(x, dy, nb, out_dtype, name, tm_c=(1024, 512, 256, 128), tn_c=(768, 512, 256, 128), tk_c=(2304, 2048, 1152, 1024, 768, 512, 256, 128, 16)):
    t, r = x.shape
    t2, n = dy.shape
    assert t == t2 and n % nb == 0
    cl = n // nb
    tm, tn, tk = _pick(r, tm_c), _pick(cl, tn_c), _pick(t, tk_c)
    q, nk = cl // tn, t // tk

    def body(x_ref, dy_ref, o_ref, *acc):
        p = lax.dot_general(x_ref[...].astype(BF16), dy_ref[...].astype(BF16), (((0,), (0,)), ((), ())), preferred_element_type=F32)
        if nk == 1:
            o_ref[...] = p.astype(o_ref.dtype)
        else:
            k = pl.program_id(2)

            @pl.when(k == 0)
            def _():
                acc[0][...] = p

            @pl.when(k > 0)
            def _():
                acc[0][...] += p

            @pl.when(k == nk - 1)
            def _():
                o_ref[...] = acc[0][...].astype(o_ref.dtype)

    return _pcall(
        body, name=name, grid=(r // tm, nb * q, nk),
        in_specs=[pl.BlockSpec((tk, tm), lambda i, j, k: (k, i)), pl.BlockSpec((tk, tn), lambda i, j, k: (k, j))],
        out_specs=pl.BlockSpec((None, tm, tn), lambda i, j, k: (j // q, i, j % q)),
        out_shape=_sds((nb, r, cl), out_dtype),
        scratch=[] if nk == 1 else [pltpu.VMEM((tm, tn), F32)])(x, dy)


def _mm_f32(a, b, name, trans_b=False):
    dims = (((1,), (1,)), ((), ())) if trans_b else (((1,), (0,)), ((), ()))
    n = b.shape[0] if trans_b else b.shape[1]

    def body(a_ref, b_ref, o_ref):
        o_ref[...] = lax.dot_general(a_ref[...], b_ref[...], dims, precision=HIGHEST, preferred_element_type=F32)

    return _pcall(body, name=name, out_shape=_sds((a.shape[0], n), F32))(a, b)


def _tiled(fn, ins, outs, grid, name, acc=()):
    n_in = len(ins)
    grid = tuple(grid) or (1,)
    nd = len(grid)

    def body(*refs):
        vals = fn(*[r[...] for r in refs[:n_in]])
        if not isinstance(vals, (tuple, list)):
            vals = (vals,)
        first = None
        for o, (ref, v) in enumerate(zip(refs[n_in:], vals)):
            if o in acc:
                if first is None:
                    first = pl.program_id(0) == 0
                    for a in range(1, nd):
                        first = jnp.logical_and(first, pl.program_id(a) == 0)

                @pl.when(first)
                def _(ref=ref, v=v):
                    ref[...] = v.astype(ref.dtype)

                @pl.when(jnp.logical_not(first))
                def _(ref=ref, v=v):
                    ref[...] += v.astype(ref.dtype)
            else:
                ref[...] = v.astype(ref.dtype)

    res = _pcall(body, name=name, grid=grid,
                 in_specs=[pl.BlockSpec(b, im) for _, b, im in ins],
                 out_specs=[pl.BlockSpec(b, im) for _, _, b, im in outs],
                 out_shape=[_sds(s, d) for s, d, _, _ in outs])(*[a for a, _, _ in ins])
    return list(res)


def _rows(a, tr):
    return (a, (tr, a.shape[1]), lambda i, *_: (i, 0))


def _const(a):
    nd = a.ndim
    return (a, a.shape, lambda *_: (0,) * nd)


def _cast_bf16(w, name):
    r, c = w.shape
    tr = _pick(r, (256, 128, 64, 16))
    return _tiled(lambda v: v, [_rows(w, tr)], [((r, c), BF16, (tr, c), lambda i: (i, 0))], (r // tr,), name)[0]


def _f_norm_mod(x, g, sh, sc):
    y = x * lax.rsqrt(jnp.mean(x * x, axis=-1, keepdims=True) + EPS) * g
    return y * (1.0 + sc) + sh


def _rope_partner_impl(y):
    nf = HEAD_DIM // 4
    lane = lax.broadcasted_iota(jnp.int32, y.shape, 1)
    return jnp.where(lane % (2 * nf) < nf, pltpu.roll(y, HEAD_DIM - nf, 1), pltpu.roll(y, nf, 1))


_rope_partner = jax.custom_vjp(_rope_partner_impl)
_rope_partner.defvjp(lambda y: (_rope_partner_impl(y), None), lambda _, ct: (_rope_partner_impl(ct),))


def _f_qk(t, g, cos, sin):
    y = t * lax.rsqrt(jnp.mean(t * t, axis=-1, keepdims=True) + EPS) * g
    return y * cos + _rope_partner(y) * sin


def _f_readout(of, ob, gate, g):
    o = of + ob
    on = o * lax.rsqrt(jnp.mean(o * o, axis=-1, keepdims=True) + EPS) * g
    return on * (gate * jax.nn.sigmoid(gate))


def _f_merge(ga, gb, pa, pb):
    return jax.nn.sigmoid(ga) * pa + jax.nn.sigmoid(gb) * pb


def _f_resid(x, gate, m):
    return x + gate * m


def _norm_mod_fwd(xcat, g, mods, n_ctx_tiles, tr):
    tt, d = xcat.shape
    which = lambda i: (jnp.where(i >= n_ctx_tiles, 1, 0), 0, 0, 0)

    def fn(x, gg, md):
        return _f_norm_mod(x, gg, md[0], md[1])

    return _tiled(fn, [_rows(xcat, tr), _const(g), (mods, (None, 2, 1, d), which)],
                  [((tt, d), BF16, (tr, d), lambda i: (i, 0))], (tt // tr,), "norm_mod_fwd")[0]


def _norm_mod_bwd(xcat, g, mods, dh, extra, n_ctx_tiles, tr, name):
    tt, d = xcat.shape
    nt = tt // tr
    has_extra = extra is not None

    def body(*refs):
        if has_extra:
            x_ref, g_ref, m_ref, dh_ref, e_ref, dx_ref, dg_ref, dm_ref = refs
        else:
            x_ref, g_ref, m_ref, dh_ref, dx_ref, dg_ref, dm_ref = refs
        i = pl.program_id(0)
        md = m_ref[...]
        _, vjp = jax.vjp(_f_norm_mod, x_ref[...], g_ref[...], md[0], md[1])
        dx, dg, dsh, dsc = vjp(dh_ref[...].astype(F32))
        dx_ref[...] = dx + e_ref[...] if has_extra else dx

        @pl.when(i == 0)
        def _():
            dg_ref[...] = dg

        @pl.when(i > 0)
        def _():
            dg_ref[...] += dg

        fresh = jnp.logical_or(i == 0, i == n_ctx_tiles)

        @pl.when(fresh)
        def _():
            dm_ref[0] = dsh
            dm_ref[1] = dsc

        @pl.when(jnp.logical_not(fresh))
        def _():
            dm_ref[0] += dsh
            dm_ref[1] += dsc

    which = lambda i: (jnp.where(i >= n_ctx_tiles, 1, 0), 0, 0, 0)
    row = pl.BlockSpec((tr, d), lambda i: (i, 0))
    in_specs = [row, pl.BlockSpec((1, d), lambda i: (0, 0)), pl.BlockSpec((None, 2, 1, d), which), row]
    args = [xcat, g, mods, dh]
    if has_extra:
        in_specs.append(row)
        args.append(extra)
    return _pcall(body, name=name, grid=(nt,), in_specs=in_specs,
                  out_specs=[row, pl.BlockSpec((1, d), lambda i: (0, 0)), pl.BlockSpec((None, 2, 1, d), which)],
                  out_shape=[_sds((tt, d), F32), _sds((1, d), F32), _sds((2, 2, 1, d), F32)])(*args)


def _hgrn_tri(reverse):
    t = np.arange(HGRN_BLOCK)
    tri = (t[None, :] >= t[:, None]) if reverse else (t[None, :] <= t[:, None])
    tri = tri.astype(np.float32)
    return jnp.asarray(tri), jnp.asarray(tri.T.copy())


def _hgrn_rowblock(n, n_ctx_blocks, n_blocks, reverse):
    if not reverse:
        return n
    return jnp.where(n < n_ctx_blocks, n_ctx_blocks - 1 - n, n_blocks - 1 - n + n_ctx_blocks)


def _hgrn_gates(fl, lb):
    sg = jax.nn.sigmoid(fl)
    f = lb + (1.0 - lb) * sg
    return sg, f, jnp.log(f), 1.0 - f


def _hgrn_intra_mask(reverse):
    tio = lax.broadcasted_iota(jnp.int32, (HGRN_BLOCK, HEAD_DIM), 0)
    return (lambda s: tio <= s) if reverse else (lambda s: tio >= s)


class _Halves:
    def __init__(self, reverse):
        self.reverse = reverse
        self.h = HGRN_BLOCK // 2
        tio = lax.broadcasted_iota(jnp.int32, (self.h, HEAD_DIM), 0)
        self.tio = tio if reverse else tio + self.h

    def is_half(self, s):
        return s < self.h if self.reverse else s >= self.h

    def rows(self, a):
        return a[:self.h] if self.reverse else a[self.h:]

    def mask(self, s):
        return self.tio <= s if self.reverse else self.tio >= s

    def widen(self, full, half):
        z = jnp.zeros_like(half)
        return full + jnp.concatenate([half, z] if self.reverse else [z, half], axis=0)


def _unrolled(nblk, u, fn, init):
    assert nblk % u == 0

    def trip(b, c):
        for j in range(u):
            c = fn(b * u + j, j, c)
        return c

    return lax.fori_loop(0, nblk // u, trip, init)


HGRN_UNROLL_FWD = (4, 8)
HGRN_UNROLL_BWD = (8, 4)


def _hgrn_fwd(p, lb, seg_f, reverse, n_ctx_rows):
    tt = p.shape[0]
    hb = HGRN_BLOCK
    nblk, nctx = tt // hb, n_ctx_rows // hb
    tri, _ = _hgrn_tri(reverse)
    u1, u3 = HGRN_UNROLL_FWD

    def body(q_ref, f_ref, v_ref, lb_ref, tri_ref, o_ref, st_all, dec_all, cum_out, qe_buf, cum_blk, k_blk, v_blk):
        mask = _hgrn_intra_mask(reverse)
        hv = _Halves(reverse)
        lbv = lb_ref[...]

        def phase1(n, slot, c):
            r0 = pl.multiple_of(_hgrn_rowblock(n, nctx, nblk, reverse) * hb, hb)
            q, v = q_ref[pl.ds(r0, hb), :], v_ref[pl.ds(r0, hb), :]
            _, f, g, k = _hgrn_gates(f_ref[pl.ds(r0, hb), :], lbv)
            cum = jnp.dot(tri_ref[...], g, precision=HIGHEST, preferred_element_type=F32)
            tot = jnp.sum(g, axis=0, keepdims=True)
            cum_blk[slot] = cum
            k_blk[slot] = k
            v_blk[slot] = v
            oi = jnp.zeros((hb, HEAD_DIM), F32)
            oi_h = jnp.zeros((hb // 2, HEAD_DIM), F32)
            q_h, cum_h = hv.rows(q), hv.rows(cum)
            for s in range(hb):
                if hv.is_half(s):
                    e = jnp.where(hv.mask(s), jnp.exp(jnp.minimum(cum_h - cum_blk[slot, s:s + 1, :], 0.0)), 0.0)
                    a_s = jnp.sum(q_h * e * k_blk[slot, s:s + 1, :], axis=-1, keepdims=True)
                    oi_h = oi_h + a_s * v_blk[slot, s:s + 1, :]
                else:
                    e = jnp.where(mask(s), jnp.exp(jnp.minimum(cum - cum_blk[slot, s:s + 1, :], 0.0)), 0.0)
                    a_s = jnp.sum(q * e * k_blk[slot, s:s + 1, :], axis=-1, keepdims=True)
                    oi = oi + a_s * v_blk[slot, s:s + 1, :]
            o_ref[pl.ds(r0, hb), :] = hv.widen(oi, oi_h)
            cum_out[pl.ds(r0, hb), :] = cum
            qe_buf[pl.ds(r0, hb), :] = q * jnp.exp(cum)
            kl = k * jnp.exp(tot - cum)
            st_all[n] = lax.dot_general(v.astype(BF16), kl.astype(BF16), (((0,), (0,)), ((), ())), preferred_element_type=F32)
            dec_all[pl.ds(n, 1), :] = jnp.exp(tot)
            return c

        _unrolled(nblk, u1, phase1, 0)

        def phase2(n, st):
            kv = st_all[n]
            st_all[n] = st
            return st * dec_all[pl.ds(n, 1), :] + kv

        lax.fori_loop(0, nblk, phase2, jnp.zeros((HEAD_DIM, HEAD_DIM), F32))

        def phase3(n, slot, c):
            r0 = pl.multiple_of(_hgrn_rowblock(n, nctx, nblk, reverse) * hb, hb)
            o_ref[pl.ds(r0, hb), :] += lax.dot_general(qe_buf[pl.ds(r0, hb), :].astype(BF16), st_all[n].astype(BF16),
                                                        (((1,), (1,)), ((), ())), preferred_element_type=F32)
            return c

        _unrolled(nblk, u3, phase3, 0)

    col = lambda seg: pl.BlockSpec((tt, HEAD_DIM), lambda h, seg=seg: (0, seg * N_HEADS + h))
    blk = pltpu.VMEM((u1, hb, HEAD_DIM), F32)
    head = pl.BlockSpec((tt, HEAD_DIM), lambda h: (0, h))
    return _pcall(
        body, name="hgrn_fwd_rev" if reverse else "hgrn_fwd", grid=(N_HEADS,),
        in_specs=[col(0), col(seg_f), col(3), pl.BlockSpec((1, HEAD_DIM), lambda h: (0, h)), pl.BlockSpec((hb, hb), lambda h: (0, 0))],
        out_specs=[head, pl.BlockSpec((None, nblk, HEAD_DIM, HEAD_DIM), lambda h: (h, 0, 0, 0)),
                   pl.BlockSpec((None, nblk, HEAD_DIM), lambda h: (h, 0, 0)), head],
        out_shape=[_sds((tt, N_HEADS * HEAD_DIM), F32), _sds((N_HEADS, nblk, HEAD_DIM, HEAD_DIM), F32),
                   _sds((N_HEADS, nblk, HEAD_DIM), F32), _sds((tt, N_HEADS * HEAD_DIM), F32)],
        scratch=[pltpu.VMEM((tt, HEAD_DIM), F32), blk, blk, blk])(p, p, p, lb, tri)


def _hgrn_bwd(p, lb, do, saved, seg_f, reverse, n_ctx_rows, prev):
    tt = p.shape[0]
    hb = HGRN_BLOCK
    nblk, nctx = tt // hb, n_ctx_rows // hb
    _, tri_t = _hgrn_tri(reverse)
    last_row = 0 if reverse else hb - 1
    has_prev = prev is not None
    u1, u3 = HGRN_UNROLL_BWD

    def body(*refs):
        q_ref, f_ref, v_ref, lb_ref, trit_ref, do_ref, st_all, dec_all, cum_buf = refs[:9]
        refs = refs[9:]
        if has_prev:
            pq_ref, pv_ref = refs[:2]
            refs = refs[2:]
        dq_ref, dfl_ref, dv_ref, dlb_ref, dd_all, cum_blk, k_blk, v_blk, dk_blk, dv_blk = refs
        mask = _hgrn_intra_mask(reverse)
        hv = _Halves(reverse)
        lbv = lb_ref[...]
        tio = lax.broadcasted_iota(jnp.int32, (hb, HEAD_DIM), 0)

        def rows_of(n):
            rb = _hgrn_rowblock(n, nctx, nblk, reverse)
            return rb, pl.multiple_of(rb * hb, hb)

        def load_do(rb):
            lat0 = pl.multiple_of(jnp.maximum(rb - nctx, 0) * hb, hb)
            return jnp.where(rb >= nctx, do_ref[pl.ds(lat0, hb), :], 0.0)

        def phase1(n, slot, c):
            rb, r0 = rows_of(n)
            qe = q_ref[pl.ds(r0, hb), :] * jnp.exp(cum_buf[pl.ds(r0, hb), :])
            dd_all[n] = lax.dot_general(load_do(rb).astype(BF16), qe.astype(BF16), (((0,), (0,)), ((), ())), preferred_element_type=F32)
            return c

        _unrolled(nblk, u1, phase1, 0)

        def phase2r(i, dst):
            n = nblk - 1 - i
            u = dd_all[n]
            dd_all[n] = dst
            return u + dst * dec_all[pl.ds(n, 1), :]

        lax.fori_loop(0, nblk, phase2r, jnp.zeros((HEAD_DIM, HEAD_DIM), F32))

        def phase3(n, slot, dlb):
            rb, r0 = rows_of(n)
            q, v = q_ref[pl.ds(r0, hb), :], v_ref[pl.ds(r0, hb), :]
            sg, f, g, k = _hgrn_gates(f_ref[pl.ds(r0, hb), :], lbv)
            cum = cum_buf[pl.ds(r0, hb), :]
            tot = jnp.sum(g, axis=0, keepdims=True)
            dob = load_do(rb)
            st, dst = st_all[n], dd_all[n]
            e_cum = jnp.exp(cum)
            e_rest = jnp.exp(tot - cum)
            dq = jnp.dot(dob.astype(BF16), st.astype(BF16), preferred_element_type=F32) * e_cum
            dk_inter = jnp.dot(v.astype(BF16), dst.astype(BF16), preferred_element_type=F32) * e_rest
            dv = lax.dot_general((k * e_rest).astype(BF16), dst.astype(BF16), (((1,), (1,)), ((), ())), preferred_element_type=F32)
            cum_blk[slot] = cum
            k_blk[slot] = k
            v_blk[slot] = v
            dq_h = jnp.zeros((hb // 2, HEAD_DIM), F32)
            q_h, cum_h, dob_h = hv.rows(q), hv.rows(cum), hv.rows(dob)
            for s in range(hb):
                half = hv.is_half(s)
                qs, cs, ds = (q_h, cum_h, dob_h) if half else (q, cum, dob)
                e = jnp.where(hv.mask(s) if half else mask(s), jnp.exp(jnp.minimum(cs - cum_blk[slot, s:s + 1, :], 0.0)), 0.0)
                a_s = jnp.sum(qs * e * k_blk[slot, s:s + 1, :], axis=-1, keepdims=True)
                da_s = jnp.sum(ds * v_blk[slot, s:s + 1, :], axis=-1, keepdims=True)
                gs = da_s * e
                if half:
                    dq_h = dq_h + gs * k_blk[slot, s:s + 1, :]
                else:
                    dq = dq + gs * k_blk[slot, s:s + 1, :]
                dk_blk[slot, s:s + 1, :] = jnp.sum(gs * qs, axis=0, keepdims=True)
                dv_blk[slot, s:s + 1, :] = jnp.sum(a_s * ds, axis=0, keepdims=True)
            dq = hv.widen(dq, dq_h)
            dk = dk_inter + dk_blk[slot]
            dv = dv + dv_blk[slot]
            d_tot = jnp.sum(k * dk_inter, axis=0, keepdims=True) + jnp.exp(tot) * jnp.sum(dst * st, axis=0, keepdims=True)
            dcum = q * dq - k * dk + jnp.where(tio == last_row, d_tot, 0.0)
            dg = jnp.dot(trit_ref[...], dcum, precision=HIGHEST, preferred_element_type=F32)
            df = dg / f - dk
            if has_prev:
                dq = dq + pq_ref[pl.ds(r0, hb), :].astype(F32)
                dv = dv + pv_ref[pl.ds(r0, hb), :].astype(F32)
            dq_ref[pl.ds(r0, hb), :] = dq.astype(dq_ref.dtype)
            dv_ref[pl.ds(r0, hb), :] = dv.astype(dv_ref.dtype)
            dfl_ref[pl.ds(r0, hb), :] = (df * (1.0 - lbv) * sg * (1.0 - sg)).astype(dfl_ref.dtype)
            return dlb + jnp.sum(df * (1.0 - sg), axis=0, keepdims=True)

        dlb_ref[...] = _unrolled(nblk, u3, phase3, jnp.zeros((1, HEAD_DIM), F32))

    col = lambda seg: pl.BlockSpec((tt, HEAD_DIM), lambda h, seg=seg: (0, seg * N_HEADS + h))
    head = pl.BlockSpec((tt, HEAD_DIM), lambda h: (0, h))
    lbs = pl.BlockSpec((1, HEAD_DIM), lambda h: (0, h))
    tris = pl.BlockSpec((hb, hb), lambda h: (0, 0))
    in_specs = [col(0), col(seg_f), col(3), lbs, tris, pl.BlockSpec((do.shape[0], HEAD_DIM), lambda h: (0, h)),
                pl.BlockSpec((None, nblk, HEAD_DIM, HEAD_DIM), lambda h: (h, 0, 0, 0)), pl.BlockSpec((None, nblk, HEAD_DIM), lambda h: (h, 0, 0)), head]
    args = [p, p, p, lb, tri_t, do, *saved]
    mid = F32 if not has_prev else BF16
    if has_prev:
        in_specs += [head, head]
        args += list(prev)
    w = N_HEADS * HEAD_DIM
    blk = pltpu.VMEM((u3, hb, HEAD_DIM), F32)
    return _pcall(
        body, name="hgrn_bwd_rev" if reverse else "hgrn_bwd", grid=(N_HEADS,), in_specs=in_specs,
        out_specs=[head, head, head, lbs],
        out_shape=[_sds((tt, w), mid), _sds((tt, w), BF16), _sds((tt, w), mid), _sds((1, w), F32)],
        scratch=[pltpu.VMEM((nblk, HEAD_DIM, HEAD_DIM), F32), blk, blk, blk, blk, blk])(*args)


NA_HEADS_PER_STEP = 4


def _na_geometry(rows):
    r = pl.program_id(1)
    rs = jnp.clip(r - WIN_R // 2, 0, rows - WIN_R)
    return r, rs, r - rs


def _na_scores(q, kb, kc, bias):
    scale = HEAD_DIM ** -0.5
    nt = (((1,), (1,)), ((), ()))
    sb = lax.dot_general(q, kb, nt, preferred_element_type=F32) * scale + bias
    sc = lax.dot_general(q, kc, nt, preferred_element_type=F32) * scale
    m = jnp.maximum(jnp.max(sb, axis=-1, keepdims=True), jnp.max(sc, axis=-1, keepdims=True))
    pb, pc = jnp.exp(sb - m), jnp.exp(sc - m)
    inv = 1.0 / (jnp.sum(pb, axis=-1, keepdims=True) + jnp.sum(pc, axis=-1, keepdims=True))
    return pb * inv, pc * inv


def _na_fwd(qn, kall, p, seg_v, bias, n_ctx_rows):
    t, tt = qn.shape[0], kall.shape[0]
    rows = t // GRID_W
    nband = WIN_R * GRID_W

    nh = NA_HEADS_PER_STEP
    wide = nh * HEAD_DIM

    def body(q_ref, k_ref, v_ref, b_ref, o_ref):
        r, rs, _ = _na_geometry(rows)
        k0 = pl.multiple_of(n_ctx_rows + rs * GRID_W, GRID_W)
        for j in range(nh):
            sl = slice(j * HEAD_DIM, (j + 1) * HEAD_DIM)
            q = q_ref[:, sl]
            kb, kc = k_ref[pl.ds(k0, nband), sl], k_ref[pl.ds(0, n_ctx_rows), sl]
            vb, vc = v_ref[pl.ds(k0, nband), sl].astype(BF16), v_ref[pl.ds(0, n_ctx_rows), sl].astype(BF16)
            pb, pc = _na_scores(q, kb, kc, b_ref[j])
            o = jnp.dot(pb.astype(BF16), vb, preferred_element_type=F32) + jnp.dot(pc.astype(BF16), vc, preferred_element_type=F32)
            o_ref[:, sl] = o.astype(o_ref.dtype)

    variant = lambda h, r: (h, r - jnp.clip(r - WIN_R // 2, 0, rows - WIN_R), 0, 0)
    return _pcall(
        body, name="na_fwd", grid=(N_HEADS // nh, rows),
        in_specs=[pl.BlockSpec((GRID_W, wide), lambda h, r: (r, h)),
                  pl.BlockSpec((tt, wide), lambda h, r: (0, h)),
                  pl.BlockSpec((tt, wide), lambda h, r: (0, seg_v * (N_HEADS // nh) + h)),
                  pl.BlockSpec((nh, None, GRID_W, nband), variant)],
        out_specs=pl.BlockSpec((GRID_W, wide), lambda h, r: (r, h)),
        out_shape=_sds((t, N_HEADS * HEAD_DIM), BF16))(qn, kall, p, bias)


def _na_bwd(qn, kall, p, seg_v, bias, do, n_ctx_rows):
    t, tt = qn.shape[0], kall.shape[0]
    rows = t // GRID_W
    nband = WIN_R * GRID_W
    scale = HEAD_DIM ** -0.5
    tn = (((0,), (0,)), ((), ()))
    nt = (((1,), (1,)), ((), ()))

    nh = NA_HEADS_PER_STEP
    wide = nh * HEAD_DIM

    def body(q_ref, k_ref, v_ref, b_ref, do_ref, dq_ref, dk_ref, dv_ref, db_ref, dv_acc):
        r, rs, var = _na_geometry(rows)
        k0 = pl.multiple_of(n_ctx_rows + rs * GRID_W, GRID_W)
        fresh = jnp.logical_or(r <= WIN_R // 2, r > rows - WIN_R // 2)

        @pl.when(r == 0)
        def _():
            dk_ref[...] = jnp.zeros_like(dk_ref)
            dv_acc[...] = jnp.zeros_like(dv_acc)

        for j in range(nh):
            sl = slice(j * HEAD_DIM, (j + 1) * HEAD_DIM)
            q = q_ref[:, sl]
            kb, kc = k_ref[pl.ds(k0, nband), sl], k_ref[pl.ds(0, n_ctx_rows), sl]
            vb, vc = v_ref[pl.ds(k0, nband), sl].astype(BF16), v_ref[pl.ds(0, n_ctx_rows), sl].astype(BF16)
            pb, pc = _na_scores(q, kb, kc, b_ref[j])
            dof = do_ref[:, sl].astype(F32)
            dob = dof.astype(BF16)
            o = jnp.dot(pb.astype(BF16), vb, preferred_element_type=F32) + jnp.dot(pc.astype(BF16), vc, preferred_element_type=F32)
            delta = jnp.sum(dof * o, axis=-1, keepdims=True)
            dsb = pb * (lax.dot_general(dob, vb, nt, preferred_element_type=F32) - delta)
            dsc = pc * (lax.dot_general(dob, vc, nt, preferred_element_type=F32) - delta)
            dsb16, dsc16 = dsb.astype(BF16), dsc.astype(BF16)
            dq_ref[:, sl] = (jnp.dot(dsb16, kb, preferred_element_type=F32) + jnp.dot(dsc16, kc, preferred_element_type=F32)) * scale
            dk_ref[pl.ds(k0, nband), sl] += lax.dot_general(dsb16, q, tn, preferred_element_type=F32) * scale
            dk_ref[pl.ds(0, n_ctx_rows), sl] += lax.dot_general(dsc16, q, tn, preferred_element_type=F32) * scale
            dv_acc[pl.ds(k0, nband), sl] += lax.dot_general(pb.astype(BF16), dob, tn, preferred_element_type=F32)
            dv_acc[pl.ds(0, n_ctx_rows), sl] += lax.dot_general(pc.astype(BF16), dob, tn, preferred_element_type=F32)

            @pl.when(fresh)
            def _(j=j, dsb=dsb):
                db_ref[j] = dsb

            @pl.when(jnp.logical_not(fresh))
            def _(j=j, dsb=dsb):
                db_ref[j] += dsb

        @pl.when(r == rows - 1)
        def _():
            dv_ref[...] = dv_acc[...].astype(dv_ref.dtype)

    variant = lambda h, r: (h, r - jnp.clip(r - WIN_R // 2, 0, rows - WIN_R), 0, 0)
    head_all = pl.BlockSpec((tt, wide), lambda h, r: (0, h))
    qspec = pl.BlockSpec((GRID_W, wide), lambda h, r: (r, h))
    w = N_HEADS * HEAD_DIM
    return _pcall(
        body, name="na_bwd", grid=(N_HEADS // nh, rows),
        in_specs=[qspec, head_all, pl.BlockSpec((tt, wide), lambda h, r: (0, seg_v * (N_HEADS // nh) + h)),
                  pl.BlockSpec((nh, None, GRID_W, nband), variant), qspec],
        out_specs=[qspec, head_all, head_all, pl.BlockSpec((nh, None, GRID_W, nband), variant)],
        out_shape=[_sds((t, w), F32), _sds((tt, w), F32), _sds((tt, w), BF16), _sds((N_HEADS, WIN_R, GRID_W, nband), F32)],
        scratch=[pltpu.VMEM((tt, wide), F32)])(qn, kall, p, bias, do)


def _na_tables(t, n_ctx_rows):
    half, nf = HEAD_DIM // 2, HEAD_DIM // 4
    pos = np.arange(t)
    lane = np.arange(HEAD_DIM)
    inv = ROPE_THETA ** (-(np.arange(nf, dtype=np.float32)) / nf)
    which = np.where(lane < half, pos[:, None] // GRID_W, pos[:, None] % GRID_W).astype(np.float32)
    ang = which * inv[lane % nf][None, :]
    first = (lane % half) < nf
    cos = np.concatenate([np.ones((n_ctx_rows, HEAD_DIM), np.float32), np.cos(ang).astype(np.float32)])
    sin = np.concatenate([np.zeros((n_ctx_rows, HEAD_DIM), np.float32), np.where(first[None, :], -np.sin(ang), np.sin(ang)).astype(np.float32)])
    w = np.arange(GRID_W)
    dc = np.clip(w[None, :] - w[:, None], -(WIN_C - 1), WIN_C - 1) + WIN_C - 1
    onehot = np.zeros((32, GRID_W * GRID_W), np.float32)
    onehot[dc.reshape(-1), np.arange(GRID_W * GRID_W)] = 1.0
    cs = np.clip(w - WIN_C // 2, 0, GRID_W - WIN_C)
    col_in = (w[None, :] >= cs[:, None]) & (w[None, :] < cs[:, None] + WIN_C)
    onehot *= col_in.reshape(1, -1)
    neg = np.where(col_in, 0.0, NEG_BIG).astype(np.float32)
    return jnp.asarray(cos), jnp.asarray(sin), jnp.asarray(onehot), jnp.asarray(neg)


def _bias_slabs(rel_bias, onehot, neg):
    nr = 2 * WIN_R - 1
    rb = jnp.pad(rel_bias.reshape(N_HEADS * nr, 2 * WIN_C - 1), ((0, 0), (0, 1)))
    spread = _mm_f32(rb, onehot, "bias_spread").reshape(N_HEADS, nr, GRID_W, GRID_W)
    slabs = [spread[:, WIN_R - 1 - v:2 * WIN_R - 1 - v] for v in range(WIN_R)]
    b = jnp.stack(slabs, axis=1) + neg[None, None, None]
    return b.transpose(0, 1, 3, 2, 4).reshape(N_HEADS, WIN_R, GRID_W, WIN_R * GRID_W)


def _bias_grad(dbias, onehot):
    nr = 2 * WIN_R - 1
    d = dbias.reshape(N_HEADS, WIN_R, GRID_W, WIN_R, GRID_W).transpose(0, 1, 3, 2, 4)
    tot = jnp.zeros((N_HEADS, nr, GRID_W, GRID_W), F32)
    for v in range(WIN_R):
        tot = tot + jnp.pad(d[:, v], ((0, 0), (WIN_R - 1 - v, v), (0, 0), (0, 0)))
    g = _mm_f32(tot.reshape(N_HEADS * nr, GRID_W * GRID_W), onehot, "bias_grad", trans_b=True)
    return g[:, :2 * WIN_C - 1].reshape(1, N_HEADS, nr, 2 * WIN_C - 1)


def _shift_rows(u, up):
    n = u.shape[0]
    tio = lax.broadcasted_iota(jnp.int32, u.shape, 0)
    if up:
        return jnp.where(tio == n - 1, 0.0, pltpu.roll(u, n - 1, 0))
    return jnp.where(tio == 0, 0.0, pltpu.roll(u, 1, 0))


def _conv3(u, w_ref, b_ref):
    um, up = _shift_rows(u, False), _shift_rows(u, True)
    return um, up, um * w_ref[0:1, :] + u * w_ref[1:2, :] + up * w_ref[2:3, :] + b_ref[...]


def _ffn_act_fwd(u0, t3, cw, cb):
    t, n = u0.shape
    tc = _pick(n, (256, 128))

    def body(u_ref, t_ref, w_ref, b_ref, a_ref):
        _, _, uc = _conv3(u_ref[...], w_ref, b_ref)
        a_ref[...] = (uc * jax.nn.sigmoid(uc) * t_ref[...]).astype(a_ref.dtype)

    col = lambda rows_: pl.BlockSpec((rows_, tc), lambda j: (0, j))
    return _pcall(body, name="ffn_act_fwd", grid=(n // tc,), in_specs=[col(t), col(t), col(8), col(1)], out_specs=col(t),
                  out_shape=_sds((t, n), BF16))(u0, t3, cw, cb)


def _ffn_act_bwd(u0, t3, cw, cb, da):
    t, n = u0.shape
    tc = _pick(n, (256, 128))

    def body(u_ref, t_ref, w_ref, b_ref, da_ref, du_ref, dt_ref, dw_ref, db_ref):
        u = u_ref[...]
        um, up, uc = _conv3(u, w_ref, b_ref)
        sg = jax.nn.sigmoid(uc)
        dav = da_ref[...].astype(F32)
        dt_ref[...] = (dav * uc * sg).astype(dt_ref.dtype)
        duc = dav * t_ref[...] * sg * (1.0 + uc * (1.0 - sg))
        du = _shift_rows(duc, True) * w_ref[0:1, :] + duc * w_ref[1:2, :] + _shift_rows(duc, False) * w_ref[2:3, :]
        du_ref[...] = du.astype(du_ref.dtype)
        dw_ref[...] = jnp.zeros_like(dw_ref)
        dw_ref[0:1, :] = jnp.sum(duc * um, axis=0, keepdims=True)
        dw_ref[1:2, :] = jnp.sum(duc * u, axis=0, keepdims=True)
        dw_ref[2:3, :] = jnp.sum(duc * up, axis=0, keepdims=True)
        db_ref[...] = jnp.sum(duc, axis=0, keepdims=True)

    col = lambda rows_: pl.BlockSpec((rows_, tc), lambda j: (0, j))
    return _pcall(body, name="ffn_act_bwd", grid=(n // tc,), in_specs=[col(t), col(t), col(8), col(1), col(t)],
                  out_specs=[col(t), col(t), col(8), col(1)],
                  out_shape=[_sds((t, n), BF16), _sds((t, n), BF16), _sds((8, n), F32), _sds((1, n), F32)])(u0, t3, cw, cb, da)


def _adam_math(g, w, m, v):
    m2 = ADAM_B1 * m + (1.0 - ADAM_B1) * g
    v2 = ADAM_B2 * v + (1.0 - ADAM_B2) * (g * g)
    m_hat = m2 / (1.0 - ADAM_B1 ** ADAM_STEP)
    v_hat = v2 / (1.0 - ADAM_B2 ** ADAM_STEP)
    return -ADAM_LR * (m_hat / (jnp.sqrt(v_hat) + ADAM_EPS) + ADAM_WD * w), m2, v2


def _adam_big(parts, w, m, v, name):
    r, c = w.shape
    npart, _, cp = parts.shape
    tr = _pick(r, (64, 32, 16, 8))

    def body(p_ref, w_ref, m_ref, v_ref, g_ref, d_ref, m2_ref, v2_ref):
        g = p_ref[0, :, 0:c].astype(F32)
        for i in range(1, npart):
            g = g + p_ref[i, :, 0:c].astype(F32)
        d, m2, v2 = _adam_math(g, w_ref[...], m_ref[...], v_ref[...])
        g_ref[...] = g
        d_ref[...] = d
        m2_ref[...] = m2
        v2_ref[...] = v2

    row = pl.BlockSpec((tr, c), lambda i: (i, 0))
    return _pcall(body, name=name, grid=(r // tr,),
                  in_specs=[pl.BlockSpec((npart, tr, cp), lambda i: (0, i, 0)), row, row, row],
                  out_specs=[row] * 4, out_shape=[_sds((r, c), F32)] * 4)(parts, w, m, v)


def _adam_small(g, w, m, v):
    def body(g_ref, w_ref, m_ref, v_ref, d_ref, m2_ref, v2_ref):
        d_ref[...], m2_ref[...], v2_ref[...] = _adam_math(g_ref[...], w_ref[...], m_ref[...], v_ref[...])

    return _pcall(body, name="adam_small", out_shape=[_sds(g.shape, F32)] * 3)(g, w, m, v)


def _sum_parts(parts, name):
    def body(p_ref, o_ref):
        s = p_ref[0]
        for i in range(1, N_DEV):
            s = s + p_ref[i]
        o_ref[...] = s

    return _pcall(body, name=name, out_shape=_sds(parts.shape[1:], F32))(parts)


class _Pack:
    def __init__(self, shapes):
        self.shapes = shapes
        self.sizes = [int(np.prod(s)) for s in shapes]
        self.padded = [-(-n // (8 * LANE)) * 8 * LANE for n in self.sizes]
        self.offs = np.concatenate([[0], np.cumsum(self.padded)]).tolist()

    def pack(self, arrs):
        flat = [jnp.pad(a.reshape(-1).astype(F32), (0, p - n)) for a, n, p in zip(arrs, self.sizes, self.padded)]
        return jnp.concatenate(flat).reshape(-1, LANE)

    def unpack(self, slab):
        flat = slab.reshape(-1)
        return [flat[o:o + n].reshape(s) for o, n, s in zip(self.offs, self.sizes, self.shapes)]


def kernel(x, c, ctx, c_ctx, ada_w, ada_b, norm1_g, norm2_g, w_in, hgrn_lb_logits, hgrn_norm_g, na_q_norm_g, na_k_norm_g, na_rel_bias, w_branch_a, w_branch_b, w_out, ffn_w1, ffn_w3, ffn_conv_w, ffn_conv_b, ffn_w2, loss_target, m_c_ctx, m_ada_w, m_ada_b, m_norm1_g, m_norm2_g, m_w_in, m_hgrn_lb_logits, m_hgrn_norm_g, m_na_q_norm_g, m_na_k_norm_g, m_na_rel_bias, m_w_branch_a, m_w_branch_b, m_w_out, m_ffn_w1, m_ffn_w3, m_ffn_conv_w, m_ffn_conv_b, m_ffn_w2, v_c_ctx, v_ada_w, v_ada_b, v_norm1_g, v_norm2_g, v_w_in, v_hgrn_lb_logits, v_hgrn_norm_g, v_na_q_norm_g, v_na_k_norm_g, v_na_rel_bias, v_w_branch_a, v_w_branch_b, v_w_out, v_ffn_w1, v_ffn_w3, v_ffn_conv_w, v_ffn_conv_b, v_ffn_w2):
    t, d = x.shape[1], x.shape[2]
    n_ctx = ctx.shape[1]
    tt = n_ctx + t
    hw = N_HEADS * HEAD_DIM
    ci = w_in.shape[2]
    ca = ada_w.shape[2]
    ff_l = ffn_w1.shape[2]
    ff_p = -(-ff_l // LANE) * LANE
    rows = t // GRID_W
    assert rows >= WIN_R and t % GRID_W == 0 and n_ctx % GRID_W == 0 and ci % LANE == 0 and d % LANE == 0
    me = 4 * lax.axis_index("x") + 2 * lax.axis_index("y") + lax.axis_index("c")
    tr = _pick(n_ctx, (256, 128, 64))
    n_ctx_tiles = n_ctx // tr

    pad_c = lambda w: jnp.pad(w, ((0, 0), (0, ff_p - ff_l)))
    small_in = [c, hgrn_lb_logits.reshape(4, HEAD_DIM), jnp.pad(ffn_conv_w[0], ((0, 5), (0, ff_p - ff_l)))]
    c_all, lb_parts, cw_all = _exchange(small_in, "gather_params", scatter=False)
    ff = N_DEV * ff_p

    cc = jnp.concatenate([c_all.reshape(N_DEV, d), jnp.broadcast_to(c_ctx[None, :], (N_DEV, d))], axis=0)
    act = _tiled(lambda v: v * jax.nn.sigmoid(v), [_const(cc)], [(cc.shape, BF16, cc.shape, lambda *_: (0, 0))], (), "silu_c")[0]
    ada16 = _cast_bf16(ada_w[0], "cast_ada")
    mod_cols = _mm_xw(act, ada16.reshape(1, d, ca), F32, "ada_fwd")
    mod_all = _exchange([mod_cols], "gather_mod", scatter=False)[0]

    with _after(mod_all):
        w_in16 = _cast_bf16(w_in[0], "cast_w_in")
    g_in, = _gather_sc([w_in16], "gather_w_in", 1)
    rest16 = []
    for w_, nm in ((w_branch_a[0], "cast_w_a"), (w_branch_b[0], "cast_w_b"), (w_out[0], "cast_w_out"), (pad_c(ffn_w1[0]), "cast_w1"),
                   (pad_c(ffn_w3[0]), "cast_w3"), (jnp.pad(ffn_w2[0], ((0, ff_p - ff_l), (0, 0))), "cast_w2")):
        with _after(w_in16):
            rest16.append(_cast_bf16(w_, nm))
    g_a, g_b, g_out = _gather_sc(rest16[:3], "gather_mix", 2)
    g_w1, g_w3, g_w2 = _gather_sc(rest16[3:], "gather_ffn", 10)
    g_out = g_out.reshape(1, d, d)
    g_w2 = g_w2.reshape(1, ff, d)
    mod_all = mod_all.transpose(1, 0, 2).reshape(2 * N_DEV, N_MOD * d) + ada_b
    mod_l = lax.dynamic_slice_in_dim(mod_all, me, 1, axis=0).reshape(N_MOD, 1, d)
    mod_c = mod_all[N_DEV:N_DEV + 1].reshape(N_MOD, 1, d)
    mods1 = jnp.stack([mod_c[0:2], mod_l[0:2]])
    mods2 = jnp.stack([mod_l[3:5], mod_l[3:5]])
    gate1, gate2 = mod_l[2], mod_l[5]

    xcat = jnp.concatenate([ctx[0], x[0]], axis=0)
    hcat = _norm_mod_fwd(xcat, norm1_g, mods1, n_ctx_tiles, tr)
    p = _mm_xw(hcat, g_in, F32, "in_proj")
    lb_logits = lb_parts.transpose(1, 0, 2).reshape(2, 2, hw)
    lb_soft = _tiled(lambda a, b: (1.0 / (1.0 + jnp.exp(b - a)),), [_const(lb_logits[:, 0]), _const(lb_logits[:, 1])],
                     [((2, hw), F32, (2, hw), lambda *_: (0, 0))], (), "lb_softmax")[0]
    lb_f, lb_b = lb_soft[0:1], lb_soft[1:2]
    o_f, *saved_f = _hgrn_fwd(p, lb_f, 1, False, n_ctx)
    o_b, *saved_b = _hgrn_fwd(p, lb_b, 2, True, n_ctx)

    cos, sin, onehot, neg = _na_tables(t, n_ctx)
    bias = _bias_slabs(na_rel_bias[0], onehot, neg)
    hcol = lambda seg, off=0: (lambda i, h: (i + off, seg * N_HEADS + h))
    tq = tr
    lat0 = n_ctx // tq
    qk_fn = lambda tv, g, cs, sn: (_f_qk(tv, g, cs, sn),)
    tab = lambda a, off=0: (a, (tq, HEAD_DIM), lambda i, h: (i + off, 0))
    qn = _tiled(qk_fn, [(p, (tq, HEAD_DIM), hcol(5, lat0)), _const(na_q_norm_g), tab(cos, lat0), tab(sin, lat0)],
                [((t, hw), BF16, (tq, HEAD_DIM), lambda i, h: (i, h))], (t // tq, N_HEADS), "q_norm_rope")[0]
    kall = _tiled(qk_fn, [(p, (tq, HEAD_DIM), hcol(6)), _const(na_k_norm_g), tab(cos), tab(sin)],
                  [((tt, hw), BF16, (tq, HEAD_DIM), lambda i, h: (i, h))], (tt // tq, N_HEADS), "k_norm_rope")[0]
    y_b = _na_fwd(qn, kall, p, 7, bias, n_ctx)

    trh = tr
    lat_h = n_ctx // trh
    ospec = lambda a: (a, (trh, HEAD_DIM), lambda i, h: (i + lat_h, h))
    y_a = _tiled(lambda a, b, gt, g: (_f_readout(a, b, gt, g),),
                 [ospec(o_f), ospec(o_b), (p, (trh, HEAD_DIM), hcol(4, lat_h)), _const(hgrn_norm_g)],
                 [((t, hw), BF16, (trh, HEAD_DIM), lambda i, h: (i, h))], (t // trh, N_HEADS), "hgrn_readout")[0]

    p_a = _mm_xw(y_a, g_a, F32, "branch_a")
    p_b = _mm_xw(y_b, g_b, F32, "branch_b")
    td = _pick(d, (512, 256, 128))
    nd_t = d // td
    lat_r = n_ctx // tr
    gcol = lambda k: (p, (tr, td), lambda i, j, k=k: (i + lat_r, 8 * hw // td + k * nd_t + j))
    dtile = lambda a: (a, (tr, td), lambda i, j: (i, j))
    z = _tiled(lambda ga, gb, pa, pb: (_f_merge(ga, gb, pa, pb),), [gcol(0), gcol(1), dtile(p_a), dtile(p_b)],
               [((t, d), BF16, (tr, td), lambda i, j: (i, j))], (t // tr, nd_t), "merge")[0]
    mix = _mm_xw(z, g_out, F32, "out_proj")
    xl = x[0]
    x_mid = _tiled(lambda a, g, m_: (_f_resid(a, g, m_),), [_rows(xl, tr), _const(gate1), _rows(mix, tr)],
                   [((t, d), F32, (tr, d), lambda i: (i, 0))], (t // tr,), "resid1")[0]

    h2 = _norm_mod_fwd(x_mid, norm2_g, mods2, 0, tr)
    u0 = _mm_xw(h2, g_w1, F32, "ffn_up1")
    t3 = _mm_xw(h2, g_w3, F32, "ffn_up3")
    cw_full = cw_all.transpose(1, 0, 2).reshape(8, ff)
    cb_full = jnp.pad(ffn_conv_b.reshape(N_DEV, ff_l), ((0, 0), (0, ff_p - ff_l))).reshape(1, ff)
    a_act = _ffn_act_fwd(u0, t3, cw_full, cb_full)
    f_out = _mm_xw(a_act, g_w2, F32, "ffn_down")

    def loss_fn(xm, g, f, tg):
        err = xm + g * f - tg
        return err * (1.0 / d), jnp.sum(err * err, axis=0, keepdims=True) * (0.5 / d), jnp.sum(err * (1.0 / d) * f, axis=0, keepdims=True)

    dy, loss_cols, d_gate2 = _tiled(loss_fn, [_rows(x_mid, tr), _const(gate2), _rows(f_out, tr), _rows(loss_target[0], tr)],
                                    [((t, d), F32, (tr, d), lambda i: (i, 0)), ((1, d), F32, (1, d), lambda i: (0, 0)),
                                     ((1, d), F32, (1, d), lambda i: (0, 0))], (t // tr,), "loss", acc=(1, 2))

    df = _tiled(lambda a, g: (a * g,), [_rows(dy, tr), _const(gate2)], [((t, d), BF16, (tr, d), lambda i: (i, 0))], (t // tr,), "d_ffn_out")[0]
    d_w2 = _mm_xtdy(a_act, df, 1, BF16, "d_w2")
    da = _mm_dyw(df, g_w2, BF16, "d_act")
    du0, dt3, d_cw, d_cb = _ffn_act_bwd(u0, t3, cw_full, cb_full, da)
    d_w1 = _mm_xtdy(h2, du0, N_DEV, BF16, "d_w1")
    d_w3 = _mm_xtdy(h2, dt3, N_DEV, BF16, "d_w3")
    ffn_blocks = [d_w1, d_w3, d_w2.reshape(N_DEV, ff_p, d)]
    ffn_theirs = _pair_swap_sc(ffn_blocks, "scatter_ffn_pair", 3)
    dh2 = _mm_dyw(du0, g_w1, F32, "d_h2_a")
    dh2 = _mm_dyw(dt3, g_w3, F32, "d_h2_b", init=dh2)
    with _after(dh2):
        ffn_sums = [_pair_add(ffn_blocks[0], ffn_theirs[0], "pair_add_w1")]
    ffn_sums += [_pair_add(ffn_blocks[1], ffn_theirs[1], "pair_add_w3"), _pair_add(ffn_blocks[2], ffn_theirs[2], "pair_add_w2")]
    r_w1, r_w3, r_w2 = _chip_scatter_sc(ffn_sums, "scatter_ffn_chips", 4)
    with _after(*ffn_sums):
        dx_mid, d_norm2, d_mods2 = _norm_mod_bwd(x_mid, norm2_g, mods2, dh2, dy, 0, tr, "norm_mod2_bwd")

    dm, d_gate1 = _tiled(lambda dxm, g, m_: (dxm * g, jnp.sum(dxm * m_, axis=0, keepdims=True)),
                         [_rows(dx_mid, tr), _const(gate1), _rows(mix, tr)],
                         [((t, d), BF16, (tr, d), lambda i: (i, 0)), ((1, d), F32, (1, d), lambda i: (0, 0))], (t // tr,), "d_resid1", acc=(1,))
    d_wout = _mm_xtdy(z, dm, 1, BF16, "d_w_out")
    dz = _mm_dyw(dm, g_out, F32, "d_merge")

    def merge_bwd(ga, gb, pa, pb, dzv):
        _, vjp = jax.vjp(_f_merge, ga, gb, pa, pb)
        return vjp(dzv)

    dga, dgb, dpa, dpb = _tiled(merge_bwd, [gcol(0), gcol(1), dtile(p_a), dtile(p_b), dtile(dz)],
                                [((t, d), BF16, (tr, td), lambda i, j: (i, j))] * 4, (t // tr, nd_t), "merge_bwd")
    d_wa = _mm_xtdy(y_a, dpa, N_DEV, BF16, "d_w_a")
    d_wb = _mm_xtdy(y_b, dpb, N_DEV, BF16, "d_w_b")
    mix_blocks = [d_wa, d_wb, d_wout.reshape(N_DEV, d // N_DEV, d)]
    mix_theirs = _pair_swap_sc(mix_blocks, "scatter_mix_pair", 5)
    dy_a = _mm_dyw(dpa, g_a, F32, "d_y_a")
    dy_b = _mm_dyw(dpb, g_b, BF16, "d_y_b")
    with _after(dy_b):
        mix_sums = [_pair_add(mix_blocks[0], mix_theirs[0], "pair_add_wa")]
    mix_sums += [_pair_add(mix_blocks[1], mix_theirs[1], "pair_add_wb"), _pair_add(mix_blocks[2], mix_theirs[2], "pair_add_wout")]
    r_a, r_b, r_out = _chip_scatter_sc(mix_sums, "scatter_mix_chips", 6)

    def readout_bwd(a, b, gt, g, ct):
        _, vjp = jax.vjp(_f_readout, a, b, gt, g)
        da_, _, dgt, dg = vjp(ct)
        return da_, dgt, dg

    hsp = lambda dt: ((t, hw), dt, (trh, HEAD_DIM), lambda i, h: (i, h))
    with _after(*mix_sums):
        do_h, d_gate_o, d_hnorm = _tiled(
            readout_bwd, [ospec(o_f), ospec(o_b), (p, (trh, HEAD_DIM), hcol(4, lat_h)), _const(hgrn_norm_g), (dy_a, (trh, HEAD_DIM), lambda i, h: (i, h))],
            [hsp(F32), hsp(BF16), ((1, HEAD_DIM), F32, (1, HEAD_DIM), lambda i, h: (0, 0))], (t // trh, N_HEADS), "readout_bwd", acc=(2,))
    dq1, dfl_f, dv1, dlb_f = _hgrn_bwd(p, lb_f, do_h, saved_f, 1, False, n_ctx, None)
    dq_h, dfl_b, dv_h, dlb_b = _hgrn_bwd(p, lb_b, do_h, saved_b, 2, True, n_ctx, (dq1, dv1))

    dqn, dkall, dv_na, dbias = _na_bwd(qn, kall, p, 7, bias, dy_b, n_ctx)

    def qk_bwd(tv, g, cs, sn, ct):
        _, vjp = jax.vjp(lambda a, b: _f_qk(a, b, cs, sn), tv, g)
        return vjp(ct)

    d_pq, d_qnorm = _tiled(qk_bwd, [(p, (tq, HEAD_DIM), hcol(5, lat0)), _const(na_q_norm_g), tab(cos, lat0), tab(sin, lat0),
                                    (dqn, (tq, HEAD_DIM), lambda i, h: (i, h))],
                           [((t, hw), BF16, (tq, HEAD_DIM), lambda i, h: (i, h)), ((1, HEAD_DIM), F32, (1, HEAD_DIM), lambda i, h: (0, 0))],
                           (t // tq, N_HEADS), "q_norm_rope_bwd", acc=(1,))
    d_pk, d_knorm = _tiled(qk_bwd, [(p, (tq, HEAD_DIM), hcol(6)), _const(na_k_norm_g), tab(cos), tab(sin),
                                    (dkall, (tq, HEAD_DIM), lambda i, h: (i, h))],
                           [((tt, hw), BF16, (tq, HEAD_DIM), lambda i, h: (i, h)), ((1, HEAD_DIM), F32, (1, HEAD_DIM), lambda i, h: (0, 0))],
                           (tt // tq, N_HEADS), "k_norm_rope_bwd", acc=(1,))

    zc = lambda w_: jnp.zeros((n_ctx, w_), BF16)
    lat_only = lambda a: jnp.concatenate([zc(a.shape[1]), a], axis=0)
    dp = jnp.concatenate([dq_h, dfl_f, dfl_b, dv_h, lat_only(d_gate_o), lat_only(d_pq), d_pk, dv_na, lat_only(dga), lat_only(dgb)], axis=1)
    d_win = _mm_xtdy(hcat, dp, N_DEV, BF16, "d_w_in")
    win_theirs, = _pair_swap_sc([d_win], "scatter_w_in_pair", 7)
    with _after(d_win):
        dhcat = _mm_dyw(dp, g_in, BF16, "d_hcat")
    zero_ctx = jnp.concatenate([jnp.zeros((n_ctx, d), F32), dx_mid], axis=0)
    dxcat, d_norm1, d_mods1 = _norm_mod_bwd(xcat, norm1_g, mods1, dhcat, zero_ctx, n_ctx_tiles, tr, "norm_mod1_bwd")
    grad_x = dxcat[n_ctx:][None]

    zd = jnp.zeros((1, d), F32)
    dmod_l = jnp.concatenate([d_mods1[1, 0], d_mods1[1, 1], d_gate1, d_mods2[1, 0], d_mods2[1, 1], d_gate2], axis=1)
    dmod_c = jnp.concatenate([d_mods1[0, 0], d_mods1[0, 1], zd, zd, zd, zd], axis=1)
    dmods = jnp.concatenate([dmod_l, dmod_c], axis=0).reshape(2, N_DEV, ca).transpose(1, 0, 2)
    dmods = jnp.pad(dmods, ((0, 0), (0, 6), (0, 0)))
    got = _exchange([dmods], "scatter_dmod", scatter=True)[0]
    dm_rows = jnp.concatenate([got[:, 0], got[:, 1]], axis=0)
    d_ada = _mm_xtdy(act, dm_rows, 1, F32, "d_ada_w")[0]
    back = _mm_dyw(dm_rows, ada16.reshape(1, d, ca), F32, "d_silu_c")
    d_cctx_part = _tiled(lambda b, v: (jnp.sum(b[N_DEV:], axis=0, keepdims=True) * (jax.nn.sigmoid(v) * (1.0 + v * (1.0 - jax.nn.sigmoid(v)))),),
                         [_const(back), _const(c_ctx.reshape(1, d))], [((1, d), F32, (1, d), lambda *_: (0, 0))], (), "d_c_ctx")[0]

    d_rel = _bias_grad(dbias, onehot)
    d_lb_soft = jnp.concatenate([dlb_f, dlb_b], axis=0)
    d_lb0 = _tiled(lambda s, g: (g * s * (1.0 - s),), [_const(lb_soft), _const(d_lb_soft)], [((2, hw), F32, (2, hw), lambda *_: (0, 0))], (), "d_lb")[0]
    d_lb_full = jnp.stack([d_lb0, -d_lb0], axis=1)
    d_cw_l = d_cw[:3].reshape(3, N_DEV, ff_p)[:, :, :ff_l].reshape(1, 3, N_DEV * ff_l)
    d_cb_l = d_cb.reshape(N_DEV, ff_p)[:, :ff_l].reshape(1, N_DEV * ff_l)
    small = [d_cctx_part.reshape(d), (dmod_l + dmod_c), d_norm1, d_norm2, d_lb_full, d_hnorm, d_qnorm, d_knorm, d_rel, d_cw_l, d_cb_l, loss_cols]
    pk = _Pack([a.shape for a in small])
    small_slab = pk.pack(small)
    small_parts, = _gather_sc([small_slab], "gather_small", 9)

    res = {}
    with _after(small_slab):
        win_sums = _pair_add(d_win, win_theirs, "pair_add_w_in")
    r_in, = _chip_scatter_sc([win_sums], "scatter_w_in_chips", 8)
    with _after(win_sums):
        res["ada_w"] = _adam_big(d_ada[None], ada_w[0], m_ada_w[0], v_ada_w[0], "adam_ada")
    res["ffn_w1"] = _adam_big(r_w1, ffn_w1[0], m_ffn_w1[0], v_ffn_w1[0], "adam_w1")
    res["ffn_w3"] = _adam_big(r_w3, ffn_w3[0], m_ffn_w3[0], v_ffn_w3[0], "adam_w3")
    res["ffn_w2"] = _adam_big(r_w2, ffn_w2[0], m_ffn_w2[0], v_ffn_w2[0], "adam_w2")
    res["w_branch_a"] = _adam_big(r_a, w_branch_a[0], m_w_branch_a[0], v_w_branch_a[0], "adam_w_a")
    res["w_branch_b"] = _adam_big(r_b, w_branch_b[0], m_w_branch_b[0], v_w_branch_b[0], "adam_w_b")
    res["w_out"] = _adam_big(r_out, w_out[0], m_w_out[0], v_w_out[0], "adam_w_out")

    with _after(res["w_out"][1]):
        tot = _sum_parts(small_parts, "sum_small")
    g_cctx, g_ada_b, g_n1, g_n2, g_lb, g_hn, g_qn, g_kn, g_rel, g_cw, g_cb, loss_all = pk.unpack(tot)
    loss = _tiled(lambda v: (jnp.sum(v, axis=1, keepdims=True),), [_const(loss_all)], [((1, 1), F32, (1, 1), lambda *_: (0, 0))], (), "loss_total")[0][0, 0]
    g_lb = lax.dynamic_slice_in_dim(g_lb, me * HEAD_DIM, HEAD_DIM, axis=2)
    g_cw = lax.dynamic_slice_in_dim(g_cw, me * ff_l, ff_l, axis=2)
    small_names = [("c_ctx", g_cctx, c_ctx, m_c_ctx, v_c_ctx), ("ada_b", g_ada_b, ada_b, m_ada_b, v_ada_b),
                   ("norm1_g", g_n1, norm1_g, m_norm1_g, v_norm1_g), ("norm2_g", g_n2, norm2_g, m_norm2_g, v_norm2_g),
                   ("hgrn_lb_logits", g_lb, hgrn_lb_logits, m_hgrn_lb_logits, v_hgrn_lb_logits),
                   ("hgrn_norm_g", g_hn, hgrn_norm_g, m_hgrn_norm_g, v_hgrn_norm_g), ("na_q_norm_g", g_qn, na_q_norm_g, m_na_q_norm_g, v_na_q_norm_g),
                   ("na_k_norm_g", g_kn, na_k_norm_g, m_na_k_norm_g, v_na_k_norm_g), ("na_rel_bias", g_rel, na_rel_bias, m_na_rel_bias, v_na_rel_bias),
                   ("ffn_conv_w", g_cw, ffn_conv_w, m_ffn_conv_w, v_ffn_conv_w), ("ffn_conv_b", g_cb, ffn_conv_b, m_ffn_conv_b, v_ffn_conv_b)]
    pk2 = _Pack([s[1].shape for s in small_names])
    sd, sm, sv = _adam_small(*[pk2.pack([s[i] for s in small_names]) for i in (1, 2, 3, 4)])
    sd, sm, sv = pk2.unpack(sd), pk2.unpack(sm), pk2.unpack(sv)
    res.update({s[0]: (s[1], sd[i], sm[i], sv[i]) for i, s in enumerate(small_names)})
    with _after(sd[0]):
        res["w_in"] = _adam_big(r_in, w_in[0], m_w_in[0], v_w_in[0], "adam_w_in")
    for k in ("w_in", "w_branch_a", "w_branch_b", "w_out", "ffn_w1", "ffn_w3", "ffn_w2", "ada_w"):
        res[k] = tuple(a[None] for a in res[k])

    order = ["c_ctx", "ada_w", "ada_b", "norm1_g", "norm2_g", "w_in", "hgrn_lb_logits", "hgrn_norm_g", "na_q_norm_g", "na_k_norm_g",
             "na_rel_bias", "w_branch_a", "w_branch_b", "w_out", "ffn_w1", "ffn_w3", "ffn_conv_w", "ffn_conv_b", "ffn_w2"]
    shapes = {"c_ctx": c_ctx.shape, "ada_b": ada_b.shape, "norm1_g": norm1_g.shape, "norm2_g": norm2_g.shape,
              "hgrn_lb_logits": hgrn_lb_logits.shape, "hgrn_norm_g": hgrn_norm_g.shape, "na_q_norm_g": na_q_norm_g.shape,
              "na_k_norm_g": na_k_norm_g.shape, "na_rel_bias": na_rel_bias.shape, "ffn_conv_w": ffn_conv_w.shape, "ffn_conv_b": ffn_conv_b.shape}
    outs = [loss, grad_x]
    for part in range(4):
        for k in order:
            a = res[k][part]
            outs.append(a.reshape(shapes[k]) if k in shapes else a)
    return tuple(outs)
```

```python
import functools

import numpy as np
import jax
import jax.numpy as jnp
from jax import lax
from jax.experimental import pallas as pl
from jax.experimental.pallas import tpu as pltpu
from jax.experimental.pallas import tpu_sc as plsc

F32 = jnp.float32
BF16 = jnp.bfloat16
HIGHEST = lax.Precision.HIGHEST

N_DEV = 8
MESH_ID = pl.DeviceIdType.MESH
LANE = 128
HEAD_DIM = 128
N_HEADS = 8
GRID_W = 64
WIN_R = 8
WIN_C = 16
ROPE_THETA = 10000.0
EPS = 1e-6
N_MOD = 6
HGRN_BLOCK = 16
NEG_BIG = -1e30
VMEM_LIMIT = 56 << 20

ADAM_LR = 0.001
ADAM_B1 = 0.9
ADAM_B2 = 0.999
ADAM_EPS = 1e-08
ADAM_WD = 0.01
ADAM_STEP = 10

HBM_SPEC = pl.BlockSpec(memory_space=pltpu.HBM)


_ORDER_AFTER = []


class _after:
    def __init__(self, *arrs):
        self.arrs = list(arrs)

    def __enter__(self):
        _ORDER_AFTER.extend(self.arrs)

    def __exit__(self, *exc):
        del _ORDER_AFTER[:]


def _pcall(body, *, name, out_shape, grid=None, in_specs=None, out_specs=None, scratch=(), aliases=None):
    kw = {}
    if grid is not None:
        kw["grid"] = grid
    extra = []
    if _ORDER_AFTER and in_specs is not None:
        extra = list(_ORDER_AFTER)
        del _ORDER_AFTER[:]
        n_in, n_extra, inner = len(in_specs), len(extra), body
        in_specs = list(in_specs) + [pl.BlockSpec(memory_space=pl.ANY)] * n_extra

        def body(*refs):
            return inner(*refs[:n_in], *refs[n_in + n_extra:])

    if extra:
        call = _pcall_inner(body, name, out_shape, kw, in_specs, out_specs, scratch, aliases)
        return lambda *args: call(*args, *extra)
    return _pcall_inner(body, name, out_shape, kw, in_specs, out_specs, scratch, aliases)


def _pcall_inner(body, name, out_shape, kw, in_specs, out_specs, scratch, aliases):
    if in_specs is not None:
        kw["in_specs"] = in_specs
    if out_specs is not None:
        kw["out_specs"] = out_specs
    if scratch:
        kw["scratch_shapes"] = list(scratch)
    if aliases:
        kw["input_output_aliases"] = aliases
    return pl.pallas_call(body, name=name, out_shape=out_shape,
                          compiler_params=pltpu.CompilerParams(vmem_limit_bytes=VMEM_LIMIT), **kw)


def _pick(dim, cands):
    for c in cands:
        if c <= dim and dim % c == 0:
            return c
    return dim


def _sds(shape, dtype):
    return jax.ShapeDtypeStruct(tuple(shape), dtype)


def _peers():
    x, y, c = lax.axis_index("x"), lax.axis_index("y"), lax.axis_index("c")
    out = []
    for k in range(1, N_DEV):
        px = 1 - x if (k >> 2) & 1 else x
        py = 1 - y if (k >> 1) & 1 else y
        pc = 1 - c if k & 1 else c
        out.append((k, (px, py, pc), 4 * px + 2 * py + pc))
    return 4 * x + 2 * y + c, out


def _exchange(arrs, name, scatter):
    n = len(arrs)

    def body(*refs):
        ins, outs = refs[:n], refs[n:2 * n]
        send, recv, loc = refs[2 * n:]
        me, peers = _peers()
        started = []
        for i in range(n):
            src = ins[i].at[me] if scatter else ins[i]
            cp = pltpu.make_async_copy(src, outs[i].at[me], loc.at[i])
            cp.start()
            started.append(cp)
        sends = []
        for k, peer, pidx in peers:
            for i in range(n):
                src = ins[i].at[pidx] if scatter else ins[i]
                cp = pltpu.make_async_remote_copy(src_ref=src, dst_ref=outs[i].at[me], send_sem=send.at[i * 7 + k - 1],
                                                  recv_sem=recv.at[i * 7 + k - 1], device_id=peer, device_id_type=MESH_ID)
                cp.start()
                sends.append(cp)
        for k, peer, pidx in peers:
            for i in range(n):
                src = ins[i].at[pidx] if scatter else ins[i]
                pltpu.make_async_remote_copy(src_ref=src, dst_ref=outs[i].at[pidx], send_sem=send.at[i * 7 + k - 1],
                                             recv_sem=recv.at[i * 7 + k - 1], device_id=peer, device_id_type=MESH_ID).wait_recv()
        for cp in sends:
            cp.wait_send()
        for cp in started:
            cp.wait()

    out_shape = [_sds(a.shape if scatter else (N_DEV,) + a.shape, a.dtype) for a in arrs]
    res = _pcall(body, name=name, out_shape=out_shape, in_specs=[HBM_SPEC] * n, out_specs=[HBM_SPEC] * n,
                 scratch=[pltpu.SemaphoreType.DMA((7 * n,)), pltpu.SemaphoreType.DMA((7 * n,)), pltpu.SemaphoreType.DMA((n,))])(*arrs)
    return list(res)


def _exchange_sc(arrs, name, scatter, collective_id):
    n = len(arrs)
    srcs = [jax.new_ref(a, memory_space=pltpu.MemorySpace.HBM) for a in arrs]
    lands = [jax.empty_ref(_sds(a.shape if scatter else (N_DEV,) + a.shape, a.dtype), memory_space=pltpu.MemorySpace.HBM) for a in arrs]

    @pl.kernel(mesh=plsc.ScalarSubcoreMesh(axis_name="seq", num_cores=1), name=name,
               scratch_types=(pltpu.SemaphoreType.DMA((7 * n,)), pltpu.SemaphoreType.DMA((7 * n,)), pltpu.SemaphoreType.DMA((n,))),
               compiler_params=pltpu.CompilerParams(collective_id=collective_id))
    def launch(send, recv, loc):
        me, peers = _peers()
        barrier = pltpu.get_barrier_semaphore()
        for _, peer, _ in peers:
            pl.semaphore_signal(barrier, inc=1, device_id=peer, device_id_type=MESH_ID)
        pl.semaphore_wait(barrier, N_DEV - 1)
        own = [pltpu.make_async_copy(srcs[i].at[me] if scatter else srcs[i], lands[i].at[me], loc.at[i]) for i in range(n)]
        for cp in own:
            cp.start()
        sends = []
        for k, peer, pidx in peers:
            for i in range(n):
                src = srcs[i].at[pidx] if scatter else srcs[i]
                cp = pltpu.make_async_remote_copy(src_ref=src, dst_ref=lands[i].at[me], send_sem=send.at[i * 7 + k - 1],
                                                  recv_sem=recv.at[i * 7 + k - 1], device_id=peer, device_id_type=MESH_ID)
                cp.start()
                sends.append(cp)
        for k, peer, pidx in peers:
            for i in range(n):
                src = srcs[i].at[pidx] if scatter else srcs[i]
                pltpu.make_async_remote_copy(src_ref=src, dst_ref=lands[i].at[pidx], send_sem=send.at[i * 7 + k - 1],
                                             recv_sem=recv.at[i * 7 + k - 1], device_id=peer, device_id_type=MESH_ID).wait_recv()
        for cp in sends:
            cp.wait_send()
        for cp in own:
            cp.wait()

    launch()
    return [r[...] for r in lands]


def _gather_sc(arrs, name, collective_id):
    n = len(arrs)
    srcs = [jax.new_ref(a, memory_space=pltpu.MemorySpace.HBM) for a in arrs]
    lands = [jax.empty_ref(_sds((N_DEV,) + a.shape, a.dtype), memory_space=pltpu.MemorySpace.HBM) for a in arrs]

    @pl.kernel(mesh=plsc.ScalarSubcoreMesh(axis_name="seq", num_cores=1), name=name,
               scratch_types=(pltpu.SemaphoreType.DMA((7 * n,)), pltpu.SemaphoreType.DMA((7 * n,)), pltpu.SemaphoreType.DMA((n,))),
               compiler_params=pltpu.CompilerParams(collective_id=collective_id))
    def launch(send, recv, loc):
        x, y, c = lax.axis_index("x"), lax.axis_index("y"), lax.axis_index("c")
        me, sibling = (x, y, c), (x, y, 1 - c)
        chips = [(1 - x, y), (x, 1 - y), (1 - x, 1 - y)]
        index = lambda px, py, pc: 4 * px + 2 * py + pc
        barrier = pltpu.get_barrier_semaphore()
        for peer in [sibling] + [(*chip, c) for chip in chips]:
            pl.semaphore_signal(barrier, inc=1, device_id=peer, device_id_type=MESH_ID)
        pl.semaphore_wait(barrier, 4)

        def copy(i, k, block, to, from_src):
            return pltpu.make_async_remote_copy(src_ref=srcs[i] if from_src else lands[i].at[index(*block)], dst_ref=lands[i].at[index(*block)],
                                                send_sem=send.at[i * 7 + k], recv_sem=recv.at[i * 7 + k], device_id=to, device_id_type=MESH_ID)

        own = [pltpu.make_async_copy(srcs[i], lands[i].at[index(*me)], loc.at[i]) for i in range(n)]
        for cp in own:
            cp.start()
        started = []
        for i in range(n):
            started.append(copy(i, 0, me, sibling, True))
            started += [copy(i, 1 + j, me, (*chip, c), True) for j, chip in enumerate(chips)]
        for cp in started:
            cp.start()
        for j, chip in enumerate(chips):
            for i in range(n):
                copy(i, 1 + j, (*chip, c), me, False).wait_recv()
                fwd = copy(i, 4 + j, (*chip, c), sibling, False)
                fwd.start()
                started.append(fwd)
        for i in range(n):
            copy(i, 0, sibling, me, False).wait_recv()
            for j, chip in enumerate(chips):
                copy(i, 4 + j, (*chip, 1 - c), me, False).wait_recv()
        for cp in started:
            cp.wait_send()
        for cp in own:
            cp.wait()

    launch()
    return [r[...] for r in lands]


def _pair_swap_sc(blocks, name, collective_id):
    n = len(blocks)
    srcs = [jax.new_ref(b, memory_space=pltpu.MemorySpace.HBM) for b in blocks]
    lands = [jax.empty_ref(_sds((4,) + b.shape[1:], b.dtype), memory_space=pltpu.MemorySpace.HBM) for b in blocks]

    @pl.kernel(mesh=plsc.ScalarSubcoreMesh(axis_name="seq", num_cores=1), name=name,
               scratch_types=(pltpu.SemaphoreType.DMA((4 * n,)), pltpu.SemaphoreType.DMA((4 * n,))),
               compiler_params=pltpu.CompilerParams(collective_id=collective_id))
    def launch(send, recv):
        x, y, c = lax.axis_index("x"), lax.axis_index("y"), lax.axis_index("c")
        sibling = (x, y, 1 - c)
        barrier = pltpu.get_barrier_semaphore()
        pl.semaphore_signal(barrier, inc=1, device_id=sibling, device_id_type=MESH_ID)
        pl.semaphore_wait(barrier, 1)
        copies = [pltpu.make_async_remote_copy(src_ref=srcs[i].at[2 * q + 1 - c], dst_ref=lands[i].at[q], send_sem=send.at[4 * i + q],
                                               recv_sem=recv.at[4 * i + q], device_id=sibling, device_id_type=MESH_ID)
                  for i in range(n) for q in range(4)]
        for cp in copies:
            cp.start()
        for cp in copies:
            cp.wait()

    launch()
    return [r[...] for r in lands]


def _pair_add(blocks, theirs, name):
    _, r, c_ = blocks.shape
    mine = lax.dynamic_index_in_dim(blocks.reshape(4, 2, r, c_), lax.axis_index("c"), axis=1, keepdims=False)
    tr = _pick(r, (256, 128, 64, 16))
    spec = (None, tr, c_)
    return _tiled(lambda a, b: (a.astype(F32) + b.astype(F32),), [(mine, spec, lambda q, i: (q, i, 0)), (theirs, spec, lambda q, i: (q, i, 0))],
                  [((4, r, c_), BF16, spec, lambda q, i: (q, i, 0))], (4, r // tr), name)[0]


def _chip_scatter_sc(sums, name, collective_id):
    n = len(sums)
    srcs = [jax.new_ref(s, memory_space=pltpu.MemorySpace.HBM) for s in sums]
    lands = [jax.empty_ref(_sds(s.shape, s.dtype), memory_space=pltpu.MemorySpace.HBM) for s in sums]

    @pl.kernel(mesh=plsc.ScalarSubcoreMesh(axis_name="seq", num_cores=1), name=name,
               scratch_types=(pltpu.SemaphoreType.DMA((3 * n,)), pltpu.SemaphoreType.DMA((3 * n,)), pltpu.SemaphoreType.DMA((n,))),
               compiler_params=pltpu.CompilerParams(collective_id=collective_id))
    def launch(send, recv, loc):
        x, y, c = lax.axis_index("x"), lax.axis_index("y"), lax.axis_index("c")
        chips = [(1 - x, y), (x, 1 - y), (1 - x, 1 - y)]
        my_chip = 2 * x + y
        barrier = pltpu.get_barrier_semaphore()
        for chip in chips:
            pl.semaphore_signal(barrier, inc=1, device_id=(*chip, c), device_id_type=MESH_ID)
        pl.semaphore_wait(barrier, 3)
        own = [pltpu.make_async_copy(srcs[i].at[my_chip], lands[i].at[my_chip], loc.at[i]) for i in range(n)]
        for cp in own:
            cp.start()
        sends = [pltpu.make_async_remote_copy(src_ref=srcs[i].at[2 * px + py], dst_ref=lands[i].at[my_chip], send_sem=send.at[3 * i + j],
                                              recv_sem=recv.at[3 * i + j], device_id=(px, py, c), device_id_type=MESH_ID)
                 for j, (px, py) in enumerate(chips) for i in range(n)]
        for cp in sends:
            cp.start()
        for j, (px, py) in enumerate(chips):
            for i in range(n):
                pltpu.make_async_remote_copy(src_ref=srcs[i].at[my_chip], dst_ref=lands[i].at[2 * px + py], send_sem=send.at[3 * i + j],
                                             recv_sem=recv.at[3 * i + j], device_id=(px, py, c), device_id_type=MESH_ID).wait_recv()
        for cp in sends:
            cp.wait_send()
        for cp in own:
            cp.wait()

    launch()
    return [r[...] for r in lands]


class _InFlight:
    def __init__(self, send, recv, srcs, lands, token, scatter):
        self.send, self.recv, self.srcs, self.lands, self.token, self.scatter = send, recv, srcs, lands, token, scatter


SEM_SPEC = pl.BlockSpec(memory_space=pltpu.SEMAPHORE)
SIDE_EFFECT = pltpu.SideEffectType.DATAFLOW_SIDE_EFFECTING


def _exchange_start(arrs, name, scatter):
    n = len(arrs)

    def body(*refs):
        ins, lands = refs[:n], refs[n:2 * n]
        send, recv = refs[2 * n], refs[2 * n + 1]
        token = refs[4 * n + 2]
        loc = refs[4 * n + 3]
        me, peers = _peers()
        own = [pltpu.make_async_copy(ins[i].at[me] if scatter else ins[i], lands[i].at[me], loc.at[i]) for i in range(n)]
        for cp in own:
            cp.start()
        for cp in own:
            cp.wait()
        for k, peer, pidx in peers:
            for i in range(n):
                src = ins[i].at[pidx] if scatter else ins[i]
                pltpu.make_async_remote_copy(src_ref=src, dst_ref=lands[i].at[me], send_sem=send.at[i * 7 + k - 1],
                                             recv_sem=recv.at[i * 7 + k - 1], device_id=peer, device_id_type=MESH_ID).start()
        token[...] = jnp.zeros_like(token)

    land_shapes = [a.shape if scatter else (N_DEV,) + a.shape for a in arrs]
    hbm = lambda a: pltpu.with_memory_space_constraint(a, pltpu.HBM)
    args = [hbm(a) for a in arrs] + [hbm(lax.empty(s, a.dtype)) for s, a in zip(land_shapes, arrs)]
    out_shape = ([pltpu.SemaphoreType.DMA((7 * n,)), pltpu.SemaphoreType.DMA((7 * n,))]
                 + [pltpu.HBM(a.shape, a.dtype) for a in arrs] + [pltpu.HBM(s, a.dtype) for s, a in zip(land_shapes, arrs)]
                 + [_sds((8, LANE), F32)])
    res = pl.pallas_call(
        body, name=name, out_shape=out_shape, in_specs=[HBM_SPEC] * (2 * n),
        out_specs=[SEM_SPEC, SEM_SPEC] + [HBM_SPEC] * (2 * n) + [pl.BlockSpec(memory_space=pltpu.VMEM)],
        input_output_aliases={i: 2 + i for i in range(2 * n)},
        scratch_shapes=[pltpu.SemaphoreType.DMA((n,))],
        compiler_params=pltpu.CompilerParams(has_side_effects=SIDE_EFFECT))(*args)
    return _InFlight(res[0], res[1], list(res[2:2 + n]), list(res[2 + n:2 + 2 * n]), res[2 + 2 * n], scatter)


def _exchange_wait(h, after, name):
    n = len(h.srcs)
    scatter = h.scatter
    after = list(after)

    def body(*refs):
        ins, lands = refs[:n], refs[n:2 * n]
        send, recv = refs[2 * n], refs[2 * n + 1]
        _, peers = _peers()
        for k, peer, pidx in peers:
            for i in range(n):
                src = ins[i].at[pidx] if scatter else ins[i]
                cp = pltpu.make_async_remote_copy(src_ref=src, dst_ref=lands[i].at[pidx], send_sem=send.at[i * 7 + k - 1],
                                                  recv_sem=recv.at[i * 7 + k - 1], device_id=peer, device_id_type=MESH_ID)
                cp.wait_send()
                cp.wait_recv()

    res = pl.pallas_call(
        body, name=name, out_shape=[pltpu.HBM(a.shape, a.dtype) for a in h.srcs + h.lands],
        in_specs=[HBM_SPEC] * (2 * n) + [SEM_SPEC, SEM_SPEC] + [pl.BlockSpec(memory_space=pl.ANY)] * len(after),
        out_specs=[HBM_SPEC] * (2 * n), input_output_aliases={i: i for i in range(2 * n)},
        compiler_params=pltpu.CompilerParams(has_side_effects=SIDE_EFFECT))(*h.srcs, *h.lands, h.send, h.recv, *after)
    return list(res[n:])


def _mm_xw(x, g, out_dtype, name, tm_c=(768, 512, 384, 256, 128, 64, 16), tn_c=(1024, 768, 512, 256, 128), tk_c=(2048, 1024, 768, 512, 256)):
    m, r = x.shape
    nb, r2, cl = g.shape
    assert r == r2
    tm, tn, tk = _pick(m, tm_c), _pick(cl, tn_c), _pick(r, tk_c)
    q, nk = cl // tn, r // tk

    def body(x_ref, g_ref, o_ref, *acc):
        p = lax.dot_general(x_ref[...].astype(BF16), g_ref[...], (((1,), (0,)), ((), ())), preferred_element_type=F32)
        if nk == 1:
            o_ref[...] = p.astype(o_ref.dtype)
        else:
            k = pl.program_id(2)

            @pl.when(k == 0)
            def _():
                acc[0][...] = p

            @pl.when(k > 0)
            def _():
                acc[0][...] += p

            @pl.when(k == nk - 1)
            def _():
                o_ref[...] = acc[0][...].astype(o_ref.dtype)

    return _pcall(
        body, name=name, grid=(m // tm, nb * q, nk),
        in_specs=[pl.BlockSpec((tm, tk), lambda i, j, k: (i, k)), pl.BlockSpec((None, tk, tn), lambda i, j, k: (j // q, k, j % q))],
        out_specs=pl.BlockSpec((tm, tn), lambda i, j, k: (i, j)),
        out_shape=_sds((m, nb * cl), out_dtype),
        scratch=[] if nk == 1 else [pltpu.VMEM((tm, tn), F32)])(x, g)


def _mm_dyw(dy, g, out_dtype, name, init=None, tm_c=(1024, 768, 512, 384, 256, 128), tn_c=(1024, 512, 256, 128), tk_c=(2048, 1536, 1024, 768, 512, 256, 128)):
    m, n = dy.shape
    nb, r, cl = g.shape
    assert n == nb * cl
    tm, tn, tk = _pick(m, tm_c), _pick(r, tn_c), _pick(cl, tk_c)
    q = cl // tk
    nk = nb * q
    has_init = init is not None

    def body(*refs):
        if has_init:
            dy_ref, g_ref, i_ref, o_ref, acc = refs
        else:
            dy_ref, g_ref, o_ref, acc = refs
        k = pl.program_id(2)
        p = lax.dot_general(dy_ref[...].astype(BF16), g_ref[...], (((1,), (1,)), ((), ())), preferred_element_type=F32)

        @pl.when(k == 0)
        def _():
            acc[...] = p + i_ref[...].astype(F32) if has_init else p

        @pl.when(k > 0)
        def _():
            acc[...] += p

        @pl.when(k == nk - 1)
        def _():
            o_ref[...] = acc[...].astype(o_ref.dtype)

    in_specs = [pl.BlockSpec((tm, tk), lambda i, j, k: (i, k)), pl.BlockSpec((None, tn, tk), lambda i, j, k: (k // q, j, k % q))]
    args = [dy, g]
    if has_init:
        in_specs.append(pl.BlockSpec((tm, tn), lambda i, j, k: (i, j)))
        args.append(init)
    return _pcall(body, name=name, grid=(m // tm, r // tn, nk), in_specs=in_specs,
                  out_specs=pl.BlockSpec((tm, tn), lambda i, j, k: (i, j)), out_shape=_sds((m, r), out_dtype),
                  scratch=[pltpu.VMEM((tm, tn), F32)])(*args)


def _mm_xtdy(x, dy, nb, out_dtype, name, tm_c=(1024, 512, 256, 128), tn_c=(768, 512, 256, 128), tk_c=(2304, 2048, 1152, 1024, 768, 512, 256, 128, 16)):
    t, r = x.shape
    t2, n = dy.shape
    assert t == t2 and n % nb == 0
    cl = n // nb
    tm, tn, tk = _pick(r, tm_c), _pick(cl, tn_c), _pick(t, tk_c)
    q, nk = cl // tn, t // tk

    def body(x_ref, dy_ref, o_ref, *acc):
        p = lax.dot_general(x_ref[...].astype(BF16), dy_ref[...].astype(BF16), (((0,), (0,)), ((), ())), preferred_element_type=F32)
        if nk == 1:
            o_ref[...] = p.astype(o_ref.dtype)
        else:
            k = pl.program_id(2)

            @pl.when(k == 0)
            def _():
                acc[0][...] = p

            @pl.when(k > 0)
            def _():
                acc[0][...] += p

            @pl.when(k == nk - 1)
            def _():
                o_ref[...] = acc[0][...].astype(o_ref.dtype)

    return _pcall(
        body, name=name, grid=(r // tm, nb * q, nk),
        in_specs=[pl.BlockSpec((tk, tm), lambda i, j, k: (k, i)), pl.BlockSpec((tk, tn), lambda i, j, k: (k, j))],
        out_specs=pl.BlockSpec((None, tm, tn), lambda i, j, k: (j // q, i, j % q)),
        out_shape=_sds((nb, r, cl), out_dtype),
        scratch=[] if nk == 1 else [pltpu.VMEM((tm, tn), F32)])(x, dy)


def _mm_f32(a, b, name, trans_b=False):
    dims = (((1,), (1,)), ((), ())) if trans_b else (((1,), (0,)), ((), ()))
    n = b.shape[0] if trans_b else b.shape[1]

    def body(a_ref, b_ref, o_ref):
        o_ref[...] = lax.dot_general(a_ref[...], b_ref[...], dims, precision=HIGHEST, preferred_element_type=F32)

    return _pcall(body, name=name, out_shape=_sds((a.shape[0], n), F32))(a, b)


def _tiled(fn, ins, outs, grid, name, acc=()):
    n_in = len(ins)
    grid = tuple(grid) or (1,)
    nd = len(grid)

    def body(*refs):
        vals = fn(*[r[...] for r in refs[:n_in]])
        if not isinstance(vals, (tuple, list)):
            vals = (vals,)
        first = None
        for o, (ref, v) in enumerate(zip(refs[n_in:], vals)):
            if o in acc:
                if first is None:
                    first = pl.program_id(0) == 0
                    for a in range(1, nd):
                        first = jnp.logical_and(first, pl.program_id(a) == 0)

                @pl.when(first)
                def _(ref=ref, v=v):
                    ref[...] = v.astype(ref.dtype)

                @pl.when(jnp.logical_not(first))
                def _(ref=ref, v=v):
                    ref[...] += v.astype(ref.dtype)
            else:
                ref[...] = v.astype(ref.dtype)

    res = _pcall(body, name=name, grid=grid,
                 in_specs=[pl.BlockSpec(b, im) for _, b, im in ins],
                 out_specs=[pl.BlockSpec(b, im) for _, _, b, im in outs],
                 out_shape=[_sds(s, d) for s, d, _, _ in outs])(*[a for a, _, _ in ins])
    return list(res)


def _rows(a, tr):
    return (a, (tr, a.shape[1]), lambda i, *_: (i, 0))


def _const(a):
    nd = a.ndim
    return (a, a.shape, lambda *_: (0,) * nd)


def _cast_bf16(w, name):
    r, c = w.shape
    tr = _pick(r, (256, 128, 64, 16))
    return _tiled(lambda v: v, [_rows(w, tr)], [((r, c), BF16, (tr, c), lambda i: (i, 0))], (r // tr,), name)[0]


def _f_norm_mod(x, g, sh, sc):
    y = x * lax.rsqrt(jnp.mean(x * x, axis=-1, keepdims=True) + EPS) * g
    return y * (1.0 + sc) + sh


def _rope_partner_impl(y):
    nf = HEAD_DIM // 4
    lane = lax.broadcasted_iota(jnp.int32, y.shape, 1)
    return jnp.where(lane % (2 * nf) < nf, pltpu.roll(y, HEAD_DIM - nf, 1), pltpu.roll(y, nf, 1))


_rope_partner = jax.custom_vjp(_rope_partner_impl)
_rope_partner.defvjp(lambda y: (_rope_partner_impl(y), None), lambda _, ct: (_rope_partner_impl(ct),))


def _f_qk(t, g, cos, sin):
    y = t * lax.rsqrt(jnp.mean(t * t, axis=-1, keepdims=True) + EPS) * g
    return y * cos + _rope_partner(y) * sin


def _f_readout(of, ob, gate, g):
    o = of + ob
    on = o * lax.rsqrt(jnp.mean(o * o, axis=-1, keepdims=True) + EPS) * g
    return on * (gate * jax.nn.sigmoid(gate))


def _f_merge(ga, gb, pa, pb):
    return jax.nn.sigmoid(ga) * pa + jax.nn.sigmoid(gb) * pb


def _f_resid(x, gate, m):
    return x + gate * m


def _norm_mod_fwd(xcat, g, mods, n_ctx_tiles, tr):
    tt, d = xcat.shape
    which = lambda i: (jnp.where(i >= n_ctx_tiles, 1, 0), 0, 0, 0)

    def fn(x, gg, md):
        return _f_norm_mod(x, gg, md[0], md[1])

    return _tiled(fn, [_rows(xcat, tr), _const(g), (mods, (None, 2, 1, d), which)],
                  [((tt, d), BF16, (tr, d), lambda i: (i, 0))], (tt // tr,), "norm_mod_fwd")[0]


def _norm_mod_bwd(xcat, g, mods, dh, extra, n_ctx_tiles, tr, name):
    tt, d = xcat.shape
    nt = tt // tr
    has_extra = extra is not None

    def body(*refs):
        if has_extra:
            x_ref, g_ref, m_ref, dh_ref, e_ref, dx_ref, dg_ref, dm_ref = refs
        else:
            x_ref, g_ref, m_ref, dh_ref, dx_ref, dg_ref, dm_ref = refs
        i = pl.program_id(0)
        md = m_ref[...]
        _, vjp = jax.vjp(_f_norm_mod, x_ref[...], g_ref[...], md[0], md[1])
        dx, dg, dsh, dsc = vjp(dh_ref[...].astype(F32))
        dx_ref[...] = dx + e_ref[...] if has_extra else dx

        @pl.when(i == 0)
        def _():
            dg_ref[...] = dg

        @pl.when(i > 0)
        def _():
            dg_ref[...] += dg

        fresh = jnp.logical_or(i == 0, i == n_ctx_tiles)

        @pl.when(fresh)
        def _():
            dm_ref[0] = dsh
            dm_ref[1] = dsc

        @pl.when(jnp.logical_not(fresh))
        def _():
            dm_ref[0] += dsh
            dm_ref[1] += dsc

    which = lambda i: (jnp.where(i >= n_ctx_tiles, 1, 0), 0, 0, 0)
    row = pl.BlockSpec((tr, d), lambda i: (i, 0))
    in_specs = [row, pl.BlockSpec((1, d), lambda i: (0, 0)), pl.BlockSpec((None, 2, 1, d), which), row]
    args = [xcat, g, mods, dh]
    if has_extra:
        in_specs.append(row)
        args.append(extra)
    return _pcall(body, name=name, grid=(nt,), in_specs=in_specs,
                  out_specs=[row, pl.BlockSpec((1, d), lambda i: (0, 0)), pl.BlockSpec((None, 2, 1, d), which)],
                  out_shape=[_sds((tt, d), F32), _sds((1, d), F32), _sds((2, 2, 1, d), F32)])(*args)


def _hgrn_tri(reverse):
    t = np.arange(HGRN_BLOCK)
    tri = (t[None, :] >= t[:, None]) if reverse else (t[None, :] <= t[:, None])
    tri = tri.astype(np.float32)
    return jnp.asarray(tri), jnp.asarray(tri.T.copy())


def _hgrn_rowblock(n, n_ctx_blocks, n_blocks, reverse):
    if not reverse:
        return n
    return jnp.where(n < n_ctx_blocks, n_ctx_blocks - 1 - n, n_blocks - 1 - n + n_ctx_blocks)


def _hgrn_gates(fl, lb):
    sg = jax.nn.sigmoid(fl)
    f = lb + (1.0 - lb) * sg
    return sg, f, jnp.log(f), 1.0 - f


def _hgrn_intra_mask(reverse):
    tio = lax.broadcasted_iota(jnp.int32, (HGRN_BLOCK, HEAD_DIM), 0)
    return (lambda s: tio <= s) if reverse else (lambda s: tio >= s)


class _Halves:
    def __init__(self, reverse):
        self.reverse = reverse
        self.h = HGRN_BLOCK // 2
        tio = lax.broadcasted_iota(jnp.int32, (self.h, HEAD_DIM), 0)
        self.tio = tio if reverse else tio + self.h

    def is_half(self, s):
        return s < self.h if self.reverse else s >= self.h

    def rows(self, a):
        return a[:self.h] if self.reverse else a[self.h:]

    def mask(self, s):
        return self.tio <= s if self.reverse else self.tio >= s

    def widen(self, full, half):
        z = jnp.zeros_like(half)
        return full + jnp.concatenate([half, z] if self.reverse else [z, half], axis=0)


def _unrolled(nblk, u, fn, init):
    assert nblk % u == 0

    def trip(b, c):
        for j in range(u):
            c = fn(b * u + j, j, c)
        return c

    return lax.fori_loop(0, nblk // u, trip, init)


HGRN_UNROLL_FWD = (4, 8)
HGRN_UNROLL_BWD = (8, 4)


def _hgrn_fwd(p, lb, seg_f, reverse, n_ctx_rows):
    tt = p.shape[0]
    hb = HGRN_BLOCK
    nblk, nctx = tt // hb, n_ctx_rows // hb
    tri, _ = _hgrn_tri(reverse)
    u1, u3 = HGRN_UNROLL_FWD

    def body(q_ref, f_ref, v_ref, lb_ref, tri_ref, o_ref, st_all, dec_all, cum_out, qe_buf, cum_blk, k_blk, v_blk):
        mask = _hgrn_intra_mask(reverse)
        hv = _Halves(reverse)
        lbv = lb_ref[...]

        def phase1(n, slot, c):
            r0 = pl.multiple_of(_hgrn_rowblock(n, nctx, nblk, reverse) * hb, hb)
            q, v = q_ref[pl.ds(r0, hb), :], v_ref[pl.ds(r0, hb), :]
            _, f, g, k = _hgrn_gates(f_ref[pl.ds(r0, hb), :], lbv)
            cum = jnp.dot(tri_ref[...], g, precision=HIGHEST, preferred_element_type=F32)
            tot = jnp.sum(g, axis=0, keepdims=True)
            cum_blk[slot] = cum
            k_blk[slot] = k
            v_blk[slot] = v
            oi = jnp.zeros((hb, HEAD_DIM), F32)
            oi_h = jnp.zeros((hb // 2, HEAD_DIM), F32)
            q_h, cum_h = hv.rows(q), hv.rows(cum)
            for s in range(hb):
                if hv.is_half(s):
                    e = jnp.where(hv.mask(s), jnp.exp(jnp.minimum(cum_h - cum_blk[slot, s:s + 1, :], 0.0)), 0.0)
                    a_s = jnp.sum(q_h * e * k_blk[slot, s:s + 1, :], axis=-1, keepdims=True)
                    oi_h = oi_h + a_s * v_blk[slot, s:s + 1, :]
                else:
                    e = jnp.where(mask(s), jnp.exp(jnp.minimum(cum - cum_blk[slot, s:s + 1, :], 0.0)), 0.0)
                    a_s = jnp.sum(q * e * k_blk[slot, s:s + 1, :], axis=-1, keepdims=True)
                    oi = oi + a_s * v_blk[slot, s:s + 1, :]
            o_ref[pl.ds(r0, hb), :] = hv.widen(oi, oi_h)
            cum_out[pl.ds(r0, hb), :] = cum
            qe_buf[pl.ds(r0, hb), :] = q * jnp.exp(cum)
            kl = k * jnp.exp(tot - cum)
            st_all[n] = lax.dot_general(v.astype(BF16), kl.astype(BF16), (((0,), (0,)), ((), ())), preferred_element_type=F32)
            dec_all[pl.ds(n, 1), :] = jnp.exp(tot)
            return c

        _unrolled(nblk, u1, phase1, 0)

        def phase2(n, st):
            kv = st_all[n]
            st_all[n] = st
            return st * dec_all[pl.ds(n, 1), :] + kv

        lax.fori_loop(0, nblk, phase2, jnp.zeros((HEAD_DIM, HEAD_DIM), F32))

        def phase3(n, slot, c):
            r0 = pl.multiple_of(_hgrn_rowblock(n, nctx, nblk, reverse) * hb, hb)
            o_ref[pl.ds(r0, hb), :] += lax.dot_general(qe_buf[pl.ds(r0, hb), :].astype(BF16), st_all[n].astype(BF16),
                                                        (((1,), (1,)), ((), ())), preferred_element_type=F32)
            return c

        _unrolled(nblk, u3, phase3, 0)

    col = lambda seg: pl.BlockSpec((tt, HEAD_DIM), lambda h, seg=seg: (0, seg * N_HEADS + h))
    blk = pltpu.VMEM((u1, hb, HEAD_DIM), F32)
    head = pl.BlockSpec((tt, HEAD_DIM), lambda h: (0, h))
    return _pcall(
        body, name="hgrn_fwd_rev" if reverse else "hgrn_fwd", grid=(N_HEADS,),
        in_specs=[col(0), col(seg_f), col(3), pl.BlockSpec((1, HEAD_DIM), lambda h: (0, h)), pl.BlockSpec((hb, hb), lambda h: (0, 0))],
        out_specs=[head, pl.BlockSpec((None, nblk, HEAD_DIM, HEAD_DIM), lambda h: (h, 0, 0, 0)),
                   pl.BlockSpec((None, nblk, HEAD_DIM), lambda h: (h, 0, 0)), head],
        out_shape=[_sds((tt, N_HEADS * HEAD_DIM), F32), _sds((N_HEADS, nblk, HEAD_DIM, HEAD_DIM), F32),
                   _sds((N_HEADS, nblk, HEAD_DIM), F32), _sds((tt, N_HEADS * HEAD_DIM), F32)],
        scratch=[pltpu.VMEM((tt, HEAD_DIM), F32), blk, blk, blk])(p, p, p, lb, tri)


def _hgrn_bwd(p, lb, do, saved, seg_f, reverse, n_ctx_rows, prev):
    tt = p.shape[0]
    hb = HGRN_BLOCK
    nblk, nctx = tt // hb, n_ctx_rows // hb
    _, tri_t = _hgrn_tri(reverse)
    last_row = 0 if reverse else hb - 1
    has_prev = prev is not None
    u1, u3 = HGRN_UNROLL_BWD

    def body(*refs):
        q_ref, f_ref, v_ref, lb_ref, trit_ref, do_ref, st_all, dec_all, cum_buf = refs[:9]
        refs = refs[9:]
        if has_prev:
            pq_ref, pv_ref = refs[:2]
            refs = refs[2:]
        dq_ref, dfl_ref, dv_ref, dlb_ref, dd_all, cum_blk, k_blk, v_blk, dk_blk, dv_blk = refs
        mask = _hgrn_intra_mask(reverse)
        hv = _Halves(reverse)
        lbv = lb_ref[...]
        tio = lax.broadcasted_iota(jnp.int32, (hb, HEAD_DIM), 0)

        def rows_of(n):
            rb = _hgrn_rowblock(n, nctx, nblk, reverse)
            return rb, pl.multiple_of(rb * hb, hb)

        def load_do(rb):
            lat0 = pl.multiple_of(jnp.maximum(rb - nctx, 0) * hb, hb)
            return jnp.where(rb >= nctx, do_ref[pl.ds(lat0, hb), :], 0.0)

        def phase1(n, slot, c):
            rb, r0 = rows_of(n)
            qe = q_ref[pl.ds(r0, hb), :] * jnp.exp(cum_buf[pl.ds(r0, hb), :])
            dd_all[n] = lax.dot_general(load_do(rb).astype(BF16), qe.astype(BF16), (((0,), (0,)), ((), ())), preferred_element_type=F32)
            return c

        _unrolled(nblk, u1, phase1, 0)

        def phase2r(i, dst):
            n = nblk - 1 - i
            u = dd_all[n]
            dd_all[n] = dst
            return u + dst * dec_all[pl.ds(n, 1), :]

        lax.fori_loop(0, nblk, phase2r, jnp.zeros((HEAD_DIM, HEAD_DIM), F32))

        def phase3(n, slot, dlb):
            rb, r0 = rows_of(n)
            q, v = q_ref[pl.ds(r0, hb), :], v_ref[pl.ds(r0, hb), :]
            sg, f, g, k = _hgrn_gates(f_ref[pl.ds(r0, hb), :], lbv)
            cum = cum_buf[pl.ds(r0, hb), :]
            tot = jnp.sum(g, axis=0, keepdims=True)
            dob = load_do(rb)
            st, dst = st_all[n], dd_all[n]
            e_cum = jnp.exp(cum)
            e_rest = jnp.exp(tot - cum)
            dq = jnp.dot(dob.astype(BF16), st.astype(BF16), preferred_element_type=F32) * e_cum
            dk_inter = jnp.dot(v.astype(BF16), dst.astype(BF16), preferred_element_type=F32) * e_rest
            dv = lax.dot_general((k * e_rest).astype(BF16), dst.astype(BF16), (((1,), (1,)), ((), ())), preferred_element_type=F32)
            cum_blk[slot] = cum
            k_blk[slot] = k
            v_blk[slot] = v
            dq_h = jnp.zeros((hb // 2, HEAD_DIM), F32)
            q_h, cum_h, dob_h = hv.rows(q), hv.rows(cum), hv.rows(dob)
            for s in range(hb):
                half = hv.is_half(s)
                qs, cs, ds = (q_h, cum_h, dob_h) if half else (q, cum, dob)
                e = jnp.where(hv.mask(s) if half else mask(s), jnp.exp(jnp.minimum(cs - cum_blk[slot, s:s + 1, :], 0.0)), 0.0)
                a_s = jnp.sum(qs * e * k_blk[slot, s:s + 1, :], axis=-1, keepdims=True)
                da_s = jnp.sum(ds * v_blk[slot, s:s + 1, :], axis=-1, keepdims=True)
                gs = da_s * e
                if half:
                    dq_h = dq_h + gs * k_blk[slot, s:s + 1, :]
                else:
                    dq = dq + gs * k_blk[slot, s:s + 1, :]
                dk_blk[slot, s:s + 1, :] = jnp.sum(gs * qs, axis=0, keepdims=True)
                dv_blk[slot, s:s + 1, :] = jnp.sum(a_s * ds, axis=0, keepdims=True)
            dq = hv.widen(dq, dq_h)
            dk = dk_inter + dk_blk[slot]
            dv = dv + dv_blk[slot]
            d_tot = jnp.sum(k * dk_inter, axis=0, keepdims=True) + jnp.exp(tot) * jnp.sum(dst * st, axis=0, keepdims=True)
            dcum = q * dq - k * dk + jnp.where(tio == last_row, d_tot, 0.0)
            dg = jnp.dot(trit_ref[...], dcum, precision=HIGHEST, preferred_element_type=F32)
            df = dg / f - dk
            if has_prev:
                dq = dq + pq_ref[pl.ds(r0, hb), :].astype(F32)
                dv = dv + pv_ref[pl.ds(r0, hb), :].astype(F32)
            dq_ref[pl.ds(r0, hb), :] = dq.astype(dq_ref.dtype)
            dv_ref[pl.ds(r0, hb), :] = dv.astype(dv_ref.dtype)
            dfl_ref[pl.ds(r0, hb), :] = (df * (1.0 - lbv) * sg * (1.0 - sg)).astype(dfl_ref.dtype)
            return dlb + jnp.sum(df * (1.0 - sg), axis=0, keepdims=True)

        dlb_ref[...] = _unrolled(nblk, u3, phase3, jnp.zeros((1, HEAD_DIM), F32))

    col = lambda seg: pl.BlockSpec((tt, HEAD_DIM), lambda h, seg=seg: (0, seg * N_HEADS + h))
    head = pl.BlockSpec((tt, HEAD_DIM), lambda h: (0, h))
    lbs = pl.BlockSpec((1, HEAD_DIM), lambda h: (0, h))
    tris = pl.BlockSpec((hb, hb), lambda h: (0, 0))
    in_specs = [col(0), col(seg_f), col(3), lbs, tris, pl.BlockSpec((do.shape[0], HEAD_DIM), lambda h: (0, h)),
                pl.BlockSpec((None, nblk, HEAD_DIM, HEAD_DIM), lambda h: (h, 0, 0, 0)), pl.BlockSpec((None, nblk, HEAD_DIM), lambda h: (h, 0, 0)), head]
    args = [p, p, p, lb, tri_t, do, *saved]
    mid = F32 if not has_prev else BF16
    if has_prev:
        in_specs += [head, head]
        args += list(prev)
    w = N_HEADS * HEAD_DIM
    blk = pltpu.VMEM((u3, hb, HEAD_DIM), F32)
    return _pcall(
        body, name="hgrn_bwd_rev" if reverse else "hgrn_bwd", grid=(N_HEADS,), in_specs=in_specs,
        out_specs=[head, head, head, lbs],
        out_shape=[_sds((tt, w), mid), _sds((tt, w), BF16), _sds((tt, w), mid), _sds((1, w), F32)],
        scratch=[pltpu.VMEM((nblk, HEAD_DIM, HEAD_DIM), F32), blk, blk, blk, blk, blk])(*args)


NA_HEADS_PER_STEP = 4


def _na_geometry(rows):
    r = pl.program_id(1)
    rs = jnp.clip(r - WIN_R // 2, 0, rows - WIN_R)
    return r, rs, r - rs


def _na_scores(q, kb, kc, bias):
    scale = HEAD_DIM ** -0.5
    nt = (((1,), (1,)), ((), ()))
    sb = lax.dot_general(q, kb, nt, preferred_element_type=F32) * scale + bias
    sc = lax.dot_general(q, kc, nt, preferred_element_type=F32) * scale
    m = jnp.maximum(jnp.max(sb, axis=-1, keepdims=True), jnp.max(sc, axis=-1, keepdims=True))
    pb, pc = jnp.exp(sb - m), jnp.exp(sc - m)
    inv = 1.0 / (jnp.sum(pb, axis=-1, keepdims=True) + jnp.sum(pc, axis=-1, keepdims=True))
    return pb * inv, pc * inv


def _na_fwd(qn, kall, p, seg_v, bias, n_ctx_rows):
    t, tt = qn.shape[0], kall.shape[0]
    rows = t // GRID_W
    nband = WIN_R * GRID_W

    nh = NA_HEADS_PER_STEP
    wide = nh * HEAD_DIM

    def body(q_ref, k_ref, v_ref, b_ref, o_ref):
        r, rs, _ = _na_geometry(rows)
        k0 = pl.multiple_of(n_ctx_rows + rs * GRID_W, GRID_W)
        for j in range(nh):
            sl = slice(j * HEAD_DIM, (j + 1) * HEAD_DIM)
            q = q_ref[:, sl]
            kb, kc = k_ref[pl.ds(k0, nband), sl], k_ref[pl.ds(0, n_ctx_rows), sl]
            vb, vc = v_ref[pl.ds(k0, nband), sl].astype(BF16), v_ref[pl.ds(0, n_ctx_rows), sl].astype(BF16)
            pb, pc = _na_scores(q, kb, kc, b_ref[j])
            o = jnp.dot(pb.astype(BF16), vb, preferred_element_type=F32) + jnp.dot(pc.astype(BF16), vc, preferred_element_type=F32)
            o_ref[:, sl] = o.astype(o_ref.dtype)

    variant = lambda h, r: (h, r - jnp.clip(r - WIN_R // 2, 0, rows - WIN_R), 0, 0)
    return _pcall(
        body, name="na_fwd", grid=(N_HEADS // nh, rows),
        in_specs=[pl.BlockSpec((GRID_W, wide), lambda h, r: (r, h)),
                  pl.BlockSpec((tt, wide), lambda h, r: (0, h)),
                  pl.BlockSpec((tt, wide), lambda h, r: (0, seg_v * (N_HEADS // nh) + h)),
                  pl.BlockSpec((nh, None, GRID_W, nband), variant)],
        out_specs=pl.BlockSpec((GRID_W, wide), lambda h, r: (r, h)),
        out_shape=_sds((t, N_HEADS * HEAD_DIM), BF16))(qn, kall, p, bias)


def _na_bwd(qn, kall, p, seg_v, bias, do, n_ctx_rows):
    t, tt = qn.shape[0], kall.shape[0]
    rows = t // GRID_W
    nband = WIN_R * GRID_W
    scale = HEAD_DIM ** -0.5
    tn = (((0,), (0,)), ((), ()))
    nt = (((1,), (1,)), ((), ()))

    nh = NA_HEADS_PER_STEP
    wide = nh * HEAD_DIM

    def body(q_ref, k_ref, v_ref, b_ref, do_ref, dq_ref, dk_ref, dv_ref, db_ref, dv_acc):
        r, rs, var = _na_geometry(rows)
        k0 = pl.multiple_of(n_ctx_rows + rs * GRID_W, GRID_W)
        fresh = jnp.logical_or(r <= WIN_R // 2, r > rows - WIN_R // 2)

        @pl.when(r == 0)
        def _():
            dk_ref[...] = jnp.zeros_like(dk_ref)
            dv_acc[...] = jnp.zeros_like(dv_acc)

        for j in range(nh):
            sl = slice(j * HEAD_DIM, (j + 1) * HEAD_DIM)
            q = q_ref[:, sl]
            kb, kc = k_ref[pl.ds(k0, nband), sl], k_ref[pl.ds(0, n_ctx_rows), sl]
            vb, vc = v_ref[pl.ds(k0, nband), sl].astype(BF16), v_ref[pl.ds(0, n_ctx_rows), sl].astype(BF16)
            pb, pc = _na_scores(q, kb, kc, b_ref[j])
            dof = do_ref[:, sl].astype(F32)
            dob = dof.astype(BF16)
            o = jnp.dot(pb.astype(BF16), vb, preferred_element_type=F32) + jnp.dot(pc.astype(BF16), vc, preferred_element_type=F32)
            delta = jnp.sum(dof * o, axis=-1, keepdims=True)
            dsb = pb * (lax.dot_general(dob, vb, nt, preferred_element_type=F32) - delta)
            dsc = pc * (lax.dot_general(dob, vc, nt, preferred_element_type=F32) - delta)
            dsb16, dsc16 = dsb.astype(BF16), dsc.astype(BF16)
            dq_ref[:, sl] = (jnp.dot(dsb16, kb, preferred_element_type=F32) + jnp.dot(dsc16, kc, preferred_element_type=F32)) * scale
            dk_ref[pl.ds(k0, nband), sl] += lax.dot_general(dsb16, q, tn, preferred_element_type=F32) * scale
            dk_ref[pl.ds(0, n_ctx_rows), sl] += lax.dot_general(dsc16, q, tn, preferred_element_type=F32) * scale
            dv_acc[pl.ds(k0, nband), sl] += lax.dot_general(pb.astype(BF16), dob, tn, preferred_element_type=F32)
            dv_acc[pl.ds(0, n_ctx_rows), sl] += lax.dot_general(pc.astype(BF16), dob, tn, preferred_element_type=F32)

            @pl.when(fresh)
            def _(j=j, dsb=dsb):
                db_ref[j] = dsb

            @pl.when(jnp.logical_not(fresh))
            def _(j=j, dsb=dsb):
                db_ref[j] += dsb

        @pl.when(r == rows - 1)
        def _():
            dv_ref[...] = dv_acc[...].astype(dv_ref.dtype)

    variant = lambda h, r: (h, r - jnp.clip(r - WIN_R // 2, 0, rows - WIN_R), 0, 0)
    head_all = pl.BlockSpec((tt, wide), lambda h, r: (0, h))
    qspec = pl.BlockSpec((GRID_W, wide), lambda h, r: (r, h))
    w = N_HEADS * HEAD_DIM
    return _pcall(
        body, name="na_bwd", grid=(N_HEADS // nh, rows),
        in_specs=[qspec, head_all, pl.BlockSpec((tt, wide), lambda h, r: (0, seg_v * (N_HEADS // nh) + h)),
                  pl.BlockSpec((nh, None, GRID_W, nband), variant), qspec],
        out_specs=[qspec, head_all, head_all, pl.BlockSpec((nh, None, GRID_W, nband), variant)],
        out_shape=[_sds((t, w), F32), _sds((tt, w), F32), _sds((tt, w), BF16), _sds((N_HEADS, WIN_R, GRID_W, nband), F32)],
        scratch=[pltpu.VMEM((tt, wide), F32)])(qn, kall, p, bias, do)


def _na_tables(t, n_ctx_rows):
    half, nf = HEAD_DIM // 2, HEAD_DIM // 4
    pos = np.arange(t)
    lane = np.arange(HEAD_DIM)
    inv = ROPE_THETA ** (-(np.arange(nf, dtype=np.float32)) / nf)
    which = np.where(lane < half, pos[:, None] // GRID_W, pos[:, None] % GRID_W).astype(np.float32)
    ang = which * inv[lane % nf][None, :]
    first = (lane % half) < nf
    cos = np.concatenate([np.ones((n_ctx_rows, HEAD_DIM), np.float32), np.cos(ang).astype(np.float32)])
    sin = np.concatenate([np.zeros((n_ctx_rows, HEAD_DIM), np.float32), np.where(first[None, :], -np.sin(ang), np.sin(ang)).astype(np.float32)])
    w = np.arange(GRID_W)
    dc = np.clip(w[None, :] - w[:, None], -(WIN_C - 1), WIN_C - 1) + WIN_C - 1
    onehot = np.zeros((32, GRID_W * GRID_W), np.float32)
    onehot[dc.reshape(-1), np.arange(GRID_W * GRID_W)] = 1.0
    cs = np.clip(w - WIN_C // 2, 0, GRID_W - WIN_C)
    col_in = (w[None, :] >= cs[:, None]) & (w[None, :] < cs[:, None] + WIN_C)
    onehot *= col_in.reshape(1, -1)
    neg = np.where(col_in, 0.0, NEG_BIG).astype(np.float32)
    return jnp.asarray(cos), jnp.asarray(sin), jnp.asarray(onehot), jnp.asarray(neg)


def _bias_slabs(rel_bias, onehot, neg):
    nr = 2 * WIN_R - 1
    rb = jnp.pad(rel_bias.reshape(N_HEADS * nr, 2 * WIN_C - 1), ((0, 0), (0, 1)))
    spread = _mm_f32(rb, onehot, "bias_spread").reshape(N_HEADS, nr, GRID_W, GRID_W)
    slabs = [spread[:, WIN_R - 1 - v:2 * WIN_R - 1 - v] for v in range(WIN_R)]
    b = jnp.stack(slabs, axis=1) + neg[None, None, None]
    return b.transpose(0, 1, 3, 2, 4).reshape(N_HEADS, WIN_R, GRID_W, WIN_R * GRID_W)


def _bias_grad(dbias, onehot):
    nr = 2 * WIN_R - 1
    d = dbias.reshape(N_HEADS, WIN_R, GRID_W, WIN_R, GRID_W).transpose(0, 1, 3, 2, 4)
    tot = jnp.zeros((N_HEADS, nr, GRID_W, GRID_W), F32)
    for v in range(WIN_R):
        tot = tot + jnp.pad(d[:, v], ((0, 0), (WIN_R - 1 - v, v), (0, 0), (0, 0)))
    g = _mm_f32(tot.reshape(N_HEADS * nr, GRID_W * GRID_W), onehot, "bias_grad", trans_b=True)
    return g[:, :2 * WIN_C - 1].reshape(1, N_HEADS, nr, 2 * WIN_C - 1)


def _shift_rows(u, up):
    n = u.shape[0]
    tio = lax.broadcasted_iota(jnp.int32, u.shape, 0)
    if up:
        return jnp.where(tio == n - 1, 0.0, pltpu.roll(u, n - 1, 0))
    return jnp.where(tio == 0, 0.0, pltpu.roll(u, 1, 0))


def _conv3(u, w_ref, b_ref):
    um, up = _shift_rows(u, False), _shift_rows(u, True)
    return um, up, um * w_ref[0:1, :] + u * w_ref[1:2, :] + up * w_ref[2:3, :] + b_ref[...]


def _ffn_act_fwd(u0, t3, cw, cb):
    t, n = u0.shape
    tc = _pick(n, (256, 128))

    def body(u_ref, t_ref, w_ref, b_ref, a_ref):
        _, _, uc = _conv3(u_ref[...], w_ref, b_ref)
        a_ref[...] = (uc * jax.nn.sigmoid(uc) * t_ref[...]).astype(a_ref.dtype)

    col = lambda rows_: pl.BlockSpec((rows_, tc), lambda j: (0, j))
    return _pcall(body, name="ffn_act_fwd", grid=(n // tc,), in_specs=[col(t), col(t), col(8), col(1)], out_specs=col(t),
                  out_shape=_sds((t, n), BF16))(u0, t3, cw, cb)


def _ffn_act_bwd(u0, t3, cw, cb, da):
    t, n = u0.shape
    tc = _pick(n, (256, 128))

    def body(u_ref, t_ref, w_ref, b_ref, da_ref, du_ref, dt_ref, dw_ref, db_ref):
        u = u_ref[...]
        um, up, uc = _conv3(u, w_ref, b_ref)
        sg = jax.nn.sigmoid(uc)
        dav = da_ref[...].astype(F32)
        dt_ref[...] = (dav * uc * sg).astype(dt_ref.dtype)
        duc = dav * t_ref[...] * sg * (1.0 + uc * (1.0 - sg))
        du = _shift_rows(duc, True) * w_ref[0:1, :] + duc * w_ref[1:2, :] + _shift_rows(duc, False) * w_ref[2:3, :]
        du_ref[...] = du.astype(du_ref.dtype)
        dw_ref[...] = jnp.zeros_like(dw_ref)
        dw_ref[0:1, :] = jnp.sum(duc * um, axis=0, keepdims=True)
        dw_ref[1:2, :] = jnp.sum(duc * u, axis=0, keepdims=True)
        dw_ref[2:3, :] = jnp.sum(duc * up, axis=0, keepdims=True)
        db_ref[...] = jnp.sum(duc, axis=0, keepdims=True)

    col = lambda rows_: pl.BlockSpec((rows_, tc), lambda j: (0, j))
    return _pcall(body, name="ffn_act_bwd", grid=(n // tc,), in_specs=[col(t), col(t), col(8), col(1), col(t)],
                  out_specs=[col(t), col(t), col(8), col(1)],
                  out_shape=[_sds((t, n), BF16), _sds((t, n), BF16), _sds((8, n), F32), _sds((1, n), F32)])(u0, t3, cw, cb, da)


def _adam_math(g, w, m, v):
    m2 = ADAM_B1 * m + (1.0 - ADAM_B1) * g
    v2 = ADAM_B2 * v + (1.0 - ADAM_B2) * (g * g)
    m_hat = m2 / (1.0 - ADAM_B1 ** ADAM_STEP)
    v_hat = v2 / (1.0 - ADAM_B2 ** ADAM_STEP)
    return -ADAM_LR * (m_hat / (jnp.sqrt(v_hat) + ADAM_EPS) + ADAM_WD * w), m2, v2


def _adam_big(parts, w, m, v, name):
    r, c = w.shape
    npart, _, cp = parts.shape
    tr = _pick(r, (64, 32, 16, 8))

    def body(p_ref, w_ref, m_ref, v_ref, g_ref, d_ref, m2_ref, v2_ref):
        g = p_ref[0, :, 0:c].astype(F32)
        for i in range(1, npart):
            g = g + p_ref[i, :, 0:c].astype(F32)
        d, m2, v2 = _adam_math(g, w_ref[...], m_ref[...], v_ref[...])
        g_ref[...] = g
        d_ref[...] = d
        m2_ref[...] = m2
        v2_ref[...] = v2

    row = pl.BlockSpec((tr, c), lambda i: (i, 0))
    return _pcall(body, name=name, grid=(r // tr,),
                  in_specs=[pl.BlockSpec((npart, tr, cp), lambda i: (0, i, 0)), row, row, row],
                  out_specs=[row] * 4, out_shape=[_sds((r, c), F32)] * 4)(parts, w, m, v)


def _adam_small(g, w, m, v):
    def body(g_ref, w_ref, m_ref, v_ref, d_ref, m2_ref, v2_ref):
        d_ref[...], m2_ref[...], v2_ref[...] = _adam_math(g_ref[...], w_ref[...], m_ref[...], v_ref[...])

    return _pcall(body, name="adam_small", out_shape=[_sds(g.shape, F32)] * 3)(g, w, m, v)


def _sum_parts(parts, name):
    def body(p_ref, o_ref):
        s = p_ref[0]
        for i in range(1, N_DEV):
            s = s + p_ref[i]
        o_ref[...] = s

    return _pcall(body, name=name, out_shape=_sds(parts.shape[1:], F32))(parts)


class _Pack:
    def __init__(self, shapes):
        self.shapes = shapes
        self.sizes = [int(np.prod(s)) for s in shapes]
        self.padded = [-(-n // (8 * LANE)) * 8 * LANE for n in self.sizes]
        self.offs = np.concatenate([[0], np.cumsum(self.padded)]).tolist()

    def pack(self, arrs):
        flat = [jnp.pad(a.reshape(-1).astype(F32), (0, p - n)) for a, n, p in zip(arrs, self.sizes, self.padded)]
        return jnp.concatenate(flat).reshape(-1, LANE)

    def unpack(self, slab):
        flat = slab.reshape(-1)
        return [flat[o:o + n].reshape(s) for o, n, s in zip(self.offs, self.sizes, self.shapes)]


def kernel(x, c, ctx, c_ctx, ada_w, ada_b, norm1_g, norm2_g, w_in, hgrn_lb_logits, hgrn_norm_g, na_q_norm_g, na_k_norm_g, na_rel_bias, w_branch_a, w_branch_b, w_out, ffn_w1, ffn_w3, ffn_conv_w, ffn_conv_b, ffn_w2, loss_target, m_c_ctx, m_ada_w, m_ada_b, m_norm1_g, m_norm2_g, m_w_in, m_hgrn_lb_logits, m_hgrn_norm_g, m_na_q_norm_g, m_na_k_norm_g, m_na_rel_bias, m_w_branch_a, m_w_branch_b, m_w_out, m_ffn_w1, m_ffn_w3, m_ffn_conv_w, m_ffn_conv_b, m_ffn_w2, v_c_ctx, v_ada_w, v_ada_b, v_norm1_g, v_norm2_g, v_w_in, v_hgrn_lb_logits, v_hgrn_norm_g, v_na_q_norm_g, v_na_k_norm_g, v_na_rel_bias, v_w_branch_a, v_w_branch_b, v_w_out, v_ffn_w1, v_ffn_w3, v_ffn_conv_w, v_ffn_conv_b, v_ffn_w2):
    t, d = x.shape[1], x.shape[2]
    n_ctx = ctx.shape[1]
    tt = n_ctx + t
    hw = N_HEADS * HEAD_DIM
    ci = w_in.shape[2]
    ca = ada_w.shape[2]
    ff_l = ffn_w1.shape[2]
    ff_p = -(-ff_l // LANE) * LANE
    rows = t // GRID_W
    assert rows >= WIN_R and t % GRID_W == 0 and n_ctx % GRID_W == 0 and ci % LANE == 0 and d % LANE == 0
    me = 4 * lax.axis_index("x") + 2 * lax.axis_index("y") + lax.axis_index("c")
    tr = _pick(n_ctx, (256, 128, 64))
    n_ctx_tiles = n_ctx // tr

    pad_c = lambda w: jnp.pad(w, ((0, 0), (0, ff_p - ff_l)))
    small_in = [c, hgrn_lb_logits.reshape(4, HEAD_DIM), jnp.pad(ffn_conv_w[0], ((0, 5), (0, ff_p - ff_l)))]
    c_all, lb_parts, cw_all = _exchange(small_in, "gather_params", scatter=False)
    ff = N_DEV * ff_p

    cc = jnp.concatenate([c_all.reshape(N_DEV, d), jnp.broadcast_to(c_ctx[None, :], (N_DEV, d))], axis=0)
    act = _tiled(lambda v: v * jax.nn.sigmoid(v), [_const(cc)], [(cc.shape, BF16, cc.shape, lambda *_: (0, 0))], (), "silu_c")[0]
    ada16 = _cast_bf16(ada_w[0], "cast_ada")
    mod_cols = _mm_xw(act, ada16.reshape(1, d, ca), F32, "ada_fwd")
    mod_all = _exchange([mod_cols], "gather_mod", scatter=False)[0]

    with _after(mod_all):
        w_in16 = _cast_bf16(w_in[0], "cast_w_in")
    g_in, = _gather_sc([w_in16], "gather_w_in", 1)
    rest16 = []
    for w_, nm in ((w_branch_a[0], "cast_w_a"), (w_branch_b[0], "cast_w_b"), (w_out[0], "cast_w_out"), (pad_c(ffn_w1[0]), "cast_w1"),
                   (pad_c(ffn_w3[0]), "cast_w3"), (jnp.pad(ffn_w2[0], ((0, ff_p - ff_l), (0, 0))), "cast_w2")):
        with _after(w_in16):
            rest16.append(_cast_bf16(w_, nm))
    g_a, g_b, g_out = _gather_sc(rest16[:3], "gather_mix", 2)
    g_w1, g_w3, g_w2 = _gather_sc(rest16[3:], "gather_ffn", 10)
    g_out = g_out.reshape(1, d, d)
    g_w2 = g_w2.reshape(1, ff, d)
    mod_all = mod_all.transpose(1, 0, 2).reshape(2 * N_DEV, N_MOD * d) + ada_b
    mod_l = lax.dynamic_slice_in_dim(mod_all, me, 1, axis=0).reshape(N_MOD, 1, d)
    mod_c = mod_all[N_DEV:N_DEV + 1].reshape(N_MOD, 1, d)
    mods1 = jnp.stack([mod_c[0:2], mod_l[0:2]])
    mods2 = jnp.stack([mod_l[3:5], mod_l[3:5]])
    gate1, gate2 = mod_l[2], mod_l[5]

    xcat = jnp.concatenate([ctx[0], x[0]], axis=0)
    hcat = _norm_mod_fwd(xcat, norm1_g, mods1, n_ctx_tiles, tr)
    p = _mm_xw(hcat, g_in, F32, "in_proj")
    lb_logits = lb_parts.transpose(1, 0, 2).reshape(2, 2, hw)
    lb_soft = _tiled(lambda a, b: (1.0 / (1.0 + jnp.exp(b - a)),), [_const(lb_logits[:, 0]), _const(lb_logits[:, 1])],
                     [((2, hw), F32, (2, hw), lambda *_: (0, 0))], (), "lb_softmax")[0]
    lb_f, lb_b = lb_soft[0:1], lb_soft[1:2]
    o_f, *saved_f = _hgrn_fwd(p, lb_f, 1, False, n_ctx)
    o_b, *saved_b = _hgrn_fwd(p, lb_b, 2, True, n_ctx)

    cos, sin, onehot, neg = _na_tables(t, n_ctx)
    bias = _bias_slabs(na_rel_bias[0], onehot, neg)
    heads = lambda a: [a[:, h * HEAD_DIM:(h + 1) * HEAD_DIM] for h in range(N_HEADS)]
    seg_rows = lambda seg, off=0: (p, (tr, hw), lambda i: (i + off, seg))
    wide = lambda a, off=0: (a, (tr, hw), lambda i: (i + off, 0))
    out_wide = lambda n_rows, dt: ((n_rows, hw), dt, (tr, hw), lambda i: (i, 0))
    acc_head = ((1, HEAD_DIM), F32, (1, HEAD_DIM), lambda i: (0, 0))
    lat0 = n_ctx // tr
    tab = lambda a, off=0: (a, (tr, HEAD_DIM), lambda i: (i + off, 0))

    def qk_fn(tv, g, cs, sn):
        return (jnp.concatenate([_f_qk(th, g, cs, sn) for th in heads(tv)], axis=1),)

    qn = _tiled(qk_fn, [seg_rows(5, lat0), _const(na_q_norm_g), tab(cos, lat0), tab(sin, lat0)], [out_wide(t, BF16)], (t // tr,), "q_norm_rope")[0]
    kall = _tiled(qk_fn, [seg_rows(6), _const(na_k_norm_g), tab(cos), tab(sin)], [out_wide(tt, BF16)], (tt // tr,), "k_norm_rope")[0]
    y_b = _na_fwd(qn, kall, p, 7, bias, n_ctx)

    def readout_fn(a, b, gt, g):
        return (jnp.concatenate([_f_readout(ah, bh, gh, g) for ah, bh, gh in zip(heads(a), heads(b), heads(gt))], axis=1),)

    y_a = _tiled(readout_fn, [wide(o_f, lat0), wide(o_b, lat0), seg_rows(4, lat0), _const(hgrn_norm_g)], [out_wide(t, BF16)], (t // tr,), "hgrn_readout")[0]

    p_a = _mm_xw(y_a, g_a, F32, "branch_a")
    p_b = _mm_xw(y_b, g_b, F32, "branch_b")
    td = _pick(d, (512, 256, 128))
    nd_t = d // td
    lat_r = n_ctx // tr
    gcol = lambda k: (p, (tr, td), lambda i, j, k=k: (i + lat_r, 8 * hw // td + k * nd_t + j))
    dtile = lambda a: (a, (tr, td), lambda i, j: (i, j))
    z = _tiled(lambda ga, gb, pa, pb: (_f_merge(ga, gb, pa, pb),), [gcol(0), gcol(1), dtile(p_a), dtile(p_b)],
               [((t, d), BF16, (tr, td), lambda i, j: (i, j))], (t // tr, nd_t), "merge")[0]
    mix = _mm_xw(z, g_out, F32, "out_proj")
    xl = x[0]
    x_mid = _tiled(lambda a, g, m_: (_f_resid(a, g, m_),), [_rows(xl, tr), _const(gate1), _rows(mix, tr)],
                   [((t, d), F32, (tr, d), lambda i: (i, 0))], (t // tr,), "resid1")[0]

    h2 = _norm_mod_fwd(x_mid, norm2_g, mods2, 0, tr)
    u0 = _mm_xw(h2, g_w1, F32, "ffn_up1")
    t3 = _mm_xw(h2, g_w3, F32, "ffn_up3")
    cw_full = cw_all.transpose(1, 0, 2).reshape(8, ff)
    cb_full = jnp.pad(ffn_conv_b.reshape(N_DEV, ff_l), ((0, 0), (0, ff_p - ff_l))).reshape(1, ff)
    a_act = _ffn_act_fwd(u0, t3, cw_full, cb_full)
    f_out = _mm_xw(a_act, g_w2, F32, "ffn_down")

    def loss_fn(xm, g, f, tg):
        err = xm + g * f - tg
        return err * (1.0 / d), jnp.sum(err * err, axis=0, keepdims=True) * (0.5 / d), jnp.sum(err * (1.0 / d) * f, axis=0, keepdims=True)

    dy, loss_cols, d_gate2 = _tiled(loss_fn, [_rows(x_mid, tr), _const(gate2), _rows(f_out, tr), _rows(loss_target[0], tr)],
                                    [((t, d), F32, (tr, d), lambda i: (i, 0)), ((1, d), F32, (1, d), lambda i: (0, 0)),
                                     ((1, d), F32, (1, d), lambda i: (0, 0))], (t // tr,), "loss", acc=(1, 2))

    df = _tiled(lambda a, g: (a * g,), [_rows(dy, tr), _const(gate2)], [((t, d), BF16, (tr, d), lambda i: (i, 0))], (t // tr,), "d_ffn_out")[0]
    d_w2 = _mm_xtdy(a_act, df, 1, BF16, "d_w2")
    da = _mm_dyw(df, g_w2, BF16, "d_act")
    du0, dt3, d_cw, d_cb = _ffn_act_bwd(u0, t3, cw_full, cb_full, da)
    d_w1 = _mm_xtdy(h2, du0, N_DEV, BF16, "d_w1")
    d_w3 = _mm_xtdy(h2, dt3, N_DEV, BF16, "d_w3")
    ffn_blocks = [d_w1, d_w3, d_w2.reshape(N_DEV, ff_p, d)]
    ffn_theirs = _pair_swap_sc(ffn_blocks, "scatter_ffn_pair", 3)
    dh2 = _mm_dyw(du0, g_w1, F32, "d_h2_a")
    dh2 = _mm_dyw(dt3, g_w3, F32, "d_h2_b", init=dh2)
    with _after(dh2):
        ffn_sums = [_pair_add(ffn_blocks[0], ffn_theirs[0], "pair_add_w1")]
    ffn_sums += [_pair_add(ffn_blocks[1], ffn_theirs[1], "pair_add_w3"), _pair_add(ffn_blocks[2], ffn_theirs[2], "pair_add_w2")]
    r_w1, r_w3, r_w2 = _chip_scatter_sc(ffn_sums, "scatter_ffn_chips", 4)
    with _after(*ffn_sums):
        dx_mid, d_norm2, d_mods2 = _norm_mod_bwd(x_mid, norm2_g, mods2, dh2, dy, 0, tr, "norm_mod2_bwd")

    dm, d_gate1 = _tiled(lambda dxm, g, m_: (dxm * g, jnp.sum(dxm * m_, axis=0, keepdims=True)),
                         [_rows(dx_mid, tr), _const(gate1), _rows(mix, tr)],
                         [((t, d), BF16, (tr, d), lambda i: (i, 0)), ((1, d), F32, (1, d), lambda i: (0, 0))], (t // tr,), "d_resid1", acc=(1,))
    d_wout = _mm_xtdy(z, dm, 1, BF16, "d_w_out")
    dz = _mm_dyw(dm, g_out, F32, "d_merge")

    def merge_bwd(ga, gb, pa, pb, dzv):
        _, vjp = jax.vjp(_f_merge, ga, gb, pa, pb)
        return vjp(dzv)

    dga, dgb, dpa, dpb = _tiled(merge_bwd, [gcol(0), gcol(1), dtile(p_a), dtile(p_b), dtile(dz)],
                                [((t, d), BF16, (tr, td), lambda i, j: (i, j))] * 4, (t // tr, nd_t), "merge_bwd")
    d_wa = _mm_xtdy(y_a, dpa, N_DEV, BF16, "d_w_a")
    d_wb = _mm_xtdy(y_b, dpb, N_DEV, BF16, "d_w_b")
    mix_blocks = [d_wa, d_wb, d_wout.reshape(N_DEV, d // N_DEV, d)]
    mix_theirs = _pair_swap_sc(mix_blocks, "scatter_mix_pair", 5)
    dy_a = _mm_dyw(dpa, g_a, F32, "d_y_a")
    dy_b = _mm_dyw(dpb, g_b, BF16, "d_y_b")
    with _after(dy_b):
        mix_sums = [_pair_add(mix_blocks[0], mix_theirs[0], "pair_add_wa")]
    mix_sums += [_pair_add(mix_blocks[1], mix_theirs[1], "pair_add_wb"), _pair_add(mix_blocks[2], mix_theirs[2], "pair_add_wout")]
    r_a, r_b, r_out = _chip_scatter_sc(mix_sums, "scatter_mix_chips", 6)

    def readout_bwd(a, b, gt, g, ct):
        das, dgts, dg = [], [], 0.0
        for ah, bh, gh, ch in zip(heads(a), heads(b), heads(gt), heads(ct)):
            _, vjp = jax.vjp(_f_readout, ah, bh, gh, g)
            da_h, _, dgt_h, dg_h = vjp(ch)
            das.append(da_h)
            dgts.append(dgt_h)
            dg = dg + dg_h
        return jnp.concatenate(das, axis=1), jnp.concatenate(dgts, axis=1), dg

    with _after(*mix_sums):
        do_h, d_gate_o, d_hnorm = _tiled(
            readout_bwd, [wide(o_f, lat0), wide(o_b, lat0), seg_rows(4, lat0), _const(hgrn_norm_g), wide(dy_a)],
            [out_wide(t, F32), out_wide(t, BF16), acc_head], (t // tr,), "readout_bwd", acc=(2,))
    dq1, dfl_f, dv1, dlb_f = _hgrn_bwd(p, lb_f, do_h, saved_f, 1, False, n_ctx, None)
    dq_h, dfl_b, dv_h, dlb_b = _hgrn_bwd(p, lb_b, do_h, saved_b, 2, True, n_ctx, (dq1, dv1))

    dqn, dkall, dv_na, dbias = _na_bwd(qn, kall, p, 7, bias, dy_b, n_ctx)

    def qk_bwd(tv, g, cs, sn, ct):
        dts, dg = [], 0.0
        for th, ch in zip(heads(tv), heads(ct)):
            _, vjp = jax.vjp(lambda a, b: _f_qk(a, b, cs, sn), th, g)
            dt_h, dg_h = vjp(ch)
            dts.append(dt_h)
            dg = dg + dg_h
        return jnp.concatenate(dts, axis=1), dg

    d_pq, d_qnorm = _tiled(qk_bwd, [seg_rows(5, lat0), _const(na_q_norm_g), tab(cos, lat0), tab(sin, lat0), wide(dqn)],
                           [out_wide(t, BF16), acc_head], (t // tr,), "q_norm_rope_bwd", acc=(1,))
    d_pk, d_knorm = _tiled(qk_bwd, [seg_rows(6), _const(na_k_norm_g), tab(cos), tab(sin), wide(dkall)],
                           [out_wide(tt, BF16), acc_head], (tt // tr,), "k_norm_rope_bwd", acc=(1,))

    zc = lambda w_: jnp.zeros((n_ctx, w_), BF16)
    lat_only = lambda a: jnp.concatenate([zc(a.shape[1]), a], axis=0)
    dp = jnp.concatenate([dq_h, dfl_f, dfl_b, dv_h, lat_only(d_gate_o), lat_only(d_pq), d_pk, dv_na, lat_only(dga), lat_only(dgb)], axis=1)
    d_win = _mm_xtdy(hcat, dp, N_DEV, BF16, "d_w_in")
    win_theirs, = _pair_swap_sc([d_win], "scatter_w_in_pair", 7)
    with _after(d_win):
        dhcat = _mm_dyw(dp, g_in, BF16, "d_hcat")
    zero_ctx = jnp.concatenate([jnp.zeros((n_ctx, d), F32), dx_mid], axis=0)
    dxcat, d_norm1, d_mods1 = _norm_mod_bwd(xcat, norm1_g, mods1, dhcat, zero_ctx, n_ctx_tiles, tr, "norm_mod1_bwd")
    grad_x = dxcat[n_ctx:][None]

    zd = jnp.zeros((1, d), F32)
    dmod_l = jnp.concatenate([d_mods1[1, 0], d_mods1[1, 1], d_gate1, d_mods2[1, 0], d_mods2[1, 1], d_gate2], axis=1)
    dmod_c = jnp.concatenate([d_mods1[0, 0], d_mods1[0, 1], zd, zd, zd, zd], axis=1)
    dmods = jnp.concatenate([dmod_l, dmod_c], axis=0).reshape(2, N_DEV, ca).transpose(1, 0, 2)
    dmods = jnp.pad(dmods, ((0, 0), (0, 6), (0, 0)))
    got = _exchange([dmods], "scatter_dmod", scatter=True)[0]
    dm_rows = jnp.concatenate([got[:, 0], got[:, 1]], axis=0)
    d_ada = _mm_xtdy(act, dm_rows, 1, F32, "d_ada_w")[0]
    back = _mm_dyw(dm_rows, ada16.reshape(1, d, ca), F32, "d_silu_c")
    d_cctx_part = _tiled(lambda b, v: (jnp.sum(b[N_DEV:], axis=0, keepdims=True) * (jax.nn.sigmoid(v) * (1.0 + v * (1.0 - jax.nn.sigmoid(v)))),),
                         [_const(back), _const(c_ctx.reshape(1, d))], [((1, d), F32, (1, d), lambda *_: (0, 0))], (), "d_c_ctx")[0]

    d_rel = _bias_grad(dbias, onehot)
    d_lb_soft = jnp.concatenate([dlb_f, dlb_b], axis=0)
    d_lb0 = _tiled(lambda s, g: (g * s * (1.0 - s),), [_const(lb_soft), _const(d_lb_soft)], [((2, hw), F32, (2, hw), lambda *_: (0, 0))], (), "d_lb")[0]
    d_lb_full = jnp.stack([d_lb0, -d_lb0], axis=1)
    d_cw_l = d_cw[:3].reshape(3, N_DEV, ff_p)[:, :, :ff_l].reshape(1, 3, N_DEV * ff_l)
    d_cb_l = d_cb.reshape(N_DEV, ff_p)[:, :ff_l].reshape(1, N_DEV * ff_l)
    small = [d_cctx_part.reshape(d), (dmod_l + dmod_c), d_norm1, d_norm2, d_lb_full, d_hnorm, d_qnorm, d_knorm, d_rel, d_cw_l, d_cb_l, loss_cols]
    pk = _Pack([a.shape for a in small])
    small_slab = pk.pack(small)
    small_parts, = _gather_sc([small_slab], "gather_small", 9)

    res = {}
    with _after(small_slab):
        win_sums = _pair_add(d_win, win_theirs, "pair_add_w_in")
    r_in, = _chip_scatter_sc([win_sums], "scatter_w_in_chips", 8)
    with _after(win_sums):
        res["ada_w"] = _adam_big(d_ada[None], ada_w[0], m_ada_w[0], v_ada_w[0], "adam_ada")
    res["ffn_w1"] = _adam_big(r_w1, ffn_w1[0], m_ffn_w1[0], v_ffn_w1[0], "adam_w1")
    res["ffn_w3"] = _adam_big(r_w3, ffn_w3[0], m_ffn_w3[0], v_ffn_w3[0], "adam_w3")
    res["ffn_w2"] = _adam_big(r_w2, ffn_w2[0], m_ffn_w2[0], v_ffn_w2[0], "adam_w2")
    res["w_branch_a"] = _adam_big(r_a, w_branch_a[0], m_w_branch_a[0], v_w_branch_a[0], "adam_w_a")
    res["w_branch_b"] = _adam_big(r_b, w_branch_b[0], m_w_branch_b[0], v_w_branch_b[0], "adam_w_b")
    res["w_out"] = _adam_big(r_out, w_out[0], m_w_out[0], v_w_out[0], "adam_w_out")

    with _after(res["w_out"][1]):
        tot = _sum_parts(small_parts, "sum_small")
    g_cctx, g_ada_b, g_n1, g_n2, g_lb, g_hn, g_qn, g_kn, g_rel, g_cw, g_cb, loss_all = pk.unpack(tot)
    loss = _tiled(lambda v: (jnp.sum(v, axis=1, keepdims=True),), [_const(loss_all)], [((1, 1), F32, (1, 1), lambda *_: (0, 0))], (), "loss_total")[0][0, 0]
    g_lb = lax.dynamic_slice_in_dim(g_lb, me * HEAD_DIM, HEAD_DIM, axis=2)
    g_cw = lax.dynamic_slice_in_dim(g_cw, me * ff_l, ff_l, axis=2)
    small_names = [("c_ctx", g_cctx, c_ctx, m_c_ctx, v_c_ctx), ("ada_b", g_ada_b, ada_b, m_ada_b, v_ada_b),
                   ("norm1_g", g_n1, norm1_g, m_norm1_g, v_norm1_g), ("norm2_g", g_n2, norm2_g, m_norm2_g, v_norm2_g),
                   ("hgrn_lb_logits", g_lb, hgrn_lb_logits, m_hgrn_lb_logits, v_hgrn_lb_logits),
                   ("hgrn_norm_g", g_hn, hgrn_norm_g, m_hgrn_norm_g, v_hgrn_norm_g), ("na_q_norm_g", g_qn, na_q_norm_g, m_na_q_norm_g, v_na_q_norm_g),
                   ("na_k_norm_g", g_kn, na_k_norm_g, m_na_k_norm_g, v_na_k_norm_g), ("na_rel_bias", g_rel, na_rel_bias, m_na_rel_bias, v_na_rel_bias),
                   ("ffn_conv_w", g_cw, ffn_conv_w, m_ffn_conv_w, v_ffn_conv_w), ("ffn_conv_b", g_cb, ffn_conv_b, m_ffn_conv_b, v_ffn_conv_b)]
    pk2 = _Pack([s[1].shape for s in small_names])
    sd, sm, sv = _adam_small(*[pk2.pack([s[i] for s in small_names]) for i in (1, 2, 3, 4)])
    sd, sm, sv = pk2.unpack(sd), pk2.unpack(sm), pk2.unpack(sv)
    res.update({s[0]: (s[1], sd[i], sm[i], sv[i]) for i, s in enumerate(small_names)})
    with _after(sd[0]):
        res["w_in"] = _adam_big(r_in, w_in[0], m_w_in[0], v_w_in[0], "adam_w_in")
    for k in ("w_in", "w_branch_a", "w_branch_b", "w_out", "ffn_w1", "ffn_w3", "ffn_w2", "ada_w"):
        res[k] = tuple(a[None] for a in res[k])

    order = ["c_ctx", "ada_w", "ada_b", "norm1_g", "norm2_g", "w_in", "hgrn_lb_logits", "hgrn_norm_g", "na_q_norm_g", "na_k_norm_g",
             "na_rel_bias", "w_branch_a", "w_branch_b", "w_out", "ffn_w1", "ffn_w3", "ffn_conv_w", "ffn_conv_b", "ffn_w2"]
    shapes = {"c_ctx": c_ctx.shape, "ada_b": ada_b.shape, "norm1_g": norm1_g.shape, "norm2_g": norm2_g.shape,
              "hgrn_lb_logits": hgrn_lb_logits.shape, "hgrn_norm_g": hgrn_norm_g.shape, "na_q_norm_g": na_q_norm_g.shape,
              "na_k_norm_g": na_k_norm_g.shape, "na_rel_bias": na_rel_bias.shape, "ffn_conv_w": ffn_conv_w.shape, "ffn_conv_b": ffn_conv_b.shape}
    outs = [loss, grad_x]
    for part in range(4):
        for k in order:
            a = res[k][part]
            outs.append(a.reshape(shapes[k]) if k in shapes else a)
    return tuple(outs)
```

```python
import functools

import numpy as np
import jax
import jax.numpy as jnp
from jax import lax
from jax.experimental import pallas as pl
from jax.experimental.pallas import tpu as pltpu
from jax.experimental.pallas import tpu_sc as plsc

F32 = jnp.float32
BF16 = jnp.bfloat16
HIGHEST = lax.Precision.HIGHEST

N_DEV = 8
MESH_ID = pl.DeviceIdType.MESH
LANE = 128
HEAD_DIM = 128
N_HEADS = 8
GRID_W = 64
WIN_R = 8
WIN_C = 16
ROPE_THETA = 10000.0
EPS = 1e-6
N_MOD = 6
HGRN_BLOCK = 16
NEG_BIG = -1e30
VMEM_LIMIT = 56 << 20

ADAM_LR = 0.001
ADAM_B1 = 0.9
ADAM_B2 = 0.999
ADAM_EPS = 1e-08
ADAM_WD = 0.01
ADAM_STEP = 10

HBM_SPEC = pl.BlockSpec(memory_space=pltpu.HBM)


_ORDER_AFTER = []


class _after:
    def __init__(self, *arrs):
        self.arrs = list(arrs)

    def __enter__(self):
        _ORDER_AFTER.extend(self.arrs)

    def __exit__(self, *exc):
        del _ORDER_AFTER[:]


def _pcall(body, *, name, out_shape, grid=None, in_specs=None, out_specs=None, scratch=(), aliases=None):
    kw = {}
    if grid is not None:
        kw["grid"] = grid
    extra = []
    if _ORDER_AFTER and in_specs is not None:
        extra = list(_ORDER_AFTER)
        del _ORDER_AFTER[:]
        n_in, n_extra, inner = len(in_specs), len(extra), body
        in_specs = list(in_specs) + [pl.BlockSpec(memory_space=pl.ANY)] * n_extra

        def body(*refs):
            return inner(*refs[:n_in], *refs[n_in + n_extra:])

    if extra:
        call = _pcall_inner(body, name, out_shape, kw, in_specs, out_specs, scratch, aliases)
        return lambda *args: call(*args, *extra)
    return _pcall_inner(body, name, out_shape, kw, in_specs, out_specs, scratch, aliases)


def _pcall_inner(body, name, out_shape, kw, in_specs, out_specs, scratch, aliases):
    if in_specs is not None:
        kw["in_specs"] = in_specs
    if out_specs is not None:
        kw["out_specs"] = out_specs
    if scratch:
        kw["scratch_shapes"] = list(scratch)
    if aliases:
        kw["input_output_aliases"] = aliases
    return pl.pallas_call(body, name=name, out_shape=out_shape,
                          compiler_params=pltpu.CompilerParams(vmem_limit_bytes=VMEM_LIMIT), **kw)


def _pick(dim, cands):
    for c in cands:
        if c <= dim and dim % c == 0:
            return c
    return dim


def _sds(shape, dtype):
    return jax.ShapeDtypeStruct(tuple(shape), dtype)


def _peers():
    x, y, c = lax.axis_index("x"), lax.axis_index("y"), lax.axis_index("c")
    out = []
    for k in range(1, N_DEV):
        px = 1 - x if (k >> 2) & 1 else x
        py = 1 - y if (k >> 1) & 1 else y
        pc = 1 - c if k & 1 else c
        out.append((k, (px, py, pc), 4 * px + 2 * py + pc))
    return 4 * x + 2 * y + c, out


def _exchange(arrs, name, scatter):
    n = len(arrs)

    def body(*refs):
        ins, outs = refs[:n], refs[n:2 * n]
        send, recv, loc = refs[2 * n:]
        me, peers = _peers()
        started = []
        for i in range(n):
            src = ins[i].at[me] if scatter else ins[i]
            cp = pltpu.make_async_copy(src, outs[i].at[me], loc.at[i])
            cp.start()
            started.append(cp)
        sends = []
        for k, peer, pidx in peers:
            for i in range(n):
                src = ins[i].at[pidx] if scatter else ins[i]
                cp = pltpu.make_async_remote_copy(src_ref=src, dst_ref=outs[i].at[me], send_sem=send.at[i * 7 + k - 1],
                                                  recv_sem=recv.at[i * 7 + k - 1], device_id=peer, device_id_type=MESH_ID)
                cp.start()
                sends.append(cp)
        for k, peer, pidx in peers:
            for i in range(n):
                src = ins[i].at[pidx] if scatter else ins[i]
                pltpu.make_async_remote_copy(src_ref=src, dst_ref=outs[i].at[pidx], send_sem=send.at[i * 7 + k - 1],
                                             recv_sem=recv.at[i * 7 + k - 1], device_id=peer, device_id_type=MESH_ID).wait_recv()
        for cp in sends:
            cp.wait_send()
        for cp in started:
            cp.wait()

    out_shape = [_sds(a.shape if scatter else (N_DEV,) + a.shape, a.dtype) for a in arrs]
    res = _pcall(body, name=name, out_shape=out_shape, in_specs=[HBM_SPEC] * n, out_specs=[HBM_SPEC] * n,
                 scratch=[pltpu.SemaphoreType.DMA((7 * n,)), pltpu.SemaphoreType.DMA((7 * n,)), pltpu.SemaphoreType.DMA((n,))])(*arrs)
    return list(res)


def _exchange_sc(arrs, name, scatter, collective_id):
    n = len(arrs)
    srcs = [jax.new_ref(a, memory_space=pltpu.MemorySpace.HBM) for a in arrs]
    lands = [jax.empty_ref(_sds(a.shape if scatter else (N_DEV,) + a.shape, a.dtype), memory_space=pltpu.MemorySpace.HBM) for a in arrs]

    @pl.kernel(mesh=plsc.ScalarSubcoreMesh(axis_name="seq", num_cores=1), name=name,
               scratch_types=(pltpu.SemaphoreType.DMA((7 * n,)), pltpu.SemaphoreType.DMA((7 * n,)), pltpu.SemaphoreType.DMA((n,))),
               compiler_params=pltpu.CompilerParams(collective_id=collective_id))
    def launch(send, recv, loc):
        me, peers = _peers()
        barrier = pltpu.get_barrier_semaphore()
        for _, peer, _ in peers:
            pl.semaphore_signal(barrier, inc=1, device_id=peer, device_id_type=MESH_ID)
        pl.semaphore_wait(barrier, N_DEV - 1)
        own = [pltpu.make_async_copy(srcs[i].at[me] if scatter else srcs[i], lands[i].at[me], loc.at[i]) for i in range(n)]
        for cp in own:
            cp.start()
        sends = []
        for k, peer, pidx in peers:
            for i in range(n):
                src = srcs[i].at[pidx] if scatter else srcs[i]
                cp = pltpu.make_async_remote_copy(src_ref=src, dst_ref=lands[i].at[me], send_sem=send.at[i * 7 + k - 1],
                                                  recv_sem=recv.at[i * 7 + k - 1], device_id=peer, device_id_type=MESH_ID)
                cp.start()
                sends.append(cp)
        for k, peer, pidx in peers:
            for i in range(n):
                src = srcs[i].at[pidx] if scatter else srcs[i]
                pltpu.make_async_remote_copy(src_ref=src, dst_ref=lands[i].at[pidx], send_sem=send.at[i * 7 + k - 1],
                                             recv_sem=recv.at[i * 7 + k - 1], device_id=peer, device_id_type=MESH_ID).wait_recv()
        for cp in sends:
            cp.wait_send()
        for cp in own:
            cp.wait()

    launch()
    return [r[...] for r in lands]


def _gather_sc(arrs, name, collective_id):
    n = len(arrs)
    srcs = [jax.new_ref(a, memory_space=pltpu.MemorySpace.HBM) for a in arrs]
    lands = [jax.empty_ref(_sds((N_DEV,) + a.shape, a.dtype), memory_space=pltpu.MemorySpace.HBM) for a in arrs]

    @pl.kernel(mesh=plsc.ScalarSubcoreMesh(axis_name="seq", num_cores=1), name=name,
               scratch_types=(pltpu.SemaphoreType.DMA((7 * n,)), pltpu.SemaphoreType.DMA((7 * n,)), pltpu.SemaphoreType.DMA((n,))),
               compiler_params=pltpu.CompilerParams(collective_id=collective_id))
    def launch(send, recv, loc):
        x, y, c = lax.axis_index("x"), lax.axis_index("y"), lax.axis_index("c")
        me, sibling = (x, y, c), (x, y, 1 - c)
        chips = [(1 - x, y), (x, 1 - y), (1 - x, 1 - y)]
        index = lambda px, py, pc: 4 * px + 2 * py + pc
        barrier = pltpu.get_barrier_semaphore()
        for peer in [sibling] + [(*chip, c) for chip in chips]:
            pl.semaphore_signal(barrier, inc=1, device_id=peer, device_id_type=MESH_ID)
        pl.semaphore_wait(barrier, 4)

        def copy(i, k, block, to, from_src):
            return pltpu.make_async_remote_copy(src_ref=srcs[i] if from_src else lands[i].at[index(*block)], dst_ref=lands[i].at[index(*block)],
                                                send_sem=send.at[i * 7 + k], recv_sem=recv.at[i * 7 + k], device_id=to, device_id_type=MESH_ID)

        own = [pltpu.make_async_copy(srcs[i], lands[i].at[index(*me)], loc.at[i]) for i in range(n)]
        for cp in own:
            cp.start()
        started = []
        for i in range(n):
            started.append(copy(i, 0, me, sibling, True))
            started += [copy(i, 1 + j, me, (*chip, c), True) for j, chip in enumerate(chips)]
        for cp in started:
            cp.start()
        for j, chip in enumerate(chips):
            for i in range(n):
                copy(i, 1 + j, (*chip, c), me, False).wait_recv()
                fwd = copy(i, 4 + j, (*chip, c), sibling, False)
                fwd.start()
                started.append(fwd)
        for i in range(n):
            copy(i, 0, sibling, me, False).wait_recv()
            for j, chip in enumerate(chips):
                copy(i, 4 + j, (*chip, 1 - c), me, False).wait_recv()
        for cp in started:
            cp.wait_send()
        for cp in own:
            cp.wait()

    launch()
    return [r[...] for r in lands]


def _pair_swap_sc(blocks, name, collective_id):
    n = len(blocks)
    srcs = [jax.new_ref(b, memory_space=pltpu.MemorySpace.HBM) for b in blocks]
    lands = [jax.empty_ref(_sds((4,) + b.shape[1:], b.dtype), memory_space=pltpu.MemorySpace.HBM) for b in blocks]

    @pl.kernel(mesh=plsc.ScalarSubcoreMesh(axis_name="seq", num_cores=1), name=name,
               scratch_types=(pltpu.SemaphoreType.DMA((4 * n,)), pltpu.SemaphoreType.DMA((4 * n,))),
               compiler_params=pltpu.CompilerParams(collective_id=collective_id))
    def launch(send, recv):
        x, y, c = lax.axis_index("x"), lax.axis_index("y"), lax.axis_index("c")
        sibling = (x, y, 1 - c)
        barrier = pltpu.get_barrier_semaphore()
        pl.semaphore_signal(barrier, inc=1, device_id=sibling, device_id_type=MESH_ID)
        pl.semaphore_wait(barrier, 1)
        copies = [pltpu.make_async_remote_copy(src_ref=srcs[i].at[2 * q + 1 - c], dst_ref=lands[i].at[q], send_sem=send.at[4 * i + q],
                                               recv_sem=recv.at[4 * i + q], device_id=sibling, device_id_type=MESH_ID)
                  for i in range(n) for q in range(4)]
        for cp in copies:
            cp.start()
        for cp in copies:
            cp.wait()

    launch()
    return [r[...] for r in lands]


def _pair_add(blocks, theirs, name):
    _, r, c_ = blocks.shape
    mine = lax.dynamic_index_in_dim(blocks.reshape(4, 2, r, c_), lax.axis_index("c"), axis=1, keepdims=False)
    tr = _pick(r, (256, 128, 64, 16))
    spec = (None, tr, c_)
    return _tiled(lambda a, b: (a.astype(F32) + b.astype(F32),), [(mine, spec, lambda q, i: (q, i, 0)), (theirs, spec, lambda q, i: (q, i, 0))],
                  [((4, r, c_), BF16, spec, lambda q, i: (q, i, 0))], (4, r // tr), name)[0]


def _chip_scatter_sc(sums, name, collective_id):
    n = len(sums)
    srcs = [jax.new_ref(s, memory_space=pltpu.MemorySpace.HBM) for s in sums]
    lands = [jax.empty_ref(_sds(s.shape, s.dtype), memory_space=pltpu.MemorySpace.HBM) for s in sums]

    @pl.kernel(mesh=plsc.ScalarSubcoreMesh(axis_name="seq", num_cores=1), name=name,
               scratch_types=(pltpu.SemaphoreType.DMA((3 * n,)), pltpu.SemaphoreType.DMA((3 * n,)), pltpu.SemaphoreType.DMA((n,))),
               compiler_params=pltpu.CompilerParams(collective_id=collective_id))
    def launch(send, recv, loc):
        x, y, c = lax.axis_index("x"), lax.axis_index("y"), lax.axis_index("c")
        chips = [(1 - x, y), (x, 1 - y), (1 - x, 1 - y)]
        my_chip = 2 * x + y
        barrier = pltpu.get_barrier_semaphore()
        for chip in chips:
            pl.semaphore_signal(barrier, inc=1, device_id=(*chip, c), device_id_type=MESH_ID)
        pl.semaphore_wait(barrier, 3)
        own = [pltpu.make_async_copy(srcs[i].at[my_chip], lands[i].at[my_chip], loc.at[i]) for i in range(n)]
        for cp in own:
            cp.start()
        sends = [pltpu.make_async_remote_copy(src_ref=srcs[i].at[2 * px + py], dst_ref=lands[i].at[my_chip], send_sem=send.at[3 * i + j],
                                              recv_sem=recv.at[3 * i + j], device_id=(px, py, c), device_id_type=MESH_ID)
                 for j, (px, py) in enumerate(chips) for i in range(n)]
        for cp in sends:
            cp.start()
        for j, (px, py) in enumerate(chips):
            for i in range(n):
                pltpu.make_async_remote_copy(src_ref=srcs[i].at[my_chip], dst_ref=lands[i].at[2 * px + py], send_sem=send.at[3 * i + j],
                                             recv_sem=recv.at[3 * i + j], device_id=(px, py, c), device_id_type=MESH_ID).wait_recv()
        for cp in sends:
            cp.wait_send()
        for cp in own:
            cp.wait()

    launch()
    return [r[...] for r in lands]


class _InFlight:
    def __init__(self, send, recv, srcs, lands, token, scatter):
        self.send, self.recv, self.srcs, self.lands, self.token, self.scatter = send, recv, srcs, lands, token, scatter


SEM_SPEC = pl.BlockSpec(memory_space=pltpu.SEMAPHORE)
SIDE_EFFECT = pltpu.SideEffectType.DATAFLOW_SIDE_EFFECTING


def _exchange_start(arrs, name, scatter):
    n = len(arrs)

    def body(*refs):
        ins, lands = refs[:n], refs[n:2 * n]
        send, recv = refs[2 * n], refs[2 * n + 1]
        token = refs[4 * n + 2]
        loc = refs[4 * n + 3]
        me, peers = _peers()
        own = [pltpu.make_async_copy(ins[i].at[me] if scatter else ins[i], lands[i].at[me], loc.at[i]) for i in range(n)]
        for cp in own:
            cp.start()
        for cp in own:
            cp.wait()
        for k, peer, pidx in peers:
            for i in range(n):
                src = ins[i].at[pidx] if scatter else ins[i]
                pltpu.make_async_remote_copy(src_ref=src, dst_ref=lands[i].at[me], send_sem=send.at[i * 7 + k - 1],
                                             recv_sem=recv.at[i * 7 + k - 1], device_id=peer, device_id_type=MESH_ID).start()
        token[...] = jnp.zeros_like(token)

    land_shapes = [a.shape if scatter else (N_DEV,) + a.shape for a in arrs]
    hbm = lambda a: pltpu.with_memory_space_constraint(a, pltpu.HBM)
    args = [hbm(a) for a in arrs] + [hbm(lax.empty(s, a.dtype)) for s, a in zip(land_shapes, arrs)]
    out_shape = ([pltpu.SemaphoreType.DMA((7 * n,)), pltpu.SemaphoreType.DMA((7 * n,))]
                 + [pltpu.HBM(a.shape, a.dtype) for a in arrs] + [pltpu.HBM(s, a.dtype) for s, a in zip(land_shapes, arrs)]
                 + [_sds((8, LANE), F32)])
    res = pl.pallas_call(
        body, name=name, out_shape=out_shape, in_specs=[HBM_SPEC] * (2 * n),
        out_specs=[SEM_SPEC, SEM_SPEC] + [HBM_SPEC] * (2 * n) + [pl.BlockSpec(memory_space=pltpu.VMEM)],
        input_output_aliases={i: 2 + i for i in range(2 * n)},
        scratch_shapes=[pltpu.SemaphoreType.DMA((n,))],
        compiler_params=pltpu.CompilerParams(has_side_effects=SIDE_EFFECT))(*args)
    return _InFlight(res[0], res[1], list(res[2:2 + n]), list(res[2 + n:2 + 2 * n]), res[2 + 2 * n], scatter)


def _exchange_wait(h, after, name):
    n = len(h.srcs)
    scatter = h.scatter
    after = list(after)

    def body(*refs):
        ins, lands = refs[:n], refs[n:2 * n]
        send, recv = refs[2 * n], refs[2 * n + 1]
        _, peers = _peers()
        for k, peer, pidx in peers:
            for i in range(n):
                src = ins[i].at[pidx] if scatter else ins[i]
                cp = pltpu.make_async_remote_copy(src_ref=src, dst_ref=lands[i].at[pidx], send_sem=send.at[i * 7 + k - 1],
                                                  recv_sem=recv.at[i * 7 + k - 1], device_id=peer, device_id_type=MESH_ID)
                cp.wait_send()
                cp.wait_recv()

    res = pl.pallas_call(
        body, name=name, out_shape=[pltpu.HBM(a.shape, a.dtype) for a in h.srcs + h.lands],
        in_specs=[HBM_SPEC] * (2 * n) + [SEM_SPEC, SEM_SPEC] + [pl.BlockSpec(memory_space=pl.ANY)] * len(after),
        out_specs=[HBM_SPEC] * (2 * n), input_output_aliases={i: i for i in range(2 * n)},
        compiler_params=pltpu.CompilerParams(has_side_effects=SIDE_EFFECT))(*h.srcs, *h.lands, h.send, h.recv, *after)
    return list(res[n:])


def _mm_xw(x, g, out_dtype, name, tm_c=(768, 512, 384, 256, 128, 64, 16), tn_c=(1024, 768, 512, 256, 128), tk_c=(2048, 1024, 768, 512, 256)):
    m, r = x.shape
    nb, r2, cl = g.shape
    assert r == r2
    tm, tn, tk = _pick(m, tm_c), _pick(cl, tn_c), _pick(r, tk_c)
    q, nk = cl // tn, r // tk

    def body(x_ref, g_ref, o_ref, *acc):
        p = lax.dot_general(x_ref[...].astype(BF16), g_ref[...], (((1,), (0,)), ((), ())), preferred_element_type=F32)
        if nk == 1:
            o_ref[...] = p.astype(o_ref.dtype)
        else:
            k = pl.program_id(2)

            @pl.when(k == 0)
            def _():
                acc[0][...] = p

            @pl.when(k > 0)
            def _():
                acc[0][...] += p

            @pl.when(k == nk - 1)
            def _():
                o_ref[...] = acc[0][...].astype(o_ref.dtype)

    return _pcall(
        body, name=name, grid=(m // tm, nb * q, nk),
        in_specs=[pl.BlockSpec((tm, tk), lambda i, j, k: (i, k)), pl.BlockSpec((None, tk, tn), lambda i, j, k: (j // q, k, j % q))],
        out_specs=pl.BlockSpec((tm, tn), lambda i, j, k: (i, j)),
        out_shape=_sds((m, nb * cl), out_dtype),
        scratch=[] if nk == 1 else [pltpu.VMEM((tm, tn), F32)])(x, g)


def _mm_dyw(dy, g, out_dtype, name, init=None, tm_c=(1024, 768, 512, 384, 256, 128), tn_c=(1024, 512, 256, 128), tk_c=(2048, 1536, 1024, 768, 512, 256, 128)):
    m, n = dy.shape
    nb, r, cl = g.shape
    assert n == nb * cl
    tm, tn, tk = _pick(m, tm_c), _pick(r, tn_c), _pick(cl, tk_c)
    q = cl // tk
    nk = nb * q
    has_init = init is not None

    def body(*refs):
        if has_init:
            dy_ref, g_ref, i_ref, o_ref, acc = refs
        else:
            dy_ref, g_ref, o_ref, acc = refs
        k = pl.program_id(2)
        p = lax.dot_general(dy_ref[...].astype(BF16), g_ref[...], (((1,), (1,)), ((), ())), preferred_element_type=F32)

        @pl.when(k == 0)
        def _():
            acc[...] = p + i_ref[...].astype(F32) if has_init else p

        @pl.when(k > 0)
        def _():
            acc[...] += p

        @pl.when(k == nk - 1)
        def _():
            o_ref[...] = acc[...].astype(o_ref.dtype)

    in_specs = [pl.BlockSpec((tm, tk), lambda i, j, k: (i, k)), pl.BlockSpec((None, tn, tk), lambda i, j, k: (k // q, j, k % q))]
    args = [dy, g]
    if has_init:
        in_specs.append(pl.BlockSpec((tm, tn), lambda i, j, k: (i, j)))
        args.append(init)
    return _pcall(body, name=name, grid=(m // tm, r // tn, nk), in_specs=in_specs,
                  out_specs=pl.BlockSpec((tm, tn), lambda i, j, k: (i, j)), out_shape=_sds((m, r), out_dtype),
                  scratch=[pltpu.VMEM((tm, tn), F32)])(*args)


def _mm_xtdy(x, dy, nb, out_dtype, name, tm_c=(1024, 512, 256, 128), tn_c=(768, 512, 256, 128), tk_c=(2304, 2048, 1152, 1024, 768, 512, 256, 128, 16)):
    t, r = x.shape
    t2, n = dy.shape
    assert t == t2 and n % nb == 0
    cl = n // nb
    tm, tn, tk = _pick(r, tm_c), _pick(cl, tn_c), _pick(t, tk_c)
    q, nk = cl // tn, t // tk

    def body(x_ref, dy_ref, o_ref, *acc):
        p = lax.dot_general(x_ref[...].astype(BF16), dy_ref[...].astype(BF16), (((0,), (0,)), ((), ())), preferred_element_type=F32)
        if nk == 1:
            o_ref[...] = p.astype(o_ref.dtype)
        else:
            k = pl.program_id(2)

            @pl.when(k == 0)
            def _():
                acc[0][...] = p

            @pl.when(k > 0)
            def _():
                acc[0][...] += p

            @pl.when(k == nk - 1)
            def _():
                o_ref[...] = acc[0][...].astype(o_ref.dtype)

    return _pcall(
        body, name=name, grid=(r // tm, nb * q, nk),
        in_specs=[pl.BlockSpec((tk, tm), lambda i, j, k: (k, i)), pl.BlockSpec((tk, tn), lambda i, j, k: (k, j))],
        out_specs=pl.BlockSpec((None, tm, tn), lambda i, j, k: (j // q, i, j % q)),
        out_shape=_sds((nb, r, cl), out_dtype),
        scratch=[] if nk == 1 else [pltpu.VMEM((tm, tn), F32)])(x, dy)


def _mm_f32(a, b, name, trans_b=False):
    dims = (((1,), (1,)), ((), ())) if trans_b else (((1,), (0,)), ((), ()))
    n = b.shape[0] if trans_b else b.shape[1]

    def body(a_ref, b_ref, o_ref):
        o_ref[...] = lax.dot_general(a_ref[...], b_ref[...], dims, precision=HIGHEST, preferred_element_type=F32)

    return _pcall(body, name=name, out_shape=_sds((a.shape[0], n), F32))(a, b)


def _tiled(fn, ins, outs, grid, name, acc=()):
    n_in = len(ins)
    grid = tuple(grid) or (1,)
    nd = len(grid)

    def body(*refs):
        vals = fn(*[r[...] for r in refs[:n_in]])
        if not isinstance(vals, (tuple, list)):
            vals = (vals,)
        first = None
        for o, (ref, v) in enumerate(zip(refs[n_in:], vals)):
            if o in acc:
                if first is None:
                    first = pl.program_id(0) == 0
                    for a in range(1, nd):
                        first = jnp.logical_and(first, pl.program_id(a) == 0)

                @pl.when(first)
                def _(ref=ref, v=v):
                    ref[...] = v.astype(ref.dtype)

                @pl.when(jnp.logical_not(first))
                def _(ref=ref, v=v):
                    ref[...] += v.astype(ref.dtype)
            else:
                ref[...] = v.astype(ref.dtype)

    res = _pcall(body, name=name, grid=grid,
                 in_specs=[pl.BlockSpec(b, im) for _, b, im in ins],
                 out_specs=[pl.BlockSpec(b, im) for _, _, b, im in outs],
                 out_shape=[_sds(s, d) for s, d, _, _ in outs])(*[a for a, _, _ in ins])
    return list(res)


def _rows(a, tr):
    return (a, (tr, a.shape[1]), lambda i, *_: (i, 0))


def _const(a):
    nd = a.ndim
    return (a, a.shape, lambda *_: (0,) * nd)


def _cast_bf16(w, name):
    r, c = w.shape
    tr = _pick(r, (256, 128, 64, 16))
    return _tiled(lambda v: v, [_rows(w, tr)], [((r, c), BF16, (tr, c), lambda i: (i, 0))], (r // tr,), name)[0]


def _f_norm_mod(x, g, sh, sc):
    y = x * lax.rsqrt(jnp.mean(x * x, axis=-1, keepdims=True) + EPS) * g
    return y * (1.0 + sc) + sh


def _rope_partner_impl(y):
    nf = HEAD_DIM // 4
    lane = lax.broadcasted_iota(jnp.int32, y.shape, 1)
    return jnp.where(lane % (2 * nf) < nf, pltpu.roll(y, HEAD_DIM - nf, 1), pltpu.roll(y, nf, 1))


_rope_partner = jax.custom_vjp(_rope_partner_impl)
_rope_partner.defvjp(lambda y: (_rope_partner_impl(y), None), lambda _, ct: (_rope_partner_impl(ct),))


def _f_qk(t, g, cos, sin):
    y = t * lax.rsqrt(jnp.mean(t * t, axis=-1, keepdims=True) + EPS) * g
    return y * cos + _rope_partner(y) * sin


def _f_readout(of, ob, gate, g):
    o = of + ob
    on = o * lax.rsqrt(jnp.mean(o * o, axis=-1, keepdims=True) + EPS) * g
    return on * (gate * jax.nn.sigmoid(gate))


def _f_merge(ga, gb, pa, pb):
    return jax.nn.sigmoid(ga) * pa + jax.nn.sigmoid(gb) * pb


def _f_resid(x, gate, m):
    return x + gate * m


def _norm_mod_fwd(xcat, g, mods, n_ctx_tiles, tr):
    tt, d = xcat.shape
    which = lambda i: (jnp.where(i >= n_ctx_tiles, 1, 0), 0, 0, 0)

    def fn(x, gg, md):
        return _f_norm_mod(x, gg, md[0], md[1])

    return _tiled(fn, [_rows(xcat, tr), _const(g), (mods, (None, 2, 1, d), which)],
                  [((tt, d), BF16, (tr, d), lambda i: (i, 0))], (tt // tr,), "norm_mod_fwd")[0]


def _norm_mod_bwd(xcat, g, mods, dh, extra, n_ctx_tiles, tr, name):
    tt, d = xcat.shape
    nt = tt // tr
    has_extra = extra is not None

    def body(*refs):
        if has_extra:
            x_ref, g_ref, m_ref, dh_ref, e_ref, dx_ref, dg_ref, dm_ref = refs
        else:
            x_ref, g_ref, m_ref, dh_ref, dx_ref, dg_ref, dm_ref = refs
        i = pl.program_id(0)
        md = m_ref[...]
        _, vjp = jax.vjp(_f_norm_mod, x_ref[...], g_ref[...], md[0], md[1])
        dx, dg, dsh, dsc = vjp(dh_ref[...].astype(F32))
        dx_ref[...] = dx + e_ref[...] if has_extra else dx

        @pl.when(i == 0)
        def _():
            dg_ref[...] = dg

        @pl.when(i > 0)
        def _():
            dg_ref[...] += dg

        fresh = jnp.logical_or(i == 0, i == n_ctx_tiles)

        @pl.when(fresh)
        def _():
            dm_ref[0] = dsh
            dm_ref[1] = dsc

        @pl.when(jnp.logical_not(fresh))
        def _():
            dm_ref[0] += dsh
            dm_ref[1] += dsc

    which = lambda i: (jnp.where(i >= n_ctx_tiles, 1, 0), 0, 0, 0)
    row = pl.BlockSpec((tr, d), lambda i: (i, 0))
    in_specs = [row, pl.BlockSpec((1, d), lambda i: (0, 0)), pl.BlockSpec((None, 2, 1, d), which), row]
    args = [xcat, g, mods, dh]
    if has_extra:
        in_specs.append(row)
        args.append(extra)
    return _pcall(body, name=name, grid=(nt,), in_specs=in_specs,
                  out_specs=[row, pl.BlockSpec((1, d), lambda i: (0, 0)), pl.BlockSpec((None, 2, 1, d), which)],
                  out_shape=[_sds((tt, d), F32), _sds((1, d), F32), _sds((2, 2, 1, d), F32)])(*args)


def _hgrn_tri(reverse):
    t = np.arange(HGRN_BLOCK)
    tri = (t[None, :] >= t[:, None]) if reverse else (t[None, :] <= t[:, None])
    tri = tri.astype(np.float32)
    return jnp.asarray(tri), jnp.asarray(tri.T.copy())


def _hgrn_rowblock(n, n_ctx_blocks, n_blocks, reverse):
    if not reverse:
        return n
    return jnp.where(n < n_ctx_blocks, n_ctx_blocks - 1 - n, n_blocks - 1 - n + n_ctx_blocks)


def _hgrn_gates(fl, lb):
    sg = jax.nn.sigmoid(fl)
    f = lb + (1.0 - lb) * sg
    return sg, f, jnp.log(f), 1.0 - f


def _hgrn_intra_mask(reverse):
    tio = lax.broadcasted_iota(jnp.int32, (HGRN_BLOCK, HEAD_DIM), 0)
    return (lambda s: tio <= s) if reverse else (lambda s: tio >= s)


class _Halves:
    def __init__(self, reverse):
        self.reverse = reverse
        self.h = HGRN_BLOCK // 2
        tio = lax.broadcasted_iota(jnp.int32, (self.h, HEAD_DIM), 0)
        self.tio = tio if reverse else tio + self.h

    def is_half(self, s):
        return s < self.h if self.reverse else s >= self.h

    def rows(self, a):
        return a[:self.h] if self.reverse else a[self.h:]

    def mask(self, s):
        return self.tio <= s if self.reverse else self.tio >= s

    def widen(self, full, half):
        z = jnp.zeros_like(half)
        return full + jnp.concatenate([half, z] if self.reverse else [z, half], axis=0)


def _unrolled(nblk, u, fn, init):
    assert nblk % u == 0

    def trip(b, c):
        for j in range(u):
            c = fn(b * u + j, j, c)
        return c

    return lax.fori_loop(0, nblk // u, trip, init)


HGRN_UNROLL_FWD = (4, 8)
HGRN_UNROLL_BWD = (8, 4)


def _hgrn_fwd(p, lb, seg_f, reverse, n_ctx_rows):
    tt = p.shape[0]
    hb = HGRN_BLOCK
    nblk, nctx = tt // hb, n_ctx_rows // hb
    tri, _ = _hgrn_tri(reverse)
    u1, u3 = HGRN_UNROLL_FWD

    def body(q_ref, f_ref, v_ref, lb_ref, tri_ref, o_ref, st_all, dec_all, cum_out, qe_buf, cum_blk, k_blk, v_blk):
        mask = _hgrn_intra_mask(reverse)
        hv = _Halves(reverse)
        lbv = lb_ref[...]

        def phase1(n, slot, c):
            r0 = pl.multiple_of(_hgrn_rowblock(n, nctx, nblk, reverse) * hb, hb)
            q, v = q_ref[pl.ds(r0, hb), :], v_ref[pl.ds(r0, hb), :]
            _, f, g, k = _hgrn_gates(f_ref[pl.ds(r0, hb), :], lbv)
            cum = jnp.dot(tri_ref[...], g, precision=HIGHEST, preferred_element_type=F32)
            tot = jnp.sum(g, axis=0, keepdims=True)
            cum_blk[slot] = cum
            k_blk[slot] = k
            v_blk[slot] = v
            oi = jnp.zeros((hb, HEAD_DIM), F32)
            oi_h = jnp.zeros((hb // 2, HEAD_DIM), F32)
            q_h, cum_h = hv.rows(q), hv.rows(cum)
            for s in range(hb):
                if hv.is_half(s):
                    e = jnp.where(hv.mask(s), jnp.exp(jnp.minimum(cum_h - cum_blk[slot, s:s + 1, :], 0.0)), 0.0)
                    a_s = jnp.sum(q_h * e * k_blk[slot, s:s + 1, :], axis=-1, keepdims=True)
                    oi_h = oi_h + a_s * v_blk[slot, s:s + 1, :]
                else:
                    e = jnp.where(mask(s), jnp.exp(jnp.minimum(cum - cum_blk[slot, s:s + 1, :], 0.0)), 0.0)
                    a_s = jnp.sum(q * e * k_blk[slot, s:s + 1, :], axis=-1, keepdims=True)
                    oi = oi + a_s * v_blk[slot, s:s + 1, :]
            o_ref[pl.ds(r0, hb), :] = hv.widen(oi, oi_h)
            cum_out[pl.ds(r0, hb), :] = cum
            qe_buf[pl.ds(r0, hb), :] = q * jnp.exp(cum)
            kl = k * jnp.exp(tot - cum)
            st_all[n] = lax.dot_general(v.astype(BF16), kl.astype(BF16), (((0,), (0,)), ((), ())), preferred_element_type=F32)
            dec_all[pl.ds(n, 1), :] = jnp.exp(tot)
            return c

        _unrolled(nblk, u1, phase1, 0)

        def phase2(n, st):
            kv = st_all[n]
            st_all[n] = st
            return st * dec_all[pl.ds(n, 1), :] + kv

        lax.fori_loop(0, nblk, phase2, jnp.zeros((HEAD_DIM, HEAD_DIM), F32))

        def phase3(n, slot, c):
            r0 = pl.multiple_of(_hgrn_rowblock(n, nctx, nblk, reverse) * hb, hb)
            o_ref[pl.ds(r0, hb), :] += lax.dot_general(qe_buf[pl.ds(r0, hb), :].astype(BF16), st_all[n].astype(BF16),
                                                        (((1,), (1,)), ((), ())), preferred_element_type=F32)
            return c

        _unrolled(nblk, u3, phase3, 0)

    col = lambda seg: pl.BlockSpec((tt, HEAD_DIM), lambda h, seg=seg: (0, seg * N_HEADS + h))
    blk = pltpu.VMEM((u1, hb, HEAD_DIM), F32)
    head = pl.BlockSpec((tt, HEAD_DIM), lambda h: (0, h))
    return _pcall(
        body, name="hgrn_fwd_rev" if reverse else "hgrn_fwd", grid=(N_HEADS,),
        in_specs=[col(0), col(seg_f), col(3), pl.BlockSpec((1, HEAD_DIM), lambda h: (0, h)), pl.BlockSpec((hb, hb), lambda h: (0, 0))],
        out_specs=[head, pl.BlockSpec((None, nblk, HEAD_DIM, HEAD_DIM), lambda h: (h, 0, 0, 0)),
                   pl.BlockSpec((None, nblk, HEAD_DIM), lambda h: (h, 0, 0)), head],
        out_shape=[_sds((tt, N_HEADS * HEAD_DIM), F32), _sds((N_HEADS, nblk, HEAD_DIM, HEAD_DIM), F32),
                   _sds((N_HEADS, nblk, HEAD_DIM), F32), _sds((tt, N_HEADS * HEAD_DIM), F32)],
        scratch=[pltpu.VMEM((tt, HEAD_DIM), F32), blk, blk, blk])(p, p, p, lb, tri)


def _hgrn_bwd(p, lb, do, saved, seg_f, reverse, n_ctx_rows, prev):
    tt = p.shape[0]
    hb = HGRN_BLOCK
    nblk, nctx = tt // hb, n_ctx_rows // hb
    _, tri_t = _hgrn_tri(reverse)
    last_row = 0 if reverse else hb - 1
    has_prev = prev is not None
    u1, u3 = HGRN_UNROLL_BWD

    def body(*refs):
        q_ref, f_ref, v_ref, lb_ref, trit_ref, do_ref, st_all, dec_all, cum_buf = refs[:9]
        refs = refs[9:]
        if has_prev:
            pq_ref, pv_ref = refs[:2]
            refs = refs[2:]
        dq_ref, dfl_ref, dv_ref, dlb_ref, dd_all, cum_blk, k_blk, v_blk, dk_blk, dv_blk = refs
        mask = _hgrn_intra_mask(reverse)
        hv = _Halves(reverse)
        lbv = lb_ref[...]
        tio = lax.broadcasted_iota(jnp.int32, (hb, HEAD_DIM), 0)

        def rows_of(n):
            rb = _hgrn_rowblock(n, nctx, nblk, reverse)
            return rb, pl.multiple_of(rb * hb, hb)

        def load_do(rb):
            lat0 = pl.multiple_of(jnp.maximum(rb - nctx, 0) * hb, hb)
            return jnp.where(rb >= nctx, do_ref[pl.ds(lat0, hb), :], 0.0)

        def phase1(n, slot, c):
            rb, r0 = rows_of(n)
            qe = q_ref[pl.ds(r0, hb), :] * jnp.exp(cum_buf[pl.ds(r0, hb), :])
            dd_all[n] = lax.dot_general(load_do(rb).astype(BF16), qe.astype(BF16), (((0,), (0,)), ((), ())), preferred_element_type=F32)
            return c

        _unrolled(nblk, u1, phase1, 0)

        def phase2r(i, dst):
            n = nblk - 1 - i
            u = dd_all[n]
            dd_all[n] = dst
            return u + dst * dec_all[pl.ds(n, 1), :]

        lax.fori_loop(0, nblk, phase2r, jnp.zeros((HEAD_DIM, HEAD_DIM), F32))

        def phase3(n, slot, dlb):
            rb, r0 = rows_of(n)
            q, v = q_ref[pl.ds(r0, hb), :], v_ref[pl.ds(r0, hb), :]
            sg, f, g, k = _hgrn_gates(f_ref[pl.ds(r0, hb), :], lbv)
            cum = cum_buf[pl.ds(r0, hb), :]
            tot = jnp.sum(g, axis=0, keepdims=True)
            dob = load_do(rb)
            st, dst = st_all[n], dd_all[n]
            e_cum = jnp.exp(cum)
            e_rest = jnp.exp(tot - cum)
            dq = jnp.dot(dob.astype(BF16), st.astype(BF16), preferred_element_type=F32) * e_cum
            dk_inter = jnp.dot(v.astype(BF16), dst.astype(BF16), preferred_element_type=F32) * e_rest
            dv = lax.dot_general((k * e_rest).astype(BF16), dst.astype(BF16), (((1,), (1,)), ((), ())), preferred_element_type=F32)
            cum_blk[slot] = cum
            k_blk[slot] = k
            v_blk[slot] = v
            dq_h = jnp.zeros((hb // 2, HEAD_DIM), F32)
            q_h, cum_h, dob_h = hv.rows(q), hv.rows(cum), hv.rows(dob)
            for s in range(hb):
                half = hv.is_half(s)
                qs, cs, ds = (q_h, cum_h, dob_h) if half else (q, cum, dob)
                e = jnp.where(hv.mask(s) if half else mask(s), jnp.exp(jnp.minimum(cs - cum_blk[slot, s:s + 1, :], 0.0)), 0.0)
                a_s = jnp.sum(qs * e * k_blk[slot, s:s + 1, :], axis=-1, keepdims=True)
                da_s = jnp.sum(ds * v_blk[slot, s:s + 1, :], axis=-1, keepdims=True)
                gs = da_s * e
                if half:
                    dq_h = dq_h + gs * k_blk[slot, s:s + 1, :]
                else:
                    dq = dq + gs * k_blk[slot, s:s + 1, :]
                dk_blk[slot, s:s + 1, :] = jnp.sum(gs * qs, axis=0, keepdims=True)
                dv_blk[slot, s:s + 1, :] = jnp.sum(a_s * ds, axis=0, keepdims=True)
            dq = hv.widen(dq, dq_h)
            dk = dk_inter + dk_blk[slot]
            dv = dv + dv_blk[slot]
            d_tot = jnp.sum(k * dk_inter, axis=0, keepdims=True) + jnp.exp(tot) * jnp.sum(dst * st, axis=0, keepdims=True)
            dcum = q * dq - k * dk + jnp.where(tio == last_row, d_tot, 0.0)
            dg = jnp.dot(trit_ref[...], dcum, precision=HIGHEST, preferred_element_type=F32)
            df = dg / f - dk
            if has_prev:
                dq = dq + pq_ref[pl.ds(r0, hb), :].astype(F32)
                dv = dv + pv_ref[pl.ds(r0, hb), :].astype(F32)
            dq_ref[pl.ds(r0, hb), :] = dq.astype(dq_ref.dtype)
            dv_ref[pl.ds(r0, hb), :] = dv.astype(dv_ref.dtype)
            dfl_ref[pl.ds(r0, hb), :] = (df * (1.0 - lbv) * sg * (1.0 - sg)).astype(dfl_ref.dtype)
            return dlb + jnp.sum(df * (1.0 - sg), axis=0, keepdims=True)

        dlb_ref[...] = _unrolled(nblk, u3, phase3, jnp.zeros((1, HEAD_DIM), F32))

    col = lambda seg: pl.BlockSpec((tt, HEAD_DIM), lambda h, seg=seg: (0, seg * N_HEADS + h))
    head = pl.BlockSpec((tt, HEAD_DIM), lambda h: (0, h))
    lbs = pl.BlockSpec((1, HEAD_DIM), lambda h: (0, h))
    tris = pl.BlockSpec((hb, hb), lambda h: (0, 0))
    in_specs = [col(0), col(seg_f), col(3), lbs, tris, pl.BlockSpec((do.shape[0], HEAD_DIM), lambda h: (0, h)),
                pl.BlockSpec((None, nblk, HEAD_DIM, HEAD_DIM), lambda h: (h, 0, 0, 0)), pl.BlockSpec((None, nblk, HEAD_DIM), lambda h: (h, 0, 0)), head]
    args = [p, p, p, lb, tri_t, do, *saved]
    mid = F32 if not has_prev else BF16
    if has_prev:
        in_specs += [head, head]
        args += list(prev)
    w = N_HEADS * HEAD_DIM
    blk = pltpu.VMEM((u3, hb, HEAD_DIM), F32)
    return _pcall(
        body, name="hgrn_bwd_rev" if reverse else "hgrn_bwd", grid=(N_HEADS,), in_specs=in_specs,
        out_specs=[head, head, head, lbs],
        out_shape=[_sds((tt, w), mid), _sds((tt, w), BF16), _sds((tt, w), mid), _sds((1, w), F32)],
        scratch=[pltpu.VMEM((nblk, HEAD_DIM, HEAD_DIM), F32), blk, blk, blk, blk, blk])(*args)


NA_HEADS_PER_STEP = 4


def _na_geometry(rows):
    r = pl.program_id(1)
    rs = jnp.clip(r - WIN_R // 2, 0, rows - WIN_R)
    return r, rs, r - rs


def _na_scores(q, kb, kc, bias):
    scale = HEAD_DIM ** -0.5
    nt = (((1,), (1,)), ((), ()))
    sb = lax.dot_general(q, kb, nt, preferred_element_type=F32) * scale + bias
    sc = lax.dot_general(q, kc, nt, preferred_element_type=F32) * scale
    m = jnp.maximum(jnp.max(sb, axis=-1, keepdims=True), jnp.max(sc, axis=-1, keepdims=True))
    pb, pc = jnp.exp(sb - m), jnp.exp(sc - m)
    inv = 1.0 / (jnp.sum(pb, axis=-1, keepdims=True) + jnp.sum(pc, axis=-1, keepdims=True))
    return pb * inv, pc * inv


def _na_fwd(qn, kall, p, seg_v, bias, n_ctx_rows):
    t, tt = qn.shape[0], kall.shape[0]
    rows = t // GRID_W
    nband = WIN_R * GRID_W

    nh = NA_HEADS_PER_STEP
    wide = nh * HEAD_DIM

    def body(q_ref, k_ref, v_ref, b_ref, o_ref):
        r, rs, _ = _na_geometry(rows)
        k0 = pl.multiple_of(n_ctx_rows + rs * GRID_W, GRID_W)
        for j in range(nh):
            sl = slice(j * HEAD_DIM, (j + 1) * HEAD_DIM)
            q = q_ref[:, sl]
            kb, kc = k_ref[pl.ds(k0, nband), sl], k_ref[pl.ds(0, n_ctx_rows), sl]
            vb, vc = v_ref[pl.ds(k0, nband), sl].astype(BF16), v_ref[pl.ds(0, n_ctx_rows), sl].astype(BF16)
            pb, pc = _na_scores(q, kb, kc, b_ref[j])
            o = jnp.dot(pb.astype(BF16), vb, preferred_element_type=F32) + jnp.dot(pc.astype(BF16), vc, preferred_element_type=F32)
            o_ref[:, sl] = o.astype(o_ref.dtype)

    variant = lambda h, r: (h, r - jnp.clip(r - WIN_R // 2, 0, rows - WIN_R), 0, 0)
    return _pcall(
        body, name="na_fwd", grid=(N_HEADS // nh, rows),
        in_specs=[pl.BlockSpec((GRID_W, wide), lambda h, r: (r, h)),
                  pl.BlockSpec((tt, wide), lambda h, r: (0, h)),
                  pl.BlockSpec((tt, wide), lambda h, r: (0, seg_v * (N_HEADS // nh) + h)),
                  pl.BlockSpec((nh, None, GRID_W, nband), variant)],
        out_specs=pl.BlockSpec((GRID_W, wide), lambda h, r: (r, h)),
        out_shape=_sds((t, N_HEADS * HEAD_DIM), BF16))(qn, kall, p, bias)


def _na_bwd(qn, kall, p, seg_v, bias, do, n_ctx_rows):
    t, tt = qn.shape[0], kall.shape[0]
    rows = t // GRID_W
    nband = WIN_R * GRID_W
    scale = HEAD_DIM ** -0.5
    tn = (((0,), (0,)), ((), ()))
    nt = (((1,), (1,)), ((), ()))

    nh = NA_HEADS_PER_STEP
    wide = nh * HEAD_DIM

    def body(q_ref, k_ref, v_ref, b_ref, do_ref, dq_ref, dk_ref, dv_ref, db_ref, dv_acc):
        r, rs, var = _na_geometry(rows)
        k0 = pl.multiple_of(n_ctx_rows + rs * GRID_W, GRID_W)
        fresh = jnp.logical_or(r <= WIN_R // 2, r > rows - WIN_R // 2)

        @pl.when(r == 0)
        def _():
            dk_ref[...] = jnp.zeros_like(dk_ref)
            dv_acc[...] = jnp.zeros_like(dv_acc)

        for j in range(nh):
            sl = slice(j * HEAD_DIM, (j + 1) * HEAD_DIM)
            q = q_ref[:, sl]
            kb, kc = k_ref[pl.ds(k0, nband), sl], k_ref[pl.ds(0, n_ctx_rows), sl]
            vb, vc = v_ref[pl.ds(k0, nband), sl].astype(BF16), v_ref[pl.ds(0, n_ctx_rows), sl].astype(BF16)
            pb, pc = _na_scores(q, kb, kc, b_ref[j])
            dof = do_ref[:, sl].astype(F32)
            dob = dof.astype(BF16)
            o = jnp.dot(pb.astype(BF16), vb, preferred_element_type=F32) + jnp.dot(pc.astype(BF16), vc, preferred_element_type=F32)
            delta = jnp.sum(dof * o, axis=-1, keepdims=True)
            dsb = pb * (lax.dot_general(dob, vb, nt, preferred_element_type=F32) - delta)
            dsc = pc * (lax.dot_general(dob, vc, nt, preferred_element_type=F32) - delta)
            dsb16, dsc16 = dsb.astype(BF16), dsc.astype(BF16)
            dq_ref[:, sl] = (jnp.dot(dsb16, kb, preferred_element_type=F32) + jnp.dot(dsc16, kc, preferred_element_type=F32)) * scale
            dk_ref[pl.ds(k0, nband), sl] += lax.dot_general(dsb16, q, tn, preferred_element_type=F32) * scale
            dk_ref[pl.ds(0, n_ctx_rows), sl] += lax.dot_general(dsc16, q, tn, preferred_element_type=F32) * scale
            dv_acc[pl.ds(k0, nband), sl] += lax.dot_general(pb.astype(BF16), dob, tn, preferred_element_type=F32)
            dv_acc[pl.ds(0, n_ctx_rows), sl] += lax.dot_general(pc.astype(BF16), dob, tn, preferred_element_type=F32)

            @pl.when(fresh)
            def _(j=j, dsb=dsb):
                db_ref[j] = dsb

            @pl.when(jnp.logical_not(fresh))
            def _(j=j, dsb=dsb):
                db_ref[j] += dsb

        @pl.when(r == rows - 1)
        def _():
            dv_ref[...] = dv_acc[...].astype(dv_ref.dtype)

    variant = lambda h, r: (h, r - jnp.clip(r - WIN_R // 2, 0, rows - WIN_R), 0, 0)
    head_all = pl.BlockSpec((tt, wide), lambda h, r: (0, h))
    qspec = pl.BlockSpec((GRID_W, wide), lambda h, r: (r, h))
    w = N_HEADS * HEAD_DIM
    return _pcall(
        body, name="na_bwd", grid=(N_HEADS // nh, rows),
        in_specs=[qspec, head_all, pl.BlockSpec((tt, wide), lambda h, r: (0, seg_v * (N_HEADS // nh) + h)),
                  pl.BlockSpec((nh, None, GRID_W, nband), variant), qspec],
        out_specs=[qspec, head_all, head_all, pl.BlockSpec((nh, None, GRID_W, nband), variant)],
        out_shape=[_sds((t, w), F32), _sds((tt, w), F32), _sds((tt, w), BF16), _sds((N_HEADS, WIN_R, GRID_W, nband), F32)],
        scratch=[pltpu.VMEM((tt, wide), F32)])(qn, kall, p, bias, do)


def _na_tables(t, n_ctx_rows):
    half, nf = HEAD_DIM // 2, HEAD_DIM // 4
    pos = np.arange(t)
    lane = np.arange(HEAD_DIM)
    inv = ROPE_THETA ** (-(np.arange(nf, dtype=np.float32)) / nf)
    which = np.where(lane < half, pos[:, None] // GRID_W, pos[:, None] % GRID_W).astype(np.float32)
    ang = which * inv[lane % nf][None, :]
    first = (lane % half) < nf
    cos = np.concatenate([np.ones((n_ctx_rows, HEAD_DIM), np.float32), np.cos(ang).astype(np.float32)])
    sin = np.concatenate([np.zeros((n_ctx_rows, HEAD_DIM), np.float32), np.where(first[None, :], -np.sin(ang), np.sin(ang)).astype(np.float32)])
    w = np.arange(GRID_W)
    dc = np.clip(w[None, :] - w[:, None], -(WIN_C - 1), WIN_C - 1) + WIN_C - 1
    onehot = np.zeros((32, GRID_W * GRID_W), np.float32)
    onehot[dc.reshape(-1), np.arange(GRID_W * GRID_W)] = 1.0
    cs = np.clip(w - WIN_C // 2, 0, GRID_W - WIN_C)
    col_in = (w[None, :] >= cs[:, None]) & (w[None, :] < cs[:, None] + WIN_C)
    onehot *= col_in.reshape(1, -1)
    neg = np.where(col_in, 0.0, NEG_BIG).astype(np.float32)
    return jnp.asarray(cos), jnp.asarray(sin), jnp.asarray(onehot), jnp.asarray(neg)


def _bias_slabs(rel_bias, onehot, neg):
    nr = 2 * WIN_R - 1
    rb = jnp.pad(rel_bias.reshape(N_HEADS * nr, 2 * WIN_C - 1), ((0, 0), (0, 1)))
    spread = _mm_f32(rb, onehot, "bias_spread").reshape(N_HEADS, nr, GRID_W, GRID_W)
    slabs = [spread[:, WIN_R - 1 - v:2 * WIN_R - 1 - v] for v in range(WIN_R)]
    b = jnp.stack(slabs, axis=1) + neg[None, None, None]
    return b.transpose(0, 1, 3, 2, 4).reshape(N_HEADS, WIN_R, GRID_W, WIN_R * GRID_W)


def _bias_grad(dbias, onehot):
    nr = 2 * WIN_R - 1
    d = dbias.reshape(N_HEADS, WIN_R, GRID_W, WIN_R, GRID_W).transpose(0, 1, 3, 2, 4)
    tot = jnp.zeros((N_HEADS, nr, GRID_W, GRID_W), F32)
    for v in range(WIN_R):
        tot = tot + jnp.pad(d[:, v], ((0, 0), (WIN_R - 1 - v, v), (0, 0), (0, 0)))
    g = _mm_f32(tot.reshape(N_HEADS * nr, GRID_W * GRID_W), onehot, "bias_grad", trans_b=True)
    return g[:, :2 * WIN_C - 1].reshape(1, N_HEADS, nr, 2 * WIN_C - 1)


def _shift_rows(u, up):
    n = u.shape[0]
    tio = lax.broadcasted_iota(jnp.int32, u.shape, 0)
    if up:
        return jnp.where(tio == n - 1, 0.0, pltpu.roll(u, n - 1, 0))
    return jnp.where(tio == 0, 0.0, pltpu.roll(u, 1, 0))


def _conv3(u, w_ref, b_ref):
    um, up = _shift_rows(u, False), _shift_rows(u, True)
    return um, up, um * w_ref[0:1, :] + u * w_ref[1:2, :] + up * w_ref[2:3, :] + b_ref[...]


def _ffn_act_fwd(u0, t3, cw, cb):
    t, n = u0.shape
    tc = _pick(n, (256, 128))

    def body(u_ref, t_ref, w_ref, b_ref, a_ref):
        _, _, uc = _conv3(u_ref[...], w_ref, b_ref)
        a_ref[...] = (uc * jax.nn.sigmoid(uc) * t_ref[...]).astype(a_ref.dtype)

    col = lambda rows_: pl.BlockSpec((rows_, tc), lambda j: (0, j))
    return _pcall(body, name="ffn_act_fwd", grid=(n // tc,), in_specs=[col(t), col(t), col(8), col(1)], out_specs=col(t),
                  out_shape=_sds((t, n), BF16))(u0, t3, cw, cb)


def _ffn_act_bwd(u0, t3, cw, cb, da):
    t, n = u0.shape
    tc = _pick(n, (256, 128))

    def body(u_ref, t_ref, w_ref, b_ref, da_ref, du_ref, dt_ref, dw_ref, db_ref):
        u = u_ref[...]
        um, up, uc = _conv3(u, w_ref, b_ref)
        sg = jax.nn.sigmoid(uc)
        dav = da_ref[...].astype(F32)
        dt_ref[...] = (dav * uc * sg).astype(dt_ref.dtype)
        duc = dav * t_ref[...] * sg * (1.0 + uc * (1.0 - sg))
        du = _shift_rows(duc, True) * w_ref[0:1, :] + duc * w_ref[1:2, :] + _shift_rows(duc, False) * w_ref[2:3, :]
        du_ref[...] = du.astype(du_ref.dtype)
        dw_ref[...] = jnp.zeros_like(dw_ref)
        dw_ref[0:1, :] = jnp.sum(duc * um, axis=0, keepdims=True)
        dw_ref[1:2, :] = jnp.sum(duc * u, axis=0, keepdims=True)
        dw_ref[2:3, :] = jnp.sum(duc * up, axis=0, keepdims=True)
        db_ref[...] = jnp.sum(duc, axis=0, keepdims=True)

    col = lambda rows_: pl.BlockSpec((rows_, tc), lambda j: (0, j))
    return _pcall(body, name="ffn_act_bwd", grid=(n // tc,), in_specs=[col(t), col(t), col(8), col(1), col(t)],
                  out_specs=[col(t), col(t), col(8), col(1)],
                  out_shape=[_sds((t, n), BF16), _sds((t, n), BF16), _sds((8, n), F32), _sds((1, n), F32)])(u0, t3, cw, cb, da)


def _adam_math(g, w, m, v):
    m2 = ADAM_B1 * m + (1.0 - ADAM_B1) * g
    v2 = ADAM_B2 * v + (1.0 - ADAM_B2) * (g * g)
    m_hat = m2 / (1.0 - ADAM_B1 ** ADAM_STEP)
    v_hat = v2 / (1.0 - ADAM_B2 ** ADAM_STEP)
    return -ADAM_LR * (m_hat / (jnp.sqrt(v_hat) + ADAM_EPS) + ADAM_WD * w), m2, v2


def _adam_big(parts, w, m, v, name):
    r, c = w.shape
    npart, _, cp = parts.shape
    tr = _pick(r, (64, 32, 16, 8))

    def body(p_ref, w_ref, m_ref, v_ref, g_ref, d_ref, m2_ref, v2_ref):
        g = p_ref[0, :, 0:c].astype(F32)
        for i in range(1, npart):
            g = g + p_ref[i, :, 0:c].astype(F32)
        d, m2, v2 = _adam_math(g, w_ref[...], m_ref[...], v_ref[...])
        g_ref[...] = g
        d_ref[...] = d
        m2_ref[...] = m2
        v2_ref[...] = v2

    row = pl.BlockSpec((tr, c), lambda i: (i, 0))
    return _pcall(body, name=name, grid=(r // tr,),
                  in_specs=[pl.BlockSpec((npart, tr, cp), lambda i: (0, i, 0)), row, row, row],
                  out_specs=[row] * 4, out_shape=[_sds((r, c), F32)] * 4)(parts, w, m, v)


def _adam_small(g, w, m, v):
    def body(g_ref, w_ref, m_ref, v_ref, d_ref, m2_ref, v2_ref):
        d_ref[...], m2_ref[...], v2_ref[...] = _adam_math(g_ref[...], w_ref[...], m_ref[...], v_ref[...])

    return _pcall(body, name="adam_small", out_shape=[_sds(g.shape, F32)] * 3)(g, w, m, v)


def _sum_parts(parts, name):
    def body(p_ref, o_ref):
        s = p_ref[0]
        for i in range(1, N_DEV):
            s = s + p_ref[i]
        o_ref[...] = s

    return _pcall(body, name=name, out_shape=_sds(parts.shape[1:], F32))(parts)


class _Pack:
    def __init__(self, shapes):
        self.shapes = shapes
        self.sizes = [int(np.prod(s)) for s in shapes]
        self.padded = [-(-n // (8 * LANE)) * 8 * LANE for n in self.sizes]
        self.offs = np.concatenate([[0], np.cumsum(self.padded)]).tolist()

    def pack(self, arrs):
        flat = [jnp.pad(a.reshape(-1).astype(F32), (0, p - n)) for a, n, p in zip(arrs, self.sizes, self.padded)]
        return jnp.concatenate(flat).reshape(-1, LANE)

    def unpack(self, slab):
        flat = slab.reshape(-1)
        return [flat[o:o + n].reshape(s) for o, n, s in zip(self.offs, self.sizes, self.shapes)]


def kernel(x, c, ctx, c_ctx, ada_w, ada_b, norm1_g, norm2_g, w_in, hgrn_lb_logits, hgrn_norm_g, na_q_norm_g, na_k_norm_g, na_rel_bias, w_branch_a, w_branch_b, w_out, ffn_w1, ffn_w3, ffn_conv_w, ffn_conv_b, ffn_w2, loss_target, m_c_ctx, m_ada_w, m_ada_b, m_norm1_g, m_norm2_g, m_w_in, m_hgrn_lb_logits, m_hgrn_norm_g, m_na_q_norm_g, m_na_k_norm_g, m_na_rel_bias, m_w_branch_a, m_w_branch_b, m_w_out, m_ffn_w1, m_ffn_w3, m_ffn_conv_w, m_ffn_conv_b, m_ffn_w2, v_c_ctx, v_ada_w, v_ada_b, v_norm1_g, v_norm2_g, v_w_in, v_hgrn_lb_logits, v_hgrn_norm_g, v_na_q_norm_g, v_na_k_norm_g, v_na_rel_bias, v_w_branch_a, v_w_branch_b, v_w_out, v_ffn_w1, v_ffn_w3, v_ffn_conv_w, v_ffn_conv_b, v_ffn_w2):
    t, d = x.shape[1], x.shape[2]
    n_ctx = ctx.shape[1]
    tt = n_ctx + t
    hw = N_HEADS * HEAD_DIM
    ci = w_in.shape[2]
    ca = ada_w.shape[2]
    ff_l = ffn_w1.shape[2]
    ff_p = -(-ff_l // LANE) * LANE
    rows = t // GRID_W
    assert rows >= WIN_R and t % GRID_W == 0 and n_ctx % GRID_W == 0 and ci % LANE == 0 and d % LANE == 0
    me = 4 * lax.axis_index("x") + 2 * lax.axis_index("y") + lax.axis_index("c")
    tr = _pick(n_ctx, (256, 128, 64))
    n_ctx_tiles = n_ctx // tr

    pad_c = lambda w: jnp.pad(w, ((0, 0), (0, ff_p - ff_l)))
    small_in = [c, hgrn_lb_logits.reshape(4, HEAD_DIM), jnp.pad(ffn_conv_w[0], ((0, 5), (0, ff_p - ff_l)))]
    c_all, lb_parts, cw_all = _exchange(small_in, "gather_params", scatter=False)
    ff = N_DEV * ff_p

    cc = jnp.concatenate([c_all.reshape(N_DEV, d), jnp.broadcast_to(c_ctx[None, :], (N_DEV, d))], axis=0)
    act = _tiled(lambda v: v * jax.nn.sigmoid(v), [_const(cc)], [(cc.shape, BF16, cc.shape, lambda *_: (0, 0))], (), "silu_c")[0]
    ada16 = _cast_bf16(ada_w[0], "cast_ada")
    mod_cols = _mm_xw(act, ada16.reshape(1, d, ca), F32, "ada_fwd")
    mod_all = _exchange([mod_cols], "gather_mod", scatter=False)[0]

    with _after(mod_all):
        w_in16 = _cast_bf16(w_in[0], "cast_w_in")
    g_in, = _gather_sc([w_in16], "gather_w_in", 1)
    rest16 = []
    for w_, nm in ((w_branch_a[0], "cast_w_a"), (w_branch_b[0], "cast_w_b"), (w_out[0], "cast_w_out"), (pad_c(ffn_w1[0]), "cast_w1"),
                   (pad_c(ffn_w3[0]), "cast_w3"), (jnp.pad(ffn_w2[0], ((0, ff_p - ff_l), (0, 0))), "cast_w2")):
        with _after(w_in16):
            rest16.append(_cast_bf16(w_, nm))
    g_a, g_b, g_out = _gather_sc(rest16[:3], "gather_mix", 2)
    g_w1, g_w3, g_w2 = _gather_sc(rest16[3:], "gather_ffn", 10)
    g_out = g_out.reshape(1, d, d)
    g_w2 = g_w2.reshape(1, ff, d)
    mod_all = mod_all.transpose(1, 0, 2).reshape(2 * N_DEV, N_MOD * d) + ada_b
    mod_l = lax.dynamic_slice_in_dim(mod_all, me, 1, axis=0).reshape(N_MOD, 1, d)
    mod_c = mod_all[N_DEV:N_DEV + 1].reshape(N_MOD, 1, d)
    mods1 = jnp.stack([mod_c[0:2], mod_l[0:2]])
    mods2 = jnp.stack([mod_l[3:5], mod_l[3:5]])
    gate1, gate2 = mod_l[2], mod_l[5]

    xcat = jnp.concatenate([ctx[0], x[0]], axis=0)
    hcat = _norm_mod_fwd(xcat, norm1_g, mods1, n_ctx_tiles, tr)
    p = _mm_xw(hcat, g_in, F32, "in_proj")
    lb_logits = lb_parts.transpose(1, 0, 2).reshape(2, 2, hw)
    lb_soft = _tiled(lambda a, b: (1.0 / (1.0 + jnp.exp(b - a)),), [_const(lb_logits[:, 0]), _const(lb_logits[:, 1])],
                     [((2, hw), F32, (2, hw), lambda *_: (0, 0))], (), "lb_softmax")[0]
    lb_f, lb_b = lb_soft[0:1], lb_soft[1:2]
    o_f, *saved_f = _hgrn_fwd(p, lb_f, 1, False, n_ctx)
    o_b, *saved_b = _hgrn_fwd(p, lb_b, 2, True, n_ctx)

    cos, sin, onehot, neg = _na_tables(t, n_ctx)
    bias = _bias_slabs(na_rel_bias[0], onehot, neg)
    heads = lambda a: [a[:, h * HEAD_DIM:(h + 1) * HEAD_DIM] for h in range(N_HEADS)]
    seg_rows = lambda seg, off=0: (p, (tr, hw), lambda i: (i + off, seg))
    wide = lambda a, off=0: (a, (tr, hw), lambda i: (i + off, 0))
    out_wide = lambda n_rows, dt: ((n_rows, hw), dt, (tr, hw), lambda i: (i, 0))
    acc_head = ((1, HEAD_DIM), F32, (1, HEAD_DIM), lambda i: (0, 0))
    lat0 = n_ctx // tr
    tab = lambda a, off=0: (a, (tr, HEAD_DIM), lambda i: (i + off, 0))

    def qk_fn(tv, g, cs, sn):
        return (jnp.concatenate([_f_qk(th, g, cs, sn) for th in heads(tv)], axis=1),)

    qn = _tiled(qk_fn, [seg_rows(5, lat0), _const(na_q_norm_g), tab(cos, lat0), tab(sin, lat0)], [out_wide(t, BF16)], (t // tr,), "q_norm_rope")[0]
    kall = _tiled(qk_fn, [seg_rows(6), _const(na_k_norm_g), tab(cos), tab(sin)], [out_wide(tt, BF16)], (tt // tr,), "k_norm_rope")[0]
    y_b = _na_fwd(qn, kall, p, 7, bias, n_ctx)

    def readout_fn(a, b, gt, g):
        return (jnp.concatenate([_f_readout(ah, bh, gh, g) for ah, bh, gh in zip(heads(a), heads(b), heads(gt))], axis=1),)

    y_a = _tiled(readout_fn, [wide(o_f, lat0), wide(o_b, lat0), seg_rows(4, lat0), _const(hgrn_norm_g)], [out_wide(t, BF16)], (t // tr,), "hgrn_readout")[0]

    p_a = _mm_xw(y_a, g_a, F32, "branch_a")
    p_b = _mm_xw(y_b, g_b, F32, "branch_b")
    td = _pick(d, (512, 256, 128))
    nd_t = d // td
    lat_r = n_ctx // tr
    gcol = lambda k: (p, (tr, td), lambda i, j, k=k: (i + lat_r, 8 * hw // td + k * nd_t + j))
    dtile = lambda a: (a, (tr, td), lambda i, j: (i, j))
    z = _tiled(lambda ga, gb, pa, pb: (_f_merge(ga, gb, pa, pb),), [gcol(0), gcol(1), dtile(p_a), dtile(p_b)],
               [((t, d), BF16, (tr, td), lambda i, j: (i, j))], (t // tr, nd_t), "merge")[0]
    mix = _mm_xw(z, g_out, F32, "out_proj")
    xl = x[0]
    x_mid = _tiled(lambda a, g, m_: (_f_resid(a, g, m_),), [_rows(xl, tr), _const(gate1), _rows(mix, tr)],
                   [((t, d), F32, (tr, d), lambda i: (i, 0))], (t // tr,), "resid1")[0]

    h2 = _norm_mod_fwd(x_mid, norm2_g, mods2, 0, tr)
    u0 = _mm_xw(h2, g_w1, F32, "ffn_up1")
    t3 = _mm_xw(h2, g_w3, F32, "ffn_up3")
    cw_full = cw_all.transpose(1, 0, 2).reshape(8, ff)
    cb_full = jnp.pad(ffn_conv_b.reshape(N_DEV, ff_l), ((0, 0), (0, ff_p - ff_l))).reshape(1, ff)
    a_act = _ffn_act_fwd(u0, t3, cw_full, cb_full)
    f_out = _mm_xw(a_act, g_w2, F32, "ffn_down")

    def loss_fn(xm, g, f, tg):
        err = xm + g * f - tg
        return err * (1.0 / d), jnp.sum(err * err, axis=0, keepdims=True) * (0.5 / d), jnp.sum(err * (1.0 / d) * f, axis=0, keepdims=True)

    dy, loss_cols, d_gate2 = _tiled(loss_fn, [_rows(x_mid, tr), _const(gate2), _rows(f_out, tr), _rows(loss_target[0], tr)],
                                    [((t, d), F32, (tr, d), lambda i: (i, 0)), ((1, d), F32, (1, d), lambda i: (0, 0)),
                                     ((1, d), F32, (1, d), lambda i: (0, 0))], (t // tr,), "loss", acc=(1, 2))

    df = _tiled(lambda a, g: (a * g,), [_rows(dy, tr), _const(gate2)], [((t, d), BF16, (tr, d), lambda i: (i, 0))], (t // tr,), "d_ffn_out")[0]
    d_w2 = _mm_xtdy(a_act, df, 1, BF16, "d_w2")
    da = _mm_dyw(df, g_w2, BF16, "d_act")
    du0, dt3, d_cw, d_cb = _ffn_act_bwd(u0, t3, cw_full, cb_full, da)
    d_w1 = _mm_xtdy(h2, du0, N_DEV, BF16, "d_w1")
    d_w3 = _mm_xtdy(h2, dt3, N_DEV, BF16, "d_w3")
    ffn_blocks = [d_w1, d_w3, d_w2.reshape(N_DEV, ff_p, d)]
    ffn_theirs = _pair_swap_sc(ffn_blocks, "scatter_ffn_pair", 3)
    dh2 = _mm_dyw(du0, g_w1, F32, "d_h2_a")
    dh2 = _mm_dyw(dt3, g_w3, F32, "d_h2_b", init=dh2)
    with _after(dh2):
        ffn_sums = [_pair_add(ffn_blocks[0], ffn_theirs[0], "pair_add_w1")]
    ffn_sums += [_pair_add(ffn_blocks[1], ffn_theirs[1], "pair_add_w3"), _pair_add(ffn_blocks[2], ffn_theirs[2], "pair_add_w2")]
    r_w1, r_w3, r_w2 = _chip_scatter_sc(ffn_sums, "scatter_ffn_chips", 4)
    with _after(*ffn_sums):
        dx_mid, d_norm2, d_mods2 = _norm_mod_bwd(x_mid, norm2_g, mods2, dh2, dy, 0, tr, "norm_mod2_bwd")

    dm, d_gate1 = _tiled(lambda dxm, g, m_: (dxm * g, jnp.sum(dxm * m_, axis=0, keepdims=True)),
                         [_rows(dx_mid, tr), _const(gate1), _rows(mix, tr)],
                         [((t, d), BF16, (tr, d), lambda i: (i, 0)), ((1, d), F32, (1, d), lambda i: (0, 0))], (t // tr,), "d_resid1", acc=(1,))
    d_wout = _mm_xtdy(z, dm, 1, BF16, "d_w_out")
    dz = _mm_dyw(dm, g_out, F32, "d_merge")

    def merge_bwd(ga, gb, pa, pb, dzv):
        _, vjp = jax.vjp(_f_merge, ga, gb, pa, pb)
        return vjp(dzv)

    dga, dgb, dpa, dpb = _tiled(merge_bwd, [gcol(0), gcol(1), dtile(p_a), dtile(p_b), dtile(dz)],
                                [((t, d), BF16, (tr, td), lambda i, j: (i, j))] * 4, (t // tr, nd_t), "merge_bwd")
    d_wa = _mm_xtdy(y_a, dpa, N_DEV, BF16, "d_w_a")
    d_wb = _mm_xtdy(y_b, dpb, N_DEV, BF16, "d_w_b")
    mix_blocks = [d_wa, d_wb, d_wout.reshape(N_DEV, d // N_DEV, d)]
    mix_theirs = _pair_swap_sc(mix_blocks, "scatter_mix_pair", 5)
    dy_a = _mm_dyw(dpa, g_a, F32, "d_y_a")
    dy_b = _mm_dyw(dpb, g_b, BF16, "d_y_b")
    with _after(dy_b):
        mix_sums = [_pair_add(mix_blocks[0], mix_theirs[0], "pair_add_wa")]
    mix_sums += [_pair_add(mix_blocks[1], mix_theirs[1], "pair_add_wb"), _pair_add(mix_blocks[2], mix_theirs[2], "pair_add_wout")]
    r_a, r_b, r_out = _chip_scatter_sc(mix_sums, "scatter_mix_chips", 6)

    def readout_bwd(a, b, gt, g, ct):
        das, dgts, dg = [], [], 0.0
        for ah, bh, gh, ch in zip(heads(a), heads(b), heads(gt), heads(ct)):
            _, vjp = jax.vjp(_f_readout, ah, bh, gh, g)
            da_h, _, dgt_h, dg_h = vjp(ch)
            das.append(da_h)
            dgts.append(dgt_h)
            dg = dg + dg_h
        return jnp.concatenate(das, axis=1), jnp.concatenate(dgts, axis=1), dg

    with _after(*mix_sums):
        do_h, d_gate_o, d_hnorm = _tiled(
            readout_bwd, [wide(o_f, lat0), wide(o_b, lat0), seg_rows(4, lat0), _const(hgrn_norm_g), wide(dy_a)],
            [out_wide(t, F32), out_wide(t, BF16), acc_head], (t // tr,), "readout_bwd", acc=(2,))
    dq1, dfl_f, dv1, dlb_f = _hgrn_bwd(p, lb_f, do_h, saved_f, 1, False, n_ctx, None)
    dq_h, dfl_b, dv_h, dlb_b = _hgrn_bwd(p, lb_b, do_h, saved_b, 2, True, n_ctx, (dq1, dv1))

    dqn, dkall, dv_na, dbias = _na_bwd(qn, kall, p, 7, bias, dy_b, n_ctx)

    def qk_bwd(tv, g, cs, sn, ct):
        dts, dg = [], 0.0
        for th, ch in zip(heads(tv), heads(ct)):
            _, vjp = jax.vjp(lambda a, b: _f_qk(a, b, cs, sn), th, g)
            dt_h, dg_h = vjp(ch)
            dts.append(dt_h)
            dg = dg + dg_h
        return jnp.concatenate(dts, axis=1), dg

    d_pq, d_qnorm = _tiled(qk_bwd, [seg_rows(5, lat0), _const(na_q_norm_g), tab(cos, lat0), tab(sin, lat0), wide(dqn)],
                           [out_wide(t, BF16), acc_head], (t // tr,), "q_norm_rope_bwd", acc=(1,))
    d_pk, d_knorm = _tiled(qk_bwd, [seg_rows(6), _const(na_k_norm_g), tab(cos), tab(sin), wide(dkall)],
                           [out_wide(tt, BF16), acc_head], (tt // tr,), "k_norm_rope_bwd", acc=(1,))

    zc = lambda w_: jnp.zeros((n_ctx, w_), BF16)
    lat_only = lambda a: jnp.concatenate([zc(a.shape[1]), a], axis=0)
    dp = jnp.concatenate([dq_h, dfl_f, dfl_b, dv_h, lat_only(d_gate_o), lat_only(d_pq), d_pk, dv_na, lat_only(dga), lat_only(dgb)], axis=1)
    d_win = _mm_xtdy(hcat, dp, N_DEV, BF16, "d_w_in")
    win_theirs, = _pair_swap_sc([d_win], "scatter_w_in_pair", 7)
    with _after(d_win):
        dhcat = _mm_dyw(dp, g_in, BF16, "d_hcat")
    zero_ctx = jnp.concatenate([jnp.zeros((n_ctx, d), F32), dx_mid], axis=0)
    dxcat, d_norm1, d_mods1 = _norm_mod_bwd(xcat, norm1_g, mods1, dhcat, zero_ctx, n_ctx_tiles, tr, "norm_mod1_bwd")
    grad_x = dxcat[n_ctx:][None]

    zd = jnp.zeros((1, d), F32)
    dmod_l = jnp.concatenate([d_mods1[1, 0], d_mods1[1, 1], d_gate1, d_mods2[1, 0], d_mods2[1, 1], d_gate2], axis=1)
    dmod_c = jnp.concatenate([d_mods1[0, 0], d_mods1[0, 1], zd, zd, zd, zd], axis=1)
    dmods = jnp.concatenate([dmod_l, dmod_c], axis=0).reshape(2, N_DEV, ca).transpose(1, 0, 2)
    dmods = jnp.pad(dmods, ((0, 0), (0, 6), (0, 0)))
    got, = _exchange_sc([dmods], "scatter_dmod", True, 11)

    res = {}
    with _after(dmods):
        win_sums = _pair_add(d_win, win_theirs, "pair_add_w_in")
    r_in, = _chip_scatter_sc([win_sums], "scatter_w_in_chips", 8)
    with _after(win_sums):
        res["ffn_w1"] = _adam_big(r_w1, ffn_w1[0], m_ffn_w1[0], v_ffn_w1[0], "adam_w1")
    res["ffn_w3"] = _adam_big(r_w3, ffn_w3[0], m_ffn_w3[0], v_ffn_w3[0], "adam_w3")
    res["ffn_w2"] = _adam_big(r_w2, ffn_w2[0], m_ffn_w2[0], v_ffn_w2[0], "adam_w2")
    res["w_branch_a"] = _adam_big(r_a, w_branch_a[0], m_w_branch_a[0], v_w_branch_a[0], "adam_w_a")
    res["w_branch_b"] = _adam_big(r_b, w_branch_b[0], m_w_branch_b[0], v_w_branch_b[0], "adam_w_b")
    res["w_out"] = _adam_big(r_out, w_out[0], m_w_out[0], v_w_out[0], "adam_w_out")

    dm_rows = jnp.concatenate([got[:, 0], got[:, 1]], axis=0)
    with _after(res["w_out"][1]):
        d_ada = _mm_xtdy(act, dm_rows, 1, F32, "d_ada_w")[0]
    back = _mm_dyw(dm_rows, ada16.reshape(1, d, ca), F32, "d_silu_c")
    d_cctx_part = _tiled(lambda b, v: (jnp.sum(b[N_DEV:], axis=0, keepdims=True) * (jax.nn.sigmoid(v) * (1.0 + v * (1.0 - jax.nn.sigmoid(v)))),),
                         [_const(back), _const(c_ctx.reshape(1, d))], [((1, d), F32, (1, d), lambda *_: (0, 0))], (), "d_c_ctx")[0]
    res["ada_w"] = _adam_big(d_ada[None], ada_w[0], m_ada_w[0], v_ada_w[0], "adam_ada")

    d_rel = _bias_grad(dbias, onehot)
    d_lb_soft = jnp.concatenate([dlb_f, dlb_b], axis=0)
    d_lb0 = _tiled(lambda s, g: (g * s * (1.0 - s),), [_const(lb_soft), _const(d_lb_soft)], [((2, hw), F32, (2, hw), lambda *_: (0, 0))], (), "d_lb")[0]
    d_lb_full = jnp.stack([d_lb0, -d_lb0], axis=1)
    d_cw_l = d_cw[:3].reshape(3, N_DEV, ff_p)[:, :, :ff_l].reshape(1, 3, N_DEV * ff_l)
    d_cb_l = d_cb.reshape(N_DEV, ff_p)[:, :ff_l].reshape(1, N_DEV * ff_l)
    small = [d_cctx_part.reshape(d), (dmod_l + dmod_c), d_norm1, d_norm2, d_lb_full, d_hnorm, d_qnorm, d_knorm, d_rel, d_cw_l, d_cb_l, loss_cols]
    pk = _Pack([a.shape for a in small])
    small_slab = pk.pack(small)
    small_parts, = _gather_sc([small_slab], "gather_small", 9)
    with _after(small_slab):
        res["w_in"] = _adam_big(r_in, w_in[0], m_w_in[0], v_w_in[0], "adam_w_in")
    tot = _sum_parts(small_parts, "sum_small")
    g_cctx, g_ada_b, g_n1, g_n2, g_lb, g_hn, g_qn, g_kn, g_rel, g_cw, g_cb, loss_all = pk.unpack(tot)
    loss = _tiled(lambda v: (jnp.sum(v, axis=1, keepdims=True),), [_const(loss_all)], [((1, 1), F32, (1, 1), lambda *_: (0, 0))], (), "loss_total")[0][0, 0]
    g_lb = lax.dynamic_slice_in_dim(g_lb, me * HEAD_DIM, HEAD_DIM, axis=2)
    g_cw = lax.dynamic_slice_in_dim(g_cw, me * ff_l, ff_l, axis=2)
    small_names = [("c_ctx", g_cctx, c_ctx, m_c_ctx, v_c_ctx), ("ada_b", g_ada_b, ada_b, m_ada_b, v_ada_b),
                   ("norm1_g", g_n1, norm1_g, m_norm1_g, v_norm1_g), ("norm2_g", g_n2, norm2_g, m_norm2_g, v_norm2_g),
                   ("hgrn_lb_logits", g_lb, hgrn_lb_logits, m_hgrn_lb_logits, v_hgrn_lb_logits),
                   ("hgrn_norm_g", g_hn, hgrn_norm_g, m_hgrn_norm_g, v_hgrn_norm_g), ("na_q_norm_g", g_qn, na_q_norm_g, m_na_q_norm_g, v_na_q_norm_g),
                   ("na_k_norm_g", g_kn, na_k_norm_g, m_na_k_norm_g, v_na_k_norm_g), ("na_rel_bias", g_rel, na_rel_bias, m_na_rel_bias, v_na_rel_bias),
                   ("ffn_conv_w", g_cw, ffn_conv_w, m_ffn_conv_w, v_ffn_conv_w), ("ffn_conv_b", g_cb, ffn_conv_b, m_ffn_conv_b, v_ffn_conv_b)]
    pk2 = _Pack([s[1].shape for s in small_names])
    sd, sm, sv = _adam_small(*[pk2.pack([s[i] for s in small_names]) for i in (1, 2, 3, 4)])
    sd, sm, sv = pk2.unpack(sd), pk2.unpack(sm), pk2.unpack(sv)
    res.update({s[0]: (s[1], sd[i], sm[i], sv[i]) for i, s in enumerate(small_names)})
    for k in ("w_in", "w_branch_a", "w_branch_b", "w_out", "ffn_w1", "ffn_w3", "ffn_w2", "ada_w"):
        res[k] = tuple(a[None] for a in res[k])

    order = ["c_ctx", "ada_w", "ada_b", "norm1_g", "norm2_g", "w_in", "hgrn_lb_logits", "hgrn_norm_g", "na_q_norm_g", "na_k_norm_g",
             "na_rel_bias", "w_branch_a", "w_branch_b", "w_out", "ffn_w1", "ffn_w3", "ffn_conv_w", "ffn_conv_b", "ffn_w2"]
    shapes = {"c_ctx": c_ctx.shape, "ada_b": ada_b.shape, "norm1_g": norm1_g.shape, "norm2_g": norm2_g.shape,
              "hgrn_lb_logits": hgrn_lb_logits.shape, "hgrn_norm_g": hgrn_norm_g.shape, "na_q_norm_g": na_q_norm_g.shape,
              "na_k_norm_g": na_k_norm_g.shape, "na_rel_bias": na_rel_bias.shape, "ffn_conv_w": ffn_conv_w.shape, "ffn_conv_b": ffn_conv_b.shape}
    outs = [loss, grad_x]
    for part in range(4):
        for k in order:
            a = res[k][part]
            outs.append(a.reshape(shapes[k]) if k in shapes else a)
    return tuple(outs)
```

```python
import functools

import numpy as np
import jax
import jax.numpy as jnp
from jax import lax
from jax.experimental import pallas as pl
from jax.experimental.pallas import tpu as pltpu
from jax.experimental.pallas import tpu_sc as plsc

F32 = jnp.float32
BF16 = jnp.bfloat16
HIGHEST = lax.Precision.HIGHEST

N_DEV = 8
MESH_ID = pl.DeviceIdType.MESH
LANE = 128
HEAD_DIM = 128
N_HEADS = 8
GRID_W = 64
WIN_R = 8
WIN_C = 16
ROPE_THETA = 10000.0
EPS = 1e-6
N_MOD = 6
HGRN_BLOCK = 16
NEG_BIG = -1e30
VMEM_LIMIT = 56 << 20

ADAM_LR = 0.001
ADAM_B1 = 0.9
ADAM_B2 = 0.999
ADAM_EPS = 1e-08
ADAM_WD = 0.01
ADAM_STEP = 10

HBM_SPEC = pl.BlockSpec(memory_space=pltpu.HBM)


_ORDER_AFTER = []


class _after:
    def __init__(self, *arrs):
        self.arrs = list(arrs)

    def __enter__(self):
        _ORDER_AFTER.extend(self.arrs)

    def __exit__(self, *exc):
        del _ORDER_AFTER[:]


def _tie(x, *deps):
    return lax.optimization_barrier((x, *deps))[0]


def _pcall(body, *, name, out_shape, grid=None, in_specs=None, out_specs=None, scratch=(), aliases=None):
    kw = {}
    if grid is not None:
        kw["grid"] = grid
    extra = []
    if _ORDER_AFTER and in_specs is not None:
        extra = list(_ORDER_AFTER)
        del _ORDER_AFTER[:]
        n_in, n_extra, inner = len(in_specs), len(extra), body
        in_specs = list(in_specs) + [pl.BlockSpec(memory_space=pl.ANY)] * n_extra

        def body(*refs):
            return inner(*refs[:n_in], *refs[n_in + n_extra:])

    if extra:
        call = _pcall_inner(body, name, out_shape, kw, in_specs, out_specs, scratch, aliases)
        return lambda *args: call(*args, *extra)
    return _pcall_inner(body, name, out_shape, kw, in_specs, out_specs, scratch, aliases)


def _pcall_inner(body, name, out_shape, kw, in_specs, out_specs, scratch, aliases):
    if in_specs is not None:
        kw["in_specs"] = in_specs
    if out_specs is not None:
        kw["out_specs"] = out_specs
    if scratch:
        kw["scratch_shapes"] = list(scratch)
    if aliases:
        kw["input_output_aliases"] = aliases
    return pl.pallas_call(body, name=name, out_shape=out_shape,
                          compiler_params=pltpu.CompilerParams(vmem_limit_bytes=VMEM_LIMIT), **kw)


def _pick(dim, cands):
    for c in cands:
        if c <= dim and dim % c == 0:
            return c
    return dim


def _sds(shape, dtype):
    return jax.ShapeDtypeStruct(tuple(shape), dtype)


def _peers():
    x, y, c = lax.axis_index("x"), lax.axis_index("y"), lax.axis_index("c")
    out = []
    for k in range(1, N_DEV):
        px = 1 - x if (k >> 2) & 1 else x
        py = 1 - y if (k >> 1) & 1 else y
        pc = 1 - c if k & 1 else c
        out.append((k, (px, py, pc), 4 * px + 2 * py + pc))
    return 4 * x + 2 * y + c, out


def _exchange(arrs, name, scatter):
    n = len(arrs)

    def body(*refs):
        ins, outs = refs[:n], refs[n:2 * n]
        send, recv, loc = refs[2 * n:]
        me, peers = _peers()
        started = []
        for i in range(n):
            src = ins[i].at[me] if scatter else ins[i]
            cp = pltpu.make_async_copy(src, outs[i].at[me], loc.at[i])
            cp.start()
            started.append(cp)
        sends = []
        for k, peer, pidx in peers:
            for i in range(n):
                src = ins[i].at[pidx] if scatter else ins[i]
                cp = pltpu.make_async_remote_copy(src_ref=src, dst_ref=outs[i].at[me], send_sem=send.at[i * 7 + k - 1],
                                                  recv_sem=recv.at[i * 7 + k - 1], device_id=peer, device_id_type=MESH_ID)
                cp.start()
                sends.append(cp)
        for k, peer, pidx in peers:
            for i in range(n):
                src = ins[i].at[pidx] if scatter else ins[i]
                pltpu.make_async_remote_copy(src_ref=src, dst_ref=outs[i].at[pidx], send_sem=send.at[i * 7 + k - 1],
                                             recv_sem=recv.at[i * 7 + k - 1], device_id=peer, device_id_type=MESH_ID).wait_recv()
        for cp in sends:
            cp.wait_send()
        for cp in started:
            cp.wait()

    out_shape = [_sds(a.shape if scatter else (N_DEV,) + a.shape, a.dtype) for a in arrs]
    res = _pcall(body, name=name, out_shape=out_shape, in_specs=[HBM_SPEC] * n, out_specs=[HBM_SPEC] * n,
                 scratch=[pltpu.SemaphoreType.DMA((7 * n,)), pltpu.SemaphoreType.DMA((7 * n,)), pltpu.SemaphoreType.DMA((n,))])(*arrs)
    return list(res)


def _exchange_sc(arrs, name, scatter, collective_id):
    n = len(arrs)
    srcs = [jax.new_ref(a, memory_space=pltpu.MemorySpace.HBM) for a in arrs]
    lands = [jax.empty_ref(_sds(a.shape if scatter else (N_DEV,) + a.shape, a.dtype), memory_space=pltpu.MemorySpace.HBM) for a in arrs]

    @pl.kernel(mesh=plsc.ScalarSubcoreMesh(axis_name="seq", num_cores=1), name=name,
               scratch_types=(pltpu.SemaphoreType.DMA((7 * n,)), pltpu.SemaphoreType.DMA((7 * n,)), pltpu.SemaphoreType.DMA((n,))),
               compiler_params=pltpu.CompilerParams(collective_id=collective_id))
    def launch(send, recv, loc):
        me, peers = _peers()
        barrier = pltpu.get_barrier_semaphore()
        for _, peer, _ in peers:
            pl.semaphore_signal(barrier, inc=1, device_id=peer, device_id_type=MESH_ID)
        pl.semaphore_wait(barrier, N_DEV - 1)
        own = [pltpu.make_async_copy(srcs[i].at[me] if scatter else srcs[i], lands[i].at[me], loc.at[i]) for i in range(n)]
        for cp in own:
            cp.start()
        sends = []
        for k, peer, pidx in peers:
            for i in range(n):
                src = srcs[i].at[pidx] if scatter else srcs[i]
                cp = pltpu.make_async_remote_copy(src_ref=src, dst_ref=lands[i].at[me], send_sem=send.at[i * 7 + k - 1],
                                                  recv_sem=recv.at[i * 7 + k - 1], device_id=peer, device_id_type=MESH_ID)
                cp.start()
                sends.append(cp)
        for k, peer, pidx in peers:
            for i in range(n):
                src = srcs[i].at[pidx] if scatter else srcs[i]
                pltpu.make_async_remote_copy(src_ref=src, dst_ref=lands[i].at[pidx], send_sem=send.at[i * 7 + k - 1],
                                             recv_sem=recv.at[i * 7 + k - 1], device_id=peer, device_id_type=MESH_ID).wait_recv()
        for cp in sends:
            cp.wait_send()
        for cp in own:
            cp.wait()

    launch()
    return [r[...] for r in lands]


def _gather_sc(arrs, name, collective_id):
    n = len(arrs)
    srcs = [jax.new_ref(a, memory_space=pltpu.MemorySpace.HBM) for a in arrs]
    lands = [jax.empty_ref(_sds((N_DEV,) + a.shape, a.dtype), memory_space=pltpu.MemorySpace.HBM) for a in arrs]

    @pl.kernel(mesh=plsc.ScalarSubcoreMesh(axis_name="seq", num_cores=1), name=name,
               scratch_types=(pltpu.SemaphoreType.DMA((7 * n,)), pltpu.SemaphoreType.DMA((7 * n,)), pltpu.SemaphoreType.DMA((n,))),
               compiler_params=pltpu.CompilerParams(collective_id=collective_id))
    def launch(send, recv, loc):
        x, y, c = lax.axis_index("x"), lax.axis_index("y"), lax.axis_index("c")
        me, sibling = (x, y, c), (x, y, 1 - c)
        chips = [(1 - x, y), (x, 1 - y), (1 - x, 1 - y)]
        index = lambda px, py, pc: 4 * px + 2 * py + pc
        barrier = pltpu.get_barrier_semaphore()
        for peer in [sibling] + [(*chip, c) for chip in chips]:
            pl.semaphore_signal(barrier, inc=1, device_id=peer, device_id_type=MESH_ID)
        pl.semaphore_wait(barrier, 4)

        def copy(i, k, block, to, from_src):
            return pltpu.make_async_remote_copy(src_ref=srcs[i] if from_src else lands[i].at[index(*block)], dst_ref=lands[i].at[index(*block)],
                                                send_sem=send.at[i * 7 + k], recv_sem=recv.at[i * 7 + k], device_id=to, device_id_type=MESH_ID)

        own = [pltpu.make_async_copy(srcs[i], lands[i].at[index(*me)], loc.at[i]) for i in range(n)]
        for cp in own:
            cp.start()
        started = []
        for i in range(n):
            started.append(copy(i, 0, me, sibling, True))
            started += [copy(i, 1 + j, me, (*chip, c), True) for j, chip in enumerate(chips)]
        for cp in started:
            cp.start()
        for j, chip in enumerate(chips):
            for i in range(n):
                copy(i, 1 + j, (*chip, c), me, False).wait_recv()
                fwd = copy(i, 4 + j, (*chip, c), sibling, False)
                fwd.start()
                started.append(fwd)
        for i in range(n):
            copy(i, 0, sibling, me, False).wait_recv()
            for j, chip in enumerate(chips):
                copy(i, 4 + j, (*chip, 1 - c), me, False).wait_recv()
        for cp in started:
            cp.wait_send()
        for cp in own:
            cp.wait()

    launch()
    return [r[...] for r in lands]


def _pair_swap_sc(blocks, name, collective_id):
    n = len(blocks)
    srcs = [jax.new_ref(b, memory_space=pltpu.MemorySpace.HBM) for b in blocks]
    lands = [jax.empty_ref(_sds((4,) + b.shape[1:], b.dtype), memory_space=pltpu.MemorySpace.HBM) for b in blocks]

    @pl.kernel(mesh=plsc.ScalarSubcoreMesh(axis_name="seq", num_cores=1), name=name,
               scratch_types=(pltpu.SemaphoreType.DMA((4 * n,)), pltpu.SemaphoreType.DMA((4 * n,))),
               compiler_params=pltpu.CompilerParams(collective_id=collective_id))
    def launch(send, recv):
        x, y, c = lax.axis_index("x"), lax.axis_index("y"), lax.axis_index("c")
        sibling = (x, y, 1 - c)
        barrier = pltpu.get_barrier_semaphore()
        pl.semaphore_signal(barrier, inc=1, device_id=sibling, device_id_type=MESH_ID)
        pl.semaphore_wait(barrier, 1)
        copies = [pltpu.make_async_remote_copy(src_ref=srcs[i].at[2 * q + 1 - c], dst_ref=lands[i].at[q], send_sem=send.at[4 * i + q],
                                               recv_sem=recv.at[4 * i + q], device_id=sibling, device_id_type=MESH_ID)
                  for i in range(n) for q in range(4)]
        for cp in copies:
            cp.start()
        for cp in copies:
            cp.wait()

    launch()
    return [r[...] for r in lands]


def _pair_add(blocks, theirs, name):
    _, r, c_ = blocks.shape
    mine = lax.dynamic_index_in_dim(blocks.reshape(4, 2, r, c_), lax.axis_index("c"), axis=1, keepdims=False)
    tr = _pick(r, (256, 128, 64, 16))
    spec = (None, tr, c_)
    return _tiled(lambda a, b: (a.astype(F32) + b.astype(F32),), [(mine, spec, lambda q, i: (q, i, 0)), (theirs, spec, lambda q, i: (q, i, 0))],
                  [((4, r, c_), BF16, spec, lambda q, i: (q, i, 0))], (4, r // tr), name)[0]


def _chip_scatter_sc(sums, name, collective_id):
    n = len(sums)
    srcs = [jax.new_ref(s, memory_space=pltpu.MemorySpace.HBM) for s in sums]
    lands = [jax.empty_ref(_sds(s.shape, s.dtype), memory_space=pltpu.MemorySpace.HBM) for s in sums]

    @pl.kernel(mesh=plsc.ScalarSubcoreMesh(axis_name="seq", num_cores=1), name=name,
               scratch_types=(pltpu.SemaphoreType.DMA((3 * n,)), pltpu.SemaphoreType.DMA((3 * n,)), pltpu.SemaphoreType.DMA((n,))),
               compiler_params=pltpu.CompilerParams(collective_id=collective_id))
    def launch(send, recv, loc):
        x, y, c = lax.axis_index("x"), lax.axis_index("y"), lax.axis_index("c")
        chips = [(1 - x, y), (x, 1 - y), (1 - x, 1 - y)]
        my_chip = 2 * x + y
        barrier = pltpu.get_barrier_semaphore()
        for chip in chips:
            pl.semaphore_signal(barrier, inc=1, device_id=(*chip, c), device_id_type=MESH_ID)
        pl.semaphore_wait(barrier, 3)
        own = [pltpu.make_async_copy(srcs[i].at[my_chip], lands[i].at[my_chip], loc.at[i]) for i in range(n)]
        for cp in own:
            cp.start()
        sends = [pltpu.make_async_remote_copy(src_ref=srcs[i].at[2 * px + py], dst_ref=lands[i].at[my_chip], send_sem=send.at[3 * i + j],
                                              recv_sem=recv.at[3 * i + j], device_id=(px, py, c), device_id_type=MESH_ID)
                 for j, (px, py) in enumerate(chips) for i in range(n)]
        for cp in sends:
            cp.start()
        for j, (px, py) in enumerate(chips):
            for i in range(n):
                pltpu.make_async_remote_copy(src_ref=srcs[i].at[my_chip], dst_ref=lands[i].at[2 * px + py], send_sem=send.at[3 * i + j],
                                             recv_sem=recv.at[3 * i + j], device_id=(px, py, c), device_id_type=MESH_ID).wait_recv()
        for cp in sends:
            cp.wait_send()
        for cp in own:
            cp.wait()

    launch()
    return [r[...] for r in lands]


class _InFlight:
    def __init__(self, send, recv, srcs, lands, token, scatter):
        self.send, self.recv, self.srcs, self.lands, self.token, self.scatter = send, recv, srcs, lands, token, scatter


SEM_SPEC = pl.BlockSpec(memory_space=pltpu.SEMAPHORE)
SIDE_EFFECT = pltpu.SideEffectType.DATAFLOW_SIDE_EFFECTING


def _exchange_start(arrs, name, scatter):
    n = len(arrs)

    def body(*refs):
        ins, lands = refs[:n], refs[n:2 * n]
        send, recv = refs[2 * n], refs[2 * n + 1]
        token = refs[4 * n + 2]
        loc = refs[4 * n + 3]
        me, peers = _peers()
        own = [pltpu.make_async_copy(ins[i].at[me] if scatter else ins[i], lands[i].at[me], loc.at[i]) for i in range(n)]
        for cp in own:
            cp.start()
        for cp in own:
            cp.wait()
        for k, peer, pidx in peers:
            for i in range(n):
                src = ins[i].at[pidx] if scatter else ins[i]
                pltpu.make_async_remote_copy(src_ref=src, dst_ref=lands[i].at[me], send_sem=send.at[i * 7 + k - 1],
                                             recv_sem=recv.at[i * 7 + k - 1], device_id=peer, device_id_type=MESH_ID).start()
        token[...] = jnp.zeros_like(token)

    land_shapes = [a.shape if scatter else (N_DEV,) + a.shape for a in arrs]
    hbm = lambda a: pltpu.with_memory_space_constraint(a, pltpu.HBM)
    args = [hbm(a) for a in arrs] + [hbm(lax.empty(s, a.dtype)) for s, a in zip(land_shapes, arrs)]
    out_shape = ([pltpu.SemaphoreType.DMA((7 * n,)), pltpu.SemaphoreType.DMA((7 * n,))]
                 + [pltpu.HBM(a.shape, a.dtype) for a in arrs] + [pltpu.HBM(s, a.dtype) for s, a in zip(land_shapes, arrs)]
                 + [_sds((8, LANE), F32)])
    res = pl.pallas_call(
        body, name=name, out_shape=out_shape, in_specs=[HBM_SPEC] * (2 * n),
        out_specs=[SEM_SPEC, SEM_SPEC] + [HBM_SPEC] * (2 * n) + [pl.BlockSpec(memory_space=pltpu.VMEM)],
        input_output_aliases={i: 2 + i for i in range(2 * n)},
        scratch_shapes=[pltpu.SemaphoreType.DMA((n,))],
        compiler_params=pltpu.CompilerParams(has_side_effects=SIDE_EFFECT))(*args)
    return _InFlight(res[0], res[1], list(res[2:2 + n]), list(res[2 + n:2 + 2 * n]), res[2 + 2 * n], scatter)


def _exchange_wait(h, after, name):
    n = len(h.srcs)
    scatter = h.scatter
    after = list(after)

    def body(*refs):
        ins, lands = refs[:n], refs[n:2 * n]
        send, recv = refs[2 * n], refs[2 * n + 1]
        _, peers = _peers()
        for k, peer, pidx in peers:
            for i in range(n):
                src = ins[i].at[pidx] if scatter else ins[i]
                cp = pltpu.make_async_remote_copy(src_ref=src, dst_ref=lands[i].at[pidx], send_sem=send.at[i * 7 + k - 1],
                                                  recv_sem=recv.at[i * 7 + k - 1], device_id=peer, device_id_type=MESH_ID)
                cp.wait_send()
                cp.wait_recv()

    res = pl.pallas_call(
        body, name=name, out_shape=[pltpu.HBM(a.shape, a.dtype) for a in h.srcs + h.lands],
        in_specs=[HBM_SPEC] * (2 * n) + [SEM_SPEC, SEM_SPEC] + [pl.BlockSpec(memory_space=pl.ANY)] * len(after),
        out_specs=[HBM_SPEC] * (2 * n), input_output_aliases={i: i for i in range(2 * n)},
        compiler_params=pltpu.CompilerParams(has_side_effects=SIDE_EFFECT))(*h.srcs, *h.lands, h.send, h.recv, *after)
    return list(res[n:])


def _mm_xw(x, g, out_dtype, name, tm_c=(768, 512, 384, 256, 128, 64, 16), tn_c=(1024, 768, 512, 256, 128), tk_c=(2048, 1024, 768, 512, 256)):
    m, r = x.shape
    nb, r2, cl = g.shape
    assert r == r2
    tm, tn, tk = _pick(m, tm_c), _pick(cl, tn_c), _pick(r, tk_c)
    q, nk = cl // tn, r // tk

    def body(x_ref, g_ref, o_ref, *acc):
        p = lax.dot_general(x_ref[...].astype(BF16), g_ref[...], (((1,), (0,)), ((), ())), preferred_element_type=F32)
        if nk == 1:
            o_ref[...] = p.astype(o_ref.dtype)
        else:
            k = pl.program_id(2)

            @pl.when(k == 0)
            def _():
                acc[0][...] = p

            @pl.when(k > 0)
            def _():
                acc[0][...] += p

            @pl.when(k == nk - 1)
            def _():
                o_ref[...] = acc[0][...].astype(o_ref.dtype)

    return _pcall(
        body, name=name, grid=(m // tm, nb * q, nk),
        in_specs=[pl.BlockSpec((tm, tk), lambda i, j, k: (i, k)), pl.BlockSpec((None, tk, tn), lambda i, j, k: (j // q, k, j % q))],
        out_specs=pl.BlockSpec((tm, tn), lambda i, j, k: (i, j)),
        out_shape=_sds((m, nb * cl), out_dtype),
        scratch=[] if nk == 1 else [pltpu.VMEM((tm, tn), F32)])(x, g)


def _mm_dyw(dy, g, out_dtype, name, init=None, tm_c=(1024, 768, 512, 384, 256, 128), tn_c=(1024, 512, 256, 128), tk_c=(2048, 1536, 1024, 768, 512, 256, 128)):
    m, n = dy.shape
    nb, r, cl = g.shape
    assert n == nb * cl
    tm, tn, tk = _pick(m, tm_c), _pick(r, tn_c), _pick(cl, tk_c)
    q = cl // tk
    nk = nb * q
    has_init = init is not None

    def body(*refs):
        if has_init:
            dy_ref, g_ref, i_ref, o_ref, acc = refs
        else:
            dy_ref, g_ref, o_ref, acc = refs
        k = pl.program_id(2)
        p = lax.dot_general(dy_ref[...].astype(BF16), g_ref[...], (((1,), (1,)), ((), ())), preferred_element_type=F32)

        @pl.when(k == 0)
        def _():
            acc[...] = p + i_ref[...].astype(F32) if has_init else p

        @pl.when(k > 0)
        def _():
            acc[...] += p

        @pl.when(k == nk - 1)
        def _():
            o_ref[...] = acc[...].astype(o_ref.dtype)

    in_specs = [pl.BlockSpec((tm, tk), lambda i, j, k: (i, k)), pl.BlockSpec((None, tn, tk), lambda i, j, k: (k // q, j, k % q))]
    args = [dy, g]
    if has_init:
        in_specs.append(pl.BlockSpec((tm, tn), lambda i, j, k: (i, j)))
        args.append(init)
    return _pcall(body, name=name, grid=(m // tm, r // tn, nk), in_specs=in_specs,
                  out_specs=pl.BlockSpec((tm, tn), lambda i, j, k: (i, j)), out_shape=_sds((m, r), out_dtype),
                  scratch=[pltpu.VMEM((tm, tn), F32)])(*args)


def _mm_xtdy(x, dy, nb, out_dtype, name, tm_c=(1024, 512, 256, 128), tn_c=(768, 512, 256, 128), tk_c=(2304, 2048, 1152, 1024, 768, 512, 256, 128, 16)):
    t, r = x.shape
    t2, n = dy.shape
    assert t == t2 and n % nb == 0
    cl = n // nb
    tm, tn, tk = _pick(r, tm_c), _pick(cl, tn_c), _pick(t, tk_c)
    q, nk = cl // tn, t // tk

    def body(x_ref, dy_ref, o_ref, *acc):
        p = lax.dot_general(x_ref[...].astype(BF16), dy_ref[...].astype(BF16), (((0,), (0,)), ((), ())), preferred_element_type=F32)
        if nk == 1:
            o_ref[...] = p.astype(o_ref.dtype)
        else:
            k = pl.program_id(2)

            @pl.when(k == 0)
            def _():
                acc[0][...] = p

            @pl.when(k > 0)
            def _():
                acc[0][...] += p

            @pl.when(k == nk - 1)
            def _():
                o_ref[...] = acc[0][...].astype(o_ref.dtype)

    return _pcall(
        body, name=name, grid=(r // tm, nb * q, nk),
        in_specs=[pl.BlockSpec((tk, tm), lambda i, j, k: (k, i)), pl.BlockSpec((tk, tn), lambda i, j, k: (k, j))],
        out_specs=pl.BlockSpec((None, tm, tn), lambda i, j, k: (j // q, i, j % q)),
        out_shape=_sds((nb, r, cl), out_dtype),
        scratch=[] if nk == 1 else [pltpu.VMEM((tm, tn), F32)])(x, dy)


def _mm_f32(a, b, name, trans_b=False):
    dims = (((1,), (1,)), ((), ())) if trans_b else (((1,), (0,)), ((), ()))
    n = b.shape[0] if trans_b else b.shape[1]

    def body(a_ref, b_ref, o_ref):
        o_ref[...] = lax.dot_general(a_ref[...], b_ref[...], dims, precision=HIGHEST, preferred_element_type=F32)

    return _pcall(body, name=name, out_shape=_sds((a.shape[0], n), F32))(a, b)


def _tiled(fn, ins, outs, grid, name, acc=()):
    n_in = len(ins)
    grid = tuple(grid) or (1,)
    nd = len(grid)

    def body(*refs):
        vals = fn(*[r[...] for r in refs[:n_in]])
        if not isinstance(vals, (tuple, list)):
            vals = (vals,)
        first = None
        for o, (ref, v) in enumerate(zip(refs[n_in:], vals)):
            if o in acc:
                if first is None:
                    first = pl.program_id(0) == 0
                    for a in range(1, nd):
                        first = jnp.logical_and(first, pl.program_id(a) == 0)

                @pl.when(first)
                def _(ref=ref, v=v):
                    ref[...] = v.astype(ref.dtype)

                @pl.when(jnp.logical_not(first))
                def _(ref=ref, v=v):
                    ref[...] += v.astype(ref.dtype)
            else:
                ref[...] = v.astype(ref.dtype)

    res = _pcall(body, name=name, grid=grid,
                 in_specs=[pl.BlockSpec(b, im) for _, b, im in ins],
                 out_specs=[pl.BlockSpec(b, im) for _, _, b, im in outs],
                 out_shape=[_sds(s, d) for s, d, _, _ in outs])(*[a for a, _, _ in ins])
    return list(res)


def _rows(a, tr):
    return (a, (tr, a.shape[1]), lambda i, *_: (i, 0))


def _const(a):
    nd = a.ndim
    return (a, a.shape, lambda *_: (0,) * nd)


def _cast_bf16(w, name):
    r, c = w.shape
    tr = _pick(r, (256, 128, 64, 16))
    return _tiled(lambda v: v, [_rows(w, tr)], [((r, c), BF16, (tr, c), lambda i: (i, 0))], (r // tr,), name)[0]


def _f_norm_mod(x, g, sh, sc):
    y = x * lax.rsqrt(jnp.mean(x * x, axis=-1, keepdims=True) + EPS) * g
    return y * (1.0 + sc) + sh


def _rope_partner_impl(y):
    nf = HEAD_DIM // 4
    lane = lax.broadcasted_iota(jnp.int32, y.shape, 1)
    return jnp.where(lane % (2 * nf) < nf, pltpu.roll(y, HEAD_DIM - nf, 1), pltpu.roll(y, nf, 1))


_rope_partner = jax.custom_vjp(_rope_partner_impl)
_rope_partner.defvjp(lambda y: (_rope_partner_impl(y), None), lambda _, ct: (_rope_partner_impl(ct),))


def _f_qk(t, g, cos, sin):
    y = t * lax.rsqrt(jnp.mean(t * t, axis=-1, keepdims=True) + EPS) * g
    return y * cos + _rope_partner(y) * sin


def _f_readout(of, ob, gate, g):
    o = of + ob
    on = o * lax.rsqrt(jnp.mean(o * o, axis=-1, keepdims=True) + EPS) * g
    return on * (gate * jax.nn.sigmoid(gate))


def _f_merge(ga, gb, pa, pb):
    return jax.nn.sigmoid(ga) * pa + jax.nn.sigmoid(gb) * pb


def _f_resid(x, gate, m):
    return x + gate * m


def _norm_mod_fwd(xcat, g, mods, n_ctx_tiles, tr):
    tt, d = xcat.shape
    which = lambda i: (jnp.where(i >= n_ctx_tiles, 1, 0), 0, 0, 0)

    def fn(x, gg, md):
        return _f_norm_mod(x, gg, md[0], md[1])

    return _tiled(fn, [_rows(xcat, tr), _const(g), (mods, (None, 2, 1, d), which)],
                  [((tt, d), BF16, (tr, d), lambda i: (i, 0))], (tt // tr,), "norm_mod_fwd")[0]


def _norm_mod_bwd(xcat, g, mods, dh, extra, n_ctx_tiles, tr, name):
    tt, d = xcat.shape
    nt = tt // tr
    has_extra = extra is not None

    def body(*refs):
        if has_extra:
            x_ref, g_ref, m_ref, dh_ref, e_ref, dx_ref, dg_ref, dm_ref = refs
        else:
            x_ref, g_ref, m_ref, dh_ref, dx_ref, dg_ref, dm_ref = refs
        i = pl.program_id(0)
        md = m_ref[...]
        _, vjp = jax.vjp(_f_norm_mod, x_ref[...], g_ref[...], md[0], md[1])
        dx, dg, dsh, dsc = vjp(dh_ref[...].astype(F32))
        dx_ref[...] = dx + e_ref[...] if has_extra else dx

        @pl.when(i == 0)
        def _():
            dg_ref[...] = dg

        @pl.when(i > 0)
        def _():
            dg_ref[...] += dg

        fresh = jnp.logical_or(i == 0, i == n_ctx_tiles)

        @pl.when(fresh)
        def _():
            dm_ref[0] = dsh
            dm_ref[1] = dsc

        @pl.when(jnp.logical_not(fresh))
        def _():
            dm_ref[0] += dsh
            dm_ref[1] += dsc

    which = lambda i: (jnp.where(i >= n_ctx_tiles, 1, 0), 0, 0, 0)
    row = pl.BlockSpec((tr, d), lambda i: (i, 0))
    in_specs = [row, pl.BlockSpec((1, d), lambda i: (0, 0)), pl.BlockSpec((None, 2, 1, d), which), row]
    args = [xcat, g, mods, dh]
    if has_extra:
        in_specs.append(row)
        args.append(extra)
    return _pcall(body, name=name, grid=(nt,), in_specs=in_specs,
                  out_specs=[row, pl.BlockSpec((1, d), lambda i: (0, 0)), pl.BlockSpec((None, 2, 1, d), which)],
                  out_shape=[_sds((tt, d), F32), _sds((1, d), F32), _sds((2, 2, 1, d), F32)])(*args)


def _hgrn_tri(reverse):
    t = np.arange(HGRN_BLOCK)
    tri = (t[None, :] >= t[:, None]) if reverse else (t[None, :] <= t[:, None])
    tri = tri.astype(np.float32)
    return jnp.asarray(tri), jnp.asarray(tri.T.copy())


def _hgrn_rowblock(n, n_ctx_blocks, n_blocks, reverse):
    if not reverse:
        return n
    return jnp.where(n < n_ctx_blocks, n_ctx_blocks - 1 - n, n_blocks - 1 - n + n_ctx_blocks)


def _hgrn_gates(fl, lb):
    sg = jax.nn.sigmoid(fl)
    f = lb + (1.0 - lb) * sg
    return sg, f, jnp.log(f), 1.0 - f


def _hgrn_intra_mask(reverse):
    tio = lax.broadcasted_iota(jnp.int32, (HGRN_BLOCK, HEAD_DIM), 0)
    return (lambda s: tio <= s) if reverse else (lambda s: tio >= s)


class _Halves:
    def __init__(self, reverse):
        self.reverse = reverse
        self.h = HGRN_BLOCK // 2
        tio = lax.broadcasted_iota(jnp.int32, (self.h, HEAD_DIM), 0)
        self.tio = tio if reverse else tio + self.h

    def is_half(self, s):
        return s < self.h if self.reverse else s >= self.h

    def rows(self, a):
        return a[:self.h] if self.reverse else a[self.h:]

    def mask(self, s):
        return self.tio <= s if self.reverse else self.tio >= s

    def widen(self, full, half):
        z = jnp.zeros_like(half)
        return full + jnp.concatenate([half, z] if self.reverse else [z, half], axis=0)


def _unrolled(nblk, u, fn, init):
    assert nblk % u == 0

    def trip(b, c):
        for j in range(u):
            c = fn(b * u + j, j, c)
        return c

    return lax.fori_loop(0, nblk // u, trip, init)


HGRN_UNROLL_FWD = (4, 8)
HGRN_UNROLL_BWD = (8, 4)


def _hgrn_fwd(p, lb, seg_f, reverse, n_ctx_rows):
    tt = p.shape[0]
    hb = HGRN_BLOCK
    nblk, nctx = tt // hb, n_ctx_rows // hb
    tri, _ = _hgrn_tri(reverse)
    u1, u3 = HGRN_UNROLL_FWD

    def body(q_ref, f_ref, v_ref, lb_ref, tri_ref, o_ref, st_all, dec_all, cum_out, qe_buf, cum_blk, k_blk, v_blk):
        mask = _hgrn_intra_mask(reverse)
        hv = _Halves(reverse)
        lbv = lb_ref[...]

        def phase1(n, slot, c):
            r0 = pl.multiple_of(_hgrn_rowblock(n, nctx, nblk, reverse) * hb, hb)
            q, v = q_ref[pl.ds(r0, hb), :], v_ref[pl.ds(r0, hb), :]
            _, f, g, k = _hgrn_gates(f_ref[pl.ds(r0, hb), :], lbv)
            cum = jnp.dot(tri_ref[...], g, precision=HIGHEST, preferred_element_type=F32)
            tot = jnp.sum(g, axis=0, keepdims=True)
            cum_blk[slot] = cum
            k_blk[slot] = k
            v_blk[slot] = v
            oi = jnp.zeros((hb, HEAD_DIM), F32)
            oi_h = jnp.zeros((hb // 2, HEAD_DIM), F32)
            q_h, cum_h = hv.rows(q), hv.rows(cum)
            for s in range(hb):
                if hv.is_half(s):
                    e = jnp.where(hv.mask(s), jnp.exp(jnp.minimum(cum_h - cum_blk[slot, s:s + 1, :], 0.0)), 0.0)
                    a_s = jnp.sum(q_h * e * k_blk[slot, s:s + 1, :], axis=-1, keepdims=True)
                    oi_h = oi_h + a_s * v_blk[slot, s:s + 1, :]
                else:
                    e = jnp.where(mask(s), jnp.exp(jnp.minimum(cum - cum_blk[slot, s:s + 1, :], 0.0)), 0.0)
                    a_s = jnp.sum(q * e * k_blk[slot, s:s + 1, :], axis=-1, keepdims=True)
                    oi = oi + a_s * v_blk[slot, s:s + 1, :]
            o_ref[pl.ds(r0, hb), :] = hv.widen(oi, oi_h)
            cum_out[pl.ds(r0, hb), :] = cum
            qe_buf[pl.ds(r0, hb), :] = q * jnp.exp(cum)
            kl = k * jnp.exp(tot - cum)
            st_all[n] = lax.dot_general(v.astype(BF16), kl.astype(BF16), (((0,), (0,)), ((), ())), preferred_element_type=F32)
            dec_all[pl.ds(n, 1), :] = jnp.exp(tot)
            return c

        _unrolled(nblk, u1, phase1, 0)

        def phase2(n, st):
            kv = st_all[n]
            st_all[n] = st
            return st * dec_all[pl.ds(n, 1), :] + kv

        lax.fori_loop(0, nblk, phase2, jnp.zeros((HEAD_DIM, HEAD_DIM), F32))

        def phase3(n, slot, c):
            r0 = pl.multiple_of(_hgrn_rowblock(n, nctx, nblk, reverse) * hb, hb)
            o_ref[pl.ds(r0, hb), :] += lax.dot_general(qe_buf[pl.ds(r0, hb), :].astype(BF16), st_all[n].astype(BF16),
                                                        (((1,), (1,)), ((), ())), preferred_element_type=F32)
            return c

        _unrolled(nblk, u3, phase3, 0)

    col = lambda seg: pl.BlockSpec((tt, HEAD_DIM), lambda h, seg=seg: (0, seg * N_HEADS + h))
    blk = pltpu.VMEM((u1, hb, HEAD_DIM), F32)
    head = pl.BlockSpec((tt, HEAD_DIM), lambda h: (0, h))
    return _pcall(
        body, name="hgrn_fwd_rev" if reverse else "hgrn_fwd", grid=(N_HEADS,),
        in_specs=[col(0), col(seg_f), col(3), pl.BlockSpec((1, HEAD_DIM), lambda h: (0, h)), pl.BlockSpec((hb, hb), lambda h: (0, 0))],
        out_specs=[head, pl.BlockSpec((None, nblk, HEAD_DIM, HEAD_DIM), lambda h: (h, 0, 0, 0)),
                   pl.BlockSpec((None, nblk, HEAD_DIM), lambda h: (h, 0, 0)), head],
        out_shape=[_sds((tt, N_HEADS * HEAD_DIM), F32), _sds((N_HEADS, nblk, HEAD_DIM, HEAD_DIM), F32),
                   _sds((N_HEADS, nblk, HEAD_DIM), F32), _sds((tt, N_HEADS * HEAD_DIM), F32)],
        scratch=[pltpu.VMEM((tt, HEAD_DIM), F32), blk, blk, blk])(p, p, p, lb, tri)


def _hgrn_bwd(p, lb, do, saved, seg_f, reverse, n_ctx_rows, prev):
    tt = p.shape[0]
    hb = HGRN_BLOCK
    nblk, nctx = tt // hb, n_ctx_rows // hb
    _, tri_t = _hgrn_tri(reverse)
    last_row = 0 if reverse else hb - 1
    has_prev = prev is not None
    u1, u3 = HGRN_UNROLL_BWD

    def body(*refs):
        q_ref, f_ref, v_ref, lb_ref, trit_ref, do_ref, st_all, dec_all, cum_buf = refs[:9]
        refs = refs[9:]
        if has_prev:
            pq_ref, pv_ref = refs[:2]
            refs = refs[2:]
        dq_ref, dfl_ref, dv_ref, dlb_ref, dd_all, cum_blk, k_blk, v_blk, dk_blk, dv_blk = refs
        mask = _hgrn_intra_mask(reverse)
        hv = _Halves(reverse)
        lbv = lb_ref[...]
        tio = lax.broadcasted_iota(jnp.int32, (hb, HEAD_DIM), 0)

        def rows_of(n):
            rb = _hgrn_rowblock(n, nctx, nblk, reverse)
            return rb, pl.multiple_of(rb * hb, hb)

        def load_do(rb):
            lat0 = pl.multiple_of(jnp.maximum(rb - nctx, 0) * hb, hb)
            return jnp.where(rb >= nctx, do_ref[pl.ds(lat0, hb), :], 0.0)

        def phase1(n, slot, c):
            rb, r0 = rows_of(n)
            qe = q_ref[pl.ds(r0, hb), :] * jnp.exp(cum_buf[pl.ds(r0, hb), :])
            dd_all[n] = lax.dot_general(load_do(rb).astype(BF16), qe.astype(BF16), (((0,), (0,)), ((), ())), preferred_element_type=F32)
            return c

        _unrolled(nblk, u1, phase1, 0)

        def phase2r(i, dst):
            n = nblk - 1 - i
            u = dd_all[n]
            dd_all[n] = dst
            return u + dst * dec_all[pl.ds(n, 1), :]

        lax.fori_loop(0, nblk, phase2r, jnp.zeros((HEAD_DIM, HEAD_DIM), F32))

        def phase3(n, slot, dlb):
            rb, r0 = rows_of(n)
            q, v = q_ref[pl.ds(r0, hb), :], v_ref[pl.ds(r0, hb), :]
            sg, f, g, k = _hgrn_gates(f_ref[pl.ds(r0, hb), :], lbv)
            cum = cum_buf[pl.ds(r0, hb), :]
            tot = jnp.sum(g, axis=0, keepdims=True)
            dob = load_do(rb)
            st, dst = st_all[n], dd_all[n]
            e_cum = jnp.exp(cum)
            e_rest = jnp.exp(tot - cum)
            dq = jnp.dot(dob.astype(BF16), st.astype(BF16), preferred_element_type=F32) * e_cum
            dk_inter = jnp.dot(v.astype(BF16), dst.astype(BF16), preferred_element_type=F32) * e_rest
            dv = lax.dot_general((k * e_rest).astype(BF16), dst.astype(BF16), (((1,), (1,)), ((), ())), preferred_element_type=F32)
            cum_blk[slot] = cum
            k_blk[slot] = k
            v_blk[slot] = v
            dq_h = jnp.zeros((hb // 2, HEAD_DIM), F32)
            q_h, cum_h, dob_h = hv.rows(q), hv.rows(cum), hv.rows(dob)
            for s in range(hb):
                half = hv.is_half(s)
                qs, cs, ds = (q_h, cum_h, dob_h) if half else (q, cum, dob)
                e = jnp.where(hv.mask(s) if half else mask(s), jnp.exp(jnp.minimum(cs - cum_blk[slot, s:s + 1, :], 0.0)), 0.0)
                a_s = jnp.sum(qs * e * k_blk[slot, s:s + 1, :], axis=-1, keepdims=True)
                da_s = jnp.sum(ds * v_blk[slot, s:s + 1, :], axis=-1, keepdims=True)
                gs = da_s * e
                if half:
                    dq_h = dq_h + gs * k_blk[slot, s:s + 1, :]
                else:
                    dq = dq + gs * k_blk[slot, s:s + 1, :]
                dk_blk[slot, s:s + 1, :] = jnp.sum(gs * qs, axis=0, keepdims=True)
                dv_blk[slot, s:s + 1, :] = jnp.sum(a_s * ds, axis=0, keepdims=True)
            dq = hv.widen(dq, dq_h)
            dk = dk_inter + dk_blk[slot]
            dv = dv + dv_blk[slot]
            d_tot = jnp.sum(k * dk_inter, axis=0, keepdims=True) + jnp.exp(tot) * jnp.sum(dst * st, axis=0, keepdims=True)
            dcum = q * dq - k * dk + jnp.where(tio == last_row, d_tot, 0.0)
            dg = jnp.dot(trit_ref[...], dcum, precision=HIGHEST, preferred_element_type=F32)
            df = dg / f - dk
            if has_prev:
                dq = dq + pq_ref[pl.ds(r0, hb), :].astype(F32)
                dv = dv + pv_ref[pl.ds(r0, hb), :].astype(F32)
            dq_ref[pl.ds(r0, hb), :] = dq.astype(dq_ref.dtype)
            dv_ref[pl.ds(r0, hb), :] = dv.astype(dv_ref.dtype)
            dfl_ref[pl.ds(r0, hb), :] = (df * (1.0 - lbv) * sg * (1.0 - sg)).astype(dfl_ref.dtype)
            return dlb + jnp.sum(df * (1.0 - sg), axis=0, keepdims=True)

        dlb_ref[...] = _unrolled(nblk, u3, phase3, jnp.zeros((1, HEAD_DIM), F32))

    col = lambda seg: pl.BlockSpec((tt, HEAD_DIM), lambda h, seg=seg: (0, seg * N_HEADS + h))
    head = pl.BlockSpec((tt, HEAD_DIM), lambda h: (0, h))
    lbs = pl.BlockSpec((1, HEAD_DIM), lambda h: (0, h))
    tris = pl.BlockSpec((hb, hb), lambda h: (0, 0))
    in_specs = [col(0), col(seg_f), col(3), lbs, tris, pl.BlockSpec((do.shape[0], HEAD_DIM), lambda h: (0, h)),
                pl.BlockSpec((None, nblk, HEAD_DIM, HEAD_DIM), lambda h: (h, 0, 0, 0)), pl.BlockSpec((None, nblk, HEAD_DIM), lambda h: (h, 0, 0)), head]
    args = [p, p, p, lb, tri_t, do, *saved]
    mid = F32 if not has_prev else BF16
    if has_prev:
        in_specs += [head, head]
        args += list(prev)
    w = N_HEADS * HEAD_DIM
    blk = pltpu.VMEM((u3, hb, HEAD_DIM), F32)
    return _pcall(
        body, name="hgrn_bwd_rev" if reverse else "hgrn_bwd", grid=(N_HEADS,), in_specs=in_specs,
        out_specs=[head, head, head, lbs],
        out_shape=[_sds((tt, w), mid), _sds((tt, w), BF16), _sds((tt, w), mid), _sds((1, w), F32)],
        scratch=[pltpu.VMEM((nblk, HEAD_DIM, HEAD_DIM), F32), blk, blk, blk, blk, blk])(*args)


NA_HEADS_PER_STEP = 4


def _na_geometry(rows):
    r = pl.program_id(1)
    rs = jnp.clip(r - WIN_R // 2, 0, rows - WIN_R)
    return r, rs, r - rs


def _na_scores(q, kb, kc, bias):
    scale = HEAD_DIM ** -0.5
    nt = (((1,), (1,)), ((), ()))
    sb = lax.dot_general(q, kb, nt, preferred_element_type=F32) * scale + bias
    sc = lax.dot_general(q, kc, nt, preferred_element_type=F32) * scale
    m = jnp.maximum(jnp.max(sb, axis=-1, keepdims=True), jnp.max(sc, axis=-1, keepdims=True))
    pb, pc = jnp.exp(sb - m), jnp.exp(sc - m)
    inv = 1.0 / (jnp.sum(pb, axis=-1, keepdims=True) + jnp.sum(pc, axis=-1, keepdims=True))
    return pb * inv, pc * inv


def _na_fwd(qn, kall, p, seg_v, bias, n_ctx_rows):
    t, tt = qn.shape[0], kall.shape[0]
    rows = t // GRID_W
    nband = WIN_R * GRID_W

    nh = NA_HEADS_PER_STEP
    wide = nh * HEAD_DIM

    def body(q_ref, k_ref, v_ref, b_ref, o_ref):
        r, rs, _ = _na_geometry(rows)
        k0 = pl.multiple_of(n_ctx_rows + rs * GRID_W, GRID_W)
        for j in range(nh):
            sl = slice(j * HEAD_DIM, (j + 1) * HEAD_DIM)
            q = q_ref[:, sl]
            kb, kc = k_ref[pl.ds(k0, nband), sl], k_ref[pl.ds(0, n_ctx_rows), sl]
            vb, vc = v_ref[pl.ds(k0, nband), sl].astype(BF16), v_ref[pl.ds(0, n_ctx_rows), sl].astype(BF16)
            pb, pc = _na_scores(q, kb, kc, b_ref[j])
            o = jnp.dot(pb.astype(BF16), vb, preferred_element_type=F32) + jnp.dot(pc.astype(BF16), vc, preferred_element_type=F32)
            o_ref[:, sl] = o.astype(o_ref.dtype)

    variant = lambda h, r: (h, r - jnp.clip(r - WIN_R // 2, 0, rows - WIN_R), 0, 0)
    return _pcall(
        body, name="na_fwd", grid=(N_HEADS // nh, rows),
        in_specs=[pl.BlockSpec((GRID_W, wide), lambda h, r: (r, h)),
                  pl.BlockSpec((tt, wide), lambda h, r: (0, h)),
                  pl.BlockSpec((tt, wide), lambda h, r: (0, seg_v * (N_HEADS // nh) + h)),
                  pl.BlockSpec((nh, None, GRID_W, nband), variant)],
        out_specs=pl.BlockSpec((GRID_W, wide), lambda h, r: (r, h)),
        out_shape=_sds((t, N_HEADS * HEAD_DIM), BF16))(qn, kall, p, bias)


def _na_bwd(qn, kall, p, seg_v, bias, do, n_ctx_rows):
    t, tt = qn.shape[0], kall.shape[0]
    rows = t // GRID_W
    nband = WIN_R * GRID_W
    scale = HEAD_DIM ** -0.5
    tn = (((0,), (0,)), ((), ()))
    nt = (((1,), (1,)), ((), ()))

    nh = NA_HEADS_PER_STEP
    wide = nh * HEAD_DIM

    def body(q_ref, k_ref, v_ref, b_ref, do_ref, dq_ref, dk_ref, dv_ref, db_ref, dv_acc):
        r, rs, var = _na_geometry(rows)
        k0 = pl.multiple_of(n_ctx_rows + rs * GRID_W, GRID_W)
        fresh = jnp.logical_or(r <= WIN_R // 2, r > rows - WIN_R // 2)

        @pl.when(r == 0)
        def _():
            dk_ref[...] = jnp.zeros_like(dk_ref)
            dv_acc[...] = jnp.zeros_like(dv_acc)

        for j in range(nh):
            sl = slice(j * HEAD_DIM, (j + 1) * HEAD_DIM)
            q = q_ref[:, sl]
            kb, kc = k_ref[pl.ds(k0, nband), sl], k_ref[pl.ds(0, n_ctx_rows), sl]
            vb, vc = v_ref[pl.ds(k0, nband), sl].astype(BF16), v_ref[pl.ds(0, n_ctx_rows), sl].astype(BF16)
            pb, pc = _na_scores(q, kb, kc, b_ref[j])
            dof = do_ref[:, sl].astype(F32)
            dob = dof.astype(BF16)
            o = jnp.dot(pb.astype(BF16), vb, preferred_element_type=F32) + jnp.dot(pc.astype(BF16), vc, preferred_element_type=F32)
            delta = jnp.sum(dof * o, axis=-1, keepdims=True)
            dsb = pb * (lax.dot_general(dob, vb, nt, preferred_element_type=F32) - delta)
            dsc = pc * (lax.dot_general(dob, vc, nt, preferred_element_type=F32) - delta)
            dsb16, dsc16 = dsb.astype(BF16), dsc.astype(BF16)
            dq_ref[:, sl] = (jnp.dot(dsb16, kb, preferred_element_type=F32) + jnp.dot(dsc16, kc, preferred_element_type=F32)) * scale
            dk_ref[pl.ds(k0, nband), sl] += lax.dot_general(dsb16, q, tn, preferred_element_type=F32) * scale
            dk_ref[pl.ds(0, n_ctx_rows), sl] += lax.dot_general(dsc16, q, tn, preferred_element_type=F32) * scale
            dv_acc[pl.ds(k0, nband), sl] += lax.dot_general(pb.astype(BF16), dob, tn, preferred_element_type=F32)
            dv_acc[pl.ds(0, n_ctx_rows), sl] += lax.dot_general(pc.astype(BF16), dob, tn, preferred_element_type=F32)

            @pl.when(fresh)
            def _(j=j, dsb=dsb):
                db_ref[j] = dsb

            @pl.when(jnp.logical_not(fresh))
            def _(j=j, dsb=dsb):
                db_ref[j] += dsb

        @pl.when(r == rows - 1)
        def _():
            dv_ref[...] = dv_acc[...].astype(dv_ref.dtype)

    variant = lambda h, r: (h, r - jnp.clip(r - WIN_R // 2, 0, rows - WIN_R), 0, 0)
    head_all = pl.BlockSpec((tt, wide), lambda h, r: (0, h))
    qspec = pl.BlockSpec((GRID_W, wide), lambda h, r: (r, h))
    w = N_HEADS * HEAD_DIM
    return _pcall(
        body, name="na_bwd", grid=(N_HEADS // nh, rows),
        in_specs=[qspec, head_all, pl.BlockSpec((tt, wide), lambda h, r: (0, seg_v * (N_HEADS // nh) + h)),
                  pl.BlockSpec((nh, None, GRID_W, nband), variant), qspec],
        out_specs=[qspec, head_all, head_all, pl.BlockSpec((nh, None, GRID_W, nband), variant)],
        out_shape=[_sds((t, w), F32), _sds((tt, w), F32), _sds((tt, w), BF16), _sds((N_HEADS, WIN_R, GRID_W, nband), F32)],
        scratch=[pltpu.VMEM((tt, wide), F32)])(qn, kall, p, bias, do)


def _na_tables(t, n_ctx_rows):
    half, nf = HEAD_DIM // 2, HEAD_DIM // 4
    pos = np.arange(t)
    lane = np.arange(HEAD_DIM)
    inv = ROPE_THETA ** (-(np.arange(nf, dtype=np.float32)) / nf)
    which = np.where(lane < half, pos[:, None] // GRID_W, pos[:, None] % GRID_W).astype(np.float32)
    ang = which * inv[lane % nf][None, :]
    first = (lane % half) < nf
    cos = np.concatenate([np.ones((n_ctx_rows, HEAD_DIM), np.float32), np.cos(ang).astype(np.float32)])
    sin = np.concatenate([np.zeros((n_ctx_rows, HEAD_DIM), np.float32), np.where(first[None, :], -np.sin(ang), np.sin(ang)).astype(np.float32)])
    w = np.arange(GRID_W)
    dc = np.clip(w[None, :] - w[:, None], -(WIN_C - 1), WIN_C - 1) + WIN_C - 1
    onehot = np.zeros((32, GRID_W * GRID_W), np.float32)
    onehot[dc.reshape(-1), np.arange(GRID_W * GRID_W)] = 1.0
    cs = np.clip(w - WIN_C // 2, 0, GRID_W - WIN_C)
    col_in = (w[None, :] >= cs[:, None]) & (w[None, :] < cs[:, None] + WIN_C)
    onehot *= col_in.reshape(1, -1)
    neg = np.where(col_in, 0.0, NEG_BIG).astype(np.float32)
    return jnp.asarray(cos), jnp.asarray(sin), jnp.asarray(onehot), jnp.asarray(neg)


def _bias_slabs(rel_bias, onehot, neg):
    nr = 2 * WIN_R - 1
    rb = jnp.pad(rel_bias.reshape(N_HEADS * nr, 2 * WIN_C - 1), ((0, 0), (0, 1)))
    spread = _mm_f32(rb, onehot, "bias_spread").reshape(N_HEADS, nr, GRID_W, GRID_W)
    slabs = [spread[:, WIN_R - 1 - v:2 * WIN_R - 1 - v] for v in range(WIN_R)]
    b = jnp.stack(slabs, axis=1) + neg[None, None, None]
    return b.transpose(0, 1, 3, 2, 4).reshape(N_HEADS, WIN_R, GRID_W, WIN_R * GRID_W)


def _bias_grad(dbias, onehot):
    nr = 2 * WIN_R - 1
    d = dbias.reshape(N_HEADS, WIN_R, GRID_W, WIN_R, GRID_W).transpose(0, 1, 3, 2, 4)
    tot = jnp.zeros((N_HEADS, nr, GRID_W, GRID_W), F32)
    for v in range(WIN_R):
        tot = tot + jnp.pad(d[:, v], ((0, 0), (WIN_R - 1 - v, v), (0, 0), (0, 0)))
    g = _mm_f32(tot.reshape(N_HEADS * nr, GRID_W * GRID_W), onehot, "bias_grad", trans_b=True)
    return g[:, :2 * WIN_C - 1].reshape(1, N_HEADS, nr, 2 * WIN_C - 1)


def _shift_rows(u, up):
    n = u.shape[0]
    tio = lax.broadcasted_iota(jnp.int32, u.shape, 0)
    if up:
        return jnp.where(tio == n - 1, 0.0, pltpu.roll(u, n - 1, 0))
    return jnp.where(tio == 0, 0.0, pltpu.roll(u, 1, 0))


def _conv3(u, w_ref, b_ref):
    um, up = _shift_rows(u, False), _shift_rows(u, True)
    return um, up, um * w_ref[0:1, :] + u * w_ref[1:2, :] + up * w_ref[2:3, :] + b_ref[...]


def _ffn_act_fwd(u0, t3, cw, cb):
    t, n = u0.shape
    tc = _pick(n, (256, 128))

    def body(u_ref, t_ref, w_ref, b_ref, a_ref):
        _, _, uc = _conv3(u_ref[...], w_ref, b_ref)
        a_ref[...] = (uc * jax.nn.sigmoid(uc) * t_ref[...]).astype(a_ref.dtype)

    col = lambda rows_: pl.BlockSpec((rows_, tc), lambda j: (0, j))
    return _pcall(body, name="ffn_act_fwd", grid=(n // tc,), in_specs=[col(t), col(t), col(8), col(1)], out_specs=col(t),
                  out_shape=_sds((t, n), BF16))(u0, t3, cw, cb)


def _ffn_act_bwd(u0, t3, cw, cb, da):
    t, n = u0.shape
    tc = _pick(n, (256, 128))

    def body(u_ref, t_ref, w_ref, b_ref, da_ref, du_ref, dt_ref, dw_ref, db_ref):
        u = u_ref[...]
        um, up, uc = _conv3(u, w_ref, b_ref)
        sg = jax.nn.sigmoid(uc)
        dav = da_ref[...].astype(F32)
        dt_ref[...] = (dav * uc * sg).astype(dt_ref.dtype)
        duc = dav * t_ref[...] * sg * (1.0 + uc * (1.0 - sg))
        du = _shift_rows(duc, True) * w_ref[0:1, :] + duc * w_ref[1:2, :] + _shift_rows(duc, False) * w_ref[2:3, :]
        du_ref[...] = du.astype(du_ref.dtype)
        dw_ref[...] = jnp.zeros_like(dw_ref)
        dw_ref[0:1, :] = jnp.sum(duc * um, axis=0, keepdims=True)
        dw_ref[1:2, :] = jnp.sum(duc * u, axis=0, keepdims=True)
        dw_ref[2:3, :] = jnp.sum(duc * up, axis=0, keepdims=True)
        db_ref[...] = jnp.sum(duc, axis=0, keepdims=True)

    col = lambda rows_: pl.BlockSpec((rows_, tc), lambda j: (0, j))
    return _pcall(body, name="ffn_act_bwd", grid=(n // tc,), in_specs=[col(t), col(t), col(8), col(1), col(t)],
                  out_specs=[col(t), col(t), col(8), col(1)],
                  out_shape=[_sds((t, n), BF16), _sds((t, n), BF16), _sds((8, n), F32), _sds((1, n), F32)])(u0, t3, cw, cb, da)


def _adam_math(g, w, m, v):
    m2 = ADAM_B1 * m + (1.0 - ADAM_B1) * g
    v2 = ADAM_B2 * v + (1.0 - ADAM_B2) * (g * g)
    m_hat = m2 / (1.0 - ADAM_B1 ** ADAM_STEP)
    v_hat = v2 / (1.0 - ADAM_B2 ** ADAM_STEP)
    return -ADAM_LR * (m_hat / (jnp.sqrt(v_hat) + ADAM_EPS) + ADAM_WD * w), m2, v2


def _adam_big(parts, w, m, v, name):
    r, c = w.shape
    npart, _, cp = parts.shape
    tr = _pick(r, (64, 32, 16, 8))

    def body(p_ref, w_ref, m_ref, v_ref, g_ref, d_ref, m2_ref, v2_ref):
        g = p_ref[0, :, 0:c].astype(F32)
        for i in range(1, npart):
            g = g + p_ref[i, :, 0:c].astype(F32)
        d, m2, v2 = _adam_math(g, w_ref[...], m_ref[...], v_ref[...])
        g_ref[...] = g
        d_ref[...] = d
        m2_ref[...] = m2
        v2_ref[...] = v2

    row = pl.BlockSpec((tr, c), lambda i: (i, 0))
    return _pcall(body, name=name, grid=(r // tr,),
                  in_specs=[pl.BlockSpec((npart, tr, cp), lambda i: (0, i, 0)), row, row, row],
                  out_specs=[row] * 4, out_shape=[_sds((r, c), F32)] * 4)(parts, w, m, v)


def _adam_small(g, w, m, v):
    def body(g_ref, w_ref, m_ref, v_ref, d_ref, m2_ref, v2_ref):
        d_ref[...], m2_ref[...], v2_ref[...] = _adam_math(g_ref[...], w_ref[...], m_ref[...], v_ref[...])

    return _pcall(body, name="adam_small", out_shape=[_sds(g.shape, F32)] * 3)(g, w, m, v)


def _sum_parts(parts, name):
    def body(p_ref, o_ref):
        s = p_ref[0]
        for i in range(1, N_DEV):
            s = s + p_ref[i]
        o_ref[...] = s

    return _pcall(body, name=name, out_shape=_sds(parts.shape[1:], F32))(parts)


class _Pack:
    def __init__(self, shapes):
        self.shapes = shapes
        self.sizes = [int(np.prod(s)) for s in shapes]
        self.padded = [-(-n // (8 * LANE)) * 8 * LANE for n in self.sizes]
        self.offs = np.concatenate([[0], np.cumsum(self.padded)]).tolist()

    def pack(self, arrs):
        flat = [jnp.pad(a.reshape(-1).astype(F32), (0, p - n)) for a, n, p in zip(arrs, self.sizes, self.padded)]
        return jnp.concatenate(flat).reshape(-1, LANE)

    def unpack(self, slab):
        flat = slab.reshape(-1)
        return [flat[o:o + n].reshape(s) for o, n, s in zip(self.offs, self.sizes, self.shapes)]


def kernel(x, c, ctx, c_ctx, ada_w, ada_b, norm1_g, norm2_g, w_in, hgrn_lb_logits, hgrn_norm_g, na_q_norm_g, na_k_norm_g, na_rel_bias, w_branch_a, w_branch_b, w_out, ffn_w1, ffn_w3, ffn_conv_w, ffn_conv_b, ffn_w2, loss_target, m_c_ctx, m_ada_w, m_ada_b, m_norm1_g, m_norm2_g, m_w_in, m_hgrn_lb_logits, m_hgrn_norm_g, m_na_q_norm_g, m_na_k_norm_g, m_na_rel_bias, m_w_branch_a, m_w_branch_b, m_w_out, m_ffn_w1, m_ffn_w3, m_ffn_conv_w, m_ffn_conv_b, m_ffn_w2, v_c_ctx, v_ada_w, v_ada_b, v_norm1_g, v_norm2_g, v_w_in, v_hgrn_lb_logits, v_hgrn_norm_g, v_na_q_norm_g, v_na_k_norm_g, v_na_rel_bias, v_w_branch_a, v_w_branch_b, v_w_out, v_ffn_w1, v_ffn_w3, v_ffn_conv_w, v_ffn_conv_b, v_ffn_w2):
    t, d = x.shape[1], x.shape[2]
    n_ctx = ctx.shape[1]
    tt = n_ctx + t
    hw = N_HEADS * HEAD_DIM
    ci = w_in.shape[2]
    ca = ada_w.shape[2]
    ff_l = ffn_w1.shape[2]
    ff_p = -(-ff_l // LANE) * LANE
    rows = t // GRID_W
    assert rows >= WIN_R and t % GRID_W == 0 and n_ctx % GRID_W == 0 and ci % LANE == 0 and d % LANE == 0
    me = 4 * lax.axis_index("x") + 2 * lax.axis_index("y") + lax.axis_index("c")
    tr = _pick(n_ctx, (256, 128, 64))
    n_ctx_tiles = n_ctx // tr

    pad_c = lambda w: jnp.pad(w, ((0, 0), (0, ff_p - ff_l)))
    small_in = [c, hgrn_lb_logits.reshape(4, HEAD_DIM), jnp.pad(ffn_conv_w[0], ((0, 5), (0, ff_p - ff_l)))]
    c_all, lb_parts, cw_all = _exchange(small_in, "gather_params", scatter=False)
    ff = N_DEV * ff_p

    cc = jnp.concatenate([c_all.reshape(N_DEV, d), jnp.broadcast_to(c_ctx[None, :], (N_DEV, d))], axis=0)
    act = _tiled(lambda v: v * jax.nn.sigmoid(v), [_const(cc)], [(cc.shape, BF16, cc.shape, lambda *_: (0, 0))], (), "silu_c")[0]
    ada16 = _cast_bf16(ada_w[0], "cast_ada")
    mod_cols = _mm_xw(act, ada16.reshape(1, d, ca), F32, "ada_fwd")
    mod_all = _exchange([mod_cols], "gather_mod", scatter=False)[0]

    with _after(mod_all):
        w_in16 = _cast_bf16(w_in[0], "cast_w_in")
    g_in, = _gather_sc([w_in16], "gather_w_in", 1)
    rest16 = []
    for w_, nm in ((w_branch_a[0], "cast_w_a"), (w_branch_b[0], "cast_w_b"), (w_out[0], "cast_w_out"), (pad_c(ffn_w1[0]), "cast_w1"),
                   (pad_c(ffn_w3[0]), "cast_w3"), (jnp.pad(ffn_w2[0], ((0, ff_p - ff_l), (0, 0))), "cast_w2")):
        with _after(w_in16):
            rest16.append(_cast_bf16(w_, nm))
    g_a, g_b, g_out = _gather_sc(rest16[:3], "gather_mix", 2)
    g_w1, g_w3, g_w2 = _gather_sc(rest16[3:], "gather_ffn", 10)
    g_out = g_out.reshape(1, d, d)
    g_w2 = g_w2.reshape(1, ff, d)
    mod_all = mod_all.transpose(1, 0, 2).reshape(2 * N_DEV, N_MOD * d) + ada_b
    mod_l = lax.dynamic_slice_in_dim(mod_all, me, 1, axis=0).reshape(N_MOD, 1, d)
    mod_c = mod_all[N_DEV:N_DEV + 1].reshape(N_MOD, 1, d)
    mods1 = jnp.stack([mod_c[0:2], mod_l[0:2]])
    mods2 = jnp.stack([mod_l[3:5], mod_l[3:5]])
    gate1, gate2 = mod_l[2], mod_l[5]

    xcat = jnp.concatenate([ctx[0], x[0]], axis=0)
    hcat = _norm_mod_fwd(xcat, norm1_g, mods1, n_ctx_tiles, tr)
    p = _mm_xw(hcat, g_in, F32, "in_proj")
    lb_logits = lb_parts.transpose(1, 0, 2).reshape(2, 2, hw)
    lb_soft = _tiled(lambda a, b: (1.0 / (1.0 + jnp.exp(b - a)),), [_const(lb_logits[:, 0]), _const(lb_logits[:, 1])],
                     [((2, hw), F32, (2, hw), lambda *_: (0, 0))], (), "lb_softmax")[0]
    lb_f, lb_b = lb_soft[0:1], lb_soft[1:2]
    o_f, *saved_f = _hgrn_fwd(p, lb_f, 1, False, n_ctx)
    o_b, *saved_b = _hgrn_fwd(p, lb_b, 2, True, n_ctx)

    cos, sin, onehot, neg = _na_tables(t, n_ctx)
    bias = _bias_slabs(na_rel_bias[0], onehot, neg)
    heads = lambda a: [a[:, h * HEAD_DIM:(h + 1) * HEAD_DIM] for h in range(N_HEADS)]
    seg_rows = lambda seg, off=0: (p, (tr, hw), lambda i: (i + off, seg))
    wide = lambda a, off=0: (a, (tr, hw), lambda i: (i + off, 0))
    out_wide = lambda n_rows, dt: ((n_rows, hw), dt, (tr, hw), lambda i: (i, 0))
    acc_head = ((1, HEAD_DIM), F32, (1, HEAD_DIM), lambda i: (0, 0))
    lat0 = n_ctx // tr
    tab = lambda a, off=0: (a, (tr, HEAD_DIM), lambda i: (i + off, 0))

    def qk_fn(tv, g, cs, sn):
        return (jnp.concatenate([_f_qk(th, g, cs, sn) for th in heads(tv)], axis=1),)

    qn = _tiled(qk_fn, [seg_rows(5, lat0), _const(na_q_norm_g), tab(cos, lat0), tab(sin, lat0)], [out_wide(t, BF16)], (t // tr,), "q_norm_rope")[0]
    kall = _tiled(qk_fn, [seg_rows(6), _const(na_k_norm_g), tab(cos), tab(sin)], [out_wide(tt, BF16)], (tt // tr,), "k_norm_rope")[0]
    y_b = _na_fwd(qn, kall, p, 7, bias, n_ctx)

    def readout_fn(a, b, gt, g):
        return (jnp.concatenate([_f_readout(ah, bh, gh, g) for ah, bh, gh in zip(heads(a), heads(b), heads(gt))], axis=1),)

    y_a = _tiled(readout_fn, [wide(o_f, lat0), wide(o_b, lat0), seg_rows(4, lat0), _const(hgrn_norm_g)], [out_wide(t, BF16)], (t // tr,), "hgrn_readout")[0]

    p_a = _mm_xw(y_a, g_a, F32, "branch_a")
    p_b = _mm_xw(y_b, g_b, F32, "branch_b")
    td = _pick(d, (512, 256, 128))
    nd_t = d // td
    lat_r = n_ctx // tr
    gcol = lambda k: (p, (tr, td), lambda i, j, k=k: (i + lat_r, 8 * hw // td + k * nd_t + j))
    dtile = lambda a: (a, (tr, td), lambda i, j: (i, j))
    z = _tiled(lambda ga, gb, pa, pb: (_f_merge(ga, gb, pa, pb),), [gcol(0), gcol(1), dtile(p_a), dtile(p_b)],
               [((t, d), BF16, (tr, td), lambda i, j: (i, j))], (t // tr, nd_t), "merge")[0]
    mix = _mm_xw(z, g_out, F32, "out_proj")
    xl = x[0]
    x_mid = _tiled(lambda a, g, m_: (_f_resid(a, g, m_),), [_rows(xl, tr), _const(gate1), _rows(mix, tr)],
                   [((t, d), F32, (tr, d), lambda i: (i, 0))], (t // tr,), "resid1")[0]

    h2 = _norm_mod_fwd(x_mid, norm2_g, mods2, 0, tr)
    u0 = _mm_xw(h2, g_w1, F32, "ffn_up1")
    t3 = _mm_xw(h2, g_w3, F32, "ffn_up3")
    cw_full = cw_all.transpose(1, 0, 2).reshape(8, ff)
    cb_full = jnp.pad(ffn_conv_b.reshape(N_DEV, ff_l), ((0, 0), (0, ff_p - ff_l))).reshape(1, ff)
    a_act = _ffn_act_fwd(u0, t3, cw_full, cb_full)
    f_out = _mm_xw(a_act, g_w2, F32, "ffn_down")

    def loss_fn(xm, g, f, tg):
        err = xm + g * f - tg
        return err * (1.0 / d), jnp.sum(err * err, axis=0, keepdims=True) * (0.5 / d), jnp.sum(err * (1.0 / d) * f, axis=0, keepdims=True)

    dy, loss_cols, d_gate2 = _tiled(loss_fn, [_rows(x_mid, tr), _const(gate2), _rows(f_out, tr), _rows(loss_target[0], tr)],
                                    [((t, d), F32, (tr, d), lambda i: (i, 0)), ((1, d), F32, (1, d), lambda i: (0, 0)),
                                     ((1, d), F32, (1, d), lambda i: (0, 0))], (t // tr,), "loss", acc=(1, 2))

    df = _tiled(lambda a, g: (a * g,), [_rows(dy, tr), _const(gate2)], [((t, d), BF16, (tr, d), lambda i: (i, 0))], (t // tr,), "d_ffn_out")[0]
    d_w2 = _mm_xtdy(a_act, df, 1, BF16, "d_w2")
    da = _mm_dyw(df, g_w2, BF16, "d_act")
    du0, dt3, d_cw, d_cb = _ffn_act_bwd(u0, t3, cw_full, cb_full, da)
    d_w1 = _mm_xtdy(h2, du0, N_DEV, BF16, "d_w1")
    d_w3 = _mm_xtdy(h2, dt3, N_DEV, BF16, "d_w3")
    ffn_blocks = [d_w1, d_w3, d_w2.reshape(N_DEV, ff_p, d)]
    ffn_theirs = _pair_swap_sc(ffn_blocks, "scatter_ffn_pair", 3)
    dh2 = _mm_dyw(du0, g_w1, F32, "d_h2_a")
    dh2 = _mm_dyw(dt3, g_w3, F32, "d_h2_b", init=dh2)
    with _after(dh2):
        ffn_sums = [_pair_add(ffn_blocks[0], ffn_theirs[0], "pair_add_w1")]
    ffn_sums += [_pair_add(ffn_blocks[1], ffn_theirs[1], "pair_add_w3"), _pair_add(ffn_blocks[2], ffn_theirs[2], "pair_add_w2")]
    r_w1, r_w3, r_w2 = _chip_scatter_sc(ffn_sums, "scatter_ffn_chips", 4)
    with _after(*ffn_sums):
        dx_mid, d_norm2, d_mods2 = _norm_mod_bwd(x_mid, norm2_g, mods2, dh2, dy, 0, tr, "norm_mod2_bwd")

    dm, d_gate1 = _tiled(lambda dxm, g, m_: (dxm * g, jnp.sum(dxm * m_, axis=0, keepdims=True)),
                         [_rows(dx_mid, tr), _const(gate1), _rows(mix, tr)],
                         [((t, d), BF16, (tr, d), lambda i: (i, 0)), ((1, d), F32, (1, d), lambda i: (0, 0))], (t // tr,), "d_resid1", acc=(1,))
    d_wout = _mm_xtdy(z, dm, 1, BF16, "d_w_out")
    dz = _mm_dyw(dm, g_out, F32, "d_merge")

    def merge_bwd(ga, gb, pa, pb, dzv):
        _, vjp = jax.vjp(_f_merge, ga, gb, pa, pb)
        return vjp(dzv)

    dga, dgb, dpa, dpb = _tiled(merge_bwd, [gcol(0), gcol(1), dtile(p_a), dtile(p_b), dtile(dz)],
                                [((t, d), BF16, (tr, td), lambda i, j: (i, j))] * 4, (t // tr, nd_t), "merge_bwd")
    d_wa = _mm_xtdy(y_a, dpa, N_DEV, BF16, "d_w_a")
    d_wb = _mm_xtdy(y_b, dpb, N_DEV, BF16, "d_w_b")
    dy_a = _mm_dyw(dpa, g_a, F32, "d_y_a")
    dy_b = _mm_dyw(dpb, g_b, BF16, "d_y_b")
    dqn, dkall, dv_na, dbias = _na_bwd(qn, kall, p, 7, bias, dy_b, n_ctx)
    mix_blocks = [_tie(d_wa, r_w1, dqn), d_wb, d_wout.reshape(N_DEV, d // N_DEV, d)]
    mix_theirs = _pair_swap_sc(mix_blocks, "scatter_mix_pair", 5)

    def readout_bwd(a, b, gt, g, ct):
        das, dgts, dg = [], [], 0.0
        for ah, bh, gh, ch in zip(heads(a), heads(b), heads(gt), heads(ct)):
            _, vjp = jax.vjp(_f_readout, ah, bh, gh, g)
            da_h, _, dgt_h, dg_h = vjp(ch)
            das.append(da_h)
            dgts.append(dgt_h)
            dg = dg + dg_h
        return jnp.concatenate(das, axis=1), jnp.concatenate(dgts, axis=1), dg

    with _after(mix_blocks[0]):
        do_h, d_gate_o, d_hnorm = _tiled(
            readout_bwd, [wide(o_f, lat0), wide(o_b, lat0), seg_rows(4, lat0), _const(hgrn_norm_g), wide(dy_a)],
            [out_wide(t, F32), out_wide(t, BF16), acc_head], (t // tr,), "readout_bwd", acc=(2,))
    dq1, dfl_f, dv1, dlb_f = _hgrn_bwd(p, lb_f, do_h, saved_f, 1, False, n_ctx, None)
    with _after(dq1):
        mix_sums = [_pair_add(mix_blocks[0], mix_theirs[0], "pair_add_wa")]
    mix_sums += [_pair_add(mix_blocks[1], mix_theirs[1], "pair_add_wb"), _pair_add(mix_blocks[2], mix_theirs[2], "pair_add_wout")]
    r_a, r_b, r_out = _chip_scatter_sc(mix_sums, "scatter_mix_chips", 6)
    with _after(*mix_sums):
        dq_h, dfl_b, dv_h, dlb_b = _hgrn_bwd(p, lb_b, do_h, saved_b, 2, True, n_ctx, (dq1, dv1))

    def qk_bwd(tv, g, cs, sn, ct):
        dts, dg = [], 0.0
        for th, ch in zip(heads(tv), heads(ct)):
            _, vjp = jax.vjp(lambda a, b: _f_qk(a, b, cs, sn), th, g)
            dt_h, dg_h = vjp(ch)
            dts.append(dt_h)
            dg = dg + dg_h
        return jnp.concatenate(dts, axis=1), dg

    d_pq, d_qnorm = _tiled(qk_bwd, [seg_rows(5, lat0), _const(na_q_norm_g), tab(cos, lat0), tab(sin, lat0), wide(dqn)],
                           [out_wide(t, BF16), acc_head], (t // tr,), "q_norm_rope_bwd", acc=(1,))
    d_pk, d_knorm = _tiled(qk_bwd, [seg_rows(6), _const(na_k_norm_g), tab(cos), tab(sin), wide(dkall)],
                           [out_wide(tt, BF16), acc_head], (tt // tr,), "k_norm_rope_bwd", acc=(1,))

    zc = lambda w_: jnp.zeros((n_ctx, w_), BF16)
    lat_only = lambda a: jnp.concatenate([zc(a.shape[1]), a], axis=0)
    dp = jnp.concatenate([dq_h, dfl_f, dfl_b, dv_h, lat_only(d_gate_o), lat_only(d_pq), d_pk, dv_na, lat_only(dga), lat_only(dgb)], axis=1)
    d_win = _tie(_mm_xtdy(hcat, dp, N_DEV, BF16, "d_w_in"), r_a)
    win_theirs, = _pair_swap_sc([d_win], "scatter_w_in_pair", 7)
    with _after(d_win):
        dhcat = _mm_dyw(dp, g_in, BF16, "d_hcat")
    zero_ctx = jnp.concatenate([jnp.zeros((n_ctx, d), F32), dx_mid], axis=0)
    dxcat, d_norm1, d_mods1 = _norm_mod_bwd(xcat, norm1_g, mods1, dhcat, zero_ctx, n_ctx_tiles, tr, "norm_mod1_bwd")
    grad_x = dxcat[n_ctx:][None]

    zd = jnp.zeros((1, d), F32)
    dmod_l = jnp.concatenate([d_mods1[1, 0], d_mods1[1, 1], d_gate1, d_mods2[1, 0], d_mods2[1, 1], d_gate2], axis=1)
    dmod_c = jnp.concatenate([d_mods1[0, 0], d_mods1[0, 1], zd, zd, zd, zd], axis=1)
    dmods = jnp.concatenate([dmod_l, dmod_c], axis=0).reshape(2, N_DEV, ca).transpose(1, 0, 2)
    dmods = jnp.pad(dmods, ((0, 0), (0, 6), (0, 0)))
    got, = _exchange_sc([dmods], "scatter_dmod", True, 11)

    res = {}
    with _after(dmods):
        win_sums = _pair_add(d_win, win_theirs, "pair_add_w_in")
    r_in, = _chip_scatter_sc([win_sums], "scatter_w_in_chips", 8)
    with _after(win_sums):
        res["ffn_w1"] = _adam_big(r_w1, ffn_w1[0], m_ffn_w1[0], v_ffn_w1[0], "adam_w1")
    res["ffn_w3"] = _adam_big(r_w3, ffn_w3[0], m_ffn_w3[0], v_ffn_w3[0], "adam_w3")
    res["ffn_w2"] = _adam_big(r_w2, ffn_w2[0], m_ffn_w2[0], v_ffn_w2[0], "adam_w2")
    res["w_branch_a"] = _adam_big(r_a, w_branch_a[0], m_w_branch_a[0], v_w_branch_a[0], "adam_w_a")
    res["w_branch_b"] = _adam_big(r_b, w_branch_b[0], m_w_branch_b[0], v_w_branch_b[0], "adam_w_b")
    res["w_out"] = _adam_big(r_out, w_out[0], m_w_out[0], v_w_out[0], "adam_w_out")

    dm_rows = jnp.concatenate([got[:, 0], got[:, 1]], axis=0)
    with _after(res["w_out"][1]):
        d_ada = _mm_xtdy(act, dm_rows, 1, F32, "d_ada_w")[0]
    back = _mm_dyw(dm_rows, ada16.reshape(1, d, ca), F32, "d_silu_c")
    d_cctx_part = _tiled(lambda b, v: (jnp.sum(b[N_DEV:], axis=0, keepdims=True) * (jax.nn.sigmoid(v) * (1.0 + v * (1.0 - jax.nn.sigmoid(v)))),),
                         [_const(back), _const(c_ctx.reshape(1, d))], [((1, d), F32, (1, d), lambda *_: (0, 0))], (), "d_c_ctx")[0]
    res["ada_w"] = _adam_big(d_ada[None], ada_w[0], m_ada_w[0], v_ada_w[0], "adam_ada")

    d_rel = _bias_grad(dbias, onehot)
    d_lb_soft = jnp.concatenate([dlb_f, dlb_b], axis=0)
    d_lb0 = _tiled(lambda s, g: (g * s * (1.0 - s),), [_const(lb_soft), _const(d_lb_soft)], [((2, hw), F32, (2, hw), lambda *_: (0, 0))], (), "d_lb")[0]
    d_lb_full = jnp.stack([d_lb0, -d_lb0], axis=1)
    d_cw_l = d_cw[:3].reshape(3, N_DEV, ff_p)[:, :, :ff_l].reshape(1, 3, N_DEV * ff_l)
    d_cb_l = d_cb.reshape(N_DEV, ff_p)[:, :ff_l].reshape(1, N_DEV * ff_l)
    small = [d_cctx_part.reshape(d), (dmod_l + dmod_c), d_norm1, d_norm2, d_lb_full, d_hnorm, d_qnorm, d_knorm, d_rel, d_cw_l, d_cb_l, loss_cols]
    pk = _Pack([a.shape for a in small])
    small_slab = pk.pack(small)
    small_parts, = _gather_sc([small_slab], "gather_small", 9)
    with _after(small_slab):
        res["w_in"] = _adam_big(r_in, w_in[0], m_w_in[0], v_w_in[0], "adam_w_in")
    tot = _sum_parts(small_parts, "sum_small")
    g_cctx, g_ada_b, g_n1, g_n2, g_lb, g_hn, g_qn, g_kn, g_rel, g_cw, g_cb, loss_all = pk.unpack(tot)
    loss = _tiled(lambda v: (jnp.sum(v, axis=1, keepdims=True),), [_const(loss_all)], [((1, 1), F32, (1, 1), lambda *_: (0, 0))], (), "loss_total")[0][0, 0]
    g_lb = lax.dynamic_slice_in_dim(g_lb, me * HEAD_DIM, HEAD_DIM, axis=2)
    g_cw = lax.dynamic_slice_in_dim(g_cw, me * ff_l, ff_l, axis=2)
    small_names = [("c_ctx", g_cctx, c_ctx, m_c_ctx, v_c_ctx), ("ada_b", g_ada_b, ada_b, m_ada_b, v_ada_b),
                   ("norm1_g", g_n1, norm1_g, m_norm1_g, v_norm1_g), ("norm2_g", g_n2, norm2_g, m_norm2_g, v_norm2_g),
                   ("hgrn_lb_logits", g_lb, hgrn_lb_logits, m_hgrn_lb_logits, v_hgrn_lb_logits),
                   ("hgrn_norm_g", g_hn, hgrn_norm_g, m_hgrn_norm_g, v_hgrn_norm_g), ("na_q_norm_g", g_qn, na_q_norm_g, m_na_q_norm_g, v_na_q_norm_g),
                   ("na_k_norm_g", g_kn, na_k_norm_g, m_na_k_norm_g, v_na_k_norm_g), ("na_rel_bias", g_rel, na_rel_bias, m_na_rel_bias, v_na_rel_bias),
                   ("ffn_conv_w", g_cw, ffn_conv_w, m_ffn_conv_w, v_ffn_conv_w), ("ffn_conv_b", g_cb, ffn_conv_b, m_ffn_conv_b, v_ffn_conv_b)]
    pk2 = _Pack([s[1].shape for s in small_names])
    sd, sm, sv = _adam_small(*[pk2.pack([s[i] for s in small_names]) for i in (1, 2, 3, 4)])
    sd, sm, sv = pk2.unpack(sd), pk2.unpack(sm), pk2.unpack(sv)
    res.update({s[0]: (s[1], sd[i], sm[i], sv[i]) for i, s in enumerate(small_names)})
    for k in ("w_in", "w_branch_a", "w_branch_b", "w_out", "ffn_w1", "ffn_w3", "ffn_w2", "ada_w"):
        res[k] = tuple(a[None] for a in res[k])

    order = ["c_ctx", "ada_w", "ada_b", "norm1_g", "norm2_g", "w_in", "hgrn_lb_logits", "hgrn_norm_g", "na_q_norm_g", "na_k_norm_g",
             "na_rel_bias", "w_branch_a", "w_branch_b", "w_out", "ffn_w1", "ffn_w3", "ffn_conv_w", "ffn_conv_b", "ffn_w2"]
    shapes = {"c_ctx": c_ctx.shape, "ada_b": ada_b.shape, "norm1_g": norm1_g.shape, "norm2_g": norm2_g.shape,
              "hgrn_lb_logits": hgrn_lb_logits.shape, "hgrn_norm_g": hgrn_norm_g.shape, "na_q_norm_g": na_q_norm_g.shape,
              "na_k_norm_g": na_k_norm_g.shape, "na_rel_bias": na_rel_bias.shape, "ffn_conv_w": ffn_conv_w.shape, "ffn_conv_b": ffn_conv_b.shape}
    outs = [loss, grad_x]
    for part in range(4):
        for k in order:
            a = res[k][part]
            outs.append(a.reshape(shapes[k]) if k in shapes else a)
    return tuple(outs)
```

```python
import functools

import numpy as np
import jax
import jax.numpy as jnp
from jax import lax
from jax.experimental import pallas as pl
from jax.experimental.pallas import tpu as pltpu
from jax.experimental.pallas import tpu_sc as plsc

F32 = jnp.float32
BF16 = jnp.bfloat16
HIGHEST = lax.Precision.HIGHEST

N_DEV = 8
MESH_ID = pl.DeviceIdType.MESH
LANE = 128
HEAD_DIM = 128
N_HEADS = 8
GRID_W = 64
WIN_R = 8
WIN_C = 16
ROPE_THETA = 10000.0
EPS = 1e-6
N_MOD = 6
HGRN_BLOCK = 16
NEG_BIG = -1e30
VMEM_LIMIT = 56 << 20

ADAM_LR = 0.001
ADAM_B1 = 0.9
ADAM_B2 = 0.999
ADAM_EPS = 1e-08
ADAM_WD = 0.01
ADAM_STEP = 10

HBM_SPEC = pl.BlockSpec(memory_space=pltpu.HBM)


_ORDER_AFTER = []


class _after:
    def __init__(self, *arrs):
        self.arrs = list(arrs)

    def __enter__(self):
        _ORDER_AFTER.extend(self.arrs)

    def __exit__(self, *exc):
        del _ORDER_AFTER[:]


def _tie(x, *deps):
    return lax.optimization_barrier((x, *deps))[0]


def _pcall(body, *, name, out_shape, grid=None, in_specs=None, out_specs=None, scratch=(), aliases=None):
    kw = {}
    if grid is not None:
        kw["grid"] = grid
    extra = []
    if _ORDER_AFTER and in_specs is not None:
        extra = list(_ORDER_AFTER)
        del _ORDER_AFTER[:]
        n_in, n_extra, inner = len(in_specs), len(extra), body
        in_specs = list(in_specs) + [pl.BlockSpec(memory_space=pl.ANY)] * n_extra

        def body(*refs):
            return inner(*refs[:n_in], *refs[n_in + n_extra:])

    if extra:
        call = _pcall_inner(body, name, out_shape, kw, in_specs, out_specs, scratch, aliases)
        return lambda *args: call(*args, *extra)
    return _pcall_inner(body, name, out_shape, kw, in_specs, out_specs, scratch, aliases)


def _pcall_inner(body, name, out_shape, kw, in_specs, out_specs, scratch, aliases):
    if in_specs is not None:
        kw["in_specs"] = in_specs
    if out_specs is not None:
        kw["out_specs"] = out_specs
    if scratch:
        kw["scratch_shapes"] = list(scratch)
    if aliases:
        kw["input_output_aliases"] = aliases
    return pl.pallas_call(body, name=name, out_shape=out_shape,
                          compiler_params=pltpu.CompilerParams(vmem_limit_bytes=VMEM_LIMIT), **kw)


def _pick(dim, cands):
    for c in cands:
        if c <= dim and dim % c == 0:
            return c
    return dim


def _sds(shape, dtype):
    return jax.ShapeDtypeStruct(tuple(shape), dtype)


def _peers():
    x, y, c = lax.axis_index("x"), lax.axis_index("y"), lax.axis_index("c")
    out = []
    for k in range(1, N_DEV):
        px = 1 - x if (k >> 2) & 1 else x
        py = 1 - y if (k >> 1) & 1 else y
        pc = 1 - c if k & 1 else c
        out.append((k, (px, py, pc), 4 * px + 2 * py + pc))
    return 4 * x + 2 * y + c, out


def _exchange(arrs, name, scatter):
    n = len(arrs)

    def body(*refs):
        ins, outs = refs[:n], refs[n:2 * n]
        send, recv, loc = refs[2 * n:]
        me, peers = _peers()
        started = []
        for i in range(n):
            src = ins[i].at[me] if scatter else ins[i]
            cp = pltpu.make_async_copy(src, outs[i].at[me], loc.at[i])
            cp.start()
            started.append(cp)
        sends = []
        for k, peer, pidx in peers:
            for i in range(n):
                src = ins[i].at[pidx] if scatter else ins[i]
                cp = pltpu.make_async_remote_copy(src_ref=src, dst_ref=outs[i].at[me], send_sem=send.at[i * 7 + k - 1],
                                                  recv_sem=recv.at[i * 7 + k - 1], device_id=peer, device_id_type=MESH_ID)
                cp.start()
                sends.append(cp)
        for k, peer, pidx in peers:
            for i in range(n):
                src = ins[i].at[pidx] if scatter else ins[i]
                pltpu.make_async_remote_copy(src_ref=src, dst_ref=outs[i].at[pidx], send_sem=send.at[i * 7 + k - 1],
                                             recv_sem=recv.at[i * 7 + k - 1], device_id=peer, device_id_type=MESH_ID).wait_recv()
        for cp in sends:
            cp.wait_send()
        for cp in started:
            cp.wait()

    out_shape = [_sds(a.shape if scatter else (N_DEV,) + a.shape, a.dtype) for a in arrs]
    res = _pcall(body, name=name, out_shape=out_shape, in_specs=[HBM_SPEC] * n, out_specs=[HBM_SPEC] * n,
                 scratch=[pltpu.SemaphoreType.DMA((7 * n,)), pltpu.SemaphoreType.DMA((7 * n,)), pltpu.SemaphoreType.DMA((n,))])(*arrs)
    return list(res)


def _exchange_sc(arrs, name, scatter, collective_id):
    n = len(arrs)
    srcs = [jax.new_ref(a, memory_space=pltpu.MemorySpace.HBM) for a in arrs]
    lands = [jax.empty_ref(_sds(a.shape if scatter else (N_DEV,) + a.shape, a.dtype), memory_space=pltpu.MemorySpace.HBM) for a in arrs]

    @pl.kernel(mesh=plsc.ScalarSubcoreMesh(axis_name="seq", num_cores=1), name=name,
               scratch_types=(pltpu.SemaphoreType.DMA((7 * n,)), pltpu.SemaphoreType.DMA((7 * n,)), pltpu.SemaphoreType.DMA((n,))),
               compiler_params=pltpu.CompilerParams(collective_id=collective_id))
    def launch(send, recv, loc):
        me, peers = _peers()
        barrier = pltpu.get_barrier_semaphore()
        for _, peer, _ in peers:
            pl.semaphore_signal(barrier, inc=1, device_id=peer, device_id_type=MESH_ID)
        pl.semaphore_wait(barrier, N_DEV - 1)
        own = [pltpu.make_async_copy(srcs[i].at[me] if scatter else srcs[i], lands[i].at[me], loc.at[i]) for i in range(n)]
        for cp in own:
            cp.start()
        sends = []
        for k, peer, pidx in peers:
            for i in range(n):
                src = srcs[i].at[pidx] if scatter else srcs[i]
                cp = pltpu.make_async_remote_copy(src_ref=src, dst_ref=lands[i].at[me], send_sem=send.at[i * 7 + k - 1],
                                                  recv_sem=recv.at[i * 7 + k - 1], device_id=peer, device_id_type=MESH_ID)
                cp.start()
                sends.append(cp)
        for k, peer, pidx in peers:
            for i in range(n):
                src = srcs[i].at[pidx] if scatter else srcs[i]
                pltpu.make_async_remote_copy(src_ref=src, dst_ref=lands[i].at[pidx], send_sem=send.at[i * 7 + k - 1],
                                             recv_sem=recv.at[i * 7 + k - 1], device_id=peer, device_id_type=MESH_ID).wait_recv()
        for cp in sends:
            cp.wait_send()
        for cp in own:
            cp.wait()

    launch()
    return [r[...] for r in lands]


def _gather_sc(arrs, name, collective_id):
    n = len(arrs)
    srcs = [jax.new_ref(a, memory_space=pltpu.MemorySpace.HBM) for a in arrs]
    lands = [jax.empty_ref(_sds((N_DEV,) + a.shape, a.dtype), memory_space=pltpu.MemorySpace.HBM) for a in arrs]

    @pl.kernel(mesh=plsc.ScalarSubcoreMesh(axis_name="seq", num_cores=1), name=name,
               scratch_types=(pltpu.SemaphoreType.DMA((7 * n,)), pltpu.SemaphoreType.DMA((7 * n,)), pltpu.SemaphoreType.DMA((n,))),
               compiler_params=pltpu.CompilerParams(collective_id=collective_id))
    def launch(send, recv, loc):
        x, y, c = lax.axis_index("x"), lax.axis_index("y"), lax.axis_index("c")
        me, sibling = (x, y, c), (x, y, 1 - c)
        chips = [(1 - x, y), (x, 1 - y), (1 - x, 1 - y)]
        index = lambda px, py, pc: 4 * px + 2 * py + pc
        barrier = pltpu.get_barrier_semaphore()
        for peer in [sibling] + [(*chip, c) for chip in chips]:
            pl.semaphore_signal(barrier, inc=1, device_id=peer, device_id_type=MESH_ID)
        pl.semaphore_wait(barrier, 4)

        def copy(i, k, block, to, from_src):
            return pltpu.make_async_remote_copy(src_ref=srcs[i] if from_src else lands[i].at[index(*block)], dst_ref=lands[i].at[index(*block)],
                                                send_sem=send.at[i * 7 + k], recv_sem=recv.at[i * 7 + k], device_id=to, device_id_type=MESH_ID)

        own = [pltpu.make_async_copy(srcs[i], lands[i].at[index(*me)], loc.at[i]) for i in range(n)]
        for cp in own:
            cp.start()
        started = []
        for i in range(n):
            started.append(copy(i, 0, me, sibling, True))
            started += [copy(i, 1 + j, me, (*chip, c), True) for j, chip in enumerate(chips)]
        for cp in started:
            cp.start()
        for j, chip in enumerate(chips):
            for i in range(n):
                copy(i, 1 + j, (*chip, c), me, False).wait_recv()
                fwd = copy(i, 4 + j, (*chip, c), sibling, False)
                fwd.start()
                started.append(fwd)
        for i in range(n):
            copy(i, 0, sibling, me, False).wait_recv()
            for j, chip in enumerate(chips):
                copy(i, 4 + j, (*chip, 1 - c), me, False).wait_recv()
        for cp in started:
            cp.wait_send()
        for cp in own:
            cp.wait()

    launch()
    return [r[...] for r in lands]


def _pair_swap_sc(blocks, name, collective_id):
    n = len(blocks)
    srcs = [jax.new_ref(b, memory_space=pltpu.MemorySpace.HBM) for b in blocks]
    lands = [jax.empty_ref(_sds((4,) + b.shape[1:], b.dtype), memory_space=pltpu.MemorySpace.HBM) for b in blocks]

    @pl.kernel(mesh=plsc.ScalarSubcoreMesh(axis_name="seq", num_cores=1), name=name,
               scratch_types=(pltpu.SemaphoreType.DMA((4 * n,)), pltpu.SemaphoreType.DMA((4 * n,))),
               compiler_params=pltpu.CompilerParams(collective_id=collective_id))
    def launch(send, recv):
        x, y, c = lax.axis_index("x"), lax.axis_index("y"), lax.axis_index("c")
        sibling = (x, y, 1 - c)
        barrier = pltpu.get_barrier_semaphore()
        pl.semaphore_signal(barrier, inc=1, device_id=sibling, device_id_type=MESH_ID)
        pl.semaphore_wait(barrier, 1)
        copies = [pltpu.make_async_remote_copy(src_ref=srcs[i].at[2 * q + 1 - c], dst_ref=lands[i].at[q], send_sem=send.at[4 * i + q],
                                               recv_sem=recv.at[4 * i + q], device_id=sibling, device_id_type=MESH_ID)
                  for i in range(n) for q in range(4)]
        for cp in copies:
            cp.start()
        for cp in copies:
            cp.wait()

    launch()
    return [r[...] for r in lands]


def _pair_add(blocks, theirs, name):
    _, r, c_ = blocks.shape
    mine = lax.dynamic_index_in_dim(blocks.reshape(4, 2, r, c_), lax.axis_index("c"), axis=1, keepdims=False)
    tr = _pick(r, (256, 128, 64, 16))
    spec = (None, tr, c_)
    return _tiled(lambda a, b: (a.astype(F32) + b.astype(F32),), [(mine, spec, lambda q, i: (q, i, 0)), (theirs, spec, lambda q, i: (q, i, 0))],
                  [((4, r, c_), BF16, spec, lambda q, i: (q, i, 0))], (4, r // tr), name)[0]


def _chip_scatter_sc(sums, name, collective_id):
    n = len(sums)
    srcs = [jax.new_ref(s, memory_space=pltpu.MemorySpace.HBM) for s in sums]
    lands = [jax.empty_ref(_sds(s.shape, s.dtype), memory_space=pltpu.MemorySpace.HBM) for s in sums]

    @pl.kernel(mesh=plsc.ScalarSubcoreMesh(axis_name="seq", num_cores=1), name=name,
               scratch_types=(pltpu.SemaphoreType.DMA((3 * n,)), pltpu.SemaphoreType.DMA((3 * n,)), pltpu.SemaphoreType.DMA((n,))),
               compiler_params=pltpu.CompilerParams(collective_id=collective_id))
    def launch(send, recv, loc):
        x, y, c = lax.axis_index("x"), lax.axis_index("y"), lax.axis_index("c")
        chips = [(1 - x, y), (x, 1 - y), (1 - x, 1 - y)]
        my_chip = 2 * x + y
        barrier = pltpu.get_barrier_semaphore()
        for chip in chips:
            pl.semaphore_signal(barrier, inc=1, device_id=(*chip, c), device_id_type=MESH_ID)
        pl.semaphore_wait(barrier, 3)
        own = [pltpu.make_async_copy(srcs[i].at[my_chip], lands[i].at[my_chip], loc.at[i]) for i in range(n)]
        for cp in own:
            cp.start()
        sends = [pltpu.make_async_remote_copy(src_ref=srcs[i].at[2 * px + py], dst_ref=lands[i].at[my_chip], send_sem=send.at[3 * i + j],
                                              recv_sem=recv.at[3 * i + j], device_id=(px, py, c), device_id_type=MESH_ID)
                 for j, (px, py) in enumerate(chips) for i in range(n)]
        for cp in sends:
            cp.start()
        for j, (px, py) in enumerate(chips):
            for i in range(n):
                pltpu.make_async_remote_copy(src_ref=srcs[i].at[my_chip], dst_ref=lands[i].at[2 * px + py], send_sem=send.at[3 * i + j],
                                             recv_sem=recv.at[3 * i + j], device_id=(px, py, c), device_id_type=MESH_ID).wait_recv()
        for cp in sends:
            cp.wait_send()
        for cp in own:
            cp.wait()

    launch()
    return [r[...] for r in lands]


class _InFlight:
    def __init__(self, send, recv, srcs, lands, token, scatter):
        self.send, self.recv, self.srcs, self.lands, self.token, self.scatter = send, recv, srcs, lands, token, scatter


SEM_SPEC = pl.BlockSpec(memory_space=pltpu.SEMAPHORE)
SIDE_EFFECT = pltpu.SideEffectType.DATAFLOW_SIDE_EFFECTING


def _exchange_start(arrs, name, scatter):
    n = len(arrs)

    def body(*refs):
        ins, lands = refs[:n], refs[n:2 * n]
        send, recv = refs[2 * n], refs[2 * n + 1]
        token = refs[4 * n + 2]
        loc = refs[4 * n + 3]
        me, peers = _peers()
        own = [pltpu.make_async_copy(ins[i].at[me] if scatter else ins[i], lands[i].at[me], loc.at[i]) for i in range(n)]
        for cp in own:
            cp.start()
        for cp in own:
            cp.wait()
        for k, peer, pidx in peers:
            for i in range(n):
                src = ins[i].at[pidx] if scatter else ins[i]
                pltpu.make_async_remote_copy(src_ref=src, dst_ref=lands[i].at[me], send_sem=send.at[i * 7 + k - 1],
                                             recv_sem=recv.at[i * 7 + k - 1], device_id=peer, device_id_type=MESH_ID).start()
        token[...] = jnp.zeros_like(token)

    land_shapes = [a.shape if scatter else (N_DEV,) + a.shape for a in arrs]
    hbm = lambda a: pltpu.with_memory_space_constraint(a, pltpu.HBM)
    args = [hbm(a) for a in arrs] + [hbm(lax.empty(s, a.dtype)) for s, a in zip(land_shapes, arrs)]
    out_shape = ([pltpu.SemaphoreType.DMA((7 * n,)), pltpu.SemaphoreType.DMA((7 * n,))]
                 + [pltpu.HBM(a.shape, a.dtype) for a in arrs] + [pltpu.HBM(s, a.dtype) for s, a in zip(land_shapes, arrs)]
                 + [_sds((8, LANE), F32)])
    res = pl.pallas_call(
        body, name=name, out_shape=out_shape, in_specs=[HBM_SPEC] * (2 * n),
        out_specs=[SEM_SPEC, SEM_SPEC] + [HBM_SPEC] * (2 * n) + [pl.BlockSpec(memory_space=pltpu.VMEM)],
        input_output_aliases={i: 2 + i for i in range(2 * n)},
        scratch_shapes=[pltpu.SemaphoreType.DMA((n,))],
        compiler_params=pltpu.CompilerParams(has_side_effects=SIDE_EFFECT))(*args)
    return _InFlight(res[0], res[1], list(res[2:2 + n]), list(res[2 + n:2 + 2 * n]), res[2 + 2 * n], scatter)


def _exchange_wait(h, after, name):
    n = len(h.srcs)
    scatter = h.scatter
    after = list(after)

    def body(*refs):
        ins, lands = refs[:n], refs[n:2 * n]
        send, recv = refs[2 * n], refs[2 * n + 1]
        _, peers = _peers()
        for k, peer, pidx in peers:
            for i in range(n):
                src = ins[i].at[pidx] if scatter else ins[i]
                cp = pltpu.make_async_remote_copy(src_ref=src, dst_ref=lands[i].at[pidx], send_sem=send.at[i * 7 + k - 1],
                                                  recv_sem=recv.at[i * 7 + k - 1], device_id=peer, device_id_type=MESH_ID)
                cp.wait_send()
                cp.wait_recv()

    res = pl.pallas_call(
        body, name=name, out_shape=[pltpu.HBM(a.shape, a.dtype) for a in h.srcs + h.lands],
        in_specs=[HBM_SPEC] * (2 * n) + [SEM_SPEC, SEM_SPEC] + [pl.BlockSpec(memory_space=pl.ANY)] * len(after),
        out_specs=[HBM_SPEC] * (2 * n), input_output_aliases={i: i for i in range(2 * n)},
        compiler_params=pltpu.CompilerParams(has_side_effects=SIDE_EFFECT))(*h.srcs, *h.lands, h.send, h.recv, *after)
    return list(res[n:])


def _mm_xw(x, g, out_dtype, name, tm_c=(768, 512, 384, 256, 128, 64, 16), tn_c=(1024, 768, 512, 256, 128), tk_c=(2048, 1024, 768, 512, 256)):
    m, r = x.shape
    nb, r2, cl = g.shape
    assert r == r2
    tm, tn, tk = _pick(m, tm_c), _pick(cl, tn_c), _pick(r, tk_c)
    q, nk = cl // tn, r // tk

    def body(x_ref, g_ref, o_ref, *acc):
        p = lax.dot_general(x_ref[...].astype(BF16), g_ref[...], (((1,), (0,)), ((), ())), preferred_element_type=F32)
        if nk == 1:
            o_ref[...] = p.astype(o_ref.dtype)
        else:
            k = pl.program_id(2)

            @pl.when(k == 0)
            def _():
                acc[0][...] = p

            @pl.when(k > 0)
            def _():
                acc[0][...] += p

            @pl.when(k == nk - 1)
            def _():
                o_ref[...] = acc[0][...].astype(o_ref.dtype)

    return _pcall(
        body, name=name, grid=(m // tm, nb * q, nk),
        in_specs=[pl.BlockSpec((tm, tk), lambda i, j, k: (i, k)), pl.BlockSpec((None, tk, tn), lambda i, j, k: (j // q, k, j % q))],
        out_specs=pl.BlockSpec((tm, tn), lambda i, j, k: (i, j)),
        out_shape=_sds((m, nb * cl), out_dtype),
        scratch=[] if nk == 1 else [pltpu.VMEM((tm, tn), F32)])(x, g)


def _mm_dyw(dy, g, out_dtype, name, init=None, tm_c=(1024, 768, 512, 384, 256, 128), tn_c=(1024, 512, 256, 128), tk_c=(2048, 1536, 1024, 768, 512, 256, 128)):
    m, n = dy.shape
    nb, r, cl = g.shape
    assert n == nb * cl
    tm, tn, tk = _pick(m, tm_c), _pick(r, tn_c), _pick(cl, tk_c)
    q = cl // tk
    nk = nb * q
    has_init = init is not None

    def body(*refs):
        if has_init:
            dy_ref, g_ref, i_ref, o_ref, acc = refs
        else:
            dy_ref, g_ref, o_ref, acc = refs
        k = pl.program_id(2)
        p = lax.dot_general(dy_ref[...].astype(BF16), g_ref[...], (((1,), (1,)), ((), ())), preferred_element_type=F32)

        @pl.when(k == 0)
        def _():
            acc[...] = p + i_ref[...].astype(F32) if has_init else p

        @pl.when(k > 0)
        def _():
            acc[...] += p

        @pl.when(k == nk - 1)
        def _():
            o_ref[...] = acc[...].astype(o_ref.dtype)

    in_specs = [pl.BlockSpec((tm, tk), lambda i, j, k: (i, k)), pl.BlockSpec((None, tn, tk), lambda i, j, k: (k // q, j, k % q))]
    args = [dy, g]
    if has_init:
        in_specs.append(pl.BlockSpec((tm, tn), lambda i, j, k: (i, j)))
        args.append(init)
    return _pcall(body, name=name, grid=(m // tm, r // tn, nk), in_specs=in_specs,
                  out_specs=pl.BlockSpec((tm, tn), lambda i, j, k: (i, j)), out_shape=_sds((m, r), out_dtype),
                  scratch=[pltpu.VMEM((tm, tn), F32)])(*args)


def _mm_xtdy(x, dy, nb, out_dtype, name, tm_c=(1024, 512, 256, 128), tn_c=(768, 512, 256, 128), tk_c=(2304, 2048, 1152, 1024, 768, 512, 256, 128, 16)):
    t, r = x.shape
    t2, n = dy.shape
    assert t == t2 and n % nb == 0
    cl = n // nb
    tm, tn, tk = _pick(r, tm_c), _pick(cl, tn_c), _pick(t, tk_c)
    q, nk = cl // tn, t // tk

    def body(x_ref, dy_ref, o_ref, *acc):
        p = lax.dot_general(x_ref[...].astype(BF16), dy_ref[...].astype(BF16), (((0,), (0,)), ((), ())), preferred_element_type=F32)
        if nk == 1:
            o_ref[...] = p.astype(o_ref.dtype)
        else:
            k = pl.program_id(2)

            @pl.when(k == 0)
            def _():
                acc[0][...] = p

            @pl.when(k > 0)
            def _():
                acc[0][...] += p

            @pl.when(k == nk - 1)
            def _():
                o_ref[...] = acc[0][...].astype(o_ref.dtype)

    return _pcall(
        body, name=name, grid=(r // tm, nb * q, nk),
        in_specs=[pl.BlockSpec((tk, tm), lambda i, j, k: (k, i)), pl.BlockSpec((tk, tn), lambda i, j, k: (k, j))],
        out_specs=pl.BlockSpec((None, tm, tn), lambda i, j, k: (j // q, i, j % q)),
        out_shape=_sds((nb, r, cl), out_dtype),
        scratch=[] if nk == 1 else [pltpu.VMEM((tm, tn), F32)])(x, dy)


def _mm_f32(a, b, name, trans_b=False):
    dims = (((1,), (1,)), ((), ())) if trans_b else (((1,), (0,)), ((), ()))
    n = b.shape[0] if trans_b else b.shape[1]

    def body(a_ref, b_ref, o_ref):
        o_ref[...] = lax.dot_general(a_ref[...], b_ref[...], dims, precision=HIGHEST, preferred_element_type=F32)

    return _pcall(body, name=name, out_shape=_sds((a.shape[0], n), F32))(a, b)


def _tiled(fn, ins, outs, grid, name, acc=()):
    n_in = len(ins)
    grid = tuple(grid) or (1,)
    nd = len(grid)

    def body(*refs):
        vals = fn(*[r[...] for r in refs[:n_in]])
        if not isinstance(vals, (tuple, list)):
            vals = (vals,)
        first = None
        for o, (ref, v) in enumerate(zip(refs[n_in:], vals)):
            if o in acc:
                if first is None:
                    first = pl.program_id(0) == 0
                    for a in range(1, nd):
                        first = jnp.logical_and(first, pl.program_id(a) == 0)

                @pl.when(first)
                def _(ref=ref, v=v):
                    ref[...] = v.astype(ref.dtype)

                @pl.when(jnp.logical_not(first))
                def _(ref=ref, v=v):
                    ref[...] += v.astype(ref.dtype)
            else:
                ref[...] = v.astype(ref.dtype)

    res = _pcall(body, name=name, grid=grid,
                 in_specs=[pl.BlockSpec(b, im) for _, b, im in ins],
                 out_specs=[pl.BlockSpec(b, im) for _, _, b, im in outs],
                 out_shape=[_sds(s, d) for s, d, _, _ in outs])(*[a for a, _, _ in ins])
    return list(res)


def _rows(a, tr):
    return (a, (tr, a.shape[1]), lambda i, *_: (i, 0))


def _const(a):
    nd = a.ndim
    return (a, a.shape, lambda *_: (0,) * nd)


def _cast_bf16(w, name):
    r, c = w.shape
    tr = _pick(r, (256, 128, 64, 16))
    return _tiled(lambda v: v, [_rows(w, tr)], [((r, c), BF16, (tr, c), lambda i: (i, 0))], (r // tr,), name)[0]


def _f_norm_mod(x, g, sh, sc):
    y = x * lax.rsqrt(jnp.mean(x * x, axis=-1, keepdims=True) + EPS) * g
    return y * (1.0 + sc) + sh


def _rope_partner_impl(y):
    nf = HEAD_DIM // 4
    lane = lax.broadcasted_iota(jnp.int32, y.shape, 1)
    return jnp.where(lane % (2 * nf) < nf, pltpu.roll(y, HEAD_DIM - nf, 1), pltpu.roll(y, nf, 1))


_rope_partner = jax.custom_vjp(_rope_partner_impl)
_rope_partner.defvjp(lambda y: (_rope_partner_impl(y), None), lambda _, ct: (_rope_partner_impl(ct),))


def _f_qk(t, g, cos, sin):
    y = t * lax.rsqrt(jnp.mean(t * t, axis=-1, keepdims=True) + EPS) * g
    return y * cos + _rope_partner(y) * sin


def _f_readout(of, ob, gate, g):
    o = of + ob
    on = o * lax.rsqrt(jnp.mean(o * o, axis=-1, keepdims=True) + EPS) * g
    return on * (gate * jax.nn.sigmoid(gate))


def _f_merge(ga, gb, pa, pb):
    return jax.nn.sigmoid(ga) * pa + jax.nn.sigmoid(gb) * pb


def _f_resid(x, gate, m):
    return x + gate * m


def _norm_mod_fwd(xcat, g, mods, n_ctx_tiles, tr):
    tt, d = xcat.shape
    which = lambda i: (jnp.where(i >= n_ctx_tiles, 1, 0), 0, 0, 0)

    def fn(x, gg, md):
        return _f_norm_mod(x, gg, md[0], md[1])

    return _tiled(fn, [_rows(xcat, tr), _const(g), (mods, (None, 2, 1, d), which)],
                  [((tt, d), BF16, (tr, d), lambda i: (i, 0))], (tt // tr,), "norm_mod_fwd")[0]


def _norm_mod_bwd(xcat, g, mods, dh, extra, n_ctx_tiles, tr, name):
    tt, d = xcat.shape
    nt = tt // tr
    has_extra = extra is not None

    def body(*refs):
        if has_extra:
            x_ref, g_ref, m_ref, dh_ref, e_ref, dx_ref, dg_ref, dm_ref = refs
        else:
            x_ref, g_ref, m_ref, dh_ref, dx_ref, dg_ref, dm_ref = refs
        i = pl.program_id(0)
        md = m_ref[...]
        _, vjp = jax.vjp(_f_norm_mod, x_ref[...], g_ref[...], md[0], md[1])
        dx, dg, dsh, dsc = vjp(dh_ref[...].astype(F32))
        dx_ref[...] = dx + e_ref[...] if has_extra else dx

        @pl.when(i == 0)
        def _():
            dg_ref[...] = dg

        @pl.when(i > 0)
        def _():
            dg_ref[...] += dg

        fresh = jnp.logical_or(i == 0, i == n_ctx_tiles)

        @pl.when(fresh)
        def _():
            dm_ref[0] = dsh
            dm_ref[1] = dsc

        @pl.when(jnp.logical_not(fresh))
        def _():
            dm_ref[0] += dsh
            dm_ref[1] += dsc

    which = lambda i: (jnp.where(i >= n_ctx_tiles, 1, 0), 0, 0, 0)
    row = pl.BlockSpec((tr, d), lambda i: (i, 0))
    in_specs = [row, pl.BlockSpec((1, d), lambda i: (0, 0)), pl.BlockSpec((None, 2, 1, d), which), row]
    args = [xcat, g, mods, dh]
    if has_extra:
        in_specs.append(row)
        args.append(extra)
    return _pcall(body, name=name, grid=(nt,), in_specs=in_specs,
                  out_specs=[row, pl.BlockSpec((1, d), lambda i: (0, 0)), pl.BlockSpec((None, 2, 1, d), which)],
                  out_shape=[_sds((tt, d), F32), _sds((1, d), F32), _sds((2, 2, 1, d), F32)])(*args)


def _hgrn_tri(reverse):
    t = np.arange(HGRN_BLOCK)
    tri = (t[None, :] >= t[:, None]) if reverse else (t[None, :] <= t[:, None])
    tri = tri.astype(np.float32)
    return jnp.asarray(tri), jnp.asarray(tri.T.copy())


def _hgrn_rowblock(n, n_ctx_blocks, n_blocks, reverse):
    if not reverse:
        return n
    return jnp.where(n < n_ctx_blocks, n_ctx_blocks - 1 - n, n_blocks - 1 - n + n_ctx_blocks)


def _hgrn_gates(fl, lb):
    sg = jax.nn.sigmoid(fl)
    f = lb + (1.0 - lb) * sg
    return sg, f, jnp.log(f), 1.0 - f


def _hgrn_intra_mask(reverse):
    tio = lax.broadcasted_iota(jnp.int32, (HGRN_BLOCK, HEAD_DIM), 0)
    return (lambda s: tio <= s) if reverse else (lambda s: tio >= s)


class _Halves:
    def __init__(self, reverse):
        self.reverse = reverse
        self.h = HGRN_BLOCK // 2
        tio = lax.broadcasted_iota(jnp.int32, (self.h, HEAD_DIM), 0)
        self.tio = tio if reverse else tio + self.h

    def is_half(self, s):
        return s < self.h if self.reverse else s >= self.h

    def rows(self, a):
        return a[:self.h] if self.reverse else a[self.h:]

    def mask(self, s):
        return self.tio <= s if self.reverse else self.tio >= s

    def widen(self, full, half):
        z = jnp.zeros_like(half)
        return full + jnp.concatenate([half, z] if self.reverse else [z, half], axis=0)


def _unrolled(nblk, u, fn, init):
    assert nblk % u == 0

    def trip(b, c):
        for j in range(u):
            c = fn(b * u + j, j, c)
        return c

    return lax.fori_loop(0, nblk // u, trip, init)


HGRN_UNROLL_FWD = (4, 8)
HGRN_UNROLL_BWD = (8, 4)


def _hgrn_fwd(p, lb, seg_f, reverse, n_ctx_rows):
    tt = p.shape[0]
    hb = HGRN_BLOCK
    nblk, nctx = tt // hb, n_ctx_rows // hb
    tri, _ = _hgrn_tri(reverse)
    u1, u3 = HGRN_UNROLL_FWD

    def body(q_ref, f_ref, v_ref, lb_ref, tri_ref, o_ref, st_all, dec_all, cum_out, qe_buf, cum_blk, k_blk, v_blk):
        mask = _hgrn_intra_mask(reverse)
        hv = _Halves(reverse)
        lbv = lb_ref[...]

        def phase1(n, slot, c):
            r0 = pl.multiple_of(_hgrn_rowblock(n, nctx, nblk, reverse) * hb, hb)
            q, v = q_ref[pl.ds(r0, hb), :], v_ref[pl.ds(r0, hb), :]
            _, f, g, k = _hgrn_gates(f_ref[pl.ds(r0, hb), :], lbv)
            cum = jnp.dot(tri_ref[...], g, precision=HIGHEST, preferred_element_type=F32)
            tot = jnp.sum(g, axis=0, keepdims=True)
            cum_blk[slot] = cum
            k_blk[slot] = k
            v_blk[slot] = v
            oi = jnp.zeros((hb, HEAD_DIM), F32)
            oi_h = jnp.zeros((hb // 2, HEAD_DIM), F32)
            q_h, cum_h = hv.rows(q), hv.rows(cum)
            for s in range(hb):
                if hv.is_half(s):
                    e = jnp.where(hv.mask(s), jnp.exp(jnp.minimum(cum_h - cum_blk[slot, s:s + 1, :], 0.0)), 0.0)
                    a_s = jnp.sum(q_h * e * k_blk[slot, s:s + 1, :], axis=-1, keepdims=True)
                    oi_h = oi_h + a_s * v_blk[slot, s:s + 1, :]
                else:
                    e = jnp.where(mask(s), jnp.exp(jnp.minimum(cum - cum_blk[slot, s:s + 1, :], 0.0)), 0.0)
                    a_s = jnp.sum(q * e * k_blk[slot, s:s + 1, :], axis=-1, keepdims=True)
                    oi = oi + a_s * v_blk[slot, s:s + 1, :]
            o_ref[pl.ds(r0, hb), :] = hv.widen(oi, oi_h)
            cum_out[pl.ds(r0, hb), :] = cum
            qe_buf[pl.ds(r0, hb), :] = q * jnp.exp(cum)
            kl = k * jnp.exp(tot - cum)
            st_all[n] = lax.dot_general(v.astype(BF16), kl.astype(BF16), (((0,), (0,)), ((), ())), preferred_element_type=F32)
            dec_all[pl.ds(n, 1), :] = jnp.exp(tot)
            return c

        _unrolled(nblk, u1, phase1, 0)

        def phase2(n, st):
            kv = st_all[n]
            st_all[n] = st
            return st * dec_all[pl.ds(n, 1), :] + kv

        lax.fori_loop(0, nblk, phase2, jnp.zeros((HEAD_DIM, HEAD_DIM), F32))

        def phase3(n, slot, c):
            r0 = pl.multiple_of(_hgrn_rowblock(n, nctx, nblk, reverse) * hb, hb)
            o_ref[pl.ds(r0, hb), :] += lax.dot_general(qe_buf[pl.ds(r0, hb), :].astype(BF16), st_all[n].astype(BF16),
                                                        (((1,), (1,)), ((), ())), preferred_element_type=F32)
            return c

        _unrolled(nblk, u3, phase3, 0)

    col = lambda seg: pl.BlockSpec((tt, HEAD_DIM), lambda h, seg=seg: (0, seg * N_HEADS + h))
    blk = pltpu.VMEM((u1, hb, HEAD_DIM), F32)
    head = pl.BlockSpec((tt, HEAD_DIM), lambda h: (0, h))
    return _pcall(
        body, name="hgrn_fwd_rev" if reverse else "hgrn_fwd", grid=(N_HEADS,),
        in_specs=[col(0), col(seg_f), col(3), pl.BlockSpec((1, HEAD_DIM), lambda h: (0, h)), pl.BlockSpec((hb, hb), lambda h: (0, 0))],
        out_specs=[head, pl.BlockSpec((None, nblk, HEAD_DIM, HEAD_DIM), lambda h: (h, 0, 0, 0)),
                   pl.BlockSpec((None, nblk, HEAD_DIM), lambda h: (h, 0, 0)), head],
        out_shape=[_sds((tt, N_HEADS * HEAD_DIM), F32), _sds((N_HEADS, nblk, HEAD_DIM, HEAD_DIM), F32),
                   _sds((N_HEADS, nblk, HEAD_DIM), F32), _sds((tt, N_HEADS * HEAD_DIM), F32)],
        scratch=[pltpu.VMEM((tt, HEAD_DIM), F32), blk, blk, blk])(p, p, p, lb, tri)


def _hgrn_bwd(p, lb, do, saved, seg_f, reverse, n_ctx_rows, prev):
    tt = p.shape[0]
    hb = HGRN_BLOCK
    nblk, nctx = tt // hb, n_ctx_rows // hb
    _, tri_t = _hgrn_tri(reverse)
    last_row = 0 if reverse else hb - 1
    has_prev = prev is not None
    u1, u3 = HGRN_UNROLL_BWD

    def body(*refs):
        q_ref, f_ref, v_ref, lb_ref, trit_ref, do_ref, st_all, dec_all, cum_buf = refs[:9]
        refs = refs[9:]
        if has_prev:
            pq_ref, pv_ref = refs[:2]
            refs = refs[2:]
        dq_ref, dfl_ref, dv_ref, dlb_ref, dd_all, cum_blk, k_blk, v_blk, dk_blk, dv_blk = refs
        mask = _hgrn_intra_mask(reverse)
        hv = _Halves(reverse)
        lbv = lb_ref[...]
        tio = lax.broadcasted_iota(jnp.int32, (hb, HEAD_DIM), 0)

        def rows_of(n):
            rb = _hgrn_rowblock(n, nctx, nblk, reverse)
            return rb, pl.multiple_of(rb * hb, hb)

        def load_do(rb):
            lat0 = pl.multiple_of(jnp.maximum(rb - nctx, 0) * hb, hb)
            return jnp.where(rb >= nctx, do_ref[pl.ds(lat0, hb), :], 0.0)

        def phase1(n, slot, c):
            rb, r0 = rows_of(n)
            qe = q_ref[pl.ds(r0, hb), :] * jnp.exp(cum_buf[pl.ds(r0, hb), :])
            dd_all[n] = lax.dot_general(load_do(rb).astype(BF16), qe.astype(BF16), (((0,), (0,)), ((), ())), preferred_element_type=F32)
            return c

        _unrolled(nblk, u1, phase1, 0)

        def phase2r(i, dst):
            n = nblk - 1 - i
            u = dd_all[n]
            dd_all[n] = dst
            return u + dst * dec_all[pl.ds(n, 1), :]

        lax.fori_loop(0, nblk, phase2r, jnp.zeros((HEAD_DIM, HEAD_DIM), F32))

        def phase3(n, slot, dlb):
            rb, r0 = rows_of(n)
            q, v = q_ref[pl.ds(r0, hb), :], v_ref[pl.ds(r0, hb), :]
            sg, f, g, k = _hgrn_gates(f_ref[pl.ds(r0, hb), :], lbv)
            cum = cum_buf[pl.ds(r0, hb), :]
            tot = jnp.sum(g, axis=0, keepdims=True)
            dob = load_do(rb)
            st, dst = st_all[n], dd_all[n]
            e_cum = jnp.exp(cum)
            e_rest = jnp.exp(tot - cum)
            dq = jnp.dot(dob.astype(BF16), st.astype(BF16), preferred_element_type=F32) * e_cum
            dk_inter = jnp.dot(v.astype(BF16), dst.astype(BF16), preferred_element_type=F32) * e_rest
            dv = lax.dot_general((k * e_rest).astype(BF16), dst.astype(BF16), (((1,), (1,)), ((), ())), preferred_element_type=F32)
            cum_blk[slot] = cum
            k_blk[slot] = k
            v_blk[slot] = v
            dq_h = jnp.zeros((hb // 2, HEAD_DIM), F32)
            q_h, cum_h, dob_h = hv.rows(q), hv.rows(cum), hv.rows(dob)
            for s in range(hb):
                half = hv.is_half(s)
                qs, cs, ds = (q_h, cum_h, dob_h) if half else (q, cum, dob)
                e = jnp.where(hv.mask(s) if half else mask(s), jnp.exp(jnp.minimum(cs - cum_blk[slot, s:s + 1, :], 0.0)), 0.0)
                a_s = jnp.sum(qs * e * k_blk[slot, s:s + 1, :], axis=-1, keepdims=True)
                da_s = jnp.sum(ds * v_blk[slot, s:s + 1, :], axis=-1, keepdims=True)
                gs = da_s * e
                if half:
                    dq_h = dq_h + gs * k_blk[slot, s:s + 1, :]
                else:
                    dq = dq + gs * k_blk[slot, s:s + 1, :]
                dk_blk[slot, s:s + 1, :] = jnp.sum(gs * qs, axis=0, keepdims=True)
                dv_blk[slot, s:s + 1, :] = jnp.sum(a_s * ds, axis=0, keepdims=True)
            dq = hv.widen(dq, dq_h)
            dk = dk_inter + dk_blk[slot]
            dv = dv + dv_blk[slot]
            d_tot = jnp.sum(k * dk_inter, axis=0, keepdims=True) + jnp.exp(tot) * jnp.sum(dst * st, axis=0, keepdims=True)
            dcum = q * dq - k * dk + jnp.where(tio == last_row, d_tot, 0.0)
            dg = jnp.dot(trit_ref[...], dcum, precision=HIGHEST, preferred_element_type=F32)
            df = dg / f - dk
            if has_prev:
                dq = dq + pq_ref[pl.ds(r0, hb), :].astype(F32)
                dv = dv + pv_ref[pl.ds(r0, hb), :].astype(F32)
            dq_ref[pl.ds(r0, hb), :] = dq.astype(dq_ref.dtype)
            dv_ref[pl.ds(r0, hb), :] = dv.astype(dv_ref.dtype)
            dfl_ref[pl.ds(r0, hb), :] = (df * (1.0 - lbv) * sg * (1.0 - sg)).astype(dfl_ref.dtype)
            return dlb + jnp.sum(df * (1.0 - sg), axis=0, keepdims=True)

        dlb_ref[...] = _unrolled(nblk, u3, phase3, jnp.zeros((1, HEAD_DIM), F32))

    col = lambda seg: pl.BlockSpec((tt, HEAD_DIM), lambda h, seg=seg: (0, seg * N_HEADS + h))
    head = pl.BlockSpec((tt, HEAD_DIM), lambda h: (0, h))
    lbs = pl.BlockSpec((1, HEAD_DIM), lambda h: (0, h))
    tris = pl.BlockSpec((hb, hb), lambda h: (0, 0))
    in_specs = [col(0), col(seg_f), col(3), lbs, tris, pl.BlockSpec((do.shape[0], HEAD_DIM), lambda h: (0, h)),
                pl.BlockSpec((None, nblk, HEAD_DIM, HEAD_DIM), lambda h: (h, 0, 0, 0)), pl.BlockSpec((None, nblk, HEAD_DIM), lambda h: (h, 0, 0)), head]
    args = [p, p, p, lb, tri_t, do, *saved]
    mid = F32 if not has_prev else BF16
    if has_prev:
        in_specs += [head, head]
        args += list(prev)
    w = N_HEADS * HEAD_DIM
    blk = pltpu.VMEM((u3, hb, HEAD_DIM), F32)
    return _pcall(
        body, name="hgrn_bwd_rev" if reverse else "hgrn_bwd", grid=(N_HEADS,), in_specs=in_specs,
        out_specs=[head, head, head, lbs],
        out_shape=[_sds((tt, w), mid), _sds((tt, w), BF16), _sds((tt, w), mid), _sds((1, w), F32)],
        scratch=[pltpu.VMEM((nblk, HEAD_DIM, HEAD_DIM), F32), blk, blk, blk, blk, blk])(*args)


NA_HEADS_PER_STEP = 4


def _na_geometry(rows):
    r = pl.program_id(1)
    rs = jnp.clip(r - WIN_R // 2, 0, rows - WIN_R)
    return r, rs, r - rs


def _na_scores(q, kb, kc, bias):
    scale = HEAD_DIM ** -0.5
    nt = (((1,), (1,)), ((), ()))
    sb = lax.dot_general(q, kb, nt, preferred_element_type=F32) * scale + bias
    sc = lax.dot_general(q, kc, nt, preferred_element_type=F32) * scale
    m = jnp.maximum(jnp.max(sb, axis=-1, keepdims=True), jnp.max(sc, axis=-1, keepdims=True))
    pb, pc = jnp.exp(sb - m), jnp.exp(sc - m)
    inv = 1.0 / (jnp.sum(pb, axis=-1, keepdims=True) + jnp.sum(pc, axis=-1, keepdims=True))
    return pb * inv, pc * inv


def _na_fwd(qn, kall, p, seg_v, bias, n_ctx_rows):
    t, tt = qn.shape[0], kall.shape[0]
    rows = t // GRID_W
    nband = WIN_R * GRID_W

    nh = NA_HEADS_PER_STEP
    wide = nh * HEAD_DIM

    def body(q_ref, k_ref, v_ref, b_ref, o_ref):
        r, rs, _ = _na_geometry(rows)
        k0 = pl.multiple_of(n_ctx_rows + rs * GRID_W, GRID_W)
        for j in range(nh):
            sl = slice(j * HEAD_DIM, (j + 1) * HEAD_DIM)
            q = q_ref[:, sl]
            kb, kc = k_ref[pl.ds(k0, nband), sl], k_ref[pl.ds(0, n_ctx_rows), sl]
            vb, vc = v_ref[pl.ds(k0, nband), sl].astype(BF16), v_ref[pl.ds(0, n_ctx_rows), sl].astype(BF16)
            pb, pc = _na_scores(q, kb, kc, b_ref[j])
            o = jnp.dot(pb.astype(BF16), vb, preferred_element_type=F32) + jnp.dot(pc.astype(BF16), vc, preferred_element_type=F32)
            o_ref[:, sl] = o.astype(o_ref.dtype)

    variant = lambda h, r: (h, r - jnp.clip(r - WIN_R // 2, 0, rows - WIN_R), 0, 0)
    return _pcall(
        body, name="na_fwd", grid=(N_HEADS // nh, rows),
        in_specs=[pl.BlockSpec((GRID_W, wide), lambda h, r: (r, h)),
                  pl.BlockSpec((tt, wide), lambda h, r: (0, h)),
                  pl.BlockSpec((tt, wide), lambda h, r: (0, seg_v * (N_HEADS // nh) + h)),
                  pl.BlockSpec((nh, None, GRID_W, nband), variant)],
        out_specs=pl.BlockSpec((GRID_W, wide), lambda h, r: (r, h)),
        out_shape=_sds((t, N_HEADS * HEAD_DIM), BF16))(qn, kall, p, bias)


def _na_bwd(qn, kall, p, seg_v, bias, do, n_ctx_rows):
    t, tt = qn.shape[0], kall.shape[0]
    rows = t // GRID_W
    nband = WIN_R * GRID_W
    scale = HEAD_DIM ** -0.5
    tn = (((0,), (0,)), ((), ()))
    nt = (((1,), (1,)), ((), ()))

    nh = NA_HEADS_PER_STEP
    wide = nh * HEAD_DIM

    def body(q_ref, k_ref, v_ref, b_ref, do_ref, dq_ref, dk_ref, dv_ref, db_ref, dv_acc):
        r, rs, var = _na_geometry(rows)
        k0 = pl.multiple_of(n_ctx_rows + rs * GRID_W, GRID_W)
        fresh = jnp.logical_or(r <= WIN_R // 2, r > rows - WIN_R // 2)

        @pl.when(r == 0)
        def _():
            dk_ref[...] = jnp.zeros_like(dk_ref)
            dv_acc[...] = jnp.zeros_like(dv_acc)

        for j in range(nh):
            sl = slice(j * HEAD_DIM, (j + 1) * HEAD_DIM)
            q = q_ref[:, sl]
            kb, kc = k_ref[pl.ds(k0, nband), sl], k_ref[pl.ds(0, n_ctx_rows), sl]
            vb, vc = v_ref[pl.ds(k0, nband), sl].astype(BF16), v_ref[pl.ds(0, n_ctx_rows), sl].astype(BF16)
            pb, pc = _na_scores(q, kb, kc, b_ref[j])
            dof = do_ref[:, sl].astype(F32)
            dob = dof.astype(BF16)
            o = jnp.dot(pb.astype(BF16), vb, preferred_element_type=F32) + jnp.dot(pc.astype(BF16), vc, preferred_element_type=F32)
            delta = jnp.sum(dof * o, axis=-1, keepdims=True)
            dsb = pb * (lax.dot_general(dob, vb, nt, preferred_element_type=F32) - delta)
            dsc = pc * (lax.dot_general(dob, vc, nt, preferred_element_type=F32) - delta)
            dsb16, dsc16 = dsb.astype(BF16), dsc.astype(BF16)
            dq_ref[:, sl] = (jnp.dot(dsb16, kb, preferred_element_type=F32) + jnp.dot(dsc16, kc, preferred_element_type=F32)) * scale
            dk_ref[pl.ds(k0, nband), sl] += lax.dot_general(dsb16, q, tn, preferred_element_type=F32) * scale
            dk_ref[pl.ds(0, n_ctx_rows), sl] += lax.dot_general(dsc16, q, tn, preferred_element_type=F32) * scale
            dv_acc[pl.ds(k0, nband), sl] += lax.dot_general(pb.astype(BF16), dob, tn, preferred_element_type=F32)
            dv_acc[pl.ds(0, n_ctx_rows), sl] += lax.dot_general(pc.astype(BF16), dob, tn, preferred_element_type=F32)

            @pl.when(fresh)
            def _(j=j, dsb=dsb):
                db_ref[j] = dsb

            @pl.when(jnp.logical_not(fresh))
            def _(j=j, dsb=dsb):
                db_ref[j] += dsb

        @pl.when(r == rows - 1)
        def _():
            dv_ref[...] = dv_acc[...].astype(dv_ref.dtype)

    variant = lambda h, r: (h, r - jnp.clip(r - WIN_R // 2, 0, rows - WIN_R), 0, 0)
    head_all = pl.BlockSpec((tt, wide), lambda h, r: (0, h))
    qspec = pl.BlockSpec((GRID_W, wide), lambda h, r: (r, h))
    w = N_HEADS * HEAD_DIM
    return _pcall(
        body, name="na_bwd", grid=(N_HEADS // nh, rows),
        in_specs=[qspec, head_all, pl.BlockSpec((tt, wide), lambda h, r: (0, seg_v * (N_HEADS // nh) + h)),
                  pl.BlockSpec((nh, None, GRID_W, nband), variant), qspec],
        out_specs=[qspec, head_all, head_all, pl.BlockSpec((nh, None, GRID_W, nband), variant)],
        out_shape=[_sds((t, w), F32), _sds((tt, w), F32), _sds((tt, w), BF16), _sds((N_HEADS, WIN_R, GRID_W, nband), F32)],
        scratch=[pltpu.VMEM((tt, wide), F32)])(qn, kall, p, bias, do)


def _na_tables(t, n_ctx_rows):
    half, nf = HEAD_DIM // 2, HEAD_DIM // 4
    pos = np.arange(t)
    lane = np.arange(HEAD_DIM)
    inv = ROPE_THETA ** (-(np.arange(nf, dtype=np.float32)) / nf)
    which = np.where(lane < half, pos[:, None] // GRID_W, pos[:, None] % GRID_W).astype(np.float32)
    ang = which * inv[lane % nf][None, :]
    first = (lane % half) < nf
    cos = np.concatenate([np.ones((n_ctx_rows, HEAD_DIM), np.float32), np.cos(ang).astype(np.float32)])
    sin = np.concatenate([np.zeros((n_ctx_rows, HEAD_DIM), np.float32), np.where(first[None, :], -np.sin(ang), np.sin(ang)).astype(np.float32)])
    w = np.arange(GRID_W)
    dc = np.clip(w[None, :] - w[:, None], -(WIN_C - 1), WIN_C - 1) + WIN_C - 1
    onehot = np.zeros((32, GRID_W * GRID_W), np.float32)
    onehot[dc.reshape(-1), np.arange(GRID_W * GRID_W)] = 1.0
    cs = np.clip(w - WIN_C // 2, 0, GRID_W - WIN_C)
    col_in = (w[None, :] >= cs[:, None]) & (w[None, :] < cs[:, None] + WIN_C)
    onehot *= col_in.reshape(1, -1)
    neg = np.where(col_in, 0.0, NEG_BIG).astype(np.float32)
    return jnp.asarray(cos), jnp.asarray(sin), jnp.asarray(onehot), jnp.asarray(neg)


def _bias_slabs(rel_bias, onehot, neg):
    nr = 2 * WIN_R - 1
    rb = jnp.pad(rel_bias.reshape(N_HEADS * nr, 2 * WIN_C - 1), ((0, 0), (0, 1)))
    spread = _mm_f32(rb, onehot, "bias_spread").reshape(N_HEADS, nr, GRID_W, GRID_W)
    slabs = [spread[:, WIN_R - 1 - v:2 * WIN_R - 1 - v] for v in range(WIN_R)]
    b = jnp.stack(slabs, axis=1) + neg[None, None, None]
    return b.transpose(0, 1, 3, 2, 4).reshape(N_HEADS, WIN_R, GRID_W, WIN_R * GRID_W)


def _bias_grad(dbias, onehot):
    nr = 2 * WIN_R - 1
    d = dbias.reshape(N_HEADS, WIN_R, GRID_W, WIN_R, GRID_W).transpose(0, 1, 3, 2, 4)
    tot = jnp.zeros((N_HEADS, nr, GRID_W, GRID_W), F32)
    for v in range(WIN_R):
        tot = tot + jnp.pad(d[:, v], ((0, 0), (WIN_R - 1 - v, v), (0, 0), (0, 0)))
    g = _mm_f32(tot.reshape(N_HEADS * nr, GRID_W * GRID_W), onehot, "bias_grad", trans_b=True)
    return g[:, :2 * WIN_C - 1].reshape(1, N_HEADS, nr, 2 * WIN_C - 1)


def _shift_rows(u, up):
    n = u.shape[0]
    tio = lax.broadcasted_iota(jnp.int32, u.shape, 0)
    if up:
        return jnp.where(tio == n - 1, 0.0, pltpu.roll(u, n - 1, 0))
    return jnp.where(tio == 0, 0.0, pltpu.roll(u, 1, 0))


def _conv3(u, w_ref, b_ref):
    um, up = _shift_rows(u, False), _shift_rows(u, True)
    return um, up, um * w_ref[0:1, :] + u * w_ref[1:2, :] + up * w_ref[2:3, :] + b_ref[...]


def _ffn_act_fwd(u0, t3, cw, cb):
    t, n = u0.shape
    tc = _pick(n, (256, 128))

    def body(u_ref, t_ref, w_ref, b_ref, a_ref):
        _, _, uc = _conv3(u_ref[...], w_ref, b_ref)
        a_ref[...] = (uc * jax.nn.sigmoid(uc) * t_ref[...]).astype(a_ref.dtype)

    col = lambda rows_: pl.BlockSpec((rows_, tc), lambda j: (0, j))
    return _pcall(body, name="ffn_act_fwd", grid=(n // tc,), in_specs=[col(t), col(t), col(8), col(1)], out_specs=col(t),
                  out_shape=_sds((t, n), BF16))(u0, t3, cw, cb)


def _ffn_act_bwd(u0, t3, cw, cb, da):
    t, n = u0.shape
    tc = _pick(n, (256, 128))

    def body(u_ref, t_ref, w_ref, b_ref, da_ref, du_ref, dt_ref, dw_ref, db_ref):
        u = u_ref[...]
        um, up, uc = _conv3(u, w_ref, b_ref)
        sg = jax.nn.sigmoid(uc)
        dav = da_ref[...].astype(F32)
        dt_ref[...] = (dav * uc * sg).astype(dt_ref.dtype)
        duc = dav * t_ref[...] * sg * (1.0 + uc * (1.0 - sg))
        du = _shift_rows(duc, True) * w_ref[0:1, :] + duc * w_ref[1:2, :] + _shift_rows(duc, False) * w_ref[2:3, :]
        du_ref[...] = du.astype(du_ref.dtype)
        dw_ref[...] = jnp.zeros_like(dw_ref)
        dw_ref[0:1, :] = jnp.sum(duc * um, axis=0, keepdims=True)
        dw_ref[1:2, :] = jnp.sum(duc * u, axis=0, keepdims=True)
        dw_ref[2:3, :] = jnp.sum(duc * up, axis=0, keepdims=True)
        db_ref[...] = jnp.sum(duc, axis=0, keepdims=True)

    col = lambda rows_: pl.BlockSpec((rows_, tc), lambda j: (0, j))
    return _pcall(body, name="ffn_act_bwd", grid=(n // tc,), in_specs=[col(t), col(t), col(8), col(1), col(t)],
                  out_specs=[col(t), col(t), col(8), col(1)],
                  out_shape=[_sds((t, n), BF16), _sds((t, n), BF16), _sds((8, n), F32), _sds((1, n), F32)])(u0, t3, cw, cb, da)


def _adam_math(g, w, m, v):
    m2 = ADAM_B1 * m + (1.0 - ADAM_B1) * g
    v2 = ADAM_B2 * v + (1.0 - ADAM_B2) * (g * g)
    m_hat = m2 / (1.0 - ADAM_B1 ** ADAM_STEP)
    v_hat = v2 / (1.0 - ADAM_B2 ** ADAM_STEP)
    return -ADAM_LR * (m_hat / (jnp.sqrt(v_hat) + ADAM_EPS) + ADAM_WD * w), m2, v2


def _adam_big(parts, w, m, v, name):
    r, c = w.shape
    npart, _, cp = parts.shape
    tr = _pick(r, (64, 32, 16, 8))

    def body(p_ref, w_ref, m_ref, v_ref, g_ref, d_ref, m2_ref, v2_ref):
        g = p_ref[0, :, 0:c].astype(F32)
        for i in range(1, npart):
            g = g + p_ref[i, :, 0:c].astype(F32)
        d, m2, v2 = _adam_math(g, w_ref[...], m_ref[...], v_ref[...])
        g_ref[...] = g
        d_ref[...] = d
        m2_ref[...] = m2
        v2_ref[...] = v2

    row = pl.BlockSpec((tr, c), lambda i: (i, 0))
    return _pcall(body, name=name, grid=(r // tr,),
                  in_specs=[pl.BlockSpec((npart, tr, cp), lambda i: (0, i, 0)), row, row, row],
                  out_specs=[row] * 4, out_shape=[_sds((r, c), F32)] * 4)(parts, w, m, v)


def _adam_small(g, w, m, v):
    def body(g_ref, w_ref, m_ref, v_ref, d_ref, m2_ref, v2_ref):
        d_ref[...], m2_ref[...], v2_ref[...] = _adam_math(g_ref[...], w_ref[...], m_ref[...], v_ref[...])

    return _pcall(body, name="adam_small", out_shape=[_sds(g.shape, F32)] * 3)(g, w, m, v)


def _sum_parts(parts, name):
    def body(p_ref, o_ref):
        s = p_ref[0]
        for i in range(1, N_DEV):
            s = s + p_ref[i]
        o_ref[...] = s

    return _pcall(body, name=name, out_shape=_sds(parts.shape[1:], F32))(parts)


class _Pack:
    def __init__(self, shapes):
        self.shapes = shapes
        self.sizes = [int(np.prod(s)) for s in shapes]
        self.padded = [-(-n // (8 * LANE)) * 8 * LANE for n in self.sizes]
        self.offs = np.concatenate([[0], np.cumsum(self.padded)]).tolist()

    def pack(self, arrs):
        flat = [jnp.pad(a.reshape(-1).astype(F32), (0, p - n)) for a, n, p in zip(arrs, self.sizes, self.padded)]
        return jnp.concatenate(flat).reshape(-1, LANE)

    def unpack(self, slab):
        flat = slab.reshape(-1)
        return [flat[o:o + n].reshape(s) for o, n, s in zip(self.offs, self.sizes, self.shapes)]


def kernel(x, c, ctx, c_ctx, ada_w, ada_b, norm1_g, norm2_g, w_in, hgrn_lb_logits, hgrn_norm_g, na_q_norm_g, na_k_norm_g, na_rel_bias, w_branch_a, w_branch_b, w_out, ffn_w1, ffn_w3, ffn_conv_w, ffn_conv_b, ffn_w2, loss_target, m_c_ctx, m_ada_w, m_ada_b, m_norm1_g, m_norm2_g, m_w_in, m_hgrn_lb_logits, m_hgrn_norm_g, m_na_q_norm_g, m_na_k_norm_g, m_na_rel_bias, m_w_branch_a, m_w_branch_b, m_w_out, m_ffn_w1, m_ffn_w3, m_ffn_conv_w, m_ffn_conv_b, m_ffn_w2, v_c_ctx, v_ada_w, v_ada_b, v_norm1_g, v_norm2_g, v_w_in, v_hgrn_lb_logits, v_hgrn_norm_g, v_na_q_norm_g, v_na_k_norm_g, v_na_rel_bias, v_w_branch_a, v_w_branch_b, v_w_out, v_ffn_w1, v_ffn_w3, v_ffn_conv_w, v_ffn_conv_b, v_ffn_w2):
    t, d = x.shape[1], x.shape[2]
    n_ctx = ctx.shape[1]
    tt = n_ctx + t
    hw = N_HEADS * HEAD_DIM
    ci = w_in.shape[2]
    ca = ada_w.shape[2]
    ff_l = ffn_w1.shape[2]
    ff_p = -(-ff_l // LANE) * LANE
    rows = t // GRID_W
    assert rows >= WIN_R and t % GRID_W == 0 and n_ctx % GRID_W == 0 and ci % LANE == 0 and d % LANE == 0
    me = 4 * lax.axis_index("x") + 2 * lax.axis_index("y") + lax.axis_index("c")
    tr = _pick(n_ctx, (256, 128, 64))
    n_ctx_tiles = n_ctx // tr

    pad_c = lambda w: jnp.pad(w, ((0, 0), (0, ff_p - ff_l)))
    w_in16 = _cast_bf16(w_in[0], "cast_w_in")
    g_in, = _gather_sc([w_in16], "gather_w_in", 1)
    small_in = [_tie(c, w_in16), hgrn_lb_logits.reshape(4, HEAD_DIM), jnp.pad(ffn_conv_w[0], ((0, 5), (0, ff_p - ff_l)))]
    c_all, lb_parts, cw_all = _exchange(small_in, "gather_params", scatter=False)
    ff = N_DEV * ff_p

    cc = jnp.concatenate([c_all.reshape(N_DEV, d), jnp.broadcast_to(c_ctx[None, :], (N_DEV, d))], axis=0)
    act = _tiled(lambda v: v * jax.nn.sigmoid(v), [_const(cc)], [(cc.shape, BF16, cc.shape, lambda *_: (0, 0))], (), "silu_c")[0]
    ada16 = _cast_bf16(ada_w[0], "cast_ada")
    mod_cols = _mm_xw(act, ada16.reshape(1, d, ca), F32, "ada_fwd")
    mod_all = _exchange([mod_cols], "gather_mod", scatter=False)[0]

    rest16 = []
    for w_, nm in ((w_branch_a[0], "cast_w_a"), (w_branch_b[0], "cast_w_b"), (w_out[0], "cast_w_out"), (pad_c(ffn_w1[0]), "cast_w1"),
                   (pad_c(ffn_w3[0]), "cast_w3"), (jnp.pad(ffn_w2[0], ((0, ff_p - ff_l), (0, 0))), "cast_w2")):
        with _after(mod_all):
            rest16.append(_cast_bf16(w_, nm))
    g_a, g_b, g_out = _gather_sc(rest16[:3], "gather_mix", 2)
    g_w1, g_w3, g_w2 = _gather_sc(rest16[3:], "gather_ffn", 10)
    g_out = g_out.reshape(1, d, d)
    g_w2 = g_w2.reshape(1, ff, d)
    mod_all = mod_all.transpose(1, 0, 2).reshape(2 * N_DEV, N_MOD * d) + ada_b
    mod_l = lax.dynamic_slice_in_dim(mod_all, me, 1, axis=0).reshape(N_MOD, 1, d)
    mod_c = mod_all[N_DEV:N_DEV + 1].reshape(N_MOD, 1, d)
    mods1 = jnp.stack([mod_c[0:2], mod_l[0:2]])
    mods2 = jnp.stack([mod_l[3:5], mod_l[3:5]])
    gate1, gate2 = mod_l[2], mod_l[5]

    xcat = jnp.concatenate([ctx[0], x[0]], axis=0)
    hcat = _norm_mod_fwd(xcat, norm1_g, mods1, n_ctx_tiles, tr)
    p = _mm_xw(hcat, g_in, F32, "in_proj")
    lb_logits = lb_parts.transpose(1, 0, 2).reshape(2, 2, hw)
    lb_soft = _tiled(lambda a, b: (1.0 / (1.0 + jnp.exp(b - a)),), [_const(lb_logits[:, 0]), _const(lb_logits[:, 1])],
                     [((2, hw), F32, (2, hw), lambda *_: (0, 0))], (), "lb_softmax")[0]
    lb_f, lb_b = lb_soft[0:1], lb_soft[1:2]
    o_f, *saved_f = _hgrn_fwd(p, lb_f, 1, False, n_ctx)
    o_b, *saved_b = _hgrn_fwd(p, lb_b, 2, True, n_ctx)

    cos, sin, onehot, neg = _na_tables(t, n_ctx)
    bias = _bias_slabs(na_rel_bias[0], onehot, neg)
    heads = lambda a: [a[:, h * HEAD_DIM:(h + 1) * HEAD_DIM] for h in range(N_HEADS)]
    seg_rows = lambda seg, off=0: (p, (tr, hw), lambda i: (i + off, seg))
    wide = lambda a, off=0: (a, (tr, hw), lambda i: (i + off, 0))
    out_wide = lambda n_rows, dt: ((n_rows, hw), dt, (tr, hw), lambda i: (i, 0))
    acc_head = ((1, HEAD_DIM), F32, (1, HEAD_DIM), lambda i: (0, 0))
    lat0 = n_ctx // tr
    tab = lambda a, off=0: (a, (tr, HEAD_DIM), lambda i: (i + off, 0))

    def qk_fn(tv, g, cs, sn):
        return (jnp.concatenate([_f_qk(th, g, cs, sn) for th in heads(tv)], axis=1),)

    qn = _tiled(qk_fn, [seg_rows(5, lat0), _const(na_q_norm_g), tab(cos, lat0), tab(sin, lat0)], [out_wide(t, BF16)], (t // tr,), "q_norm_rope")[0]
    kall = _tiled(qk_fn, [seg_rows(6), _const(na_k_norm_g), tab(cos), tab(sin)], [out_wide(tt, BF16)], (tt // tr,), "k_norm_rope")[0]
    y_b = _na_fwd(qn, kall, p, 7, bias, n_ctx)

    def readout_fn(a, b, gt, g):
        return (jnp.concatenate([_f_readout(ah, bh, gh, g) for ah, bh, gh in zip(heads(a), heads(b), heads(gt))], axis=1),)

    y_a = _tiled(readout_fn, [wide(o_f, lat0), wide(o_b, lat0), seg_rows(4, lat0), _const(hgrn_norm_g)], [out_wide(t, BF16)], (t // tr,), "hgrn_readout")[0]

    p_a = _mm_xw(y_a, g_a, F32, "branch_a")
    p_b = _mm_xw(y_b, g_b, F32, "branch_b")
    td = _pick(d, (512, 256, 128))
    nd_t = d // td
    lat_r = n_ctx // tr
    gcol = lambda k: (p, (tr, td), lambda i, j, k=k: (i + lat_r, 8 * hw // td + k * nd_t + j))
    dtile = lambda a: (a, (tr, td), lambda i, j: (i, j))
    z = _tiled(lambda ga, gb, pa, pb: (_f_merge(ga, gb, pa, pb),), [gcol(0), gcol(1), dtile(p_a), dtile(p_b)],
               [((t, d), BF16, (tr, td), lambda i, j: (i, j))], (t // tr, nd_t), "merge")[0]
    mix = _mm_xw(z, g_out, F32, "out_proj")
    xl = x[0]
    x_mid = _tiled(lambda a, g, m_: (_f_resid(a, g, m_),), [_rows(xl, tr), _const(gate1), _rows(mix, tr)],
                   [((t, d), F32, (tr, d), lambda i: (i, 0))], (t // tr,), "resid1")[0]

    h2 = _norm_mod_fwd(x_mid, norm2_g, mods2, 0, tr)
    u0 = _mm_xw(h2, g_w1, F32, "ffn_up1")
    t3 = _mm_xw(h2, g_w3, F32, "ffn_up3")
    cw_full = cw_all.transpose(1, 0, 2).reshape(8, ff)
    cb_full = jnp.pad(ffn_conv_b.reshape(N_DEV, ff_l), ((0, 0), (0, ff_p - ff_l))).reshape(1, ff)
    a_act = _ffn_act_fwd(u0, t3, cw_full, cb_full)
    f_out = _mm_xw(a_act, g_w2, F32, "ffn_down")

    def loss_fn(xm, g, f, tg):
        err = xm + g * f - tg
        return err * (1.0 / d), jnp.sum(err * err, axis=0, keepdims=True) * (0.5 / d), jnp.sum(err * (1.0 / d) * f, axis=0, keepdims=True)

    dy, loss_cols, d_gate2 = _tiled(loss_fn, [_rows(x_mid, tr), _const(gate2), _rows(f_out, tr), _rows(loss_target[0], tr)],
                                    [((t, d), F32, (tr, d), lambda i: (i, 0)), ((1, d), F32, (1, d), lambda i: (0, 0)),
                                     ((1, d), F32, (1, d), lambda i: (0, 0))], (t // tr,), "loss", acc=(1, 2))

    df = _tiled(lambda a, g: (a * g,), [_rows(dy, tr), _const(gate2)], [((t, d), BF16, (tr, d), lambda i: (i, 0))], (t // tr,), "d_ffn_out")[0]
    d_w2 = _mm_xtdy(a_act, df, 1, BF16, "d_w2")
    da = _mm_dyw(df, g_w2, BF16, "d_act")
    du0, dt3, d_cw, d_cb = _ffn_act_bwd(u0, t3, cw_full, cb_full, da)
    d_w1 = _mm_xtdy(h2, du0, N_DEV, BF16, "d_w1")
    d_w3 = _mm_xtdy(h2, dt3, N_DEV, BF16, "d_w3")
    ffn_blocks = [d_w1, d_w3, d_w2.reshape(N_DEV, ff_p, d)]
    ffn_theirs = _pair_swap_sc(ffn_blocks, "scatter_ffn_pair", 3)
    dh2 = _mm_dyw(du0, g_w1, F32, "d_h2_a")
    dh2 = _mm_dyw(dt3, g_w3, F32, "d_h2_b", init=dh2)
    with _after(dh2):
        ffn_sums = [_pair_add(ffn_blocks[0], ffn_theirs[0], "pair_add_w1")]
    ffn_sums += [_pair_add(ffn_blocks[1], ffn_theirs[1], "pair_add_w3"), _pair_add(ffn_blocks[2], ffn_theirs[2], "pair_add_w2")]
    r_w1, r_w3, r_w2 = _chip_scatter_sc(ffn_sums, "scatter_ffn_chips", 4)
    with _after(*ffn_sums):
        dx_mid, d_norm2, d_mods2 = _norm_mod_bwd(x_mid, norm2_g, mods2, dh2, dy, 0, tr, "norm_mod2_bwd")

    dm, d_gate1 = _tiled(lambda dxm, g, m_: (dxm * g, jnp.sum(dxm * m_, axis=0, keepdims=True)),
                         [_rows(dx_mid, tr), _const(gate1), _rows(mix, tr)],
                         [((t, d), BF16, (tr, d), lambda i: (i, 0)), ((1, d), F32, (1, d), lambda i: (0, 0))], (t // tr,), "d_resid1", acc=(1,))
    d_wout = _mm_xtdy(z, dm, 1, BF16, "d_w_out")
    dz = _mm_dyw(dm, g_out, F32, "d_merge")

    def merge_bwd(ga, gb, pa, pb, dzv):
        _, vjp = jax.vjp(_f_merge, ga, gb, pa, pb)
        return vjp(dzv)

    dga, dgb, dpa, dpb = _tiled(merge_bwd, [gcol(0), gcol(1), dtile(p_a), dtile(p_b), dtile(dz)],
                                [((t, d), BF16, (tr, td), lambda i, j: (i, j))] * 4, (t // tr, nd_t), "merge_bwd")
    d_wa = _mm_xtdy(y_a, dpa, N_DEV, BF16, "d_w_a")
    d_wb = _mm_xtdy(y_b, dpb, N_DEV, BF16, "d_w_b")
    dy_a = _mm_dyw(dpa, g_a, F32, "d_y_a")
    dy_b = _mm_dyw(dpb, g_b, BF16, "d_y_b")
    dqn, dkall, dv_na, dbias = _na_bwd(qn, kall, p, 7, bias, dy_b, n_ctx)
    mix_blocks = [_tie(d_wa, r_w1, dqn), d_wb, d_wout.reshape(N_DEV, d // N_DEV, d)]
    mix_theirs = _pair_swap_sc(mix_blocks, "scatter_mix_pair", 5)

    def readout_bwd(a, b, gt, g, ct):
        das, dgts, dg = [], [], 0.0
        for ah, bh, gh, ch in zip(heads(a), heads(b), heads(gt), heads(ct)):
            _, vjp = jax.vjp(_f_readout, ah, bh, gh, g)
            da_h, _, dgt_h, dg_h = vjp(ch)
            das.append(da_h)
            dgts.append(dgt_h)
            dg = dg + dg_h
        return jnp.concatenate(das, axis=1), jnp.concatenate(dgts, axis=1), dg

    with _after(mix_blocks[0]):
        do_h, d_gate_o, d_hnorm = _tiled(
            readout_bwd, [wide(o_f, lat0), wide(o_b, lat0), seg_rows(4, lat0), _const(hgrn_norm_g), wide(dy_a)],
            [out_wide(t, F32), out_wide(t, BF16), acc_head], (t // tr,), "readout_bwd", acc=(2,))
    dq1, dfl_f, dv1, dlb_f = _hgrn_bwd(p, lb_f, do_h, saved_f, 1, False, n_ctx, None)
    with _after(dq1):
        mix_sums = [_pair_add(mix_blocks[0], mix_theirs[0], "pair_add_wa")]
    mix_sums += [_pair_add(mix_blocks[1], mix_theirs[1], "pair_add_wb"), _pair_add(mix_blocks[2], mix_theirs[2], "pair_add_wout")]
    r_a, r_b, r_out = _chip_scatter_sc(mix_sums, "scatter_mix_chips", 6)
    with _after(*mix_sums):
        dq_h, dfl_b, dv_h, dlb_b = _hgrn_bwd(p, lb_b, do_h, saved_b, 2, True, n_ctx, (dq1, dv1))

    def qk_bwd(tv, g, cs, sn, ct):
        dts, dg = [], 0.0
        for th, ch in zip(heads(tv), heads(ct)):
            _, vjp = jax.vjp(lambda a, b: _f_qk(a, b, cs, sn), th, g)
            dt_h, dg_h = vjp(ch)
            dts.append(dt_h)
            dg = dg + dg_h
        return jnp.concatenate(dts, axis=1), dg

    d_pq, d_qnorm = _tiled(qk_bwd, [seg_rows(5, lat0), _const(na_q_norm_g), tab(cos, lat0), tab(sin, lat0), wide(dqn)],
                           [out_wide(t, BF16), acc_head], (t // tr,), "q_norm_rope_bwd", acc=(1,))
    d_pk, d_knorm = _tiled(qk_bwd, [seg_rows(6), _const(na_k_norm_g), tab(cos), tab(sin), wide(dkall)],
                           [out_wide(tt, BF16), acc_head], (tt // tr,), "k_norm_rope_bwd", acc=(1,))

    zc = lambda w_: jnp.zeros((n_ctx, w_), BF16)
    lat_only = lambda a: jnp.concatenate([zc(a.shape[1]), a], axis=0)
    dp = jnp.concatenate([dq_h, dfl_f, dfl_b, dv_h, lat_only(d_gate_o), lat_only(d_pq), d_pk, dv_na, lat_only(dga), lat_only(dgb)], axis=1)
    d_win = _tie(_mm_xtdy(hcat, dp, N_DEV, BF16, "d_w_in"), r_a)
    win_theirs, = _pair_swap_sc([d_win], "scatter_w_in_pair", 7)
    with _after(d_win):
        dhcat = _mm_dyw(dp, g_in, BF16, "d_hcat")
    zero_ctx = jnp.concatenate([jnp.zeros((n_ctx, d), F32), dx_mid], axis=0)
    dxcat, d_norm1, d_mods1 = _norm_mod_bwd(xcat, norm1_g, mods1, dhcat, zero_ctx, n_ctx_tiles, tr, "norm_mod1_bwd")
    grad_x = dxcat[n_ctx:][None]

    zd = jnp.zeros((1, d), F32)
    dmod_l = jnp.concatenate([d_mods1[1, 0], d_mods1[1, 1], d_gate1, d_mods2[1, 0], d_mods2[1, 1], d_gate2], axis=1)
    dmod_c = jnp.concatenate([d_mods1[0, 0], d_mods1[0, 1], zd, zd, zd, zd], axis=1)
    dmods = jnp.concatenate([dmod_l, dmod_c], axis=0).reshape(2, N_DEV, ca).transpose(1, 0, 2)
    dmods = jnp.pad(dmods, ((0, 0), (0, 6), (0, 0)))
    got, = _exchange_sc([dmods], "scatter_dmod", True, 11)

    res = {}
    with _after(dmods):
        win_sums = _pair_add(d_win, win_theirs, "pair_add_w_in")
    r_in, = _chip_scatter_sc([win_sums], "scatter_w_in_chips", 8)
    with _after(win_sums):
        res["ffn_w1"] = _adam_big(r_w1, ffn_w1[0], m_ffn_w1[0], v_ffn_w1[0], "adam_w1")
    res["ffn_w3"] = _adam_big(r_w3, ffn_w3[0], m_ffn_w3[0], v_ffn_w3[0], "adam_w3")
    res["ffn_w2"] = _adam_big(r_w2, ffn_w2[0], m_ffn_w2[0], v_ffn_w2[0], "adam_w2")
    res["w_branch_a"] = _adam_big(r_a, w_branch_a[0], m_w_branch_a[0], v_w_branch_a[0], "adam_w_a")
    res["w_branch_b"] = _adam_big(r_b, w_branch_b[0], m_w_branch_b[0], v_w_branch_b[0], "adam_w_b")
    res["w_out"] = _adam_big(r_out, w_out[0], m_w_out[0], v_w_out[0], "adam_w_out")

    dm_rows = jnp.concatenate([got[:, 0], got[:, 1]], axis=0)
    with _after(res["w_out"][1]):
        d_ada = _mm_xtdy(act, dm_rows, 1, F32, "d_ada_w")[0]
    back = _mm_dyw(dm_rows, ada16.reshape(1, d, ca), F32, "d_silu_c")
    d_cctx_part = _tiled(lambda b, v: (jnp.sum(b[N_DEV:], axis=0, keepdims=True) * (jax.nn.sigmoid(v) * (1.0 + v * (1.0 - jax.nn.sigmoid(v)))),),
                         [_const(back), _const(c_ctx.reshape(1, d))], [((1, d), F32, (1, d), lambda *_: (0, 0))], (), "d_c_ctx")[0]
    res["ada_w"] = _adam_big(d_ada[None], ada_w[0], m_ada_w[0], v_ada_w[0], "adam_ada")

    d_rel = _bias_grad(dbias, onehot)
    d_lb_soft = jnp.concatenate([dlb_f, dlb_b], axis=0)
    d_lb0 = _tiled(lambda s, g: (g * s * (1.0 - s),), [_const(lb_soft), _const(d_lb_soft)], [((2, hw), F32, (2, hw), lambda *_: (0, 0))], (), "d_lb")[0]
    d_lb_full = jnp.stack([d_lb0, -d_lb0], axis=1)
    d_cw_l = d_cw[:3].reshape(3, N_DEV, ff_p)[:, :, :ff_l].reshape(1, 3, N_DEV * ff_l)
    d_cb_l = d_cb.reshape(N_DEV, ff_p)[:, :ff_l].reshape(1, N_DEV * ff_l)
    small = [d_cctx_part.reshape(d), (dmod_l + dmod_c), d_norm1, d_norm2, d_lb_full, d_hnorm, d_qnorm, d_knorm, d_rel, d_cw_l, d_cb_l, loss_cols]
    pk = _Pack([a.shape for a in small])
    small_slab = pk.pack(small)
    small_parts, = _gather_sc([small_slab], "gather_small", 9)
    with _after(small_slab):
        res["w_in"] = _adam_big(r_in, w_in[0], m_w_in[0], v_w_in[0], "adam_w_in")
    tot = _sum_parts(small_parts, "sum_small")
    g_cctx, g_ada_b, g_n1, g_n2, g_lb, g_hn, g_qn, g_kn, g_rel, g_cw, g_cb, loss_all = pk.unpack(tot)
    loss = _tiled(lambda v: (jnp.sum(v, axis=1, keepdims=True),), [_const(loss_all)], [((1, 1), F32, (1, 1), lambda *_: (0, 0))], (), "loss_total")[0][0, 0]
    g_lb = lax.dynamic_slice_in_dim(g_lb, me * HEAD_DIM, HEAD_DIM, axis=2)
    g_cw = lax.dynamic_slice_in_dim(g_cw, me * ff_l, ff_l, axis=2)
    small_names = [("c_ctx", g_cctx, c_ctx, m_c_ctx, v_c_ctx), ("ada_b", g_ada_b, ada_b, m_ada_b, v_ada_b),
                   ("norm1_g", g_n1, norm1_g, m_norm1_g, v_norm1_g), ("norm2_g", g_n2, norm2_g, m_norm2_g, v_norm2_g),
                   ("hgrn_lb_logits", g_lb, hgrn_lb_logits, m_hgrn_lb_logits, v_hgrn_lb_logits),
                   ("hgrn_norm_g", g_hn, hgrn_norm_g, m_hgrn_norm_g, v_hgrn_norm_g), ("na_q_norm_g", g_qn, na_q_norm_g, m_na_q_norm_g, v_na_q_norm_g),
                   ("na_k_norm_g", g_kn, na_k_norm_g, m_na_k_norm_g, v_na_k_norm_g), ("na_rel_bias", g_rel, na_rel_bias, m_na_rel_bias, v_na_rel_bias),
                   ("ffn_conv_w", g_cw, ffn_conv_w, m_ffn_conv_w, v_ffn_conv_w), ("ffn_conv_b", g_cb, ffn_conv_b, m_ffn_conv_b, v_ffn_conv_b)]
    pk2 = _Pack([s[1].shape for s in small_names])
    sd, sm, sv = _adam_small(*[pk2.pack([s[i] for s in small_names]) for i in (1, 2, 3, 4)])
    sd, sm, sv = pk2.unpack(sd), pk2.unpack(sm), pk2.unpack(sv)
    res.update({s[0]: (s[1], sd[i], sm[i], sv[i]) for i, s in enumerate(small_names)})
    for k in ("w_in", "w_branch_a", "w_branch_b", "w_out", "ffn_w1", "ffn_w3", "ffn_w2", "ada_w"):
        res[k] = tuple(a[None] for a in res[k])

    order = ["c_ctx", "ada_w", "ada_b", "norm1_g", "norm2_g", "w_in", "hgrn_lb_logits", "hgrn_norm_g", "na_q_norm_g", "na_k_norm_g",
             "na_rel_bias", "w_branch_a", "w_branch_b", "w_out", "ffn_w1", "ffn_w3", "ffn_conv_w", "ffn_conv_b", "ffn_w2"]
    shapes = {"c_ctx": c_ctx.shape, "ada_b": ada_b.shape, "norm1_g": norm1_g.shape, "norm2_g": norm2_g.shape,
              "hgrn_lb_logits": hgrn_lb_logits.shape, "hgrn_norm_g": hgrn_norm_g.shape, "na_q_norm_g": na_q_norm_g.shape,
              "na_k_norm_g": na_k_norm_g.shape, "na_rel_bias": na_rel_bias.shape, "ffn_conv_w": ffn_conv_w.shape, "ffn_conv_b": ffn_conv_b.shape}
    outs = [loss, grad_x]
    for part in range(4):
        for k in order:
            a = res[k][part]
            outs.append(a.reshape(shapes[k]) if k in shapes else a)
    return tuple(outs)
```

```python
import functools

import numpy as np
import jax
import jax.numpy as jnp
from jax import lax
from jax.experimental import pallas as pl
from jax.experimental.pallas import tpu as pltpu
from jax.experimental.pallas import tpu_sc as plsc

F32 = jnp.float32
BF16 = jnp.bfloat16
HIGHEST = lax.Precision.HIGHEST

N_DEV = 8
MESH_ID = pl.DeviceIdType.MESH
LANE = 128
HEAD_DIM = 128
N_HEADS = 8
GRID_W = 64
WIN_R = 8
WIN_C = 16
ROPE_THETA = 10000.0
EPS = 1e-6
N_MOD = 6
HGRN_BLOCK = 16
NEG_BIG = -1e30
VMEM_LIMIT = 56 << 20

ADAM_LR = 0.001
ADAM_B1 = 0.9
ADAM_B2 = 0.999
ADAM_EPS = 1e-08
ADAM_WD = 0.01
ADAM_STEP = 10

HBM_SPEC = pl.BlockSpec(memory_space=pltpu.HBM)


_ORDER_AFTER = []


class _after:
    def __init__(self, *arrs):
        self.arrs = list(arrs)

    def __enter__(self):
        _ORDER_AFTER.extend(self.arrs)

    def __exit__(self, *exc):
        del _ORDER_AFTER[:]


def _tie(x, *deps):
    return lax.optimization_barrier((x, *deps))[0]


def _pcall(body, *, name, out_shape, grid=None, in_specs=None, out_specs=None, scratch=(), aliases=None):
    kw = {}
    if grid is not None:
        kw["grid"] = grid
    extra = []
    if _ORDER_AFTER and in_specs is not None:
        extra = list(_ORDER_AFTER)
        del _ORDER_AFTER[:]
        n_in, n_extra, inner = len(in_specs), len(extra), body
        in_specs = list(in_specs) + [pl.BlockSpec(memory_space=pl.ANY)] * n_extra

        def body(*refs):
            return inner(*refs[:n_in], *refs[n_in + n_extra:])

    if extra:
        call = _pcall_inner(body, name, out_shape, kw, in_specs, out_specs, scratch, aliases)
        return lambda *args: call(*args, *extra)
    return _pcall_inner(body, name, out_shape, kw, in_specs, out_specs, scratch, aliases)


def _pcall_inner(body, name, out_shape, kw, in_specs, out_specs, scratch, aliases):
    if in_specs is not None:
        kw["in_specs"] = in_specs
    if out_specs is not None:
        kw["out_specs"] = out_specs
    if scratch:
        kw["scratch_shapes"] = list(scratch)
    if aliases:
        kw["input_output_aliases"] = aliases
    return pl.pallas_call(body, name=name, out_shape=out_shape,
                          compiler_params=pltpu.CompilerParams(vmem_limit_bytes=VMEM_LIMIT), **kw)


def _pick(dim, cands):
    for c in cands:
        if c <= dim and dim % c == 0:
            return c
    return dim


def _sds(shape, dtype):
    return jax.ShapeDtypeStruct(tuple(shape), dtype)


def _peers():
    x, y, c = lax.axis_index("x"), lax.axis_index("y"), lax.axis_index("c")
    out = []
    for k in range(1, N_DEV):
        px = 1 - x if (k >> 2) & 1 else x
        py = 1 - y if (k >> 1) & 1 else y
        pc = 1 - c if k & 1 else c
        out.append((k, (px, py, pc), 4 * px + 2 * py + pc))
    return 4 * x + 2 * y + c, out


def _exchange(arrs, name, scatter):
    n = len(arrs)

    def body(*refs):
        ins, outs = refs[:n], refs[n:2 * n]
        send, recv, loc = refs[2 * n:]
        me, peers = _peers()
        started = []
        for i in range(n):
            src = ins[i].at[me] if scatter else ins[i]
            cp = pltpu.make_async_copy(src, outs[i].at[me], loc.at[i])
            cp.start()
            started.append(cp)
        sends = []
        for k, peer, pidx in peers:
            for i in range(n):
                src = ins[i].at[pidx] if scatter else ins[i]
                cp = pltpu.make_async_remote_copy(src_ref=src, dst_ref=outs[i].at[me], send_sem=send.at[i * 7 + k - 1],
                                                  recv_sem=recv.at[i * 7 + k - 1], device_id=peer, device_id_type=MESH_ID)
                cp.start()
                sends.append(cp)
        for k, peer, pidx in peers:
            for i in range(n):
                src = ins[i].at[pidx] if scatter else ins[i]
                pltpu.make_async_remote_copy(src_ref=src, dst_ref=outs[i].at[pidx], send_sem=send.at[i * 7 + k - 1],
                                             recv_sem=recv.at[i * 7 + k - 1], device_id=peer, device_id_type=MESH_ID).wait_recv()
        for cp in sends:
            cp.wait_send()
        for cp in started:
            cp.wait()

    out_shape = [_sds(a.shape if scatter else (N_DEV,) + a.shape, a.dtype) for a in arrs]
    res = _pcall(body, name=name, out_shape=out_shape, in_specs=[HBM_SPEC] * n, out_specs=[HBM_SPEC] * n,
                 scratch=[pltpu.SemaphoreType.DMA((7 * n,)), pltpu.SemaphoreType.DMA((7 * n,)), pltpu.SemaphoreType.DMA((n,))])(*arrs)
    return list(res)


def _exchange_sc(arrs, name, scatter, collective_id):
    n = len(arrs)
    srcs = [jax.new_ref(a, memory_space=pltpu.MemorySpace.HBM) for a in arrs]
    lands = [jax.empty_ref(_sds(a.shape if scatter else (N_DEV,) + a.shape, a.dtype), memory_space=pltpu.MemorySpace.HBM) for a in arrs]

    @pl.kernel(mesh=plsc.ScalarSubcoreMesh(axis_name="seq", num_cores=1), name=name,
               scratch_types=(pltpu.SemaphoreType.DMA((7 * n,)), pltpu.SemaphoreType.DMA((7 * n,)), pltpu.SemaphoreType.DMA((n,))),
               compiler_params=pltpu.CompilerParams(collective_id=collective_id))
    def launch(send, recv, loc):
        me, peers = _peers()
        barrier = pltpu.get_barrier_semaphore()
        for _, peer, _ in peers:
            pl.semaphore_signal(barrier, inc=1, device_id=peer, device_id_type=MESH_ID)
        pl.semaphore_wait(barrier, N_DEV - 1)
        own = [pltpu.make_async_copy(srcs[i].at[me] if scatter else srcs[i], lands[i].at[me], loc.at[i]) for i in range(n)]
        for cp in own:
            cp.start()
        sends = []
        for k, peer, pidx in peers:
            for i in range(n):
                src = srcs[i].at[pidx] if scatter else srcs[i]
                cp = pltpu.make_async_remote_copy(src_ref=src, dst_ref=lands[i].at[me], send_sem=send.at[i * 7 + k - 1],
                                                  recv_sem=recv.at[i * 7 + k - 1], device_id=peer, device_id_type=MESH_ID)
                cp.start()
                sends.append(cp)
        for k, peer, pidx in peers:
            for i in range(n):
                src = srcs[i].at[pidx] if scatter else srcs[i]
                pltpu.make_async_remote_copy(src_ref=src, dst_ref=lands[i].at[pidx], send_sem=send.at[i * 7 + k - 1],
                                             recv_sem=recv.at[i * 7 + k - 1], device_id=peer, device_id_type=MESH_ID).wait_recv()
        for cp in sends:
            cp.wait_send()
        for cp in own:
            cp.wait()

    launch()
    return [r[...] for r in lands]


def _gather_sc(arrs, name, collective_id):
    n = len(arrs)
    srcs = [jax.new_ref(a, memory_space=pltpu.MemorySpace.HBM) for a in arrs]
    lands = [jax.empty_ref(_sds((N_DEV,) + a.shape, a.dtype), memory_space=pltpu.MemorySpace.HBM) for a in arrs]

    @pl.kernel(mesh=plsc.ScalarSubcoreMesh(axis_name="seq", num_cores=1), name=name,
               scratch_types=(pltpu.SemaphoreType.DMA((7 * n,)), pltpu.SemaphoreType.DMA((7 * n,)), pltpu.SemaphoreType.DMA((n,))),
               compiler_params=pltpu.CompilerParams(collective_id=collective_id))
    def launch(send, recv, loc):
        x, y, c = lax.axis_index("x"), lax.axis_index("y"), lax.axis_index("c")
        me, sibling = (x, y, c), (x, y, 1 - c)
        chips = [(1 - x, y), (x, 1 - y), (1 - x, 1 - y)]
        index = lambda px, py, pc: 4 * px + 2 * py + pc
        barrier = pltpu.get_barrier_semaphore()
        for peer in [sibling] + [(*chip, c) for chip in chips]:
            pl.semaphore_signal(barrier, inc=1, device_id=peer, device_id_type=MESH_ID)
        pl.semaphore_wait(barrier, 4)

        def copy(i, k, block, to, from_src):
            return pltpu.make_async_remote_copy(src_ref=srcs[i] if from_src else lands[i].at[index(*block)], dst_ref=lands[i].at[index(*block)],
                                                send_sem=send.at[i * 7 + k], recv_sem=recv.at[i * 7 + k], device_id=to, device_id_type=MESH_ID)

        own = [pltpu.make_async_copy(srcs[i], lands[i].at[index(*me)], loc.at[i]) for i in range(n)]
        for cp in own:
            cp.start()
        started = []
        for i in range(n):
            started.append(copy(i, 0, me, sibling, True))
            started += [copy(i, 1 + j, me, (*chip, c), True) for j, chip in enumerate(chips)]
        for cp in started:
            cp.start()
        for j, chip in enumerate(chips):
            for i in range(n):
                copy(i, 1 + j, (*chip, c), me, False).wait_recv()
                fwd = copy(i, 4 + j, (*chip, c), sibling, False)
                fwd.start()
                started.append(fwd)
        for i in range(n):
            copy(i, 0, sibling, me, False).wait_recv()
            for j, chip in enumerate(chips):
                copy(i, 4 + j, (*chip, 1 - c), me, False).wait_recv()
        for cp in started:
            cp.wait_send()
        for cp in own:
            cp.wait()

    launch()
    return [r[...] for r in lands]


def _pair_swap_sc(blocks, name, collective_id):
    n = len(blocks)
    srcs = [jax.new_ref(b, memory_space=pltpu.MemorySpace.HBM) for b in blocks]
    lands = [jax.empty_ref(_sds((4,) + b.shape[1:], b.dtype), memory_space=pltpu.MemorySpace.HBM) for b in blocks]

    @pl.kernel(mesh=plsc.ScalarSubcoreMesh(axis_name="seq", num_cores=1), name=name,
               scratch_types=(pltpu.SemaphoreType.DMA((4 * n,)), pltpu.SemaphoreType.DMA((4 * n,))),
               compiler_params=pltpu.CompilerParams(collective_id=collective_id))
    def launch(send, recv):
        x, y, c = lax.axis_index("x"), lax.axis_index("y"), lax.axis_index("c")
        sibling = (x, y, 1 - c)
        barrier = pltpu.get_barrier_semaphore()
        pl.semaphore_signal(barrier, inc=1, device_id=sibling, device_id_type=MESH_ID)
        pl.semaphore_wait(barrier, 1)
        copies = [pltpu.make_async_remote_copy(src_ref=srcs[i].at[2 * q + 1 - c], dst_ref=lands[i].at[q], send_sem=send.at[4 * i + q],
                                               recv_sem=recv.at[4 * i + q], device_id=sibling, device_id_type=MESH_ID)
                  for i in range(n) for q in range(4)]
        for cp in copies:
            cp.start()
        for cp in copies:
            cp.wait()

    launch()
    return [r[...] for r in lands]


def _pair_add(blocks, theirs, name):
    _, r, c_ = blocks.shape
    mine = lax.dynamic_index_in_dim(blocks.reshape(4, 2, r, c_), lax.axis_index("c"), axis=1, keepdims=False)
    tr = _pick(r, (256, 128, 64, 16))
    spec = (None, tr, c_)
    return _tiled(lambda a, b: (a.astype(F32) + b.astype(F32),), [(mine, spec, lambda q, i: (q, i, 0)), (theirs, spec, lambda q, i: (q, i, 0))],
                  [((4, r, c_), BF16, spec, lambda q, i: (q, i, 0))], (4, r // tr), name)[0]


def _chip_scatter_sc(sums, name, collective_id):
    n = len(sums)
    srcs = [jax.new_ref(s, memory_space=pltpu.MemorySpace.HBM) for s in sums]
    lands = [jax.empty_ref(_sds(s.shape, s.dtype), memory_space=pltpu.MemorySpace.HBM) for s in sums]

    @pl.kernel(mesh=plsc.ScalarSubcoreMesh(axis_name="seq", num_cores=1), name=name,
               scratch_types=(pltpu.SemaphoreType.DMA((3 * n,)), pltpu.SemaphoreType.DMA((3 * n,)), pltpu.SemaphoreType.DMA((n,))),
               compiler_params=pltpu.CompilerParams(collective_id=collective_id))
    def launch(send, recv, loc):
        x, y, c = lax.axis_index("x"), lax.axis_index("y"), lax.axis_index("c")
        chips = [(1 - x, y), (x, 1 - y), (1 - x, 1 - y)]
        my_chip = 2 * x + y
        barrier = pltpu.get_barrier_semaphore()
        for chip in chips:
            pl.semaphore_signal(barrier, inc=1, device_id=(*chip, c), device_id_type=MESH_ID)
        pl.semaphore_wait(barrier, 3)
        own = [pltpu.make_async_copy(srcs[i].at[my_chip], lands[i].at[my_chip], loc.at[i]) for i in range(n)]
        for cp in own:
            cp.start()
        sends = [pltpu.make_async_remote_copy(src_ref=srcs[i].at[2 * px + py], dst_ref=lands[i].at[my_chip], send_sem=send.at[3 * i + j],
                                              recv_sem=recv.at[3 * i + j], device_id=(px, py, c), device_id_type=MESH_ID)
                 for j, (px, py) in enumerate(chips) for i in range(n)]
        for cp in sends:
            cp.start()
        for j, (px, py) in enumerate(chips):
            for i in range(n):
                pltpu.make_async_remote_copy(src_ref=srcs[i].at[my_chip], dst_ref=lands[i].at[2 * px + py], send_sem=send.at[3 * i + j],
                                             recv_sem=recv.at[3 * i + j], device_id=(px, py, c), device_id_type=MESH_ID).wait_recv()
        for cp in sends:
            cp.wait_send()
        for cp in own:
            cp.wait()

    launch()
    return [r[...] for r in lands]


def _mm_xw(x, g, out_dtype, name, tm_c=(768, 512, 384, 256, 128, 64, 16), tn_c=(1024, 768, 512, 256, 128), tk_c=(2048, 1024, 768, 512, 256)):
    m, r = x.shape
    nb, r2, cl = g.shape
    assert r == r2
    tm, tn, tk = _pick(m, tm_c), _pick(cl, tn_c), _pick(r, tk_c)
    q, nk = cl // tn, r // tk

    def body(x_ref, g_ref, o_ref, *acc):
        p = lax.dot_general(x_ref[...].astype(BF16), g_ref[...], (((1,), (0,)), ((), ())), preferred_element_type=F32)
        if nk == 1:
            o_ref[...] = p.astype(o_ref.dtype)
        else:
            k = pl.program_id(2)

            @pl.when(k == 0)
            def _():
                acc[0][...] = p

            @pl.when(k > 0)
            def _():
                acc[0][...] += p

            @pl.when(k == nk - 1)
            def _():
                o_ref[...] = acc[0][...].astype(o_ref.dtype)

    return _pcall(
        body, name=name, grid=(m // tm, nb * q, nk),
        in_specs=[pl.BlockSpec((tm, tk), lambda i, j, k: (i, k)), pl.BlockSpec((None, tk, tn), lambda i, j, k: (j // q, k, j % q))],
        out_specs=pl.BlockSpec((tm, tn), lambda i, j, k: (i, j)),
        out_shape=_sds((m, nb * cl), out_dtype),
        scratch=[] if nk == 1 else [pltpu.VMEM((tm, tn), F32)])(x, g)


def _mm_dyw(dy, g, out_dtype, name, init=None, tm_c=(1024, 768, 512, 384, 256, 128), tn_c=(1024, 512, 256, 128), tk_c=(2048, 1536, 1024, 768, 512, 256, 128)):
    m, n = dy.shape
    nb, r, cl = g.shape
    assert n == nb * cl
    tm, tn, tk = _pick(m, tm_c), _pick(r, tn_c), _pick(cl, tk_c)
    q = cl // tk
    nk = nb * q
    has_init = init is not None

    def body(*refs):
        if has_init:
            dy_ref, g_ref, i_ref, o_ref, acc = refs
        else:
            dy_ref, g_ref, o_ref, acc = refs
        k = pl.program_id(2)
        p = lax.dot_general(dy_ref[...].astype(BF16), g_ref[...], (((1,), (1,)), ((), ())), preferred_element_type=F32)

        @pl.when(k == 0)
        def _():
            acc[...] = p + i_ref[...].astype(F32) if has_init else p

        @pl.when(k > 0)
        def _():
            acc[...] += p

        @pl.when(k == nk - 1)
        def _():
            o_ref[...] = acc[...].astype(o_ref.dtype)

    in_specs = [pl.BlockSpec((tm, tk), lambda i, j, k: (i, k)), pl.BlockSpec((None, tn, tk), lambda i, j, k: (k // q, j, k % q))]
    args = [dy, g]
    if has_init:
        in_specs.append(pl.BlockSpec((tm, tn), lambda i, j, k: (i, j)))
        args.append(init)
    return _pcall(body, name=name, grid=(m // tm, r // tn, nk), in_specs=in_specs,
                  out_specs=pl.BlockSpec((tm, tn), lambda i, j, k: (i, j)), out_shape=_sds((m, r), out_dtype),
                  scratch=[pltpu.VMEM((tm, tn), F32)])(*args)


def _mm_xtdy(x, dy, nb, out_dtype, name, tm_c=(1024, 512, 256, 128), tn_c=(768, 512, 256, 128), tk_c=(2304, 2048, 1152, 1024, 768, 512, 256, 128, 16)):
    t, r = x.shape
    t2, n = dy.shape
    assert t == t2 and n % nb == 0
    cl = n // nb
    tm, tn, tk = _pick(r, tm_c), _pick(cl, tn_c), _pick(t, tk_c)
    q, nk = cl // tn, t // tk

    def body(x_ref, dy_ref, o_ref, *acc):
        p = lax.dot_general(x_ref[...].astype(BF16), dy_ref[...].astype(BF16), (((0,), (0,)), ((), ())), preferred_element_type=F32)
        if nk == 1:
            o_ref[...] = p.astype(o_ref.dtype)
        else:
            k = pl.program_id(2)

            @pl.when(k == 0)
            def _():
                acc[0][...] = p

            @pl.when(k > 0)
            def _():
                acc[0][...] += p

            @pl.when(k == nk - 1)
            def _():
                o_ref[...] = acc[0][...].astype(o_ref.dtype)

    return _pcall(
        body, name=name, grid=(r // tm, nb * q, nk),
        in_specs=[pl.BlockSpec((tk, tm), lambda i, j, k: (k, i)), pl.BlockSpec((tk, tn), lambda i, j, k: (k, j))],
        out_specs=pl.BlockSpec((None, tm, tn), lambda i, j, k: (j // q, i, j % q)),
        out_shape=_sds((nb, r, cl), out_dtype),
        scratch=[] if nk == 1 else [pltpu.VMEM((tm, tn), F32)])(x, dy)


def _mm_f32(a, b, name, trans_b=False):
    dims = (((1,), (1,)), ((), ())) if trans_b else (((1,), (0,)), ((), ()))
    n = b.shape[0] if trans_b else b.shape[1]

    def body(a_ref, b_ref, o_ref):
        o_ref[...] = lax.dot_general(a_ref[...], b_ref[...], dims, precision=HIGHEST, preferred_element_type=F32)

    return _pcall(body, name=name, out_shape=_sds((a.shape[0], n), F32))(a, b)


def _tiled(fn, ins, outs, grid, name, acc=()):
    n_in = len(ins)
    grid = tuple(grid) or (1,)
    nd = len(grid)

    def body(*refs):
        vals = fn(*[r[...] for r in refs[:n_in]])
        if not isinstance(vals, (tuple, list)):
            vals = (vals,)
        first = None
        for o, (ref, v) in enumerate(zip(refs[n_in:], vals)):
            if o in acc:
                if first is None:
                    first = pl.program_id(0) == 0
                    for a in range(1, nd):
                        first = jnp.logical_and(first, pl.program_id(a) == 0)

                @pl.when(first)
                def _(ref=ref, v=v):
                    ref[...] = v.astype(ref.dtype)

                @pl.when(jnp.logical_not(first))
                def _(ref=ref, v=v):
                    ref[...] += v.astype(ref.dtype)
            else:
                ref[...] = v.astype(ref.dtype)

    res = _pcall(body, name=name, grid=grid,
                 in_specs=[pl.BlockSpec(b, im) for _, b, im in ins],
                 out_specs=[pl.BlockSpec(b, im) for _, _, b, im in outs],
                 out_shape=[_sds(s, d) for s, d, _, _ in outs])(*[a for a, _, _ in ins])
    return list(res)


def _rows(a, tr):
    return (a, (tr, a.shape[1]), lambda i, *_: (i, 0))


def _const(a):
    nd = a.ndim
    return (a, a.shape, lambda *_: (0,) * nd)


def _cast_bf16(w, name):
    r, c = w.shape
    tr = _pick(r, (256, 128, 64, 16))
    return _tiled(lambda v: v, [_rows(w, tr)], [((r, c), BF16, (tr, c), lambda i: (i, 0))], (r // tr,), name)[0]


def _f_norm_mod(x, g, sh, sc):
    y = x * lax.rsqrt(jnp.mean(x * x, axis=-1, keepdims=True) + EPS) * g
    return y * (1.0 + sc) + sh


def _rope_partner_impl(y):
    nf = HEAD_DIM // 4
    lane = lax.broadcasted_iota(jnp.int32, y.shape, 1)
    return jnp.where(lane % (2 * nf) < nf, pltpu.roll(y, HEAD_DIM - nf, 1), pltpu.roll(y, nf, 1))


_rope_partner = jax.custom_vjp(_rope_partner_impl)
_rope_partner.defvjp(lambda y: (_rope_partner_impl(y), None), lambda _, ct: (_rope_partner_impl(ct),))


def _f_qk(t, g, cos, sin):
    y = t * lax.rsqrt(jnp.mean(t * t, axis=-1, keepdims=True) + EPS) * g
    return y * cos + _rope_partner(y) * sin


def _f_readout(of, ob, gate, g):
    o = of + ob
    on = o * lax.rsqrt(jnp.mean(o * o, axis=-1, keepdims=True) + EPS) * g
    return on * (gate * jax.nn.sigmoid(gate))


def _f_merge(ga, gb, pa, pb):
    return jax.nn.sigmoid(ga) * pa + jax.nn.sigmoid(gb) * pb


def _f_resid(x, gate, m):
    return x + gate * m


def _norm_mod_fwd(xcat, g, mods, n_ctx_tiles, tr):
    tt, d = xcat.shape
    which = lambda i: (jnp.where(i >= n_ctx_tiles, 1, 0), 0, 0, 0)

    def fn(x, gg, md):
        return _f_norm_mod(x, gg, md[0], md[1])

    return _tiled(fn, [_rows(xcat, tr), _const(g), (mods, (None, 2, 1, d), which)],
                  [((tt, d), BF16, (tr, d), lambda i: (i, 0))], (tt // tr,), "norm_mod_fwd")[0]


def _norm_mod_bwd(xcat, g, mods, dh, extra, n_ctx_tiles, tr, name):
    tt, d = xcat.shape
    nt = tt // tr
    has_extra = extra is not None

    def body(*refs):
        if has_extra:
            x_ref, g_ref, m_ref, dh_ref, e_ref, dx_ref, dg_ref, dm_ref = refs
        else:
            x_ref, g_ref, m_ref, dh_ref, dx_ref, dg_ref, dm_ref = refs
        i = pl.program_id(0)
        md = m_ref[...]
        _, vjp = jax.vjp(_f_norm_mod, x_ref[...], g_ref[...], md[0], md[1])
        dx, dg, dsh, dsc = vjp(dh_ref[...].astype(F32))
        dx_ref[...] = dx + e_ref[...] if has_extra else dx

        @pl.when(i == 0)
        def _():
            dg_ref[...] = dg

        @pl.when(i > 0)
        def _():
            dg_ref[...] += dg

        fresh = jnp.logical_or(i == 0, i == n_ctx_tiles)

        @pl.when(fresh)
        def _():
            dm_ref[0] = dsh
            dm_ref[1] = dsc

        @pl.when(jnp.logical_not(fresh))
        def _():
            dm_ref[0] += dsh
            dm_ref[1] += dsc

    which = lambda i: (jnp.where(i >= n_ctx_tiles, 1, 0), 0, 0, 0)
    row = pl.BlockSpec((tr, d), lambda i: (i, 0))
    in_specs = [row, pl.BlockSpec((1, d), lambda i: (0, 0)), pl.BlockSpec((None, 2, 1, d), which), row]
    args = [xcat, g, mods, dh]
    if has_extra:
        in_specs.append(row)
        args.append(extra)
    return _pcall(body, name=name, grid=(nt,), in_specs=in_specs,
                  out_specs=[row, pl.BlockSpec((1, d), lambda i: (0, 0)), pl.BlockSpec((None, 2, 1, d), which)],
                  out_shape=[_sds((tt, d), F32), _sds((1, d), F32), _sds((2, 2, 1, d), F32)])(*args)


def _hgrn_tri(reverse):
    t = np.arange(HGRN_BLOCK)
    tri = (t[None, :] >= t[:, None]) if reverse else (t[None, :] <= t[:, None])
    tri = tri.astype(np.float32)
    return jnp.asarray(tri), jnp.asarray(tri.T.copy())


def _hgrn_rowblock(n, n_ctx_blocks, n_blocks, reverse):
    if not reverse:
        return n
    return jnp.where(n < n_ctx_blocks, n_ctx_blocks - 1 - n, n_blocks - 1 - n + n_ctx_blocks)


def _hgrn_gates(fl, lb):
    sg = jax.nn.sigmoid(fl)
    f = lb + (1.0 - lb) * sg
    return sg, f, jnp.log(f), 1.0 - f


def _hgrn_intra_mask(reverse):
    tio = lax.broadcasted_iota(jnp.int32, (HGRN_BLOCK, HEAD_DIM), 0)
    return (lambda s: tio <= s) if reverse else (lambda s: tio >= s)


class _Halves:
    def __init__(self, reverse):
        self.reverse = reverse
        self.h = HGRN_BLOCK // 2
        tio = lax.broadcasted_iota(jnp.int32, (self.h, HEAD_DIM), 0)
        self.tio = tio if reverse else tio + self.h

    def is_half(self, s):
        return s < self.h if self.reverse else s >= self.h

    def rows(self, a):
        return a[:self.h] if self.reverse else a[self.h:]

    def mask(self, s):
        return self.tio <= s if self.reverse else self.tio >= s

    def widen(self, full, half):
        z = jnp.zeros_like(half)
        return full + jnp.concatenate([half, z] if self.reverse else [z, half], axis=0)


def _unrolled(nblk, u, fn, init):
    assert nblk % u == 0

    def trip(b, c):
        for j in range(u):
            c = fn(b * u + j, j, c)
        return c

    return lax.fori_loop(0, nblk // u, trip, init)


HGRN_UNROLL_FWD = (4, 8)
HGRN_UNROLL_BWD = (8, 4)


def _hgrn_fwd(p, lb, seg_f, reverse, n_ctx_rows):
    tt = p.shape[0]
    hb = HGRN_BLOCK
    nblk, nctx = tt // hb, n_ctx_rows // hb
    tri, _ = _hgrn_tri(reverse)
    u1, u3 = HGRN_UNROLL_FWD

    def body(q_ref, f_ref, v_ref, lb_ref, tri_ref, o_ref, st_all, dec_all, cum_out, qe_buf, cum_blk, k_blk, v_blk):
        mask = _hgrn_intra_mask(reverse)
        hv = _Halves(reverse)
        lbv = lb_ref[...]

        def phase1(n, slot, c):
            r0 = pl.multiple_of(_hgrn_rowblock(n, nctx, nblk, reverse) * hb, hb)
            q, v = q_ref[pl.ds(r0, hb), :], v_ref[pl.ds(r0, hb), :]
            _, f, g, k = _hgrn_gates(f_ref[pl.ds(r0, hb), :], lbv)
            cum = jnp.dot(tri_ref[...], g, precision=HIGHEST, preferred_element_type=F32)
            tot = jnp.sum(g, axis=0, keepdims=True)
            cum_blk[slot] = cum
            k_blk[slot] = k
            v_blk[slot] = v
            oi = jnp.zeros((hb, HEAD_DIM), F32)
            oi_h = jnp.zeros((hb // 2, HEAD_DIM), F32)
            q_h, cum_h = hv.rows(q), hv.rows(cum)
            for s in range(hb):
                if hv.is_half(s):
                    e = jnp.where(hv.mask(s), jnp.exp(jnp.minimum(cum_h - cum_blk[slot, s:s + 1, :], 0.0)), 0.0)
                    a_s = jnp.sum(q_h * e * k_blk[slot, s:s + 1, :], axis=-1, keepdims=True)
                    oi_h = oi_h + a_s * v_blk[slot, s:s + 1, :]
                else:
                    e = jnp.where(mask(s), jnp.exp(jnp.minimum(cum - cum_blk[slot, s:s + 1, :], 0.0)), 0.0)
                    a_s = jnp.sum(q * e * k_blk[slot, s:s + 1, :], axis=-1, keepdims=True)
                    oi = oi + a_s * v_blk[slot, s:s + 1, :]
            o_ref[pl.ds(r0, hb), :] = hv.widen(oi, oi_h)
            cum_out[pl.ds(r0, hb), :] = cum
            qe_buf[pl.ds(r0, hb), :] = q * jnp.exp(cum)
            kl = k * jnp.exp(tot - cum)
            st_all[n] = lax.dot_general(v.astype(BF16), kl.astype(BF16), (((0,), (0,)), ((), ())), preferred_element_type=F32)
            dec_all[pl.ds(n, 1), :] = jnp.exp(tot)
            return c

        _unrolled(nblk, u1, phase1, 0)

        def phase2(n, st):
            kv = st_all[n]
            st_all[n] = st
            return st * dec_all[pl.ds(n, 1), :] + kv

        lax.fori_loop(0, nblk, phase2, jnp.zeros((HEAD_DIM, HEAD_DIM), F32))

        def phase3(n, slot, c):
            r0 = pl.multiple_of(_hgrn_rowblock(n, nctx, nblk, reverse) * hb, hb)
            o_ref[pl.ds(r0, hb), :] += lax.dot_general(qe_buf[pl.ds(r0, hb), :].astype(BF16), st_all[n].astype(BF16),
                                                        (((1,), (1,)), ((), ())), preferred_element_type=F32)
            return c

        _unrolled(nblk, u3, phase3, 0)

    col = lambda seg: pl.BlockSpec((tt, HEAD_DIM), lambda h, seg=seg: (0, seg * N_HEADS + h))
    blk = pltpu.VMEM((u1, hb, HEAD_DIM), F32)
    head = pl.BlockSpec((tt, HEAD_DIM), lambda h: (0, h))
    return _pcall(
        body, name="hgrn_fwd_rev" if reverse else "hgrn_fwd", grid=(N_HEADS,),
        in_specs=[col(0), col(seg_f), col(3), pl.BlockSpec((1, HEAD_DIM), lambda h: (0, h)), pl.BlockSpec((hb, hb), lambda h: (0, 0))],
        out_specs=[head, pl.BlockSpec((None, nblk, HEAD_DIM, HEAD_DIM), lambda h: (h, 0, 0, 0)),
                   pl.BlockSpec((None, nblk, HEAD_DIM), lambda h: (h, 0, 0)), head],
        out_shape=[_sds((tt, N_HEADS * HEAD_DIM), F32), _sds((N_HEADS, nblk, HEAD_DIM, HEAD_DIM), F32),
                   _sds((N_HEADS, nblk, HEAD_DIM), F32), _sds((tt, N_HEADS * HEAD_DIM), F32)],
        scratch=[pltpu.VMEM((tt, HEAD_DIM), F32), blk, blk, blk])(p, p, p, lb, tri)


def _hgrn_bwd(p, lb, do, saved, seg_f, reverse, n_ctx_rows, prev):
    tt = p.shape[0]
    hb = HGRN_BLOCK
    nblk, nctx = tt // hb, n_ctx_rows // hb
    _, tri_t = _hgrn_tri(reverse)
    last_row = 0 if reverse else hb - 1
    has_prev = prev is not None
    u1, u3 = HGRN_UNROLL_BWD

    def body(*refs):
        q_ref, f_ref, v_ref, lb_ref, trit_ref, do_ref, st_all, dec_all, cum_buf = refs[:9]
        refs = refs[9:]
        if has_prev:
            pq_ref, pv_ref = refs[:2]
            refs = refs[2:]
        dq_ref, dfl_ref, dv_ref, dlb_ref, dd_all, cum_blk, k_blk, v_blk, dk_blk, dv_blk = refs
        mask = _hgrn_intra_mask(reverse)
        hv = _Halves(reverse)
        lbv = lb_ref[...]
        tio = lax.broadcasted_iota(jnp.int32, (hb, HEAD_DIM), 0)

        def rows_of(n):
            rb = _hgrn_rowblock(n, nctx, nblk, reverse)
            return rb, pl.multiple_of(rb * hb, hb)

        def load_do(rb):
            lat0 = pl.multiple_of(jnp.maximum(rb - nctx, 0) * hb, hb)
            return jnp.where(rb >= nctx, do_ref[pl.ds(lat0, hb), :], 0.0)

        def phase1(n, slot, c):
            rb, r0 = rows_of(n)
            qe = q_ref[pl.ds(r0, hb), :] * jnp.exp(cum_buf[pl.ds(r0, hb), :])
            dd_all[n] = lax.dot_general(load_do(rb).astype(BF16), qe.astype(BF16), (((0,), (0,)), ((), ())), preferred_element_type=F32)
            return c

        _unrolled(nblk, u1, phase1, 0)

        def phase2r(i, dst):
            n = nblk - 1 - i
            u = dd_all[n]
            dd_all[n] = dst
            return u + dst * dec_all[pl.ds(n, 1), :]

        lax.fori_loop(0, nblk, phase2r, jnp.zeros((HEAD_DIM, HEAD_DIM), F32))

        def phase3(n, slot, dlb):
            rb, r0 = rows_of(n)
            q, v = q_ref[pl.ds(r0, hb), :], v_ref[pl.ds(r0, hb), :]
            sg, f, g, k = _hgrn_gates(f_ref[pl.ds(r0, hb), :], lbv)
            cum = cum_buf[pl.ds(r0, hb), :]
            tot = jnp.sum(g, axis=0, keepdims=True)
            dob = load_do(rb)
            st, dst = st_all[n], dd_all[n]
            e_cum = jnp.exp(cum)
            e_rest = jnp.exp(tot - cum)
            dq = jnp.dot(dob.astype(BF16), st.astype(BF16), preferred_element_type=F32) * e_cum
            dk_inter = jnp.dot(v.astype(BF16), dst.astype(BF16), preferred_element_type=F32) * e_rest
            dv = lax.dot_general((k * e_rest).astype(BF16), dst.astype(BF16), (((1,), (1,)), ((), ())), preferred_element_type=F32)
            cum_blk[slot] = cum
            k_blk[slot] = k
            v_blk[slot] = v
            dq_h = jnp.zeros((hb // 2, HEAD_DIM), F32)
            q_h, cum_h, dob_h = hv.rows(q), hv.rows(cum), hv.rows(dob)
            for s in range(hb):
                half = hv.is_half(s)
                qs, cs, ds = (q_h, cum_h, dob_h) if half else (q, cum, dob)
                e = jnp.where(hv.mask(s) if half else mask(s), jnp.exp(jnp.minimum(cs - cum_blk[slot, s:s + 1, :], 0.0)), 0.0)
                a_s = jnp.sum(qs * e * k_blk[slot, s:s + 1, :], axis=-1, keepdims=True)
                da_s = jnp.sum(ds * v_blk[slot, s:s + 1, :], axis=-1, keepdims=True)
                gs = da_s * e
                if half:
                    dq_h = dq_h + gs * k_blk[slot, s:s + 1, :]
                else:
                    dq = dq + gs * k_blk[slot, s:s + 1, :]
                dk_blk[slot, s:s + 1, :] = jnp.sum(gs * qs, axis=0, keepdims=True)
                dv_blk[slot, s:s + 1, :] = jnp.sum(a_s * ds, axis=0, keepdims=True)
            dq = hv.widen(dq, dq_h)
            dk = dk_inter + dk_blk[slot]
            dv = dv + dv_blk[slot]
            d_tot = jnp.sum(k * dk_inter, axis=0, keepdims=True) + jnp.exp(tot) * jnp.sum(dst * st, axis=0, keepdims=True)
            dcum = q * dq - k * dk + jnp.where(tio == last_row, d_tot, 0.0)
            dg = jnp.dot(trit_ref[...], dcum, precision=HIGHEST, preferred_element_type=F32)
            df = dg / f - dk
            if has_prev:
                dq = dq + pq_ref[pl.ds(r0, hb), :].astype(F32)
                dv = dv + pv_ref[pl.ds(r0, hb), :].astype(F32)
            dq_ref[pl.ds(r0, hb), :] = dq.astype(dq_ref.dtype)
            dv_ref[pl.ds(r0, hb), :] = dv.astype(dv_ref.dtype)
            dfl_ref[pl.ds(r0, hb), :] = (df * (1.0 - lbv) * sg * (1.0 - sg)).astype(dfl_ref.dtype)
            return dlb + jnp.sum(df * (1.0 - sg), axis=0, keepdims=True)

        dlb_ref[...] = _unrolled(nblk, u3, phase3, jnp.zeros((1, HEAD_DIM), F32))

    col = lambda seg: pl.BlockSpec((tt, HEAD_DIM), lambda h, seg=seg: (0, seg * N_HEADS + h))
    head = pl.BlockSpec((tt, HEAD_DIM), lambda h: (0, h))
    lbs = pl.BlockSpec((1, HEAD_DIM), lambda h: (0, h))
    tris = pl.BlockSpec((hb, hb), lambda h: (0, 0))
    in_specs = [col(0), col(seg_f), col(3), lbs, tris, pl.BlockSpec((do.shape[0], HEAD_DIM), lambda h: (0, h)),
                pl.BlockSpec((None, nblk, HEAD_DIM, HEAD_DIM), lambda h: (h, 0, 0, 0)), pl.BlockSpec((None, nblk, HEAD_DIM), lambda h: (h, 0, 0)), head]
    args = [p, p, p, lb, tri_t, do, *saved]
    mid = F32 if not has_prev else BF16
    if has_prev:
        in_specs += [head, head]
        args += list(prev)
    w = N_HEADS * HEAD_DIM
    blk = pltpu.VMEM((u3, hb, HEAD_DIM), F32)
    return _pcall(
        body, name="hgrn_bwd_rev" if reverse else "hgrn_bwd", grid=(N_HEADS,), in_specs=in_specs,
        out_specs=[head, head, head, lbs],
        out_shape=[_sds((tt, w), mid), _sds((tt, w), BF16), _sds((tt, w), mid), _sds((1, w), F32)],
        scratch=[pltpu.VMEM((nblk, HEAD_DIM, HEAD_DIM), F32), blk, blk, blk, blk, blk])(*args)


NA_HEADS_PER_STEP = 4


def _na_geometry(rows):
    r = pl.program_id(1)
    rs = jnp.clip(r - WIN_R // 2, 0, rows - WIN_R)
    return r, rs, r - rs


def _na_scores(q, kb, kc, bias):
    scale = HEAD_DIM ** -0.5
    nt = (((1,), (1,)), ((), ()))
    sb = lax.dot_general(q, kb, nt, preferred_element_type=F32) * scale + bias
    sc = lax.dot_general(q, kc, nt, preferred_element_type=F32) * scale
    m = jnp.maximum(jnp.max(sb, axis=-1, keepdims=True), jnp.max(sc, axis=-1, keepdims=True))
    pb, pc = jnp.exp(sb - m), jnp.exp(sc - m)
    inv = 1.0 / (jnp.sum(pb, axis=-1, keepdims=True) + jnp.sum(pc, axis=-1, keepdims=True))
    return pb * inv, pc * inv


def _na_fwd(qn, kall, p, seg_v, bias, n_ctx_rows):
    t, tt = qn.shape[0], kall.shape[0]
    rows = t // GRID_W
    nband = WIN_R * GRID_W

    nh = NA_HEADS_PER_STEP
    wide = nh * HEAD_DIM

    def body(q_ref, k_ref, v_ref, b_ref, o_ref):
        r, rs, _ = _na_geometry(rows)
        k0 = pl.multiple_of(n_ctx_rows + rs * GRID_W, GRID_W)
        for j in range(nh):
            sl = slice(j * HEAD_DIM, (j + 1) * HEAD_DIM)
            q = q_ref[:, sl]
            kb, kc = k_ref[pl.ds(k0, nband), sl], k_ref[pl.ds(0, n_ctx_rows), sl]
            vb, vc = v_ref[pl.ds(k0, nband), sl].astype(BF16), v_ref[pl.ds(0, n_ctx_rows), sl].astype(BF16)
            pb, pc = _na_scores(q, kb, kc, b_ref[j])
            o = jnp.dot(pb.astype(BF16), vb, preferred_element_type=F32) + jnp.dot(pc.astype(BF16), vc, preferred_element_type=F32)
            o_ref[:, sl] = o.astype(o_ref.dtype)

    variant = lambda h, r: (h, r - jnp.clip(r - WIN_R // 2, 0, rows - WIN_R), 0, 0)
    return _pcall(
        body, name="na_fwd", grid=(N_HEADS // nh, rows),
        in_specs=[pl.BlockSpec((GRID_W, wide), lambda h, r: (r, h)),
                  pl.BlockSpec((tt, wide), lambda h, r: (0, h)),
                  pl.BlockSpec((tt, wide), lambda h, r: (0, seg_v * (N_HEADS // nh) + h)),
                  pl.BlockSpec((nh, None, GRID_W, nband), variant)],
        out_specs=pl.BlockSpec((GRID_W, wide), lambda h, r: (r, h)),
        out_shape=_sds((t, N_HEADS * HEAD_DIM), BF16))(qn, kall, p, bias)


def _na_bwd(qn, kall, p, seg_v, bias, do, n_ctx_rows):
    t, tt = qn.shape[0], kall.shape[0]
    rows = t // GRID_W
    nband = WIN_R * GRID_W
    scale = HEAD_DIM ** -0.5
    tn = (((0,), (0,)), ((), ()))
    nt = (((1,), (1,)), ((), ()))

    nh = NA_HEADS_PER_STEP
    wide = nh * HEAD_DIM

    def body(q_ref, k_ref, v_ref, b_ref, do_ref, dq_ref, dk_ref, dv_ref, db_ref, dv_acc):
        r, rs, var = _na_geometry(rows)
        k0 = pl.multiple_of(n_ctx_rows + rs * GRID_W, GRID_W)
        fresh = jnp.logical_or(r <= WIN_R // 2, r > rows - WIN_R // 2)

        @pl.when(r == 0)
        def _():
            dk_ref[...] = jnp.zeros_like(dk_ref)
            dv_acc[...] = jnp.zeros_like(dv_acc)

        for j in range(nh):
            sl = slice(j * HEAD_DIM, (j + 1) * HEAD_DIM)
            q = q_ref[:, sl]
            kb, kc = k_ref[pl.ds(k0, nband), sl], k_ref[pl.ds(0, n_ctx_rows), sl]
            vb, vc = v_ref[pl.ds(k0, nband), sl].astype(BF16), v_ref[pl.ds(0, n_ctx_rows), sl].astype(BF16)
            pb, pc = _na_scores(q, kb, kc, b_ref[j])
            dof = do_ref[:, sl].astype(F32)
            dob = dof.astype(BF16)
            o = jnp.dot(pb.astype(BF16), vb, preferred_element_type=F32) + jnp.dot(pc.astype(BF16), vc, preferred_element_type=F32)
            delta = jnp.sum(dof * o, axis=-1, keepdims=True)
            dsb = pb * (lax.dot_general(dob, vb, nt, preferred_element_type=F32) - delta)
            dsc = pc * (lax.dot_general(dob, vc, nt, preferred_element_type=F32) - delta)
            dsb16, dsc16 = dsb.astype(BF16), dsc.astype(BF16)
            dq_ref[:, sl] = (jnp.dot(dsb16, kb, preferred_element_type=F32) + jnp.dot(dsc16, kc, preferred_element_type=F32)) * scale
            dk_ref[pl.ds(k0, nband), sl] += lax.dot_general(dsb16, q, tn, preferred_element_type=F32) * scale
            dk_ref[pl.ds(0, n_ctx_rows), sl] += lax.dot_general(dsc16, q, tn, preferred_element_type=F32) * scale
            dv_acc[pl.ds(k0, nband), sl] += lax.dot_general(pb.astype(BF16), dob, tn, preferred_element_type=F32)
            dv_acc[pl.ds(0, n_ctx_rows), sl] += lax.dot_general(pc.astype(BF16), dob, tn, preferred_element_type=F32)

            @pl.when(fresh)
            def _(j=j, dsb=dsb):
                db_ref[j] = dsb

            @pl.when(jnp.logical_not(fresh))
            def _(j=j, dsb=dsb):
                db_ref[j] += dsb

        @pl.when(r == rows - 1)
        def _():
            dv_ref[...] = dv_acc[...].astype(dv_ref.dtype)

    variant = lambda h, r: (h, r - jnp.clip(r - WIN_R // 2, 0, rows - WIN_R), 0, 0)
    head_all = pl.BlockSpec((tt, wide), lambda h, r: (0, h))
    qspec = pl.BlockSpec((GRID_W, wide), lambda h, r: (r, h))
    w = N_HEADS * HEAD_DIM
    return _pcall(
        body, name="na_bwd", grid=(N_HEADS // nh, rows),
        in_specs=[qspec, head_all, pl.BlockSpec((tt, wide), lambda h, r: (0, seg_v * (N_HEADS // nh) + h)),
                  pl.BlockSpec((nh, None, GRID_W, nband), variant), qspec],
        out_specs=[qspec, head_all, head_all, pl.BlockSpec((nh, None, GRID_W, nband), variant)],
        out_shape=[_sds((t, w), F32), _sds((tt, w), F32), _sds((tt, w), BF16), _sds((N_HEADS, WIN_R, GRID_W, nband), F32)],
        scratch=[pltpu.VMEM((tt, wide), F32)])(qn, kall, p, bias, do)


def _na_tables(t, n_ctx_rows):
    half, nf = HEAD_DIM // 2, HEAD_DIM // 4
    pos = np.arange(t)
    lane = np.arange(HEAD_DIM)
    inv = ROPE_THETA ** (-(np.arange(nf, dtype=np.float32)) / nf)
    which = np.where(lane < half, pos[:, None] // GRID_W, pos[:, None] % GRID_W).astype(np.float32)
    ang = which * inv[lane % nf][None, :]
    first = (lane % half) < nf
    cos = np.concatenate([np.ones((n_ctx_rows, HEAD_DIM), np.float32), np.cos(ang).astype(np.float32)])
    sin = np.concatenate([np.zeros((n_ctx_rows, HEAD_DIM), np.float32), np.where(first[None, :], -np.sin(ang), np.sin(ang)).astype(np.float32)])
    w = np.arange(GRID_W)
    dc = np.clip(w[None, :] - w[:, None], -(WIN_C - 1), WIN_C - 1) + WIN_C - 1
    onehot = np.zeros((32, GRID_W * GRID_W), np.float32)
    onehot[dc.reshape(-1), np.arange(GRID_W * GRID_W)] = 1.0
    cs = np.clip(w - WIN_C // 2, 0, GRID_W - WIN_C)
    col_in = (w[None, :] >= cs[:, None]) & (w[None, :] < cs[:, None] + WIN_C)
    onehot *= col_in.reshape(1, -1)
    neg = np.where(col_in, 0.0, NEG_BIG).astype(np.float32)
    return jnp.asarray(cos), jnp.asarray(sin), jnp.asarray(onehot), jnp.asarray(neg)


def _bias_slabs(rel_bias, onehot, neg):
    nr = 2 * WIN_R - 1
    rb = jnp.pad(rel_bias.reshape(N_HEADS * nr, 2 * WIN_C - 1), ((0, 0), (0, 1)))
    spread = _mm_f32(rb, onehot, "bias_spread").reshape(N_HEADS, nr, GRID_W, GRID_W)
    slabs = [spread[:, WIN_R - 1 - v:2 * WIN_R - 1 - v] for v in range(WIN_R)]
    b = jnp.stack(slabs, axis=1) + neg[None, None, None]
    return b.transpose(0, 1, 3, 2, 4).reshape(N_HEADS, WIN_R, GRID_W, WIN_R * GRID_W)


def _bias_grad(dbias, onehot):
    nr = 2 * WIN_R - 1
    d = dbias.reshape(N_HEADS, WIN_R, GRID_W, WIN_R, GRID_W).transpose(0, 1, 3, 2, 4)
    tot = jnp.zeros((N_HEADS, nr, GRID_W, GRID_W), F32)
    for v in range(WIN_R):
        tot = tot + jnp.pad(d[:, v], ((0, 0), (WIN_R - 1 - v, v), (0, 0), (0, 0)))
    g = _mm_f32(tot.reshape(N_HEADS * nr, GRID_W * GRID_W), onehot, "bias_grad", trans_b=True)
    return g[:, :2 * WIN_C - 1].reshape(1, N_HEADS, nr, 2 * WIN_C - 1)


def _shift_rows(u, up):
    n = u.shape[0]
    tio = lax.broadcasted_iota(jnp.int32, u.shape, 0)
    if up:
        return jnp.where(tio == n - 1, 0.0, pltpu.roll(u, n - 1, 0))
    return jnp.where(tio == 0, 0.0, pltpu.roll(u, 1, 0))


def _conv3(u, w_ref, b_ref):
    um, up = _shift_rows(u, False), _shift_rows(u, True)
    return um, up, um * w_ref[0:1, :] + u * w_ref[1:2, :] + up * w_ref[2:3, :] + b_ref[...]


def _ffn_act_fwd(u0, t3, cw, cb):
    t, n = u0.shape
    tc = _pick(n, (256, 128))

    def body(u_ref, t_ref, w_ref, b_ref, a_ref):
        _, _, uc = _conv3(u_ref[...], w_ref, b_ref)
        a_ref[...] = (uc * jax.nn.sigmoid(uc) * t_ref[...]).astype(a_ref.dtype)

    col = lambda rows_: pl.BlockSpec((rows_, tc), lambda j: (0, j))
    return _pcall(body, name="ffn_act_fwd", grid=(n // tc,), in_specs=[col(t), col(t), col(8), col(1)], out_specs=col(t),
                  out_shape=_sds((t, n), BF16))(u0, t3, cw, cb)


def _ffn_act_bwd(u0, t3, cw, cb, da):
    t, n = u0.shape
    tc = _pick(n, (256, 128))

    def body(u_ref, t_ref, w_ref, b_ref, da_ref, du_ref, dt_ref, dw_ref, db_ref):
        u = u_ref[...]
        um, up, uc = _conv3(u, w_ref, b_ref)
        sg = jax.nn.sigmoid(uc)
        dav = da_ref[...].astype(F32)
        dt_ref[...] = (dav * uc * sg).astype(dt_ref.dtype)
        duc = dav * t_ref[...] * sg * (1.0 + uc * (1.0 - sg))
        du = _shift_rows(duc, True) * w_ref[0:1, :] + duc * w_ref[1:2, :] + _shift_rows(duc, False) * w_ref[2:3, :]
        du_ref[...] = du.astype(du_ref.dtype)
        dw_ref[...] = jnp.zeros_like(dw_ref)
        dw_ref[0:1, :] = jnp.sum(duc * um, axis=0, keepdims=True)
        dw_ref[1:2, :] = jnp.sum(duc * u, axis=0, keepdims=True)
        dw_ref[2:3, :] = jnp.sum(duc * up, axis=0, keepdims=True)
        db_ref[...] = jnp.sum(duc, axis=0, keepdims=True)

    col = lambda rows_: pl.BlockSpec((rows_, tc), lambda j: (0, j))
    return _pcall(body, name="ffn_act_bwd", grid=(n // tc,), in_specs=[col(t), col(t), col(8), col(1), col(t)],
                  out_specs=[col(t), col(t), col(8), col(1)],
                  out_shape=[_sds((t, n), BF16), _sds((t, n), BF16), _sds((8, n), F32), _sds((1, n), F32)])(u0, t3, cw, cb, da)


def _adam_math(g, w, m, v):
    m2 = ADAM_B1 * m + (1.0 - ADAM_B1) * g
    v2 = ADAM_B2 * v + (1.0 - ADAM_B2) * (g * g)
    m_hat = m2 / (1.0 - ADAM_B1 ** ADAM_STEP)
    v_hat = v2 / (1.0 - ADAM_B2 ** ADAM_STEP)
    return -ADAM_LR * (m_hat / (jnp.sqrt(v_hat) + ADAM_EPS) + ADAM_WD * w), m2, v2


def _adam_big(parts, w, m, v, name):
    r, c = w.shape
    npart, _, cp = parts.shape
    tr = _pick(r, (64, 32, 16, 8))

    def body(p_ref, w_ref, m_ref, v_ref, g_ref, d_ref, m2_ref, v2_ref):
        g = p_ref[0, :, 0:c].astype(F32)
        for i in range(1, npart):
            g = g + p_ref[i, :, 0:c].astype(F32)
        d, m2, v2 = _adam_math(g, w_ref[...], m_ref[...], v_ref[...])
        g_ref[...] = g
        d_ref[...] = d
        m2_ref[...] = m2
        v2_ref[...] = v2

    row = pl.BlockSpec((tr, c), lambda i: (i, 0))
    return _pcall(body, name=name, grid=(r // tr,),
                  in_specs=[pl.BlockSpec((npart, tr, cp), lambda i: (0, i, 0)), row, row, row],
                  out_specs=[row] * 4, out_shape=[_sds((r, c), F32)] * 4)(parts, w, m, v)


def _adam_small(g, w, m, v):
    def body(g_ref, w_ref, m_ref, v_ref, d_ref, m2_ref, v2_ref):
        d_ref[...], m2_ref[...], v2_ref[...] = _adam_math(g_ref[...], w_ref[...], m_ref[...], v_ref[...])

    return _pcall(body, name="adam_small", out_shape=[_sds(g.shape, F32)] * 3)(g, w, m, v)


def _sum_parts(parts, name):
    def body(p_ref, o_ref):
        s = p_ref[0]
        for i in range(1, N_DEV):
            s = s + p_ref[i]
        o_ref[...] = s

    return _pcall(body, name=name, out_shape=_sds(parts.shape[1:], F32))(parts)


class _Pack:
    def __init__(self, shapes):
        self.shapes = shapes
        self.sizes = [int(np.prod(s)) for s in shapes]
        self.padded = [-(-n // (8 * LANE)) * 8 * LANE for n in self.sizes]
        self.offs = np.concatenate([[0], np.cumsum(self.padded)]).tolist()

    def pack(self, arrs):
        flat = [jnp.pad(a.reshape(-1).astype(F32), (0, p - n)) for a, n, p in zip(arrs, self.sizes, self.padded)]
        return jnp.concatenate(flat).reshape(-1, LANE)

    def unpack(self, slab):
        flat = slab.reshape(-1)
        return [flat[o:o + n].reshape(s) for o, n, s in zip(self.offs, self.sizes, self.shapes)]


def kernel(x, c, ctx, c_ctx, ada_w, ada_b, norm1_g, norm2_g, w_in, hgrn_lb_logits, hgrn_norm_g, na_q_norm_g, na_k_norm_g, na_rel_bias, w_branch_a, w_branch_b, w_out, ffn_w1, ffn_w3, ffn_conv_w, ffn_conv_b, ffn_w2, loss_target, m_c_ctx, m_ada_w, m_ada_b, m_norm1_g, m_norm2_g, m_w_in, m_hgrn_lb_logits, m_hgrn_norm_g, m_na_q_norm_g, m_na_k_norm_g, m_na_rel_bias, m_w_branch_a, m_w_branch_b, m_w_out, m_ffn_w1, m_ffn_w3, m_ffn_conv_w, m_ffn_conv_b, m_ffn_w2, v_c_ctx, v_ada_w, v_ada_b, v_norm1_g, v_norm2_g, v_w_in, v_hgrn_lb_logits, v_hgrn_norm_g, v_na_q_norm_g, v_na_k_norm_g, v_na_rel_bias, v_w_branch_a, v_w_branch_b, v_w_out, v_ffn_w1, v_ffn_w3, v_ffn_conv_w, v_ffn_conv_b, v_ffn_w2):
    t, d = x.shape[1], x.shape[2]
    n_ctx = ctx.shape[1]
    tt = n_ctx + t
    hw = N_HEADS * HEAD_DIM
    ci = w_in.shape[2]
    ca = ada_w.shape[2]
    ff_l = ffn_w1.shape[2]
    ff_p = -(-ff_l // LANE) * LANE
    rows = t // GRID_W
    assert rows >= WIN_R and t % GRID_W == 0 and n_ctx % GRID_W == 0 and ci % LANE == 0 and d % LANE == 0
    me = 4 * lax.axis_index("x") + 2 * lax.axis_index("y") + lax.axis_index("c")
    tr = _pick(n_ctx, (256, 128, 64))
    n_ctx_tiles = n_ctx // tr

    pad_c = lambda w: jnp.pad(w, ((0, 0), (0, ff_p - ff_l)))
    small_in = [c, hgrn_lb_logits.reshape(4, HEAD_DIM), jnp.pad(ffn_conv_w[0], ((0, 5), (0, ff_p - ff_l)))]
    c_all, lb_parts, cw_all = _exchange(small_in, "gather_params", scatter=False)
    ff = N_DEV * ff_p

    cc = jnp.concatenate([c_all.reshape(N_DEV, d), jnp.broadcast_to(c_ctx[None, :], (N_DEV, d))], axis=0)
    act = _tiled(lambda v: v * jax.nn.sigmoid(v), [_const(cc)], [(cc.shape, BF16, cc.shape, lambda *_: (0, 0))], (), "silu_c")[0]
    ada16 = _cast_bf16(ada_w[0], "cast_ada")
    mod_cols = _mm_xw(act, ada16.reshape(1, d, ca), F32, "ada_fwd")
    mod_all = _exchange([mod_cols], "gather_mod", scatter=False)[0]

    with _after(mod_all):
        w_in16 = _cast_bf16(w_in[0], "cast_w_in")
    g_in, = _gather_sc([w_in16], "gather_w_in", 1)
    rest16 = []
    for w_, nm in ((w_branch_a[0], "cast_w_a"), (w_branch_b[0], "cast_w_b"), (w_out[0], "cast_w_out"), (pad_c(ffn_w1[0]), "cast_w1"),
                   (pad_c(ffn_w3[0]), "cast_w3"), (jnp.pad(ffn_w2[0], ((0, ff_p - ff_l), (0, 0))), "cast_w2")):
        with _after(w_in16):
            rest16.append(_cast_bf16(w_, nm))
    g_a, g_b, g_out = _gather_sc(rest16[:3], "gather_mix", 2)
    g_w1, g_w3, g_w2 = _gather_sc(rest16[3:], "gather_ffn", 10)
    g_out = g_out.reshape(1, d, d)
    g_w2 = g_w2.reshape(1, ff, d)
    mod_all = mod_all.transpose(1, 0, 2).reshape(2 * N_DEV, N_MOD * d) + ada_b
    mod_l = lax.dynamic_slice_in_dim(mod_all, me, 1, axis=0).reshape(N_MOD, 1, d)
    mod_c = mod_all[N_DEV:N_DEV + 1].reshape(N_MOD, 1, d)
    mods1 = jnp.stack([mod_c[0:2], mod_l[0:2]])
    mods2 = jnp.stack([mod_l[3:5], mod_l[3:5]])
    gate1, gate2 = mod_l[2], mod_l[5]

    xcat = jnp.concatenate([ctx[0], x[0]], axis=0)
    hcat = _norm_mod_fwd(xcat, norm1_g, mods1, n_ctx_tiles, tr)
    p = _mm_xw(hcat, g_in, F32, "in_proj")
    lb_logits = lb_parts.transpose(1, 0, 2).reshape(2, 2, hw)
    lb_soft = _tiled(lambda a, b: (1.0 / (1.0 + jnp.exp(b - a)),), [_const(lb_logits[:, 0]), _const(lb_logits[:, 1])],
                     [((2, hw), F32, (2, hw), lambda *_: (0, 0))], (), "lb_softmax")[0]
    lb_f, lb_b = lb_soft[0:1], lb_soft[1:2]
    o_f, *saved_f = _hgrn_fwd(p, lb_f, 1, False, n_ctx)
    o_b, *saved_b = _hgrn_fwd(p, lb_b, 2, True, n_ctx)

    cos, sin, onehot, neg = _na_tables(t, n_ctx)
    bias = _bias_slabs(na_rel_bias[0], onehot, neg)
    heads = lambda a: [a[:, h * HEAD_DIM:(h + 1) * HEAD_DIM] for h in range(N_HEADS)]
    seg_rows = lambda seg, off=0: (p, (tr, hw), lambda i: (i + off, seg))
    wide = lambda a, off=0: (a, (tr, hw), lambda i: (i + off, 0))
    out_wide = lambda n_rows, dt: ((n_rows, hw), dt, (tr, hw), lambda i: (i, 0))
    acc_head = ((1, HEAD_DIM), F32, (1, HEAD_DIM), lambda i: (0, 0))
    lat0 = n_ctx // tr
    tab = lambda a, off=0: (a, (tr, HEAD_DIM), lambda i: (i + off, 0))

    def qk_fn(tv, g, cs, sn):
        return (jnp.concatenate([_f_qk(th, g, cs, sn) for th in heads(tv)], axis=1),)

    qn = _tiled(qk_fn, [seg_rows(5, lat0), _const(na_q_norm_g), tab(cos, lat0), tab(sin, lat0)], [out_wide(t, BF16)], (t // tr,), "q_norm_rope")[0]
    kall = _tiled(qk_fn, [seg_rows(6), _const(na_k_norm_g), tab(cos), tab(sin)], [out_wide(tt, BF16)], (tt // tr,), "k_norm_rope")[0]
    y_b = _na_fwd(qn, kall, p, 7, bias, n_ctx)

    def readout_fn(a, b, gt, g):
        return (jnp.concatenate([_f_readout(ah, bh, gh, g) for ah, bh, gh in zip(heads(a), heads(b), heads(gt))], axis=1),)

    y_a = _tiled(readout_fn, [wide(o_f, lat0), wide(o_b, lat0), seg_rows(4, lat0), _const(hgrn_norm_g)], [out_wide(t, BF16)], (t // tr,), "hgrn_readout")[0]

    p_a = _mm_xw(y_a, g_a, F32, "branch_a")
    p_b = _mm_xw(y_b, g_b, F32, "branch_b")
    td = _pick(d, (512, 256, 128))
    nd_t = d // td
    lat_r = n_ctx // tr
    gcol = lambda k: (p, (tr, td), lambda i, j, k=k: (i + lat_r, 8 * hw // td + k * nd_t + j))
    dtile = lambda a: (a, (tr, td), lambda i, j: (i, j))
    z = _tiled(lambda ga, gb, pa, pb: (_f_merge(ga, gb, pa, pb),), [gcol(0), gcol(1), dtile(p_a), dtile(p_b)],
               [((t, d), BF16, (tr, td), lambda i, j: (i, j))], (t // tr, nd_t), "merge")[0]
    mix = _mm_xw(z, g_out, F32, "out_proj")
    xl = x[0]
    def resid_norm(a, g, m_, ng, md):
        xm = _f_resid(a, g, m_)
        return xm, _f_norm_mod(xm, ng, md[0], md[1])

    x_mid, h2 = _tiled(resid_norm, [_rows(xl, tr), _const(gate1), _rows(mix, tr), _const(norm2_g), _const(mods2[1])],
                       [((t, d), F32, (tr, d), lambda i: (i, 0)), ((t, d), BF16, (tr, d), lambda i: (i, 0))], (t // tr,), "resid1_norm2")
    u0 = _mm_xw(h2, g_w1, F32, "ffn_up1")
    t3 = _mm_xw(h2, g_w3, F32, "ffn_up3")
    cw_full = cw_all.transpose(1, 0, 2).reshape(8, ff)
    cb_full = jnp.pad(ffn_conv_b.reshape(N_DEV, ff_l), ((0, 0), (0, ff_p - ff_l))).reshape(1, ff)
    a_act = _ffn_act_fwd(u0, t3, cw_full, cb_full)
    f_out = _mm_xw(a_act, g_w2, F32, "ffn_down")

    def loss_fn(xm, g, f, tg):
        err = xm + g * f - tg
        dyv = err * (1.0 / d)
        return dyv, jnp.sum(err * err, axis=0, keepdims=True) * (0.5 / d), jnp.sum(dyv * f, axis=0, keepdims=True), dyv * g

    dy, loss_cols, d_gate2, df = _tiled(loss_fn, [_rows(x_mid, tr), _const(gate2), _rows(f_out, tr), _rows(loss_target[0], tr)],
                                        [((t, d), F32, (tr, d), lambda i: (i, 0)), ((1, d), F32, (1, d), lambda i: (0, 0)),
                                         ((1, d), F32, (1, d), lambda i: (0, 0)), ((t, d), BF16, (tr, d), lambda i: (i, 0))],
                                        (t // tr,), "loss", acc=(1, 2))

    d_w2 = _mm_xtdy(a_act, df, 1, BF16, "d_w2")
    da = _mm_dyw(df, g_w2, BF16, "d_act")
    du0, dt3, d_cw, d_cb = _ffn_act_bwd(u0, t3, cw_full, cb_full, da)
    d_w1 = _mm_xtdy(h2, du0, N_DEV, BF16, "d_w1")
    d_w3 = _mm_xtdy(h2, dt3, N_DEV, BF16, "d_w3")
    ffn_blocks = [d_w1, d_w3, d_w2.reshape(N_DEV, ff_p, d)]
    ffn_theirs = _pair_swap_sc(ffn_blocks, "scatter_ffn_pair", 3)
    dh2 = _mm_dyw(du0, g_w1, F32, "d_h2_a")
    dh2 = _mm_dyw(dt3, g_w3, F32, "d_h2_b", init=dh2)
    with _after(dh2):
        ffn_sums = [_pair_add(ffn_blocks[0], ffn_theirs[0], "pair_add_w1")]
    ffn_sums += [_pair_add(ffn_blocks[1], ffn_theirs[1], "pair_add_w3"), _pair_add(ffn_blocks[2], ffn_theirs[2], "pair_add_w2")]
    r_w1, r_w3, r_w2 = _chip_scatter_sc(ffn_sums, "scatter_ffn_chips", 4)
    with _after(*ffn_sums):
        dx_mid, d_norm2, d_mods2 = _norm_mod_bwd(x_mid, norm2_g, mods2, dh2, dy, 0, tr, "norm_mod2_bwd")

    dm, d_gate1 = _tiled(lambda dxm, g, m_: (dxm * g, jnp.sum(dxm * m_, axis=0, keepdims=True)),
                         [_rows(dx_mid, tr), _const(gate1), _rows(mix, tr)],
                         [((t, d), BF16, (tr, d), lambda i: (i, 0)), ((1, d), F32, (1, d), lambda i: (0, 0))], (t // tr,), "d_resid1", acc=(1,))
    d_wout = _mm_xtdy(z, dm, 1, BF16, "d_w_out")
    dz = _mm_dyw(dm, g_out, F32, "d_merge")

    def merge_bwd(ga, gb, pa, pb, dzv):
        _, vjp = jax.vjp(_f_merge, ga, gb, pa, pb)
        return vjp(dzv)

    dga, dgb, dpa, dpb = _tiled(merge_bwd, [gcol(0), gcol(1), dtile(p_a), dtile(p_b), dtile(dz)],
                                [((t, d), BF16, (tr, td), lambda i, j: (i, j))] * 4, (t // tr, nd_t), "merge_bwd")
    d_wa = _mm_xtdy(y_a, dpa, N_DEV, BF16, "d_w_a")
    d_wb = _mm_xtdy(y_b, dpb, N_DEV, BF16, "d_w_b")
    dy_a = _mm_dyw(dpa, g_a, F32, "d_y_a")
    dy_b = _mm_dyw(dpb, g_b, BF16, "d_y_b")
    dqn, dkall, dv_na, dbias = _na_bwd(qn, kall, p, 7, bias, dy_b, n_ctx)
    mix_blocks = [_tie(d_wa, r_w1, dqn), d_wb, d_wout.reshape(N_DEV, d // N_DEV, d)]
    mix_theirs = _pair_swap_sc(mix_blocks, "scatter_mix_pair", 5)

    def readout_bwd(a, b, gt, g, ct):
        das, dgts, dg = [], [], 0.0
        for ah, bh, gh, ch in zip(heads(a), heads(b), heads(gt), heads(ct)):
            _, vjp = jax.vjp(_f_readout, ah, bh, gh, g)
            da_h, _, dgt_h, dg_h = vjp(ch)
            das.append(da_h)
            dgts.append(dgt_h)
            dg = dg + dg_h
        return jnp.concatenate(das, axis=1), jnp.concatenate(dgts, axis=1), dg

    with _after(mix_blocks[0]):
        do_h, d_gate_o, d_hnorm = _tiled(
            readout_bwd, [wide(o_f, lat0), wide(o_b, lat0), seg_rows(4, lat0), _const(hgrn_norm_g), wide(dy_a)],
            [out_wide(t, F32), out_wide(t, BF16), acc_head], (t // tr,), "readout_bwd", acc=(2,))
    dq1, dfl_f, dv1, dlb_f = _hgrn_bwd(p, lb_f, do_h, saved_f, 1, False, n_ctx, None)
    with _after(dq1):
        mix_sums = [_pair_add(mix_blocks[0], mix_theirs[0], "pair_add_wa")]
    mix_sums += [_pair_add(mix_blocks[1], mix_theirs[1], "pair_add_wb"), _pair_add(mix_blocks[2], mix_theirs[2], "pair_add_wout")]
    r_a, r_b, r_out = _chip_scatter_sc(mix_sums, "scatter_mix_chips", 6)
    with _after(*mix_sums):
        dq_h, dfl_b, dv_h, dlb_b = _hgrn_bwd(p, lb_b, do_h, saved_b, 2, True, n_ctx, (dq1, dv1))

    def qk_bwd(tv, g, cs, sn, ct):
        dts, dg = [], 0.0
        for th, ch in zip(heads(tv), heads(ct)):
            _, vjp = jax.vjp(lambda a, b: _f_qk(a, b, cs, sn), th, g)
            dt_h, dg_h = vjp(ch)
            dts.append(dt_h)
            dg = dg + dg_h
        return jnp.concatenate(dts, axis=1), dg

    d_pq, d_qnorm = _tiled(qk_bwd, [seg_rows(5, lat0), _const(na_q_norm_g), tab(cos, lat0), tab(sin, lat0), wide(dqn)],
                           [out_wide(t, BF16), acc_head], (t // tr,), "q_norm_rope_bwd", acc=(1,))
    d_pk, d_knorm = _tiled(qk_bwd, [seg_rows(6), _const(na_k_norm_g), tab(cos), tab(sin), wide(dkall)],
                           [out_wide(tt, BF16), acc_head], (tt // tr,), "k_norm_rope_bwd", acc=(1,))

    zc = lambda w_: jnp.zeros((n_ctx, w_), BF16)
    lat_only = lambda a: jnp.concatenate([zc(a.shape[1]), a], axis=0)
    dp = jnp.concatenate([dq_h, dfl_f, dfl_b, dv_h, lat_only(d_gate_o), lat_only(d_pq), d_pk, dv_na, lat_only(dga), lat_only(dgb)], axis=1)
    d_win = _tie(_mm_xtdy(hcat, dp, N_DEV, BF16, "d_w_in"), r_a)
    win_theirs, = _pair_swap_sc([d_win], "scatter_w_in_pair", 7)
    with _after(d_win):
        dhcat = _mm_dyw(dp, g_in, BF16, "d_hcat")
    zero_ctx = jnp.concatenate([jnp.zeros((n_ctx, d), F32), dx_mid], axis=0)
    dxcat, d_norm1, d_mods1 = _norm_mod_bwd(xcat, norm1_g, mods1, dhcat, zero_ctx, n_ctx_tiles, tr, "norm_mod1_bwd")
    grad_x = dxcat[n_ctx:][None]

    zd = jnp.zeros((1, d), F32)
    dmod_l = jnp.concatenate([d_mods1[1, 0], d_mods1[1, 1], d_gate1, d_mods2[1, 0], d_mods2[1, 1], d_gate2], axis=1)
    dmod_c = jnp.concatenate([d_mods1[0, 0], d_mods1[0, 1], zd, zd, zd, zd], axis=1)
    dmods = jnp.concatenate([dmod_l, dmod_c], axis=0).reshape(2, N_DEV, ca).transpose(1, 0, 2)
    dmods = jnp.pad(dmods, ((0, 0), (0, 6), (0, 0)))
    got, = _exchange_sc([dmods], "scatter_dmod", True, 11)

    res = {}
    with _after(dmods):
        win_sums = _pair_add(d_win, win_theirs, "pair_add_w_in")
    r_in, = _chip_scatter_sc([win_sums], "scatter_w_in_chips", 8)
    with _after(win_sums):
        res["ffn_w1"] = _adam_big(r_w1, ffn_w1[0], m_ffn_w1[0], v_ffn_w1[0], "adam_w1")
    res["ffn_w3"] = _adam_big(r_w3, ffn_w3[0], m_ffn_w3[0], v_ffn_w3[0], "adam_w3")
    res["ffn_w2"] = _adam_big(r_w2, ffn_w2[0], m_ffn_w2[0], v_ffn_w2[0], "adam_w2")
    res["w_branch_a"] = _adam_big(r_a, w_branch_a[0], m_w_branch_a[0], v_w_branch_a[0], "adam_w_a")
    res["w_branch_b"] = _adam_big(r_b, w_branch_b[0], m_w_branch_b[0], v_w_branch_b[0], "adam_w_b")
    res["w_out"] = _adam_big(r_out, w_out[0], m_w_out[0], v_w_out[0], "adam_w_out")

    dm_rows = jnp.concatenate([got[:, 0], got[:, 1]], axis=0)
    with _after(res["w_out"][1]):
        d_ada = _mm_xtdy(act, dm_rows, 1, F32, "d_ada_w")[0]
    back = _mm_dyw(dm_rows, ada16.reshape(1, d, ca), F32, "d_silu_c")
    d_cctx_part = _tiled(lambda b, v: (jnp.sum(b[N_DEV:], axis=0, keepdims=True) * (jax.nn.sigmoid(v) * (1.0 + v * (1.0 - jax.nn.sigmoid(v)))),),
                         [_const(back), _const(c_ctx.reshape(1, d))], [((1, d), F32, (1, d), lambda *_: (0, 0))], (), "d_c_ctx")[0]
    res["ada_w"] = _adam_big(d_ada[None], ada_w[0], m_ada_w[0], v_ada_w[0], "adam_ada")

    d_rel = _bias_grad(dbias, onehot)
    d_lb_soft = jnp.concatenate([dlb_f, dlb_b], axis=0)
    d_lb0 = _tiled(lambda s, g: (g * s * (1.0 - s),), [_const(lb_soft), _const(d_lb_soft)], [((2, hw), F32, (2, hw), lambda *_: (0, 0))], (), "d_lb")[0]
    d_lb_full = jnp.stack([d_lb0, -d_lb0], axis=1)
    d_cw_l = d_cw[:3].reshape(3, N_DEV, ff_p)[:, :, :ff_l].reshape(1, 3, N_DEV * ff_l)
    d_cb_l = d_cb.reshape(N_DEV, ff_p)[:, :ff_l].reshape(1, N_DEV * ff_l)
    small = [d_cctx_part.reshape(d), (dmod_l + dmod_c), d_norm1, d_norm2, d_lb_full, d_hnorm, d_qnorm, d_knorm, d_rel, d_cw_l, d_cb_l, loss_cols]
    pk = _Pack([a.shape for a in small])
    small_slab = pk.pack(small)
    small_parts, = _gather_sc([small_slab], "gather_small", 9)
    with _after(small_slab):
        res["w_in"] = _adam_big(r_in, w_in[0], m_w_in[0], v_w_in[0], "adam_w_in")
    tot = _sum_parts(small_parts, "sum_small")
    g_cctx, g_ada_b, g_n1, g_n2, g_lb, g_hn, g_qn, g_kn, g_rel, g_cw, g_cb, loss_all = pk.unpack(tot)
    loss = _tiled(lambda v: (jnp.sum(v, axis=1, keepdims=True),), [_const(loss_all)], [((1, 1), F32, (1, 1), lambda *_: (0, 0))], (), "loss_total")[0][0, 0]
    g_lb = lax.dynamic_slice_in_dim(g_lb, me * HEAD_DIM, HEAD_DIM, axis=2)
    g_cw = lax.dynamic_slice_in_dim(g_cw, me * ff_l, ff_l, axis=2)
    small_names = [("c_ctx", g_cctx, c_ctx, m_c_ctx, v_c_ctx), ("ada_b", g_ada_b, ada_b, m_ada_b, v_ada_b),
                   ("norm1_g", g_n1, norm1_g, m_norm1_g, v_norm1_g), ("norm2_g", g_n2, norm2_g, m_norm2_g, v_norm2_g),
                   ("hgrn_lb_logits", g_lb, hgrn_lb_logits, m_hgrn_lb_logits, v_hgrn_lb_logits),
                   ("hgrn_norm_g", g_hn, hgrn_norm_g, m_hgrn_norm_g, v_hgrn_norm_g), ("na_q_norm_g", g_qn, na_q_norm_g, m_na_q_norm_g, v_na_q_norm_g),
                   ("na_k_norm_g", g_kn, na_k_norm_g, m_na_k_norm_g, v_na_k_norm_g), ("na_rel_bias", g_rel, na_rel_bias, m_na_rel_bias, v_na_rel_bias),
                   ("ffn_conv_w", g_cw, ffn_conv_w, m_ffn_conv_w, v_ffn_conv_w), ("ffn_conv_b", g_cb, ffn_conv_b, m_ffn_conv_b, v_ffn_conv_b)]
    pk2 = _Pack([s[1].shape for s in small_names])
    sd, sm, sv = _adam_small(*[pk2.pack([s[i] for s in small_names]) for i in (1, 2, 3, 4)])
    sd, sm, sv = pk2.unpack(sd), pk2.unpack(sm), pk2.unpack(sv)
    res.update({s[0]: (s[1], sd[i], sm[i], sv[i]) for i, s in enumerate(small_names)})
    for k in ("w_in", "w_branch_a", "w_branch_b", "w_out", "ffn_w1", "ffn_w3", "ffn_w2", "ada_w"):
        res[k] = tuple(a[None] for a in res[k])

    order = ["c_ctx", "ada_w", "ada_b", "norm1_g", "norm2_g", "w_in", "hgrn_lb_logits", "hgrn_norm_g", "na_q_norm_g", "na_k_norm_g",
             "na_rel_bias", "w_branch_a", "w_branch_b", "w_out", "ffn_w1", "ffn_w3", "ffn_conv_w", "ffn_conv_b", "ffn_w2"]
    shapes = {"c_ctx": c_ctx.shape, "ada_b": ada_b.shape, "norm1_g": norm1_g.shape, "norm2_g": norm2_g.shape,
              "hgrn_lb_logits": hgrn_lb_logits.shape, "hgrn_norm_g": hgrn_norm_g.shape, "na_q_norm_g": na_q_norm_g.shape,
              "na_k_norm_g": na_k_norm_g.shape, "na_rel_bias": na_rel_bias.shape, "ffn_conv_w": ffn_conv_w.shape, "ffn_conv_b": ffn_conv_b.shape}
    outs = [loss, grad_x]
    for part in range(4):
        for k in order:
            a = res[k][part]
            outs.append(a.reshape(shapes[k]) if k in shapes else a)
    return tuple(outs)
```
